```python
import math
import jax, jax.numpy as jnp
from jax import lax
import numpy as np

D_MODEL = 2048
BATCH = 8
SEQ = 8192
DEPTH = 1

GRID_W = 64
CTX_LEN = 256
ROWS_PER_CHUNK = 2
CHUNK = ROWS_PER_CHUNK * GRID_W
D_MIX = D_MODEL
W_A = D_MIX // 2
HEAD_DIM_A = 128
N_HEADS_A = W_A // HEAD_DIM_A
W_B = D_MIX - W_A
S5_CH = 16
S5_GROUPS = W_B // S5_CH
S5_STATE = 64
N_DIR = 2
IN_COLS = 3 * W_A + 2 * W_B
ALPHA = (2.0 * DEPTH) ** 0.25
OUT_INIT_SCALE = (8.0 * DEPTH) ** -0.25
LN_EPS = 1e-6
F32 = jnp.float32

kernel_name = "hybrid_gmlp_s5_parallel_heads_deepnorm"


def _layer_norm(x):
    x32 = x.astype(F32)
    mu = jnp.mean(x32, axis=-1, keepdims=True)
    var = jnp.mean(jnp.square(x32 - mu), axis=-1, keepdims=True)
    return ((x32 - mu) * lax.rsqrt(var + LN_EPS)).astype(x.dtype)


def _modulate(x, shift, scale):
    return _layer_norm(x) * (1 + scale[:, None, :]) + shift[:, None, :]


def _chunk_mlp(uv, n_chunks, ln_g, ln_b, w_s, b_s):
    bsz = uv.shape[0]
    u, v = jnp.split(jax.nn.gelu(uv, approximate=False), 2, axis=-1)
    v = _layer_norm(v) * ln_g + ln_b
    v = v.reshape(bsz, n_chunks, CHUNK, N_HEADS_A, HEAD_DIM_A)
    mixed = jnp.einsum("hpq,bnqhd->bnphd", w_s, v) + b_s.T[None, None, :, :, None]
    return u * mixed.reshape(bsz, n_chunks * CHUNK, W_A)


def _s5_discretize(lam_re, lam_im, log_step, b_re, b_im):
    step = jnp.exp(log_step.astype(F32))[:, None]
    lr, li = lam_re.astype(F32), lam_im.astype(F32)
    dr, di = lr * step, li * step
    mag = jnp.exp(dr)
    ab_re, ab_im = mag * jnp.cos(di), mag * jnp.sin(di)
    den = lr * lr + li * li
    nr, ni = ab_re - 1.0, ab_im
    f_re = (nr * lr + ni * li) / den
    f_im = (ni * lr - nr * li) / den
    br, bi = b_re.astype(F32), b_im.astype(F32)
    bb_re = f_re[..., None] * br - f_im[..., None] * bi
    bb_im = f_re[..., None] * bi + f_im[..., None] * br
    return ab_re, ab_im, bb_re, bb_im


def _ssm_combine(e1, e2):
    a1r, a1i, b1r, b1i = e1
    a2r, a2i, b2r, b2i = e2
    ar = a1r * a2r - a1i * a2i
    ai = a1r * a2i + a1i * a2r
    br = a2r * b1r - a2i * b1i + b2r
    bi = a2r * b1i + a2i * b1r + b2i
    return ar, ai, br, bi


def _s5_scan(u, ab_re, ab_im, bb_re, bb_im, h0):
    bu_re = jnp.einsum("lbgc,gpc->lbgp", u, bb_re)
    bu_im = jnp.einsum("lbgc,gpc->lbgp", u, bb_im)
    if h0 is not None:
        h0_re, h0_im = h0
        bu_re = bu_re.at[0].add(ab_re * h0_re - ab_im * h0_im)
        bu_im = bu_im.at[0].add(ab_re * h0_im + ab_im * h0_re)
    length = u.shape[0]
    a_re = jnp.broadcast_to(ab_re, (length, 1) + ab_re.shape)
    a_im = jnp.broadcast_to(ab_im, (length, 1) + ab_im.shape)
    _, _, h_re, h_im = lax.associative_scan(_ssm_combine, (a_re, a_im, bu_re, bu_im), axis=0)
    return h_re, h_im


def _s5_readout(h_re, h_im, c_re, c_im):
    return (jnp.einsum("lbgp,gcp->lbgc", h_re, c_re.astype(F32))
            - jnp.einsum("lbgp,gcp->lbgc", h_im, c_im.astype(F32)))


def _s5_branch(u_lat, u_ctx, lam_re, lam_im, log_step, b_re, b_im, c_re, c_im,
               d_skip, w_glu, b_glu, with_ctx_out):
    def to_lbgc(u):
        bsz, length, _ = u.shape
        return u.astype(F32).reshape(bsz, length, S5_GROUPS, S5_CH).transpose(1, 0, 2, 3)

    ul, uc = to_lbgc(u_lat), to_lbgc(u_ctx)
    ys_lat, ys_ctx = [], []
    for d in range(N_DIR):
        ab_re, ab_im, bb_re, bb_im = _s5_discretize(lam_re[d], lam_im[d], log_step[d], b_re[d], b_im[d])
        ucd, uld = (uc, ul) if d == 0 else (uc[::-1], ul[::-1])
        hc_re, hc_im = _s5_scan(ucd, ab_re, ab_im, bb_re, bb_im, None)
        hl_re, hl_im = _s5_scan(uld, ab_re, ab_im, bb_re, bb_im, (hc_re[-1], hc_im[-1]))
        yl = _s5_readout(hl_re, hl_im, c_re[d], c_im[d])
        ys_lat.append(yl if d == 0 else yl[::-1])
        if with_ctx_out:
            yc = _s5_readout(hc_re, hc_im, c_re[d], c_im[d])
            ys_ctx.append(yc if d == 0 else yc[::-1])

    d_grp = d_skip.astype(F32).reshape(S5_GROUPS, S5_CH)

    def finish(y, u, dtype):
        y = y + d_grp * u
        length, bsz = y.shape[0], y.shape[1]
        y = y.transpose(1, 0, 2, 3).reshape(bsz, length, W_B)
        y = jax.nn.gelu(y, approximate=False).astype(dtype)
        return y * jax.nn.sigmoid(y @ w_glu + b_glu)

    y_lat = finish(ys_lat[0] + ys_lat[1], ul, u_lat.dtype)
    y_ctx = finish(ys_ctx[0] + ys_ctx[1], uc, u_ctx.dtype) if with_ctx_out else None
    return y_lat, y_ctx


def _fwd_setup_inputs(seed: int = 0) -> dict:
    key = jax.random.key(seed)
    ks = jax.random.split(key, 24)
    nrm = jax.random.normal
    D = D_MODEL
    x = nrm(ks[0], (BATCH, SEQ, D), F32)
    c = nrm(ks[1], (BATCH, D), F32)
    ctx = nrm(ks[2], (BATCH, CTX_LEN, D), F32)
    c_ctx = nrm(ks[3], (D,), F32)
    w_ada = nrm(ks[4], (DEPTH, D, 3 * D), F32) * (D ** -0.5) * 0.5
    b_ada = 0.02 * nrm(ks[5], (DEPTH, 3 * D), F32) + jnp.concatenate(
        [jnp.zeros((2 * D,), F32), jnp.ones((D,), F32)])[None]
    w_in = nrm(ks[6], (DEPTH, D, IN_COLS), F32) * (D ** -0.5)
    sgu_ln_g = 1.0 + 0.02 * nrm(ks[7], (DEPTH, W_A), F32)
    sgu_ln_b = 0.02 * nrm(ks[8], (DEPTH, W_A), F32)
    w_spatial = nrm(ks[9], (DEPTH, N_HEADS_A, CHUNK, CHUNK), F32) * (CHUNK ** -0.5)
    b_spatial = 1.0 + 0.02 * nrm(ks[10], (DEPTH, N_HEADS_A, CHUNK), F32)
    n_idx = jnp.arange(S5_STATE, dtype=F32)
    s5_shape = (DEPTH, N_DIR, S5_GROUPS, S5_STATE)
    s5_lam_re = -0.5 + 0.01 * nrm(ks[11], s5_shape, F32)
    s5_lam_im = math.pi * n_idx + 0.01 * nrm(ks[12], s5_shape, F32)
    s5_log_step = jax.random.uniform(ks[13], (DEPTH, N_DIR, S5_GROUPS), F32,
                                     minval=math.log(1e-3), maxval=math.log(1e-1))
    b_shape = (DEPTH, N_DIR, S5_GROUPS, S5_STATE, S5_CH)
    s5_b_re = nrm(ks[14], b_shape, F32) * ((2 * S5_CH) ** -0.5)
    s5_b_im = nrm(ks[15], b_shape, F32) * ((2 * S5_CH) ** -0.5)
    c_shape = (DEPTH, N_DIR, S5_GROUPS, S5_CH, S5_STATE)
    s5_c_re = nrm(ks[16], c_shape, F32) * (0.5 ** 0.5)
    s5_c_im = nrm(ks[17], c_shape, F32) * (0.5 ** 0.5)
    s5_d = nrm(ks[18], (DEPTH, W_B), F32)
    w_glu = nrm(ks[19], (DEPTH, W_B, W_B), F32) * (W_B ** -0.5)
    b_glu = 0.02 * nrm(ks[20], (DEPTH, W_B), F32)
    w_out = nrm(ks[21], (DEPTH, D_MIX, D), F32) * (D_MIX ** -0.5) * OUT_INIT_SCALE
    ln_g = 1.0 + 0.02 * nrm(ks[22], (DEPTH, D), F32)
    ln_b = 0.02 * nrm(ks[23], (DEPTH, D), F32)
    return {"x": x, "c": c, "ctx": ctx, "c_ctx": c_ctx,
            "w_ada": w_ada, "b_ada": b_ada, "w_in": w_in,
            "sgu_ln_g": sgu_ln_g, "sgu_ln_b": sgu_ln_b,
            "w_spatial": w_spatial, "b_spatial": b_spatial,
            "s5_lam_re": s5_lam_re, "s5_lam_im": s5_lam_im, "s5_log_step": s5_log_step,
            "s5_b_re": s5_b_re, "s5_b_im": s5_b_im, "s5_c_re": s5_c_re, "s5_c_im": s5_c_im,
            "s5_d": s5_d, "w_glu": w_glu, "b_glu": b_glu, "w_out": w_out,
            "ln_g": ln_g, "ln_b": ln_b}


def _fwd_reference(x, c, ctx, c_ctx, w_ada, b_ada, w_in, sgu_ln_g, sgu_ln_b, w_spatial, b_spatial,
              s5_lam_re, s5_lam_im, s5_log_step, s5_b_re, s5_b_im, s5_c_re, s5_c_im,
              s5_d, w_glu, b_glu, w_out, ln_g, ln_b):
    rows = x.shape[1] // GRID_W
    n_chunks_lat = rows // ROWS_PER_CHUNK
    n_chunks_ctx = ctx.shape[1] // CHUNK
    col_b0, col_b1 = 3 * W_A, 3 * W_A + W_B
    for i in range(DEPTH):
        update_ctx = i < DEPTH - 1
        mod_x = jax.nn.silu(c) @ w_ada[i] + b_ada[i]
        mod_c = (jax.nn.silu(c_ctx) @ w_ada[i] + b_ada[i])[None]
        shift_x, scale_x, gate_x = jnp.split(mod_x, 3, axis=-1)
        shift_c, scale_c, gate_c = jnp.split(mod_c, 3, axis=-1)

        proj_x = _modulate(x, shift_x, scale_x) @ w_in[i]
        hc = _modulate(ctx, shift_c, scale_c)
        ub_c = hc @ w_in[i][:, col_b0:col_b1]
        uv_x, za_x = proj_x[..., :2 * W_A], proj_x[..., 2 * W_A:col_b0]
        ub_x, zb_x = proj_x[..., col_b0:col_b1], proj_x[..., col_b1:]

        ya_x = _chunk_mlp(uv_x, n_chunks_lat, sgu_ln_g[i], sgu_ln_b[i],
                          w_spatial[i], b_spatial[i]) * jax.nn.silu(za_x)
        yb_x, yb_c = _s5_branch(ub_x, ub_c, s5_lam_re[i], s5_lam_im[i], s5_log_step[i],
                                s5_b_re[i], s5_b_im[i], s5_c_re[i], s5_c_im[i],
                                s5_d[i], w_glu[i], b_glu[i], update_ctx)
        yb_x = yb_x * jax.nn.silu(zb_x)

        out_x = jnp.concatenate([ya_x, yb_x], axis=-1) @ w_out[i]
        x_new = _layer_norm(ALPHA * x + gate_x[:, None, :] * out_x) * ln_g[i] + ln_b[i]

        if update_ctx:
            rest_c = hc @ w_in[i]
            ya_c = _chunk_mlp(rest_c[..., :2 * W_A], n_chunks_ctx, sgu_ln_g[i], sgu_ln_b[i],
                              w_spatial[i], b_spatial[i]) * jax.nn.silu(rest_c[..., 2 * W_A:col_b0])
            yb_c = yb_c * jax.nn.silu(rest_c[..., col_b1:])
            out_c = jnp.concatenate([ya_c, yb_c], axis=-1) @ w_out[i]
            ctx = _layer_norm(ALPHA * ctx + gate_c[:, None, :] * out_c) * ln_g[i] + ln_b[i]
        x = x_new
    return x


import jax as _jax
import jax.numpy as _jnp

TWIN_FORMAT = 'train_step'
FWD_PARAMS = ['x', 'c', 'ctx', 'c_ctx', 'w_ada', 'b_ada', 'w_in', 'sgu_ln_g', 'sgu_ln_b', 'w_spatial', 'b_spatial', 's5_lam_re', 's5_lam_im', 's5_log_step', 's5_b_re', 's5_b_im', 's5_c_re', 's5_c_im', 's5_d', 'w_glu', 'b_glu', 'w_out', 'ln_g', 'ln_b']
TWIN_WEIGHTS = ['c_ctx', 'w_ada', 'b_ada', 'w_in', 'sgu_ln_g', 'sgu_ln_b', 'w_spatial', 'b_spatial', 's5_lam_re', 's5_lam_im', 's5_log_step', 's5_b_re', 's5_b_im', 's5_c_re', 's5_c_im', 's5_d', 'w_glu', 'b_glu', 'w_out', 'ln_g', 'ln_b']
TWIN_DIFF_INPUT = 'x'
TWIN_INPUTS = ['x', 'c', 'ctx', 'c_ctx', 'w_ada', 'b_ada', 'w_in', 'sgu_ln_g', 'sgu_ln_b', 'w_spatial', 'b_spatial', 's5_lam_re', 's5_lam_im', 's5_log_step', 's5_b_re', 's5_b_im', 's5_c_re', 's5_c_im', 's5_d', 'w_glu', 'b_glu', 'w_out', 'ln_g', 'ln_b', 'loss_target', 'm_c_ctx', 'm_w_ada', 'm_b_ada', 'm_w_in', 'm_sgu_ln_g', 'm_sgu_ln_b', 'm_w_spatial', 'm_b_spatial', 'm_s5_lam_re', 'm_s5_lam_im', 'm_s5_log_step', 'm_s5_b_re', 'm_s5_b_im', 'm_s5_c_re', 'm_s5_c_im', 'm_s5_d', 'm_w_glu', 'm_b_glu', 'm_w_out', 'm_ln_g', 'm_ln_b', 'v_c_ctx', 'v_w_ada', 'v_b_ada', 'v_w_in', 'v_sgu_ln_g', 'v_sgu_ln_b', 'v_w_spatial', 'v_b_spatial', 'v_s5_lam_re', 'v_s5_lam_im', 'v_s5_log_step', 'v_s5_b_re', 'v_s5_b_im', 'v_s5_c_re', 'v_s5_c_im', 'v_s5_d', 'v_w_glu', 'v_b_glu', 'v_w_out', 'v_ln_g', 'v_ln_b']
TWIN_OUTPUTS = ['loss', 'grad_x', 'grad_c_ctx', 'grad_w_ada', 'grad_b_ada', 'grad_w_in', 'grad_sgu_ln_g', 'grad_sgu_ln_b', 'grad_w_spatial', 'grad_b_spatial', 'grad_s5_lam_re', 'grad_s5_lam_im', 'grad_s5_log_step', 'grad_s5_b_re', 'grad_s5_b_im', 'grad_s5_c_re', 'grad_s5_c_im', 'grad_s5_d', 'grad_w_glu', 'grad_b_glu', 'grad_w_out', 'grad_ln_g', 'grad_ln_b', 'delta_c_ctx', 'delta_w_ada', 'delta_b_ada', 'delta_w_in', 'delta_sgu_ln_g', 'delta_sgu_ln_b', 'delta_w_spatial', 'delta_b_spatial', 'delta_s5_lam_re', 'delta_s5_lam_im', 'delta_s5_log_step', 'delta_s5_b_re', 'delta_s5_b_im', 'delta_s5_c_re', 'delta_s5_c_im', 'delta_s5_d', 'delta_w_glu', 'delta_b_glu', 'delta_w_out', 'delta_ln_g', 'delta_ln_b', 'new_m_c_ctx', 'new_m_w_ada', 'new_m_b_ada', 'new_m_w_in', 'new_m_sgu_ln_g', 'new_m_sgu_ln_b', 'new_m_w_spatial', 'new_m_b_spatial', 'new_m_s5_lam_re', 'new_m_s5_lam_im', 'new_m_s5_log_step', 'new_m_s5_b_re', 'new_m_s5_b_im', 'new_m_s5_c_re', 'new_m_s5_c_im', 'new_m_s5_d', 'new_m_w_glu', 'new_m_b_glu', 'new_m_w_out', 'new_m_ln_g', 'new_m_ln_b', 'new_v_c_ctx', 'new_v_w_ada', 'new_v_b_ada', 'new_v_w_in', 'new_v_sgu_ln_g', 'new_v_sgu_ln_b', 'new_v_w_spatial', 'new_v_b_spatial', 'new_v_s5_lam_re', 'new_v_s5_lam_im', 'new_v_s5_log_step', 'new_v_s5_b_re', 'new_v_s5_b_im', 'new_v_s5_c_re', 'new_v_s5_c_im', 'new_v_s5_d', 'new_v_w_glu', 'new_v_b_glu', 'new_v_w_out', 'new_v_ln_g', 'new_v_ln_b']
TWIN_LEAF_KINDS = {'loss': 'loss', 'grad_x': 'grad_x', 'grad_c_ctx': 'grad_w', 'grad_w_ada': 'grad_w', 'grad_b_ada': 'grad_w', 'grad_w_in': 'grad_w', 'grad_sgu_ln_g': 'grad_w', 'grad_sgu_ln_b': 'grad_w', 'grad_w_spatial': 'grad_w', 'grad_b_spatial': 'grad_w', 'grad_s5_lam_re': 'grad_w', 'grad_s5_lam_im': 'grad_w', 'grad_s5_log_step': 'grad_w', 'grad_s5_b_re': 'grad_w', 'grad_s5_b_im': 'grad_w', 'grad_s5_c_re': 'grad_w', 'grad_s5_c_im': 'grad_w', 'grad_s5_d': 'grad_w', 'grad_w_glu': 'grad_w', 'grad_b_glu': 'grad_w', 'grad_w_out': 'grad_w', 'grad_ln_g': 'grad_w', 'grad_ln_b': 'grad_w', 'delta_c_ctx': 'delta_w', 'delta_w_ada': 'delta_w', 'delta_b_ada': 'delta_w', 'delta_w_in': 'delta_w', 'delta_sgu_ln_g': 'delta_w', 'delta_sgu_ln_b': 'delta_w', 'delta_w_spatial': 'delta_w', 'delta_b_spatial': 'delta_w', 'delta_s5_lam_re': 'delta_w', 'delta_s5_lam_im': 'delta_w', 'delta_s5_log_step': 'delta_w', 'delta_s5_b_re': 'delta_w', 'delta_s5_b_im': 'delta_w', 'delta_s5_c_re': 'delta_w', 'delta_s5_c_im': 'delta_w', 'delta_s5_d': 'delta_w', 'delta_w_glu': 'delta_w', 'delta_b_glu': 'delta_w', 'delta_w_out': 'delta_w', 'delta_ln_g': 'delta_w', 'delta_ln_b': 'delta_w', 'new_m_c_ctx': 'new_m', 'new_m_w_ada': 'new_m', 'new_m_b_ada': 'new_m', 'new_m_w_in': 'new_m', 'new_m_sgu_ln_g': 'new_m', 'new_m_sgu_ln_b': 'new_m', 'new_m_w_spatial': 'new_m', 'new_m_b_spatial': 'new_m', 'new_m_s5_lam_re': 'new_m', 'new_m_s5_lam_im': 'new_m', 'new_m_s5_log_step': 'new_m', 'new_m_s5_b_re': 'new_m', 'new_m_s5_b_im': 'new_m', 'new_m_s5_c_re': 'new_m', 'new_m_s5_c_im': 'new_m', 'new_m_s5_d': 'new_m', 'new_m_w_glu': 'new_m', 'new_m_b_glu': 'new_m', 'new_m_w_out': 'new_m', 'new_m_ln_g': 'new_m', 'new_m_ln_b': 'new_m', 'new_v_c_ctx': 'new_v', 'new_v_w_ada': 'new_v', 'new_v_b_ada': 'new_v', 'new_v_w_in': 'new_v', 'new_v_sgu_ln_g': 'new_v', 'new_v_sgu_ln_b': 'new_v', 'new_v_w_spatial': 'new_v', 'new_v_b_spatial': 'new_v', 'new_v_s5_lam_re': 'new_v', 'new_v_s5_lam_im': 'new_v', 'new_v_s5_log_step': 'new_v', 'new_v_s5_b_re': 'new_v', 'new_v_s5_b_im': 'new_v', 'new_v_s5_c_re': 'new_v', 'new_v_s5_c_im': 'new_v', 'new_v_s5_d': 'new_v', 'new_v_w_glu': 'new_v', 'new_v_b_glu': 'new_v', 'new_v_w_out': 'new_v', 'new_v_ln_g': 'new_v', 'new_v_ln_b': 'new_v'}


def _forward(args):
    return _fwd_reference(*[args[k] for k in FWD_PARAMS])


def _output_shape():
    def fwd():
        inp = _fwd_setup_inputs(0)
        return _fwd_reference(*[inp[k] for k in FWD_PARAMS])
    out = _jax.eval_shape(fwd)
    return out.shape, out.dtype

N_MICROBATCH = 1
ADAM_LR = 0.001
ADAM_B1 = 0.9
ADAM_B2 = 0.999
ADAM_EPS = 1e-08
ADAM_WD = 0.01
ADAM_STEP = 10
PER_EXAMPLE_BATCH_AXIS = {'x': 0, 'c': 0, 'ctx': 0, 'loss_target': 0}
SHARED_INPUTS = []
_WEIGHT_DTYPES = {'c_ctx': _jnp.float32, 'w_ada': _jnp.float32, 'b_ada': _jnp.float32, 'w_in': _jnp.float32, 'sgu_ln_g': _jnp.float32, 'sgu_ln_b': _jnp.float32, 'w_spatial': _jnp.float32, 'b_spatial': _jnp.float32, 's5_lam_re': _jnp.float32, 's5_lam_im': _jnp.float32, 's5_log_step': _jnp.float32, 's5_b_re': _jnp.float32, 's5_b_im': _jnp.float32, 's5_c_re': _jnp.float32, 's5_c_im': _jnp.float32, 's5_d': _jnp.float32, 'w_glu': _jnp.float32, 'b_glu': _jnp.float32, 'w_out': _jnp.float32, 'ln_g': _jnp.float32, 'ln_b': _jnp.float32}
MOMENT_SCALE = {'c_ctx': 2.590185e-03, 'w_ada': 2.910885e-02, 'b_ada': 5.749987e-02, 'w_in': 3.512412e-02, 'sgu_ln_g': 3.017529e-02, 'sgu_ln_b': 2.909048e-02, 'w_spatial': 2.987549e-02, 'b_spatial': 3.017400e-02, 's5_lam_re': 1.014399e-02, 's5_lam_im': 9.856005e-03, 's5_log_step': 3.505230e+00, 's5_b_re': 5.854183e-03, 's5_b_im': 6.016171e-03, 's5_c_re': 1.466018e-03, 's5_c_im': 1.515957e-03, 's5_d': 1.874918e-02, 'w_glu': 8.468385e-03, 'b_glu': 8.828642e-03, 'w_out': 6.024727e-02, 'ln_g': 3.194604e+01, 'ln_b': 1.083696e+00}


def _to_microbatches(a, axis):
    t = _jnp.moveaxis(a, axis, 0)
    t = t.reshape((N_MICROBATCH, t.shape[0] // N_MICROBATCH) + t.shape[1:])
    return _jnp.moveaxis(t, 1, axis + 1)


def setup_inputs(seed: int = 0) -> dict:
    inp = _fwd_setup_inputs(seed)
    key = _jax.random.fold_in(_jax.random.key(seed), 7919)
    shape, _ = _output_shape()
    out = dict(inp)
    out["loss_target"] = _jax.random.normal(_jax.random.fold_in(key, 0), shape, _jnp.float32)
    for i, name in enumerate(TWIN_WEIGHTS):
        w = inp[name].astype(_jnp.float32)
        if MOMENT_SCALE is None:
            s = _jnp.sqrt(_jnp.mean(_jnp.square(w)) + 1e-30)
        else:
            s = MOMENT_SCALE[name]
        km, kv = _jax.random.split(_jax.random.fold_in(key, i + 1))
        out[name] = w
        out["m_" + name] = s * _jax.random.normal(km, w.shape, _jnp.float32)
        out["v_" + name] = (s * s) * _jax.random.uniform(kv, w.shape, _jnp.float32, 0.5, 1.5)
    if N_MICROBATCH > 1:
        for name, axis in PER_EXAMPLE_BATCH_AXIS.items():
            out[name] = _to_microbatches(out[name], axis)
    return {'x': out['x'], 'c': out['c'], 'ctx': out['ctx'], 'c_ctx': out['c_ctx'], 'w_ada': out['w_ada'], 'b_ada': out['b_ada'], 'w_in': out['w_in'], 'sgu_ln_g': out['sgu_ln_g'], 'sgu_ln_b': out['sgu_ln_b'], 'w_spatial': out['w_spatial'], 'b_spatial': out['b_spatial'], 's5_lam_re': out['s5_lam_re'], 's5_lam_im': out['s5_lam_im'], 's5_log_step': out['s5_log_step'], 's5_b_re': out['s5_b_re'], 's5_b_im': out['s5_b_im'], 's5_c_re': out['s5_c_re'], 's5_c_im': out['s5_c_im'], 's5_d': out['s5_d'], 'w_glu': out['w_glu'], 'b_glu': out['b_glu'], 'w_out': out['w_out'], 'ln_g': out['ln_g'], 'ln_b': out['ln_b'], 'loss_target': out['loss_target'], 'm_c_ctx': out['m_c_ctx'], 'm_w_ada': out['m_w_ada'], 'm_b_ada': out['m_b_ada'], 'm_w_in': out['m_w_in'], 'm_sgu_ln_g': out['m_sgu_ln_g'], 'm_sgu_ln_b': out['m_sgu_ln_b'], 'm_w_spatial': out['m_w_spatial'], 'm_b_spatial': out['m_b_spatial'], 'm_s5_lam_re': out['m_s5_lam_re'], 'm_s5_lam_im': out['m_s5_lam_im'], 'm_s5_log_step': out['m_s5_log_step'], 'm_s5_b_re': out['m_s5_b_re'], 'm_s5_b_im': out['m_s5_b_im'], 'm_s5_c_re': out['m_s5_c_re'], 'm_s5_c_im': out['m_s5_c_im'], 'm_s5_d': out['m_s5_d'], 'm_w_glu': out['m_w_glu'], 'm_b_glu': out['m_b_glu'], 'm_w_out': out['m_w_out'], 'm_ln_g': out['m_ln_g'], 'm_ln_b': out['m_ln_b'], 'v_c_ctx': out['v_c_ctx'], 'v_w_ada': out['v_w_ada'], 'v_b_ada': out['v_b_ada'], 'v_w_in': out['v_w_in'], 'v_sgu_ln_g': out['v_sgu_ln_g'], 'v_sgu_ln_b': out['v_sgu_ln_b'], 'v_w_spatial': out['v_w_spatial'], 'v_b_spatial': out['v_b_spatial'], 'v_s5_lam_re': out['v_s5_lam_re'], 'v_s5_lam_im': out['v_s5_lam_im'], 'v_s5_log_step': out['v_s5_log_step'], 'v_s5_b_re': out['v_s5_b_re'], 'v_s5_b_im': out['v_s5_b_im'], 'v_s5_c_re': out['v_s5_c_re'], 'v_s5_c_im': out['v_s5_c_im'], 'v_s5_d': out['v_s5_d'], 'v_w_glu': out['v_w_glu'], 'v_b_glu': out['v_b_glu'], 'v_w_out': out['v_w_out'], 'v_ln_g': out['v_ln_g'], 'v_ln_b': out['v_ln_b']}


def _loss(weights, diff, rest, loss_target):
    with _jax.named_scope("forward"):
        args = {**rest, TWIN_DIFF_INPUT: diff, **{k: w.astype(_WEIGHT_DTYPES[k]) for k, w in weights.items()}}
        y = _forward(args)
    with _jax.named_scope("loss_head"):
        err = _jnp.square(y.astype(_jnp.float32) - loss_target)
        return 0.5 * _jnp.sum(_jnp.mean(err, axis=-1)) if err.ndim else 0.5 * err


def _adamw(w, g, m, v):
    m = ADAM_B1 * m + (1.0 - ADAM_B1) * g
    v = ADAM_B2 * v + (1.0 - ADAM_B2) * _jnp.square(g)
    m_hat = m / (1.0 - ADAM_B1 ** ADAM_STEP)
    v_hat = v / (1.0 - ADAM_B2 ** ADAM_STEP)
    delta = -ADAM_LR * (m_hat / (_jnp.sqrt(v_hat) + ADAM_EPS) + ADAM_WD * w)
    return delta, m, v


def reference(x, c, ctx, c_ctx, w_ada, b_ada, w_in, sgu_ln_g, sgu_ln_b, w_spatial, b_spatial, s5_lam_re, s5_lam_im, s5_log_step, s5_b_re, s5_b_im, s5_c_re, s5_c_im, s5_d, w_glu, b_glu, w_out, ln_g, ln_b, loss_target, m_c_ctx, m_w_ada, m_b_ada, m_w_in, m_sgu_ln_g, m_sgu_ln_b, m_w_spatial, m_b_spatial, m_s5_lam_re, m_s5_lam_im, m_s5_log_step, m_s5_b_re, m_s5_b_im, m_s5_c_re, m_s5_c_im, m_s5_d, m_w_glu, m_b_glu, m_w_out, m_ln_g, m_ln_b, v_c_ctx, v_w_ada, v_b_ada, v_w_in, v_sgu_ln_g, v_sgu_ln_b, v_w_spatial, v_b_spatial, v_s5_lam_re, v_s5_lam_im, v_s5_log_step, v_s5_b_re, v_s5_b_im, v_s5_c_re, v_s5_c_im, v_s5_d, v_w_glu, v_b_glu, v_w_out, v_ln_g, v_ln_b):
    given = dict(x=x, c=c, ctx=ctx, c_ctx=c_ctx, w_ada=w_ada, b_ada=b_ada, w_in=w_in, sgu_ln_g=sgu_ln_g, sgu_ln_b=sgu_ln_b, w_spatial=w_spatial, b_spatial=b_spatial, s5_lam_re=s5_lam_re, s5_lam_im=s5_lam_im, s5_log_step=s5_log_step, s5_b_re=s5_b_re, s5_b_im=s5_b_im, s5_c_re=s5_c_re, s5_c_im=s5_c_im, s5_d=s5_d, w_glu=w_glu, b_glu=b_glu, w_out=w_out, ln_g=ln_g, ln_b=ln_b, loss_target=loss_target, m_c_ctx=m_c_ctx, m_w_ada=m_w_ada, m_b_ada=m_b_ada, m_w_in=m_w_in, m_sgu_ln_g=m_sgu_ln_g, m_sgu_ln_b=m_sgu_ln_b, m_w_spatial=m_w_spatial, m_b_spatial=m_b_spatial, m_s5_lam_re=m_s5_lam_re, m_s5_lam_im=m_s5_lam_im, m_s5_log_step=m_s5_log_step, m_s5_b_re=m_s5_b_re, m_s5_b_im=m_s5_b_im, m_s5_c_re=m_s5_c_re, m_s5_c_im=m_s5_c_im, m_s5_d=m_s5_d, m_w_glu=m_w_glu, m_b_glu=m_b_glu, m_w_out=m_w_out, m_ln_g=m_ln_g, m_ln_b=m_ln_b, v_c_ctx=v_c_ctx, v_w_ada=v_w_ada, v_b_ada=v_b_ada, v_w_in=v_w_in, v_sgu_ln_g=v_sgu_ln_g, v_sgu_ln_b=v_sgu_ln_b, v_w_spatial=v_w_spatial, v_b_spatial=v_b_spatial, v_s5_lam_re=v_s5_lam_re, v_s5_lam_im=v_s5_lam_im, v_s5_log_step=v_s5_log_step, v_s5_b_re=v_s5_b_re, v_s5_b_im=v_s5_b_im, v_s5_c_re=v_s5_c_re, v_s5_c_im=v_s5_c_im, v_s5_d=v_s5_d, v_w_glu=v_w_glu, v_b_glu=v_b_glu, v_w_out=v_w_out, v_ln_g=v_ln_g, v_ln_b=v_ln_b)
    weights = {n: given[n] for n in TWIN_WEIGHTS}
    shared = {n: given[n] for n in SHARED_INPUTS}
    per_example = {n: given[n] for n in ['x', 'c', 'ctx']}
    grad_fn = _jax.value_and_grad(_loss, argnums=(0, 1))

    def one_microbatch(ex, loss_target):
        ex = dict(ex)
        diff = ex.pop(TWIN_DIFF_INPUT)
        return grad_fn(weights, diff, {**shared, **ex}, loss_target)

    if N_MICROBATCH == 1:
        loss, (grad_w, grad_x) = one_microbatch(per_example, given["loss_target"])
    else:
        def body(carry, xs):
            loss_sum, grad_sum = carry
            l_k, (gw_k, gx_k) = one_microbatch(xs[0], xs[1])
            with _jax.named_scope("update"):
                return (loss_sum + l_k, _jax.tree.map(_jnp.add, grad_sum, gw_k)), gx_k

        init = (_jnp.zeros((), _jnp.float32), _jax.tree.map(_jnp.zeros_like, weights))
        (loss, grad_w), grad_x = _jax.lax.scan(body, init, (per_example, given["loss_target"]))
    with _jax.named_scope("update"):
        delta_w, new_m, new_v = {}, {}, {}
        for n in TWIN_WEIGHTS:
            delta_w[n], new_m[n], new_v[n] = _adamw(weights[n], grad_w[n], given["m_" + n], given["v_" + n])
    return (loss, grad_x, *[grad_w[n] for n in TWIN_WEIGHTS], *[delta_w[n] for n in TWIN_WEIGHTS],
            *[new_m[n] for n in TWIN_WEIGHTS], *[new_v[n] for n in TWIN_WEIGHTS])
```

```python
import functools
import math

import jax
import jax.numpy as jnp
from jax import lax
from jax.experimental import pallas as pl
from jax.experimental.pallas import tpu as pltpu

F32 = jnp.float32
MXU_DTYPE = jnp.bfloat16
N_DEV = 8
MESH_ID = pl.DeviceIdType.MESH
LN_EPS = 1e-6
DEPTH = 1
ALPHA = (2.0 * DEPTH) ** 0.25
CHUNK = 128
HEAD_DIM_A = 128
ADAM_LR, ADAM_B1, ADAM_B2, ADAM_EPS, ADAM_WD, ADAM_STEP = 0.001, 0.9, 0.999, 1e-08, 0.01, 10
LANES = 128
VMEM_LIMIT = 56 * 1024 * 1024
HBM = pl.BlockSpec(memory_space=pl.ANY)


def _cparams(*sem):
    return pltpu.CompilerParams(dimension_semantics=sem if sem else None, vmem_limit_bytes=VMEM_LIMIT)


def _tile(n, pref, mult=1):
    if n <= pref:
        return n
    t = pref - pref % mult
    while n % t:
        t -= mult
    return t


def _gelu(x):
    return 0.5 * x * (1.0 + lax.erf(x * (1.0 / math.sqrt(2.0))))


def _gelu_grad(x):
    return 0.5 * (1.0 + lax.erf(x * (1.0 / math.sqrt(2.0)))) + x * jnp.exp(-0.5 * x * x) * (1.0 / math.sqrt(2.0 * math.pi))


def _silu_grad(x):
    s = jax.nn.sigmoid(x)
    return s * (1.0 + x * (1.0 - s))


def _mxu_dot(a, b, dims=(((1,), (0,)), ((), ()))):
    return lax.dot_general(a.astype(MXU_DTYPE), b.astype(MXU_DTYPE), dims, preferred_element_type=F32)


_NT = (((1,), (1,)), ((), ()))
_TN = (((0,), (0,)), ((), ()))


def _mesh_pos():
    return lax.axis_index("x"), lax.axis_index("y"), lax.axis_index("c")


def _peer(pos, r):
    x, y, c = pos
    return ((1 - x) if r & 4 else x, (1 - y) if r & 2 else y, (1 - c) if r & 1 else c)


def _index(pos):
    return 4 * pos[0] + 2 * pos[1] + pos[2]


def _slice_of(ref, axis, idx, size):
    start = idx * size
    if axis == 0:
        return ref.at[pl.ds(start, size)]
    return ref.at[:, pl.ds(start, size)]


def _all_gather(x, axis, name):
    size = x.shape[axis]
    out_shape = tuple(s * N_DEV if a == axis else s for a, s in enumerate(x.shape))

    def body(x_ref, o_ref, send_sems, recv_sems, local_sem):
        me = _mesh_pos()
        mine = pltpu.make_async_copy(x_ref, _slice_of(o_ref, axis, _index(me), size), local_sem)
        mine.start()

        def copy(r, block):
            return pltpu.make_async_remote_copy(
                src_ref=x_ref, dst_ref=_slice_of(o_ref, axis, _index(block), size),
                send_sem=send_sems.at[r - 1], recv_sem=recv_sems.at[r - 1],
                device_id=_peer(me, r), device_id_type=MESH_ID)

        sends = [copy(r, me) for r in range(1, N_DEV)]
        for cp in sends:
            cp.start()
        for r in range(1, N_DEV):
            copy(r, _peer(me, r)).wait_recv()
        for cp in sends:
            cp.wait_send()
        mine.wait()

    return pl.pallas_call(
        body, name=name, out_shape=jax.ShapeDtypeStruct(out_shape, x.dtype),
        in_specs=[HBM], out_specs=HBM,
        scratch_shapes=[pltpu.SemaphoreType.DMA((N_DEV - 1,)), pltpu.SemaphoreType.DMA((N_DEV - 1,)),
                        pltpu.SemaphoreType.DMA],
    )(x)


def _all_to_all(x, axis, name):
    size = x.shape[axis] // N_DEV
    slot = tuple(size if a == axis else s for a, s in enumerate(x.shape))

    def body(x_ref, o_ref, send_sems, recv_sems, local_sem):
        me = _mesh_pos()
        mine = pltpu.make_async_copy(_slice_of(x_ref, axis, _index(me), size), o_ref.at[_index(me)], local_sem)
        mine.start()

        def copy(r, sender, receiver):
            return pltpu.make_async_remote_copy(
                src_ref=_slice_of(x_ref, axis, _index(receiver), size), dst_ref=o_ref.at[_index(sender)],
                send_sem=send_sems.at[r - 1], recv_sem=recv_sems.at[r - 1],
                device_id=_peer(me, r), device_id_type=MESH_ID)

        sends = [copy(r, me, _peer(me, r)) for r in range(1, N_DEV)]
        for cp in sends:
            cp.start()
        for r in range(1, N_DEV):
            copy(r, _peer(me, r), me).wait_recv()
        for cp in sends:
            cp.wait_send()
        mine.wait()

    return pl.pallas_call(
        body, name=name, out_shape=jax.ShapeDtypeStruct((N_DEV,) + slot, x.dtype),
        in_specs=[HBM], out_specs=HBM,
        scratch_shapes=[pltpu.SemaphoreType.DMA((N_DEV - 1,)), pltpu.SemaphoreType.DMA((N_DEV - 1,)),
                        pltpu.SemaphoreType.DMA],
    )(x)


def _matmul(a, b, *, mode, name, bm=512, bn=512, bk=512, out_dtype=F32, b_n0=0, n=None, b_k0=0, k=None,
            acc_in=None, acc_n0=0):
    if mode == "tn":
        kk, m = a.shape
    else:
        m, kk = a.shape
    if mode == "nn":
        n = b.shape[1] if n is None else n
    elif mode == "nt":
        n = b.shape[0]
        kk = kk if k is None else k
    else:
        n = b.shape[1]
    bm, bn, bk = _tile(m, bm), _tile(n, bn), _tile(kk, bk)
    nk = kk // bk
    assert b_n0 % bn == 0 and b_k0 % bk == 0 and acc_n0 % bn == 0
    dims = {"nn": (((1,), (0,)), ((), ())), "nt": _NT, "tn": _TN}[mode]

    def body(*refs):
        if acc_in is None:
            a_ref, b_ref, o_ref = refs[:3]
            init = None
        else:
            a_ref, b_ref, init, o_ref = refs[:4]
        acc_ref = refs[-1] if nk > 1 else None
        p = _mxu_dot(a_ref[...], b_ref[...], dims)
        if nk == 1:
            o_ref[...] = (p if init is None else p + init[...]).astype(out_dtype)
            return
        ki = pl.program_id(2)

        @pl.when(ki == 0)
        def _():
            acc_ref[...] = p if init is None else p + init[...]

        @pl.when(ki > 0)
        def _():
            acc_ref[...] += p

        @pl.when(ki == nk - 1)
        def _():
            o_ref[...] = acc_ref[...].astype(out_dtype)

    a_spec = pl.BlockSpec((bk, bm), lambda j, i, q: (q, i)) if mode == "tn" else pl.BlockSpec((bm, bk), lambda j, i, q: (i, q))
    if mode == "nt":
        b_spec = pl.BlockSpec((bn, bk), lambda j, i, q: (j, q + b_k0 // bk))
    else:
        b_spec = pl.BlockSpec((bk, bn), lambda j, i, q: (q, j + b_n0 // bn))
    in_specs, args, aliases = [a_spec, b_spec], [a, b], {}
    out_map = lambda j, i, q: (i, j + acc_n0 // bn)
    if acc_in is not None:
        in_specs.append(pl.BlockSpec((bm, bn), out_map))
        args.append(acc_in)
        aliases = {2: 0}
        out_shape = jax.ShapeDtypeStruct(acc_in.shape, out_dtype)
    else:
        out_shape = jax.ShapeDtypeStruct((m, n), out_dtype)
    return pl.pallas_call(
        body, name=name, out_shape=out_shape, grid=(n // bn, m // bm, nk),
        in_specs=in_specs, out_specs=pl.BlockSpec((bm, bn), out_map),
        scratch_shapes=[pltpu.VMEM((bm, bn), F32)] if nk > 1 else [],
        input_output_aliases=aliases,
        compiler_params=_cparams("parallel", "parallel", "arbitrary"),
    )(*args)


def _silu_rows(c, c_ctx):
    d = c.shape[-1]

    def body(c_ref, cc_ref, o_ref):
        o_ref[...] = jnp.zeros_like(o_ref)
        o_ref[0:1, :] = jax.nn.silu(c_ref[...])
        o_ref[1:2, :] = jax.nn.silu(cc_ref[...])

    return pl.pallas_call(body, name="silu_rows", out_shape=jax.ShapeDtypeStruct((8, d), F32))(
        c.reshape(1, d), c_ctx.reshape(1, d))


def _small_dot(a, b, mode, name):
    dims = {"nn": (((1,), (0,)), ((), ())), "nt": _NT, "tn": _TN}[mode]
    m = a.shape[1] if mode == "tn" else a.shape[0]
    n = b.shape[0] if mode == "nt" else b.shape[1]

    def body(a_ref, b_ref, o_ref):
        o_ref[...] = lax.dot_general(a_ref[...], b_ref[...], dims, preferred_element_type=F32,
                                     precision=lax.Precision.HIGHEST)

    return pl.pallas_call(body, name=name, out_shape=jax.ShapeDtypeStruct((m, n), F32),
                          compiler_params=_cparams())(a, b)


def _ln_stats(x):
    mu = jnp.mean(x, axis=-1, keepdims=True)
    xc = x - mu
    var = jnp.mean(xc * xc, axis=-1, keepdims=True)
    rstd = lax.rsqrt(var + LN_EPS)
    return xc * rstd, rstd


def _ln_mod(x, shift, scale, name):
    l, d = x.shape
    tl = _tile(l, 256)

    def body(x_ref, sh_ref, sc_ref, o_ref):
        xhat, _ = _ln_stats(x_ref[...])
        o_ref[...] = (xhat * (1.0 + sc_ref[...]) + sh_ref[...]).astype(o_ref.dtype)

    row = pl.BlockSpec((tl, d), lambda i: (i, 0))
    vec = pl.BlockSpec((1, d), lambda i: (0, 0))
    return pl.pallas_call(body, name=name, out_shape=jax.ShapeDtypeStruct((l, d), MXU_DTYPE), grid=(l // tl,),
                          in_specs=[row, vec, vec], out_specs=row, compiler_params=_cparams("parallel"))(x, shift, scale)


def _ln_mod_bwd(x, dxm, scale, res, name):
    l, d = x.shape
    tl = _tile(l, 256)
    with_res = res is not None

    def body(*refs):
        if with_res:
            x_ref, g_ref, sc_ref, r_ref, dx_ref, dsh_ref, dsc_ref = refs
        else:
            x_ref, g_ref, sc_ref, dx_ref, dsh_ref, dsc_ref = refs
        i = pl.program_id(0)
        xhat, rstd = _ln_stats(x_ref[...])
        g = g_ref[...]
        dxh = g * (1.0 + sc_ref[...])
        dx = rstd * (dxh - jnp.mean(dxh, axis=-1, keepdims=True) - xhat * jnp.mean(dxh * xhat, axis=-1, keepdims=True))
        dx_ref[...] = dx + r_ref[...] if with_res else dx

        @pl.when(i == 0)
        def _():
            dsh_ref[...] = jnp.zeros_like(dsh_ref)
            dsc_ref[...] = jnp.zeros_like(dsc_ref)

        dsh_ref[...] += jnp.sum(g, axis=0, keepdims=True)
        dsc_ref[...] += jnp.sum(g * xhat, axis=0, keepdims=True)

    row = pl.BlockSpec((tl, d), lambda i: (i, 0))
    vec = pl.BlockSpec((1, d), lambda i: (0, 0))
    args = [x, dxm, scale] + ([res] if with_res else [])
    return pl.pallas_call(
        body, name=name,
        out_shape=(jax.ShapeDtypeStruct((l, d), F32), jax.ShapeDtypeStruct((1, d), F32), jax.ShapeDtypeStruct((1, d), F32)),
        grid=(l // tl,), in_specs=[row, row, vec] + ([row] if with_res else []), out_specs=(row, vec, vec),
        compiler_params=_cparams("arbitrary"))(*args)


def _post_ln_loss(x, out, gate, ln_g, ln_b, target):
    l, d = x.shape
    tl = _tile(l, 256)

    def body(x_ref, o_ref, gate_ref, g_ref, b_ref, t_ref, loss_ref, dout_ref, dxr_ref, dgate_ref, dg_ref, db_ref):
        i = pl.program_id(0)
        out_t = o_ref[...]
        gate_v = gate_ref[...]
        rhat, rstd = _ln_stats(ALPHA * x_ref[...] + gate_v * out_t)
        ln_gv = g_ref[...]
        diff = rhat * ln_gv + b_ref[...] - t_ref[...]
        dy = diff * (1.0 / d)
        drh = dy * ln_gv
        dr = rstd * (drh - jnp.mean(drh, axis=-1, keepdims=True) - rhat * jnp.mean(drh * rhat, axis=-1, keepdims=True))
        dout_ref[...] = (gate_v * dr).astype(dout_ref.dtype)
        dxr_ref[...] = ALPHA * dr

        @pl.when(i == 0)
        def _():
            for r in (loss_ref, dgate_ref, dg_ref, db_ref):
                r[...] = jnp.zeros_like(r)

        loss_ref[...] += jnp.sum(diff * diff, axis=0, keepdims=True)
        dgate_ref[...] += jnp.sum(dr * out_t, axis=0, keepdims=True)
        dg_ref[...] += jnp.sum(dy * rhat, axis=0, keepdims=True)
        db_ref[...] += jnp.sum(dy, axis=0, keepdims=True)

    row = pl.BlockSpec((tl, d), lambda i: (i, 0))
    vec = pl.BlockSpec((1, d), lambda i: (0, 0))
    v = jax.ShapeDtypeStruct((1, d), F32)
    return pl.pallas_call(
        body, name="post_ln_loss",
        out_shape=(v, jax.ShapeDtypeStruct((l, d), MXU_DTYPE), jax.ShapeDtypeStruct((l, d), F32), v, v, v),
        grid=(l // tl,), in_specs=[row, row, vec, vec, vec, row], out_specs=(vec, row, row, vec, vec, vec),
        compiler_params=_cparams("arbitrary"))(x, out, gate, ln_g, ln_b, target)


def _ga_forward_tile(p, g, b, ws_ref, bsf, w, nc, nh):
    u_raw, v_raw, za = p[:, :w], p[:, w:2 * w], p[:, 2 * w:3 * w]
    gu = _gelu(u_raw)
    vhat, rstd = _ln_stats(_gelu(v_raw))
    vn = vhat * g + b
    rows = []
    for ci in range(nc):
        r0 = ci * CHUNK
        heads = [_mxu_dot(ws_ref[h], vn[r0:r0 + CHUNK, h * HEAD_DIM_A:(h + 1) * HEAD_DIM_A]) for h in range(nh)]
        rows.append(jnp.concatenate(heads, axis=1) + bsf)
    mixed = jnp.concatenate(rows, axis=0) if nc > 1 else rows[0]
    return u_raw, v_raw, za, gu, vhat, rstd, vn, mixed


def _ga_fwd(proj, g, b, ws, bsf, w):
    l = proj.shape[0]
    nh = w // HEAD_DIM_A
    nc = _tile(l // CHUNK, 2)
    tl = nc * CHUNK

    def body(p_ref, g_ref, b_ref, ws_ref, bsf_ref, o_ref):
        _, _, za, gu, _, _, _, mixed = _ga_forward_tile(p_ref[...], g_ref[...], b_ref[...], ws_ref, bsf_ref[...], w, nc, nh)
        o_ref[...] = (gu * mixed * jax.nn.silu(za)).astype(o_ref.dtype)

    vec = pl.BlockSpec((1, w), lambda i: (0, 0))
    return pl.pallas_call(
        body, name="ga_fwd", out_shape=jax.ShapeDtypeStruct((l, 2 * w), MXU_DTYPE), grid=(l // tl,),
        in_specs=[pl.BlockSpec((tl, 3 * w), lambda i: (i, 0)), vec, vec,
                  pl.BlockSpec((nh, CHUNK, CHUNK), lambda i: (0, 0, 0)), pl.BlockSpec((CHUNK, w), lambda i: (0, 0))],
        out_specs=pl.BlockSpec((tl, w), lambda i: (i, 0)), compiler_params=_cparams("parallel"))(proj, g, b, ws, bsf)


def _ga_bwd(proj, dcat, dproj, g, b, ws, bsf, w):
    l = proj.shape[0]
    nh = w // HEAD_DIM_A
    nc = _tile(l // CHUNK, 2)
    tl = nc * CHUNK

    def body(p_ref, dy_ref, dp_in, g_ref, b_ref, ws_ref, bsf_ref, dp_ref, dg_ref, db_ref, dws_ref, dbsf_ref):
        del dp_in
        i = pl.program_id(0)
        gv = g_ref[...]
        u_raw, v_raw, za, gu, vhat, rstd, vn, mixed = _ga_forward_tile(
            p_ref[...], gv, b_ref[...], ws_ref, bsf_ref[...], w, nc, nh)
        dya = dy_ref[...].astype(F32)
        sz = jax.nn.silu(za)
        dmixed = dya * gu * sz
        dza = dya * gu * mixed * _silu_grad(za)
        dgu = dya * mixed * sz

        @pl.when(i == 0)
        def _():
            for r in (dg_ref, db_ref, dws_ref, dbsf_ref):
                r[...] = jnp.zeros_like(r)

        rows = []
        for ci in range(nc):
            r0 = ci * CHUNK
            heads = []
            for h in range(nh):
                cols = slice(h * HEAD_DIM_A, (h + 1) * HEAD_DIM_A)
                dm = dmixed[r0:r0 + CHUNK, cols]
                heads.append(_mxu_dot(ws_ref[h], dm, _TN))
                dws_ref[h] += _mxu_dot(dm, vn[r0:r0 + CHUNK, cols], _NT)
            rows.append(jnp.concatenate(heads, axis=1))
            dbsf_ref[...] += dmixed[r0:r0 + CHUNK, :]
        dvn = jnp.concatenate(rows, axis=0) if nc > 1 else rows[0]
        dg_ref[...] += jnp.sum(dvn * vhat, axis=0, keepdims=True)
        db_ref[...] += jnp.sum(dvn, axis=0, keepdims=True)
        dvh = dvn * gv
        dgv = rstd * (dvh - jnp.mean(dvh, axis=-1, keepdims=True) - vhat * jnp.mean(dvh * vhat, axis=-1, keepdims=True))
        dp_ref[:, :w] = dgu * _gelu_grad(u_raw)
        dp_ref[:, w:2 * w] = dgv * _gelu_grad(v_raw)
        dp_ref[:, 2 * w:] = dza

    vec = pl.BlockSpec((1, w), lambda i: (0, 0))
    ws_spec = pl.BlockSpec((nh, CHUNK, CHUNK), lambda i: (0, 0, 0))
    bs_spec = pl.BlockSpec((CHUNK, w), lambda i: (0, 0))
    v = jax.ShapeDtypeStruct((1, w), F32)
    return pl.pallas_call(
        body, name="ga_bwd",
        out_shape=(jax.ShapeDtypeStruct(dproj.shape, F32), v, v, jax.ShapeDtypeStruct((nh, CHUNK, CHUNK), F32),
                   jax.ShapeDtypeStruct((CHUNK, w), F32)),
        grid=(l // tl,),
        in_specs=[pl.BlockSpec((tl, 3 * w), lambda i: (i, 0)), pl.BlockSpec((tl, w), lambda i: (i, 0)), HBM,
                  vec, vec, ws_spec, bs_spec],
        out_specs=(pl.BlockSpec((tl, 3 * w), lambda i: (i, 0)), vec, vec, ws_spec, bs_spec),
        input_output_aliases={2: 0}, compiler_params=_cparams("arbitrary"))(proj, dcat, dproj, g, b, ws, bsf)


def _lane_group_sum(x, expand, name):
    return _small_dot(x, expand, "nn", name)


def _disc_math(lr, li, ls, br, bi):
    step = jnp.exp(ls)
    dr, di = lr * step, li * step
    mag = jnp.exp(dr)
    ab_re, ab_im = mag * jnp.cos(di), mag * jnp.sin(di)
    den = lr * lr + li * li
    nr, ni = ab_re - 1.0, ab_im
    f_re = (nr * lr + ni * li) / den
    f_im = (ni * lr - nr * li) / den
    bb_re = f_re[:, None] * br - f_im[:, None] * bi
    bb_im = f_re[:, None] * bi + f_im[:, None] * br
    return ab_re, ab_im, bb_re, bb_im


def _disc_fwd(lr, li, ls, br, bi):
    def body(lr_ref, li_ref, ls_ref, br_ref, bi_ref, o1, o2, o3, o4):
        res = _disc_math(lr_ref[...], li_ref[...], ls_ref[...], br_ref[...], bi_ref[...])
        for o, r in zip((o1, o2, o3, o4), res):
            o[...] = r

    s = lambda a: jax.ShapeDtypeStruct(a.shape, F32)
    return pl.pallas_call(body, name="s5_disc", out_shape=(s(lr), s(lr), s(br), s(br)), compiler_params=_cparams())(
        lr, li, ls, br, bi)


def _disc_bwd(lr, li, ls, br, bi, d_ar, d_ai, d_br, d_bi):
    def body(lr_ref, li_ref, ls_ref, br_ref, bi_ref, c1, c2, c3, c4, o1, o2, o3, o4, o5):
        _, vjp = jax.vjp(_disc_math, lr_ref[...], li_ref[...], ls_ref[...], br_ref[...], bi_ref[...])
        res = vjp((c1[...], c2[...], c3[...], c4[...]))
        for o, r in zip((o1, o2, o3, o4, o5), res):
            o[...] = r

    s = lambda a: jax.ShapeDtypeStruct(a.shape, F32)
    return pl.pallas_call(body, name="s5_disc_bwd", out_shape=(s(lr), s(lr), s(ls), s(br), s(br)),
                          compiler_params=_cparams())(lr, li, ls, br, bi, d_ar, d_ai, d_br, d_bi)


def _s5_fwd(u_arr, u_col, w, h0, a_sm, wbr, wbi, cre, ncim, rev, name):
    l = u_arr.shape[0]
    nb = w // LANES
    spb = wbr.shape[-1]
    nsr = a_sm.shape[1]
    assert 2 * spb == 8 * LANES and nb % 2 == 0
    t = _tile(l, 256)
    n = l // t
    tile = (lambda i: n - 1 - i) if rev else (lambda i: i)

    def body(u_ref, h0_ref, a_ref, wbr_ref, wbi_ref, cre_ref, ncim_ref, y_ref, hr_ref, hi_ref, hfin_ref, carry_ref):
        i = pl.program_id(0)

        @pl.when(i == 0)
        def _():
            carry_ref[...] = h0_ref[...]

        u = u_ref[...].astype(MXU_DTYPE)
        for j in range(nb // 2):
            for h_ref, w_ref in ((hr_ref, wbr_ref), (hi_ref, wbi_ref)):
                blk = [_mxu_dot(u[:, k * LANES:(k + 1) * LANES], w_ref[k]) for k in (2 * j, 2 * j + 1)]
                h_ref[:, 8 * j:8 * j + 8, :] = jnp.concatenate(blk, axis=1).reshape(t, 8, LANES)
        ar, ai = a_ref[0], a_ref[1]

        def step(s, c):
            hr, hi = c
            row = t - 1 - s if rev else s
            nr = ar * hr - ai * hi + hr_ref[row]
            ni = ar * hi + ai * hr + hi_ref[row]
            hr_ref[row] = nr
            hi_ref[row] = ni
            return nr, ni

        c = lax.fori_loop(0, t, step, (carry_ref[0], carry_ref[1]), unroll=4)
        for q in range(2):
            carry_ref[q] = c[q]
            hfin_ref[q] = c[q]
        for j in range(nb // 2):
            hr = hr_ref[:, 8 * j:8 * j + 8, :].reshape(t, 8 * LANES)
            hi = hi_ref[:, 8 * j:8 * j + 8, :].reshape(t, 8 * LANES)
            for kk in range(2):
                k = 2 * j + kk
                cols = slice(kk * spb, (kk + 1) * spb)
                y_ref[:, k * LANES:(k + 1) * LANES] = (_mxu_dot(hr[:, cols], cre_ref[k]) + _mxu_dot(hi[:, cols], ncim_ref[k]))

    full = lambda a: pl.BlockSpec(a.shape, lambda i: (0,) * a.ndim)
    hspec = pl.BlockSpec((t, nsr, LANES), lambda i: (tile(i), 0, 0))
    hsh = jax.ShapeDtypeStruct((l, nsr, LANES), F32)
    return pl.pallas_call(
        body, name=name, out_shape=(jax.ShapeDtypeStruct((l, w), F32), hsh, hsh, jax.ShapeDtypeStruct((2, nsr, LANES), F32)),
        grid=(n,),
        in_specs=[pl.BlockSpec((t, w), lambda i: (tile(i), u_col)), full(h0), full(a_sm), full(wbr), full(wbi), full(cre),
                  full(ncim)],
        out_specs=(pl.BlockSpec((t, w), lambda i: (tile(i), 0)), hspec, hspec,
                   pl.BlockSpec((2, nsr, LANES), lambda i: (0, 0, 0))),
        scratch_shapes=[pltpu.VMEM((2, nsr, LANES), F32)],
        compiler_params=_cparams("arbitrary"))(u_arr, h0, a_sm, wbr, wbi, cre, ncim)


def _s5_bwd(dys, u_arr, u_col, w, hr, hi, hbound, g_in, a_sm, wbr_t, wbi_t, cre_t, ncim_t, rev, name):
    l = u_arr.shape[0]
    nb = w // LANES
    spb = wbr_t.shape[-2]
    nsr = a_sm.shape[1]
    t = _tile(l, 128)
    n = l // t
    with_dy = dys is not None
    tile = (lambda i: i) if rev else (lambda i: n - 1 - i)

    def body(*refs):
        if with_dy:
            (dy_ref, u_ref, hr_ref, hi_ref, pr_ref, pi_ref, hb_ref, gin_ref, a_ref, wbrt_ref, wbit_ref, cret_ref, ncimt_ref,
             du_ref, dwbr_ref, dwbi_ref, dcre_ref, dncim_ref, da_ref, gout_ref, gr_ref, gi_ref, carry_ref) = refs
        else:
            (u_ref, hr_ref, hi_ref, pr_ref, pi_ref, hb_ref, gin_ref, a_ref, wbrt_ref, wbit_ref,
             du_ref, dwbr_ref, dwbi_ref, da_ref, gout_ref, gr_ref, gi_ref, carry_ref) = refs
        i = pl.program_id(0)

        @pl.when(i == 0)
        def _():
            carry_ref[...] = gin_ref[...]
            accs = (dwbr_ref, dwbi_ref, da_ref) + ((dcre_ref, dncim_ref) if with_dy else ())
            for r in accs:
                r[...] = jnp.zeros_like(r)

        if with_dy:
            dy = dy_ref[...].astype(MXU_DTYPE)
            for j in range(nb // 2):
                for g_ref, c_ref in ((gr_ref, cret_ref), (gi_ref, ncimt_ref)):
                    blk = [_mxu_dot(dy[:, k * LANES:(k + 1) * LANES], c_ref[k]) for k in (2 * j, 2 * j + 1)]
                    g_ref[:, 8 * j:8 * j + 8, :] = jnp.concatenate(blk, axis=1).reshape(t, 8, LANES)
        else:
            gr_ref[...] = jnp.zeros_like(gr_ref)
            gi_ref[...] = jnp.zeros_like(gi_ref)
        ar, ai = a_ref[0], a_ref[1]

        def one(row, gr, gi, pr, pi, dr, di):
            nr = gr_ref[row] + ar * gr + ai * gi
            ni = gi_ref[row] + ar * gi - ai * gr
            gr_ref[row] = nr
            gi_ref[row] = ni
            return nr, ni, dr + nr * pr + ni * pi, di + ni * pr - nr * pi

        def step(s, c):
            gr, gi, dr, di = c
            row = s if rev else t - 1 - s
            prow = row + 1 if rev else row - 1
            return one(row, gr, gi, hr_ref[prow], hi_ref[prow], dr, di)

        gr, gi, dr, di = lax.fori_loop(0, t - 1, step, (carry_ref[0], carry_ref[1], da_ref[0], da_ref[1]), unroll=4)
        first = i == n - 1
        pr = jnp.where(first, hb_ref[0], pr_ref[0])
        pi = jnp.where(first, hb_ref[1], pi_ref[0])
        gr, gi, dr, di = one(t - 1 if rev else 0, gr, gi, pr, pi, dr, di)
        da_ref[0] = dr
        da_ref[1] = di
        for q, val in enumerate((gr, gi)):
            carry_ref[q] = val
            gout_ref[q] = val

        u = u_ref[...].astype(MXU_DTYPE)
        for j in range(nb // 2):
            sl = slice(8 * j, 8 * j + 8)
            g_r = gr_ref[:, sl, :].reshape(t, 8 * LANES).astype(MXU_DTYPE)
            g_i = gi_ref[:, sl, :].reshape(t, 8 * LANES).astype(MXU_DTYPE)
            if with_dy:
                h_r = hr_ref[:, sl, :].reshape(t, 8 * LANES).astype(MXU_DTYPE)
                h_i = hi_ref[:, sl, :].reshape(t, 8 * LANES).astype(MXU_DTYPE)
            for kk in range(2):
                k = 2 * j + kk
                cols = slice(kk * spb, (kk + 1) * spb)
                lanes = slice(k * LANES, (k + 1) * LANES)
                du_ref[:, lanes] = _mxu_dot(g_r[:, cols], wbrt_ref[k]) + _mxu_dot(g_i[:, cols], wbit_ref[k])
                dwbr_ref[k] += _mxu_dot(u[:, lanes], g_r[:, cols], _TN)
                dwbi_ref[k] += _mxu_dot(u[:, lanes], g_i[:, cols], _TN)
                if with_dy:
                    dcre_ref[k] += _mxu_dot(h_r[:, cols], dy[:, lanes], _TN)
                    dncim_ref[k] += _mxu_dot(h_i[:, cols], dy[:, lanes], _TN)

    full = lambda a: pl.BlockSpec(a.shape, lambda i: (0,) * a.ndim)
    row = lambda cb: pl.BlockSpec((t, w), lambda i: (tile(i), cb))
    hspec = pl.BlockSpec((t, nsr, LANES), lambda i: (tile(i), 0, 0))
    if rev:
        pspec = pl.BlockSpec((1, nsr, LANES), lambda i: (jnp.minimum((tile(i) + 1) * t, l - 1), 0, 0))
    else:
        pspec = pl.BlockSpec((1, nsr, LANES), lambda i: (jnp.maximum(tile(i) * t - 1, 0), 0, 0))
    sm = jax.ShapeDtypeStruct((2, nsr, LANES), F32)
    smspec = pl.BlockSpec((2, nsr, LANES), lambda i: (0, 0, 0))
    wsh = jax.ShapeDtypeStruct((nb, LANES, spb), F32)
    csh = jax.ShapeDtypeStruct((nb, spb, LANES), F32)
    in_specs = ([row(0)] if with_dy else []) + [row(u_col), hspec, hspec, pspec, pspec, full(hbound), full(g_in), full(a_sm),
                                                 full(wbr_t), full(wbi_t)] + ([full(cre_t), full(ncim_t)] if with_dy else [])
    args = ([dys] if with_dy else []) + [u_arr, hr, hi, hr, hi, hbound, g_in, a_sm, wbr_t, wbi_t] + ([cre_t, ncim_t] if with_dy else [])
    out_shape = (jax.ShapeDtypeStruct((l, w), F32), wsh, wsh) + ((csh, csh) if with_dy else ()) + (sm, sm)
    out_specs = (row(0), full(wsh), full(wsh)) + ((full(csh), full(csh)) if with_dy else ()) + (smspec, smspec)
    return pl.pallas_call(
        body, name=name, out_shape=out_shape, grid=(n,), in_specs=in_specs, out_specs=out_specs,
        scratch_shapes=[pltpu.VMEM((t, nsr, LANES), F32)] * 2 + [pltpu.VMEM((2, nsr, LANES), F32)],
        compiler_params=_cparams("arbitrary"))(*args)


def _glu_fwd(y0, y1, proj, cat, d_skip, w_glu, b_glu, w):
    l = y0.shape[0]
    tl = _tile(l, 256)

    def body(y0_ref, y1_ref, u_ref, z_ref, cat_in, d_ref, wg_ref, bg_ref, ys_ref, cat_ref):
        del cat_in
        ys = y0_ref[...] + y1_ref[...] + d_ref[...] * u_ref[...]
        ys_ref[...] = ys
        gy = _gelu(ys)
        s = _mxu_dot(gy, wg_ref[...]) + bg_ref[...]
        cat_ref[...] = (gy * jax.nn.sigmoid(s) * jax.nn.silu(z_ref[...])).astype(cat_ref.dtype)

    row = pl.BlockSpec((tl, w), lambda i: (i, 0))
    vec = pl.BlockSpec((1, w), lambda i: (0, 0))
    return pl.pallas_call(
        body, name="glu_fwd", out_shape=(jax.ShapeDtypeStruct((l, w), F32), jax.ShapeDtypeStruct(cat.shape, cat.dtype)),
        grid=(l // tl,),
        in_specs=[row, row, pl.BlockSpec((tl, w), lambda i: (i, 3)), pl.BlockSpec((tl, w), lambda i: (i, 4)), HBM,
                  vec, pl.BlockSpec((w, w), lambda i: (0, 0)), vec],
        out_specs=(row, pl.BlockSpec((tl, w), lambda i: (i, 1))), input_output_aliases={4: 1},
        compiler_params=_cparams("parallel"))(y0, y1, proj, proj, cat, d_skip, w_glu, b_glu)


def _glu_bwd(dcat, ys, proj, d_skip, w_glu, b_glu, w):
    l = ys.shape[0]
    tl = _tile(l, 256)

    def body(dy_ref, ys_ref, u_ref, z_ref, d_ref, wg_ref, bg_ref, dys_ref, dp_ref, dbg_ref, dd_ref, dwg_ref):
        i = pl.program_id(0)
        ys_t = ys_ref[...]
        z = z_ref[...]
        dyb = dy_ref[...].astype(F32)
        gy = _gelu(ys_t)
        sg = jax.nn.sigmoid(_mxu_dot(gy, wg_ref[...]) + bg_ref[...])
        dp_ref[...] = dyb * gy * sg * _silu_grad(z)
        dglu = dyb * jax.nn.silu(z)
        ds = dglu * gy * sg * (1.0 - sg)
        dgy = dglu * sg + _mxu_dot(ds, wg_ref[...], _NT)
        dys_t = dgy * _gelu_grad(ys_t)
        dys_ref[...] = dys_t

        @pl.when(i == 0)
        def _():
            for r in (dbg_ref, dd_ref, dwg_ref):
                r[...] = jnp.zeros_like(r)

        dbg_ref[...] += jnp.sum(ds, axis=0, keepdims=True)
        dd_ref[...] += jnp.sum(dys_t * u_ref[...], axis=0, keepdims=True)
        dwg_ref[...] += _mxu_dot(gy, ds, _TN)

    row = pl.BlockSpec((tl, w), lambda i: (i, 0))
    vec = pl.BlockSpec((1, w), lambda i: (0, 0))
    mat = pl.BlockSpec((w, w), lambda i: (0, 0))
    v = jax.ShapeDtypeStruct((1, w), F32)
    return pl.pallas_call(
        body, name="glu_bwd",
        out_shape=(jax.ShapeDtypeStruct((l, w), F32), jax.ShapeDtypeStruct((l, 5 * w), F32), v, v,
                   jax.ShapeDtypeStruct((w, w), F32)),
        grid=(l // tl,),
        in_specs=[pl.BlockSpec((tl, w), lambda i: (i, 1)), row, pl.BlockSpec((tl, w), lambda i: (i, 3)),
                  pl.BlockSpec((tl, w), lambda i: (i, 4)), vec, mat, vec],
        out_specs=(row, pl.BlockSpec((tl, w), lambda i: (i, 4)), vec, vec, mat),
        compiler_params=_cparams("arbitrary"))(dcat, ys, proj, proj, d_skip, w_glu, b_glu)


def _dub_combine(dys, du0, du1, d_skip, dproj, w):
    l = dys.shape[0]
    tl = _tile(l, 512)

    def body(dys_ref, a_ref, b_ref, d_ref, dp_in, dp_ref):
        del dp_in
        dp_ref[...] = dys_ref[...] * d_ref[...] + a_ref[...] + b_ref[...]

    row = pl.BlockSpec((tl, w), lambda i: (i, 0))
    return pl.pallas_call(
        body, name="dub_combine", out_shape=jax.ShapeDtypeStruct(dproj.shape, F32), grid=(l // tl,),
        in_specs=[row, row, row, pl.BlockSpec((1, w), lambda i: (0, 0)), HBM],
        out_specs=pl.BlockSpec((tl, w), lambda i: (i, 3)), input_output_aliases={4: 0},
        compiler_params=_cparams("parallel"))(dys, du0, du1, d_skip, dproj)


def _add2(a, b, name):
    l, w = a.shape
    tl = _tile(l, 512)

    def body(a_ref, b_ref, o_ref):
        o_ref[...] = a_ref[...] + b_ref[...]

    row = pl.BlockSpec((tl, w), lambda i: (i, 0))
    return pl.pallas_call(body, name=name, out_shape=jax.ShapeDtypeStruct((l, w), F32), grid=(l // tl,),
                          in_specs=[row, row], out_specs=row, compiler_params=_cparams("parallel"))(a, b)


def _adamw(w, m, v, gparts, name):
    r, c = w.shape
    np_ = gparts.shape[0]
    tr = _tile(r, max(8, (1 << 18) // c), 8)

    def body(w_ref, m_ref, v_ref, g_ref, go_ref, d_ref, mo_ref, vo_ref):
        g = g_ref[0]
        for p in range(1, np_):
            g = g + g_ref[p]
        mn = ADAM_B1 * m_ref[...] + (1.0 - ADAM_B1) * g
        vn = ADAM_B2 * v_ref[...] + (1.0 - ADAM_B2) * (g * g)
        m_hat = mn / (1.0 - ADAM_B1 ** ADAM_STEP)
        v_hat = vn / (1.0 - ADAM_B2 ** ADAM_STEP)
        go_ref[...] = g
        d_ref[...] = -ADAM_LR * (m_hat / (jnp.sqrt(v_hat) + ADAM_EPS) + ADAM_WD * w_ref[...])
        mo_ref[...] = mn
        vo_ref[...] = vn

    row = pl.BlockSpec((tr, c), lambda i: (i, 0))
    s = jax.ShapeDtypeStruct((r, c), F32)
    return pl.pallas_call(body, name=name, out_shape=(s, s, s, s), grid=(r // tr,),
                          in_specs=[row, row, row, pl.BlockSpec((np_, tr, c), lambda i: (0, i, 0))],
                          out_specs=(row, row, row, row), compiler_params=_cparams("parallel"))(w, m, v, gparts)


def _sum_slots(parts, name):
    np_, r, c = parts.shape

    def body(p_ref, o_ref):
        g = p_ref[0]
        for p in range(1, np_):
            g = g + p_ref[p]
        o_ref[...] = g

    return pl.pallas_call(body, name=name, out_shape=jax.ShapeDtypeStruct((r, c), F32), compiler_params=_cparams())(parts)


def _block_diag(x, gb):
    nd, g, a, b = x.shape
    eye = jnp.eye(gb, dtype=x.dtype)
    y = jnp.einsum("dkgab,gh->dkgahb", x.reshape(nd, g // gb, gb, a, b), eye)
    return y.reshape(nd, g // gb, gb * a, gb * b)


def _block_diag_extract(y, gb, a, b):
    nd, nbk = y.shape[:2]
    eye = jnp.eye(gb, dtype=y.dtype)
    x = jnp.einsum("dkgahb,gh->dkgab", y.reshape(nd, nbk, gb, a, gb, b), eye)
    return x.reshape(nd, nbk * gb, a, b)


def kernel(x, c, ctx, c_ctx, w_ada, b_ada, w_in, sgu_ln_g, sgu_ln_b, w_spatial, b_spatial, s5_lam_re, s5_lam_im, s5_log_step, s5_b_re, s5_b_im, s5_c_re, s5_c_im, s5_d, w_glu, b_glu, w_out, ln_g, ln_b, loss_target, m_c_ctx, m_w_ada, m_b_ada, m_w_in, m_sgu_ln_g, m_sgu_ln_b, m_w_spatial, m_b_spatial, m_s5_lam_re, m_s5_lam_im, m_s5_log_step, m_s5_b_re, m_s5_b_im, m_s5_c_re, m_s5_c_im, m_s5_d, m_w_glu, m_b_glu, m_w_out, m_ln_g, m_ln_b, v_c_ctx, v_w_ada, v_b_ada, v_w_in, v_sgu_ln_g, v_sgu_ln_b, v_w_spatial, v_b_spatial, v_s5_lam_re, v_s5_lam_im, v_s5_log_step, v_s5_b_re, v_s5_b_im, v_s5_c_re, v_s5_c_im, v_s5_d, v_w_glu, v_b_glu, v_w_out, v_ln_g, v_ln_b):
    small_names = ["c_ctx", "b_ada", "sgu_ln_g", "sgu_ln_b", "w_spatial", "b_spatial", "s5_lam_re", "s5_lam_im",
                   "s5_log_step", "s5_b_re", "s5_b_im", "s5_c_re", "s5_c_im", "s5_d", "b_glu", "ln_g", "ln_b"]
    env = dict(locals())
    x2, tgt, ctx2 = x[0], loss_target[0], ctx[0]
    l, d = x2.shape
    lc = ctx2.shape[0]
    w = d // 2
    nh = w // HEAD_DIM_A
    nd, g_s5, p_s5, c_s5 = s5_b_re.shape[1:]
    ns = g_s5 * p_s5
    nsr = ns // LANES
    gb = LANES // c_s5
    me = _index(_mesh_pos())
    ada_cols = w_ada.shape[2]

    srows = _silu_rows(c, c_ctx)
    srows_all = _all_gather(srows, 0, "gather_silu")
    s_mat = jnp.concatenate([srows_all[0::8], srows_all[1:2], jnp.zeros((7, d), F32)], axis=0)
    mod_part = _small_dot(s_mat, w_ada[0], "nn", "mod_cols")
    mod_all = _all_gather(mod_part, 1, "gather_mod") + b_ada
    mod_x = lax.dynamic_slice_in_dim(mod_all, me, 1, axis=0)
    mod_c = mod_all[8:9]
    shift_x, scale_x, gate_x = mod_x[:, :d], mod_x[:, d:2 * d], mod_x[:, 2 * d:]
    shift_c, scale_c = mod_c[:, :d], mod_c[:, d:2 * d]

    w_in_f = _all_gather(w_in[0].astype(MXU_DTYPE), 1, "gather_w_in")
    w_out_f = _all_gather(w_out[0].astype(MXU_DTYPE), 0, "gather_w_out")
    w_glu_f = _all_gather(w_glu[0].astype(MXU_DTYPE), 0, "gather_w_glu")

    lr, li = s5_lam_re[0], s5_lam_im[0]
    ls = s5_log_step[0][..., None]
    br_t = jnp.transpose(s5_b_re[0], (0, 3, 1, 2))
    bi_t = jnp.transpose(s5_b_im[0], (0, 3, 1, 2))
    ab_re, ab_im, bb_re, bb_im = _disc_fwd(lr, li, ls, br_t, bi_t)
    a_sm = jnp.stack([ab_re, ab_im], axis=1).reshape(nd, 2, nsr, LANES)
    wbr = _block_diag(jnp.transpose(bb_re, (0, 2, 1, 3)), gb).astype(MXU_DTYPE)
    wbi = _block_diag(jnp.transpose(bb_im, (0, 2, 1, 3)), gb).astype(MXU_DTYPE)
    cre_t = _block_diag(s5_c_re[0], gb).astype(MXU_DTYPE)
    ncim_t = _block_diag(-s5_c_im[0], gb).astype(MXU_DTYPE)
    cre, ncim = jnp.swapaxes(cre_t, 2, 3), jnp.swapaxes(ncim_t, 2, 3)
    wbr_t, wbi_t = jnp.swapaxes(wbr, 2, 3), jnp.swapaxes(wbi, 2, 3)
    d_skip = s5_d

    xm = _ln_mod(x2, shift_x, scale_x, "ln_mod_x")
    cm = _ln_mod(ctx2, shift_c, scale_c, "ln_mod_ctx")
    proj = _matmul(xm, w_in_f, mode="nn", name="proj", bm=512, bn=1024, bk=d)
    ub_c = _matmul(cm, w_in_f, mode="nn", name="proj_ctx", bm=256, bn=w, bk=d, b_n0=3 * w, n=w)
    bsf = jnp.repeat(b_spatial[0].T, HEAD_DIM_A, axis=1)
    ws = w_spatial[0]
    cat = _ga_fwd(proj, sgu_ln_g, sgu_ln_b, ws, bsf, w)
    zeros_state = jnp.zeros((2, nsr, LANES), F32)
    s5c, s5l = [], []
    for dr in range(nd):
        rev = dr == 1
        s5c.append(_s5_fwd(ub_c, 0, w, zeros_state, a_sm[dr], wbr[dr], wbi[dr], cre[dr], ncim[dr], rev, f"s5_fwd_ctx{dr}"))
        s5l.append(_s5_fwd(proj, 3, w, s5c[dr][3], a_sm[dr], wbr[dr], wbi[dr], cre[dr], ncim[dr], rev, f"s5_fwd{dr}"))
    ys, cat = _glu_fwd(s5l[0][0], s5l[1][0], proj, cat, d_skip, w_glu_f, b_glu, w)
    out = _matmul(cat, w_out_f, mode="nn", name="out_proj", bm=512, bn=1024, bk=2 * w)
    loss_row, dout, dx_res, dgate, dln_g, dln_b = _post_ln_loss(x2, out, gate_x, ln_g, ln_b, tgt)

    dcat = _matmul(dout, w_out_f, mode="nt", name="d_cat", bm=512, bn=1024, bk=d, out_dtype=MXU_DTYPE)
    dw_out = _matmul(cat, dout, mode="tn", name="d_w_out", bm=1024, bn=1024, bk=512)
    dys, dproj, db_glu, dd_skip, dw_glu = _glu_bwd(dcat, ys, proj, d_skip, w_glu_f, b_glu, w)
    du_l, du_c, dwbr, dwbi, dcre, dncim, da_sm = [], [], [], [], [], [], []
    nbk, spb = w // LANES, gb * p_s5
    for dr in range(nd):
        rev = dr == 1
        bl = _s5_bwd(dys, proj, 3, w, s5l[dr][1], s5l[dr][2], s5c[dr][3], zeros_state, a_sm[dr], wbr_t[dr], wbi_t[dr],
                     cre_t[dr], ncim_t[dr], rev, f"s5_bwd{dr}")
        bc = _s5_bwd(None, ub_c, 0, w, s5c[dr][1], s5c[dr][2], zeros_state, bl[6], a_sm[dr], wbr_t[dr], wbi_t[dr],
                     None, None, rev, f"s5_bwd_ctx{dr}")
        du_l.append(bl[0])
        du_c.append(bc[0])
        dwbr.append(_add2(bl[1].reshape(nbk * LANES, spb), bc[1].reshape(nbk * LANES, spb), f"sum_dwbr{dr}"))
        dwbi.append(_add2(bl[2].reshape(nbk * LANES, spb), bc[2].reshape(nbk * LANES, spb), f"sum_dwbi{dr}"))
        dcre.append(bl[3])
        dncim.append(bl[4])
        da_sm.append(_add2(bl[5].reshape(2 * nsr, LANES), bc[3].reshape(2 * nsr, LANES), f"sum_da{dr}"))
    dproj = _dub_combine(dys, du_l[0], du_l[1], d_skip, dproj, w)
    dub_c = _add2(du_c[0], du_c[1], "dub_ctx")
    dwbr = jnp.stack(dwbr).reshape(nd, nbk, LANES, spb)
    dwbi = jnp.stack(dwbi).reshape(nd, nbk, LANES, spb)
    dcre, dncim = jnp.stack(dcre), jnp.stack(dncim)
    da_sm = jnp.stack(da_sm).reshape(nd, 2, g_s5, p_s5)
    dproj, dsg, dsb, dws, dbsf = _ga_bwd(proj, dcat, dproj, sgu_ln_g, sgu_ln_b, ws, bsf, w)
    dw_in = _matmul(xm, dproj, mode="tn", name="d_w_in", bm=1024, bn=1024, bk=512)
    dw_in = _matmul(cm, dub_c, mode="tn", name="d_w_in_ctx", bm=1024, bn=w, bk=lc, acc_in=dw_in, acc_n0=3 * w)
    dxm = _matmul(dproj, w_in_f, mode="nt", name="d_xm", bm=512, bn=1024, bk=1024)
    dcm = _matmul(dub_c, w_in_f, mode="nt", name="d_cm", bm=256, bn=1024, bk=w, b_k0=3 * w, k=w)
    grad_x, dshift_x, dscale_x = _ln_mod_bwd(x2, dxm, scale_x, dx_res, "ln_mod_x_bwd")
    _, dshift_c, dscale_c = _ln_mod_bwd(ctx2, dcm, scale_c, None, "ln_mod_ctx_bwd")

    dbb_re = jnp.transpose(_block_diag_extract(dwbr, gb, c_s5, p_s5), (0, 2, 1, 3))
    dbb_im = jnp.transpose(_block_diag_extract(dwbi, gb, c_s5, p_s5), (0, 2, 1, 3))
    dc_re = jnp.swapaxes(_block_diag_extract(dcre, gb, p_s5, c_s5), 2, 3)
    dc_im = -jnp.swapaxes(_block_diag_extract(dncim, gb, p_s5, c_s5), 2, 3)
    dlr, dli, dls, dbr_t, dbi_t = _disc_bwd(lr, li, ls, br_t, bi_t, da_sm[:, 0], da_sm[:, 1], dbb_re, dbb_im)
    db_re = jnp.transpose(dbr_t, (0, 2, 3, 1))
    db_im = jnp.transpose(dbi_t, (0, 2, 3, 1))
    expand = (jnp.arange(w)[:, None] // HEAD_DIM_A == jnp.arange(LANES)[None, :]).astype(F32)
    db_sp = _lane_group_sum(dbsf, expand, "d_b_spatial")[:, :nh].T

    dmod_rows = jnp.concatenate([jnp.concatenate([dshift_x, dscale_x, dgate], axis=1),
                                 jnp.concatenate([dshift_c, dscale_c, jnp.zeros((1, d), F32)], axis=1),
                                 jnp.zeros((6, 3 * d), F32)], axis=0)
    dmod_all = _all_gather(dmod_rows, 0, "gather_dmod")
    dmod_ctx = _sum_slots(dmod_all[1::8].reshape(N_DEV, 1, 3 * d), "sum_dmod_ctx")
    dmod_mat = jnp.concatenate([dmod_all[0::8], dmod_ctx, jnp.zeros((7, 3 * d), F32)], axis=0)
    db_ada = _sum_slots(dmod_mat[:9].reshape(9, 1, 3 * d), "sum_db_ada")
    dmod_mine = lax.dynamic_slice_in_dim(dmod_mat, me * ada_cols, ada_cols, axis=1)
    dw_ada = _small_dot(s_mat, dmod_mine, "tn", "d_w_ada")
    dsilu_cc = _small_dot(dmod_mine[8:16], w_ada[0], "nt", "d_silu_cctx")[0:1]
    dc_ctx_part = dsilu_cc * _silu_grad(c_ctx.reshape(1, d))

    gp_w_in = _all_to_all(dw_in, 1, "a2a_d_w_in")
    gp_w_out = _all_to_all(dw_out, 0, "a2a_d_w_out")
    gp_w_glu = _all_to_all(dw_glu, 0, "a2a_d_w_glu")
    big = {
        "w_ada": _adamw(w_ada[0], m_w_ada[0], v_w_ada[0], dw_ada[None], "adamw_w_ada"),
        "w_in": _adamw(w_in[0], m_w_in[0], v_w_in[0], gp_w_in, "adamw_w_in"),
        "w_glu": _adamw(w_glu[0], m_w_glu[0], v_w_glu[0], gp_w_glu, "adamw_w_glu"),
        "w_out": _adamw(w_out[0], m_w_out[0], v_w_out[0], gp_w_out, "adamw_w_out"),
    }

    local = {"c_ctx": dc_ctx_part, "sgu_ln_g": dsg, "sgu_ln_b": dsb, "w_spatial": dws, "b_spatial": db_sp,
             "s5_lam_re": dlr, "s5_lam_im": dli, "s5_log_step": dls, "s5_b_re": db_re, "s5_b_im": db_im,
             "s5_c_re": dc_re, "s5_c_im": dc_im, "s5_d": dd_skip, "b_glu": db_glu, "ln_g": dln_g, "ln_b": dln_b}
    reduced = [n for n in small_names if n != "b_ada"]
    loss_part = (0.5 / d) * jnp.sum(loss_row)
    flat = jnp.concatenate([loss_part.reshape(1)] + [local[n].reshape(-1) for n in reduced])
    unit = N_DEV * 8 * LANES
    total = -(-flat.shape[0] // unit) * unit
    flat = jnp.pad(flat, (0, total - flat.shape[0])).reshape(N_DEV * 8, total // (N_DEV * 8))
    slots = _all_to_all(flat, 0, "a2a_small")
    mine = _sum_slots(slots, "sum_small")
    summed = _all_gather(mine, 0, "gather_small").reshape(-1)
    loss = summed[0]
    grads, off = {"b_ada": db_ada}, 1
    for n in reduced:
        size = math.prod(env[n].shape)
        grads[n] = summed[off:off + size]
        off += size

    pack_w = 64 * LANES

    def packed(vals):
        v = jnp.concatenate([a.reshape(-1) for a in vals])
        pad = -(-v.shape[0] // (8 * pack_w)) * (8 * pack_w) - v.shape[0]
        return jnp.pad(v, (0, pad), constant_values=1.0).reshape(-1, pack_w)

    sg, sd, sm_, sv = [a.reshape(-1) for a in _adamw(
        packed([env[n] for n in small_names]), packed([env["m_" + n] for n in small_names]),
        packed([env["v_" + n] for n in small_names]), packed([grads[n] for n in small_names])[None], "adamw_small")]
    res = {n: tuple(a[None] for a in big[n]) for n in big}
    off = 0
    for n in small_names:
        size = math.prod(env[n].shape)
        res[n] = tuple(a[off:off + size].reshape(env[n].shape) for a in (sg, sd, sm_, sv))
        off += size

    order = ["c_ctx", "w_ada", "b_ada", "w_in", "sgu_ln_g", "sgu_ln_b", "w_spatial", "b_spatial", "s5_lam_re", "s5_lam_im",
             "s5_log_step", "s5_b_re", "s5_b_im", "s5_c_re", "s5_c_im", "s5_d", "w_glu", "b_glu", "w_out", "ln_g", "ln_b"]
    return (loss, grad_x[None], *[res[n][0] for n in order], *[res[n][1] for n in order],
            *[res[n][2] for n in order], *[res[n][3] for n in order])
```

```python
import functools
import math

import jax
import jax.numpy as jnp
from jax import lax
from jax.experimental import pallas as pl
from jax.experimental.pallas import tpu as pltpu

F32 = jnp.float32
MXU_DTYPE = jnp.bfloat16
N_DEV = 8
MESH_ID = pl.DeviceIdType.MESH
LN_EPS = 1e-6
DEPTH = 1
ALPHA = (2.0 * DEPTH) ** 0.25
CHUNK = 128
HEAD_DIM_A = 128
ADAM_LR, ADAM_B1, ADAM_B2, ADAM_EPS, ADAM_WD, ADAM_STEP = 0.001, 0.9, 0.999, 1e-08, 0.01, 10
LANES = 128
VMEM_LIMIT = 56 * 1024 * 1024
HBM = pl.BlockSpec(memory_space=pl.ANY)


def _cparams(*sem):
    return pltpu.CompilerParams(dimension_semantics=sem if sem else None, vmem_limit_bytes=VMEM_LIMIT)


def _tile(n, pref, mult=1):
    if n <= pref:
        return n
    t = pref - pref % mult
    while n % t:
        t -= mult
    return t


def _gelu(x):
    return 0.5 * x * (1.0 + lax.erf(x * (1.0 / math.sqrt(2.0))))


def _gelu_grad(x):
    return 0.5 * (1.0 + lax.erf(x * (1.0 / math.sqrt(2.0)))) + x * jnp.exp(-0.5 * x * x) * (1.0 / math.sqrt(2.0 * math.pi))


def _silu_grad(x):
    s = jax.nn.sigmoid(x)
    return s * (1.0 + x * (1.0 - s))


def _mxu_dot(a, b, dims=(((1,), (0,)), ((), ()))):
    return lax.dot_general(a.astype(MXU_DTYPE), b.astype(MXU_DTYPE), dims, preferred_element_type=F32)


_NT = (((1,), (1,)), ((), ()))
_TN = (((0,), (0,)), ((), ()))


def _mesh_pos():
    return lax.axis_index("x"), lax.axis_index("y"), lax.axis_index("c")


def _peer(pos, r):
    x, y, c = pos
    return ((1 - x) if r & 4 else x, (1 - y) if r & 2 else y, (1 - c) if r & 1 else c)


def _index(pos):
    return 4 * pos[0] + 2 * pos[1] + pos[2]


def _slice_of(ref, axis, idx, size):
    start = idx * size
    if axis == 0:
        return ref.at[pl.ds(start, size)]
    return ref.at[:, pl.ds(start, size)]


def _all_gather(x, axis, name):
    size = x.shape[axis]
    out_shape = tuple(s * N_DEV if a == axis else s for a, s in enumerate(x.shape))

    def body(x_ref, o_ref, send_sems, recv_sems, local_sem):
        me = _mesh_pos()
        mine = pltpu.make_async_copy(x_ref, _slice_of(o_ref, axis, _index(me), size), local_sem)
        mine.start()

        def copy(r, block):
            return pltpu.make_async_remote_copy(
                src_ref=x_ref, dst_ref=_slice_of(o_ref, axis, _index(block), size),
                send_sem=send_sems.at[r - 1], recv_sem=recv_sems.at[r - 1],
                device_id=_peer(me, r), device_id_type=MESH_ID)

        sends = [copy(r, me) for r in range(1, N_DEV)]
        for cp in sends:
            cp.start()
        for r in range(1, N_DEV):
            copy(r, _peer(me, r)).wait_recv()
        for cp in sends:
            cp.wait_send()
        mine.wait()

    return pl.pallas_call(
        body, name=name, out_shape=jax.ShapeDtypeStruct(out_shape, x.dtype),
        in_specs=[HBM], out_specs=HBM,
        scratch_shapes=[pltpu.SemaphoreType.DMA((N_DEV - 1,)), pltpu.SemaphoreType.DMA((N_DEV - 1,)),
                        pltpu.SemaphoreType.DMA],
    )(x)


def _all_to_all(x, axis, name):
    size = x.shape[axis] // N_DEV
    slot = tuple(size if a == axis else s for a, s in enumerate(x.shape))

    def body(x_ref, o_ref, send_sems, recv_sems, local_sem):
        me = _mesh_pos()
        mine = pltpu.make_async_copy(_slice_of(x_ref, axis, _index(me), size), o_ref.at[_index(me)], local_sem)
        mine.start()

        def copy(r, sender, receiver):
            return pltpu.make_async_remote_copy(
                src_ref=_slice_of(x_ref, axis, _index(receiver), size), dst_ref=o_ref.at[_index(sender)],
                send_sem=send_sems.at[r - 1], recv_sem=recv_sems.at[r - 1],
                device_id=_peer(me, r), device_id_type=MESH_ID)

        sends = [copy(r, me, _peer(me, r)) for r in range(1, N_DEV)]
        for cp in sends:
            cp.start()
        for r in range(1, N_DEV):
            copy(r, _peer(me, r), me).wait_recv()
        for cp in sends:
            cp.wait_send()
        mine.wait()

    return pl.pallas_call(
        body, name=name, out_shape=jax.ShapeDtypeStruct((N_DEV,) + slot, x.dtype),
        in_specs=[HBM], out_specs=HBM,
        scratch_shapes=[pltpu.SemaphoreType.DMA((N_DEV - 1,)), pltpu.SemaphoreType.DMA((N_DEV - 1,)),
                        pltpu.SemaphoreType.DMA],
    )(x)


def _matmul(a, b, *, mode, name, bm=512, bn=512, bk=512, out_dtype=F32, b_n0=0, n=None, b_k0=0, k=None,
            acc_in=None, acc_n0=0):
    if mode == "tn":
        kk, m = a.shape
    else:
        m, kk = a.shape
    if mode == "nn":
        n = b.shape[1] if n is None else n
    elif mode == "nt":
        n = b.shape[0]
        kk = kk if k is None else k
    else:
        n = b.shape[1]
    bm, bn, bk = _tile(m, bm), _tile(n, bn), _tile(kk, bk)
    nk = kk // bk
    assert b_n0 % bn == 0 and b_k0 % bk == 0 and acc_n0 % bn == 0
    dims = {"nn": (((1,), (0,)), ((), ())), "nt": _NT, "tn": _TN}[mode]

    def body(*refs):
        if acc_in is None:
            a_ref, b_ref, o_ref = refs[:3]
            init = None
        else:
            a_ref, b_ref, init, o_ref = refs[:4]
        acc_ref = refs[-1] if nk > 1 else None
        p = _mxu_dot(a_ref[...], b_ref[...], dims)
        if nk == 1:
            o_ref[...] = (p if init is None else p + init[...]).astype(out_dtype)
            return
        ki = pl.program_id(2)

        @pl.when(ki == 0)
        def _():
            acc_ref[...] = p if init is None else p + init[...]

        @pl.when(ki > 0)
        def _():
            acc_ref[...] += p

        @pl.when(ki == nk - 1)
        def _():
            o_ref[...] = acc_ref[...].astype(out_dtype)

    a_spec = pl.BlockSpec((bk, bm), lambda j, i, q: (q, i)) if mode == "tn" else pl.BlockSpec((bm, bk), lambda j, i, q: (i, q))
    if mode == "nt":
        b_spec = pl.BlockSpec((bn, bk), lambda j, i, q: (j, q + b_k0 // bk))
    else:
        b_spec = pl.BlockSpec((bk, bn), lambda j, i, q: (q, j + b_n0 // bn))
    in_specs, args, aliases = [a_spec, b_spec], [a, b], {}
    out_map = lambda j, i, q: (i, j + acc_n0 // bn)
    if acc_in is not None:
        in_specs.append(pl.BlockSpec((bm, bn), out_map))
        args.append(acc_in)
        aliases = {2: 0}
        out_shape = jax.ShapeDtypeStruct(acc_in.shape, out_dtype)
    else:
        out_shape = jax.ShapeDtypeStruct((m, n), out_dtype)
    return pl.pallas_call(
        body, name=name, out_shape=out_shape, grid=(n // bn, m // bm, nk),
        in_specs=in_specs, out_specs=pl.BlockSpec((bm, bn), out_map),
        scratch_shapes=[pltpu.VMEM((bm, bn), F32)] if nk > 1 else [],
        input_output_aliases=aliases,
        compiler_params=_cparams("parallel", "parallel", "arbitrary"),
    )(*args)


def _silu_rows(c, c_ctx):
    d = c.shape[-1]

    def body(c_ref, cc_ref, o_ref):
        o_ref[...] = jnp.zeros_like(o_ref)
        o_ref[0:1, :] = jax.nn.silu(c_ref[...])
        o_ref[1:2, :] = jax.nn.silu(cc_ref[...])

    return pl.pallas_call(body, name="silu_rows", out_shape=jax.ShapeDtypeStruct((8, d), F32))(
        c.reshape(1, d), c_ctx.reshape(1, d))


def _small_dot(a, b, mode, name):
    dims = {"nn": (((1,), (0,)), ((), ())), "nt": _NT, "tn": _TN}[mode]
    m = a.shape[1] if mode == "tn" else a.shape[0]
    n = b.shape[0] if mode == "nt" else b.shape[1]

    def body(a_ref, b_ref, o_ref):
        o_ref[...] = lax.dot_general(a_ref[...], b_ref[...], dims, preferred_element_type=F32,
                                     precision=lax.Precision.HIGHEST)

    return pl.pallas_call(body, name=name, out_shape=jax.ShapeDtypeStruct((m, n), F32),
                          compiler_params=_cparams())(a, b)


def _ln_stats(x):
    mu = jnp.mean(x, axis=-1, keepdims=True)
    xc = x - mu
    var = jnp.mean(xc * xc, axis=-1, keepdims=True)
    rstd = lax.rsqrt(var + LN_EPS)
    return xc * rstd, rstd


def _ln_mod(x, shift, scale, name):
    l, d = x.shape
    tl = _tile(l, 256)

    def body(x_ref, sh_ref, sc_ref, o_ref):
        xhat, _ = _ln_stats(x_ref[...])
        o_ref[...] = (xhat * (1.0 + sc_ref[...]) + sh_ref[...]).astype(o_ref.dtype)

    row = pl.BlockSpec((tl, d), lambda i: (i, 0))
    vec = pl.BlockSpec((1, d), lambda i: (0, 0))
    return pl.pallas_call(body, name=name, out_shape=jax.ShapeDtypeStruct((l, d), MXU_DTYPE), grid=(l // tl,),
                          in_specs=[row, vec, vec], out_specs=row, compiler_params=_cparams("parallel"))(x, shift, scale)


def _ln_mod_bwd(x, dxm, scale, res, name):
    l, d = x.shape
    tl = _tile(l, 256)
    with_res = res is not None

    def body(*refs):
        if with_res:
            x_ref, g_ref, sc_ref, r_ref, dx_ref, dsh_ref, dsc_ref = refs
        else:
            x_ref, g_ref, sc_ref, dx_ref, dsh_ref, dsc_ref = refs
        i = pl.program_id(0)
        xhat, rstd = _ln_stats(x_ref[...])
        g = g_ref[...]
        dxh = g * (1.0 + sc_ref[...])
        dx = rstd * (dxh - jnp.mean(dxh, axis=-1, keepdims=True) - xhat * jnp.mean(dxh * xhat, axis=-1, keepdims=True))
        dx_ref[...] = dx + r_ref[...] if with_res else dx

        @pl.when(i == 0)
        def _():
            dsh_ref[...] = jnp.zeros_like(dsh_ref)
            dsc_ref[...] = jnp.zeros_like(dsc_ref)

        dsh_ref[...] += jnp.sum(g, axis=0, keepdims=True)
        dsc_ref[...] += jnp.sum(g * xhat, axis=0, keepdims=True)

    row = pl.BlockSpec((tl, d), lambda i: (i, 0))
    vec = pl.BlockSpec((1, d), lambda i: (0, 0))
    args = [x, dxm, scale] + ([res] if with_res else [])
    return pl.pallas_call(
        body, name=name,
        out_shape=(jax.ShapeDtypeStruct((l, d), F32), jax.ShapeDtypeStruct((1, d), F32), jax.ShapeDtypeStruct((1, d), F32)),
        grid=(l // tl,), in_specs=[row, row, vec] + ([row] if with_res else []), out_specs=(row, vec, vec),
        compiler_params=_cparams("arbitrary"))(*args)


def _post_ln_loss(x, out, gate, ln_g, ln_b, target):
    l, d = x.shape
    tl = _tile(l, 256)

    def body(x_ref, o_ref, gate_ref, g_ref, b_ref, t_ref, loss_ref, dout_ref, dxr_ref, dgate_ref, dg_ref, db_ref):
        i = pl.program_id(0)
        out_t = o_ref[...]
        gate_v = gate_ref[...]
        rhat, rstd = _ln_stats(ALPHA * x_ref[...] + gate_v * out_t)
        ln_gv = g_ref[...]
        diff = rhat * ln_gv + b_ref[...] - t_ref[...]
        dy = diff * (1.0 / d)
        drh = dy * ln_gv
        dr = rstd * (drh - jnp.mean(drh, axis=-1, keepdims=True) - rhat * jnp.mean(drh * rhat, axis=-1, keepdims=True))
        dout_ref[...] = (gate_v * dr).astype(dout_ref.dtype)
        dxr_ref[...] = ALPHA * dr

        @pl.when(i == 0)
        def _():
            for r in (loss_ref, dgate_ref, dg_ref, db_ref):
                r[...] = jnp.zeros_like(r)

        loss_ref[...] += jnp.sum(diff * diff, axis=0, keepdims=True)
        dgate_ref[...] += jnp.sum(dr * out_t, axis=0, keepdims=True)
        dg_ref[...] += jnp.sum(dy * rhat, axis=0, keepdims=True)
        db_ref[...] += jnp.sum(dy, axis=0, keepdims=True)

    row = pl.BlockSpec((tl, d), lambda i: (i, 0))
    vec = pl.BlockSpec((1, d), lambda i: (0, 0))
    v = jax.ShapeDtypeStruct((1, d), F32)
    return pl.pallas_call(
        body, name="post_ln_loss",
        out_shape=(v, jax.ShapeDtypeStruct((l, d), MXU_DTYPE), jax.ShapeDtypeStruct((l, d), F32), v, v, v),
        grid=(l // tl,), in_specs=[row, row, vec, vec, vec, row], out_specs=(vec, row, row, vec, vec, vec),
        compiler_params=_cparams("arbitrary"))(x, out, gate, ln_g, ln_b, target)


def _ga_forward_tile(p, g, b, ws_ref, bsf, w, nc, nh):
    u_raw, v_raw, za = p[:, :w], p[:, w:2 * w], p[:, 2 * w:3 * w]
    gu = _gelu(u_raw)
    vhat, rstd = _ln_stats(_gelu(v_raw))
    vn = vhat * g + b
    rows = []
    for ci in range(nc):
        r0 = ci * CHUNK
        heads = [_mxu_dot(ws_ref[h], vn[r0:r0 + CHUNK, h * HEAD_DIM_A:(h + 1) * HEAD_DIM_A]) for h in range(nh)]
        rows.append(jnp.concatenate(heads, axis=1) + bsf)
    mixed = jnp.concatenate(rows, axis=0) if nc > 1 else rows[0]
    return u_raw, v_raw, za, gu, vhat, rstd, vn, mixed


def _ga_fwd(proj, g, b, ws, bsf, w):
    l = proj.shape[0]
    nh = w // HEAD_DIM_A
    nc = _tile(l // CHUNK, 2)
    tl = nc * CHUNK

    def body(p_ref, g_ref, b_ref, ws_ref, bsf_ref, o_ref):
        _, _, za, gu, _, _, _, mixed = _ga_forward_tile(p_ref[...], g_ref[...], b_ref[...], ws_ref, bsf_ref[...], w, nc, nh)
        o_ref[...] = (gu * mixed * jax.nn.silu(za)).astype(o_ref.dtype)

    vec = pl.BlockSpec((1, w), lambda i: (0, 0))
    return pl.pallas_call(
        body, name="ga_fwd", out_shape=jax.ShapeDtypeStruct((l, 2 * w), MXU_DTYPE), grid=(l // tl,),
        in_specs=[pl.BlockSpec((tl, 3 * w), lambda i: (i, 0)), vec, vec,
                  pl.BlockSpec((nh, CHUNK, CHUNK), lambda i: (0, 0, 0)), pl.BlockSpec((CHUNK, w), lambda i: (0, 0))],
        out_specs=pl.BlockSpec((tl, w), lambda i: (i, 0)), compiler_params=_cparams("parallel"))(proj, g, b, ws, bsf)


def _ga_bwd(proj, dcat, dproj, g, b, ws, bsf, w):
    l = proj.shape[0]
    nh = w // HEAD_DIM_A
    nc = _tile(l // CHUNK, 2)
    tl = nc * CHUNK

    def body(p_ref, dy_ref, dp_in, g_ref, b_ref, ws_ref, bsf_ref, dp_ref, dg_ref, db_ref, dws_ref, dbsf_ref):
        del dp_in
        i = pl.program_id(0)
        gv = g_ref[...]
        u_raw, v_raw, za, gu, vhat, rstd, vn, mixed = _ga_forward_tile(
            p_ref[...], gv, b_ref[...], ws_ref, bsf_ref[...], w, nc, nh)
        dya = dy_ref[...].astype(F32)
        sz = jax.nn.silu(za)
        dmixed = dya * gu * sz
        dza = dya * gu * mixed * _silu_grad(za)
        dgu = dya * mixed * sz

        @pl.when(i == 0)
        def _():
            for r in (dg_ref, db_ref, dws_ref, dbsf_ref):
                r[...] = jnp.zeros_like(r)

        rows = []
        for ci in range(nc):
            r0 = ci * CHUNK
            heads = []
            for h in range(nh):
                cols = slice(h * HEAD_DIM_A, (h + 1) * HEAD_DIM_A)
                dm = dmixed[r0:r0 + CHUNK, cols]
                heads.append(_mxu_dot(ws_ref[h], dm, _TN))
                dws_ref[h] += _mxu_dot(dm, vn[r0:r0 + CHUNK, cols], _NT)
            rows.append(jnp.concatenate(heads, axis=1))
            dbsf_ref[...] += dmixed[r0:r0 + CHUNK, :]
        dvn = jnp.concatenate(rows, axis=0) if nc > 1 else rows[0]
        dg_ref[...] += jnp.sum(dvn * vhat, axis=0, keepdims=True)
        db_ref[...] += jnp.sum(dvn, axis=0, keepdims=True)
        dvh = dvn * gv
        dgv = rstd * (dvh - jnp.mean(dvh, axis=-1, keepdims=True) - vhat * jnp.mean(dvh * vhat, axis=-1, keepdims=True))
        dp_ref[:, :w] = (dgu * _gelu_grad(u_raw)).astype(dp_ref.dtype)
        dp_ref[:, w:2 * w] = (dgv * _gelu_grad(v_raw)).astype(dp_ref.dtype)
        dp_ref[:, 2 * w:] = dza.astype(dp_ref.dtype)

    vec = pl.BlockSpec((1, w), lambda i: (0, 0))
    ws_spec = pl.BlockSpec((nh, CHUNK, CHUNK), lambda i: (0, 0, 0))
    bs_spec = pl.BlockSpec((CHUNK, w), lambda i: (0, 0))
    v = jax.ShapeDtypeStruct((1, w), F32)
    return pl.pallas_call(
        body, name="ga_bwd",
        out_shape=(jax.ShapeDtypeStruct(dproj.shape, dproj.dtype), v, v, jax.ShapeDtypeStruct((nh, CHUNK, CHUNK), F32),
                   jax.ShapeDtypeStruct((CHUNK, w), F32)),
        grid=(l // tl,),
        in_specs=[pl.BlockSpec((tl, 3 * w), lambda i: (i, 0)), pl.BlockSpec((tl, w), lambda i: (i, 0)), HBM,
                  vec, vec, ws_spec, bs_spec],
        out_specs=(pl.BlockSpec((tl, 3 * w), lambda i: (i, 0)), vec, vec, ws_spec, bs_spec),
        input_output_aliases={2: 0}, compiler_params=_cparams("arbitrary"))(proj, dcat, dproj, g, b, ws, bsf)


def _lane_group_sum(x, expand, name):
    return _small_dot(x, expand, "nn", name)


def _disc_math(lr, li, ls, br, bi):
    step = jnp.exp(ls)
    dr, di = lr * step, li * step
    mag = jnp.exp(dr)
    ab_re, ab_im = mag * jnp.cos(di), mag * jnp.sin(di)
    den = lr * lr + li * li
    nr, ni = ab_re - 1.0, ab_im
    f_re = (nr * lr + ni * li) / den
    f_im = (ni * lr - nr * li) / den
    bb_re = f_re[:, None] * br - f_im[:, None] * bi
    bb_im = f_re[:, None] * bi + f_im[:, None] * br
    return ab_re, ab_im, bb_re, bb_im


def _disc_fwd(lr, li, ls, br, bi):
    def body(lr_ref, li_ref, ls_ref, br_ref, bi_ref, o1, o2, o3, o4):
        res = _disc_math(lr_ref[...], li_ref[...], ls_ref[...], br_ref[...], bi_ref[...])
        for o, r in zip((o1, o2, o3, o4), res):
            o[...] = r

    s = lambda a: jax.ShapeDtypeStruct(a.shape, F32)
    return pl.pallas_call(body, name="s5_disc", out_shape=(s(lr), s(lr), s(br), s(br)), compiler_params=_cparams())(
        lr, li, ls, br, bi)


def _disc_bwd(lr, li, ls, br, bi, d_ar, d_ai, d_br, d_bi):
    def body(lr_ref, li_ref, ls_ref, br_ref, bi_ref, c1, c2, c3, c4, o1, o2, o3, o4, o5):
        _, vjp = jax.vjp(_disc_math, lr_ref[...], li_ref[...], ls_ref[...], br_ref[...], bi_ref[...])
        res = vjp((c1[...], c2[...], c3[...], c4[...]))
        for o, r in zip((o1, o2, o3, o4, o5), res):
            o[...] = r

    s = lambda a: jax.ShapeDtypeStruct(a.shape, F32)
    return pl.pallas_call(body, name="s5_disc_bwd", out_shape=(s(lr), s(lr), s(ls), s(br), s(br)),
                          compiler_params=_cparams())(lr, li, ls, br, bi, d_ar, d_ai, d_br, d_bi)


def _dir_spec(a, dr):
    return pl.BlockSpec((None,) + a.shape[1:], lambda i: (dr,) + (0,) * (a.ndim - 1))


def _s5_fwd(u_arr, u_col, w, h0, a_sm, wbr, wbi, cre, ncim, dr, name):
    rev = dr == 1
    l = u_arr.shape[0]
    nb = w // LANES
    spb = wbr.shape[-1]
    nsr = a_sm.shape[2]
    assert 2 * spb == 8 * LANES and nb % 2 == 0
    t = _tile(l, 256)
    n = l // t
    tile = (lambda i: n - 1 - i) if rev else (lambda i: i)

    def body(u_ref, h0_ref, a_ref, wbr_ref, wbi_ref, cre_ref, ncim_ref, y_ref, hr_ref, hi_ref, hfin_ref, carry_ref):
        i = pl.program_id(0)

        @pl.when(i == 0)
        def _():
            carry_ref[...] = h0_ref[...]

        u = u_ref[...].astype(MXU_DTYPE)
        for j in range(nb // 2):
            for h_ref, w_ref in ((hr_ref, wbr_ref), (hi_ref, wbi_ref)):
                blk = [_mxu_dot(u[:, k * LANES:(k + 1) * LANES], w_ref[k]) for k in (2 * j, 2 * j + 1)]
                h_ref[:, 8 * j:8 * j + 8, :] = jnp.concatenate(blk, axis=1).reshape(t, 8, LANES)
        ar, ai = a_ref[0], a_ref[1]

        def step(s, c):
            hr, hi = c
            row = t - 1 - s if rev else s
            nr = ar * hr - ai * hi + hr_ref[row]
            ni = ar * hi + ai * hr + hi_ref[row]
            hr_ref[row] = nr
            hi_ref[row] = ni
            return nr, ni

        c = lax.fori_loop(0, t, step, (carry_ref[0], carry_ref[1]), unroll=4)
        for q in range(2):
            carry_ref[q] = c[q]
            hfin_ref[q] = c[q]
        for j in range(nb // 2):
            hr = hr_ref[:, 8 * j:8 * j + 8, :].reshape(t, 8 * LANES)
            hi = hi_ref[:, 8 * j:8 * j + 8, :].reshape(t, 8 * LANES)
            for kk in range(2):
                k = 2 * j + kk
                cols = slice(kk * spb, (kk + 1) * spb)
                y_ref[:, k * LANES:(k + 1) * LANES] = (_mxu_dot(hr[:, cols], cre_ref[k]) + _mxu_dot(hi[:, cols], ncim_ref[k]))

    full = lambda a: pl.BlockSpec(a.shape, lambda i: (0,) * a.ndim)
    hspec = pl.BlockSpec((t, nsr, LANES), lambda i: (tile(i), 0, 0))
    hsh = jax.ShapeDtypeStruct((l, nsr, LANES), F32)
    return pl.pallas_call(
        body, name=name, out_shape=(jax.ShapeDtypeStruct((l, w), F32), hsh, hsh, jax.ShapeDtypeStruct((2, nsr, LANES), F32)),
        grid=(n,),
        in_specs=[pl.BlockSpec((t, w), lambda i: (tile(i), u_col)), full(h0)] + [_dir_spec(a, dr) for a in (a_sm, wbr, wbi, cre, ncim)],
        out_specs=(pl.BlockSpec((t, w), lambda i: (tile(i), 0)), hspec, hspec,
                   pl.BlockSpec((2, nsr, LANES), lambda i: (0, 0, 0))),
        scratch_shapes=[pltpu.VMEM((2, nsr, LANES), F32)],
        compiler_params=_cparams("arbitrary"))(u_arr, h0, a_sm, wbr, wbi, cre, ncim)


def _s5_bwd(dys, u_arr, u_col, w, hr, hi, hbound, g_in, a_sm, wbr_t, wbi_t, cre_t, ncim_t, dr, name):
    rev = dr == 1
    l = u_arr.shape[0]
    nb = w // LANES
    spb = wbr_t.shape[-2]
    nsr = a_sm.shape[2]
    t = _tile(l, 128)
    n = l // t
    with_dy = dys is not None
    tile = (lambda i: i) if rev else (lambda i: n - 1 - i)

    def body(*refs):
        if with_dy:
            (dy_ref, u_ref, hr_ref, hi_ref, pr_ref, pi_ref, hb_ref, gin_ref, a_ref, wbrt_ref, wbit_ref, cret_ref, ncimt_ref,
             du_ref, dwbr_ref, dwbi_ref, dcre_ref, dncim_ref, da_ref, gout_ref, gr_ref, gi_ref, carry_ref) = refs
        else:
            (u_ref, hr_ref, hi_ref, pr_ref, pi_ref, hb_ref, gin_ref, a_ref, wbrt_ref, wbit_ref,
             du_ref, dwbr_ref, dwbi_ref, da_ref, gout_ref, gr_ref, gi_ref, carry_ref) = refs
        i = pl.program_id(0)

        @pl.when(i == 0)
        def _():
            carry_ref[...] = gin_ref[...]
            accs = (dwbr_ref, dwbi_ref, da_ref) + ((dcre_ref, dncim_ref) if with_dy else ())
            for r in accs:
                r[...] = jnp.zeros_like(r)

        if with_dy:
            dy = dy_ref[...].astype(MXU_DTYPE)
            for j in range(nb // 2):
                for g_ref, c_ref in ((gr_ref, cret_ref), (gi_ref, ncimt_ref)):
                    blk = [_mxu_dot(dy[:, k * LANES:(k + 1) * LANES], c_ref[k]) for k in (2 * j, 2 * j + 1)]
                    g_ref[:, 8 * j:8 * j + 8, :] = jnp.concatenate(blk, axis=1).reshape(t, 8, LANES)
        else:
            gr_ref[...] = jnp.zeros_like(gr_ref)
            gi_ref[...] = jnp.zeros_like(gi_ref)
        ar, ai = a_ref[0], a_ref[1]

        def one(row, gr, gi, pr, pi, dr, di):
            nr = gr_ref[row] + ar * gr + ai * gi
            ni = gi_ref[row] + ar * gi - ai * gr
            gr_ref[row] = nr
            gi_ref[row] = ni
            return nr, ni, dr + nr * pr + ni * pi, di + ni * pr - nr * pi

        def step(s, c):
            gr, gi, dr, di = c
            row = s if rev else t - 1 - s
            prow = row + 1 if rev else row - 1
            return one(row, gr, gi, hr_ref[prow], hi_ref[prow], dr, di)

        gr, gi, dr, di = lax.fori_loop(0, t - 1, step, (carry_ref[0], carry_ref[1], da_ref[0], da_ref[1]), unroll=4)
        first = i == n - 1
        pr = jnp.where(first, hb_ref[0], pr_ref[0])
        pi = jnp.where(first, hb_ref[1], pi_ref[0])
        gr, gi, dr, di = one(t - 1 if rev else 0, gr, gi, pr, pi, dr, di)
        da_ref[0] = dr
        da_ref[1] = di
        for q, val in enumerate((gr, gi)):
            carry_ref[q] = val
            gout_ref[q] = val

        u = u_ref[...].astype(MXU_DTYPE)
        for j in range(nb // 2):
            sl = slice(8 * j, 8 * j + 8)
            g_r = gr_ref[:, sl, :].reshape(t, 8 * LANES).astype(MXU_DTYPE)
            g_i = gi_ref[:, sl, :].reshape(t, 8 * LANES).astype(MXU_DTYPE)
            if with_dy:
                h_r = hr_ref[:, sl, :].reshape(t, 8 * LANES).astype(MXU_DTYPE)
                h_i = hi_ref[:, sl, :].reshape(t, 8 * LANES).astype(MXU_DTYPE)
            for kk in range(2):
                k = 2 * j + kk
                cols = slice(kk * spb, (kk + 1) * spb)
                lanes = slice(k * LANES, (k + 1) * LANES)
                du_ref[:, lanes] = _mxu_dot(g_r[:, cols], wbrt_ref[k]) + _mxu_dot(g_i[:, cols], wbit_ref[k])
                dwbr_ref[k] += _mxu_dot(u[:, lanes], g_r[:, cols], _TN)
                dwbi_ref[k] += _mxu_dot(u[:, lanes], g_i[:, cols], _TN)
                if with_dy:
                    dcre_ref[k] += _mxu_dot(h_r[:, cols], dy[:, lanes], _TN)
                    dncim_ref[k] += _mxu_dot(h_i[:, cols], dy[:, lanes], _TN)

    full = lambda a: pl.BlockSpec(a.shape, lambda i: (0,) * a.ndim)
    row = lambda cb: pl.BlockSpec((t, w), lambda i: (tile(i), cb))
    hspec = pl.BlockSpec((t, nsr, LANES), lambda i: (tile(i), 0, 0))
    if rev:
        pspec = pl.BlockSpec((1, nsr, LANES), lambda i: (jnp.minimum((tile(i) + 1) * t, l - 1), 0, 0))
    else:
        pspec = pl.BlockSpec((1, nsr, LANES), lambda i: (jnp.maximum(tile(i) * t - 1, 0), 0, 0))
    sm = jax.ShapeDtypeStruct((2, nsr, LANES), F32)
    smspec = pl.BlockSpec((2, nsr, LANES), lambda i: (0, 0, 0))
    wsh = jax.ShapeDtypeStruct((nb, LANES, spb), F32)
    csh = jax.ShapeDtypeStruct((nb, spb, LANES), F32)
    in_specs = (([row(0)] if with_dy else []) + [row(u_col), hspec, hspec, pspec, pspec, full(hbound), full(g_in)]
                + [_dir_spec(a, dr) for a in (a_sm, wbr_t, wbi_t) + ((cre_t, ncim_t) if with_dy else ())])
    args = ([dys] if with_dy else []) + [u_arr, hr, hi, hr, hi, hbound, g_in, a_sm, wbr_t, wbi_t] + ([cre_t, ncim_t] if with_dy else [])
    out_shape = (jax.ShapeDtypeStruct((l, w), F32), wsh, wsh) + ((csh, csh) if with_dy else ()) + (sm, sm)
    out_specs = (row(0), full(wsh), full(wsh)) + ((full(csh), full(csh)) if with_dy else ()) + (smspec, smspec)
    return pl.pallas_call(
        body, name=name, out_shape=out_shape, grid=(n,), in_specs=in_specs, out_specs=out_specs,
        scratch_shapes=[pltpu.VMEM((t, nsr, LANES), F32)] * 2 + [pltpu.VMEM((2, nsr, LANES), F32)],
        compiler_params=_cparams("arbitrary"))(*args)


def _glu_fwd(y0, y1, proj, cat, d_skip, w_glu, b_glu, w):
    l = y0.shape[0]
    tl = _tile(l, 256)

    def body(y0_ref, y1_ref, u_ref, z_ref, cat_in, d_ref, wg_ref, bg_ref, ys_ref, cat_ref):
        del cat_in
        ys = y0_ref[...] + y1_ref[...] + d_ref[...] * u_ref[...]
        ys_ref[...] = ys
        gy = _gelu(ys)
        s = _mxu_dot(gy, wg_ref[...]) + bg_ref[...]
        cat_ref[...] = (gy * jax.nn.sigmoid(s) * jax.nn.silu(z_ref[...])).astype(cat_ref.dtype)

    row = pl.BlockSpec((tl, w), lambda i: (i, 0))
    vec = pl.BlockSpec((1, w), lambda i: (0, 0))
    return pl.pallas_call(
        body, name="glu_fwd", out_shape=(jax.ShapeDtypeStruct((l, w), F32), jax.ShapeDtypeStruct(cat.shape, cat.dtype)),
        grid=(l // tl,),
        in_specs=[row, row, pl.BlockSpec((tl, w), lambda i: (i, 3)), pl.BlockSpec((tl, w), lambda i: (i, 4)), HBM,
                  vec, pl.BlockSpec((w, w), lambda i: (0, 0)), vec],
        out_specs=(row, pl.BlockSpec((tl, w), lambda i: (i, 1))), input_output_aliases={4: 1},
        compiler_params=_cparams("parallel"))(y0, y1, proj, proj, cat, d_skip, w_glu, b_glu)


def _glu_bwd(dcat, ys, proj, d_skip, w_glu, b_glu, w):
    l = ys.shape[0]
    tl = _tile(l, 256)

    def body(dy_ref, ys_ref, u_ref, z_ref, d_ref, wg_ref, bg_ref, dys_ref, dp_ref, dbg_ref, dd_ref, dwg_ref):
        i = pl.program_id(0)
        ys_t = ys_ref[...]
        z = z_ref[...]
        dyb = dy_ref[...].astype(F32)
        gy = _gelu(ys_t)
        sg = jax.nn.sigmoid(_mxu_dot(gy, wg_ref[...]) + bg_ref[...])
        dp_ref[...] = (dyb * gy * sg * _silu_grad(z)).astype(dp_ref.dtype)
        dglu = dyb * jax.nn.silu(z)
        ds = dglu * gy * sg * (1.0 - sg)
        dgy = dglu * sg + _mxu_dot(ds, wg_ref[...], _NT)
        dys_t = dgy * _gelu_grad(ys_t)
        dys_ref[...] = dys_t

        @pl.when(i == 0)
        def _():
            for r in (dbg_ref, dd_ref, dwg_ref):
                r[...] = jnp.zeros_like(r)

        dbg_ref[...] += jnp.sum(ds, axis=0, keepdims=True)
        dd_ref[...] += jnp.sum(dys_t * u_ref[...], axis=0, keepdims=True)
        dwg_ref[...] += _mxu_dot(gy, ds, _TN)

    row = pl.BlockSpec((tl, w), lambda i: (i, 0))
    vec = pl.BlockSpec((1, w), lambda i: (0, 0))
    mat = pl.BlockSpec((w, w), lambda i: (0, 0))
    v = jax.ShapeDtypeStruct((1, w), F32)
    return pl.pallas_call(
        body, name="glu_bwd",
        out_shape=(jax.ShapeDtypeStruct((l, w), F32), jax.ShapeDtypeStruct((l, 5 * w), MXU_DTYPE), v, v,
                   jax.ShapeDtypeStruct((w, w), F32)),
        grid=(l // tl,),
        in_specs=[pl.BlockSpec((tl, w), lambda i: (i, 1)), row, pl.BlockSpec((tl, w), lambda i: (i, 3)),
                  pl.BlockSpec((tl, w), lambda i: (i, 4)), vec, mat, vec],
        out_specs=(row, pl.BlockSpec((tl, w), lambda i: (i, 4)), vec, vec, mat),
        compiler_params=_cparams("arbitrary"))(dcat, ys, proj, proj, d_skip, w_glu, b_glu)


def _dub_combine(dys, du0, du1, d_skip, dproj, w):
    l = dys.shape[0]
    tl = _tile(l, 512)

    def body(dys_ref, a_ref, b_ref, d_ref, dp_in, dp_ref):
        del dp_in
        dp_ref[...] = (dys_ref[...] * d_ref[...] + a_ref[...] + b_ref[...]).astype(dp_ref.dtype)

    row = pl.BlockSpec((tl, w), lambda i: (i, 0))
    return pl.pallas_call(
        body, name="dub_combine", out_shape=jax.ShapeDtypeStruct(dproj.shape, dproj.dtype), grid=(l // tl,),
        in_specs=[row, row, row, pl.BlockSpec((1, w), lambda i: (0, 0)), HBM],
        out_specs=pl.BlockSpec((tl, w), lambda i: (i, 3)), input_output_aliases={4: 0},
        compiler_params=_cparams("parallel"))(dys, du0, du1, d_skip, dproj)


def _add2(a, b, name):
    l, w = a.shape
    tl = _tile(l, 512)

    def body(a_ref, b_ref, o_ref):
        o_ref[...] = a_ref[...] + b_ref[...]

    row = pl.BlockSpec((tl, w), lambda i: (i, 0))
    return pl.pallas_call(body, name=name, out_shape=jax.ShapeDtypeStruct((l, w), F32), grid=(l // tl,),
                          in_specs=[row, row], out_specs=row, compiler_params=_cparams("parallel"))(a, b)


def _adamw_nd(w, m, v, g, name):
    shape = w.shape
    lead = math.prod(shape[:-2]) if len(shape) > 2 else 1
    b, c = (shape[-2], shape[-1]) if len(shape) >= 2 else (1, shape[-1])
    t3 = (lead, b, c)
    padded_row = -(-b // 8) * 8 * -(-c // LANES) * LANES * 4
    ta = _tile(lead, max(1, (2 << 20) // padded_row))

    def body(w_ref, m_ref, v_ref, g_ref, d_ref, mo_ref, vo_ref):
        gv = g_ref[...]
        mn = ADAM_B1 * m_ref[...] + (1.0 - ADAM_B1) * gv
        vn = ADAM_B2 * v_ref[...] + (1.0 - ADAM_B2) * (gv * gv)
        m_hat = mn / (1.0 - ADAM_B1 ** ADAM_STEP)
        v_hat = vn / (1.0 - ADAM_B2 ** ADAM_STEP)
        d_ref[...] = -ADAM_LR * (m_hat / (jnp.sqrt(v_hat) + ADAM_EPS) + ADAM_WD * w_ref[...])
        mo_ref[...] = mn
        vo_ref[...] = vn

    blk = pl.BlockSpec((ta, b, c), lambda i: (i, 0, 0))
    s = jax.ShapeDtypeStruct(t3, F32)
    outs = pl.pallas_call(body, name=name, out_shape=(s, s, s), grid=(lead // ta,), in_specs=[blk] * 4, out_specs=(blk,) * 3,
                          compiler_params=_cparams("parallel"))(*[a.reshape(t3) for a in (w, m, v, g)])
    return tuple(o.reshape(shape) for o in outs)


def _adamw(w, m, v, gparts, name):
    r, c = w.shape
    np_ = gparts.shape[0]
    tr = _tile(r, max(8, (1 << 18) // c), 8)

    def body(w_ref, m_ref, v_ref, g_ref, go_ref, d_ref, mo_ref, vo_ref):
        g = g_ref[0].astype(F32)
        for p in range(1, np_):
            g = g + g_ref[p].astype(F32)
        mn = ADAM_B1 * m_ref[...] + (1.0 - ADAM_B1) * g
        vn = ADAM_B2 * v_ref[...] + (1.0 - ADAM_B2) * (g * g)
        m_hat = mn / (1.0 - ADAM_B1 ** ADAM_STEP)
        v_hat = vn / (1.0 - ADAM_B2 ** ADAM_STEP)
        go_ref[...] = g
        d_ref[...] = -ADAM_LR * (m_hat / (jnp.sqrt(v_hat) + ADAM_EPS) + ADAM_WD * w_ref[...])
        mo_ref[...] = mn
        vo_ref[...] = vn

    row = pl.BlockSpec((tr, c), lambda i: (i, 0))
    s = jax.ShapeDtypeStruct((r, c), F32)
    return pl.pallas_call(body, name=name, out_shape=(s, s, s, s), grid=(r // tr,),
                          in_specs=[row, row, row, pl.BlockSpec((np_, tr, c), lambda i: (0, i, 0))],
                          out_specs=(row, row, row, row), compiler_params=_cparams("parallel"))(w, m, v, gparts)


def _sum_slots(parts, name):
    np_, r, c = parts.shape

    def body(p_ref, o_ref):
        g = p_ref[0]
        for p in range(1, np_):
            g = g + p_ref[p]
        o_ref[...] = g

    return pl.pallas_call(body, name=name, out_shape=jax.ShapeDtypeStruct((r, c), F32), compiler_params=_cparams())(parts)


def _block_diag(x, gb):
    nd, g, a, b = x.shape
    eye = jnp.eye(gb, dtype=x.dtype)
    y = jnp.einsum("dkgab,gh->dkgahb", x.reshape(nd, g // gb, gb, a, b), eye)
    return y.reshape(nd, g // gb, gb * a, gb * b)


def _block_diag_extract(y, gb, a, b):
    nd, nbk = y.shape[:2]
    eye = jnp.eye(gb, dtype=y.dtype)
    x = jnp.einsum("dkgahb,gh->dkgab", y.reshape(nd, nbk, gb, a, gb, b), eye)
    return x.reshape(nd, nbk * gb, a, b)


def kernel(x, c, ctx, c_ctx, w_ada, b_ada, w_in, sgu_ln_g, sgu_ln_b, w_spatial, b_spatial, s5_lam_re, s5_lam_im, s5_log_step, s5_b_re, s5_b_im, s5_c_re, s5_c_im, s5_d, w_glu, b_glu, w_out, ln_g, ln_b, loss_target, m_c_ctx, m_w_ada, m_b_ada, m_w_in, m_sgu_ln_g, m_sgu_ln_b, m_w_spatial, m_b_spatial, m_s5_lam_re, m_s5_lam_im, m_s5_log_step, m_s5_b_re, m_s5_b_im, m_s5_c_re, m_s5_c_im, m_s5_d, m_w_glu, m_b_glu, m_w_out, m_ln_g, m_ln_b, v_c_ctx, v_w_ada, v_b_ada, v_w_in, v_sgu_ln_g, v_sgu_ln_b, v_w_spatial, v_b_spatial, v_s5_lam_re, v_s5_lam_im, v_s5_log_step, v_s5_b_re, v_s5_b_im, v_s5_c_re, v_s5_c_im, v_s5_d, v_w_glu, v_b_glu, v_w_out, v_ln_g, v_ln_b):
    small_names = ["c_ctx", "b_ada", "sgu_ln_g", "sgu_ln_b", "w_spatial", "b_spatial", "s5_lam_re", "s5_lam_im",
                   "s5_log_step", "s5_b_re", "s5_b_im", "s5_c_re", "s5_c_im", "s5_d", "b_glu", "ln_g", "ln_b"]
    env = dict(locals())
    x2, tgt, ctx2 = x[0], loss_target[0], ctx[0]
    l, d = x2.shape
    lc = ctx2.shape[0]
    w = d // 2
    nh = w // HEAD_DIM_A
    nd, g_s5, p_s5, c_s5 = s5_b_re.shape[1:]
    ns = g_s5 * p_s5
    nsr = ns // LANES
    gb = LANES // c_s5
    me = _index(_mesh_pos())
    ada_cols = w_ada.shape[2]

    srows = _silu_rows(c, c_ctx)
    srows_all = _all_gather(srows, 0, "gather_silu")
    s_mat = jnp.concatenate([srows_all[0::8], srows_all[1:2], jnp.zeros((7, d), F32)], axis=0)
    mod_part = _small_dot(s_mat, w_ada[0], "nn", "mod_cols")
    mod_all = _all_gather(mod_part, 1, "gather_mod") + b_ada
    mod_x = lax.dynamic_slice_in_dim(mod_all, me, 1, axis=0)
    mod_c = mod_all[8:9]
    shift_x, scale_x, gate_x = mod_x[:, :d], mod_x[:, d:2 * d], mod_x[:, 2 * d:]
    shift_c, scale_c = mod_c[:, :d], mod_c[:, d:2 * d]

    w_in_f = _all_gather(w_in[0].astype(MXU_DTYPE), 1, "gather_w_in")
    w_out_f = _all_gather(w_out[0].astype(MXU_DTYPE), 0, "gather_w_out")
    w_glu_f = _all_gather(w_glu[0].astype(MXU_DTYPE), 0, "gather_w_glu")

    lr, li = s5_lam_re[0], s5_lam_im[0]
    ls = s5_log_step[0][..., None]
    br_t = jnp.transpose(s5_b_re[0], (0, 3, 1, 2))
    bi_t = jnp.transpose(s5_b_im[0], (0, 3, 1, 2))
    ab_re, ab_im, bb_re, bb_im = _disc_fwd(lr, li, ls, br_t, bi_t)
    a_sm = jnp.stack([ab_re, ab_im], axis=1).reshape(nd, 2, nsr, LANES)
    wbr = _block_diag(jnp.transpose(bb_re, (0, 2, 1, 3)), gb).astype(MXU_DTYPE)
    wbi = _block_diag(jnp.transpose(bb_im, (0, 2, 1, 3)), gb).astype(MXU_DTYPE)
    cre_t = _block_diag(s5_c_re[0], gb).astype(MXU_DTYPE)
    ncim_t = _block_diag(-s5_c_im[0], gb).astype(MXU_DTYPE)
    cre, ncim = jnp.swapaxes(cre_t, 2, 3), jnp.swapaxes(ncim_t, 2, 3)
    wbr_t, wbi_t = jnp.swapaxes(wbr, 2, 3), jnp.swapaxes(wbi, 2, 3)
    d_skip = s5_d

    xm = _ln_mod(x2, shift_x, scale_x, "ln_mod_x")
    cm = _ln_mod(ctx2, shift_c, scale_c, "ln_mod_ctx")
    proj = _matmul(xm, w_in_f, mode="nn", name="proj", bm=512, bn=1024, bk=d)
    ub_c = _matmul(cm, w_in_f, mode="nn", name="proj_ctx", bm=256, bn=w, bk=d, b_n0=3 * w, n=w)
    bsf = jnp.repeat(b_spatial[0].T, HEAD_DIM_A, axis=1)
    ws = w_spatial[0]
    cat = _ga_fwd(proj, sgu_ln_g, sgu_ln_b, ws, bsf, w)
    zeros_state = jnp.zeros((2, nsr, LANES), F32)
    s5c, s5l = [], []
    for dr in range(nd):
        s5c.append(_s5_fwd(ub_c, 0, w, zeros_state, a_sm, wbr, wbi, cre, ncim, dr, f"s5_fwd_ctx{dr}"))
        s5l.append(_s5_fwd(proj, 3, w, s5c[dr][3], a_sm, wbr, wbi, cre, ncim, dr, f"s5_fwd{dr}"))
    ys, cat = _glu_fwd(s5l[0][0], s5l[1][0], proj, cat, d_skip, w_glu_f, b_glu, w)
    out = _matmul(cat, w_out_f, mode="nn", name="out_proj", bm=512, bn=1024, bk=2 * w)
    loss_row, dout, dx_res, dgate, dln_g, dln_b = _post_ln_loss(x2, out, gate_x, ln_g, ln_b, tgt)

    dcat = _matmul(dout, w_out_f, mode="nt", name="d_cat", bm=512, bn=1024, bk=d, out_dtype=MXU_DTYPE)
    dw_out = _matmul(cat, dout, mode="tn", name="d_w_out", bm=1024, bn=1024, bk=2048, out_dtype=MXU_DTYPE)
    dys, dproj, db_glu, dd_skip, dw_glu = _glu_bwd(dcat, ys, proj, d_skip, w_glu_f, b_glu, w)
    du_l, du_c, dwbr, dwbi, dcre, dncim, da_sm = [], [], [], [], [], [], []
    nbk, spb = w // LANES, gb * p_s5
    for dr in range(nd):
        bl = _s5_bwd(dys, proj, 3, w, s5l[dr][1], s5l[dr][2], s5c[dr][3], zeros_state, a_sm, wbr_t, wbi_t,
                     cre_t, ncim_t, dr, f"s5_bwd{dr}")
        bc = _s5_bwd(None, ub_c, 0, w, s5c[dr][1], s5c[dr][2], zeros_state, bl[6], a_sm, wbr_t, wbi_t,
                     None, None, dr, f"s5_bwd_ctx{dr}")
        du_l.append(bl[0])
        du_c.append(bc[0])
        dwbr.append(_add2(bl[1].reshape(nbk * LANES, spb), bc[1].reshape(nbk * LANES, spb), f"sum_dwbr{dr}"))
        dwbi.append(_add2(bl[2].reshape(nbk * LANES, spb), bc[2].reshape(nbk * LANES, spb), f"sum_dwbi{dr}"))
        dcre.append(bl[3])
        dncim.append(bl[4])
        da_sm.append(_add2(bl[5].reshape(2 * nsr, LANES), bc[3].reshape(2 * nsr, LANES), f"sum_da{dr}"))
    dproj = _dub_combine(dys, du_l[0], du_l[1], d_skip, dproj, w)
    dub_c = _add2(du_c[0], du_c[1], "dub_ctx")
    dwbr = jnp.stack(dwbr).reshape(nd, nbk, LANES, spb)
    dwbi = jnp.stack(dwbi).reshape(nd, nbk, LANES, spb)
    dcre, dncim = jnp.stack(dcre), jnp.stack(dncim)
    da_sm = jnp.stack(da_sm).reshape(nd, 2, g_s5, p_s5)
    dproj, dsg, dsb, dws, dbsf = _ga_bwd(proj, dcat, dproj, sgu_ln_g, sgu_ln_b, ws, bsf, w)
    dw_in = _matmul(xm, dproj, mode="tn", name="d_w_in", bm=1024, bn=1280, bk=2048, out_dtype=MXU_DTYPE)
    dw_in = _matmul(cm, dub_c, mode="tn", name="d_w_in_ctx", bm=1024, bn=w, bk=lc, acc_in=dw_in, acc_n0=3 * w,
                    out_dtype=MXU_DTYPE)
    dxm = _matmul(dproj, w_in_f, mode="nt", name="d_xm", bm=512, bn=1024, bk=5 * w)
    dcm = _matmul(dub_c, w_in_f, mode="nt", name="d_cm", bm=256, bn=1024, bk=w, b_k0=3 * w, k=w)
    grad_x, dshift_x, dscale_x = _ln_mod_bwd(x2, dxm, scale_x, dx_res, "ln_mod_x_bwd")
    _, dshift_c, dscale_c = _ln_mod_bwd(ctx2, dcm, scale_c, None, "ln_mod_ctx_bwd")

    dbb_re = jnp.transpose(_block_diag_extract(dwbr, gb, c_s5, p_s5), (0, 2, 1, 3))
    dbb_im = jnp.transpose(_block_diag_extract(dwbi, gb, c_s5, p_s5), (0, 2, 1, 3))
    dc_re = jnp.swapaxes(_block_diag_extract(dcre, gb, p_s5, c_s5), 2, 3)
    dc_im = -jnp.swapaxes(_block_diag_extract(dncim, gb, p_s5, c_s5), 2, 3)
    dlr, dli, dls, dbr_t, dbi_t = _disc_bwd(lr, li, ls, br_t, bi_t, da_sm[:, 0], da_sm[:, 1], dbb_re, dbb_im)
    db_re = jnp.transpose(dbr_t, (0, 2, 3, 1))
    db_im = jnp.transpose(dbi_t, (0, 2, 3, 1))
    expand = (jnp.arange(w)[:, None] // HEAD_DIM_A == jnp.arange(LANES)[None, :]).astype(F32)
    db_sp = _lane_group_sum(dbsf, expand, "d_b_spatial")[:, :nh].T

    dmod_rows = jnp.concatenate([jnp.concatenate([dshift_x, dscale_x, dgate], axis=1),
                                 jnp.concatenate([dshift_c, dscale_c, jnp.zeros((1, d), F32)], axis=1),
                                 jnp.zeros((6, 3 * d), F32)], axis=0)
    dmod_all = _all_gather(dmod_rows, 0, "gather_dmod")
    dmod_ctx = _sum_slots(dmod_all[1::8].reshape(N_DEV, 1, 3 * d), "sum_dmod_ctx")
    dmod_mat = jnp.concatenate([dmod_all[0::8], dmod_ctx, jnp.zeros((7, 3 * d), F32)], axis=0)
    db_ada = _sum_slots(dmod_mat[:9].reshape(9, 1, 3 * d), "sum_db_ada")
    dmod_mine = lax.dynamic_slice_in_dim(dmod_mat, me * ada_cols, ada_cols, axis=1)
    dw_ada = _small_dot(s_mat, dmod_mine, "tn", "d_w_ada")
    dsilu_cc = _small_dot(dmod_mine[8:16], w_ada[0], "nt", "d_silu_cctx")[0:1]
    dc_ctx_part = dsilu_cc * _silu_grad(c_ctx.reshape(1, d))

    gp_w_in = _all_to_all(dw_in, 1, "a2a_d_w_in")
    gp_w_out = _all_to_all(dw_out, 0, "a2a_d_w_out")
    gp_w_glu = _all_to_all(dw_glu.astype(MXU_DTYPE), 0, "a2a_d_w_glu")
    big = {
        "w_ada": _adamw(w_ada[0], m_w_ada[0], v_w_ada[0], dw_ada[None], "adamw_w_ada"),
        "w_in": _adamw(w_in[0], m_w_in[0], v_w_in[0], gp_w_in, "adamw_w_in"),
        "w_glu": _adamw(w_glu[0], m_w_glu[0], v_w_glu[0], gp_w_glu, "adamw_w_glu"),
        "w_out": _adamw(w_out[0], m_w_out[0], v_w_out[0], gp_w_out, "adamw_w_out"),
    }

    local = {"c_ctx": dc_ctx_part, "sgu_ln_g": dsg, "sgu_ln_b": dsb, "w_spatial": dws, "b_spatial": db_sp,
             "s5_lam_re": dlr, "s5_lam_im": dli, "s5_log_step": dls, "s5_b_re": db_re, "s5_b_im": db_im,
             "s5_c_re": dc_re, "s5_c_im": dc_im, "s5_d": dd_skip, "b_glu": db_glu, "ln_g": dln_g, "ln_b": dln_b}
    reduced = [n for n in small_names if n != "b_ada"]
    loss_part = (0.5 / d) * jnp.sum(loss_row)
    flat = jnp.concatenate([loss_part.reshape(1)] + [local[n].reshape(-1) for n in reduced])
    unit = N_DEV * 8 * LANES
    total = -(-flat.shape[0] // unit) * unit
    flat = jnp.pad(flat, (0, total - flat.shape[0])).reshape(N_DEV * 8, total // (N_DEV * 8))
    slots = _all_to_all(flat, 0, "a2a_small")
    mine = _sum_slots(slots, "sum_small")
    summed = _all_gather(mine, 0, "gather_small").reshape(-1)
    loss = summed[0]
    grads, off = {"b_ada": db_ada}, 1
    for n in reduced:
        size = math.prod(env[n].shape)
        grads[n] = summed[off:off + size].reshape(env[n].shape)
        off += size
    res = {n: tuple(a[None] for a in big[n]) for n in big}
    for n in small_names:
        res[n] = (grads[n],) + _adamw_nd(env[n], env["m_" + n], env["v_" + n], grads[n], "adamw_" + n)

    order = ["c_ctx", "w_ada", "b_ada", "w_in", "sgu_ln_g", "sgu_ln_b", "w_spatial", "b_spatial", "s5_lam_re", "s5_lam_im",
             "s5_log_step", "s5_b_re", "s5_b_im", "s5_c_re", "s5_c_im", "s5_d", "w_glu", "b_glu", "w_out", "ln_g", "ln_b"]
    return (loss, grad_x[None], *[res[n][0] for n in order], *[res[n][1] for n in order],
            *[res[n][2] for n in order], *[res[n][3] for n in order])
```

```python
import functools
import math

import jax
import jax.numpy as jnp
from jax import lax
from jax.experimental import pallas as pl
from jax.experimental.pallas import tpu as pltpu

F32 = jnp.float32
MXU_DTYPE = jnp.bfloat16
N_DEV = 8
MESH_ID = pl.DeviceIdType.MESH
LN_EPS = 1e-6
DEPTH = 1
ALPHA = (2.0 * DEPTH) ** 0.25
CHUNK = 128
HEAD_DIM_A = 128
ADAM_LR, ADAM_B1, ADAM_B2, ADAM_EPS, ADAM_WD, ADAM_STEP = 0.001, 0.9, 0.999, 1e-08, 0.01, 10
LANES = 128
VMEM_LIMIT = 56 * 1024 * 1024
HBM = pl.BlockSpec(memory_space=pl.ANY)


def _cparams(*sem):
    return pltpu.CompilerParams(dimension_semantics=sem if sem else None, vmem_limit_bytes=VMEM_LIMIT)


def _tile(n, pref, mult=1):
    if n <= pref:
        return n
    t = pref - pref % mult
    while n % t:
        t -= mult
    return t


def _gelu(x):
    return 0.5 * x * (1.0 + lax.erf(x * (1.0 / math.sqrt(2.0))))


def _gelu_grad(x):
    return 0.5 * (1.0 + lax.erf(x * (1.0 / math.sqrt(2.0)))) + x * jnp.exp(-0.5 * x * x) * (1.0 / math.sqrt(2.0 * math.pi))


def _silu_grad(x):
    s = jax.nn.sigmoid(x)
    return s * (1.0 + x * (1.0 - s))


def _mxu_dot(a, b, dims=(((1,), (0,)), ((), ()))):
    return lax.dot_general(a.astype(MXU_DTYPE), b.astype(MXU_DTYPE), dims, preferred_element_type=F32)


_NT = (((1,), (1,)), ((), ()))
_TN = (((0,), (0,)), ((), ()))


def _mesh_pos():
    return lax.axis_index("x"), lax.axis_index("y"), lax.axis_index("c")


def _peer(pos, r):
    x, y, c = pos
    return ((1 - x) if r & 4 else x, (1 - y) if r & 2 else y, (1 - c) if r & 1 else c)


def _index(pos):
    return 4 * pos[0] + 2 * pos[1] + pos[2]


def _slice_of(ref, axis, idx, size):
    start = idx * size
    if axis == 0:
        return ref.at[pl.ds(start, size)]
    return ref.at[:, pl.ds(start, size)]


def _all_gather(x, axis, name):
    size = x.shape[axis]
    out_shape = tuple(s * N_DEV if a == axis else s for a, s in enumerate(x.shape))

    def body(x_ref, o_ref, send_sems, recv_sems, local_sem):
        me = _mesh_pos()
        mine = pltpu.make_async_copy(x_ref, _slice_of(o_ref, axis, _index(me), size), local_sem)
        mine.start()

        def copy(r, block):
            return pltpu.make_async_remote_copy(
                src_ref=x_ref, dst_ref=_slice_of(o_ref, axis, _index(block), size),
                send_sem=send_sems.at[r - 1], recv_sem=recv_sems.at[r - 1],
                device_id=_peer(me, r), device_id_type=MESH_ID)

        sends = [copy(r, me) for r in range(1, N_DEV)]
        for cp in sends:
            cp.start()
        for r in range(1, N_DEV):
            copy(r, _peer(me, r)).wait_recv()
        for cp in sends:
            cp.wait_send()
        mine.wait()

    return pl.pallas_call(
        body, name=name, out_shape=jax.ShapeDtypeStruct(out_shape, x.dtype),
        in_specs=[HBM], out_specs=HBM,
        scratch_shapes=[pltpu.SemaphoreType.DMA((N_DEV - 1,)), pltpu.SemaphoreType.DMA((N_DEV - 1,)),
                        pltpu.SemaphoreType.DMA],
    )(x)


def _all_to_all(x, axis, name):
    size = x.shape[axis] // N_DEV
    slot = tuple(size if a == axis else s for a, s in enumerate(x.shape))

    def body(x_ref, o_ref, send_sems, recv_sems, local_sem):
        me = _mesh_pos()
        mine = pltpu.make_async_copy(_slice_of(x_ref, axis, _index(me), size), o_ref.at[_index(me)], local_sem)
        mine.start()

        def copy(r, sender, receiver):
            return pltpu.make_async_remote_copy(
                src_ref=_slice_of(x_ref, axis, _index(receiver), size), dst_ref=o_ref.at[_index(sender)],
                send_sem=send_sems.at[r - 1], recv_sem=recv_sems.at[r - 1],
                device_id=_peer(me, r), device_id_type=MESH_ID)

        sends = [copy(r, me, _peer(me, r)) for r in range(1, N_DEV)]
        for cp in sends:
            cp.start()
        for r in range(1, N_DEV):
            copy(r, _peer(me, r), me).wait_recv()
        for cp in sends:
            cp.wait_send()
        mine.wait()

    return pl.pallas_call(
        body, name=name, out_shape=jax.ShapeDtypeStruct((N_DEV,) + slot, x.dtype),
        in_specs=[HBM], out_specs=HBM,
        scratch_shapes=[pltpu.SemaphoreType.DMA((N_DEV - 1,)), pltpu.SemaphoreType.DMA((N_DEV - 1,)),
                        pltpu.SemaphoreType.DMA],
    )(x)


_SEM = pl.BlockSpec(memory_space=pltpu.SEMAPHORE)
_HBM = pl.BlockSpec(memory_space=pltpu.HBM)
_EFFECT = pltpu.SideEffectType.DATAFLOW_SIDE_EFFECTING


def _exchange_copy(kind, x_ref, land_ref, axis, size, send_sems, recv_sems, me, r, arriving):
    peer = _peer(me, r)
    sender, receiver = (peer, me) if arriving else (me, peer)
    if kind == "gather":
        src, dst = x_ref, _slice_of(land_ref, axis, _index(sender), size)
    else:
        src, dst = _slice_of(x_ref, axis, _index(receiver), size), land_ref.at[_index(sender)]
    return pltpu.make_async_remote_copy(src_ref=src, dst_ref=dst, send_sem=send_sems.at[r - 1], recv_sem=recv_sems.at[r - 1],
                                        device_id=peer, device_id_type=MESH_ID)


def _exchange_start(x, axis, kind, name):
    size = x.shape[axis] if kind == "gather" else x.shape[axis] // N_DEV
    if kind == "gather":
        land_shape = tuple(s * N_DEV if a == axis else s for a, s in enumerate(x.shape))
    else:
        land_shape = (N_DEV,) + tuple(size if a == axis else s for a, s in enumerate(x.shape))

    def body(x_ref, land_ref, send_sems, recv_sems, x_thru, land_thru, token, local_sem):
        del x_thru, land_thru
        me = _mesh_pos()
        if kind == "gather":
            mine = pltpu.make_async_copy(x_ref, _slice_of(land_ref, axis, _index(me), size), local_sem)
        else:
            mine = pltpu.make_async_copy(_slice_of(x_ref, axis, _index(me), size), land_ref.at[_index(me)], local_sem)
        mine.start()
        for r in range(1, N_DEV):
            _exchange_copy(kind, x_ref, land_ref, axis, size, send_sems, recv_sems, me, r, False).start()
        mine.wait()
        token[...] = jnp.zeros_like(token)

    sems = pltpu.SemaphoreType.DMA((N_DEV - 1,))
    send_sems, recv_sems, x_thru, land_thru, token = pl.pallas_call(
        body, name=name,
        out_shape=(sems, sems, pltpu.HBM(x.shape, x.dtype), pltpu.HBM(land_shape, x.dtype), jax.ShapeDtypeStruct((8, LANES), F32)),
        in_specs=(_HBM, _HBM), out_specs=(_SEM, _SEM, _HBM, _HBM, pl.BlockSpec(memory_space=pltpu.VMEM)),
        scratch_shapes=[pltpu.SemaphoreType.DMA], input_output_aliases={0: 2, 1: 3},
        compiler_params=pltpu.CompilerParams(has_side_effects=_EFFECT),
    )(pltpu.with_memory_space_constraint(x, pltpu.HBM),
      pltpu.with_memory_space_constraint(lax.empty(land_shape, x.dtype), pltpu.HBM))
    return (kind, axis, size, send_sems, recv_sems, x_thru, land_thru), token


def _exchange_wait(handle, after, name):
    kind, axis, size, send_sems, recv_sems, x_thru, land_thru = handle

    def body(x_ref, land_ref, send_sems, recv_sems, after_ref, x_dead, got_ref):
        del after_ref, x_dead, got_ref
        me = _mesh_pos()
        for r in range(1, N_DEV):
            _exchange_copy(kind, x_ref, land_ref, axis, size, send_sems, recv_sems, me, r, False).wait_send()
        for r in range(1, N_DEV):
            _exchange_copy(kind, x_ref, land_ref, axis, size, send_sems, recv_sems, me, r, True).wait_recv()

    return pl.pallas_call(
        body, name=name, out_shape=(pltpu.HBM(x_thru.shape, x_thru.dtype), pltpu.HBM(land_thru.shape, land_thru.dtype)),
        in_specs=(_HBM, _HBM, _SEM, _SEM, HBM), out_specs=(_HBM, _HBM), input_output_aliases={0: 0, 1: 1},
        compiler_params=pltpu.CompilerParams(has_side_effects=_EFFECT),
    )(x_thru, land_thru, send_sems, recv_sems, after)[1]


def _matmul(a, b, *, mode, name, bm=512, bn=512, bk=512, out_dtype=F32, b_n0=0, n=None, b_k0=0, k=None,
            acc_in=None, acc_n0=0, dep=None):
    if mode == "tn":
        kk, m = a.shape
    else:
        m, kk = a.shape
    if mode == "nn":
        n = b.shape[1] if n is None else n
    elif mode == "nt":
        n = b.shape[0]
        kk = kk if k is None else k
    else:
        n = b.shape[1]
    bm, bn, bk = _tile(m, bm), _tile(n, bn), _tile(kk, bk)
    nk = kk // bk
    assert b_n0 % bn == 0 and b_k0 % bk == 0 and acc_n0 % bn == 0
    dims = {"nn": (((1,), (0,)), ((), ())), "nt": _NT, "tn": _TN}[mode]

    n_in = 2 + (acc_in is not None) + (dep is not None)

    def body(*refs):
        a_ref, b_ref = refs[:2]
        init = refs[2] if acc_in is not None else None
        o_ref = refs[n_in]
        acc_ref = refs[-1] if nk > 1 else None
        p = _mxu_dot(a_ref[...], b_ref[...], dims)
        if nk == 1:
            o_ref[...] = (p if init is None else p + init[...]).astype(out_dtype)
            return
        ki = pl.program_id(2)

        @pl.when(ki == 0)
        def _():
            acc_ref[...] = p if init is None else p + init[...]

        @pl.when(ki > 0)
        def _():
            acc_ref[...] += p

        @pl.when(ki == nk - 1)
        def _():
            o_ref[...] = acc_ref[...].astype(out_dtype)

    a_spec = pl.BlockSpec((bk, bm), lambda j, i, q: (q, i)) if mode == "tn" else pl.BlockSpec((bm, bk), lambda j, i, q: (i, q))
    if mode == "nt":
        b_spec = pl.BlockSpec((bn, bk), lambda j, i, q: (j, q + b_k0 // bk))
    else:
        b_spec = pl.BlockSpec((bk, bn), lambda j, i, q: (q, j + b_n0 // bn))
    in_specs, args, aliases = [a_spec, b_spec], [a, b], {}
    out_map = lambda j, i, q: (i, j + acc_n0 // bn)
    if acc_in is not None:
        in_specs.append(pl.BlockSpec((bm, bn), out_map))
        args.append(acc_in)
        aliases = {2: 0}
        out_shape = jax.ShapeDtypeStruct(acc_in.shape, out_dtype)
    else:
        out_shape = jax.ShapeDtypeStruct((m, n), out_dtype)
    if dep is not None:
        in_specs.append(HBM)
        args.append(dep)
    return pl.pallas_call(
        body, name=name, out_shape=out_shape, grid=(n // bn, m // bm, nk),
        in_specs=in_specs, out_specs=pl.BlockSpec((bm, bn), out_map),
        scratch_shapes=[pltpu.VMEM((bm, bn), F32)] if nk > 1 else [],
        input_output_aliases=aliases,
        compiler_params=_cparams("parallel", "parallel", "arbitrary"),
    )(*args)


def _silu_rows(c, c_ctx):
    d = c.shape[-1]

    def body(c_ref, cc_ref, o_ref):
        o_ref[...] = jnp.zeros_like(o_ref)
        o_ref[0:1, :] = jax.nn.silu(c_ref[...])
        o_ref[1:2, :] = jax.nn.silu(cc_ref[...])

    return pl.pallas_call(body, name="silu_rows", out_shape=jax.ShapeDtypeStruct((8, d), F32))(
        c.reshape(1, d), c_ctx.reshape(1, d))


def _small_dot(a, b, mode, name):
    dims = {"nn": (((1,), (0,)), ((), ())), "nt": _NT, "tn": _TN}[mode]
    m = a.shape[1] if mode == "tn" else a.shape[0]
    n = b.shape[0] if mode == "nt" else b.shape[1]

    def body(a_ref, b_ref, o_ref):
        o_ref[...] = lax.dot_general(a_ref[...], b_ref[...], dims, preferred_element_type=F32,
                                     precision=lax.Precision.HIGHEST)

    return pl.pallas_call(body, name=name, out_shape=jax.ShapeDtypeStruct((m, n), F32),
                          compiler_params=_cparams())(a, b)


def _ln_stats(x):
    mu = jnp.mean(x, axis=-1, keepdims=True)
    xc = x - mu
    var = jnp.mean(xc * xc, axis=-1, keepdims=True)
    rstd = lax.rsqrt(var + LN_EPS)
    return xc * rstd, rstd


def _ln_mod(x, shift, scale, name):
    l, d = x.shape
    tl = _tile(l, 256)

    def body(x_ref, sh_ref, sc_ref, o_ref):
        xhat, _ = _ln_stats(x_ref[...])
        o_ref[...] = (xhat * (1.0 + sc_ref[...]) + sh_ref[...]).astype(o_ref.dtype)

    row = pl.BlockSpec((tl, d), lambda i: (i, 0))
    vec = pl.BlockSpec((1, d), lambda i: (0, 0))
    return pl.pallas_call(body, name=name, out_shape=jax.ShapeDtypeStruct((l, d), MXU_DTYPE), grid=(l // tl,),
                          in_specs=[row, vec, vec], out_specs=row, compiler_params=_cparams("parallel"))(x, shift, scale)


def _ln_mod_bwd(x, dxm, scale, res, name):
    l, d = x.shape
    tl = _tile(l, 256)
    with_res = res is not None

    def body(*refs):
        if with_res:
            x_ref, g_ref, sc_ref, r_ref, dx_ref, dsh_ref, dsc_ref = refs
        else:
            x_ref, g_ref, sc_ref, dx_ref, dsh_ref, dsc_ref = refs
        i = pl.program_id(0)
        xhat, rstd = _ln_stats(x_ref[...])
        g = g_ref[...]
        dxh = g * (1.0 + sc_ref[...])
        dx = rstd * (dxh - jnp.mean(dxh, axis=-1, keepdims=True) - xhat * jnp.mean(dxh * xhat, axis=-1, keepdims=True))
        dx_ref[...] = dx + r_ref[...] if with_res else dx

        @pl.when(i == 0)
        def _():
            dsh_ref[...] = jnp.zeros_like(dsh_ref)
            dsc_ref[...] = jnp.zeros_like(dsc_ref)

        dsh_ref[...] += jnp.sum(g, axis=0, keepdims=True)
        dsc_ref[...] += jnp.sum(g * xhat, axis=0, keepdims=True)

    row = pl.BlockSpec((tl, d), lambda i: (i, 0))
    vec = pl.BlockSpec((1, d), lambda i: (0, 0))
    args = [x, dxm, scale] + ([res] if with_res else [])
    return pl.pallas_call(
        body, name=name,
        out_shape=(jax.ShapeDtypeStruct((l, d), F32), jax.ShapeDtypeStruct((1, d), F32), jax.ShapeDtypeStruct((1, d), F32)),
        grid=(l // tl,), in_specs=[row, row, vec] + ([row] if with_res else []), out_specs=(row, vec, vec),
        compiler_params=_cparams("arbitrary"))(*args)


def _post_ln_loss(x, out, gate, ln_g, ln_b, target):
    l, d = x.shape
    tl = _tile(l, 256)

    def body(x_ref, o_ref, gate_ref, g_ref, b_ref, t_ref, loss_ref, dout_ref, dxr_ref, dgate_ref, dg_ref, db_ref):
        i = pl.program_id(0)
        out_t = o_ref[...]
        gate_v = gate_ref[...]
        rhat, rstd = _ln_stats(ALPHA * x_ref[...] + gate_v * out_t)
        ln_gv = g_ref[...]
        diff = rhat * ln_gv + b_ref[...] - t_ref[...]
        dy = diff * (1.0 / d)
        drh = dy * ln_gv
        dr = rstd * (drh - jnp.mean(drh, axis=-1, keepdims=True) - rhat * jnp.mean(drh * rhat, axis=-1, keepdims=True))
        dout_ref[...] = (gate_v * dr).astype(dout_ref.dtype)
        dxr_ref[...] = ALPHA * dr

        @pl.when(i == 0)
        def _():
            for r in (loss_ref, dgate_ref, dg_ref, db_ref):
                r[...] = jnp.zeros_like(r)

        loss_ref[...] += jnp.sum(diff * diff, axis=0, keepdims=True)
        dgate_ref[...] += jnp.sum(dr * out_t, axis=0, keepdims=True)
        dg_ref[...] += jnp.sum(dy * rhat, axis=0, keepdims=True)
        db_ref[...] += jnp.sum(dy, axis=0, keepdims=True)

    row = pl.BlockSpec((tl, d), lambda i: (i, 0))
    vec = pl.BlockSpec((1, d), lambda i: (0, 0))
    v = jax.ShapeDtypeStruct((1, d), F32)
    return pl.pallas_call(
        body, name="post_ln_loss",
        out_shape=(v, jax.ShapeDtypeStruct((l, d), MXU_DTYPE), jax.ShapeDtypeStruct((l, d), F32), v, v, v),
        grid=(l // tl,), in_specs=[row, row, vec, vec, vec, row], out_specs=(vec, row, row, vec, vec, vec),
        compiler_params=_cparams("arbitrary"))(x, out, gate, ln_g, ln_b, target)


def _ga_forward_tile(p, g, b, ws_ref, bsf, w, nc, nh):
    u_raw, v_raw, za = p[:, :w], p[:, w:2 * w], p[:, 2 * w:3 * w]
    gu = _gelu(u_raw)
    vhat, rstd = _ln_stats(_gelu(v_raw))
    vn = vhat * g + b
    rows = []
    for ci in range(nc):
        r0 = ci * CHUNK
        heads = [_mxu_dot(ws_ref[h], vn[r0:r0 + CHUNK, h * HEAD_DIM_A:(h + 1) * HEAD_DIM_A]) for h in range(nh)]
        rows.append(jnp.concatenate(heads, axis=1) + bsf)
    mixed = jnp.concatenate(rows, axis=0) if nc > 1 else rows[0]
    return u_raw, v_raw, za, gu, vhat, rstd, vn, mixed


def _ga_fwd(proj, g, b, ws, bsf, w):
    l = proj.shape[0]
    nh = w // HEAD_DIM_A
    nc = _tile(l // CHUNK, 2)
    tl = nc * CHUNK

    def body(p_ref, g_ref, b_ref, ws_ref, bsf_ref, o_ref):
        _, _, za, gu, _, _, _, mixed = _ga_forward_tile(p_ref[...], g_ref[...], b_ref[...], ws_ref, bsf_ref[...], w, nc, nh)
        o_ref[...] = (gu * mixed * jax.nn.silu(za)).astype(o_ref.dtype)

    vec = pl.BlockSpec((1, w), lambda i: (0, 0))
    return pl.pallas_call(
        body, name="ga_fwd", out_shape=jax.ShapeDtypeStruct((l, 2 * w), MXU_DTYPE), grid=(l // tl,),
        in_specs=[pl.BlockSpec((tl, 3 * w), lambda i: (i, 0)), vec, vec,
                  pl.BlockSpec((nh, CHUNK, CHUNK), lambda i: (0, 0, 0)), pl.BlockSpec((CHUNK, w), lambda i: (0, 0))],
        out_specs=pl.BlockSpec((tl, w), lambda i: (i, 0)), compiler_params=_cparams("parallel"))(proj, g, b, ws, bsf)


def _ga_bwd(proj, dcat, dproj, g, b, ws, bsf, w):
    l = proj.shape[0]
    nh = w // HEAD_DIM_A
    nc = _tile(l // CHUNK, 2)
    tl = nc * CHUNK

    def body(p_ref, dy_ref, dp_in, g_ref, b_ref, ws_ref, bsf_ref, dp_ref, dg_ref, db_ref, dws_ref, dbsf_ref):
        del dp_in
        i = pl.program_id(0)
        gv = g_ref[...]
        u_raw, v_raw, za, gu, vhat, rstd, vn, mixed = _ga_forward_tile(
            p_ref[...], gv, b_ref[...], ws_ref, bsf_ref[...], w, nc, nh)
        dya = dy_ref[...].astype(F32)
        sz = jax.nn.silu(za)
        dmixed = dya * gu * sz
        dza = dya * gu * mixed * _silu_grad(za)
        dgu = dya * mixed * sz

        @pl.when(i == 0)
        def _():
            for r in (dg_ref, db_ref, dws_ref, dbsf_ref):
                r[...] = jnp.zeros_like(r)

        rows = []
        for ci in range(nc):
            r0 = ci * CHUNK
            heads = []
            for h in range(nh):
                cols = slice(h * HEAD_DIM_A, (h + 1) * HEAD_DIM_A)
                dm = dmixed[r0:r0 + CHUNK, cols]
                heads.append(_mxu_dot(ws_ref[h], dm, _TN))
                dws_ref[h] += _mxu_dot(dm, vn[r0:r0 + CHUNK, cols], _NT)
            rows.append(jnp.concatenate(heads, axis=1))
            dbsf_ref[...] += dmixed[r0:r0 + CHUNK, :]
        dvn = jnp.concatenate(rows, axis=0) if nc > 1 else rows[0]
        dg_ref[...] += jnp.sum(dvn * vhat, axis=0, keepdims=True)
        db_ref[...] += jnp.sum(dvn, axis=0, keepdims=True)
        dvh = dvn * gv
        dgv = rstd * (dvh - jnp.mean(dvh, axis=-1, keepdims=True) - vhat * jnp.mean(dvh * vhat, axis=-1, keepdims=True))
        dp_ref[:, :w] = (dgu * _gelu_grad(u_raw)).astype(dp_ref.dtype)
        dp_ref[:, w:2 * w] = (dgv * _gelu_grad(v_raw)).astype(dp_ref.dtype)
        dp_ref[:, 2 * w:] = dza.astype(dp_ref.dtype)

    vec = pl.BlockSpec((1, w), lambda i: (0, 0))
    ws_spec = pl.BlockSpec((nh, CHUNK, CHUNK), lambda i: (0, 0, 0))
    bs_spec = pl.BlockSpec((CHUNK, w), lambda i: (0, 0))
    v = jax.ShapeDtypeStruct((1, w), F32)
    return pl.pallas_call(
        body, name="ga_bwd",
        out_shape=(jax.ShapeDtypeStruct(dproj.shape, dproj.dtype), v, v, jax.ShapeDtypeStruct((nh, CHUNK, CHUNK), F32),
                   jax.ShapeDtypeStruct((CHUNK, w), F32)),
        grid=(l // tl,),
        in_specs=[pl.BlockSpec((tl, 3 * w), lambda i: (i, 0)), pl.BlockSpec((tl, w), lambda i: (i, 0)), HBM,
                  vec, vec, ws_spec, bs_spec],
        out_specs=(pl.BlockSpec((tl, 3 * w), lambda i: (i, 0)), vec, vec, ws_spec, bs_spec),
        input_output_aliases={2: 0}, compiler_params=_cparams("arbitrary"))(proj, dcat, dproj, g, b, ws, bsf)


def _lane_group_sum(x, expand, name):
    return _small_dot(x, expand, "nn", name)


def _disc_math(lr, li, ls, br, bi):
    step = jnp.exp(ls)
    dr, di = lr * step, li * step
    mag = jnp.exp(dr)
    ab_re, ab_im = mag * jnp.cos(di), mag * jnp.sin(di)
    den = lr * lr + li * li
    nr, ni = ab_re - 1.0, ab_im
    f_re = (nr * lr + ni * li) / den
    f_im = (ni * lr - nr * li) / den
    bb_re = f_re[:, None] * br - f_im[:, None] * bi
    bb_im = f_re[:, None] * bi + f_im[:, None] * br
    return ab_re, ab_im, bb_re, bb_im


def _disc_fwd(lr, li, ls, br, bi):
    def body(lr_ref, li_ref, ls_ref, br_ref, bi_ref, o1, o2, o3, o4):
        res = _disc_math(lr_ref[...], li_ref[...], ls_ref[...], br_ref[...], bi_ref[...])
        for o, r in zip((o1, o2, o3, o4), res):
            o[...] = r

    s = lambda a: jax.ShapeDtypeStruct(a.shape, F32)
    return pl.pallas_call(body, name="s5_disc", out_shape=(s(lr), s(lr), s(br), s(br)), compiler_params=_cparams())(
        lr, li, ls, br, bi)


def _disc_bwd(lr, li, ls, br, bi, d_ar, d_ai, d_br, d_bi):
    def body(lr_ref, li_ref, ls_ref, br_ref, bi_ref, c1, c2, c3, c4, o1, o2, o3, o4, o5):
        _, vjp = jax.vjp(_disc_math, lr_ref[...], li_ref[...], ls_ref[...], br_ref[...], bi_ref[...])
        res = vjp((c1[...], c2[...], c3[...], c4[...]))
        for o, r in zip((o1, o2, o3, o4, o5), res):
            o[...] = r

    s = lambda a: jax.ShapeDtypeStruct(a.shape, F32)
    return pl.pallas_call(body, name="s5_disc_bwd", out_shape=(s(lr), s(lr), s(ls), s(br), s(br)),
                          compiler_params=_cparams())(lr, li, ls, br, bi, d_ar, d_ai, d_br, d_bi)


def _dir_spec(a, dr):
    return pl.BlockSpec((None,) + a.shape[1:], lambda i: (dr,) + (0,) * (a.ndim - 1))


def _s5_fwd(u_arr, u_col, w, h0, a_sm, wbr, wbi, cre, ncim, dr, name):
    rev = dr == 1
    l = u_arr.shape[0]
    nb = w // LANES
    spb = wbr.shape[-1]
    nsr = a_sm.shape[2]
    assert 2 * spb == 8 * LANES and nb % 2 == 0
    t = _tile(l, 256)
    n = l // t
    tile = (lambda i: n - 1 - i) if rev else (lambda i: i)

    def body(u_ref, h0_ref, a_ref, wbr_ref, wbi_ref, cre_ref, ncim_ref, y_ref, hr_ref, hi_ref, hfin_ref, carry_ref):
        i = pl.program_id(0)

        @pl.when(i == 0)
        def _():
            carry_ref[...] = h0_ref[...]

        u = u_ref[...].astype(MXU_DTYPE)
        for j in range(nb // 2):
            for h_ref, w_ref in ((hr_ref, wbr_ref), (hi_ref, wbi_ref)):
                blk = [_mxu_dot(u[:, k * LANES:(k + 1) * LANES], w_ref[k]) for k in (2 * j, 2 * j + 1)]
                h_ref[:, 8 * j:8 * j + 8, :] = jnp.concatenate(blk, axis=1).reshape(t, 8, LANES)
        ar, ai = a_ref[0], a_ref[1]

        def step(s, c):
            hr, hi = c
            row = t - 1 - s if rev else s
            nr = ar * hr - ai * hi + hr_ref[row]
            ni = ar * hi + ai * hr + hi_ref[row]
            hr_ref[row] = nr
            hi_ref[row] = ni
            return nr, ni

        c = lax.fori_loop(0, t, step, (carry_ref[0], carry_ref[1]), unroll=4)
        for q in range(2):
            carry_ref[q] = c[q]
            hfin_ref[q] = c[q]
        for j in range(nb // 2):
            hr = hr_ref[:, 8 * j:8 * j + 8, :].reshape(t, 8 * LANES)
            hi = hi_ref[:, 8 * j:8 * j + 8, :].reshape(t, 8 * LANES)
            for kk in range(2):
                k = 2 * j + kk
                cols = slice(kk * spb, (kk + 1) * spb)
                y_ref[:, k * LANES:(k + 1) * LANES] = (_mxu_dot(hr[:, cols], cre_ref[k]) + _mxu_dot(hi[:, cols], ncim_ref[k]))

    full = lambda a: pl.BlockSpec(a.shape, lambda i: (0,) * a.ndim)
    hspec = pl.BlockSpec((t, nsr, LANES), lambda i: (tile(i), 0, 0))
    hsh = jax.ShapeDtypeStruct((l, nsr, LANES), F32)
    return pl.pallas_call(
        body, name=name, out_shape=(jax.ShapeDtypeStruct((l, w), F32), hsh, hsh, jax.ShapeDtypeStruct((2, nsr, LANES), F32)),
        grid=(n,),
        in_specs=[pl.BlockSpec((t, w), lambda i: (tile(i), u_col)), full(h0)] + [_dir_spec(a, dr) for a in (a_sm, wbr, wbi, cre, ncim)],
        out_specs=(pl.BlockSpec((t, w), lambda i: (tile(i), 0)), hspec, hspec,
                   pl.BlockSpec((2, nsr, LANES), lambda i: (0, 0, 0))),
        scratch_shapes=[pltpu.VMEM((2, nsr, LANES), F32)],
        compiler_params=_cparams("arbitrary"))(u_arr, h0, a_sm, wbr, wbi, cre, ncim)


def _s5_bwd(dys, u_arr, u_col, w, hr, hi, hbound, g_in, a_sm, wbr_t, wbi_t, cre_t, ncim_t, dr, name):
    rev = dr == 1
    l = u_arr.shape[0]
    nb = w // LANES
    spb = wbr_t.shape[-2]
    nsr = a_sm.shape[2]
    t = _tile(l, 128)
    n = l // t
    with_dy = dys is not None
    tile = (lambda i: i) if rev else (lambda i: n - 1 - i)

    def body(*refs):
        if with_dy:
            (dy_ref, u_ref, hr_ref, hi_ref, pr_ref, pi_ref, hb_ref, gin_ref, a_ref, wbrt_ref, wbit_ref, cret_ref, ncimt_ref,
             du_ref, dwbr_ref, dwbi_ref, dcre_ref, dncim_ref, da_ref, gout_ref, gr_ref, gi_ref, carry_ref) = refs
        else:
            (u_ref, hr_ref, hi_ref, pr_ref, pi_ref, hb_ref, gin_ref, a_ref, wbrt_ref, wbit_ref,
             du_ref, dwbr_ref, dwbi_ref, da_ref, gout_ref, gr_ref, gi_ref, carry_ref) = refs
        i = pl.program_id(0)

        @pl.when(i == 0)
        def _():
            carry_ref[...] = gin_ref[...]
            accs = (dwbr_ref, dwbi_ref, da_ref) + ((dcre_ref, dncim_ref) if with_dy else ())
            for r in accs:
                r[...] = jnp.zeros_like(r)

        if with_dy:
            dy = dy_ref[...].astype(MXU_DTYPE)
            for j in range(nb // 2):
                for g_ref, c_ref in ((gr_ref, cret_ref), (gi_ref, ncimt_ref)):
                    blk = [_mxu_dot(dy[:, k * LANES:(k + 1) * LANES], c_ref[k]) for k in (2 * j, 2 * j + 1)]
                    g_ref[:, 8 * j:8 * j + 8, :] = jnp.concatenate(blk, axis=1).reshape(t, 8, LANES)
        else:
            gr_ref[...] = jnp.zeros_like(gr_ref)
            gi_ref[...] = jnp.zeros_like(gi_ref)
        ar, ai = a_ref[0], a_ref[1]

        def one(row, gr, gi, pr, pi, dr, di):
            nr = gr_ref[row] + ar * gr + ai * gi
            ni = gi_ref[row] + ar * gi - ai * gr
            gr_ref[row] = nr
            gi_ref[row] = ni
            return nr, ni, dr + nr * pr + ni * pi, di + ni * pr - nr * pi

        def step(s, c):
            gr, gi, dr, di = c
            row = s if rev else t - 1 - s
            prow = row + 1 if rev else row - 1
            return one(row, gr, gi, hr_ref[prow], hi_ref[prow], dr, di)

        gr, gi, dr, di = lax.fori_loop(0, t - 1, step, (carry_ref[0], carry_ref[1], da_ref[0], da_ref[1]), unroll=4)
        first = i == n - 1
        pr = jnp.where(first, hb_ref[0], pr_ref[0])
        pi = jnp.where(first, hb_ref[1], pi_ref[0])
        gr, gi, dr, di = one(t - 1 if rev else 0, gr, gi, pr, pi, dr, di)
        da_ref[0] = dr
        da_ref[1] = di
        for q, val in enumerate((gr, gi)):
            carry_ref[q] = val
            gout_ref[q] = val

        u = u_ref[...].astype(MXU_DTYPE)
        for j in range(nb // 2):
            sl = slice(8 * j, 8 * j + 8)
            g_r = gr_ref[:, sl, :].reshape(t, 8 * LANES).astype(MXU_DTYPE)
            g_i = gi_ref[:, sl, :].reshape(t, 8 * LANES).astype(MXU_DTYPE)
            if with_dy:
                h_r = hr_ref[:, sl, :].reshape(t, 8 * LANES).astype(MXU_DTYPE)
                h_i = hi_ref[:, sl, :].reshape(t, 8 * LANES).astype(MXU_DTYPE)
            for kk in range(2):
                k = 2 * j + kk
                cols = slice(kk * spb, (kk + 1) * spb)
                lanes = slice(k * LANES, (k + 1) * LANES)
                du_ref[:, lanes] = _mxu_dot(g_r[:, cols], wbrt_ref[k]) + _mxu_dot(g_i[:, cols], wbit_ref[k])
                dwbr_ref[k] += _mxu_dot(u[:, lanes], g_r[:, cols], _TN)
                dwbi_ref[k] += _mxu_dot(u[:, lanes], g_i[:, cols], _TN)
                if with_dy:
                    dcre_ref[k] += _mxu_dot(h_r[:, cols], dy[:, lanes], _TN)
                    dncim_ref[k] += _mxu_dot(h_i[:, cols], dy[:, lanes], _TN)

    full = lambda a: pl.BlockSpec(a.shape, lambda i: (0,) * a.ndim)
    row = lambda cb: pl.BlockSpec((t, w), lambda i: (tile(i), cb))
    hspec = pl.BlockSpec((t, nsr, LANES), lambda i: (tile(i), 0, 0))
    if rev:
        pspec = pl.BlockSpec((1, nsr, LANES), lambda i: (jnp.minimum((tile(i) + 1) * t, l - 1), 0, 0))
    else:
        pspec = pl.BlockSpec((1, nsr, LANES), lambda i: (jnp.maximum(tile(i) * t - 1, 0), 0, 0))
    sm = jax.ShapeDtypeStruct((2, nsr, LANES), F32)
    smspec = pl.BlockSpec((2, nsr, LANES), lambda i: (0, 0, 0))
    wsh = jax.ShapeDtypeStruct((nb, LANES, spb), F32)
    csh = jax.ShapeDtypeStruct((nb, spb, LANES), F32)
    in_specs = (([row(0)] if with_dy else []) + [row(u_col), hspec, hspec, pspec, pspec, full(hbound), full(g_in)]
                + [_dir_spec(a, dr) for a in (a_sm, wbr_t, wbi_t) + ((cre_t, ncim_t) if with_dy else ())])
    args = ([dys] if with_dy else []) + [u_arr, hr, hi, hr, hi, hbound, g_in, a_sm, wbr_t, wbi_t] + ([cre_t, ncim_t] if with_dy else [])
    out_shape = (jax.ShapeDtypeStruct((l, w), F32), wsh, wsh) + ((csh, csh) if with_dy else ()) + (sm, sm)
    out_specs = (row(0), full(wsh), full(wsh)) + ((full(csh), full(csh)) if with_dy else ()) + (smspec, smspec)
    return pl.pallas_call(
        body, name=name, out_shape=out_shape, grid=(n,), in_specs=in_specs, out_specs=out_specs,
        scratch_shapes=[pltpu.VMEM((t, nsr, LANES), F32)] * 2 + [pltpu.VMEM((2, nsr, LANES), F32)],
        compiler_params=_cparams("arbitrary"))(*args)


def _glu_fwd(y0, y1, proj, cat, d_skip, w_glu, b_glu, w):
    l = y0.shape[0]
    tl = _tile(l, 256)

    def body(y0_ref, y1_ref, u_ref, z_ref, cat_in, d_ref, wg_ref, bg_ref, ys_ref, cat_ref):
        del cat_in
        ys = y0_ref[...] + y1_ref[...] + d_ref[...] * u_ref[...]
        ys_ref[...] = ys
        gy = _gelu(ys)
        s = _mxu_dot(gy, wg_ref[...]) + bg_ref[...]
        cat_ref[...] = (gy * jax.nn.sigmoid(s) * jax.nn.silu(z_ref[...])).astype(cat_ref.dtype)

    row = pl.BlockSpec((tl, w), lambda i: (i, 0))
    vec = pl.BlockSpec((1, w), lambda i: (0, 0))
    return pl.pallas_call(
        body, name="glu_fwd", out_shape=(jax.ShapeDtypeStruct((l, w), F32), jax.ShapeDtypeStruct(cat.shape, cat.dtype)),
        grid=(l // tl,),
        in_specs=[row, row, pl.BlockSpec((tl, w), lambda i: (i, 3)), pl.BlockSpec((tl, w), lambda i: (i, 4)), HBM,
                  vec, pl.BlockSpec((w, w), lambda i: (0, 0)), vec],
        out_specs=(row, pl.BlockSpec((tl, w), lambda i: (i, 1))), input_output_aliases={4: 1},
        compiler_params=_cparams("parallel"))(y0, y1, proj, proj, cat, d_skip, w_glu, b_glu)


def _glu_bwd(dcat, ys, proj, d_skip, w_glu, b_glu, w):
    l = ys.shape[0]
    tl = _tile(l, 256)

    def body(dy_ref, ys_ref, u_ref, z_ref, d_ref, wg_ref, bg_ref, dys_ref, dp_ref, dbg_ref, dd_ref, dwg_ref):
        i = pl.program_id(0)
        ys_t = ys_ref[...]
        z = z_ref[...]
        dyb = dy_ref[...].astype(F32)
        gy = _gelu(ys_t)
        sg = jax.nn.sigmoid(_mxu_dot(gy, wg_ref[...]) + bg_ref[...])
        dp_ref[...] = (dyb * gy * sg * _silu_grad(z)).astype(dp_ref.dtype)
        dglu = dyb * jax.nn.silu(z)
        ds = dglu * gy * sg * (1.0 - sg)
        dgy = dglu * sg + _mxu_dot(ds, wg_ref[...], _NT)
        dys_t = dgy * _gelu_grad(ys_t)
        dys_ref[...] = dys_t

        @pl.when(i == 0)
        def _():
            for r in (dbg_ref, dd_ref, dwg_ref):
                r[...] = jnp.zeros_like(r)

        dbg_ref[...] += jnp.sum(ds, axis=0, keepdims=True)
        dd_ref[...] += jnp.sum(dys_t * u_ref[...], axis=0, keepdims=True)
        dwg_ref[...] += _mxu_dot(gy, ds, _TN)

    row = pl.BlockSpec((tl, w), lambda i: (i, 0))
    vec = pl.BlockSpec((1, w), lambda i: (0, 0))
    mat = pl.BlockSpec((w, w), lambda i: (0, 0))
    v = jax.ShapeDtypeStruct((1, w), F32)
    return pl.pallas_call(
        body, name="glu_bwd",
        out_shape=(jax.ShapeDtypeStruct((l, w), F32), jax.ShapeDtypeStruct((l, 5 * w), MXU_DTYPE), v, v,
                   jax.ShapeDtypeStruct((w, w), F32)),
        grid=(l // tl,),
        in_specs=[pl.BlockSpec((tl, w), lambda i: (i, 1)), row, pl.BlockSpec((tl, w), lambda i: (i, 3)),
                  pl.BlockSpec((tl, w), lambda i: (i, 4)), vec, mat, vec],
        out_specs=(row, pl.BlockSpec((tl, w), lambda i: (i, 4)), vec, vec, mat),
        compiler_params=_cparams("arbitrary"))(dcat, ys, proj, proj, d_skip, w_glu, b_glu)


def _dub_combine(dys, du0, du1, d_skip, dproj, w):
    l = dys.shape[0]
    tl = _tile(l, 512)

    def body(dys_ref, a_ref, b_ref, d_ref, dp_in, dp_ref):
        del dp_in
        dp_ref[...] = (dys_ref[...] * d_ref[...] + a_ref[...] + b_ref[...]).astype(dp_ref.dtype)

    row = pl.BlockSpec((tl, w), lambda i: (i, 0))
    return pl.pallas_call(
        body, name="dub_combine", out_shape=jax.ShapeDtypeStruct(dproj.shape, dproj.dtype), grid=(l // tl,),
        in_specs=[row, row, row, pl.BlockSpec((1, w), lambda i: (0, 0)), HBM],
        out_specs=pl.BlockSpec((tl, w), lambda i: (i, 3)), input_output_aliases={4: 0},
        compiler_params=_cparams("parallel"))(dys, du0, du1, d_skip, dproj)


def _add2(a, b, name):
    l, w = a.shape
    tl = _tile(l, 512)

    def body(a_ref, b_ref, o_ref):
        o_ref[...] = a_ref[...] + b_ref[...]

    row = pl.BlockSpec((tl, w), lambda i: (i, 0))
    return pl.pallas_call(body, name=name, out_shape=jax.ShapeDtypeStruct((l, w), F32), grid=(l // tl,),
                          in_specs=[row, row], out_specs=row, compiler_params=_cparams("parallel"))(a, b)


def _adamw_nd(w, m, v, g, name):
    shape = w.shape
    lead = math.prod(shape[:-2]) if len(shape) > 2 else 1
    b, c = (shape[-2], shape[-1]) if len(shape) >= 2 else (1, shape[-1])
    t3 = (lead, b, c)
    padded_row = -(-b // 8) * 8 * -(-c // LANES) * LANES * 4
    ta = _tile(lead, max(1, (2 << 20) // padded_row))

    def body(w_ref, m_ref, v_ref, g_ref, d_ref, mo_ref, vo_ref):
        gv = g_ref[...]
        mn = ADAM_B1 * m_ref[...] + (1.0 - ADAM_B1) * gv
        vn = ADAM_B2 * v_ref[...] + (1.0 - ADAM_B2) * (gv * gv)
        m_hat = mn / (1.0 - ADAM_B1 ** ADAM_STEP)
        v_hat = vn / (1.0 - ADAM_B2 ** ADAM_STEP)
        d_ref[...] = -ADAM_LR * (m_hat / (jnp.sqrt(v_hat) + ADAM_EPS) + ADAM_WD * w_ref[...])
        mo_ref[...] = mn
        vo_ref[...] = vn

    blk = pl.BlockSpec((ta, b, c), lambda i: (i, 0, 0))
    s = jax.ShapeDtypeStruct(t3, F32)
    outs = pl.pallas_call(body, name=name, out_shape=(s, s, s), grid=(lead // ta,), in_specs=[blk] * 4, out_specs=(blk,) * 3,
                          compiler_params=_cparams("parallel"))(*[a.reshape(t3) for a in (w, m, v, g)])
    return tuple(o.reshape(shape) for o in outs)


def _adamw(w, m, v, gparts, name):
    r, c = w.shape
    np_ = gparts.shape[0]
    tr = _tile(r, max(8, (1 << 18) // c), 8)

    def body(w_ref, m_ref, v_ref, g_ref, go_ref, d_ref, mo_ref, vo_ref):
        g = g_ref[0].astype(F32)
        for p in range(1, np_):
            g = g + g_ref[p].astype(F32)
        mn = ADAM_B1 * m_ref[...] + (1.0 - ADAM_B1) * g
        vn = ADAM_B2 * v_ref[...] + (1.0 - ADAM_B2) * (g * g)
        m_hat = mn / (1.0 - ADAM_B1 ** ADAM_STEP)
        v_hat = vn / (1.0 - ADAM_B2 ** ADAM_STEP)
        go_ref[...] = g
        d_ref[...] = -ADAM_LR * (m_hat / (jnp.sqrt(v_hat) + ADAM_EPS) + ADAM_WD * w_ref[...])
        mo_ref[...] = mn
        vo_ref[...] = vn

    row = pl.BlockSpec((tr, c), lambda i: (i, 0))
    s = jax.ShapeDtypeStruct((r, c), F32)
    return pl.pallas_call(body, name=name, out_shape=(s, s, s, s), grid=(r // tr,),
                          in_specs=[row, row, row, pl.BlockSpec((np_, tr, c), lambda i: (0, i, 0))],
                          out_specs=(row, row, row, row), compiler_params=_cparams("parallel"))(w, m, v, gparts)


def _sum_slots(parts, name):
    np_, r, c = parts.shape

    def body(p_ref, o_ref):
        g = p_ref[0]
        for p in range(1, np_):
            g = g + p_ref[p]
        o_ref[...] = g

    return pl.pallas_call(body, name=name, out_shape=jax.ShapeDtypeStruct((r, c), F32), compiler_params=_cparams())(parts)


def _block_diag(x, gb):
    nd, g, a, b = x.shape
    eye = jnp.eye(gb, dtype=x.dtype)
    y = jnp.einsum("dkgab,gh->dkgahb", x.reshape(nd, g // gb, gb, a, b), eye)
    return y.reshape(nd, g // gb, gb * a, gb * b)


def _block_diag_extract(y, gb, a, b):
    nd, nbk = y.shape[:2]
    eye = jnp.eye(gb, dtype=y.dtype)
    x = jnp.einsum("dkgahb,gh->dkgab", y.reshape(nd, nbk, gb, a, gb, b), eye)
    return x.reshape(nd, nbk * gb, a, b)


def kernel(x, c, ctx, c_ctx, w_ada, b_ada, w_in, sgu_ln_g, sgu_ln_b, w_spatial, b_spatial, s5_lam_re, s5_lam_im, s5_log_step, s5_b_re, s5_b_im, s5_c_re, s5_c_im, s5_d, w_glu, b_glu, w_out, ln_g, ln_b, loss_target, m_c_ctx, m_w_ada, m_b_ada, m_w_in, m_sgu_ln_g, m_sgu_ln_b, m_w_spatial, m_b_spatial, m_s5_lam_re, m_s5_lam_im, m_s5_log_step, m_s5_b_re, m_s5_b_im, m_s5_c_re, m_s5_c_im, m_s5_d, m_w_glu, m_b_glu, m_w_out, m_ln_g, m_ln_b, v_c_ctx, v_w_ada, v_b_ada, v_w_in, v_sgu_ln_g, v_sgu_ln_b, v_w_spatial, v_b_spatial, v_s5_lam_re, v_s5_lam_im, v_s5_log_step, v_s5_b_re, v_s5_b_im, v_s5_c_re, v_s5_c_im, v_s5_d, v_w_glu, v_b_glu, v_w_out, v_ln_g, v_ln_b):
    small_names = ["c_ctx", "b_ada", "sgu_ln_g", "sgu_ln_b", "w_spatial", "b_spatial", "s5_lam_re", "s5_lam_im",
                   "s5_log_step", "s5_b_re", "s5_b_im", "s5_c_re", "s5_c_im", "s5_d", "b_glu", "ln_g", "ln_b"]
    env = dict(locals())
    x2, tgt, ctx2 = x[0], loss_target[0], ctx[0]
    l, d = x2.shape
    lc = ctx2.shape[0]
    w = d // 2
    nh = w // HEAD_DIM_A
    nd, g_s5, p_s5, c_s5 = s5_b_re.shape[1:]
    ns = g_s5 * p_s5
    nsr = ns // LANES
    gb = LANES // c_s5
    me = _index(_mesh_pos())
    ada_cols = w_ada.shape[2]

    hw_in, tok_a = _exchange_start(w_in[0].astype(MXU_DTYPE), 1, "gather", "start_gather_w_in")
    hw_out, tok_b = _exchange_start(w_out[0].astype(MXU_DTYPE), 0, "gather", "start_gather_w_out")
    hw_glu, tok_c = _exchange_start(w_glu[0].astype(MXU_DTYPE), 0, "gather", "start_gather_w_glu")
    srows = _silu_rows(c + (tok_a[0, 0] + tok_b[0, 0] + tok_c[0, 0]), c_ctx)
    srows_all = _all_gather(srows, 0, "gather_silu")
    s_mat = jnp.concatenate([srows_all[0::8], srows_all[1:2], jnp.zeros((7, d), F32)], axis=0)
    mod_part = _small_dot(s_mat, w_ada[0], "nn", "mod_cols")
    mod_all = _all_gather(mod_part, 1, "gather_mod") + b_ada
    mod_x = lax.dynamic_slice_in_dim(mod_all, me, 1, axis=0)
    mod_c = mod_all[8:9]
    shift_x, scale_x, gate_x = mod_x[:, :d], mod_x[:, d:2 * d], mod_x[:, 2 * d:]
    shift_c, scale_c = mod_c[:, :d], mod_c[:, d:2 * d]

    lr, li = s5_lam_re[0], s5_lam_im[0]
    ls = s5_log_step[0][..., None]
    br_t = jnp.transpose(s5_b_re[0], (0, 3, 1, 2))
    bi_t = jnp.transpose(s5_b_im[0], (0, 3, 1, 2))
    ab_re, ab_im, bb_re, bb_im = _disc_fwd(lr, li, ls, br_t, bi_t)
    a_sm = jnp.stack([ab_re, ab_im], axis=1).reshape(nd, 2, nsr, LANES)
    wbr = _block_diag(jnp.transpose(bb_re, (0, 2, 1, 3)), gb).astype(MXU_DTYPE)
    wbi = _block_diag(jnp.transpose(bb_im, (0, 2, 1, 3)), gb).astype(MXU_DTYPE)
    cre_t = _block_diag(s5_c_re[0], gb).astype(MXU_DTYPE)
    ncim_t = _block_diag(-s5_c_im[0], gb).astype(MXU_DTYPE)
    cre, ncim = jnp.swapaxes(cre_t, 2, 3), jnp.swapaxes(ncim_t, 2, 3)
    wbr_t, wbi_t = jnp.swapaxes(wbr, 2, 3), jnp.swapaxes(wbi, 2, 3)
    d_skip = s5_d

    xm = _ln_mod(x2, shift_x, scale_x, "ln_mod_x")
    cm = _ln_mod(ctx2, shift_c, scale_c, "ln_mod_ctx")
    ready = xm[:8, :LANES].astype(F32) + cm[:8, :LANES].astype(F32) + cre[0, 0, :8, :].astype(F32)
    w_in_f = _exchange_wait(hw_in, ready, "wait_gather_w_in")
    proj = _matmul(xm, w_in_f, mode="nn", name="proj", bm=512, bn=1024, bk=d)
    ub_c = _matmul(cm, w_in_f, mode="nn", name="proj_ctx", bm=256, bn=w, bk=d, b_n0=3 * w, n=w)
    bsf = jnp.repeat(b_spatial[0].T, HEAD_DIM_A, axis=1)
    ws = w_spatial[0]
    cat = _ga_fwd(proj, sgu_ln_g, sgu_ln_b, ws, bsf, w)
    zeros_state = jnp.zeros((2, nsr, LANES), F32)
    s5c, s5l = [], []
    for dr in range(nd):
        s5c.append(_s5_fwd(ub_c, 0, w, zeros_state, a_sm, wbr, wbi, cre, ncim, dr, f"s5_fwd_ctx{dr}"))
        s5l.append(_s5_fwd(proj, 3, w, s5c[dr][3], a_sm, wbr, wbi, cre, ncim, dr, f"s5_fwd{dr}"))
    w_glu_f = _exchange_wait(hw_glu, s5l[1][0], "wait_gather_w_glu")
    ys, cat = _glu_fwd(s5l[0][0], s5l[1][0], proj, cat, d_skip, w_glu_f, b_glu, w)
    w_out_f = _exchange_wait(hw_out, ys, "wait_gather_w_out")
    out = _matmul(cat, w_out_f, mode="nn", name="out_proj", bm=512, bn=1024, bk=2 * w)
    loss_row, dout, dx_res, dgate, dln_g, dln_b = _post_ln_loss(x2, out, gate_x, ln_g, ln_b, tgt)

    dcat = _matmul(dout, w_out_f, mode="nt", name="d_cat", bm=512, bn=1024, bk=d, out_dtype=MXU_DTYPE)
    dw_out = _matmul(cat, dout, mode="tn", name="d_w_out", bm=1024, bn=1024, bk=2048, out_dtype=MXU_DTYPE)
    hg_out, tok_d = _exchange_start(dw_out, 0, "a2a", "start_a2a_d_w_out")
    dys, dproj, db_glu, dd_skip, dw_glu = _glu_bwd(dcat, ys, proj, d_skip + tok_d[0, 0], w_glu_f, b_glu, w)
    hg_glu, tok_e = _exchange_start(dw_glu.astype(MXU_DTYPE), 0, "a2a", "start_a2a_d_w_glu")
    zeros_state = zeros_state + tok_e[0, 0]
    du_l, du_c, dwbr, dwbi, dcre, dncim, da_sm = [], [], [], [], [], [], []
    nbk, spb = w // LANES, gb * p_s5
    for dr in range(nd):
        bl = _s5_bwd(dys, proj, 3, w, s5l[dr][1], s5l[dr][2], s5c[dr][3], zeros_state, a_sm, wbr_t, wbi_t,
                     cre_t, ncim_t, dr, f"s5_bwd{dr}")
        bc = _s5_bwd(None, ub_c, 0, w, s5c[dr][1], s5c[dr][2], zeros_state, bl[6], a_sm, wbr_t, wbi_t,
                     None, None, dr, f"s5_bwd_ctx{dr}")
        du_l.append(bl[0])
        du_c.append(bc[0])
        dwbr.append(_add2(bl[1].reshape(nbk * LANES, spb), bc[1].reshape(nbk * LANES, spb), f"sum_dwbr{dr}"))
        dwbi.append(_add2(bl[2].reshape(nbk * LANES, spb), bc[2].reshape(nbk * LANES, spb), f"sum_dwbi{dr}"))
        dcre.append(bl[3])
        dncim.append(bl[4])
        da_sm.append(_add2(bl[5].reshape(2 * nsr, LANES), bc[3].reshape(2 * nsr, LANES), f"sum_da{dr}"))
    dproj = _dub_combine(dys, du_l[0], du_l[1], d_skip, dproj, w)
    dub_c = _add2(du_c[0], du_c[1], "dub_ctx")
    dwbr = jnp.stack(dwbr).reshape(nd, nbk, LANES, spb)
    dwbi = jnp.stack(dwbi).reshape(nd, nbk, LANES, spb)
    dcre, dncim = jnp.stack(dcre), jnp.stack(dncim)
    da_sm = jnp.stack(da_sm).reshape(nd, 2, g_s5, p_s5)
    dproj, dsg, dsb, dws, dbsf = _ga_bwd(proj, dcat, dproj, sgu_ln_g, sgu_ln_b, ws, bsf, w)
    dw_in = _matmul(xm, dproj, mode="tn", name="d_w_in", bm=1024, bn=1280, bk=2048, out_dtype=MXU_DTYPE)
    dw_in = _matmul(cm, dub_c, mode="tn", name="d_w_in_ctx", bm=1024, bn=w, bk=lc, acc_in=dw_in, acc_n0=3 * w,
                    out_dtype=MXU_DTYPE)
    hg_in, tok_f = _exchange_start(dw_in, 1, "a2a", "start_a2a_d_w_in")
    dxm = _matmul(dproj, w_in_f, mode="nt", name="d_xm", bm=512, bn=1024, bk=5 * w, dep=tok_f)
    dcm = _matmul(dub_c, w_in_f, mode="nt", name="d_cm", bm=256, bn=1024, bk=w, b_k0=3 * w, k=w)
    grad_x, dshift_x, dscale_x = _ln_mod_bwd(x2, dxm, scale_x, dx_res, "ln_mod_x_bwd")
    _, dshift_c, dscale_c = _ln_mod_bwd(ctx2, dcm, scale_c, None, "ln_mod_ctx_bwd")

    dbb_re = jnp.transpose(_block_diag_extract(dwbr, gb, c_s5, p_s5), (0, 2, 1, 3))
    dbb_im = jnp.transpose(_block_diag_extract(dwbi, gb, c_s5, p_s5), (0, 2, 1, 3))
    dc_re = jnp.swapaxes(_block_diag_extract(dcre, gb, p_s5, c_s5), 2, 3)
    dc_im = -jnp.swapaxes(_block_diag_extract(dncim, gb, p_s5, c_s5), 2, 3)
    dlr, dli, dls, dbr_t, dbi_t = _disc_bwd(lr, li, ls, br_t, bi_t, da_sm[:, 0], da_sm[:, 1], dbb_re, dbb_im)
    db_re = jnp.transpose(dbr_t, (0, 2, 3, 1))
    db_im = jnp.transpose(dbi_t, (0, 2, 3, 1))
    expand = (jnp.arange(w)[:, None] // HEAD_DIM_A == jnp.arange(LANES)[None, :]).astype(F32)
    db_sp = _lane_group_sum(dbsf, expand, "d_b_spatial")[:, :nh].T

    dmod_rows = jnp.concatenate([jnp.concatenate([dshift_x, dscale_x, dgate], axis=1),
                                 jnp.concatenate([dshift_c, dscale_c, jnp.zeros((1, d), F32)], axis=1),
                                 jnp.zeros((6, 3 * d), F32)], axis=0)
    dmod_all = _all_gather(dmod_rows, 0, "gather_dmod")
    dmod_ctx = _sum_slots(dmod_all[1::8].reshape(N_DEV, 1, 3 * d), "sum_dmod_ctx")
    dmod_mat = jnp.concatenate([dmod_all[0::8], dmod_ctx, jnp.zeros((7, 3 * d), F32)], axis=0)
    db_ada = _sum_slots(dmod_mat[:9].reshape(9, 1, 3 * d), "sum_db_ada")
    dmod_mine = lax.dynamic_slice_in_dim(dmod_mat, me * ada_cols, ada_cols, axis=1)
    dw_ada = _small_dot(s_mat, dmod_mine, "tn", "d_w_ada")
    dsilu_cc = _small_dot(dmod_mine[8:16], w_ada[0], "nt", "d_silu_cctx")[0:1]
    dc_ctx_part = dsilu_cc * _silu_grad(c_ctx.reshape(1, d))

    local = {"c_ctx": dc_ctx_part, "sgu_ln_g": dsg, "sgu_ln_b": dsb, "w_spatial": dws, "b_spatial": db_sp,
             "s5_lam_re": dlr, "s5_lam_im": dli, "s5_log_step": dls, "s5_b_re": db_re, "s5_b_im": db_im,
             "s5_c_re": dc_re, "s5_c_im": dc_im, "s5_d": dd_skip, "b_glu": db_glu, "ln_g": dln_g, "ln_b": dln_b}
    reduced = [n for n in small_names if n != "b_ada"]
    loss_part = (0.5 / d) * jnp.sum(loss_row)
    flat = jnp.concatenate([loss_part.reshape(1)] + [local[n].reshape(-1) for n in reduced])
    unit = N_DEV * 8 * LANES
    total = -(-flat.shape[0] // unit) * unit
    flat = jnp.pad(flat, (0, total - flat.shape[0])).reshape(N_DEV * 8, total // (N_DEV * 8))
    slots = _all_to_all(flat, 0, "a2a_small")
    mine = _sum_slots(slots, "sum_small")
    summed = _all_gather(mine, 0, "gather_small").reshape(-1)
    loss = summed[0]
    grads, off = {"b_ada": db_ada}, 1
    for n in reduced:
        size = math.prod(env[n].shape)
        grads[n] = summed[off:off + size].reshape(env[n].shape)
        off += size

    gp_w_out = _exchange_wait(hg_out, summed, "wait_a2a_d_w_out")
    gp_w_glu = _exchange_wait(hg_glu, summed, "wait_a2a_d_w_glu")
    gp_w_in = _exchange_wait(hg_in, summed, "wait_a2a_d_w_in")
    big = {
        "w_ada": _adamw(w_ada[0], m_w_ada[0], v_w_ada[0], dw_ada[None], "adamw_w_ada"),
        "w_in": _adamw(w_in[0], m_w_in[0], v_w_in[0], gp_w_in, "adamw_w_in"),
        "w_glu": _adamw(w_glu[0], m_w_glu[0], v_w_glu[0], gp_w_glu, "adamw_w_glu"),
        "w_out": _adamw(w_out[0], m_w_out[0], v_w_out[0], gp_w_out, "adamw_w_out"),
    }
    res = {n: tuple(a[None] for a in big[n]) for n in big}
    for n in small_names:
        res[n] = (grads[n],) + _adamw_nd(env[n], env["m_" + n], env["v_" + n], grads[n], "adamw_" + n)

    order = ["c_ctx", "w_ada", "b_ada", "w_in", "sgu_ln_g", "sgu_ln_b", "w_spatial", "b_spatial", "s5_lam_re", "s5_lam_im",
             "s5_log_step", "s5_b_re", "s5_b_im", "s5_c_re", "s5_c_im", "s5_d", "w_glu", "b_glu", "w_out", "ln_g", "ln_b"]
    return (loss, grad_x[None], *[res[n][0] for n in order], *[res[n][1] for n in order],
            *[res[n][2] for n in order], *[res[n][3] for n in order])
```

```python
import functools
import math

import jax
import jax.numpy as jnp
from jax import lax
from jax.experimental import pallas as pl
from jax.experimental.pallas import tpu as pltpu

F32 = jnp.float32
MXU_DTYPE = jnp.bfloat16
N_DEV = 8
MESH_ID = pl.DeviceIdType.MESH
LN_EPS = 1e-6
DEPTH = 1
ALPHA = (2.0 * DEPTH) ** 0.25
CHUNK = 128
HEAD_DIM_A = 128
ADAM_LR, ADAM_B1, ADAM_B2, ADAM_EPS, ADAM_WD, ADAM_STEP = 0.001, 0.9, 0.999, 1e-08, 0.01, 10
LANES = 128
VMEM_LIMIT = 56 * 1024 * 1024
HBM = pl.BlockSpec(memory_space=pl.ANY)


def _cparams(*sem):
    return pltpu.CompilerParams(dimension_semantics=sem if sem else None, vmem_limit_bytes=VMEM_LIMIT)


def _tile(n, pref, mult=1):
    if n <= pref:
        return n
    t = pref - pref % mult
    while n % t:
        t -= mult
    return t


def _gelu(x):
    return 0.5 * x * (1.0 + lax.erf(x * (1.0 / math.sqrt(2.0))))


def _gelu_grad(x):
    return 0.5 * (1.0 + lax.erf(x * (1.0 / math.sqrt(2.0)))) + x * jnp.exp(-0.5 * x * x) * (1.0 / math.sqrt(2.0 * math.pi))


def _silu_grad(x):
    s = jax.nn.sigmoid(x)
    return s * (1.0 + x * (1.0 - s))


def _mxu_dot(a, b, dims=(((1,), (0,)), ((), ()))):
    return lax.dot_general(a.astype(MXU_DTYPE), b.astype(MXU_DTYPE), dims, preferred_element_type=F32)


_NT = (((1,), (1,)), ((), ()))
_TN = (((0,), (0,)), ((), ()))


def _mesh_pos():
    return lax.axis_index("x"), lax.axis_index("y"), lax.axis_index("c")


def _peer(pos, r):
    x, y, c = pos
    return ((1 - x) if r & 4 else x, (1 - y) if r & 2 else y, (1 - c) if r & 1 else c)


def _index(pos):
    return 4 * pos[0] + 2 * pos[1] + pos[2]


def _slice_of(ref, axis, idx, size):
    start = idx * size
    if axis == 0:
        return ref.at[pl.ds(start, size)]
    return ref.at[:, pl.ds(start, size)]


def _all_gather(x, axis, name):
    size = x.shape[axis]
    out_shape = tuple(s * N_DEV if a == axis else s for a, s in enumerate(x.shape))

    def body(x_ref, o_ref, send_sems, recv_sems, local_sem):
        me = _mesh_pos()
        mine = pltpu.make_async_copy(x_ref, _slice_of(o_ref, axis, _index(me), size), local_sem)
        mine.start()

        def copy(r, block):
            return pltpu.make_async_remote_copy(
                src_ref=x_ref, dst_ref=_slice_of(o_ref, axis, _index(block), size),
                send_sem=send_sems.at[r - 1], recv_sem=recv_sems.at[r - 1],
                device_id=_peer(me, r), device_id_type=MESH_ID)

        sends = [copy(r, me) for r in range(1, N_DEV)]
        for cp in sends:
            cp.start()
        for r in range(1, N_DEV):
            copy(r, _peer(me, r)).wait_recv()
        for cp in sends:
            cp.wait_send()
        mine.wait()

    return pl.pallas_call(
        body, name=name, out_shape=jax.ShapeDtypeStruct(out_shape, x.dtype),
        in_specs=[HBM], out_specs=HBM,
        scratch_shapes=[pltpu.SemaphoreType.DMA((N_DEV - 1,)), pltpu.SemaphoreType.DMA((N_DEV - 1,)),
                        pltpu.SemaphoreType.DMA],
    )(x)


def _all_to_all(x, axis, name):
    size = x.shape[axis] // N_DEV
    slot = tuple(size if a == axis else s for a, s in enumerate(x.shape))

    def body(x_ref, o_ref, send_sems, recv_sems, local_sem):
        me = _mesh_pos()
        mine = pltpu.make_async_copy(_slice_of(x_ref, axis, _index(me), size), o_ref.at[_index(me)], local_sem)
        mine.start()

        def copy(r, sender, receiver):
            return pltpu.make_async_remote_copy(
                src_ref=_slice_of(x_ref, axis, _index(receiver), size), dst_ref=o_ref.at[_index(sender)],
                send_sem=send_sems.at[r - 1], recv_sem=recv_sems.at[r - 1],
                device_id=_peer(me, r), device_id_type=MESH_ID)

        sends = [copy(r, me, _peer(me, r)) for r in range(1, N_DEV)]
        for cp in sends:
            cp.start()
        for r in range(1, N_DEV):
            copy(r, _peer(me, r), me).wait_recv()
        for cp in sends:
            cp.wait_send()
        mine.wait()

    return pl.pallas_call(
        body, name=name, out_shape=jax.ShapeDtypeStruct((N_DEV,) + slot, x.dtype),
        in_specs=[HBM], out_specs=HBM,
        scratch_shapes=[pltpu.SemaphoreType.DMA((N_DEV - 1,)), pltpu.SemaphoreType.DMA((N_DEV - 1,)),
                        pltpu.SemaphoreType.DMA],
    )(x)


_SEM = pl.BlockSpec(memory_space=pltpu.SEMAPHORE)
_HBM = pl.BlockSpec(memory_space=pltpu.HBM)
_EFFECT = pltpu.SideEffectType.DATAFLOW_SIDE_EFFECTING
ALL_PEERS = tuple(range(1, N_DEV))
SIBLING = 1
SAME_CORE_PEERS = (2, 4, 6)


def _exchange_copy(kind, x_ref, land_ref, axis, size, send_sems, recv_sems, me, rels, q, arriving):
    peer = _peer(me, rels[q])
    sender, receiver = (peer, me) if arriving else (me, peer)
    if kind == "gather":
        src, dst = x_ref, _slice_of(land_ref, axis, _index(sender), size)
    else:
        src, dst = _slice_of(x_ref, axis, _index(receiver), size), land_ref.at[_index(sender)]
    return pltpu.make_async_remote_copy(src_ref=src, dst_ref=dst, send_sem=send_sems.at[q], recv_sem=recv_sems.at[q],
                                        device_id=peer, device_id_type=MESH_ID)


def _local_copy(kind, x_ref, land_ref, axis, size, me, local_sem):
    if kind == "gather":
        return pltpu.make_async_copy(x_ref, _slice_of(land_ref, axis, _index(me), size), local_sem)
    return pltpu.make_async_copy(_slice_of(x_ref, axis, _index(me), size), land_ref.at[_index(me)], local_sem)


def _exchange_start(x, axis, kind, name, rels=ALL_PEERS):
    size = x.shape[axis] if kind == "gather" else x.shape[axis] // N_DEV
    if kind == "gather":
        land_shape = tuple(s * N_DEV if a == axis else s for a, s in enumerate(x.shape))
    else:
        land_shape = (N_DEV,) + tuple(size if a == axis else s for a, s in enumerate(x.shape))

    def body(x_ref, land_ref, send_sems, recv_sems, local_sem, x_thru, land_thru, token):
        del x_thru, land_thru
        me = _mesh_pos()
        _local_copy(kind, x_ref, land_ref, axis, size, me, local_sem).start()
        for q in range(len(rels)):
            _exchange_copy(kind, x_ref, land_ref, axis, size, send_sems, recv_sems, me, rels, q, False).start()
        token[...] = jnp.zeros_like(token)

    sems = pltpu.SemaphoreType.DMA((len(rels),))
    send_sems, recv_sems, local_sem, x_thru, land_thru, token = pl.pallas_call(
        body, name=name,
        out_shape=(sems, sems, pltpu.SemaphoreType.DMA(()), pltpu.HBM(x.shape, x.dtype), pltpu.HBM(land_shape, x.dtype),
                   jax.ShapeDtypeStruct((8, LANES), F32)),
        in_specs=(_HBM, _HBM), out_specs=(_SEM, _SEM, _SEM, _HBM, _HBM, pl.BlockSpec(memory_space=pltpu.VMEM)),
        input_output_aliases={0: 3, 1: 4}, compiler_params=pltpu.CompilerParams(has_side_effects=_EFFECT),
    )(pltpu.with_memory_space_constraint(x, pltpu.HBM),
      pltpu.with_memory_space_constraint(lax.empty(land_shape, x.dtype), pltpu.HBM))
    return (kind, axis, size, rels, send_sems, recv_sems, local_sem, x_thru, land_thru), token


def _exchange_wait(handle, after, name):
    kind, axis, size, rels, send_sems, recv_sems, local_sem, x_thru, land_thru = handle

    def body(x_ref, land_ref, send_sems, recv_sems, local_sem, after_ref, x_dead, got_ref):
        del after_ref, x_dead, got_ref
        me = _mesh_pos()
        _local_copy(kind, x_ref, land_ref, axis, size, me, local_sem).wait()
        for q in range(len(rels)):
            _exchange_copy(kind, x_ref, land_ref, axis, size, send_sems, recv_sems, me, rels, q, False).wait_send()
        for q in range(len(rels)):
            _exchange_copy(kind, x_ref, land_ref, axis, size, send_sems, recv_sems, me, rels, q, True).wait_recv()

    return pl.pallas_call(
        body, name=name, out_shape=(pltpu.HBM(x_thru.shape, x_thru.dtype), pltpu.HBM(land_thru.shape, land_thru.dtype)),
        in_specs=(_HBM, _HBM, _SEM, _SEM, _SEM, HBM), out_specs=(_HBM, _HBM), input_output_aliases={0: 0, 1: 1},
        compiler_params=pltpu.CompilerParams(has_side_effects=_EFFECT),
    )(x_thru, land_thru, send_sems, recv_sems, local_sem, after)[1]


def _forward_copy(land_ref, axis, size, send_sems, recv_sems, me, q, arriving):
    sibling = _peer(me, SIBLING)
    owner = _peer(sibling if arriving else me, SAME_CORE_PEERS[q])
    block = _slice_of(land_ref, axis, _index(owner), size)
    return pltpu.make_async_remote_copy(src_ref=block, dst_ref=block, send_sem=send_sems.at[q], recv_sem=recv_sems.at[q],
                                        device_id=sibling, device_id_type=MESH_ID)


def _forward_start(land, axis, name):
    size = land.shape[axis] // N_DEV

    def body(land_ref, send_sems, recv_sems, land_thru, token):
        del land_thru
        me = _mesh_pos()
        for q in range(len(SAME_CORE_PEERS)):
            _forward_copy(land_ref, axis, size, send_sems, recv_sems, me, q, False).start()
        token[...] = jnp.zeros_like(token)

    sems = pltpu.SemaphoreType.DMA((len(SAME_CORE_PEERS),))
    send_sems, recv_sems, land_thru, token = pl.pallas_call(
        body, name=name, out_shape=(sems, sems, pltpu.HBM(land.shape, land.dtype), jax.ShapeDtypeStruct((8, LANES), F32)),
        in_specs=(_HBM,), out_specs=(_SEM, _SEM, _HBM, pl.BlockSpec(memory_space=pltpu.VMEM)),
        input_output_aliases={0: 2}, compiler_params=pltpu.CompilerParams(has_side_effects=_EFFECT),
    )(land)
    return (axis, size, send_sems, recv_sems, land_thru), token


def _forward_wait(handle, after, name):
    axis, size, send_sems, recv_sems, land_thru = handle

    def body(land_ref, send_sems, recv_sems, after_ref, got_ref):
        del after_ref, got_ref
        me = _mesh_pos()
        for q in range(len(SAME_CORE_PEERS)):
            _forward_copy(land_ref, axis, size, send_sems, recv_sems, me, q, False).wait_send()
        for q in range(len(SAME_CORE_PEERS)):
            _forward_copy(land_ref, axis, size, send_sems, recv_sems, me, q, True).wait_recv()

    return pl.pallas_call(
        body, name=name, out_shape=pltpu.HBM(land_thru.shape, land_thru.dtype),
        in_specs=(_HBM, _SEM, _SEM, HBM), out_specs=_HBM, input_output_aliases={0: 0},
        compiler_params=pltpu.CompilerParams(has_side_effects=_EFFECT),
    )(land_thru, send_sems, recv_sems, after)


def _matmul(a, b, *, mode, name, bm=512, bn=512, bk=512, out_dtype=F32, b_n0=0, n=None, b_k0=0, k=None,
            acc_in=None, acc_n0=0, dep=None):
    if mode == "tn":
        kk, m = a.shape
    else:
        m, kk = a.shape
    if mode == "nn":
        n = b.shape[1] if n is None else n
    elif mode == "nt":
        n = b.shape[0]
        kk = kk if k is None else k
    else:
        n = b.shape[1]
    bm, bn, bk = _tile(m, bm), _tile(n, bn), _tile(kk, bk)
    nk = kk // bk
    assert b_n0 % bn == 0 and b_k0 % bk == 0 and acc_n0 % bn == 0
    dims = {"nn": (((1,), (0,)), ((), ())), "nt": _NT, "tn": _TN}[mode]

    n_in = 2 + (acc_in is not None) + (dep is not None)

    def body(*refs):
        a_ref, b_ref = refs[:2]
        init = refs[2] if acc_in is not None else None
        o_ref = refs[n_in]
        acc_ref = refs[-1] if nk > 1 else None
        p = _mxu_dot(a_ref[...], b_ref[...], dims)
        if nk == 1:
            o_ref[...] = (p if init is None else p + init[...]).astype(out_dtype)
            return
        ki = pl.program_id(2)

        @pl.when(ki == 0)
        def _():
            acc_ref[...] = p if init is None else p + init[...]

        @pl.when(ki > 0)
        def _():
            acc_ref[...] += p

        @pl.when(ki == nk - 1)
        def _():
            o_ref[...] = acc_ref[...].astype(out_dtype)

    a_spec = pl.BlockSpec((bk, bm), lambda j, i, q: (q, i)) if mode == "tn" else pl.BlockSpec((bm, bk), lambda j, i, q: (i, q))
    if mode == "nt":
        b_spec = pl.BlockSpec((bn, bk), lambda j, i, q: (j, q + b_k0 // bk))
    else:
        b_spec = pl.BlockSpec((bk, bn), lambda j, i, q: (q, j + b_n0 // bn))
    in_specs, args, aliases = [a_spec, b_spec], [a, b], {}
    out_map = lambda j, i, q: (i, j + acc_n0 // bn)
    if acc_in is not None:
        in_specs.append(pl.BlockSpec((bm, bn), out_map))
        args.append(acc_in)
        aliases = {2: 0}
        out_shape = jax.ShapeDtypeStruct(acc_in.shape, out_dtype)
    else:
        out_shape = jax.ShapeDtypeStruct((m, n), out_dtype)
    if dep is not None:
        in_specs.append(HBM)
        args.append(dep)
    return pl.pallas_call(
        body, name=name, out_shape=out_shape, grid=(n // bn, m // bm, nk),
        in_specs=in_specs, out_specs=pl.BlockSpec((bm, bn), out_map),
        scratch_shapes=[pltpu.VMEM((bm, bn), F32)] if nk > 1 else [],
        input_output_aliases=aliases,
        compiler_params=_cparams("parallel", "parallel", "arbitrary"),
    )(*args)


def _silu_rows(c, c_ctx):
    d = c.shape[-1]

    def body(c_ref, cc_ref, o_ref):
        o_ref[...] = jnp.zeros_like(o_ref)
        o_ref[0:1, :] = jax.nn.silu(c_ref[...])
        o_ref[1:2, :] = jax.nn.silu(cc_ref[...])

    return pl.pallas_call(body, name="silu_rows", out_shape=jax.ShapeDtypeStruct((8, d), F32))(
        c.reshape(1, d), c_ctx.reshape(1, d))


def _small_dot(a, b, mode, name):
    dims = {"nn": (((1,), (0,)), ((), ())), "nt": _NT, "tn": _TN}[mode]
    m = a.shape[1] if mode == "tn" else a.shape[0]
    n = b.shape[0] if mode == "nt" else b.shape[1]

    def body(a_ref, b_ref, o_ref):
        o_ref[...] = lax.dot_general(a_ref[...], b_ref[...], dims, preferred_element_type=F32,
                                     precision=lax.Precision.HIGHEST)

    return pl.pallas_call(body, name=name, out_shape=jax.ShapeDtypeStruct((m, n), F32),
                          compiler_params=_cparams())(a, b)


def _ln_stats(x):
    mu = jnp.mean(x, axis=-1, keepdims=True)
    xc = x - mu
    var = jnp.mean(xc * xc, axis=-1, keepdims=True)
    rstd = lax.rsqrt(var + LN_EPS)
    return xc * rstd, rstd


def _ln_mod(x, shift, scale, name):
    l, d = x.shape
    tl = _tile(l, 256)

    def body(x_ref, sh_ref, sc_ref, o_ref):
        xhat, _ = _ln_stats(x_ref[...])
        o_ref[...] = (xhat * (1.0 + sc_ref[...]) + sh_ref[...]).astype(o_ref.dtype)

    row = pl.BlockSpec((tl, d), lambda i: (i, 0))
    vec = pl.BlockSpec((1, d), lambda i: (0, 0))
    return pl.pallas_call(body, name=name, out_shape=jax.ShapeDtypeStruct((l, d), MXU_DTYPE), grid=(l // tl,),
                          in_specs=[row, vec, vec], out_specs=row, compiler_params=_cparams("parallel"))(x, shift, scale)


def _ln_mod_bwd(x, dxm, scale, res, name):
    l, d = x.shape
    tl = _tile(l, 256)
    with_res = res is not None

    def body(*refs):
        if with_res:
            x_ref, g_ref, sc_ref, r_ref, dx_ref, dsh_ref, dsc_ref = refs
        else:
            x_ref, g_ref, sc_ref, dx_ref, dsh_ref, dsc_ref = refs
        i = pl.program_id(0)
        xhat, rstd = _ln_stats(x_ref[...])
        g = g_ref[...]
        dxh = g * (1.0 + sc_ref[...])
        dx = rstd * (dxh - jnp.mean(dxh, axis=-1, keepdims=True) - xhat * jnp.mean(dxh * xhat, axis=-1, keepdims=True))
        dx_ref[...] = dx + r_ref[...] if with_res else dx

        @pl.when(i == 0)
        def _():
            dsh_ref[...] = jnp.zeros_like(dsh_ref)
            dsc_ref[...] = jnp.zeros_like(dsc_ref)

        dsh_ref[...] += jnp.sum(g, axis=0, keepdims=True)
        dsc_ref[...] += jnp.sum(g * xhat, axis=0, keepdims=True)

    row = pl.BlockSpec((tl, d), lambda i: (i, 0))
    vec = pl.BlockSpec((1, d), lambda i: (0, 0))
    args = [x, dxm, scale] + ([res] if with_res else [])
    return pl.pallas_call(
        body, name=name,
        out_shape=(jax.ShapeDtypeStruct((l, d), F32), jax.ShapeDtypeStruct((1, d), F32), jax.ShapeDtypeStruct((1, d), F32)),
        grid=(l // tl,), in_specs=[row, row, vec] + ([row] if with_res else []), out_specs=(row, vec, vec),
        compiler_params=_cparams("arbitrary"))(*args)


def _post_ln_loss(x, out, gate, ln_g, ln_b, target):
    l, d = x.shape
    tl = _tile(l, 256)

    def body(x_ref, o_ref, gate_ref, g_ref, b_ref, t_ref, loss_ref, dout_ref, dxr_ref, dgate_ref, dg_ref, db_ref):
        i = pl.program_id(0)
        out_t = o_ref[...]
        gate_v = gate_ref[...]
        rhat, rstd = _ln_stats(ALPHA * x_ref[...] + gate_v * out_t)
        ln_gv = g_ref[...]
        diff = rhat * ln_gv + b_ref[...] - t_ref[...]
        dy = diff * (1.0 / d)
        drh = dy * ln_gv
        dr = rstd * (drh - jnp.mean(drh, axis=-1, keepdims=True) - rhat * jnp.mean(drh * rhat, axis=-1, keepdims=True))
        dout_ref[...] = (gate_v * dr).astype(dout_ref.dtype)
        dxr_ref[...] = ALPHA * dr

        @pl.when(i == 0)
        def _():
            for r in (loss_ref, dgate_ref, dg_ref, db_ref):
                r[...] = jnp.zeros_like(r)

        loss_ref[...] += jnp.sum(diff * diff, axis=0, keepdims=True)
        dgate_ref[...] += jnp.sum(dr * out_t, axis=0, keepdims=True)
        dg_ref[...] += jnp.sum(dy * rhat, axis=0, keepdims=True)
        db_ref[...] += jnp.sum(dy, axis=0, keepdims=True)

    row = pl.BlockSpec((tl, d), lambda i: (i, 0))
    vec = pl.BlockSpec((1, d), lambda i: (0, 0))
    v = jax.ShapeDtypeStruct((1, d), F32)
    return pl.pallas_call(
        body, name="post_ln_loss",
        out_shape=(v, jax.ShapeDtypeStruct((l, d), MXU_DTYPE), jax.ShapeDtypeStruct((l, d), F32), v, v, v),
        grid=(l // tl,), in_specs=[row, row, vec, vec, vec, row], out_specs=(vec, row, row, vec, vec, vec),
        compiler_params=_cparams("arbitrary"))(x, out, gate, ln_g, ln_b, target)


def _ga_forward_tile(p, g, b, ws_ref, bsf, w, nc, nh):
    u_raw, v_raw, za = p[:, :w], p[:, w:2 * w], p[:, 2 * w:3 * w]
    gu = _gelu(u_raw)
    vhat, rstd = _ln_stats(_gelu(v_raw))
    vn = vhat * g + b
    rows = []
    for ci in range(nc):
        r0 = ci * CHUNK
        heads = [_mxu_dot(ws_ref[h], vn[r0:r0 + CHUNK, h * HEAD_DIM_A:(h + 1) * HEAD_DIM_A]) for h in range(nh)]
        rows.append(jnp.concatenate(heads, axis=1) + bsf)
    mixed = jnp.concatenate(rows, axis=0) if nc > 1 else rows[0]
    return u_raw, v_raw, za, gu, vhat, rstd, vn, mixed


def _ga_fwd(proj, g, b, ws, bsf, w):
    l = proj.shape[0]
    nh = w // HEAD_DIM_A
    nc = _tile(l // CHUNK, 2)
    tl = nc * CHUNK

    def body(p_ref, g_ref, b_ref, ws_ref, bsf_ref, o_ref):
        _, _, za, gu, _, _, _, mixed = _ga_forward_tile(p_ref[...], g_ref[...], b_ref[...], ws_ref, bsf_ref[...], w, nc, nh)
        o_ref[...] = (gu * mixed * jax.nn.silu(za)).astype(o_ref.dtype)

    vec = pl.BlockSpec((1, w), lambda i: (0, 0))
    return pl.pallas_call(
        body, name="ga_fwd", out_shape=jax.ShapeDtypeStruct((l, 2 * w), MXU_DTYPE), grid=(l // tl,),
        in_specs=[pl.BlockSpec((tl, 3 * w), lambda i: (i, 0)), vec, vec,
                  pl.BlockSpec((nh, CHUNK, CHUNK), lambda i: (0, 0, 0)), pl.BlockSpec((CHUNK, w), lambda i: (0, 0))],
        out_specs=pl.BlockSpec((tl, w), lambda i: (i, 0)), compiler_params=_cparams("parallel"))(proj, g, b, ws, bsf)


def _ga_bwd(proj, dcat, dproj, g, b, ws, bsf, w):
    l = proj.shape[0]
    nh = w // HEAD_DIM_A
    nc = _tile(l // CHUNK, 2)
    tl = nc * CHUNK

    def body(p_ref, dy_ref, dp_in, g_ref, b_ref, ws_ref, bsf_ref, dp_ref, dg_ref, db_ref, dws_ref, dbsf_ref):
        del dp_in
        i = pl.program_id(0)
        gv = g_ref[...]
        u_raw, v_raw, za, gu, vhat, rstd, vn, mixed = _ga_forward_tile(
            p_ref[...], gv, b_ref[...], ws_ref, bsf_ref[...], w, nc, nh)
        dya = dy_ref[...].astype(F32)
        sz = jax.nn.silu(za)
        dmixed = dya * gu * sz
        dza = dya * gu * mixed * _silu_grad(za)
        dgu = dya * mixed * sz

        @pl.when(i == 0)
        def _():
            for r in (dg_ref, db_ref, dws_ref, dbsf_ref):
                r[...] = jnp.zeros_like(r)

        rows = []
        for ci in range(nc):
            r0 = ci * CHUNK
            heads = []
            for h in range(nh):
                cols = slice(h * HEAD_DIM_A, (h + 1) * HEAD_DIM_A)
                dm = dmixed[r0:r0 + CHUNK, cols]
                heads.append(_mxu_dot(ws_ref[h], dm, _TN))
                dws_ref[h] += _mxu_dot(dm, vn[r0:r0 + CHUNK, cols], _NT)
            rows.append(jnp.concatenate(heads, axis=1))
            dbsf_ref[...] += dmixed[r0:r0 + CHUNK, :]
        dvn = jnp.concatenate(rows, axis=0) if nc > 1 else rows[0]
        dg_ref[...] += jnp.sum(dvn * vhat, axis=0, keepdims=True)
        db_ref[...] += jnp.sum(dvn, axis=0, keepdims=True)
        dvh = dvn * gv
        dgv = rstd * (dvh - jnp.mean(dvh, axis=-1, keepdims=True) - vhat * jnp.mean(dvh * vhat, axis=-1, keepdims=True))
        dp_ref[:, :w] = (dgu * _gelu_grad(u_raw)).astype(dp_ref.dtype)
        dp_ref[:, w:2 * w] = (dgv * _gelu_grad(v_raw)).astype(dp_ref.dtype)
        dp_ref[:, 2 * w:] = dza.astype(dp_ref.dtype)

    vec = pl.BlockSpec((1, w), lambda i: (0, 0))
    ws_spec = pl.BlockSpec((nh, CHUNK, CHUNK), lambda i: (0, 0, 0))
    bs_spec = pl.BlockSpec((CHUNK, w), lambda i: (0, 0))
    v = jax.ShapeDtypeStruct((1, w), F32)
    return pl.pallas_call(
        body, name="ga_bwd",
        out_shape=(jax.ShapeDtypeStruct(dproj.shape, dproj.dtype), v, v, jax.ShapeDtypeStruct((nh, CHUNK, CHUNK), F32),
                   jax.ShapeDtypeStruct((CHUNK, w), F32)),
        grid=(l // tl,),
        in_specs=[pl.BlockSpec((tl, 3 * w), lambda i: (i, 0)), pl.BlockSpec((tl, w), lambda i: (i, 0)), HBM,
                  vec, vec, ws_spec, bs_spec],
        out_specs=(pl.BlockSpec((tl, 3 * w), lambda i: (i, 0)), vec, vec, ws_spec, bs_spec),
        input_output_aliases={2: 0}, compiler_params=_cparams("arbitrary"))(proj, dcat, dproj, g, b, ws, bsf)


def _lane_group_sum(x, expand, name):
    return _small_dot(x, expand, "nn", name)


def _disc_math(lr, li, ls, br, bi):
    step = jnp.exp(ls)
    dr, di = lr * step, li * step
    mag = jnp.exp(dr)
    ab_re, ab_im = mag * jnp.cos(di), mag * jnp.sin(di)
    den = lr * lr + li * li
    nr, ni = ab_re - 1.0, ab_im
    f_re = (nr * lr + ni * li) / den
    f_im = (ni * lr - nr * li) / den
    bb_re = f_re[:, None] * br - f_im[:, None] * bi
    bb_im = f_re[:, None] * bi + f_im[:, None] * br
    return ab_re, ab_im, bb_re, bb_im


def _disc_fwd(lr, li, ls, br, bi):
    def body(lr_ref, li_ref, ls_ref, br_ref, bi_ref, o1, o2, o3, o4):
        res = _disc_math(lr_ref[...], li_ref[...], ls_ref[...], br_ref[...], bi_ref[...])
        for o, r in zip((o1, o2, o3, o4), res):
            o[...] = r

    s = lambda a: jax.ShapeDtypeStruct(a.shape, F32)
    return pl.pallas_call(body, name="s5_disc", out_shape=(s(lr), s(lr), s(br), s(br)), compiler_params=_cparams())(
        lr, li, ls, br, bi)


def _disc_bwd(lr, li, ls, br, bi, d_ar, d_ai, d_br, d_bi):
    def body(lr_ref, li_ref, ls_ref, br_ref, bi_ref, c1, c2, c3, c4, o1, o2, o3, o4, o5):
        _, vjp = jax.vjp(_disc_math, lr_ref[...], li_ref[...], ls_ref[...], br_ref[...], bi_ref[...])
        res = vjp((c1[...], c2[...], c3[...], c4[...]))
        for o, r in zip((o1, o2, o3, o4, o5), res):
            o[...] = r

    s = lambda a: jax.ShapeDtypeStruct(a.shape, F32)
    return pl.pallas_call(body, name="s5_disc_bwd", out_shape=(s(lr), s(lr), s(ls), s(br), s(br)),
                          compiler_params=_cparams())(lr, li, ls, br, bi, d_ar, d_ai, d_br, d_bi)


def _dir_spec(a, dr):
    return pl.BlockSpec((None,) + a.shape[1:], lambda i: (dr,) + (0,) * (a.ndim - 1))


def _s5_fwd(u_arr, u_col, w, h0, a_sm, wbr, wbi, cre, ncim, dr, name):
    rev = dr == 1
    l = u_arr.shape[0]
    nb = w // LANES
    spb = wbr.shape[-1]
    nsr = a_sm.shape[2]
    assert 2 * spb == 8 * LANES and nb % 2 == 0
    t = _tile(l, 256)
    n = l // t
    tile = (lambda i: n - 1 - i) if rev else (lambda i: i)

    def body(u_ref, h0_ref, a_ref, wbr_ref, wbi_ref, cre_ref, ncim_ref, y_ref, hr_ref, hi_ref, tr_ref, ti_ref, hfin_ref,
             carry_ref):
        i = pl.program_id(0)

        @pl.when(i == 0)
        def _():
            carry_ref[...] = h0_ref[...]

        for j in range(nb // 2):
            for h_ref, w_ref in ((hr_ref, wbr_ref), (hi_ref, wbi_ref)):
                blk = [_mxu_dot(u_ref[:, k * LANES:(k + 1) * LANES], w_ref[k]) for k in (2 * j, 2 * j + 1)]
                h_ref[:, 8 * j:8 * j + 8, :] = jnp.concatenate(blk, axis=1).reshape(t, 8, LANES)
        ar, ai = a_ref[0], a_ref[1]

        def step(s, c):
            hr, hi = c
            row = t - 1 - s if rev else s
            nr = ar * hr - ai * hi + hr_ref[row]
            ni = ar * hi + ai * hr + hi_ref[row]
            hr_ref[row] = nr
            hi_ref[row] = ni
            return nr, ni

        c = lax.fori_loop(0, t, step, (carry_ref[0], carry_ref[1]), unroll=4)
        for q in range(2):
            carry_ref[q] = c[q]
            hfin_ref[q] = c[q]
        for j in range(nb // 2):
            cols8 = slice(j * 8 * LANES, (j + 1) * 8 * LANES)
            tr_ref[:, cols8] = hr_ref[:, 8 * j:8 * j + 8, :].reshape(t, 8 * LANES).astype(tr_ref.dtype)
            ti_ref[:, cols8] = hi_ref[:, 8 * j:8 * j + 8, :].reshape(t, 8 * LANES).astype(ti_ref.dtype)
        for k in range(nb):
            cols = slice(k * spb, (k + 1) * spb)
            y_ref[:, k * LANES:(k + 1) * LANES] = (_mxu_dot(tr_ref[:, cols], cre_ref[k]) + _mxu_dot(ti_ref[:, cols], ncim_ref[k]))

    full = lambda a: pl.BlockSpec(a.shape, lambda i: (0,) * a.ndim)
    hspec = pl.BlockSpec((t, nsr, LANES), lambda i: (tile(i), 0, 0))
    tspec = pl.BlockSpec((t, nsr * LANES), lambda i: (tile(i), 0))
    hsh = jax.ShapeDtypeStruct((l, nsr, LANES), F32)
    tsh = jax.ShapeDtypeStruct((l, nsr * LANES), MXU_DTYPE)
    return pl.pallas_call(
        body, name=name,
        out_shape=(jax.ShapeDtypeStruct((l, w), F32), hsh, hsh, tsh, tsh, jax.ShapeDtypeStruct((2, nsr, LANES), F32)),
        grid=(n,),
        in_specs=[pl.BlockSpec((t, w), lambda i: (tile(i), u_col)), full(h0)] + [_dir_spec(a, dr) for a in (a_sm, wbr, wbi, cre, ncim)],
        out_specs=(pl.BlockSpec((t, w), lambda i: (tile(i), 0)), hspec, hspec, tspec, tspec,
                   pl.BlockSpec((2, nsr, LANES), lambda i: (0, 0, 0))),
        scratch_shapes=[pltpu.VMEM((2, nsr, LANES), F32)],
        compiler_params=_cparams("arbitrary"))(u_arr, h0, a_sm, wbr, wbi, cre, ncim)


def _s5_bwd(dys, u_arr, u_col, w, hr, hi, tr, ti, hbound, g_in, a_sm, wbr_t, wbi_t, cre_t, ncim_t, dr, name):
    rev = dr == 1
    l = u_arr.shape[0]
    nb = w // LANES
    spb = wbr_t.shape[-2]
    nsr = a_sm.shape[2]
    t = _tile(l, 128)
    n = l // t
    with_dy = dys is not None
    tile = (lambda i: i) if rev else (lambda i: n - 1 - i)

    def body(*refs):
        if with_dy:
            (dy_ref, u_ref, hr_ref, hi_ref, pr_ref, pi_ref, tr_ref, ti_ref, hb_ref, gin_ref, a_ref, wbrt_ref, wbit_ref,
             cret_ref, ncimt_ref, du_ref, dwbr_ref, dwbi_ref, dcre_ref, dncim_ref, da_ref, gout_ref,
             gr_ref, gi_ref, gtr_ref, gti_ref, carry_ref) = refs
        else:
            (u_ref, hr_ref, hi_ref, pr_ref, pi_ref, hb_ref, gin_ref, a_ref, wbrt_ref, wbit_ref,
             du_ref, dwbr_ref, dwbi_ref, da_ref, gout_ref, gr_ref, gi_ref, gtr_ref, gti_ref, carry_ref) = refs
        i = pl.program_id(0)

        @pl.when(i == 0)
        def _():
            carry_ref[...] = gin_ref[...]
            accs = (dwbr_ref, dwbi_ref, da_ref) + ((dcre_ref, dncim_ref) if with_dy else ())
            for r in accs:
                r[...] = jnp.zeros_like(r)

        if with_dy:
            for j in range(nb // 2):
                for g_ref, c_ref in ((gr_ref, cret_ref), (gi_ref, ncimt_ref)):
                    blk = [_mxu_dot(dy_ref[:, k * LANES:(k + 1) * LANES], c_ref[k]) for k in (2 * j, 2 * j + 1)]
                    g_ref[:, 8 * j:8 * j + 8, :] = jnp.concatenate(blk, axis=1).reshape(t, 8, LANES)
        else:
            gr_ref[...] = jnp.zeros_like(gr_ref)
            gi_ref[...] = jnp.zeros_like(gi_ref)
        ar, ai = a_ref[0], a_ref[1]

        def one(row, gr, gi, pr, pi, dr, di):
            nr = gr_ref[row] + ar * gr + ai * gi
            ni = gi_ref[row] + ar * gi - ai * gr
            gr_ref[row] = nr
            gi_ref[row] = ni
            return nr, ni, dr + nr * pr + ni * pi, di + ni * pr - nr * pi

        def step(s, c):
            gr, gi, dr, di = c
            row = s if rev else t - 1 - s
            prow = row + 1 if rev else row - 1
            return one(row, gr, gi, hr_ref[prow], hi_ref[prow], dr, di)

        gr, gi, dr, di = lax.fori_loop(0, t - 1, step, (carry_ref[0], carry_ref[1], da_ref[0], da_ref[1]), unroll=4)
        first = i == n - 1
        pr = jnp.where(first, hb_ref[0], pr_ref[0])
        pi = jnp.where(first, hb_ref[1], pi_ref[0])
        gr, gi, dr, di = one(t - 1 if rev else 0, gr, gi, pr, pi, dr, di)
        da_ref[0] = dr
        da_ref[1] = di
        for q, val in enumerate((gr, gi)):
            carry_ref[q] = val
            gout_ref[q] = val

        for j in range(nb // 2):
            cols8 = slice(j * 8 * LANES, (j + 1) * 8 * LANES)
            gtr_ref[:, cols8] = gr_ref[:, 8 * j:8 * j + 8, :].reshape(t, 8 * LANES).astype(gtr_ref.dtype)
            gti_ref[:, cols8] = gi_ref[:, 8 * j:8 * j + 8, :].reshape(t, 8 * LANES).astype(gti_ref.dtype)
        for k in range(nb):
            cols = slice(k * spb, (k + 1) * spb)
            lanes = slice(k * LANES, (k + 1) * LANES)
            du_ref[:, lanes] = _mxu_dot(gtr_ref[:, cols], wbrt_ref[k]) + _mxu_dot(gti_ref[:, cols], wbit_ref[k])
            dwbr_ref[k] += _mxu_dot(u_ref[:, lanes], gtr_ref[:, cols], _TN)
            dwbi_ref[k] += _mxu_dot(u_ref[:, lanes], gti_ref[:, cols], _TN)
            if with_dy:
                dcre_ref[k] += _mxu_dot(tr_ref[:, cols], dy_ref[:, lanes], _TN)
                dncim_ref[k] += _mxu_dot(ti_ref[:, cols], dy_ref[:, lanes], _TN)

    full = lambda a: pl.BlockSpec(a.shape, lambda i: (0,) * a.ndim)
    row = lambda cb: pl.BlockSpec((t, w), lambda i: (tile(i), cb))
    hspec = pl.BlockSpec((t, nsr, LANES), lambda i: (tile(i), 0, 0))
    if rev:
        pspec = pl.BlockSpec((1, nsr, LANES), lambda i: (jnp.minimum((tile(i) + 1) * t, l - 1), 0, 0))
    else:
        pspec = pl.BlockSpec((1, nsr, LANES), lambda i: (jnp.maximum(tile(i) * t - 1, 0), 0, 0))
    sm = jax.ShapeDtypeStruct((2, nsr, LANES), F32)
    smspec = pl.BlockSpec((2, nsr, LANES), lambda i: (0, 0, 0))
    wsh = jax.ShapeDtypeStruct((nb, LANES, spb), F32)
    csh = jax.ShapeDtypeStruct((nb, spb, LANES), F32)
    tspec = pl.BlockSpec((t, nsr * LANES), lambda i: (tile(i), 0))
    in_specs = (([row(0)] if with_dy else []) + [row(u_col), hspec, hspec, pspec, pspec] + ([tspec, tspec] if with_dy else [])
                + [full(hbound), full(g_in)]
                + [_dir_spec(a, dr) for a in (a_sm, wbr_t, wbi_t) + ((cre_t, ncim_t) if with_dy else ())])
    args = (([dys] if with_dy else []) + [u_arr, hr, hi, hr, hi] + ([tr, ti] if with_dy else [])
            + [hbound, g_in, a_sm, wbr_t, wbi_t] + ([cre_t, ncim_t] if with_dy else []))
    out_shape = (jax.ShapeDtypeStruct((l, w), F32), wsh, wsh) + ((csh, csh) if with_dy else ()) + (sm, sm)
    out_specs = (row(0), full(wsh), full(wsh)) + ((full(csh), full(csh)) if with_dy else ()) + (smspec, smspec)
    return pl.pallas_call(
        body, name=name, out_shape=out_shape, grid=(n,), in_specs=in_specs, out_specs=out_specs,
        scratch_shapes=[pltpu.VMEM((t, nsr, LANES), F32)] * 2 + [pltpu.VMEM((t, nsr * LANES), MXU_DTYPE)] * 2
        + [pltpu.VMEM((2, nsr, LANES), F32)],
        compiler_params=_cparams("arbitrary"))(*args)


def _glu_fwd(y0, y1, proj, cat, d_skip, w_glu, b_glu, w):
    l = y0.shape[0]
    tl = _tile(l, 256)

    def body(y0_ref, y1_ref, u_ref, z_ref, cat_in, d_ref, wg_ref, bg_ref, ys_ref, cat_ref):
        del cat_in
        ys = y0_ref[...] + y1_ref[...] + d_ref[...] * u_ref[...]
        ys_ref[...] = ys
        gy = _gelu(ys)
        s = _mxu_dot(gy, wg_ref[...]) + bg_ref[...]
        cat_ref[...] = (gy * jax.nn.sigmoid(s) * jax.nn.silu(z_ref[...])).astype(cat_ref.dtype)

    row = pl.BlockSpec((tl, w), lambda i: (i, 0))
    vec = pl.BlockSpec((1, w), lambda i: (0, 0))
    return pl.pallas_call(
        body, name="glu_fwd", out_shape=(jax.ShapeDtypeStruct((l, w), F32), jax.ShapeDtypeStruct(cat.shape, cat.dtype)),
        grid=(l // tl,),
        in_specs=[row, row, pl.BlockSpec((tl, w), lambda i: (i, 3)), pl.BlockSpec((tl, w), lambda i: (i, 4)), HBM,
                  vec, pl.BlockSpec((w, w), lambda i: (0, 0)), vec],
        out_specs=(row, pl.BlockSpec((tl, w), lambda i: (i, 1))), input_output_aliases={4: 1},
        compiler_params=_cparams("parallel"))(y0, y1, proj, proj, cat, d_skip, w_glu, b_glu)


def _glu_bwd(dcat, ys, proj, d_skip, w_glu, b_glu, w):
    l = ys.shape[0]
    tl = _tile(l, 256)

    def body(dy_ref, ys_ref, u_ref, z_ref, d_ref, wg_ref, bg_ref, dys_ref, dp_ref, dbg_ref, dd_ref, dwg_ref):
        i = pl.program_id(0)
        ys_t = ys_ref[...]
        z = z_ref[...]
        dyb = dy_ref[...].astype(F32)
        gy = _gelu(ys_t)
        sg = jax.nn.sigmoid(_mxu_dot(gy, wg_ref[...]) + bg_ref[...])
        dp_ref[...] = (dyb * gy * sg * _silu_grad(z)).astype(dp_ref.dtype)
        dglu = dyb * jax.nn.silu(z)
        ds = dglu * gy * sg * (1.0 - sg)
        dgy = dglu * sg + _mxu_dot(ds, wg_ref[...], _NT)
        dys_t = dgy * _gelu_grad(ys_t)
        dys_ref[...] = dys_t

        @pl.when(i == 0)
        def _():
            for r in (dbg_ref, dd_ref, dwg_ref):
                r[...] = jnp.zeros_like(r)

        dbg_ref[...] += jnp.sum(ds, axis=0, keepdims=True)
        dd_ref[...] += jnp.sum(dys_t * u_ref[...], axis=0, keepdims=True)
        dwg_ref[...] += _mxu_dot(gy, ds, _TN)

    row = pl.BlockSpec((tl, w), lambda i: (i, 0))
    vec = pl.BlockSpec((1, w), lambda i: (0, 0))
    mat = pl.BlockSpec((w, w), lambda i: (0, 0))
    v = jax.ShapeDtypeStruct((1, w), F32)
    return pl.pallas_call(
        body, name="glu_bwd",
        out_shape=(jax.ShapeDtypeStruct((l, w), F32), jax.ShapeDtypeStruct((l, 5 * w), MXU_DTYPE), v, v,
                   jax.ShapeDtypeStruct((w, w), F32)),
        grid=(l // tl,),
        in_specs=[pl.BlockSpec((tl, w), lambda i: (i, 1)), row, pl.BlockSpec((tl, w), lambda i: (i, 3)),
                  pl.BlockSpec((tl, w), lambda i: (i, 4)), vec, mat, vec],
        out_specs=(row, pl.BlockSpec((tl, w), lambda i: (i, 4)), vec, vec, mat),
        compiler_params=_cparams("arbitrary"))(dcat, ys, proj, proj, d_skip, w_glu, b_glu)


def _dub_combine(dys, du0, du1, d_skip, dproj, w):
    l = dys.shape[0]
    tl = _tile(l, 512)

    def body(dys_ref, a_ref, b_ref, d_ref, dp_in, dp_ref):
        del dp_in
        dp_ref[...] = (dys_ref[...] * d_ref[...] + a_ref[...] + b_ref[...]).astype(dp_ref.dtype)

    row = pl.BlockSpec((tl, w), lambda i: (i, 0))
    return pl.pallas_call(
        body, name="dub_combine", out_shape=jax.ShapeDtypeStruct(dproj.shape, dproj.dtype), grid=(l // tl,),
        in_specs=[row, row, row, pl.BlockSpec((1, w), lambda i: (0, 0)), HBM],
        out_specs=pl.BlockSpec((tl, w), lambda i: (i, 3)), input_output_aliases={4: 0},
        compiler_params=_cparams("parallel"))(dys, du0, du1, d_skip, dproj)


def _add2(a, b, name):
    l, w = a.shape
    tl = _tile(l, 512)

    def body(a_ref, b_ref, o_ref):
        o_ref[...] = a_ref[...] + b_ref[...]

    row = pl.BlockSpec((tl, w), lambda i: (i, 0))
    return pl.pallas_call(body, name=name, out_shape=jax.ShapeDtypeStruct((l, w), F32), grid=(l // tl,),
                          in_specs=[row, row], out_specs=row, compiler_params=_cparams("parallel"))(a, b)


def _adamw_nd(w, m, v, g, name):
    shape = w.shape
    lead = math.prod(shape[:-2]) if len(shape) > 2 else 1
    b, c = (shape[-2], shape[-1]) if len(shape) >= 2 else (1, shape[-1])
    t3 = (lead, b, c)
    padded_row = -(-b // 8) * 8 * -(-c // LANES) * LANES * 4
    ta = _tile(lead, max(1, (2 << 20) // padded_row))

    def body(w_ref, m_ref, v_ref, g_ref, d_ref, mo_ref, vo_ref):
        gv = g_ref[...]
        mn = ADAM_B1 * m_ref[...] + (1.0 - ADAM_B1) * gv
        vn = ADAM_B2 * v_ref[...] + (1.0 - ADAM_B2) * (gv * gv)
        m_hat = mn / (1.0 - ADAM_B1 ** ADAM_STEP)
        v_hat = vn / (1.0 - ADAM_B2 ** ADAM_STEP)
        d_ref[...] = -ADAM_LR * (m_hat / (jnp.sqrt(v_hat) + ADAM_EPS) + ADAM_WD * w_ref[...])
        mo_ref[...] = mn
        vo_ref[...] = vn

    blk = pl.BlockSpec((ta, b, c), lambda i: (i, 0, 0))
    s = jax.ShapeDtypeStruct(t3, F32)
    outs = pl.pallas_call(body, name=name, out_shape=(s, s, s), grid=(lead // ta,), in_specs=[blk] * 4, out_specs=(blk,) * 3,
                          compiler_params=_cparams("parallel"))(*[a.reshape(t3) for a in (w, m, v, g)])
    return tuple(o.reshape(shape) for o in outs)


def _adamw(w, m, v, gparts, name):
    r, c = w.shape
    np_ = gparts.shape[0]
    tr = _tile(r, max(8, (1 << 18) // c), 8)

    def body(w_ref, m_ref, v_ref, g_ref, go_ref, d_ref, mo_ref, vo_ref):
        g = g_ref[0].astype(F32)
        for p in range(1, np_):
            g = g + g_ref[p].astype(F32)
        mn = ADAM_B1 * m_ref[...] + (1.0 - ADAM_B1) * g
        vn = ADAM_B2 * v_ref[...] + (1.0 - ADAM_B2) * (g * g)
        m_hat = mn / (1.0 - ADAM_B1 ** ADAM_STEP)
        v_hat = vn / (1.0 - ADAM_B2 ** ADAM_STEP)
        go_ref[...] = g
        d_ref[...] = -ADAM_LR * (m_hat / (jnp.sqrt(v_hat) + ADAM_EPS) + ADAM_WD * w_ref[...])
        mo_ref[...] = mn
        vo_ref[...] = vn

    row = pl.BlockSpec((tr, c), lambda i: (i, 0))
    s = jax.ShapeDtypeStruct((r, c), F32)
    return pl.pallas_call(body, name=name, out_shape=(s, s, s, s), grid=(r // tr,),
                          in_specs=[row, row, row, pl.BlockSpec((np_, tr, c), lambda i: (0, i, 0))],
                          out_specs=(row, row, row, row), compiler_params=_cparams("parallel"))(w, m, v, gparts)


def _sum_slots(parts, name):
    np_, r, c = parts.shape

    def body(p_ref, o_ref):
        g = p_ref[0]
        for p in range(1, np_):
            g = g + p_ref[p]
        o_ref[...] = g

    return pl.pallas_call(body, name=name, out_shape=jax.ShapeDtypeStruct((r, c), F32), compiler_params=_cparams())(parts)


def _block_diag(x, gb):
    nd, g, a, b = x.shape
    eye = jnp.eye(gb, dtype=x.dtype)
    y = jnp.einsum("dkgab,gh->dkgahb", x.reshape(nd, g // gb, gb, a, b), eye)
    return y.reshape(nd, g // gb, gb * a, gb * b)


def _block_diag_extract(y, gb, a, b):
    nd, nbk = y.shape[:2]
    eye = jnp.eye(gb, dtype=y.dtype)
    x = jnp.einsum("dkgahb,gh->dkgab", y.reshape(nd, nbk, gb, a, gb, b), eye)
    return x.reshape(nd, nbk * gb, a, b)


def kernel(x, c, ctx, c_ctx, w_ada, b_ada, w_in, sgu_ln_g, sgu_ln_b, w_spatial, b_spatial, s5_lam_re, s5_lam_im, s5_log_step, s5_b_re, s5_b_im, s5_c_re, s5_c_im, s5_d, w_glu, b_glu, w_out, ln_g, ln_b, loss_target, m_c_ctx, m_w_ada, m_b_ada, m_w_in, m_sgu_ln_g, m_sgu_ln_b, m_w_spatial, m_b_spatial, m_s5_lam_re, m_s5_lam_im, m_s5_log_step, m_s5_b_re, m_s5_b_im, m_s5_c_re, m_s5_c_im, m_s5_d, m_w_glu, m_b_glu, m_w_out, m_ln_g, m_ln_b, v_c_ctx, v_w_ada, v_b_ada, v_w_in, v_sgu_ln_g, v_sgu_ln_b, v_w_spatial, v_b_spatial, v_s5_lam_re, v_s5_lam_im, v_s5_log_step, v_s5_b_re, v_s5_b_im, v_s5_c_re, v_s5_c_im, v_s5_d, v_w_glu, v_b_glu, v_w_out, v_ln_g, v_ln_b):
    small_names = ["c_ctx", "b_ada", "sgu_ln_g", "sgu_ln_b", "w_spatial", "b_spatial", "s5_lam_re", "s5_lam_im",
                   "s5_log_step", "s5_b_re", "s5_b_im", "s5_c_re", "s5_c_im", "s5_d", "b_glu", "ln_g", "ln_b"]
    env = dict(locals())
    x2, tgt, ctx2 = x[0], loss_target[0], ctx[0]
    l, d = x2.shape
    lc = ctx2.shape[0]
    w = d // 2
    nh = w // HEAD_DIM_A
    nd, g_s5, p_s5, c_s5 = s5_b_re.shape[1:]
    ns = g_s5 * p_s5
    nsr = ns // LANES
    gb = LANES // c_s5
    me = _index(_mesh_pos())
    ada_cols = w_ada.shape[2]

    srows = _silu_rows(c, c_ctx)
    srows_all = _all_gather(srows, 0, "gather_silu")
    s_mat = jnp.concatenate([srows_all[0::8], srows_all[1:2], jnp.zeros((7, d), F32)], axis=0)
    mod_part = _small_dot(s_mat, w_ada[0], "nn", "mod_cols")
    mod_all = _all_gather(mod_part, 1, "gather_mod") + b_ada
    hw_in, tok_a = _exchange_start(w_in[0].astype(MXU_DTYPE), 1, "gather", "start_gather_w_in", (SIBLING,) + SAME_CORE_PEERS)
    mod_all = mod_all + tok_a[0, 0]
    mod_x = lax.dynamic_slice_in_dim(mod_all, me, 1, axis=0)
    mod_c = mod_all[8:9]
    shift_x, scale_x, gate_x = mod_x[:, :d], mod_x[:, d:2 * d], mod_x[:, 2 * d:]
    shift_c, scale_c = mod_c[:, :d], mod_c[:, d:2 * d]

    lr, li = s5_lam_re[0], s5_lam_im[0]
    ls = s5_log_step[0][..., None]
    br_t = jnp.transpose(s5_b_re[0], (0, 3, 1, 2))
    bi_t = jnp.transpose(s5_b_im[0], (0, 3, 1, 2))
    ab_re, ab_im, bb_re, bb_im = _disc_fwd(lr, li, ls, br_t, bi_t)
    a_sm = jnp.stack([ab_re, ab_im], axis=1).reshape(nd, 2, nsr, LANES)
    wbr = _block_diag(jnp.transpose(bb_re, (0, 2, 1, 3)), gb).astype(MXU_DTYPE)
    wbi = _block_diag(jnp.transpose(bb_im, (0, 2, 1, 3)), gb).astype(MXU_DTYPE)
    cre_t = _block_diag(s5_c_re[0], gb).astype(MXU_DTYPE)
    ncim_t = _block_diag(-s5_c_im[0], gb).astype(MXU_DTYPE)
    cre, ncim = jnp.swapaxes(cre_t, 2, 3), jnp.swapaxes(ncim_t, 2, 3)
    wbr_t, wbi_t = jnp.swapaxes(wbr, 2, 3), jnp.swapaxes(wbi, 2, 3)
    d_skip = s5_d

    xm = _ln_mod(x2, shift_x, scale_x, "ln_mod_x")
    cm = _ln_mod(ctx2, shift_c, scale_c, "ln_mod_ctx")
    ready = xm[:8, :LANES].astype(F32) + cm[:8, :LANES].astype(F32) + cre[0, 0, :8, :].astype(F32)
    hw_in2, tok_b = _forward_start(_exchange_wait(hw_in, ready, "wait_gather_w_in"), 1, "start_forward_w_in")
    w_in_f = _forward_wait(hw_in2, tok_b, "wait_forward_w_in")
    hw_glu, tok_c = _exchange_start(w_glu[0].astype(MXU_DTYPE), 0, "gather", "start_gather_w_glu")
    hw_out, tok_o = _exchange_start(w_out[0].astype(MXU_DTYPE), 0, "gather", "start_gather_w_out")
    proj = _matmul(xm, w_in_f, mode="nn", name="proj", bm=512, bn=1024, bk=d, dep=tok_c + tok_o)
    ub_c = _matmul(cm, w_in_f, mode="nn", name="proj_ctx", bm=256, bn=w, bk=d, b_n0=3 * w, n=w)
    bsf = jnp.repeat(b_spatial[0].T, HEAD_DIM_A, axis=1)
    ws = w_spatial[0]
    cat = _ga_fwd(proj, sgu_ln_g, sgu_ln_b, ws, bsf, w)
    zeros_state = jnp.zeros((2, nsr, LANES), F32)
    s5c, s5l = [], []
    for dr in range(nd):
        s5c.append(_s5_fwd(ub_c, 0, w, zeros_state, a_sm, wbr, wbi, cre, ncim, dr, f"s5_fwd_ctx{dr}"))
        s5l.append(_s5_fwd(proj, 3, w, s5c[dr][5], a_sm, wbr, wbi, cre, ncim, dr, f"s5_fwd{dr}"))
    w_glu_f = _exchange_wait(hw_glu, s5l[1][0], "wait_gather_w_glu")
    ys, cat = _glu_fwd(s5l[0][0], s5l[1][0], proj, cat, d_skip, w_glu_f, b_glu, w)
    w_out_f = _exchange_wait(hw_out, ys, "wait_gather_w_out")
    out = _matmul(cat, w_out_f, mode="nn", name="out_proj", bm=512, bn=1024, bk=2 * w)
    loss_row, dout, dx_res, dgate, dln_g, dln_b = _post_ln_loss(x2, out, gate_x, ln_g, ln_b, tgt)

    dcat = _matmul(dout, w_out_f, mode="nt", name="d_cat", bm=512, bn=1024, bk=d, out_dtype=MXU_DTYPE)
    dw_out = _matmul(cat, dout, mode="tn", name="d_w_out", bm=1024, bn=1024, bk=2048, out_dtype=MXU_DTYPE)
    hg_out, tok_d = _exchange_start(dw_out, 0, "a2a", "start_a2a_d_w_out")
    dys, dproj, db_glu, dd_skip, dw_glu = _glu_bwd(dcat, ys, proj, d_skip + tok_d[0, 0], w_glu_f, b_glu, w)
    hg_glu, tok_e = _exchange_start(dw_glu.astype(MXU_DTYPE), 0, "a2a", "start_a2a_d_w_glu")
    zeros_state = zeros_state + tok_e[0, 0]
    du_l, du_c, dwbr, dwbi, dcre, dncim, da_sm = [], [], [], [], [], [], []
    nbk, spb = w // LANES, gb * p_s5
    for dr in range(nd):
        bl = _s5_bwd(dys, proj, 3, w, *s5l[dr][1:5], s5c[dr][5], zeros_state, a_sm, wbr_t, wbi_t,
                     cre_t, ncim_t, dr, f"s5_bwd{dr}")
        bc = _s5_bwd(None, ub_c, 0, w, s5c[dr][1], s5c[dr][2], None, None, zeros_state, bl[6], a_sm, wbr_t, wbi_t,
                     None, None, dr, f"s5_bwd_ctx{dr}")
        du_l.append(bl[0])
        du_c.append(bc[0])
        dwbr.append(_add2(bl[1].reshape(nbk * LANES, spb), bc[1].reshape(nbk * LANES, spb), f"sum_dwbr{dr}"))
        dwbi.append(_add2(bl[2].reshape(nbk * LANES, spb), bc[2].reshape(nbk * LANES, spb), f"sum_dwbi{dr}"))
        dcre.append(bl[3])
        dncim.append(bl[4])
        da_sm.append(_add2(bl[5].reshape(2 * nsr, LANES), bc[3].reshape(2 * nsr, LANES), f"sum_da{dr}"))
    dproj = _dub_combine(dys, du_l[0], du_l[1], d_skip, dproj, w)
    dub_c = _add2(du_c[0], du_c[1], "dub_ctx")
    dwbr = jnp.stack(dwbr).reshape(nd, nbk, LANES, spb)
    dwbi = jnp.stack(dwbi).reshape(nd, nbk, LANES, spb)
    dcre, dncim = jnp.stack(dcre), jnp.stack(dncim)
    da_sm = jnp.stack(da_sm).reshape(nd, 2, g_s5, p_s5)
    dproj, dsg, dsb, dws, dbsf = _ga_bwd(proj, dcat, dproj, sgu_ln_g, sgu_ln_b, ws, bsf, w)
    dw_in = _matmul(xm, dproj, mode="tn", name="d_w_in", bm=1024, bn=1280, bk=2048, out_dtype=MXU_DTYPE)
    dw_in = _matmul(cm, dub_c, mode="tn", name="d_w_in_ctx", bm=1024, bn=w, bk=lc, acc_in=dw_in, acc_n0=3 * w,
                    out_dtype=MXU_DTYPE)
    hg_in, tok_f = _exchange_start(dw_in, 1, "a2a", "start_a2a_d_w_in")
    dxm = _matmul(dproj, w_in_f, mode="nt", name="d_xm", bm=512, bn=1024, bk=5 * w, dep=tok_f)
    dcm = _matmul(dub_c, w_in_f, mode="nt", name="d_cm", bm=256, bn=1024, bk=w, b_k0=3 * w, k=w)
    grad_x, dshift_x, dscale_x = _ln_mod_bwd(x2, dxm, scale_x, dx_res, "ln_mod_x_bwd")
    _, dshift_c, dscale_c = _ln_mod_bwd(ctx2, dcm, scale_c, None, "ln_mod_ctx_bwd")

    dbb_re = jnp.transpose(_block_diag_extract(dwbr, gb, c_s5, p_s5), (0, 2, 1, 3))
    dbb_im = jnp.transpose(_block_diag_extract(dwbi, gb, c_s5, p_s5), (0, 2, 1, 3))
    dc_re = jnp.swapaxes(_block_diag_extract(dcre, gb, p_s5, c_s5), 2, 3)
    dc_im = -jnp.swapaxes(_block_diag_extract(dncim, gb, p_s5, c_s5), 2, 3)
    dlr, dli, dls, dbr_t, dbi_t = _disc_bwd(lr, li, ls, br_t, bi_t, da_sm[:, 0], da_sm[:, 1], dbb_re, dbb_im)
    db_re = jnp.transpose(dbr_t, (0, 2, 3, 1))
    db_im = jnp.transpose(dbi_t, (0, 2, 3, 1))
    expand = (jnp.arange(w)[:, None] // HEAD_DIM_A == jnp.arange(LANES)[None, :]).astype(F32)
    db_sp = _lane_group_sum(dbsf, expand, "d_b_spatial")[:, :nh].T

    dmod_rows = jnp.concatenate([jnp.concatenate([dshift_x, dscale_x, dgate], axis=1),
                                 jnp.concatenate([dshift_c, dscale_c, jnp.zeros((1, d), F32)], axis=1),
                                 jnp.zeros((6, 3 * d), F32)], axis=0)
    dmod_all = _all_gather(dmod_rows, 0, "gather_dmod")
    dmod_ctx = _sum_slots(dmod_all[1::8].reshape(N_DEV, 1, 3 * d), "sum_dmod_ctx")
    dmod_mat = jnp.concatenate([dmod_all[0::8], dmod_ctx, jnp.zeros((7, 3 * d), F32)], axis=0)
    db_ada = _sum_slots(dmod_mat[:9].reshape(9, 1, 3 * d), "sum_db_ada")
    dmod_mine = lax.dynamic_slice_in_dim(dmod_mat, me * ada_cols, ada_cols, axis=1)
    dw_ada = _small_dot(s_mat, dmod_mine, "tn", "d_w_ada")
    dsilu_cc = _small_dot(dmod_mine[8:16], w_ada[0], "nt", "d_silu_cctx")[0:1]
    dc_ctx_part = dsilu_cc * _silu_grad(c_ctx.reshape(1, d))

    local = {"c_ctx": dc_ctx_part, "sgu_ln_g": dsg, "sgu_ln_b": dsb, "w_spatial": dws, "b_spatial": db_sp,
             "s5_lam_re": dlr, "s5_lam_im": dli, "s5_log_step": dls, "s5_b_re": db_re, "s5_b_im": db_im,
             "s5_c_re": dc_re, "s5_c_im": dc_im, "s5_d": dd_skip, "b_glu": db_glu, "ln_g": dln_g, "ln_b": dln_b}
    reduced = [n for n in small_names if n != "b_ada"]
    loss_part = (0.5 / d) * jnp.sum(loss_row)
    flat = jnp.concatenate([loss_part.reshape(1)] + [local[n].reshape(-1) for n in reduced])
    unit = N_DEV * 8 * LANES
    total = -(-flat.shape[0] // unit) * unit
    flat = jnp.pad(flat, (0, total - flat.shape[0])).reshape(N_DEV * 8, total // (N_DEV * 8))
    slots = _all_to_all(flat, 0, "a2a_small")
    mine = _sum_slots(slots, "sum_small")
    summed = _all_gather(mine, 0, "gather_small").reshape(-1)
    loss = summed[0]
    grads, off = {"b_ada": db_ada}, 1
    for n in reduced:
        size = math.prod(env[n].shape)
        grads[n] = summed[off:off + size].reshape(env[n].shape)
        off += size

    gp_w_out = _exchange_wait(hg_out, summed, "wait_a2a_d_w_out")
    gp_w_glu = _exchange_wait(hg_glu, summed, "wait_a2a_d_w_glu")
    gp_w_in = _exchange_wait(hg_in, summed, "wait_a2a_d_w_in")
    big = {
        "w_ada": _adamw(w_ada[0], m_w_ada[0], v_w_ada[0], dw_ada[None], "adamw_w_ada"),
        "w_in": _adamw(w_in[0], m_w_in[0], v_w_in[0], gp_w_in, "adamw_w_in"),
        "w_glu": _adamw(w_glu[0], m_w_glu[0], v_w_glu[0], gp_w_glu, "adamw_w_glu"),
        "w_out": _adamw(w_out[0], m_w_out[0], v_w_out[0], gp_w_out, "adamw_w_out"),
    }
    res = {n: tuple(a[None] for a in big[n]) for n in big}
    for n in small_names:
        res[n] = (grads[n],) + _adamw_nd(env[n], env["m_" + n], env["v_" + n], grads[n], "adamw_" + n)

    order = ["c_ctx", "w_ada", "b_ada", "w_in", "sgu_ln_g", "sgu_ln_b", "w_spatial", "b_spatial", "s5_lam_re", "s5_lam_im",
             "s5_log_step", "s5_b_re", "s5_b_im", "s5_c_re", "s5_c_im", "s5_d", "w_glu", "b_glu", "w_out", "ln_g", "ln_b"]
    return (loss, grad_x[None], *[res[n][0] for n in order], *[res[n][1] for n in order],
            *[res[n][2] for n in order], *[res[n][3] for n in order])
```

```python
import functools
import math

import jax
import jax.numpy as jnp
from jax import lax
from jax.experimental import pallas as pl
from jax.experimental.pallas import tpu as pltpu

F32 = jnp.float32
MXU_DTYPE = jnp.bfloat16
N_DEV = 8
MESH_ID = pl.DeviceIdType.MESH
LN_EPS = 1e-6
DEPTH = 1
ALPHA = (2.0 * DEPTH) ** 0.25
CHUNK = 128
HEAD_DIM_A = 128
ADAM_LR, ADAM_B1, ADAM_B2, ADAM_EPS, ADAM_WD, ADAM_STEP = 0.001, 0.9, 0.999, 1e-08, 0.01, 10
LANES = 128
SCAN_UNROLL = 8
VMEM_LIMIT = 56 * 1024 * 1024
HBM = pl.BlockSpec(memory_space=pl.ANY)


def _cparams(*sem):
    return pltpu.CompilerParams(dimension_semantics=sem if sem else None, vmem_limit_bytes=VMEM_LIMIT)


def _tile(n, pref, mult=1):
    if n <= pref:
        return n
    t = pref - pref % mult
    while n % t:
        t -= mult
    return t


def _gelu(x):
    return 0.5 * x * (1.0 + lax.erf(x * (1.0 / math.sqrt(2.0))))


def _gelu_grad(x):
    return 0.5 * (1.0 + lax.erf(x * (1.0 / math.sqrt(2.0)))) + x * jnp.exp(-0.5 * x * x) * (1.0 / math.sqrt(2.0 * math.pi))


def _silu_grad(x):
    s = jax.nn.sigmoid(x)
    return s * (1.0 + x * (1.0 - s))


def _mxu_dot(a, b, dims=(((1,), (0,)), ((), ()))):
    return lax.dot_general(a.astype(MXU_DTYPE), b.astype(MXU_DTYPE), dims, preferred_element_type=F32)


_NT = (((1,), (1,)), ((), ()))
_TN = (((0,), (0,)), ((), ()))


def _mesh_pos():
    return lax.axis_index("x"), lax.axis_index("y"), lax.axis_index("c")


def _peer(pos, r):
    x, y, c = pos
    return ((1 - x) if r & 4 else x, (1 - y) if r & 2 else y, (1 - c) if r & 1 else c)


def _index(pos):
    return 4 * pos[0] + 2 * pos[1] + pos[2]


def _slice_of(ref, axis, idx, size):
    start = idx * size
    if axis == 0:
        return ref.at[pl.ds(start, size)]
    return ref.at[:, pl.ds(start, size)]


def _all_gather(x, axis, name):
    size = x.shape[axis]
    out_shape = tuple(s * N_DEV if a == axis else s for a, s in enumerate(x.shape))

    def body(x_ref, o_ref, send_sems, recv_sems, local_sem):
        me = _mesh_pos()
        mine = pltpu.make_async_copy(x_ref, _slice_of(o_ref, axis, _index(me), size), local_sem)
        mine.start()

        def copy(r, block):
            return pltpu.make_async_remote_copy(
                src_ref=x_ref, dst_ref=_slice_of(o_ref, axis, _index(block), size),
                send_sem=send_sems.at[r - 1], recv_sem=recv_sems.at[r - 1],
                device_id=_peer(me, r), device_id_type=MESH_ID)

        sends = [copy(r, me) for r in range(1, N_DEV)]
        for cp in sends:
            cp.start()
        for r in range(1, N_DEV):
            copy(r, _peer(me, r)).wait_recv()
        for cp in sends:
            cp.wait_send()
        mine.wait()

    return pl.pallas_call(
        body, name=name, out_shape=jax.ShapeDtypeStruct(out_shape, x.dtype),
        in_specs=[HBM], out_specs=HBM,
        scratch_shapes=[pltpu.SemaphoreType.DMA((N_DEV - 1,)), pltpu.SemaphoreType.DMA((N_DEV - 1,)),
                        pltpu.SemaphoreType.DMA],
    )(x)


def _all_to_all(x, axis, name):
    size = x.shape[axis] // N_DEV
    slot = tuple(size if a == axis else s for a, s in enumerate(x.shape))

    def body(x_ref, o_ref, send_sems, recv_sems, local_sem):
        me = _mesh_pos()
        mine = pltpu.make_async_copy(_slice_of(x_ref, axis, _index(me), size), o_ref.at[_index(me)], local_sem)
        mine.start()

        def copy(r, sender, receiver):
            return pltpu.make_async_remote_copy(
                src_ref=_slice_of(x_ref, axis, _index(receiver), size), dst_ref=o_ref.at[_index(sender)],
                send_sem=send_sems.at[r - 1], recv_sem=recv_sems.at[r - 1],
                device_id=_peer(me, r), device_id_type=MESH_ID)

        sends = [copy(r, me, _peer(me, r)) for r in range(1, N_DEV)]
        for cp in sends:
            cp.start()
        for r in range(1, N_DEV):
            copy(r, _peer(me, r), me).wait_recv()
        for cp in sends:
            cp.wait_send()
        mine.wait()

    return pl.pallas_call(
        body, name=name, out_shape=jax.ShapeDtypeStruct((N_DEV,) + slot, x.dtype),
        in_specs=[HBM], out_specs=HBM,
        scratch_shapes=[pltpu.SemaphoreType.DMA((N_DEV - 1,)), pltpu.SemaphoreType.DMA((N_DEV - 1,)),
                        pltpu.SemaphoreType.DMA],
    )(x)


_SEM = pl.BlockSpec(memory_space=pltpu.SEMAPHORE)
_HBM = pl.BlockSpec(memory_space=pltpu.HBM)
_EFFECT = pltpu.SideEffectType.DATAFLOW_SIDE_EFFECTING
ALL_PEERS = tuple(range(1, N_DEV))
SIBLING = 1
SAME_CORE_PEERS = (2, 4, 6)


def _exchange_copy(kind, x_ref, land_ref, axis, size, send_sems, recv_sems, me, rels, q, arriving):
    peer = _peer(me, rels[q])
    sender, receiver = (peer, me) if arriving else (me, peer)
    if kind == "gather":
        src, dst = x_ref, _slice_of(land_ref, axis, _index(sender), size)
    else:
        src, dst = _slice_of(x_ref, axis, _index(receiver), size), land_ref.at[_index(sender)]
    return pltpu.make_async_remote_copy(src_ref=src, dst_ref=dst, send_sem=send_sems.at[q], recv_sem=recv_sems.at[q],
                                        device_id=peer, device_id_type=MESH_ID)


def _local_copy(kind, x_ref, land_ref, axis, size, me, local_sem):
    if kind == "gather":
        return pltpu.make_async_copy(x_ref, _slice_of(land_ref, axis, _index(me), size), local_sem)
    return pltpu.make_async_copy(_slice_of(x_ref, axis, _index(me), size), land_ref.at[_index(me)], local_sem)


def _exchange_start(x, axis, kind, name, rels=ALL_PEERS):
    size = x.shape[axis] if kind == "gather" else x.shape[axis] // N_DEV
    if kind == "gather":
        land_shape = tuple(s * N_DEV if a == axis else s for a, s in enumerate(x.shape))
    else:
        land_shape = (N_DEV,) + tuple(size if a == axis else s for a, s in enumerate(x.shape))

    def body(x_ref, land_ref, send_sems, recv_sems, local_sem, x_thru, land_thru, token):
        del x_thru, land_thru
        me = _mesh_pos()
        _local_copy(kind, x_ref, land_ref, axis, size, me, local_sem).start()
        for q in range(len(rels)):
            _exchange_copy(kind, x_ref, land_ref, axis, size, send_sems, recv_sems, me, rels, q, False).start()
        token[...] = jnp.zeros_like(token)

    sems = pltpu.SemaphoreType.DMA((len(rels),))
    send_sems, recv_sems, local_sem, x_thru, land_thru, token = pl.pallas_call(
        body, name=name,
        out_shape=(sems, sems, pltpu.SemaphoreType.DMA(()), pltpu.HBM(x.shape, x.dtype), pltpu.HBM(land_shape, x.dtype),
                   jax.ShapeDtypeStruct((8, LANES), F32)),
        in_specs=(_HBM, _HBM), out_specs=(_SEM, _SEM, _SEM, _HBM, _HBM, pl.BlockSpec(memory_space=pltpu.VMEM)),
        input_output_aliases={0: 3, 1: 4}, compiler_params=pltpu.CompilerParams(has_side_effects=_EFFECT),
    )(pltpu.with_memory_space_constraint(x, pltpu.HBM),
      pltpu.with_memory_space_constraint(lax.empty(land_shape, x.dtype), pltpu.HBM))
    return (kind, axis, size, rels, send_sems, recv_sems, local_sem, x_thru, land_thru), token


def _exchange_wait(handle, after, name):
    kind, axis, size, rels, send_sems, recv_sems, local_sem, x_thru, land_thru = handle

    def body(x_ref, land_ref, send_sems, recv_sems, local_sem, after_ref, x_dead, got_ref):
        del after_ref, x_dead, got_ref
        me = _mesh_pos()
        _local_copy(kind, x_ref, land_ref, axis, size, me, local_sem).wait()
        for q in range(len(rels)):
            _exchange_copy(kind, x_ref, land_ref, axis, size, send_sems, recv_sems, me, rels, q, False).wait_send()
        for q in range(len(rels)):
            _exchange_copy(kind, x_ref, land_ref, axis, size, send_sems, recv_sems, me, rels, q, True).wait_recv()

    return pl.pallas_call(
        body, name=name, out_shape=(pltpu.HBM(x_thru.shape, x_thru.dtype), pltpu.HBM(land_thru.shape, land_thru.dtype)),
        in_specs=(_HBM, _HBM, _SEM, _SEM, _SEM, HBM), out_specs=(_HBM, _HBM), input_output_aliases={0: 0, 1: 1},
        compiler_params=pltpu.CompilerParams(has_side_effects=_EFFECT),
    )(x_thru, land_thru, send_sems, recv_sems, local_sem, after)[1]


def _forward_copy(land_ref, axis, size, send_sems, recv_sems, me, q, arriving):
    sibling = _peer(me, SIBLING)
    owner = _peer(sibling if arriving else me, SAME_CORE_PEERS[q])
    block = _slice_of(land_ref, axis, _index(owner), size)
    return pltpu.make_async_remote_copy(src_ref=block, dst_ref=block, send_sem=send_sems.at[q], recv_sem=recv_sems.at[q],
                                        device_id=sibling, device_id_type=MESH_ID)


def _forward_start(land, axis, name):
    size = land.shape[axis] // N_DEV

    def body(land_ref, send_sems, recv_sems, land_thru, token):
        del land_thru
        me = _mesh_pos()
        for q in range(len(SAME_CORE_PEERS)):
            _forward_copy(land_ref, axis, size, send_sems, recv_sems, me, q, False).start()
        token[...] = jnp.zeros_like(token)

    sems = pltpu.SemaphoreType.DMA((len(SAME_CORE_PEERS),))
    send_sems, recv_sems, land_thru, token = pl.pallas_call(
        body, name=name, out_shape=(sems, sems, pltpu.HBM(land.shape, land.dtype), jax.ShapeDtypeStruct((8, LANES), F32)),
        in_specs=(_HBM,), out_specs=(_SEM, _SEM, _HBM, pl.BlockSpec(memory_space=pltpu.VMEM)),
        input_output_aliases={0: 2}, compiler_params=pltpu.CompilerParams(has_side_effects=_EFFECT),
    )(land)
    return (axis, size, send_sems, recv_sems, land_thru), token


def _forward_wait(handle, after, name):
    axis, size, send_sems, recv_sems, land_thru = handle

    def body(land_ref, send_sems, recv_sems, after_ref, got_ref):
        del after_ref, got_ref
        me = _mesh_pos()
        for q in range(len(SAME_CORE_PEERS)):
            _forward_copy(land_ref, axis, size, send_sems, recv_sems, me, q, False).wait_send()
        for q in range(len(SAME_CORE_PEERS)):
            _forward_copy(land_ref, axis, size, send_sems, recv_sems, me, q, True).wait_recv()

    return pl.pallas_call(
        body, name=name, out_shape=pltpu.HBM(land_thru.shape, land_thru.dtype),
        in_specs=(_HBM, _SEM, _SEM, HBM), out_specs=_HBM, input_output_aliases={0: 0},
        compiler_params=pltpu.CompilerParams(has_side_effects=_EFFECT),
    )(land_thru, send_sems, recv_sems, after)


def _matmul(a, b, *, mode, name, bm=512, bn=512, bk=512, out_dtype=F32, b_n0=0, n=None, b_k0=0, k=None,
            acc_in=None, acc_n0=0, dep=None):
    if mode == "tn":
        kk, m = a.shape
    else:
        m, kk = a.shape
    if mode == "nn":
        n = b.shape[1] if n is None else n
    elif mode == "nt":
        n = b.shape[0]
        kk = kk if k is None else k
    else:
        n = b.shape[1]
    bm, bn, bk = _tile(m, bm), _tile(n, bn), _tile(kk, bk)
    nk = kk // bk
    assert b_n0 % bn == 0 and b_k0 % bk == 0 and acc_n0 % bn == 0
    dims = {"nn": (((1,), (0,)), ((), ())), "nt": _NT, "tn": _TN}[mode]

    n_in = 2 + (acc_in is not None) + (dep is not None)

    def body(*refs):
        a_ref, b_ref = refs[:2]
        init = refs[2] if acc_in is not None else None
        o_ref = refs[n_in]
        acc_ref = refs[-1] if nk > 1 else None
        p = _mxu_dot(a_ref[...], b_ref[...], dims)
        if nk == 1:
            o_ref[...] = (p if init is None else p + init[...]).astype(out_dtype)
            return
        ki = pl.program_id(2)

        @pl.when(ki == 0)
        def _():
            acc_ref[...] = p if init is None else p + init[...]

        @pl.when(ki > 0)
        def _():
            acc_ref[...] += p

        @pl.when(ki == nk - 1)
        def _():
            o_ref[...] = acc_ref[...].astype(out_dtype)

    a_spec = pl.BlockSpec((bk, bm), lambda j, i, q: (q, i)) if mode == "tn" else pl.BlockSpec((bm, bk), lambda j, i, q: (i, q))
    if mode == "nt":
        b_spec = pl.BlockSpec((bn, bk), lambda j, i, q: (j, q + b_k0 // bk))
    else:
        b_spec = pl.BlockSpec((bk, bn), lambda j, i, q: (q, j + b_n0 // bn))
    in_specs, args, aliases = [a_spec, b_spec], [a, b], {}
    out_map = lambda j, i, q: (i, j + acc_n0 // bn)
    if acc_in is not None:
        in_specs.append(pl.BlockSpec((bm, bn), out_map))
        args.append(acc_in)
        aliases = {2: 0}
        out_shape = jax.ShapeDtypeStruct(acc_in.shape, out_dtype)
    else:
        out_shape = jax.ShapeDtypeStruct((m, n), out_dtype)
    if dep is not None:
        in_specs.append(HBM)
        args.append(dep)
    return pl.pallas_call(
        body, name=name, out_shape=out_shape, grid=(n // bn, m // bm, nk),
        in_specs=in_specs, out_specs=pl.BlockSpec((bm, bn), out_map),
        scratch_shapes=[pltpu.VMEM((bm, bn), F32)] if nk > 1 else [],
        input_output_aliases=aliases,
        compiler_params=_cparams("parallel", "parallel", "arbitrary"),
    )(*args)


def _silu_rows(c, c_ctx):
    d = c.shape[-1]

    def body(c_ref, cc_ref, o_ref):
        o_ref[...] = jnp.zeros_like(o_ref)
        o_ref[0:1, :] = jax.nn.silu(c_ref[...])
        o_ref[1:2, :] = jax.nn.silu(cc_ref[...])

    return pl.pallas_call(body, name="silu_rows", out_shape=jax.ShapeDtypeStruct((8, d), F32))(
        c.reshape(1, d), c_ctx.reshape(1, d))


def _small_dot(a, b, mode, name):
    dims = {"nn": (((1,), (0,)), ((), ())), "nt": _NT, "tn": _TN}[mode]
    m = a.shape[1] if mode == "tn" else a.shape[0]
    n = b.shape[0] if mode == "nt" else b.shape[1]

    def body(a_ref, b_ref, o_ref):
        o_ref[...] = lax.dot_general(a_ref[...], b_ref[...], dims, preferred_element_type=F32,
                                     precision=lax.Precision.HIGHEST)

    return pl.pallas_call(body, name=name, out_shape=jax.ShapeDtypeStruct((m, n), F32),
                          compiler_params=_cparams())(a, b)


def _ln_stats(x):
    mu = jnp.mean(x, axis=-1, keepdims=True)
    xc = x - mu
    var = jnp.mean(xc * xc, axis=-1, keepdims=True)
    rstd = lax.rsqrt(var + LN_EPS)
    return xc * rstd, rstd


def _ln_mod(x, shift, scale, name):
    l, d = x.shape
    tl = _tile(l, 256)

    def body(x_ref, sh_ref, sc_ref, o_ref):
        xhat, _ = _ln_stats(x_ref[...])
        o_ref[...] = (xhat * (1.0 + sc_ref[...]) + sh_ref[...]).astype(o_ref.dtype)

    row = pl.BlockSpec((tl, d), lambda i: (i, 0))
    vec = pl.BlockSpec((1, d), lambda i: (0, 0))
    return pl.pallas_call(body, name=name, out_shape=jax.ShapeDtypeStruct((l, d), MXU_DTYPE), grid=(l // tl,),
                          in_specs=[row, vec, vec], out_specs=row, compiler_params=_cparams("parallel"))(x, shift, scale)


def _ln_mod_bwd(x, dxm, scale, res, name):
    l, d = x.shape
    tl = _tile(l, 256)
    with_res = res is not None

    def body(*refs):
        if with_res:
            x_ref, g_ref, sc_ref, r_ref, dx_ref, dsh_ref, dsc_ref = refs
        else:
            x_ref, g_ref, sc_ref, dx_ref, dsh_ref, dsc_ref = refs
        i = pl.program_id(0)
        xhat, rstd = _ln_stats(x_ref[...])
        g = g_ref[...].astype(F32)
        dxh = g * (1.0 + sc_ref[...])
        dx = rstd * (dxh - jnp.mean(dxh, axis=-1, keepdims=True) - xhat * jnp.mean(dxh * xhat, axis=-1, keepdims=True))
        dx_ref[...] = dx + r_ref[...].astype(F32) if with_res else dx

        @pl.when(i == 0)
        def _():
            dsh_ref[...] = jnp.zeros_like(dsh_ref)
            dsc_ref[...] = jnp.zeros_like(dsc_ref)

        dsh_ref[...] += jnp.sum(g, axis=0, keepdims=True)
        dsc_ref[...] += jnp.sum(g * xhat, axis=0, keepdims=True)

    row = pl.BlockSpec((tl, d), lambda i: (i, 0))
    vec = pl.BlockSpec((1, d), lambda i: (0, 0))
    args = [x, dxm, scale] + ([res] if with_res else [])
    return pl.pallas_call(
        body, name=name,
        out_shape=(jax.ShapeDtypeStruct((l, d), F32), jax.ShapeDtypeStruct((1, d), F32), jax.ShapeDtypeStruct((1, d), F32)),
        grid=(l // tl,), in_specs=[row, row, vec] + ([row] if with_res else []), out_specs=(row, vec, vec),
        compiler_params=_cparams("arbitrary"))(*args)


def _post_ln_loss(x, out, gate, ln_g, ln_b, target):
    l, d = x.shape
    tl = _tile(l, 256)

    def body(x_ref, o_ref, gate_ref, g_ref, b_ref, t_ref, loss_ref, dout_ref, dxr_ref, dgate_ref, dg_ref, db_ref):
        i = pl.program_id(0)
        out_t = o_ref[...]
        gate_v = gate_ref[...]
        rhat, rstd = _ln_stats(ALPHA * x_ref[...] + gate_v * out_t)
        ln_gv = g_ref[...]
        diff = rhat * ln_gv + b_ref[...] - t_ref[...]
        dy = diff * (1.0 / d)
        drh = dy * ln_gv
        dr = rstd * (drh - jnp.mean(drh, axis=-1, keepdims=True) - rhat * jnp.mean(drh * rhat, axis=-1, keepdims=True))
        dout_ref[...] = (gate_v * dr).astype(dout_ref.dtype)
        dxr_ref[...] = (ALPHA * dr).astype(dxr_ref.dtype)

        @pl.when(i == 0)
        def _():
            for r in (loss_ref, dgate_ref, dg_ref, db_ref):
                r[...] = jnp.zeros_like(r)

        loss_ref[...] += jnp.sum(diff * diff, axis=0, keepdims=True)
        dgate_ref[...] += jnp.sum(dr * out_t, axis=0, keepdims=True)
        dg_ref[...] += jnp.sum(dy * rhat, axis=0, keepdims=True)
        db_ref[...] += jnp.sum(dy, axis=0, keepdims=True)

    row = pl.BlockSpec((tl, d), lambda i: (i, 0))
    vec = pl.BlockSpec((1, d), lambda i: (0, 0))
    v = jax.ShapeDtypeStruct((1, d), F32)
    return pl.pallas_call(
        body, name="post_ln_loss",
        out_shape=(v, jax.ShapeDtypeStruct((l, d), MXU_DTYPE), jax.ShapeDtypeStruct((l, d), MXU_DTYPE), v, v, v),
        grid=(l // tl,), in_specs=[row, row, vec, vec, vec, row], out_specs=(vec, row, row, vec, vec, vec),
        compiler_params=_cparams("arbitrary"))(x, out, gate, ln_g, ln_b, target)


def _ga_forward_tile(p, g, b, ws_ref, bsf, w, nc, nh):
    u_raw, v_raw, za = p[:, :w], p[:, w:2 * w], p[:, 2 * w:3 * w]
    gu = _gelu(u_raw)
    vhat, rstd = _ln_stats(_gelu(v_raw))
    vn = vhat * g + b
    rows = []
    for ci in range(nc):
        r0 = ci * CHUNK
        heads = [_mxu_dot(ws_ref[h], vn[r0:r0 + CHUNK, h * HEAD_DIM_A:(h + 1) * HEAD_DIM_A]) for h in range(nh)]
        rows.append(jnp.concatenate(heads, axis=1) + bsf)
    mixed = jnp.concatenate(rows, axis=0) if nc > 1 else rows[0]
    return u_raw, v_raw, za, gu, vhat, rstd, vn, mixed


def _ga_fwd(proj, g, b, ws, bsf, w):
    l = proj.shape[0]
    nh = w // HEAD_DIM_A
    nc = _tile(l // CHUNK, 2)
    tl = nc * CHUNK

    def body(p_ref, g_ref, b_ref, ws_ref, bsf_ref, o_ref):
        _, _, za, gu, _, _, _, mixed = _ga_forward_tile(p_ref[...], g_ref[...], b_ref[...], ws_ref, bsf_ref[...], w, nc, nh)
        o_ref[...] = (gu * mixed * jax.nn.silu(za)).astype(o_ref.dtype)

    vec = pl.BlockSpec((1, w), lambda i: (0, 0))
    return pl.pallas_call(
        body, name="ga_fwd", out_shape=jax.ShapeDtypeStruct((l, 2 * w), MXU_DTYPE), grid=(l // tl,),
        in_specs=[pl.BlockSpec((tl, 3 * w), lambda i: (i, 0)), vec, vec,
                  pl.BlockSpec((nh, CHUNK, CHUNK), lambda i: (0, 0, 0)), pl.BlockSpec((CHUNK, w), lambda i: (0, 0))],
        out_specs=pl.BlockSpec((tl, w), lambda i: (i, 0)), compiler_params=_cparams("parallel"))(proj, g, b, ws, bsf)


def _ga_bwd(proj, dcat, dproj, g, b, ws, bsf, w):
    l = proj.shape[0]
    nh = w // HEAD_DIM_A
    nc = _tile(l // CHUNK, 2)
    tl = nc * CHUNK

    def body(p_ref, dy_ref, dp_in, g_ref, b_ref, ws_ref, bsf_ref, dp_ref, dg_ref, db_ref, dws_ref, dbsf_ref):
        del dp_in
        i = pl.program_id(0)
        gv = g_ref[...]
        u_raw, v_raw, za, gu, vhat, rstd, vn, mixed = _ga_forward_tile(
            p_ref[...], gv, b_ref[...], ws_ref, bsf_ref[...], w, nc, nh)
        dya = dy_ref[...].astype(F32)
        sz = jax.nn.silu(za)
        dmixed = dya * gu * sz
        dza = dya * gu * mixed * _silu_grad(za)
        dgu = dya * mixed * sz

        @pl.when(i == 0)
        def _():
            for r in (dg_ref, db_ref, dws_ref, dbsf_ref):
                r[...] = jnp.zeros_like(r)

        rows = []
        for ci in range(nc):
            r0 = ci * CHUNK
            heads = []
            for h in range(nh):
                cols = slice(h * HEAD_DIM_A, (h + 1) * HEAD_DIM_A)
                dm = dmixed[r0:r0 + CHUNK, cols]
                heads.append(_mxu_dot(ws_ref[h], dm, _TN))
                dws_ref[h] += _mxu_dot(dm, vn[r0:r0 + CHUNK, cols], _NT)
            rows.append(jnp.concatenate(heads, axis=1))
            dbsf_ref[...] += dmixed[r0:r0 + CHUNK, :]
        dvn = jnp.concatenate(rows, axis=0) if nc > 1 else rows[0]
        dg_ref[...] += jnp.sum(dvn * vhat, axis=0, keepdims=True)
        db_ref[...] += jnp.sum(dvn, axis=0, keepdims=True)
        dvh = dvn * gv
        dgv = rstd * (dvh - jnp.mean(dvh, axis=-1, keepdims=True) - vhat * jnp.mean(dvh * vhat, axis=-1, keepdims=True))
        dp_ref[:, :w] = (dgu * _gelu_grad(u_raw)).astype(dp_ref.dtype)
        dp_ref[:, w:2 * w] = (dgv * _gelu_grad(v_raw)).astype(dp_ref.dtype)
        dp_ref[:, 2 * w:] = dza.astype(dp_ref.dtype)

    vec = pl.BlockSpec((1, w), lambda i: (0, 0))
    ws_spec = pl.BlockSpec((nh, CHUNK, CHUNK), lambda i: (0, 0, 0))
    bs_spec = pl.BlockSpec((CHUNK, w), lambda i: (0, 0))
    v = jax.ShapeDtypeStruct((1, w), F32)
    return pl.pallas_call(
        body, name="ga_bwd",
        out_shape=(jax.ShapeDtypeStruct(dproj.shape, dproj.dtype), v, v, jax.ShapeDtypeStruct((nh, CHUNK, CHUNK), F32),
                   jax.ShapeDtypeStruct((CHUNK, w), F32)),
        grid=(l // tl,),
        in_specs=[pl.BlockSpec((tl, 3 * w), lambda i: (i, 0)), pl.BlockSpec((tl, w), lambda i: (i, 0)), HBM,
                  vec, vec, ws_spec, bs_spec],
        out_specs=(pl.BlockSpec((tl, 3 * w), lambda i: (i, 0)), vec, vec, ws_spec, bs_spec),
        input_output_aliases={2: 0}, compiler_params=_cparams("arbitrary"))(proj, dcat, dproj, g, b, ws, bsf)


def _lane_group_sum(x, expand, name):
    return _small_dot(x, expand, "nn", name)


def _disc_math(lr, li, ls, br, bi):
    step = jnp.exp(ls)
    dr, di = lr * step, li * step
    mag = jnp.exp(dr)
    ab_re, ab_im = mag * jnp.cos(di), mag * jnp.sin(di)
    den = lr * lr + li * li
    nr, ni = ab_re - 1.0, ab_im
    f_re = (nr * lr + ni * li) / den
    f_im = (ni * lr - nr * li) / den
    bb_re = f_re * br - f_im * bi
    bb_im = f_re * bi + f_im * br
    return ab_re, ab_im, bb_re, bb_im


def _disc_fwd(lr, li, ls, br, bi):
    def body(lr_ref, li_ref, ls_ref, br_ref, bi_ref, o1, o2, o3, o4):
        res = _disc_math(lr_ref[...], li_ref[...], ls_ref[...], br_ref[...], bi_ref[...])
        for o, r in zip((o1, o2, o3, o4), res):
            o[...] = r

    s = lambda a: jax.ShapeDtypeStruct(a.shape, F32)
    return pl.pallas_call(body, name="s5_disc", out_shape=(s(lr), s(lr), s(br), s(br)), compiler_params=_cparams())(
        lr, li, ls, br, bi)


def _disc_bwd(lr, li, ls, br, bi, d_ar, d_ai, d_br, d_bi):
    def body(lr_ref, li_ref, ls_ref, br_ref, bi_ref, c1, c2, c3, c4, o1, o2, o3, o4, o5):
        _, vjp = jax.vjp(_disc_math, lr_ref[...], li_ref[...], ls_ref[...], br_ref[...], bi_ref[...])
        res = vjp((c1[...], c2[...], c3[...], c4[...]))
        for o, r in zip((o1, o2, o3, o4, o5), res):
            o[...] = r

    s = lambda a: jax.ShapeDtypeStruct(a.shape, F32)
    return pl.pallas_call(body, name="s5_disc_bwd", out_shape=(s(lr), s(lr), s(ls), s(br), s(br)),
                          compiler_params=_cparams())(lr, li, ls, br, bi, d_ar, d_ai, d_br, d_bi)


def _dir_spec(a, dr):
    return pl.BlockSpec((None,) + a.shape[1:], lambda i: (dr,) + (0,) * (a.ndim - 1))


def _s5_fwd(u_arr, u_col, w, h0, a_sm, wbr, wbi, cre, ncim, dr, name):
    rev = dr == 1
    l = u_arr.shape[0]
    nb = w // LANES
    spb = wbr.shape[-1]
    nsr = a_sm.shape[2]
    assert 2 * spb == 8 * LANES and nb % 2 == 0
    t = _tile(l, 256)
    n = l // t
    tile = (lambda i: n - 1 - i) if rev else (lambda i: i)

    def body(u_ref, h0_ref, a_ref, wbr_ref, wbi_ref, cre_ref, ncim_ref, y_ref, hr_ref, hi_ref, tr_ref, ti_ref, hfin_ref,
             carry_ref):
        i = pl.program_id(0)

        @pl.when(i == 0)
        def _():
            carry_ref[...] = h0_ref[...]

        for j in range(nb // 2):
            for h_ref, w_ref in ((hr_ref, wbr_ref), (hi_ref, wbi_ref)):
                blk = [_mxu_dot(u_ref[:, k * LANES:(k + 1) * LANES], w_ref[k]) for k in (2 * j, 2 * j + 1)]
                h_ref[:, 8 * j:8 * j + 8, :] = jnp.concatenate(blk, axis=1).reshape(t, 8, LANES)
        ar, ai = a_ref[0], a_ref[1]

        def steps(blk, c):
            hr, hi = c
            for q in range(SCAN_UNROLL):
                s = blk * SCAN_UNROLL + q
                row = t - 1 - s if rev else s
                hr, hi = ar * hr - ai * hi + hr_ref[row], ar * hi + ai * hr + hi_ref[row]
                hr_ref[row] = hr
                hi_ref[row] = hi
            return hr, hi

        c = lax.fori_loop(0, t // SCAN_UNROLL, steps, (carry_ref[0], carry_ref[1]))
        for q in range(2):
            carry_ref[q] = c[q]
            hfin_ref[q] = c[q]
        for j in range(nb // 2):
            cols8 = slice(j * 8 * LANES, (j + 1) * 8 * LANES)
            tr_ref[:, cols8] = hr_ref[:, 8 * j:8 * j + 8, :].reshape(t, 8 * LANES).astype(tr_ref.dtype)
            ti_ref[:, cols8] = hi_ref[:, 8 * j:8 * j + 8, :].reshape(t, 8 * LANES).astype(ti_ref.dtype)
        for k in range(nb):
            cols = slice(k * spb, (k + 1) * spb)
            y_ref[:, k * LANES:(k + 1) * LANES] = (_mxu_dot(tr_ref[:, cols], cre_ref[k]) + _mxu_dot(ti_ref[:, cols], ncim_ref[k]))

    full = lambda a: pl.BlockSpec(a.shape, lambda i: (0,) * a.ndim)
    hspec = pl.BlockSpec((t, nsr, LANES), lambda i: (tile(i), 0, 0))
    tspec = pl.BlockSpec((t, nsr * LANES), lambda i: (tile(i), 0))
    hsh = jax.ShapeDtypeStruct((l, nsr, LANES), F32)
    tsh = jax.ShapeDtypeStruct((l, nsr * LANES), MXU_DTYPE)
    return pl.pallas_call(
        body, name=name,
        out_shape=(jax.ShapeDtypeStruct((l, w), F32), hsh, hsh, tsh, tsh, jax.ShapeDtypeStruct((2, nsr, LANES), F32)),
        grid=(n,),
        in_specs=[pl.BlockSpec((t, w), lambda i: (tile(i), u_col)), full(h0)] + [_dir_spec(a, dr) for a in (a_sm, wbr, wbi, cre, ncim)],
        out_specs=(pl.BlockSpec((t, w), lambda i: (tile(i), 0)), hspec, hspec, tspec, tspec,
                   pl.BlockSpec((2, nsr, LANES), lambda i: (0, 0, 0))),
        scratch_shapes=[pltpu.VMEM((2, nsr, LANES), F32)],
        compiler_params=_cparams("arbitrary"))(u_arr, h0, a_sm, wbr, wbi, cre, ncim)


def _s5_bwd(dys, u_arr, u_col, w, hr, hi, tr, ti, hbound, g_in, a_sm, wbr_t, wbi_t, cre_t, ncim_t, dr, name):
    rev = dr == 1
    l = u_arr.shape[0]
    nb = w // LANES
    spb = wbr_t.shape[-2]
    nsr = a_sm.shape[2]
    t = _tile(l, 128)
    n = l // t
    with_dy = dys is not None
    tile = (lambda i: i) if rev else (lambda i: n - 1 - i)

    def body(*refs):
        if with_dy:
            (dy_ref, u_ref, hr_ref, hi_ref, pr_ref, pi_ref, tr_ref, ti_ref, hb_ref, gin_ref, a_ref, wbrt_ref, wbit_ref,
             cret_ref, ncimt_ref, du_ref, dwbr_ref, dwbi_ref, dcre_ref, dncim_ref, da_ref, gout_ref,
             gr_ref, gi_ref, gtr_ref, gti_ref, carry_ref) = refs
        else:
            (u_ref, hr_ref, hi_ref, pr_ref, pi_ref, hb_ref, gin_ref, a_ref, wbrt_ref, wbit_ref,
             du_ref, dwbr_ref, dwbi_ref, da_ref, gout_ref, gr_ref, gi_ref, gtr_ref, gti_ref, carry_ref) = refs
        i = pl.program_id(0)

        @pl.when(i == 0)
        def _():
            carry_ref[...] = gin_ref[...]
            accs = (dwbr_ref, dwbi_ref, da_ref) + ((dcre_ref, dncim_ref) if with_dy else ())
            for r in accs:
                r[...] = jnp.zeros_like(r)

        if with_dy:
            for j in range(nb // 2):
                for g_ref, c_ref in ((gr_ref, cret_ref), (gi_ref, ncimt_ref)):
                    blk = [_mxu_dot(dy_ref[:, k * LANES:(k + 1) * LANES], c_ref[k]) for k in (2 * j, 2 * j + 1)]
                    g_ref[:, 8 * j:8 * j + 8, :] = jnp.concatenate(blk, axis=1).reshape(t, 8, LANES)
        else:
            gr_ref[...] = jnp.zeros_like(gr_ref)
            gi_ref[...] = jnp.zeros_like(gi_ref)
        ar, ai = a_ref[0], a_ref[1]

        last = t - 1 if rev else 0

        def steps(blk, c):
            gr, gi, dr, di = c
            for q in range(SCAN_UNROLL):
                s = blk * SCAN_UNROLL + q
                row = s if rev else t - 1 - s
                prow = jnp.minimum(row + 1, t - 1) if rev else jnp.maximum(row - 1, 0)
                pr, pi = hr_ref[prow], hi_ref[prow]
                gr, gi = gr_ref[row] + ar * gr + ai * gi, gi_ref[row] + ar * gi - ai * gr
                gr_ref[row] = gr
                gi_ref[row] = gi
                dr, di = dr + gr * pr + gi * pi, di + gi * pr - gr * pi
            return gr, gi, dr, di

        gr, gi, dr, di = lax.fori_loop(0, t // SCAN_UNROLL, steps, (carry_ref[0], carry_ref[1], da_ref[0], da_ref[1]))
        first = i == n - 1
        pr = jnp.where(first, hb_ref[0], pr_ref[0]) - hr_ref[last]
        pi = jnp.where(first, hb_ref[1], pi_ref[0]) - hi_ref[last]
        da_ref[0] = dr + gr * pr + gi * pi
        da_ref[1] = di + gi * pr - gr * pi
        for q, val in enumerate((gr, gi)):
            carry_ref[q] = val
            gout_ref[q] = val

        for j in range(nb // 2):
            cols8 = slice(j * 8 * LANES, (j + 1) * 8 * LANES)
            gtr_ref[:, cols8] = gr_ref[:, 8 * j:8 * j + 8, :].reshape(t, 8 * LANES).astype(gtr_ref.dtype)
            gti_ref[:, cols8] = gi_ref[:, 8 * j:8 * j + 8, :].reshape(t, 8 * LANES).astype(gti_ref.dtype)
        for k in range(nb):
            cols = slice(k * spb, (k + 1) * spb)
            lanes = slice(k * LANES, (k + 1) * LANES)
            du_ref[:, lanes] = _mxu_dot(gtr_ref[:, cols], wbrt_ref[k]) + _mxu_dot(gti_ref[:, cols], wbit_ref[k])
            dwbr_ref[k] += _mxu_dot(u_ref[:, lanes], gtr_ref[:, cols], _TN)
            dwbi_ref[k] += _mxu_dot(u_ref[:, lanes], gti_ref[:, cols], _TN)
            if with_dy:
                dcre_ref[k] += _mxu_dot(tr_ref[:, cols], dy_ref[:, lanes], _TN)
                dncim_ref[k] += _mxu_dot(ti_ref[:, cols], dy_ref[:, lanes], _TN)

    full = lambda a: pl.BlockSpec(a.shape, lambda i: (0,) * a.ndim)
    row = lambda cb: pl.BlockSpec((t, w), lambda i: (tile(i), cb))
    hspec = pl.BlockSpec((t, nsr, LANES), lambda i: (tile(i), 0, 0))
    if rev:
        pspec = pl.BlockSpec((1, nsr, LANES), lambda i: (jnp.minimum((tile(i) + 1) * t, l - 1), 0, 0))
    else:
        pspec = pl.BlockSpec((1, nsr, LANES), lambda i: (jnp.maximum(tile(i) * t - 1, 0), 0, 0))
    sm = jax.ShapeDtypeStruct((2, nsr, LANES), F32)
    smspec = pl.BlockSpec((2, nsr, LANES), lambda i: (0, 0, 0))
    wsh = jax.ShapeDtypeStruct((nb, LANES, spb), F32)
    csh = jax.ShapeDtypeStruct((nb, spb, LANES), F32)
    tspec = pl.BlockSpec((t, nsr * LANES), lambda i: (tile(i), 0))
    in_specs = (([row(0)] if with_dy else []) + [row(u_col), hspec, hspec, pspec, pspec] + ([tspec, tspec] if with_dy else [])
                + [full(hbound), full(g_in)]
                + [_dir_spec(a, dr) for a in (a_sm, wbr_t, wbi_t) + ((cre_t, ncim_t) if with_dy else ())])
    args = (([dys] if with_dy else []) + [u_arr, hr, hi, hr, hi] + ([tr, ti] if with_dy else [])
            + [hbound, g_in, a_sm, wbr_t, wbi_t] + ([cre_t, ncim_t] if with_dy else []))
    out_shape = (jax.ShapeDtypeStruct((l, w), F32), wsh, wsh) + ((csh, csh) if with_dy else ()) + (sm, sm)
    out_specs = (row(0), full(wsh), full(wsh)) + ((full(csh), full(csh)) if with_dy else ()) + (smspec, smspec)
    return pl.pallas_call(
        body, name=name, out_shape=out_shape, grid=(n,), in_specs=in_specs, out_specs=out_specs,
        scratch_shapes=[pltpu.VMEM((t, nsr, LANES), F32)] * 2 + [pltpu.VMEM((t, nsr * LANES), MXU_DTYPE)] * 2
        + [pltpu.VMEM((2, nsr, LANES), F32)],
        compiler_params=_cparams("arbitrary"))(*args)


def _glu_fwd(y0, y1, proj, cat, d_skip, w_glu, b_glu, w):
    l = y0.shape[0]
    tl = _tile(l, 256)

    def body(y0_ref, y1_ref, u_ref, z_ref, cat_in, d_ref, wg_ref, bg_ref, ys_ref, cat_ref):
        del cat_in
        ys = y0_ref[...] + y1_ref[...] + d_ref[...] * u_ref[...]
        ys_ref[...] = ys
        gy = _gelu(ys)
        s = _mxu_dot(gy, wg_ref[...]) + bg_ref[...]
        cat_ref[...] = (gy * jax.nn.sigmoid(s) * jax.nn.silu(z_ref[...])).astype(cat_ref.dtype)

    row = pl.BlockSpec((tl, w), lambda i: (i, 0))
    vec = pl.BlockSpec((1, w), lambda i: (0, 0))
    return pl.pallas_call(
        body, name="glu_fwd", out_shape=(jax.ShapeDtypeStruct((l, w), F32), jax.ShapeDtypeStruct(cat.shape, cat.dtype)),
        grid=(l // tl,),
        in_specs=[row, row, pl.BlockSpec((tl, w), lambda i: (i, 3)), pl.BlockSpec((tl, w), lambda i: (i, 4)), HBM,
                  vec, pl.BlockSpec((w, w), lambda i: (0, 0)), vec],
        out_specs=(row, pl.BlockSpec((tl, w), lambda i: (i, 1))), input_output_aliases={4: 1},
        compiler_params=_cparams("parallel"))(y0, y1, proj, proj, cat, d_skip, w_glu, b_glu)


def _glu_bwd(dcat, ys, proj, d_skip, w_glu, b_glu, w):
    l = ys.shape[0]
    tl = _tile(l, 256)

    def body(dy_ref, ys_ref, u_ref, z_ref, d_ref, wg_ref, bg_ref, dys_ref, dp_ref, dbg_ref, dd_ref, dwg_ref):
        i = pl.program_id(0)
        ys_t = ys_ref[...]
        z = z_ref[...]
        dyb = dy_ref[...].astype(F32)
        gy = _gelu(ys_t)
        sg = jax.nn.sigmoid(_mxu_dot(gy, wg_ref[...]) + bg_ref[...])
        dp_ref[...] = (dyb * gy * sg * _silu_grad(z)).astype(dp_ref.dtype)
        dglu = dyb * jax.nn.silu(z)
        ds = dglu * gy * sg * (1.0 - sg)
        dgy = dglu * sg + _mxu_dot(ds, wg_ref[...], _NT)
        dys_t = dgy * _gelu_grad(ys_t)
        dys_ref[...] = dys_t

        @pl.when(i == 0)
        def _():
            for r in (dbg_ref, dd_ref, dwg_ref):
                r[...] = jnp.zeros_like(r)

        dbg_ref[...] += jnp.sum(ds, axis=0, keepdims=True)
        dd_ref[...] += jnp.sum(dys_t * u_ref[...], axis=0, keepdims=True)
        dwg_ref[...] += _mxu_dot(gy, ds, _TN)

    row = pl.BlockSpec((tl, w), lambda i: (i, 0))
    vec = pl.BlockSpec((1, w), lambda i: (0, 0))
    mat = pl.BlockSpec((w, w), lambda i: (0, 0))
    v = jax.ShapeDtypeStruct((1, w), F32)
    return pl.pallas_call(
        body, name="glu_bwd",
        out_shape=(jax.ShapeDtypeStruct((l, w), F32), jax.ShapeDtypeStruct((l, 5 * w), MXU_DTYPE), v, v,
                   jax.ShapeDtypeStruct((w, w), F32)),
        grid=(l // tl,),
        in_specs=[pl.BlockSpec((tl, w), lambda i: (i, 1)), row, pl.BlockSpec((tl, w), lambda i: (i, 3)),
                  pl.BlockSpec((tl, w), lambda i: (i, 4)), vec, mat, vec],
        out_specs=(row, pl.BlockSpec((tl, w), lambda i: (i, 4)), vec, vec, mat),
        compiler_params=_cparams("arbitrary"))(dcat, ys, proj, proj, d_skip, w_glu, b_glu)


def _dub_combine(dys, du0, du1, d_skip, dproj, w):
    l = dys.shape[0]
    tl = _tile(l, 512)

    def body(dys_ref, a_ref, b_ref, d_ref, dp_in, dp_ref):
        del dp_in
        dp_ref[...] = (dys_ref[...] * d_ref[...] + a_ref[...] + b_ref[...]).astype(dp_ref.dtype)

    row = pl.BlockSpec((tl, w), lambda i: (i, 0))
    return pl.pallas_call(
        body, name="dub_combine", out_shape=jax.ShapeDtypeStruct(dproj.shape, dproj.dtype), grid=(l // tl,),
        in_specs=[row, row, row, pl.BlockSpec((1, w), lambda i: (0, 0)), HBM],
        out_specs=pl.BlockSpec((tl, w), lambda i: (i, 3)), input_output_aliases={4: 0},
        compiler_params=_cparams("parallel"))(dys, du0, du1, d_skip, dproj)


def _add2(a, b, name):
    l, w = a.shape
    tl = _tile(l, 512)

    def body(a_ref, b_ref, o_ref):
        o_ref[...] = a_ref[...] + b_ref[...]

    row = pl.BlockSpec((tl, w), lambda i: (i, 0))
    return pl.pallas_call(body, name=name, out_shape=jax.ShapeDtypeStruct((l, w), F32), grid=(l // tl,),
                          in_specs=[row, row], out_specs=row, compiler_params=_cparams("parallel"))(a, b)


def _adamw_nd(w, m, v, g, name):
    shape = w.shape
    lead = math.prod(shape[:-2]) if len(shape) > 2 else 1
    b, c = (shape[-2], shape[-1]) if len(shape) >= 2 else (1, shape[-1])
    t3 = (lead, b, c)
    padded_row = -(-b // 8) * 8 * -(-c // LANES) * LANES * 4
    ta = _tile(lead, max(1, (2 << 20) // padded_row))

    def body(w_ref, m_ref, v_ref, g_ref, d_ref, mo_ref, vo_ref):
        gv = g_ref[...]
        mn = ADAM_B1 * m_ref[...] + (1.0 - ADAM_B1) * gv
        vn = ADAM_B2 * v_ref[...] + (1.0 - ADAM_B2) * (gv * gv)
        m_hat = mn / (1.0 - ADAM_B1 ** ADAM_STEP)
        v_hat = vn / (1.0 - ADAM_B2 ** ADAM_STEP)
        d_ref[...] = -ADAM_LR * (m_hat / (jnp.sqrt(v_hat) + ADAM_EPS) + ADAM_WD * w_ref[...])
        mo_ref[...] = mn
        vo_ref[...] = vn

    blk = pl.BlockSpec((ta, b, c), lambda i: (i, 0, 0))
    s = jax.ShapeDtypeStruct(t3, F32)
    outs = pl.pallas_call(body, name=name, out_shape=(s, s, s), grid=(lead // ta,), in_specs=[blk] * 4, out_specs=(blk,) * 3,
                          compiler_params=_cparams("parallel"))(*[a.reshape(t3) for a in (w, m, v, g)])
    return tuple(o.reshape(shape) for o in outs)


def _adamw(w, m, v, gparts, name):
    r, c = w.shape
    np_ = gparts.shape[0]
    tr = _tile(r, max(8, (1 << 18) // c), 8)

    def body(w_ref, m_ref, v_ref, g_ref, go_ref, d_ref, mo_ref, vo_ref):
        g = g_ref[0].astype(F32)
        for p in range(1, np_):
            g = g + g_ref[p].astype(F32)
        mn = ADAM_B1 * m_ref[...] + (1.0 - ADAM_B1) * g
        vn = ADAM_B2 * v_ref[...] + (1.0 - ADAM_B2) * (g * g)
        m_hat = mn / (1.0 - ADAM_B1 ** ADAM_STEP)
        v_hat = vn / (1.0 - ADAM_B2 ** ADAM_STEP)
        go_ref[...] = g
        d_ref[...] = -ADAM_LR * (m_hat / (jnp.sqrt(v_hat) + ADAM_EPS) + ADAM_WD * w_ref[...])
        mo_ref[...] = mn
        vo_ref[...] = vn

    row = pl.BlockSpec((tr, c), lambda i: (i, 0))
    s = jax.ShapeDtypeStruct((r, c), F32)
    return pl.pallas_call(body, name=name, out_shape=(s, s, s, s), grid=(r // tr,),
                          in_specs=[row, row, row, pl.BlockSpec((np_, tr, c), lambda i: (0, i, 0))],
                          out_specs=(row, row, row, row), compiler_params=_cparams("parallel"))(w, m, v, gparts)


def _sum_slots(parts, name):
    np_, r, c = parts.shape

    def body(p_ref, o_ref):
        g = p_ref[0]
        for p in range(1, np_):
            g = g + p_ref[p]
        o_ref[...] = g

    return pl.pallas_call(body, name=name, out_shape=jax.ShapeDtypeStruct((r, c), F32), compiler_params=_cparams())(parts)


def _block_diag(x, gb):
    nd, g, a, b = x.shape
    eye = jnp.eye(gb, dtype=x.dtype)
    y = jnp.einsum("dkgab,gh->dkgahb", x.reshape(nd, g // gb, gb, a, b), eye)
    return y.reshape(nd, g // gb, gb * a, gb * b)


def _block_diag_extract(y, gb, a, b):
    nd, nbk = y.shape[:2]
    eye = jnp.eye(gb, dtype=y.dtype)
    x = jnp.einsum("dkgahb,gh->dkgab", y.reshape(nd, nbk, gb, a, gb, b), eye)
    return x.reshape(nd, nbk * gb, a, b)


def kernel(x, c, ctx, c_ctx, w_ada, b_ada, w_in, sgu_ln_g, sgu_ln_b, w_spatial, b_spatial, s5_lam_re, s5_lam_im, s5_log_step, s5_b_re, s5_b_im, s5_c_re, s5_c_im, s5_d, w_glu, b_glu, w_out, ln_g, ln_b, loss_target, m_c_ctx, m_w_ada, m_b_ada, m_w_in, m_sgu_ln_g, m_sgu_ln_b, m_w_spatial, m_b_spatial, m_s5_lam_re, m_s5_lam_im, m_s5_log_step, m_s5_b_re, m_s5_b_im, m_s5_c_re, m_s5_c_im, m_s5_d, m_w_glu, m_b_glu, m_w_out, m_ln_g, m_ln_b, v_c_ctx, v_w_ada, v_b_ada, v_w_in, v_sgu_ln_g, v_sgu_ln_b, v_w_spatial, v_b_spatial, v_s5_lam_re, v_s5_lam_im, v_s5_log_step, v_s5_b_re, v_s5_b_im, v_s5_c_re, v_s5_c_im, v_s5_d, v_w_glu, v_b_glu, v_w_out, v_ln_g, v_ln_b):
    small_names = ["c_ctx", "b_ada", "sgu_ln_g", "sgu_ln_b", "w_spatial", "b_spatial", "s5_lam_re", "s5_lam_im",
                   "s5_log_step", "s5_b_re", "s5_b_im", "s5_c_re", "s5_c_im", "s5_d", "b_glu", "ln_g", "ln_b"]
    env = dict(locals())
    x2, tgt, ctx2 = x[0], loss_target[0], ctx[0]
    l, d = x2.shape
    lc = ctx2.shape[0]
    w = d // 2
    nh = w // HEAD_DIM_A
    nd, g_s5, p_s5, c_s5 = s5_b_re.shape[1:]
    ns = g_s5 * p_s5
    nsr = ns // LANES
    gb = LANES // c_s5
    me = _index(_mesh_pos())
    ada_cols = w_ada.shape[2]

    srows = _silu_rows(c, c_ctx)
    srows_all = _all_gather(srows, 0, "gather_silu")
    s_mat = jnp.concatenate([srows_all[0::8], srows_all[1:2], jnp.zeros((7, d), F32)], axis=0)
    mod_part = _small_dot(s_mat, w_ada[0], "nn", "mod_cols")
    mod_all = _all_gather(mod_part, 1, "gather_mod") + b_ada
    hw_in, tok_a = _exchange_start(w_in[0].astype(MXU_DTYPE), 1, "gather", "start_gather_w_in", (SIBLING,) + SAME_CORE_PEERS)
    mod_all = mod_all + tok_a[0, 0]
    mod_x = lax.dynamic_slice_in_dim(mod_all, me, 1, axis=0)
    mod_c = mod_all[8:9]
    shift_x, scale_x, gate_x = mod_x[:, :d], mod_x[:, d:2 * d], mod_x[:, 2 * d:]
    shift_c, scale_c = mod_c[:, :d], mod_c[:, d:2 * d]

    lr, li = s5_lam_re[0][:, :, None, :], s5_lam_im[0][:, :, None, :]
    ls = s5_log_step[0][:, :, None, None]
    swapped = ("s5_b_re", "s5_b_im")
    for nm in swapped:
        for pre in ("", "m_", "v_"):
            env[pre + nm] = jnp.swapaxes(env[pre + nm], -1, -2)
    br_t, bi_t = env["s5_b_re"][0], env["s5_b_im"][0]
    ab_re, ab_im, bb_re, bb_im = _disc_fwd(lr, li, ls, br_t, bi_t)
    a_sm = jnp.stack([ab_re, ab_im], axis=1).reshape(nd, 2, nsr, LANES)
    wbr = _block_diag(bb_re, gb).astype(MXU_DTYPE)
    wbi = _block_diag(bb_im, gb).astype(MXU_DTYPE)
    cre_t = _block_diag(s5_c_re[0], gb).astype(MXU_DTYPE)
    ncim_t = _block_diag(-s5_c_im[0], gb).astype(MXU_DTYPE)
    cre, ncim = jnp.swapaxes(cre_t, 2, 3), jnp.swapaxes(ncim_t, 2, 3)
    wbr_t, wbi_t = jnp.swapaxes(wbr, 2, 3), jnp.swapaxes(wbi, 2, 3)
    d_skip = s5_d

    xm = _ln_mod(x2, shift_x, scale_x, "ln_mod_x")
    cm = _ln_mod(ctx2, shift_c, scale_c, "ln_mod_ctx")
    ready = xm[:8, :LANES].astype(F32) + cm[:8, :LANES].astype(F32) + cre[0, 0, :8, :].astype(F32)
    hw_in2, tok_b = _forward_start(_exchange_wait(hw_in, ready, "wait_gather_w_in"), 1, "start_forward_w_in")
    w_in_f = _forward_wait(hw_in2, tok_b, "wait_forward_w_in")
    hw_glu, tok_c = _exchange_start(w_glu[0].astype(MXU_DTYPE), 0, "gather", "start_gather_w_glu")
    hw_out, tok_o = _exchange_start(w_out[0].astype(MXU_DTYPE), 0, "gather", "start_gather_w_out")
    proj = _matmul(xm, w_in_f, mode="nn", name="proj", bm=512, bn=1024, bk=d, dep=tok_c + tok_o)
    ub_c = _matmul(cm, w_in_f, mode="nn", name="proj_ctx", bm=256, bn=w, bk=d, b_n0=3 * w, n=w)
    bsf = jnp.repeat(b_spatial[0].T, HEAD_DIM_A, axis=1)
    ws = w_spatial[0]
    cat = _ga_fwd(proj, sgu_ln_g, sgu_ln_b, ws, bsf, w)
    zeros_state = jnp.zeros((2, nsr, LANES), F32)
    s5c, s5l = [], []
    for dr in range(nd):
        s5c.append(_s5_fwd(ub_c, 0, w, zeros_state, a_sm, wbr, wbi, cre, ncim, dr, f"s5_fwd_ctx{dr}"))
        s5l.append(_s5_fwd(proj, 3, w, s5c[dr][5], a_sm, wbr, wbi, cre, ncim, dr, f"s5_fwd{dr}"))
    w_glu_f = _exchange_wait(hw_glu, s5l[1][0], "wait_gather_w_glu")
    ys, cat = _glu_fwd(s5l[0][0], s5l[1][0], proj, cat, d_skip, w_glu_f, b_glu, w)
    w_out_f = _exchange_wait(hw_out, ys, "wait_gather_w_out")
    out = _matmul(cat, w_out_f, mode="nn", name="out_proj", bm=512, bn=1024, bk=2 * w)
    loss_row, dout, dx_res, dgate, dln_g, dln_b = _post_ln_loss(x2, out, gate_x, ln_g, ln_b, tgt)

    dcat = _matmul(dout, w_out_f, mode="nt", name="d_cat", bm=512, bn=1024, bk=d, out_dtype=MXU_DTYPE)
    dw_out = _matmul(cat, dout, mode="tn", name="d_w_out", bm=1024, bn=1024, bk=2048, out_dtype=MXU_DTYPE)
    hg_out, tok_d = _exchange_start(dw_out, 0, "a2a", "start_a2a_d_w_out")
    dys, dproj, db_glu, dd_skip, dw_glu = _glu_bwd(dcat, ys, proj, d_skip + tok_d[0, 0], w_glu_f, b_glu, w)
    hg_glu, tok_e = _exchange_start(dw_glu.astype(MXU_DTYPE), 0, "a2a", "start_a2a_d_w_glu")
    zeros_state = zeros_state + tok_e[0, 0]
    du_l, du_c, dwbr, dwbi, dcre, dncim, da_sm = [], [], [], [], [], [], []
    nbk, spb = w // LANES, gb * p_s5
    for dr in range(nd):
        bl = _s5_bwd(dys, proj, 3, w, *s5l[dr][1:5], s5c[dr][5], zeros_state, a_sm, wbr_t, wbi_t,
                     cre_t, ncim_t, dr, f"s5_bwd{dr}")
        bc = _s5_bwd(None, ub_c, 0, w, s5c[dr][1], s5c[dr][2], None, None, zeros_state, bl[6], a_sm, wbr_t, wbi_t,
                     None, None, dr, f"s5_bwd_ctx{dr}")
        du_l.append(bl[0])
        du_c.append(bc[0])
        dwbr.append(_add2(bl[1].reshape(nbk * LANES, spb), bc[1].reshape(nbk * LANES, spb), f"sum_dwbr{dr}"))
        dwbi.append(_add2(bl[2].reshape(nbk * LANES, spb), bc[2].reshape(nbk * LANES, spb), f"sum_dwbi{dr}"))
        dcre.append(bl[3])
        dncim.append(bl[4])
        da_sm.append(_add2(bl[5].reshape(2 * nsr, LANES), bc[3].reshape(2 * nsr, LANES), f"sum_da{dr}"))
    dproj = _dub_combine(dys, du_l[0], du_l[1], d_skip, dproj, w)
    dub_c = _add2(du_c[0], du_c[1], "dub_ctx")
    dwbr = jnp.stack(dwbr).reshape(nd, nbk, LANES, spb)
    dwbi = jnp.stack(dwbi).reshape(nd, nbk, LANES, spb)
    dcre, dncim = jnp.stack(dcre), jnp.stack(dncim)
    da_sm = jnp.stack(da_sm).reshape(nd, 2, g_s5, p_s5)
    dproj, dsg, dsb, dws, dbsf = _ga_bwd(proj, dcat, dproj, sgu_ln_g, sgu_ln_b, ws, bsf, w)

    dbb_re = _block_diag_extract(dwbr, gb, c_s5, p_s5)
    dbb_im = _block_diag_extract(dwbi, gb, c_s5, p_s5)
    dc_re = jnp.swapaxes(_block_diag_extract(dcre, gb, p_s5, c_s5), 2, 3)
    dc_im = -jnp.swapaxes(_block_diag_extract(dncim, gb, p_s5, c_s5), 2, 3)
    dlr, dli, dls, db_re, db_im = _disc_bwd(lr, li, ls, br_t, bi_t, da_sm[:, 0:1].reshape(nd, g_s5, 1, p_s5),
                                            da_sm[:, 1:2].reshape(nd, g_s5, 1, p_s5), dbb_re, dbb_im)
    expand = (jnp.arange(w)[:, None] // HEAD_DIM_A == jnp.arange(LANES)[None, :]).astype(F32)
    db_sp = _lane_group_sum(dbsf, expand, "d_b_spatial")[:, :nh].T

    local = {"sgu_ln_g": dsg, "sgu_ln_b": dsb, "w_spatial": dws, "b_spatial": db_sp,
             "s5_lam_re": dlr, "s5_lam_im": dli, "s5_log_step": dls, "s5_b_re": db_re, "s5_b_im": db_im,
             "s5_c_re": dc_re, "s5_c_im": dc_im, "s5_d": dd_skip, "b_glu": db_glu, "ln_g": dln_g, "ln_b": dln_b}
    reduced = [n for n in small_names if n in local]
    loss_part = (0.5 / d) * jnp.sum(loss_row)
    flat = jnp.concatenate([loss_part.reshape(1)] + [local[n].reshape(-1) for n in reduced])
    unit = N_DEV * 8 * LANES
    total = -(-flat.shape[0] // unit) * unit
    flat = jnp.pad(flat, (0, total - flat.shape[0])).reshape(N_DEV * 8, total // (N_DEV * 8))
    h_small, tok_s = _exchange_start(flat, 0, "a2a", "start_a2a_small")

    dw_in = _matmul(xm, dproj, mode="tn", name="d_w_in", bm=1024, bn=1280, bk=2048, out_dtype=MXU_DTYPE, dep=tok_s)
    dw_in = _matmul(cm, dub_c, mode="tn", name="d_w_in_ctx", bm=1024, bn=w, bk=lc, acc_in=dw_in, acc_n0=3 * w,
                    out_dtype=MXU_DTYPE)
    hg_in, tok_f = _exchange_start(dw_in, 1, "a2a", "start_a2a_d_w_in")
    mine = _sum_slots(_exchange_wait(h_small, dw_in, "wait_a2a_small"), "sum_small")
    h_sums, tok_g = _exchange_start(mine, 0, "gather", "start_gather_small")
    dxm = _matmul(dproj, w_in_f, mode="nt", name="d_xm", bm=512, bn=1024, bk=5 * w, dep=tok_f + tok_g,
                  out_dtype=MXU_DTYPE)
    dcm = _matmul(dub_c, w_in_f, mode="nt", name="d_cm", bm=256, bn=1024, bk=w, b_k0=3 * w, k=w)
    grad_x, dshift_x, dscale_x = _ln_mod_bwd(x2, dxm, scale_x, dx_res, "ln_mod_x_bwd")
    _, dshift_c, dscale_c = _ln_mod_bwd(ctx2, dcm, scale_c, None, "ln_mod_ctx_bwd")

    dmod_rows = jnp.concatenate([jnp.concatenate([dshift_x, dscale_x, dgate], axis=1),
                                 jnp.concatenate([dshift_c, dscale_c, jnp.zeros((1, d), F32)], axis=1),
                                 jnp.zeros((6, 3 * d), F32)], axis=0)
    dmod_all = _all_gather(dmod_rows, 0, "gather_dmod")
    dmod_ctx = _sum_slots(dmod_all[1::8].reshape(N_DEV, 1, 3 * d), "sum_dmod_ctx")
    dmod_mat = jnp.concatenate([dmod_all[0::8], dmod_ctx, jnp.zeros((7, 3 * d), F32)], axis=0)
    db_ada = _sum_slots(dmod_mat[:9].reshape(9, 1, 3 * d), "sum_db_ada")
    dmod_mine = lax.dynamic_slice_in_dim(dmod_mat, me * ada_cols, ada_cols, axis=1)
    dw_ada = _small_dot(s_mat, dmod_mine, "tn", "d_w_ada")
    dsilu_cc = _small_dot(dmod_mine[8:16], w_ada[0], "nt", "d_silu_cctx")[0:1]
    dc_ctx_part = dsilu_cc * _silu_grad(c_ctx.reshape(1, d))
    dc_ctx_rows = jnp.concatenate([dc_ctx_part, jnp.zeros((7, d), F32)], axis=0)
    dc_ctx_all = _all_gather(dc_ctx_rows, 0, "gather_d_c_ctx")
    dc_ctx = _sum_slots(dc_ctx_all[0::8].reshape(N_DEV, 1, d), "sum_d_c_ctx").reshape(d)

    summed = _exchange_wait(h_sums, dc_ctx, "wait_gather_small").reshape(-1)
    loss = summed[0]
    grads, off = {"b_ada": db_ada, "c_ctx": dc_ctx}, 1
    for n in reduced:
        size = math.prod(env[n].shape)
        grads[n] = summed[off:off + size].reshape(env[n].shape)
        off += size

    gp_w_out = _exchange_wait(hg_out, summed, "wait_a2a_d_w_out")
    gp_w_glu = _exchange_wait(hg_glu, summed, "wait_a2a_d_w_glu")
    gp_w_in = _exchange_wait(hg_in, summed, "wait_a2a_d_w_in")
    big = {
        "w_ada": _adamw(w_ada[0], m_w_ada[0], v_w_ada[0], dw_ada[None], "adamw_w_ada"),
        "w_in": _adamw(w_in[0], m_w_in[0], v_w_in[0], gp_w_in, "adamw_w_in"),
        "w_glu": _adamw(w_glu[0], m_w_glu[0], v_w_glu[0], gp_w_glu, "adamw_w_glu"),
        "w_out": _adamw(w_out[0], m_w_out[0], v_w_out[0], gp_w_out, "adamw_w_out"),
    }
    res = {n: tuple(a[None] for a in big[n]) for n in big}
    for n in small_names:
        res[n] = (grads[n],) + _adamw_nd(env[n], env["m_" + n], env["v_" + n], grads[n], "adamw_" + n)
        if n in swapped:
            res[n] = tuple(jnp.swapaxes(a, -1, -2) for a in res[n])

    order = ["c_ctx", "w_ada", "b_ada", "w_in", "sgu_ln_g", "sgu_ln_b", "w_spatial", "b_spatial", "s5_lam_re", "s5_lam_im",
             "s5_log_step", "s5_b_re", "s5_b_im", "s5_c_re", "s5_c_im", "s5_d", "w_glu", "b_glu", "w_out", "ln_g", "ln_b"]
    return (loss, grad_x[None], *[res[n][0] for n in order], *[res[n][1] for n in order],
            *[res[n][2] for n in order], *[res[n][3] for n in order])
```

```python
import functools
import math

import jax
import jax.numpy as jnp
from jax import lax
from jax.experimental import pallas as pl
from jax.experimental.pallas import tpu as pltpu

F32 = jnp.float32
MXU_DTYPE = jnp.bfloat16
N_DEV = 8
MESH_ID = pl.DeviceIdType.MESH
LN_EPS = 1e-6
DEPTH = 1
ALPHA = (2.0 * DEPTH) ** 0.25
CHUNK = 128
HEAD_DIM_A = 128
ADAM_LR, ADAM_B1, ADAM_B2, ADAM_EPS, ADAM_WD, ADAM_STEP = 0.001, 0.9, 0.999, 1e-08, 0.01, 10
LANES = 128
SCAN_UNROLL = 8
VMEM_LIMIT = 56 * 1024 * 1024
HBM = pl.BlockSpec(memory_space=pl.ANY)


def _cparams(*sem):
    return pltpu.CompilerParams(dimension_semantics=sem if sem else None, vmem_limit_bytes=VMEM_LIMIT)


def _tile(n, pref, mult=1):
    if n <= pref:
        return n
    t = pref - pref % mult
    while n % t:
        t -= mult
    return t


def _gelu(x):
    return 0.5 * x * (1.0 + lax.erf(x * (1.0 / math.sqrt(2.0))))


def _gelu_grad(x):
    return 0.5 * (1.0 + lax.erf(x * (1.0 / math.sqrt(2.0)))) + x * jnp.exp(-0.5 * x * x) * (1.0 / math.sqrt(2.0 * math.pi))


def _silu_grad(x):
    s = jax.nn.sigmoid(x)
    return s * (1.0 + x * (1.0 - s))


def _mxu_dot(a, b, dims=(((1,), (0,)), ((), ()))):
    return lax.dot_general(a.astype(MXU_DTYPE), b.astype(MXU_DTYPE), dims, preferred_element_type=F32)


_NT = (((1,), (1,)), ((), ()))
_TN = (((0,), (0,)), ((), ()))


def _mesh_pos():
    return lax.axis_index("x"), lax.axis_index("y"), lax.axis_index("c")


def _peer(pos, r):
    x, y, c = pos
    return ((1 - x) if r & 4 else x, (1 - y) if r & 2 else y, (1 - c) if r & 1 else c)


def _index(pos):
    return 4 * pos[0] + 2 * pos[1] + pos[2]


def _slice_of(ref, axis, idx, size):
    start = idx * size
    if axis == 0:
        return ref.at[pl.ds(start, size)]
    return ref.at[:, pl.ds(start, size)]


def _all_gather(x, axis, name):
    size = x.shape[axis]
    out_shape = tuple(s * N_DEV if a == axis else s for a, s in enumerate(x.shape))

    def body(x_ref, o_ref, send_sems, recv_sems, local_sem):
        me = _mesh_pos()
        mine = pltpu.make_async_copy(x_ref, _slice_of(o_ref, axis, _index(me), size), local_sem)
        mine.start()

        def copy(r, block):
            return pltpu.make_async_remote_copy(
                src_ref=x_ref, dst_ref=_slice_of(o_ref, axis, _index(block), size),
                send_sem=send_sems.at[r - 1], recv_sem=recv_sems.at[r - 1],
                device_id=_peer(me, r), device_id_type=MESH_ID)

        sends = [copy(r, me) for r in range(1, N_DEV)]
        for cp in sends:
            cp.start()
        for r in range(1, N_DEV):
            copy(r, _peer(me, r)).wait_recv()
        for cp in sends:
            cp.wait_send()
        mine.wait()

    return pl.pallas_call(
        body, name=name, out_shape=jax.ShapeDtypeStruct(out_shape, x.dtype),
        in_specs=[HBM], out_specs=HBM,
        scratch_shapes=[pltpu.SemaphoreType.DMA((N_DEV - 1,)), pltpu.SemaphoreType.DMA((N_DEV - 1,)),
                        pltpu.SemaphoreType.DMA],
    )(x)


def _all_to_all(x, axis, name):
    size = x.shape[axis] // N_DEV
    slot = tuple(size if a == axis else s for a, s in enumerate(x.shape))

    def body(x_ref, o_ref, send_sems, recv_sems, local_sem):
        me = _mesh_pos()
        mine = pltpu.make_async_copy(_slice_of(x_ref, axis, _index(me), size), o_ref.at[_index(me)], local_sem)
        mine.start()

        def copy(r, sender, receiver):
            return pltpu.make_async_remote_copy(
                src_ref=_slice_of(x_ref, axis, _index(receiver), size), dst_ref=o_ref.at[_index(sender)],
                send_sem=send_sems.at[r - 1], recv_sem=recv_sems.at[r - 1],
                device_id=_peer(me, r), device_id_type=MESH_ID)

        sends = [copy(r, me, _peer(me, r)) for r in range(1, N_DEV)]
        for cp in sends:
            cp.start()
        for r in range(1, N_DEV):
            copy(r, _peer(me, r), me).wait_recv()
        for cp in sends:
            cp.wait_send()
        mine.wait()

    return pl.pallas_call(
        body, name=name, out_shape=jax.ShapeDtypeStruct((N_DEV,) + slot, x.dtype),
        in_specs=[HBM], out_specs=HBM,
        scratch_shapes=[pltpu.SemaphoreType.DMA((N_DEV - 1,)), pltpu.SemaphoreType.DMA((N_DEV - 1,)),
                        pltpu.SemaphoreType.DMA],
    )(x)


_SEM = pl.BlockSpec(memory_space=pltpu.SEMAPHORE)
_HBM = pl.BlockSpec(memory_space=pltpu.HBM)
_EFFECT = pltpu.SideEffectType.DATAFLOW_SIDE_EFFECTING
ALL_PEERS = tuple(range(1, N_DEV))
SIBLING = 1
SAME_CORE_PEERS = (2, 4, 6)


def _exchange_copy(kind, x_ref, land_ref, axis, size, send_sems, recv_sems, me, rels, q, arriving):
    peer = _peer(me, rels[q])
    sender, receiver = (peer, me) if arriving else (me, peer)
    if kind == "gather":
        src, dst = x_ref, _slice_of(land_ref, axis, _index(sender), size)
    else:
        src, dst = _slice_of(x_ref, axis, _index(receiver), size), land_ref.at[_index(sender)]
    return pltpu.make_async_remote_copy(src_ref=src, dst_ref=dst, send_sem=send_sems.at[q], recv_sem=recv_sems.at[q],
                                        device_id=peer, device_id_type=MESH_ID)


def _local_copy(kind, x_ref, land_ref, axis, size, me, local_sem):
    if kind == "gather":
        return pltpu.make_async_copy(x_ref, _slice_of(land_ref, axis, _index(me), size), local_sem)
    return pltpu.make_async_copy(_slice_of(x_ref, axis, _index(me), size), land_ref.at[_index(me)], local_sem)


def _exchange_start(x, axis, kind, name, rels=ALL_PEERS):
    size = x.shape[axis] if kind == "gather" else x.shape[axis] // N_DEV
    if kind == "gather":
        land_shape = tuple(s * N_DEV if a == axis else s for a, s in enumerate(x.shape))
    else:
        land_shape = (N_DEV,) + tuple(size if a == axis else s for a, s in enumerate(x.shape))

    def body(x_ref, land_ref, send_sems, recv_sems, local_sem, x_thru, land_thru, token):
        del x_thru, land_thru
        me = _mesh_pos()
        _local_copy(kind, x_ref, land_ref, axis, size, me, local_sem).start()
        for q in range(len(rels)):
            _exchange_copy(kind, x_ref, land_ref, axis, size, send_sems, recv_sems, me, rels, q, False).start()
        token[...] = jnp.zeros_like(token)

    sems = pltpu.SemaphoreType.DMA((len(rels),))
    send_sems, recv_sems, local_sem, x_thru, land_thru, token = pl.pallas_call(
        body, name=name,
        out_shape=(sems, sems, pltpu.SemaphoreType.DMA(()), pltpu.HBM(x.shape, x.dtype), pltpu.HBM(land_shape, x.dtype),
                   jax.ShapeDtypeStruct((8, LANES), F32)),
        in_specs=(_HBM, _HBM), out_specs=(_SEM, _SEM, _SEM, _HBM, _HBM, pl.BlockSpec(memory_space=pltpu.VMEM)),
        input_output_aliases={0: 3, 1: 4}, compiler_params=pltpu.CompilerParams(has_side_effects=_EFFECT),
    )(pltpu.with_memory_space_constraint(x, pltpu.HBM),
      pltpu.with_memory_space_constraint(lax.empty(land_shape, x.dtype), pltpu.HBM))
    return (kind, axis, size, rels, send_sems, recv_sems, local_sem, x_thru, land_thru), token


def _exchange_wait(handle, after, name):
    kind, axis, size, rels, send_sems, recv_sems, local_sem, x_thru, land_thru = handle

    def body(x_ref, land_ref, send_sems, recv_sems, local_sem, after_ref, x_dead, got_ref):
        del after_ref, x_dead, got_ref
        me = _mesh_pos()
        _local_copy(kind, x_ref, land_ref, axis, size, me, local_sem).wait()
        for q in range(len(rels)):
            _exchange_copy(kind, x_ref, land_ref, axis, size, send_sems, recv_sems, me, rels, q, False).wait_send()
        for q in range(len(rels)):
            _exchange_copy(kind, x_ref, land_ref, axis, size, send_sems, recv_sems, me, rels, q, True).wait_recv()

    return pl.pallas_call(
        body, name=name, out_shape=(pltpu.HBM(x_thru.shape, x_thru.dtype), pltpu.HBM(land_thru.shape, land_thru.dtype)),
        in_specs=(_HBM, _HBM, _SEM, _SEM, _SEM, HBM), out_specs=(_HBM, _HBM), input_output_aliases={0: 0, 1: 1},
        compiler_params=pltpu.CompilerParams(has_side_effects=_EFFECT),
    )(x_thru, land_thru, send_sems, recv_sems, local_sem, after)[1]


def _forward_copy(land_ref, axis, size, send_sems, recv_sems, me, q, arriving):
    sibling = _peer(me, SIBLING)
    owner = _peer(sibling if arriving else me, SAME_CORE_PEERS[q])
    block = _slice_of(land_ref, axis, _index(owner), size)
    return pltpu.make_async_remote_copy(src_ref=block, dst_ref=block, send_sem=send_sems.at[q], recv_sem=recv_sems.at[q],
                                        device_id=sibling, device_id_type=MESH_ID)


def _forward_start(land, axis, name):
    size = land.shape[axis] // N_DEV

    def body(land_ref, send_sems, recv_sems, land_thru, token):
        del land_thru
        me = _mesh_pos()
        for q in range(len(SAME_CORE_PEERS)):
            _forward_copy(land_ref, axis, size, send_sems, recv_sems, me, q, False).start()
        token[...] = jnp.zeros_like(token)

    sems = pltpu.SemaphoreType.DMA((len(SAME_CORE_PEERS),))
    send_sems, recv_sems, land_thru, token = pl.pallas_call(
        body, name=name, out_shape=(sems, sems, pltpu.HBM(land.shape, land.dtype), jax.ShapeDtypeStruct((8, LANES), F32)),
        in_specs=(_HBM,), out_specs=(_SEM, _SEM, _HBM, pl.BlockSpec(memory_space=pltpu.VMEM)),
        input_output_aliases={0: 2}, compiler_params=pltpu.CompilerParams(has_side_effects=_EFFECT),
    )(land)
    return (axis, size, send_sems, recv_sems, land_thru), token


def _forward_wait(handle, after, name):
    axis, size, send_sems, recv_sems, land_thru = handle

    def body(land_ref, send_sems, recv_sems, after_ref, got_ref):
        del after_ref, got_ref
        me = _mesh_pos()
        for q in range(len(SAME_CORE_PEERS)):
            _forward_copy(land_ref, axis, size, send_sems, recv_sems, me, q, False).wait_send()
        for q in range(len(SAME_CORE_PEERS)):
            _forward_copy(land_ref, axis, size, send_sems, recv_sems, me, q, True).wait_recv()

    return pl.pallas_call(
        body, name=name, out_shape=pltpu.HBM(land_thru.shape, land_thru.dtype),
        in_specs=(_HBM, _SEM, _SEM, HBM), out_specs=_HBM, input_output_aliases={0: 0},
        compiler_params=pltpu.CompilerParams(has_side_effects=_EFFECT),
    )(land_thru, send_sems, recv_sems, after)


def _matmul(a, b, *, mode, name, bm=512, bn=512, bk=512, out_dtype=F32, b_n0=0, n=None, b_k0=0, k=None,
            acc_in=None, acc_n0=0, dep=None):
    if mode == "tn":
        kk, m = a.shape
    else:
        m, kk = a.shape
    if mode == "nn":
        n = b.shape[1] if n is None else n
    elif mode == "nt":
        n = b.shape[0]
        kk = kk if k is None else k
    else:
        n = b.shape[1]
    bm, bn, bk = _tile(m, bm), _tile(n, bn), _tile(kk, bk)
    nk = kk // bk
    assert b_n0 % bn == 0 and b_k0 % bk == 0 and acc_n0 % bn == 0
    dims = {"nn": (((1,), (0,)), ((), ())), "nt": _NT, "tn": _TN}[mode]

    n_in = 2 + (acc_in is not None) + (dep is not None)

    def body(*refs):
        a_ref, b_ref = refs[:2]
        init = refs[2] if acc_in is not None else None
        o_ref = refs[n_in]
        acc_ref = refs[-1] if nk > 1 else None
        p = _mxu_dot(a_ref[...], b_ref[...], dims)
        if nk == 1:
            o_ref[...] = (p if init is None else p + init[...]).astype(out_dtype)
            return
        ki = pl.program_id(2)

        @pl.when(ki == 0)
        def _():
            acc_ref[...] = p if init is None else p + init[...]

        @pl.when(ki > 0)
        def _():
            acc_ref[...] += p

        @pl.when(ki == nk - 1)
        def _():
            o_ref[...] = acc_ref[...].astype(out_dtype)

    a_spec = pl.BlockSpec((bk, bm), lambda j, i, q: (q, i)) if mode == "tn" else pl.BlockSpec((bm, bk), lambda j, i, q: (i, q))
    if mode == "nt":
        b_spec = pl.BlockSpec((bn, bk), lambda j, i, q: (j, q + b_k0 // bk))
    else:
        b_spec = pl.BlockSpec((bk, bn), lambda j, i, q: (q, j + b_n0 // bn))
    in_specs, args, aliases = [a_spec, b_spec], [a, b], {}
    out_map = lambda j, i, q: (i, j + acc_n0 // bn)
    if acc_in is not None:
        in_specs.append(pl.BlockSpec((bm, bn), out_map))
        args.append(acc_in)
        aliases = {2: 0}
        out_shape = jax.ShapeDtypeStruct(acc_in.shape, out_dtype)
    else:
        out_shape = jax.ShapeDtypeStruct((m, n), out_dtype)
    if dep is not None:
        in_specs.append(HBM)
        args.append(dep)
    return pl.pallas_call(
        body, name=name, out_shape=out_shape, grid=(n // bn, m // bm, nk),
        in_specs=in_specs, out_specs=pl.BlockSpec((bm, bn), out_map),
        scratch_shapes=[pltpu.VMEM((bm, bn), F32)] if nk > 1 else [],
        input_output_aliases=aliases,
        compiler_params=_cparams("parallel", "parallel", "arbitrary"),
    )(*args)


def _silu_rows(c, c_ctx):
    d = c.shape[-1]

    def body(c_ref, cc_ref, o_ref):
        o_ref[...] = jnp.zeros_like(o_ref)
        o_ref[0:1, :] = jax.nn.silu(c_ref[...])
        o_ref[1:2, :] = jax.nn.silu(cc_ref[...])

    return pl.pallas_call(body, name="silu_rows", out_shape=jax.ShapeDtypeStruct((8, d), F32))(
        c.reshape(1, d), c_ctx.reshape(1, d))


def _small_dot(a, b, mode, name):
    dims = {"nn": (((1,), (0,)), ((), ())), "nt": _NT, "tn": _TN}[mode]
    m = a.shape[1] if mode == "tn" else a.shape[0]
    n = b.shape[0] if mode == "nt" else b.shape[1]

    def body(a_ref, b_ref, o_ref):
        o_ref[...] = lax.dot_general(a_ref[...], b_ref[...], dims, preferred_element_type=F32,
                                     precision=lax.Precision.HIGHEST)

    return pl.pallas_call(body, name=name, out_shape=jax.ShapeDtypeStruct((m, n), F32),
                          compiler_params=_cparams())(a, b)


def _ln_stats(x):
    mu = jnp.mean(x, axis=-1, keepdims=True)
    xc = x - mu
    var = jnp.mean(xc * xc, axis=-1, keepdims=True)
    rstd = lax.rsqrt(var + LN_EPS)
    return xc * rstd, rstd


def _ln_mod(x, shift, scale, name):
    l, d = x.shape
    tl = _tile(l, 256)

    def body(x_ref, sh_ref, sc_ref, o_ref):
        xhat, _ = _ln_stats(x_ref[...])
        o_ref[...] = (xhat * (1.0 + sc_ref[...]) + sh_ref[...]).astype(o_ref.dtype)

    row = pl.BlockSpec((tl, d), lambda i: (i, 0))
    vec = pl.BlockSpec((1, d), lambda i: (0, 0))
    return pl.pallas_call(body, name=name, out_shape=jax.ShapeDtypeStruct((l, d), MXU_DTYPE), grid=(l // tl,),
                          in_specs=[row, vec, vec], out_specs=row, compiler_params=_cparams("parallel"))(x, shift, scale)


def _ln_mod_bwd(x, dxm, scale, res, name):
    l, d = x.shape
    tl = _tile(l, 256)
    with_res = res is not None

    def body(*refs):
        if with_res:
            x_ref, g_ref, sc_ref, r_ref, dx_ref, dsh_ref, dsc_ref = refs
        else:
            x_ref, g_ref, sc_ref, dx_ref, dsh_ref, dsc_ref = refs
        i = pl.program_id(0)
        xhat, rstd = _ln_stats(x_ref[...])
        g = g_ref[...].astype(F32)
        dxh = g * (1.0 + sc_ref[...])
        dx = rstd * (dxh - jnp.mean(dxh, axis=-1, keepdims=True) - xhat * jnp.mean(dxh * xhat, axis=-1, keepdims=True))
        dx_ref[...] = dx + r_ref[...].astype(F32) if with_res else dx

        @pl.when(i == 0)
        def _():
            dsh_ref[...] = jnp.zeros_like(dsh_ref)
            dsc_ref[...] = jnp.zeros_like(dsc_ref)

        dsh_ref[...] += jnp.sum(g, axis=0, keepdims=True)
        dsc_ref[...] += jnp.sum(g * xhat, axis=0, keepdims=True)

    row = pl.BlockSpec((tl, d), lambda i: (i, 0))
    vec = pl.BlockSpec((1, d), lambda i: (0, 0))
    args = [x, dxm, scale] + ([res] if with_res else [])
    return pl.pallas_call(
        body, name=name,
        out_shape=(jax.ShapeDtypeStruct((l, d), F32), jax.ShapeDtypeStruct((1, d), F32), jax.ShapeDtypeStruct((1, d), F32)),
        grid=(l // tl,), in_specs=[row, row, vec] + ([row] if with_res else []), out_specs=(row, vec, vec),
        compiler_params=_cparams("arbitrary"))(*args)


def _post_ln_loss(x, out, gate, ln_g, ln_b, target):
    l, d = x.shape
    tl = _tile(l, 256)

    def body(x_ref, o_ref, gate_ref, g_ref, b_ref, t_ref, loss_ref, dout_ref, dxr_ref, dgate_ref, dg_ref, db_ref):
        i = pl.program_id(0)
        out_t = o_ref[...]
        gate_v = gate_ref[...]
        rhat, rstd = _ln_stats(ALPHA * x_ref[...] + gate_v * out_t)
        ln_gv = g_ref[...]
        diff = rhat * ln_gv + b_ref[...] - t_ref[...]
        dy = diff * (1.0 / d)
        drh = dy * ln_gv
        dr = rstd * (drh - jnp.mean(drh, axis=-1, keepdims=True) - rhat * jnp.mean(drh * rhat, axis=-1, keepdims=True))
        dout_ref[...] = (gate_v * dr).astype(dout_ref.dtype)
        dxr_ref[...] = (ALPHA * dr).astype(dxr_ref.dtype)

        @pl.when(i == 0)
        def _():
            for r in (loss_ref, dgate_ref, dg_ref, db_ref):
                r[...] = jnp.zeros_like(r)

        loss_ref[...] += jnp.sum(diff * diff, axis=0, keepdims=True)
        dgate_ref[...] += jnp.sum(dr * out_t, axis=0, keepdims=True)
        dg_ref[...] += jnp.sum(dy * rhat, axis=0, keepdims=True)
        db_ref[...] += jnp.sum(dy, axis=0, keepdims=True)

    row = pl.BlockSpec((tl, d), lambda i: (i, 0))
    vec = pl.BlockSpec((1, d), lambda i: (0, 0))
    v = jax.ShapeDtypeStruct((1, d), F32)
    return pl.pallas_call(
        body, name="post_ln_loss",
        out_shape=(v, jax.ShapeDtypeStruct((l, d), MXU_DTYPE), jax.ShapeDtypeStruct((l, d), MXU_DTYPE), v, v, v),
        grid=(l // tl,), in_specs=[row, row, vec, vec, vec, row], out_specs=(vec, row, row, vec, vec, vec),
        compiler_params=_cparams("arbitrary"))(x, out, gate, ln_g, ln_b, target)


def _ga_forward_tile(p, g, b, ws_ref, bsf, w, nc, nh):
    u_raw, v_raw, za = p[:, :w], p[:, w:2 * w], p[:, 2 * w:3 * w]
    gu = _gelu(u_raw)
    vhat, rstd = _ln_stats(_gelu(v_raw))
    vn = vhat * g + b
    rows = []
    for ci in range(nc):
        r0 = ci * CHUNK
        heads = [_mxu_dot(ws_ref[h], vn[r0:r0 + CHUNK, h * HEAD_DIM_A:(h + 1) * HEAD_DIM_A]) for h in range(nh)]
        rows.append(jnp.concatenate(heads, axis=1) + bsf)
    mixed = jnp.concatenate(rows, axis=0) if nc > 1 else rows[0]
    return u_raw, v_raw, za, gu, vhat, rstd, vn, mixed


def _ga_fwd(proj, g, b, ws, bsf, w):
    l = proj.shape[0]
    nh = w // HEAD_DIM_A
    nc = _tile(l // CHUNK, 2)
    tl = nc * CHUNK

    def body(p_ref, g_ref, b_ref, ws_ref, bsf_ref, o_ref):
        _, _, za, gu, _, _, _, mixed = _ga_forward_tile(p_ref[...], g_ref[...], b_ref[...], ws_ref, bsf_ref[...], w, nc, nh)
        o_ref[...] = (gu * mixed * jax.nn.silu(za)).astype(o_ref.dtype)

    vec = pl.BlockSpec((1, w), lambda i: (0, 0))
    return pl.pallas_call(
        body, name="ga_fwd", out_shape=jax.ShapeDtypeStruct((l, 2 * w), MXU_DTYPE), grid=(l // tl,),
        in_specs=[pl.BlockSpec((tl, 3 * w), lambda i: (i, 0)), vec, vec,
                  pl.BlockSpec((nh, CHUNK, CHUNK), lambda i: (0, 0, 0)), pl.BlockSpec((CHUNK, w), lambda i: (0, 0))],
        out_specs=pl.BlockSpec((tl, w), lambda i: (i, 0)), compiler_params=_cparams("parallel"))(proj, g, b, ws, bsf)


def _ga_bwd(proj, dcat, dproj, g, b, ws, bsf, dys, du0, du1, d_skip, w):
    l = proj.shape[0]
    nh = w // HEAD_DIM_A
    nc = _tile(l // CHUNK, 2)
    tl = nc * CHUNK

    def body(p_ref, dy_ref, dp_in, g_ref, b_ref, ws_ref, bsf_ref, dys_ref, du0_ref, du1_ref, d_ref,
             dp_ref, dg_ref, db_ref, dws_ref, dbsf_ref):
        del dp_in
        i = pl.program_id(0)
        dp_ref[:, 3 * w:] = (dys_ref[...] * d_ref[...] + du0_ref[...] + du1_ref[...]).astype(dp_ref.dtype)
        gv = g_ref[...]
        u_raw, v_raw, za, gu, vhat, rstd, vn, mixed = _ga_forward_tile(
            p_ref[...], gv, b_ref[...], ws_ref, bsf_ref[...], w, nc, nh)
        dya = dy_ref[...].astype(F32)
        sz = jax.nn.silu(za)
        dmixed = dya * gu * sz
        dza = dya * gu * mixed * _silu_grad(za)
        dgu = dya * mixed * sz

        @pl.when(i == 0)
        def _():
            for r in (dg_ref, db_ref, dws_ref, dbsf_ref):
                r[...] = jnp.zeros_like(r)

        rows = []
        for ci in range(nc):
            r0 = ci * CHUNK
            heads = []
            for h in range(nh):
                cols = slice(h * HEAD_DIM_A, (h + 1) * HEAD_DIM_A)
                dm = dmixed[r0:r0 + CHUNK, cols]
                heads.append(_mxu_dot(ws_ref[h], dm, _TN))
                dws_ref[h] += _mxu_dot(dm, vn[r0:r0 + CHUNK, cols], _NT)
            rows.append(jnp.concatenate(heads, axis=1))
            dbsf_ref[...] += dmixed[r0:r0 + CHUNK, :]
        dvn = jnp.concatenate(rows, axis=0) if nc > 1 else rows[0]
        dg_ref[...] += jnp.sum(dvn * vhat, axis=0, keepdims=True)
        db_ref[...] += jnp.sum(dvn, axis=0, keepdims=True)
        dvh = dvn * gv
        dgv = rstd * (dvh - jnp.mean(dvh, axis=-1, keepdims=True) - vhat * jnp.mean(dvh * vhat, axis=-1, keepdims=True))
        dp_ref[:, :w] = (dgu * _gelu_grad(u_raw)).astype(dp_ref.dtype)
        dp_ref[:, w:2 * w] = (dgv * _gelu_grad(v_raw)).astype(dp_ref.dtype)
        dp_ref[:, 2 * w:3 * w] = dza.astype(dp_ref.dtype)

    vec = pl.BlockSpec((1, w), lambda i: (0, 0))
    row = pl.BlockSpec((tl, w), lambda i: (i, 0))
    ws_spec = pl.BlockSpec((nh, CHUNK, CHUNK), lambda i: (0, 0, 0))
    bs_spec = pl.BlockSpec((CHUNK, w), lambda i: (0, 0))
    v = jax.ShapeDtypeStruct((1, w), F32)
    return pl.pallas_call(
        body, name="ga_bwd",
        out_shape=(jax.ShapeDtypeStruct(dproj.shape, dproj.dtype), v, v, jax.ShapeDtypeStruct((nh, CHUNK, CHUNK), F32),
                   jax.ShapeDtypeStruct((CHUNK, w), F32)),
        grid=(l // tl,),
        in_specs=[pl.BlockSpec((tl, 3 * w), lambda i: (i, 0)), row, HBM, vec, vec, ws_spec, bs_spec, row, row, row, vec],
        out_specs=(pl.BlockSpec((tl, 4 * w), lambda i: (i, 0)), vec, vec, ws_spec, bs_spec),
        input_output_aliases={2: 0}, compiler_params=_cparams("arbitrary"))(
            proj, dcat, dproj, g, b, ws, bsf, dys, du0, du1, d_skip)


def _lane_group_sum(x, expand, name):
    return _small_dot(x, expand, "nn", name)


def _disc_math(lr, li, ls, br, bi):
    step = jnp.exp(ls)
    dr, di = lr * step, li * step
    mag = jnp.exp(dr)
    ab_re, ab_im = mag * jnp.cos(di), mag * jnp.sin(di)
    den = lr * lr + li * li
    nr, ni = ab_re - 1.0, ab_im
    f_re = (nr * lr + ni * li) / den
    f_im = (ni * lr - nr * li) / den
    bb_re = f_re * br - f_im * bi
    bb_im = f_re * bi + f_im * br
    return ab_re, ab_im, bb_re, bb_im


def _disc_fwd(lr, li, ls, br, bi):
    def body(lr_ref, li_ref, ls_ref, br_ref, bi_ref, o1, o2, o3, o4):
        res = _disc_math(lr_ref[...], li_ref[...], ls_ref[...], br_ref[...], bi_ref[...])
        for o, r in zip((o1, o2, o3, o4), res):
            o[...] = r

    s = lambda a: jax.ShapeDtypeStruct(a.shape, F32)
    return pl.pallas_call(body, name="s5_disc", out_shape=(s(lr), s(lr), s(br), s(br)), compiler_params=_cparams())(
        lr, li, ls, br, bi)


def _disc_bwd(lr, li, ls, br, bi, d_ar, d_ai, d_br, d_bi):
    def body(lr_ref, li_ref, ls_ref, br_ref, bi_ref, c1, c2, c3, c4, o1, o2, o3, o4, o5):
        _, vjp = jax.vjp(_disc_math, lr_ref[...], li_ref[...], ls_ref[...], br_ref[...], bi_ref[...])
        res = vjp((c1[...], c2[...], c3[...], c4[...]))
        for o, r in zip((o1, o2, o3, o4, o5), res):
            o[...] = r

    s = lambda a: jax.ShapeDtypeStruct(a.shape, F32)
    return pl.pallas_call(body, name="s5_disc_bwd", out_shape=(s(lr), s(lr), s(ls), s(br), s(br)),
                          compiler_params=_cparams())(lr, li, ls, br, bi, d_ar, d_ai, d_br, d_bi)


def _dir_spec(a, dr):
    return pl.BlockSpec((None,) + a.shape[1:], lambda i: (dr,) + (0,) * (a.ndim - 1))


def _s5_fwd(u_arr, u_col, w, h0, a_sm, wbr, wbi, cre, ncim, dr, name):
    rev = dr == 1
    l = u_arr.shape[0]
    nb = w // LANES
    spb = wbr.shape[-1]
    nsr = a_sm.shape[2]
    assert 2 * spb == 8 * LANES and nb % 2 == 0
    npair = nb // 2
    t = _tile(l, 256)
    n = l // t
    tile = (lambda i: n - 1 - i) if rev else (lambda i: i)

    def body(u_ref, h0_ref, a_ref, wbr_ref, wbi_ref, cre_ref, ncim_ref, y_ref, hr_ref, hi_ref, tr_ref, ti_ref, hfin_ref,
             carry_ref):
        i = pl.program_id(0)

        @pl.when(i == 0)
        def _():
            carry_ref[...] = h0_ref[...]

        for j in range(npair):
            for h_ref, w_ref in ((hr_ref, wbr_ref), (hi_ref, wbi_ref)):
                blk = [_mxu_dot(u_ref[:, k * LANES:(k + 1) * LANES], w_ref[k]) for k in (2 * j, 2 * j + 1)]
                h_ref[j] = jnp.concatenate(blk, axis=1).reshape(t, 8, LANES)
        slab = lambda ref, part, j: ref[part, 8 * j:8 * j + 8, :]
        ar = [slab(a_ref, 0, j) for j in range(npair)]
        ai = [slab(a_ref, 1, j) for j in range(npair)]

        def steps(blk, c):
            hr, hi = list(c[:npair]), list(c[npair:])
            for q in range(SCAN_UNROLL):
                s = blk * SCAN_UNROLL + q
                row = t - 1 - s if rev else s
                for j in range(npair):
                    hr[j], hi[j] = (ar[j] * hr[j] - ai[j] * hi[j] + hr_ref[j, row],
                                    ar[j] * hi[j] + ai[j] * hr[j] + hi_ref[j, row])
                    hr_ref[j, row] = hr[j]
                    hi_ref[j, row] = hi[j]
            return tuple(hr + hi)

        init = tuple(slab(carry_ref, part, j) for part in range(2) for j in range(npair))
        c = lax.fori_loop(0, t // SCAN_UNROLL, steps, init)
        for part in range(2):
            for j in range(npair):
                carry_ref[part, 8 * j:8 * j + 8, :] = c[part * npair + j]
                hfin_ref[part, 8 * j:8 * j + 8, :] = c[part * npair + j]
        for j in range(npair):
            cols8 = slice(j * 8 * LANES, (j + 1) * 8 * LANES)
            tr_ref[:, cols8] = hr_ref[j].reshape(t, 8 * LANES).astype(tr_ref.dtype)
            ti_ref[:, cols8] = hi_ref[j].reshape(t, 8 * LANES).astype(ti_ref.dtype)
        for k in range(nb):
            cols = slice(k * spb, (k + 1) * spb)
            y_ref[:, k * LANES:(k + 1) * LANES] = (_mxu_dot(tr_ref[:, cols], cre_ref[k]) + _mxu_dot(ti_ref[:, cols], ncim_ref[k]))

    full = lambda a: pl.BlockSpec(a.shape, lambda i: (0,) * a.ndim)
    hspec = pl.BlockSpec((npair, t, 8, LANES), lambda i: (0, tile(i), 0, 0))
    tspec = pl.BlockSpec((t, nsr * LANES), lambda i: (tile(i), 0))
    hsh = jax.ShapeDtypeStruct((npair, l, 8, LANES), F32)
    tsh = jax.ShapeDtypeStruct((l, nsr * LANES), MXU_DTYPE)
    return pl.pallas_call(
        body, name=name,
        out_shape=(jax.ShapeDtypeStruct((l, w), F32), hsh, hsh, tsh, tsh, jax.ShapeDtypeStruct((2, nsr, LANES), F32)),
        grid=(n,),
        in_specs=[pl.BlockSpec((t, w), lambda i: (tile(i), u_col)), full(h0)] + [_dir_spec(a, dr) for a in (a_sm, wbr, wbi, cre, ncim)],
        out_specs=(pl.BlockSpec((t, w), lambda i: (tile(i), 0)), hspec, hspec, tspec, tspec,
                   pl.BlockSpec((2, nsr, LANES), lambda i: (0, 0, 0))),
        scratch_shapes=[pltpu.VMEM((2, nsr, LANES), F32)],
        compiler_params=_cparams("arbitrary"))(u_arr, h0, a_sm, wbr, wbi, cre, ncim)


def _s5_bwd(dys, u_arr, u_col, w, hr, hi, tr, ti, hbound, g_in, a_sm, wbr_t, wbi_t, cre_t, ncim_t, dr, name):
    rev = dr == 1
    l = u_arr.shape[0]
    nb = w // LANES
    spb = wbr_t.shape[-2]
    nsr = a_sm.shape[2]
    npair = nb // 2
    t = _tile(l, 128)
    n = l // t
    with_dy = dys is not None
    tile = (lambda i: i) if rev else (lambda i: n - 1 - i)

    def body(*refs):
        if with_dy:
            (dy_ref, u_ref, hr_ref, hi_ref, pr_ref, pi_ref, tr_ref, ti_ref, hb_ref, gin_ref, a_ref, wbrt_ref, wbit_ref,
             cret_ref, ncimt_ref, du_ref, dwbr_ref, dwbi_ref, dcre_ref, dncim_ref, da_ref, gout_ref,
             gr_ref, gi_ref, gtr_ref, gti_ref, carry_ref) = refs
        else:
            (u_ref, hr_ref, hi_ref, pr_ref, pi_ref, hb_ref, gin_ref, a_ref, wbrt_ref, wbit_ref,
             du_ref, dwbr_ref, dwbi_ref, da_ref, gout_ref, gr_ref, gi_ref, gtr_ref, gti_ref, carry_ref) = refs
        i = pl.program_id(0)

        @pl.when(i == 0)
        def _():
            carry_ref[...] = gin_ref[...]
            accs = (dwbr_ref, dwbi_ref, da_ref) + ((dcre_ref, dncim_ref) if with_dy else ())
            for r in accs:
                r[...] = jnp.zeros_like(r)

        if with_dy:
            for j in range(npair):
                for g_ref, c_ref in ((gr_ref, cret_ref), (gi_ref, ncimt_ref)):
                    blk = [_mxu_dot(dy_ref[:, k * LANES:(k + 1) * LANES], c_ref[k]) for k in (2 * j, 2 * j + 1)]
                    g_ref[j] = jnp.concatenate(blk, axis=1).reshape(t, 8, LANES)
        else:
            gr_ref[...] = jnp.zeros_like(gr_ref)
            gi_ref[...] = jnp.zeros_like(gi_ref)
        slab = lambda ref, part, j: ref[part, 8 * j:8 * j + 8, :]
        last = t - 1 if rev else 0
        first = i == n - 1

        ar = [slab(a_ref, 0, j) for j in range(npair)]
        ai = [slab(a_ref, 1, j) for j in range(npair)]

        def steps(blk, c):
            gr, gi, dr, di = (list(c[q * npair:(q + 1) * npair]) for q in range(4))
            for q in range(SCAN_UNROLL):
                s = blk * SCAN_UNROLL + q
                row = s if rev else t - 1 - s
                prow = jnp.minimum(row + 1, t - 1) if rev else jnp.maximum(row - 1, 0)
                for j in range(npair):
                    pr, pi = hr_ref[j, prow], hi_ref[j, prow]
                    gr[j], gi[j] = (gr_ref[j, row] + ar[j] * gr[j] + ai[j] * gi[j],
                                    gi_ref[j, row] + ar[j] * gi[j] - ai[j] * gr[j])
                    gr_ref[j, row] = gr[j]
                    gi_ref[j, row] = gi[j]
                    dr[j], di[j] = dr[j] + gr[j] * pr + gi[j] * pi, di[j] + gi[j] * pr - gr[j] * pi
            return tuple(gr + gi + dr + di)

        init = tuple(slab(ref, part, j) for ref in (carry_ref, da_ref) for part in range(2) for j in range(npair))
        c = lax.fori_loop(0, t // SCAN_UNROLL, steps, init)
        gr, gi, dr, di = (c[q * npair:(q + 1) * npair] for q in range(4))
        for j in range(npair):
            pr = jnp.where(first, slab(hb_ref, 0, j), pr_ref[j, 0]) - hr_ref[j, last]
            pi = jnp.where(first, slab(hb_ref, 1, j), pi_ref[j, 0]) - hi_ref[j, last]
            rows = slice(8 * j, 8 * j + 8)
            da_ref[0, rows, :] = dr[j] + gr[j] * pr + gi[j] * pi
            da_ref[1, rows, :] = di[j] + gi[j] * pr - gr[j] * pi
            for part, val in enumerate((gr[j], gi[j])):
                carry_ref[part, rows, :] = val
                gout_ref[part, rows, :] = val

        for j in range(npair):
            cols8 = slice(j * 8 * LANES, (j + 1) * 8 * LANES)
            gtr_ref[:, cols8] = gr_ref[j].reshape(t, 8 * LANES).astype(gtr_ref.dtype)
            gti_ref[:, cols8] = gi_ref[j].reshape(t, 8 * LANES).astype(gti_ref.dtype)
        for k in range(nb):
            cols = slice(k * spb, (k + 1) * spb)
            lanes = slice(k * LANES, (k + 1) * LANES)
            du_ref[:, lanes] = _mxu_dot(gtr_ref[:, cols], wbrt_ref[k]) + _mxu_dot(gti_ref[:, cols], wbit_ref[k])
            dwbr_ref[k] += _mxu_dot(u_ref[:, lanes], gtr_ref[:, cols], _TN)
            dwbi_ref[k] += _mxu_dot(u_ref[:, lanes], gti_ref[:, cols], _TN)
            if with_dy:
                dcre_ref[k] += _mxu_dot(tr_ref[:, cols], dy_ref[:, lanes], _TN)
                dncim_ref[k] += _mxu_dot(ti_ref[:, cols], dy_ref[:, lanes], _TN)

    full = lambda a: pl.BlockSpec(a.shape, lambda i: (0,) * a.ndim)
    row = lambda cb: pl.BlockSpec((t, w), lambda i: (tile(i), cb))
    hspec = pl.BlockSpec((npair, t, 8, LANES), lambda i: (0, tile(i), 0, 0))
    if rev:
        pspec = pl.BlockSpec((npair, 1, 8, LANES), lambda i: (0, jnp.minimum((tile(i) + 1) * t, l - 1), 0, 0))
    else:
        pspec = pl.BlockSpec((npair, 1, 8, LANES), lambda i: (0, jnp.maximum(tile(i) * t - 1, 0), 0, 0))
    sm = jax.ShapeDtypeStruct((2, nsr, LANES), F32)
    smspec = pl.BlockSpec((2, nsr, LANES), lambda i: (0, 0, 0))
    wsh = jax.ShapeDtypeStruct((nb, LANES, spb), F32)
    csh = jax.ShapeDtypeStruct((nb, spb, LANES), F32)
    tspec = pl.BlockSpec((t, nsr * LANES), lambda i: (tile(i), 0))
    in_specs = (([row(0)] if with_dy else []) + [row(u_col), hspec, hspec, pspec, pspec] + ([tspec, tspec] if with_dy else [])
                + [full(hbound), full(g_in)]
                + [_dir_spec(a, dr) for a in (a_sm, wbr_t, wbi_t) + ((cre_t, ncim_t) if with_dy else ())])
    args = (([dys] if with_dy else []) + [u_arr, hr, hi, hr, hi] + ([tr, ti] if with_dy else [])
            + [hbound, g_in, a_sm, wbr_t, wbi_t] + ([cre_t, ncim_t] if with_dy else []))
    out_shape = (jax.ShapeDtypeStruct((l, w), F32), wsh, wsh) + ((csh, csh) if with_dy else ()) + (sm, sm)
    out_specs = (row(0), full(wsh), full(wsh)) + ((full(csh), full(csh)) if with_dy else ()) + (smspec, smspec)
    return pl.pallas_call(
        body, name=name, out_shape=out_shape, grid=(n,), in_specs=in_specs, out_specs=out_specs,
        scratch_shapes=[pltpu.VMEM((npair, t, 8, LANES), F32)] * 2 + [pltpu.VMEM((t, nsr * LANES), MXU_DTYPE)] * 2
        + [pltpu.VMEM((2, nsr, LANES), F32)],
        compiler_params=_cparams("arbitrary"))(*args)


def _glu_fwd(y0, y1, proj, cat, d_skip, w_glu, b_glu, w):
    l = y0.shape[0]
    tl = _tile(l, 256)

    def body(y0_ref, y1_ref, u_ref, z_ref, cat_in, d_ref, wg_ref, bg_ref, ys_ref, cat_ref):
        del cat_in
        ys = y0_ref[...] + y1_ref[...] + d_ref[...] * u_ref[...]
        ys_ref[...] = ys
        gy = _gelu(ys)
        s = _mxu_dot(gy, wg_ref[...]) + bg_ref[...]
        cat_ref[...] = (gy * jax.nn.sigmoid(s) * jax.nn.silu(z_ref[...])).astype(cat_ref.dtype)

    row = pl.BlockSpec((tl, w), lambda i: (i, 0))
    vec = pl.BlockSpec((1, w), lambda i: (0, 0))
    return pl.pallas_call(
        body, name="glu_fwd", out_shape=(jax.ShapeDtypeStruct((l, w), F32), jax.ShapeDtypeStruct(cat.shape, cat.dtype)),
        grid=(l // tl,),
        in_specs=[row, row, pl.BlockSpec((tl, w), lambda i: (i, 3)), pl.BlockSpec((tl, w), lambda i: (i, 4)), HBM,
                  vec, pl.BlockSpec((w, w), lambda i: (0, 0)), vec],
        out_specs=(row, pl.BlockSpec((tl, w), lambda i: (i, 1))), input_output_aliases={4: 1},
        compiler_params=_cparams("parallel"))(y0, y1, proj, proj, cat, d_skip, w_glu, b_glu)


def _glu_bwd(dcat, ys, proj, d_skip, w_glu, b_glu, w):
    l = ys.shape[0]
    tl = _tile(l, 256)

    def body(dy_ref, ys_ref, u_ref, z_ref, d_ref, wg_ref, bg_ref, dys_ref, dp_ref, dbg_ref, dd_ref, dwg_ref):
        i = pl.program_id(0)
        ys_t = ys_ref[...]
        z = z_ref[...]
        dyb = dy_ref[...].astype(F32)
        gy = _gelu(ys_t)
        sg = jax.nn.sigmoid(_mxu_dot(gy, wg_ref[...]) + bg_ref[...])
        dp_ref[...] = (dyb * gy * sg * _silu_grad(z)).astype(dp_ref.dtype)
        dglu = dyb * jax.nn.silu(z)
        ds = dglu * gy * sg * (1.0 - sg)
        dgy = dglu * sg + _mxu_dot(ds, wg_ref[...], _NT)
        dys_t = dgy * _gelu_grad(ys_t)
        dys_ref[...] = dys_t

        @pl.when(i == 0)
        def _():
            for r in (dbg_ref, dd_ref, dwg_ref):
                r[...] = jnp.zeros_like(r)

        dbg_ref[...] += jnp.sum(ds, axis=0, keepdims=True)
        dd_ref[...] += jnp.sum(dys_t * u_ref[...], axis=0, keepdims=True)
        dwg_ref[...] += _mxu_dot(gy, ds, _TN)

    row = pl.BlockSpec((tl, w), lambda i: (i, 0))
    vec = pl.BlockSpec((1, w), lambda i: (0, 0))
    mat = pl.BlockSpec((w, w), lambda i: (0, 0))
    v = jax.ShapeDtypeStruct((1, w), F32)
    return pl.pallas_call(
        body, name="glu_bwd",
        out_shape=(jax.ShapeDtypeStruct((l, w), F32), jax.ShapeDtypeStruct((l, 5 * w), MXU_DTYPE), v, v,
                   jax.ShapeDtypeStruct((w, w), F32)),
        grid=(l // tl,),
        in_specs=[pl.BlockSpec((tl, w), lambda i: (i, 1)), row, pl.BlockSpec((tl, w), lambda i: (i, 3)),
                  pl.BlockSpec((tl, w), lambda i: (i, 4)), vec, mat, vec],
        out_specs=(row, pl.BlockSpec((tl, w), lambda i: (i, 4)), vec, vec, mat),
        compiler_params=_cparams("arbitrary"))(dcat, ys, proj, proj, d_skip, w_glu, b_glu)


def _add2(a, b, name):
    l, w = a.shape
    tl = _tile(l, 512)

    def body(a_ref, b_ref, o_ref):
        o_ref[...] = a_ref[...] + b_ref[...]

    row = pl.BlockSpec((tl, w), lambda i: (i, 0))
    return pl.pallas_call(body, name=name, out_shape=jax.ShapeDtypeStruct((l, w), F32), grid=(l // tl,),
                          in_specs=[row, row], out_specs=row, compiler_params=_cparams("parallel"))(a, b)


def _adamw_nd(w, m, v, g, name):
    shape = w.shape
    lead = math.prod(shape[:-2]) if len(shape) > 2 else 1
    b, c = (shape[-2], shape[-1]) if len(shape) >= 2 else (1, shape[-1])
    t3 = (lead, b, c)
    padded_row = -(-b // 8) * 8 * -(-c // LANES) * LANES * 4
    ta = _tile(lead, max(1, (2 << 20) // padded_row))

    def body(w_ref, m_ref, v_ref, g_ref, d_ref, mo_ref, vo_ref):
        gv = g_ref[...]
        mn = ADAM_B1 * m_ref[...] + (1.0 - ADAM_B1) * gv
        vn = ADAM_B2 * v_ref[...] + (1.0 - ADAM_B2) * (gv * gv)
        m_hat = mn / (1.0 - ADAM_B1 ** ADAM_STEP)
        v_hat = vn / (1.0 - ADAM_B2 ** ADAM_STEP)
        d_ref[...] = -ADAM_LR * (m_hat / (jnp.sqrt(v_hat) + ADAM_EPS) + ADAM_WD * w_ref[...])
        mo_ref[...] = mn
        vo_ref[...] = vn

    blk = pl.BlockSpec((ta, b, c), lambda i: (i, 0, 0))
    s = jax.ShapeDtypeStruct(t3, F32)
    outs = pl.pallas_call(body, name=name, out_shape=(s, s, s), grid=(lead // ta,), in_specs=[blk] * 4, out_specs=(blk,) * 3,
                          compiler_params=_cparams("parallel"))(*[a.reshape(t3) for a in (w, m, v, g)])
    return tuple(o.reshape(shape) for o in outs)


def _adamw(w, m, v, gparts, name):
    r, c = w.shape
    np_ = gparts.shape[0]
    tr = _tile(r, max(8, (1 << 18) // c), 8)

    def body(w_ref, m_ref, v_ref, g_ref, go_ref, d_ref, mo_ref, vo_ref):
        g = g_ref[0].astype(F32)
        for p in range(1, np_):
            g = g + g_ref[p].astype(F32)
        mn = ADAM_B1 * m_ref[...] + (1.0 - ADAM_B1) * g
        vn = ADAM_B2 * v_ref[...] + (1.0 - ADAM_B2) * (g * g)
        m_hat = mn / (1.0 - ADAM_B1 ** ADAM_STEP)
        v_hat = vn / (1.0 - ADAM_B2 ** ADAM_STEP)
        go_ref[...] = g
        d_ref[...] = -ADAM_LR * (m_hat / (jnp.sqrt(v_hat) + ADAM_EPS) + ADAM_WD * w_ref[...])
        mo_ref[...] = mn
        vo_ref[...] = vn

    row = pl.BlockSpec((tr, c), lambda i: (i, 0))
    s = jax.ShapeDtypeStruct((r, c), F32)
    return pl.pallas_call(body, name=name, out_shape=(s, s, s, s), grid=(r // tr,),
                          in_specs=[row, row, row, pl.BlockSpec((np_, tr, c), lambda i: (0, i, 0))],
                          out_specs=(row, row, row, row), compiler_params=_cparams("parallel"))(w, m, v, gparts)


def _sum_slots(parts, name):
    np_, r, c = parts.shape

    def body(p_ref, o_ref):
        g = p_ref[0]
        for p in range(1, np_):
            g = g + p_ref[p]
        o_ref[...] = g

    return pl.pallas_call(body, name=name, out_shape=jax.ShapeDtypeStruct((r, c), F32), compiler_params=_cparams())(parts)


def _block_diag(x, gb):
    nd, g, a, b = x.shape
    eye = jnp.eye(gb, dtype=x.dtype)
    y = jnp.einsum("dkgab,gh->dkgahb", x.reshape(nd, g // gb, gb, a, b), eye)
    return y.reshape(nd, g // gb, gb * a, gb * b)


def _block_diag_extract(y, gb, a, b):
    nd, nbk = y.shape[:2]
    eye = jnp.eye(gb, dtype=y.dtype)
    x = jnp.einsum("dkgahb,gh->dkgab", y.reshape(nd, nbk, gb, a, gb, b), eye)
    return x.reshape(nd, nbk * gb, a, b)


def kernel(x, c, ctx, c_ctx, w_ada, b_ada, w_in, sgu_ln_g, sgu_ln_b, w_spatial, b_spatial, s5_lam_re, s5_lam_im, s5_log_step, s5_b_re, s5_b_im, s5_c_re, s5_c_im, s5_d, w_glu, b_glu, w_out, ln_g, ln_b, loss_target, m_c_ctx, m_w_ada, m_b_ada, m_w_in, m_sgu_ln_g, m_sgu_ln_b, m_w_spatial, m_b_spatial, m_s5_lam_re, m_s5_lam_im, m_s5_log_step, m_s5_b_re, m_s5_b_im, m_s5_c_re, m_s5_c_im, m_s5_d, m_w_glu, m_b_glu, m_w_out, m_ln_g, m_ln_b, v_c_ctx, v_w_ada, v_b_ada, v_w_in, v_sgu_ln_g, v_sgu_ln_b, v_w_spatial, v_b_spatial, v_s5_lam_re, v_s5_lam_im, v_s5_log_step, v_s5_b_re, v_s5_b_im, v_s5_c_re, v_s5_c_im, v_s5_d, v_w_glu, v_b_glu, v_w_out, v_ln_g, v_ln_b):
    small_names = ["c_ctx", "b_ada", "sgu_ln_g", "sgu_ln_b", "w_spatial", "b_spatial", "s5_lam_re", "s5_lam_im",
                   "s5_log_step", "s5_b_re", "s5_b_im", "s5_c_re", "s5_c_im", "s5_d", "b_glu", "ln_g", "ln_b"]
    env = dict(locals())
    x2, tgt, ctx2 = x[0], loss_target[0], ctx[0]
    l, d = x2.shape
    lc = ctx2.shape[0]
    w = d // 2
    nh = w // HEAD_DIM_A
    nd, g_s5, p_s5, c_s5 = s5_b_re.shape[1:]
    ns = g_s5 * p_s5
    nsr = ns // LANES
    gb = LANES // c_s5
    me = _index(_mesh_pos())
    ada_cols = w_ada.shape[2]

    srows = _silu_rows(c, c_ctx)
    srows_all = _all_gather(srows, 0, "gather_silu")
    s_mat = jnp.concatenate([srows_all[0::8], srows_all[1:2], jnp.zeros((7, d), F32)], axis=0)
    mod_part = _small_dot(s_mat, w_ada[0], "nn", "mod_cols")
    mod_all = _all_gather(mod_part, 1, "gather_mod") + b_ada
    hw_in, tok_a = _exchange_start(w_in[0].astype(MXU_DTYPE), 1, "gather", "start_gather_w_in", (SIBLING,) + SAME_CORE_PEERS)
    mod_all = mod_all + tok_a[0, 0]
    mod_x = lax.dynamic_slice_in_dim(mod_all, me, 1, axis=0)
    mod_c = mod_all[8:9]
    shift_x, scale_x, gate_x = mod_x[:, :d], mod_x[:, d:2 * d], mod_x[:, 2 * d:]
    shift_c, scale_c = mod_c[:, :d], mod_c[:, d:2 * d]

    lr, li = s5_lam_re[0][:, :, None, :], s5_lam_im[0][:, :, None, :]
    ls = s5_log_step[0][:, :, None, None]
    swapped = ("s5_b_re", "s5_b_im")
    for nm in swapped:
        for pre in ("", "m_", "v_"):
            env[pre + nm] = jnp.swapaxes(env[pre + nm], -1, -2)
    br_t, bi_t = env["s5_b_re"][0], env["s5_b_im"][0]
    ab_re, ab_im, bb_re, bb_im = _disc_fwd(lr, li, ls, br_t, bi_t)
    a_sm = jnp.stack([ab_re, ab_im], axis=1).reshape(nd, 2, nsr, LANES)
    wbr = _block_diag(bb_re, gb).astype(MXU_DTYPE)
    wbi = _block_diag(bb_im, gb).astype(MXU_DTYPE)
    cre_t = _block_diag(s5_c_re[0], gb).astype(MXU_DTYPE)
    ncim_t = _block_diag(-s5_c_im[0], gb).astype(MXU_DTYPE)
    cre, ncim = jnp.swapaxes(cre_t, 2, 3), jnp.swapaxes(ncim_t, 2, 3)
    wbr_t, wbi_t = jnp.swapaxes(wbr, 2, 3), jnp.swapaxes(wbi, 2, 3)
    d_skip = s5_d

    xm = _ln_mod(x2, shift_x, scale_x, "ln_mod_x")
    cm = _ln_mod(ctx2, shift_c, scale_c, "ln_mod_ctx")
    ready = xm[:8, :LANES].astype(F32) + cm[:8, :LANES].astype(F32) + cre[0, 0, :8, :].astype(F32)
    hw_in2, tok_b = _forward_start(_exchange_wait(hw_in, ready, "wait_gather_w_in"), 1, "start_forward_w_in")
    w_in_f = _forward_wait(hw_in2, tok_b, "wait_forward_w_in")
    hw_glu, tok_c = _exchange_start(w_glu[0].astype(MXU_DTYPE), 0, "gather", "start_gather_w_glu")
    hw_out, tok_o = _exchange_start(w_out[0].astype(MXU_DTYPE), 0, "gather", "start_gather_w_out")
    proj = _matmul(xm, w_in_f, mode="nn", name="proj", bm=512, bn=1024, bk=d, dep=tok_c + tok_o)
    ub_c = _matmul(cm, w_in_f, mode="nn", name="proj_ctx", bm=256, bn=w, bk=d, b_n0=3 * w, n=w)
    bsf = jnp.repeat(b_spatial[0].T, HEAD_DIM_A, axis=1)
    ws = w_spatial[0]
    cat = _ga_fwd(proj, sgu_ln_g, sgu_ln_b, ws, bsf, w)
    zeros_state = jnp.zeros((2, nsr, LANES), F32)
    s5c, s5l = [], []
    for dr in range(nd):
        s5c.append(_s5_fwd(ub_c, 0, w, zeros_state, a_sm, wbr, wbi, cre, ncim, dr, f"s5_fwd_ctx{dr}"))
        s5l.append(_s5_fwd(proj, 3, w, s5c[dr][5], a_sm, wbr, wbi, cre, ncim, dr, f"s5_fwd{dr}"))
    w_glu_f = _exchange_wait(hw_glu, s5l[1][0], "wait_gather_w_glu")
    ys, cat = _glu_fwd(s5l[0][0], s5l[1][0], proj, cat, d_skip, w_glu_f, b_glu, w)
    w_out_f = _exchange_wait(hw_out, ys, "wait_gather_w_out")
    out = _matmul(cat, w_out_f, mode="nn", name="out_proj", bm=512, bn=1024, bk=2 * w)
    loss_row, dout, dx_res, dgate, dln_g, dln_b = _post_ln_loss(x2, out, gate_x, ln_g, ln_b, tgt)

    dcat = _matmul(dout, w_out_f, mode="nt", name="d_cat", bm=512, bn=1024, bk=d, out_dtype=MXU_DTYPE)
    dw_out = _matmul(cat, dout, mode="tn", name="d_w_out", bm=1024, bn=1024, bk=2048, out_dtype=MXU_DTYPE)
    hg_out, tok_d = _exchange_start(dw_out, 0, "a2a", "start_a2a_d_w_out")
    dys, dproj, db_glu, dd_skip, dw_glu = _glu_bwd(dcat, ys, proj, d_skip + tok_d[0, 0], w_glu_f, b_glu, w)
    hg_glu, tok_e = _exchange_start(dw_glu.astype(MXU_DTYPE), 0, "a2a", "start_a2a_d_w_glu")
    zeros_state = zeros_state + tok_e[0, 0]
    du_l, du_c, dwbr, dwbi, dcre, dncim, da_sm = [], [], [], [], [], [], []
    nbk, spb = w // LANES, gb * p_s5
    for dr in range(nd):
        bl = _s5_bwd(dys, proj, 3, w, *s5l[dr][1:5], s5c[dr][5], zeros_state, a_sm, wbr_t, wbi_t,
                     cre_t, ncim_t, dr, f"s5_bwd{dr}")
        bc = _s5_bwd(None, ub_c, 0, w, s5c[dr][1], s5c[dr][2], None, None, zeros_state, bl[6], a_sm, wbr_t, wbi_t,
                     None, None, dr, f"s5_bwd_ctx{dr}")
        du_l.append(bl[0])
        du_c.append(bc[0])
        dwbr.append(_add2(bl[1].reshape(nbk * LANES, spb), bc[1].reshape(nbk * LANES, spb), f"sum_dwbr{dr}"))
        dwbi.append(_add2(bl[2].reshape(nbk * LANES, spb), bc[2].reshape(nbk * LANES, spb), f"sum_dwbi{dr}"))
        dcre.append(bl[3])
        dncim.append(bl[4])
        da_sm.append(_add2(bl[5].reshape(2 * nsr, LANES), bc[3].reshape(2 * nsr, LANES), f"sum_da{dr}"))
    dub_c = _add2(du_c[0], du_c[1], "dub_ctx")
    dwbr = jnp.stack(dwbr).reshape(nd, nbk, LANES, spb)
    dwbi = jnp.stack(dwbi).reshape(nd, nbk, LANES, spb)
    dcre, dncim = jnp.stack(dcre), jnp.stack(dncim)
    da_sm = jnp.stack(da_sm).reshape(nd, 2, g_s5, p_s5)
    dproj, dsg, dsb, dws, dbsf = _ga_bwd(proj, dcat, dproj, sgu_ln_g, sgu_ln_b, ws, bsf, dys, du_l[0], du_l[1], d_skip, w)

    dbb_re = _block_diag_extract(dwbr, gb, c_s5, p_s5)
    dbb_im = _block_diag_extract(dwbi, gb, c_s5, p_s5)
    dc_re = jnp.swapaxes(_block_diag_extract(dcre, gb, p_s5, c_s5), 2, 3)
    dc_im = -jnp.swapaxes(_block_diag_extract(dncim, gb, p_s5, c_s5), 2, 3)
    dlr, dli, dls, db_re, db_im = _disc_bwd(lr, li, ls, br_t, bi_t, da_sm[:, 0:1].reshape(nd, g_s5, 1, p_s5),
                                            da_sm[:, 1:2].reshape(nd, g_s5, 1, p_s5), dbb_re, dbb_im)
    expand = (jnp.arange(w)[:, None] // HEAD_DIM_A == jnp.arange(LANES)[None, :]).astype(F32)
    db_sp = _lane_group_sum(dbsf, expand, "d_b_spatial")[:, :nh].T

    local = {"sgu_ln_g": dsg, "sgu_ln_b": dsb, "w_spatial": dws, "b_spatial": db_sp,
             "s5_lam_re": dlr, "s5_lam_im": dli, "s5_log_step": dls, "s5_b_re": db_re, "s5_b_im": db_im,
             "s5_c_re": dc_re, "s5_c_im": dc_im, "s5_d": dd_skip, "b_glu": db_glu, "ln_g": dln_g, "ln_b": dln_b}
    reduced = [n for n in small_names if n in local]
    loss_part = (0.5 / d) * jnp.sum(loss_row)
    flat = jnp.concatenate([loss_part.reshape(1)] + [local[n].reshape(-1) for n in reduced])
    unit = N_DEV * 8 * LANES
    total = -(-flat.shape[0] // unit) * unit
    flat = jnp.pad(flat, (0, total - flat.shape[0])).reshape(N_DEV * 8, total // (N_DEV * 8))
    h_small, tok_s = _exchange_start(flat, 0, "a2a", "start_a2a_small")

    dw_in = _matmul(xm, dproj, mode="tn", name="d_w_in", bm=1024, bn=1280, bk=2048, out_dtype=MXU_DTYPE, dep=tok_s)
    dw_in = _matmul(cm, dub_c, mode="tn", name="d_w_in_ctx", bm=1024, bn=w, bk=lc, acc_in=dw_in, acc_n0=3 * w,
                    out_dtype=MXU_DTYPE)
    hg_in, tok_f = _exchange_start(dw_in, 1, "a2a", "start_a2a_d_w_in")
    mine = _sum_slots(_exchange_wait(h_small, dw_in, "wait_a2a_small"), "sum_small")
    h_sums, tok_g = _exchange_start(mine, 0, "gather", "start_gather_small")
    dxm = _matmul(dproj, w_in_f, mode="nt", name="d_xm", bm=512, bn=1024, bk=5 * w, dep=tok_f + tok_g,
                  out_dtype=MXU_DTYPE)
    dcm = _matmul(dub_c, w_in_f, mode="nt", name="d_cm", bm=256, bn=1024, bk=w, b_k0=3 * w, k=w)
    grad_x, dshift_x, dscale_x = _ln_mod_bwd(x2, dxm, scale_x, dx_res, "ln_mod_x_bwd")
    _, dshift_c, dscale_c = _ln_mod_bwd(ctx2, dcm, scale_c, None, "ln_mod_ctx_bwd")

    dmod_rows = jnp.concatenate([jnp.concatenate([dshift_x, dscale_x, dgate], axis=1),
                                 jnp.concatenate([dshift_c, dscale_c, jnp.zeros((1, d), F32)], axis=1),
                                 jnp.zeros((6, 3 * d), F32)], axis=0)
    dmod_all = _all_gather(dmod_rows, 0, "gather_dmod")
    dmod_ctx = _sum_slots(dmod_all[1::8].reshape(N_DEV, 1, 3 * d), "sum_dmod_ctx")
    dmod_mat = jnp.concatenate([dmod_all[0::8], dmod_ctx, jnp.zeros((7, 3 * d), F32)], axis=0)
    db_ada = _sum_slots(dmod_mat[:9].reshape(9, 1, 3 * d), "sum_db_ada")
    dmod_mine = lax.dynamic_slice_in_dim(dmod_mat, me * ada_cols, ada_cols, axis=1)
    dw_ada = _small_dot(s_mat, dmod_mine, "tn", "d_w_ada")
    dsilu_cc = _small_dot(dmod_mine[8:16], w_ada[0], "nt", "d_silu_cctx")[0:1]
    dc_ctx_part = dsilu_cc * _silu_grad(c_ctx.reshape(1, d))
    dc_ctx_rows = jnp.concatenate([dc_ctx_part, jnp.zeros((7, d), F32)], axis=0)
    dc_ctx_all = _all_gather(dc_ctx_rows, 0, "gather_d_c_ctx")
    dc_ctx = _sum_slots(dc_ctx_all[0::8].reshape(N_DEV, 1, d), "sum_d_c_ctx").reshape(d)

    summed = _exchange_wait(h_sums, dc_ctx, "wait_gather_small").reshape(-1)
    loss = summed[0]
    grads, off = {"b_ada": db_ada, "c_ctx": dc_ctx}, 1
    for n in reduced:
        size = math.prod(env[n].shape)
        grads[n] = summed[off:off + size].reshape(env[n].shape)
        off += size

    gp_w_out = _exchange_wait(hg_out, summed, "wait_a2a_d_w_out")
    gp_w_glu = _exchange_wait(hg_glu, summed, "wait_a2a_d_w_glu")
    gp_w_in = _exchange_wait(hg_in, summed, "wait_a2a_d_w_in")
    big = {
        "w_ada": _adamw(w_ada[0], m_w_ada[0], v_w_ada[0], dw_ada[None], "adamw_w_ada"),
        "w_in": _adamw(w_in[0], m_w_in[0], v_w_in[0], gp_w_in, "adamw_w_in"),
        "w_glu": _adamw(w_glu[0], m_w_glu[0], v_w_glu[0], gp_w_glu, "adamw_w_glu"),
        "w_out": _adamw(w_out[0], m_w_out[0], v_w_out[0], gp_w_out, "adamw_w_out"),
    }
    res = {n: tuple(a[None] for a in big[n]) for n in big}
    for n in small_names:
        res[n] = (grads[n],) + _adamw_nd(env[n], env["m_" + n], env["v_" + n], grads[n], "adamw_" + n)
        if n in swapped:
            res[n] = tuple(jnp.swapaxes(a, -1, -2) for a in res[n])

    order = ["c_ctx", "w_ada", "b_ada", "w_in", "sgu_ln_g", "sgu_ln_b", "w_spatial", "b_spatial", "s5_lam_re", "s5_lam_im",
             "s5_log_step", "s5_b_re", "s5_b_im", "s5_c_re", "s5_c_im", "s5_d", "w_glu", "b_glu", "w_out", "ln_g", "ln_b"]
    return (loss, grad_x[None], *[res[n][0] for n in order], *[res[n][1] for n in order],
            *[res[n][2] for n in order], *[res[n][3] for n in order])
```

```python
import functools
import math

import jax
import jax.numpy as jnp
from jax import lax
from jax.experimental import pallas as pl
from jax.experimental.pallas import tpu as pltpu

F32 = jnp.float32
MXU_DTYPE = jnp.bfloat16
N_DEV = 8
MESH_ID = pl.DeviceIdType.MESH
LN_EPS = 1e-6
DEPTH = 1
ALPHA = (2.0 * DEPTH) ** 0.25
CHUNK = 128
HEAD_DIM_A = 128
ADAM_LR, ADAM_B1, ADAM_B2, ADAM_EPS, ADAM_WD, ADAM_STEP = 0.001, 0.9, 0.999, 1e-08, 0.01, 10
LANES = 128
SCAN_UNROLL = 8
VMEM_LIMIT = 56 * 1024 * 1024
HBM = pl.BlockSpec(memory_space=pl.ANY)


def _cparams(*sem):
    return pltpu.CompilerParams(dimension_semantics=sem if sem else None, vmem_limit_bytes=VMEM_LIMIT)


def _tile(n, pref, mult=1):
    if n <= pref:
        return n
    t = pref - pref % mult
    while n % t:
        t -= mult
    return t


def _gelu(x):
    return 0.5 * x * (1.0 + lax.erf(x * (1.0 / math.sqrt(2.0))))


def _gelu_grad(x):
    return 0.5 * (1.0 + lax.erf(x * (1.0 / math.sqrt(2.0)))) + x * jnp.exp(-0.5 * x * x) * (1.0 / math.sqrt(2.0 * math.pi))


def _silu_grad(x):
    s = jax.nn.sigmoid(x)
    return s * (1.0 + x * (1.0 - s))


def _mxu_dot(a, b, dims=(((1,), (0,)), ((), ()))):
    return lax.dot_general(a.astype(MXU_DTYPE), b.astype(MXU_DTYPE), dims, preferred_element_type=F32)


_NT = (((1,), (1,)), ((), ()))
_TN = (((0,), (0,)), ((), ()))


def _mesh_pos():
    return lax.axis_index("x"), lax.axis_index("y"), lax.axis_index("c")


def _peer(pos, r):
    x, y, c = pos
    return ((1 - x) if r & 4 else x, (1 - y) if r & 2 else y, (1 - c) if r & 1 else c)


def _index(pos):
    return 4 * pos[0] + 2 * pos[1] + pos[2]


def _slice_of(ref, axis, idx, size):
    start = idx * size
    if axis == 0:
        return ref.at[pl.ds(start, size)]
    return ref.at[:, pl.ds(start, size)]


def _all_gather(x, axis, name):
    size = x.shape[axis]
    out_shape = tuple(s * N_DEV if a == axis else s for a, s in enumerate(x.shape))

    def body(x_ref, o_ref, send_sems, recv_sems, local_sem):
        me = _mesh_pos()
        mine = pltpu.make_async_copy(x_ref, _slice_of(o_ref, axis, _index(me), size), local_sem)
        mine.start()

        def copy(r, block):
            return pltpu.make_async_remote_copy(
                src_ref=x_ref, dst_ref=_slice_of(o_ref, axis, _index(block), size),
                send_sem=send_sems.at[r - 1], recv_sem=recv_sems.at[r - 1],
                device_id=_peer(me, r), device_id_type=MESH_ID)

        sends = [copy(r, me) for r in range(1, N_DEV)]
        for cp in sends:
            cp.start()
        for r in range(1, N_DEV):
            copy(r, _peer(me, r)).wait_recv()
        for cp in sends:
            cp.wait_send()
        mine.wait()

    return pl.pallas_call(
        body, name=name, out_shape=jax.ShapeDtypeStruct(out_shape, x.dtype),
        in_specs=[HBM], out_specs=HBM,
        scratch_shapes=[pltpu.SemaphoreType.DMA((N_DEV - 1,)), pltpu.SemaphoreType.DMA((N_DEV - 1,)),
                        pltpu.SemaphoreType.DMA],
    )(x)


def _all_to_all(x, axis, name):
    size = x.shape[axis] // N_DEV
    slot = tuple(size if a == axis else s for a, s in enumerate(x.shape))

    def body(x_ref, o_ref, send_sems, recv_sems, local_sem):
        me = _mesh_pos()
        mine = pltpu.make_async_copy(_slice_of(x_ref, axis, _index(me), size), o_ref.at[_index(me)], local_sem)
        mine.start()

        def copy(r, sender, receiver):
            return pltpu.make_async_remote_copy(
                src_ref=_slice_of(x_ref, axis, _index(receiver), size), dst_ref=o_ref.at[_index(sender)],
                send_sem=send_sems.at[r - 1], recv_sem=recv_sems.at[r - 1],
                device_id=_peer(me, r), device_id_type=MESH_ID)

        sends = [copy(r, me, _peer(me, r)) for r in range(1, N_DEV)]
        for cp in sends:
            cp.start()
        for r in range(1, N_DEV):
            copy(r, _peer(me, r), me).wait_recv()
        for cp in sends:
            cp.wait_send()
        mine.wait()

    return pl.pallas_call(
        body, name=name, out_shape=jax.ShapeDtypeStruct((N_DEV,) + slot, x.dtype),
        in_specs=[HBM], out_specs=HBM,
        scratch_shapes=[pltpu.SemaphoreType.DMA((N_DEV - 1,)), pltpu.SemaphoreType.DMA((N_DEV - 1,)),
                        pltpu.SemaphoreType.DMA],
    )(x)


_SEM = pl.BlockSpec(memory_space=pltpu.SEMAPHORE)
_HBM = pl.BlockSpec(memory_space=pltpu.HBM)
_EFFECT = pltpu.SideEffectType.DATAFLOW_SIDE_EFFECTING
ALL_PEERS = tuple(range(1, N_DEV))
SIBLING = 1
SAME_CORE_PEERS = (2, 4, 6)


def _exchange_copy(kind, x_ref, land_ref, axis, size, send_sems, recv_sems, me, rels, q, arriving):
    peer = _peer(me, rels[q])
    sender, receiver = (peer, me) if arriving else (me, peer)
    if kind == "gather":
        src, dst = x_ref, _slice_of(land_ref, axis, _index(sender), size)
    else:
        src, dst = _slice_of(x_ref, axis, _index(receiver), size), land_ref.at[_index(sender)]
    return pltpu.make_async_remote_copy(src_ref=src, dst_ref=dst, send_sem=send_sems.at[q], recv_sem=recv_sems.at[q],
                                        device_id=peer, device_id_type=MESH_ID)


def _local_copy(kind, x_ref, land_ref, axis, size, me, local_sem):
    if kind == "gather":
        return pltpu.make_async_copy(x_ref, _slice_of(land_ref, axis, _index(me), size), local_sem)
    return pltpu.make_async_copy(_slice_of(x_ref, axis, _index(me), size), land_ref.at[_index(me)], local_sem)


def _exchange_start(x, axis, kind, name, rels=ALL_PEERS):
    size = x.shape[axis] if kind == "gather" else x.shape[axis] // N_DEV
    if kind == "gather":
        land_shape = tuple(s * N_DEV if a == axis else s for a, s in enumerate(x.shape))
    else:
        land_shape = (N_DEV,) + tuple(size if a == axis else s for a, s in enumerate(x.shape))

    def body(x_ref, land_ref, send_sems, recv_sems, local_sem, x_thru, land_thru, token):
        del x_thru, land_thru
        me = _mesh_pos()
        _local_copy(kind, x_ref, land_ref, axis, size, me, local_sem).start()
        for q in range(len(rels)):
            _exchange_copy(kind, x_ref, land_ref, axis, size, send_sems, recv_sems, me, rels, q, False).start()
        token[...] = jnp.zeros_like(token)

    sems = pltpu.SemaphoreType.DMA((len(rels),))
    send_sems, recv_sems, local_sem, x_thru, land_thru, token = pl.pallas_call(
        body, name=name,
        out_shape=(sems, sems, pltpu.SemaphoreType.DMA(()), pltpu.HBM(x.shape, x.dtype), pltpu.HBM(land_shape, x.dtype),
                   jax.ShapeDtypeStruct((8, LANES), F32)),
        in_specs=(_HBM, _HBM), out_specs=(_SEM, _SEM, _SEM, _HBM, _HBM, pl.BlockSpec(memory_space=pltpu.VMEM)),
        input_output_aliases={0: 3, 1: 4}, compiler_params=pltpu.CompilerParams(has_side_effects=_EFFECT),
    )(pltpu.with_memory_space_constraint(x, pltpu.HBM),
      pltpu.with_memory_space_constraint(lax.empty(land_shape, x.dtype), pltpu.HBM))
    return (kind, axis, size, rels, send_sems, recv_sems, local_sem, x_thru, land_thru), token


def _exchange_wait(handle, after, name):
    kind, axis, size, rels, send_sems, recv_sems, local_sem, x_thru, land_thru = handle

    def body(x_ref, land_ref, send_sems, recv_sems, local_sem, after_ref, x_dead, got_ref):
        del after_ref, x_dead, got_ref
        me = _mesh_pos()
        _local_copy(kind, x_ref, land_ref, axis, size, me, local_sem).wait()
        for q in range(len(rels)):
            _exchange_copy(kind, x_ref, land_ref, axis, size, send_sems, recv_sems, me, rels, q, False).wait_send()
        for q in range(len(rels)):
            _exchange_copy(kind, x_ref, land_ref, axis, size, send_sems, recv_sems, me, rels, q, True).wait_recv()

    return pl.pallas_call(
        body, name=name, out_shape=(pltpu.HBM(x_thru.shape, x_thru.dtype), pltpu.HBM(land_thru.shape, land_thru.dtype)),
        in_specs=(_HBM, _HBM, _SEM, _SEM, _SEM, HBM), out_specs=(_HBM, _HBM), input_output_aliases={0: 0, 1: 1},
        compiler_params=pltpu.CompilerParams(has_side_effects=_EFFECT),
    )(x_thru, land_thru, send_sems, recv_sems, local_sem, after)[1]


def _forward_copy(land_ref, axis, size, send_sems, recv_sems, me, q, arriving):
    sibling = _peer(me, SIBLING)
    owner = _peer(sibling if arriving else me, SAME_CORE_PEERS[q])
    block = _slice_of(land_ref, axis, _index(owner), size)
    return pltpu.make_async_remote_copy(src_ref=block, dst_ref=block, send_sem=send_sems.at[q], recv_sem=recv_sems.at[q],
                                        device_id=sibling, device_id_type=MESH_ID)


def _forward_start(land, axis, name):
    size = land.shape[axis] // N_DEV

    def body(land_ref, send_sems, recv_sems, land_thru, token):
        del land_thru
        me = _mesh_pos()
        for q in range(len(SAME_CORE_PEERS)):
            _forward_copy(land_ref, axis, size, send_sems, recv_sems, me, q, False).start()
        token[...] = jnp.zeros_like(token)

    sems = pltpu.SemaphoreType.DMA((len(SAME_CORE_PEERS),))
    send_sems, recv_sems, land_thru, token = pl.pallas_call(
        body, name=name, out_shape=(sems, sems, pltpu.HBM(land.shape, land.dtype), jax.ShapeDtypeStruct((8, LANES), F32)),
        in_specs=(_HBM,), out_specs=(_SEM, _SEM, _HBM, pl.BlockSpec(memory_space=pltpu.VMEM)),
        input_output_aliases={0: 2}, compiler_params=pltpu.CompilerParams(has_side_effects=_EFFECT),
    )(land)
    return (axis, size, send_sems, recv_sems, land_thru), token


def _forward_wait(handle, after, name):
    axis, size, send_sems, recv_sems, land_thru = handle

    def body(land_ref, send_sems, recv_sems, after_ref, got_ref):
        del after_ref, got_ref
        me = _mesh_pos()
        for q in range(len(SAME_CORE_PEERS)):
            _forward_copy(land_ref, axis, size, send_sems, recv_sems, me, q, False).wait_send()
        for q in range(len(SAME_CORE_PEERS)):
            _forward_copy(land_ref, axis, size, send_sems, recv_sems, me, q, True).wait_recv()

    return pl.pallas_call(
        body, name=name, out_shape=pltpu.HBM(land_thru.shape, land_thru.dtype),
        in_specs=(_HBM, _SEM, _SEM, HBM), out_specs=_HBM, input_output_aliases={0: 0},
        compiler_params=pltpu.CompilerParams(has_side_effects=_EFFECT),
    )(land_thru, send_sems, recv_sems, after)


def _matmul(a, b, *, mode, name, bm=512, bn=512, bk=512, out_dtype=F32, b_n0=0, n=None, b_k0=0, k=None,
            acc_in=None, acc_n0=0, dep=None):
    if mode == "tn":
        kk, m = a.shape
    else:
        m, kk = a.shape
    if mode == "nn":
        n = b.shape[1] if n is None else n
    elif mode == "nt":
        n = b.shape[0]
        kk = kk if k is None else k
    else:
        n = b.shape[1]
    bm, bn, bk = _tile(m, bm), _tile(n, bn), _tile(kk, bk)
    nk = kk // bk
    assert b_n0 % bn == 0 and b_k0 % bk == 0 and acc_n0 % bn == 0
    dims = {"nn": (((1,), (0,)), ((), ())), "nt": _NT, "tn": _TN}[mode]

    n_in = 2 + (acc_in is not None) + (dep is not None)

    def body(*refs):
        a_ref, b_ref = refs[:2]
        init = refs[2] if acc_in is not None else None
        o_ref = refs[n_in]
        acc_ref = refs[-1] if nk > 1 else None
        p = _mxu_dot(a_ref[...], b_ref[...], dims)
        if nk == 1:
            o_ref[...] = (p if init is None else p + init[...]).astype(out_dtype)
            return
        ki = pl.program_id(2)

        @pl.when(ki == 0)
        def _():
            acc_ref[...] = p if init is None else p + init[...]

        @pl.when(ki > 0)
        def _():
            acc_ref[...] += p

        @pl.when(ki == nk - 1)
        def _():
            o_ref[...] = acc_ref[...].astype(out_dtype)

    a_spec = pl.BlockSpec((bk, bm), lambda j, i, q: (q, i)) if mode == "tn" else pl.BlockSpec((bm, bk), lambda j, i, q: (i, q))
    if mode == "nt":
        b_spec = pl.BlockSpec((bn, bk), lambda j, i, q: (j, q + b_k0 // bk))
    else:
        b_spec = pl.BlockSpec((bk, bn), lambda j, i, q: (q, j + b_n0 // bn))
    in_specs, args, aliases = [a_spec, b_spec], [a, b], {}
    out_map = lambda j, i, q: (i, j + acc_n0 // bn)
    if acc_in is not None:
        in_specs.append(pl.BlockSpec((bm, bn), out_map))
        args.append(acc_in)
        aliases = {2: 0}
        out_shape = jax.ShapeDtypeStruct(acc_in.shape, out_dtype)
    else:
        out_shape = jax.ShapeDtypeStruct((m, n), out_dtype)
    if dep is not None:
        in_specs.append(HBM)
        args.append(dep)
    return pl.pallas_call(
        body, name=name, out_shape=out_shape, grid=(n // bn, m // bm, nk),
        in_specs=in_specs, out_specs=pl.BlockSpec((bm, bn), out_map),
        scratch_shapes=[pltpu.VMEM((bm, bn), F32)] if nk > 1 else [],
        input_output_aliases=aliases,
        compiler_params=_cparams("parallel", "parallel", "arbitrary"),
    )(*args)


def _silu_rows(c, c_ctx):
    d = c.shape[-1]

    def body(c_ref, cc_ref, o_ref):
        o_ref[...] = jnp.zeros_like(o_ref)
        o_ref[0:1, :] = jax.nn.silu(c_ref[...])
        o_ref[1:2, :] = jax.nn.silu(cc_ref[...])

    return pl.pallas_call(body, name="silu_rows", out_shape=jax.ShapeDtypeStruct((8, d), F32))(
        c.reshape(1, d), c_ctx.reshape(1, d))


def _small_dot(a, b, mode, name):
    dims = {"nn": (((1,), (0,)), ((), ())), "nt": _NT, "tn": _TN}[mode]
    m = a.shape[1] if mode == "tn" else a.shape[0]
    n = b.shape[0] if mode == "nt" else b.shape[1]

    def body(a_ref, b_ref, o_ref):
        o_ref[...] = lax.dot_general(a_ref[...], b_ref[...], dims, preferred_element_type=F32,
                                     precision=lax.Precision.HIGHEST)

    return pl.pallas_call(body, name=name, out_shape=jax.ShapeDtypeStruct((m, n), F32),
                          compiler_params=_cparams())(a, b)


def _ln_stats(x):
    mu = jnp.mean(x, axis=-1, keepdims=True)
    xc = x - mu
    var = jnp.mean(xc * xc, axis=-1, keepdims=True)
    rstd = lax.rsqrt(var + LN_EPS)
    return xc * rstd, rstd


def _ln_mod(x, shift, scale, name):
    l, d = x.shape
    tl = _tile(l, 256)

    def body(x_ref, sh_ref, sc_ref, o_ref):
        xhat, _ = _ln_stats(x_ref[...])
        o_ref[...] = (xhat * (1.0 + sc_ref[...]) + sh_ref[...]).astype(o_ref.dtype)

    row = pl.BlockSpec((tl, d), lambda i: (i, 0))
    vec = pl.BlockSpec((1, d), lambda i: (0, 0))
    return pl.pallas_call(body, name=name, out_shape=jax.ShapeDtypeStruct((l, d), MXU_DTYPE), grid=(l // tl,),
                          in_specs=[row, vec, vec], out_specs=row, compiler_params=_cparams("parallel"))(x, shift, scale)


def _ln_mod_bwd(x, dxm, scale, res, name):
    l, d = x.shape
    tl = _tile(l, 256)
    with_res = res is not None

    def body(*refs):
        if with_res:
            x_ref, g_ref, sc_ref, r_ref, dx_ref, dsh_ref, dsc_ref = refs
        else:
            x_ref, g_ref, sc_ref, dx_ref, dsh_ref, dsc_ref = refs
        i = pl.program_id(0)
        xhat, rstd = _ln_stats(x_ref[...])
        g = g_ref[...].astype(F32)
        dxh = g * (1.0 + sc_ref[...])
        dx = rstd * (dxh - jnp.mean(dxh, axis=-1, keepdims=True) - xhat * jnp.mean(dxh * xhat, axis=-1, keepdims=True))
        dx_ref[...] = dx + r_ref[...].astype(F32) if with_res else dx

        @pl.when(i == 0)
        def _():
            dsh_ref[...] = jnp.zeros_like(dsh_ref)
            dsc_ref[...] = jnp.zeros_like(dsc_ref)

        dsh_ref[...] += jnp.sum(g, axis=0, keepdims=True)
        dsc_ref[...] += jnp.sum(g * xhat, axis=0, keepdims=True)

    row = pl.BlockSpec((tl, d), lambda i: (i, 0))
    vec = pl.BlockSpec((1, d), lambda i: (0, 0))
    args = [x, dxm, scale] + ([res] if with_res else [])
    return pl.pallas_call(
        body, name=name,
        out_shape=(jax.ShapeDtypeStruct((l, d), F32), jax.ShapeDtypeStruct((1, d), F32), jax.ShapeDtypeStruct((1, d), F32)),
        grid=(l // tl,), in_specs=[row, row, vec] + ([row] if with_res else []), out_specs=(row, vec, vec),
        compiler_params=_cparams("arbitrary"))(*args)


def _post_ln_loss(x, out, gate, ln_g, ln_b, target):
    l, d = x.shape
    tl = _tile(l, 256)

    def body(x_ref, o_ref, gate_ref, g_ref, b_ref, t_ref, loss_ref, dout_ref, dxr_ref, dgate_ref, dg_ref, db_ref):
        i = pl.program_id(0)
        out_t = o_ref[...]
        gate_v = gate_ref[...]
        rhat, rstd = _ln_stats(ALPHA * x_ref[...] + gate_v * out_t)
        ln_gv = g_ref[...]
        diff = rhat * ln_gv + b_ref[...] - t_ref[...]
        dy = diff * (1.0 / d)
        drh = dy * ln_gv
        dr = rstd * (drh - jnp.mean(drh, axis=-1, keepdims=True) - rhat * jnp.mean(drh * rhat, axis=-1, keepdims=True))
        dout_ref[...] = (gate_v * dr).astype(dout_ref.dtype)
        dxr_ref[...] = (ALPHA * dr).astype(dxr_ref.dtype)

        @pl.when(i == 0)
        def _():
            for r in (loss_ref, dgate_ref, dg_ref, db_ref):
                r[...] = jnp.zeros_like(r)

        loss_ref[...] += jnp.sum(diff * diff, axis=0, keepdims=True)
        dgate_ref[...] += jnp.sum(dr * out_t, axis=0, keepdims=True)
        dg_ref[...] += jnp.sum(dy * rhat, axis=0, keepdims=True)
        db_ref[...] += jnp.sum(dy, axis=0, keepdims=True)

    row = pl.BlockSpec((tl, d), lambda i: (i, 0))
    vec = pl.BlockSpec((1, d), lambda i: (0, 0))
    v = jax.ShapeDtypeStruct((1, d), F32)
    return pl.pallas_call(
        body, name="post_ln_loss",
        out_shape=(v, jax.ShapeDtypeStruct((l, d), MXU_DTYPE), jax.ShapeDtypeStruct((l, d), MXU_DTYPE), v, v, v),
        grid=(l // tl,), in_specs=[row, row, vec, vec, vec, row], out_specs=(vec, row, row, vec, vec, vec),
        compiler_params=_cparams("arbitrary"))(x, out, gate, ln_g, ln_b, target)


def _ga_forward_tile(p, g, b, ws_ref, bsf, w, nc, nh):
    u_raw, v_raw, za = p[:, :w], p[:, w:2 * w], p[:, 2 * w:3 * w]
    gu = _gelu(u_raw)
    vhat, rstd = _ln_stats(_gelu(v_raw))
    vn = vhat * g + b
    rows = []
    for ci in range(nc):
        r0 = ci * CHUNK
        heads = [_mxu_dot(ws_ref[h], vn[r0:r0 + CHUNK, h * HEAD_DIM_A:(h + 1) * HEAD_DIM_A]) for h in range(nh)]
        rows.append(jnp.concatenate(heads, axis=1) + bsf)
    mixed = jnp.concatenate(rows, axis=0) if nc > 1 else rows[0]
    return u_raw, v_raw, za, gu, vhat, rstd, vn, mixed


def _ga_fwd(proj, g, b, ws, bsf, w):
    l = proj.shape[0]
    nh = w // HEAD_DIM_A
    nc = _tile(l // CHUNK, 2)
    tl = nc * CHUNK

    def body(p_ref, g_ref, b_ref, ws_ref, bsf_ref, o_ref):
        _, _, za, gu, _, _, _, mixed = _ga_forward_tile(p_ref[...], g_ref[...], b_ref[...], ws_ref, bsf_ref[...], w, nc, nh)
        o_ref[...] = (gu * mixed * jax.nn.silu(za)).astype(o_ref.dtype)

    vec = pl.BlockSpec((1, w), lambda i: (0, 0))
    return pl.pallas_call(
        body, name="ga_fwd", out_shape=jax.ShapeDtypeStruct((l, 2 * w), MXU_DTYPE), grid=(l // tl,),
        in_specs=[pl.BlockSpec((tl, 3 * w), lambda i: (i, 0)), vec, vec,
                  pl.BlockSpec((nh, CHUNK, CHUNK), lambda i: (0, 0, 0)), pl.BlockSpec((CHUNK, w), lambda i: (0, 0))],
        out_specs=pl.BlockSpec((tl, w), lambda i: (i, 0)), compiler_params=_cparams("parallel"))(proj, g, b, ws, bsf)


def _ga_bwd(proj, dcat, dproj, g, b, ws, bsf, dys, du0, du1, d_skip, w):
    l = proj.shape[0]
    nh = w // HEAD_DIM_A
    nc = _tile(l // CHUNK, 2)
    tl = nc * CHUNK

    def body(p_ref, dy_ref, dp_in, g_ref, b_ref, ws_ref, bsf_ref, dys_ref, du0_ref, du1_ref, d_ref,
             dp_ref, dg_ref, db_ref, dws_ref, dbsf_ref):
        del dp_in
        i = pl.program_id(0)
        dp_ref[:, 3 * w:] = (dys_ref[...] * d_ref[...] + du0_ref[...] + du1_ref[...]).astype(dp_ref.dtype)
        gv = g_ref[...]
        u_raw, v_raw, za, gu, vhat, rstd, vn, mixed = _ga_forward_tile(
            p_ref[...], gv, b_ref[...], ws_ref, bsf_ref[...], w, nc, nh)
        dya = dy_ref[...].astype(F32)
        sz = jax.nn.silu(za)
        dmixed = dya * gu * sz
        dza = dya * gu * mixed * _silu_grad(za)
        dgu = dya * mixed * sz

        @pl.when(i == 0)
        def _():
            for r in (dg_ref, db_ref, dws_ref, dbsf_ref):
                r[...] = jnp.zeros_like(r)

        rows = []
        for ci in range(nc):
            r0 = ci * CHUNK
            heads = []
            for h in range(nh):
                cols = slice(h * HEAD_DIM_A, (h + 1) * HEAD_DIM_A)
                dm = dmixed[r0:r0 + CHUNK, cols]
                heads.append(_mxu_dot(ws_ref[h], dm, _TN))
                dws_ref[h] += _mxu_dot(dm, vn[r0:r0 + CHUNK, cols], _NT)
            rows.append(jnp.concatenate(heads, axis=1))
            dbsf_ref[...] += dmixed[r0:r0 + CHUNK, :]
        dvn = jnp.concatenate(rows, axis=0) if nc > 1 else rows[0]
        dg_ref[...] += jnp.sum(dvn * vhat, axis=0, keepdims=True)
        db_ref[...] += jnp.sum(dvn, axis=0, keepdims=True)
        dvh = dvn * gv
        dgv = rstd * (dvh - jnp.mean(dvh, axis=-1, keepdims=True) - vhat * jnp.mean(dvh * vhat, axis=-1, keepdims=True))
        dp_ref[:, :w] = (dgu * _gelu_grad(u_raw)).astype(dp_ref.dtype)
        dp_ref[:, w:2 * w] = (dgv * _gelu_grad(v_raw)).astype(dp_ref.dtype)
        dp_ref[:, 2 * w:3 * w] = dza.astype(dp_ref.dtype)

    vec = pl.BlockSpec((1, w), lambda i: (0, 0))
    row = pl.BlockSpec((tl, w), lambda i: (i, 0))
    ws_spec = pl.BlockSpec((nh, CHUNK, CHUNK), lambda i: (0, 0, 0))
    bs_spec = pl.BlockSpec((CHUNK, w), lambda i: (0, 0))
    v = jax.ShapeDtypeStruct((1, w), F32)
    return pl.pallas_call(
        body, name="ga_bwd",
        out_shape=(jax.ShapeDtypeStruct(dproj.shape, dproj.dtype), v, v, jax.ShapeDtypeStruct((nh, CHUNK, CHUNK), F32),
                   jax.ShapeDtypeStruct((CHUNK, w), F32)),
        grid=(l // tl,),
        in_specs=[pl.BlockSpec((tl, 3 * w), lambda i: (i, 0)), row, HBM, vec, vec, ws_spec, bs_spec, row, row, row, vec],
        out_specs=(pl.BlockSpec((tl, 4 * w), lambda i: (i, 0)), vec, vec, ws_spec, bs_spec),
        input_output_aliases={2: 0}, compiler_params=_cparams("arbitrary"))(
            proj, dcat, dproj, g, b, ws, bsf, dys, du0, du1, d_skip)


def _lane_group_sum(x, expand, name):
    return _small_dot(x, expand, "nn", name)


def _disc_math(lr, li, ls, br, bi):
    step = jnp.exp(ls)
    dr, di = lr * step, li * step
    mag = jnp.exp(dr)
    ab_re, ab_im = mag * jnp.cos(di), mag * jnp.sin(di)
    den = lr * lr + li * li
    nr, ni = ab_re - 1.0, ab_im
    f_re = (nr * lr + ni * li) / den
    f_im = (ni * lr - nr * li) / den
    bb_re = f_re * br - f_im * bi
    bb_im = f_re * bi + f_im * br
    return ab_re, ab_im, bb_re, bb_im


def _disc_fwd(lr, li, ls, br, bi):
    def body(lr_ref, li_ref, ls_ref, br_ref, bi_ref, o1, o2, o3, o4):
        res = _disc_math(lr_ref[...], li_ref[...], ls_ref[...], br_ref[...], bi_ref[...])
        for o, r in zip((o1, o2, o3, o4), res):
            o[...] = r

    s = lambda a: jax.ShapeDtypeStruct(a.shape, F32)
    return pl.pallas_call(body, name="s5_disc", out_shape=(s(lr), s(lr), s(br), s(br)), compiler_params=_cparams())(
        lr, li, ls, br, bi)


def _disc_bwd(lr, li, ls, br, bi, d_ar, d_ai, d_br, d_bi):
    def body(lr_ref, li_ref, ls_ref, br_ref, bi_ref, c1, c2, c3, c4, o1, o2, o3, o4, o5):
        _, vjp = jax.vjp(_disc_math, lr_ref[...], li_ref[...], ls_ref[...], br_ref[...], bi_ref[...])
        res = vjp((c1[...], c2[...], c3[...], c4[...]))
        for o, r in zip((o1, o2, o3, o4, o5), res):
            o[...] = r

    s = lambda a: jax.ShapeDtypeStruct(a.shape, F32)
    return pl.pallas_call(body, name="s5_disc_bwd", out_shape=(s(lr), s(lr), s(ls), s(br), s(br)),
                          compiler_params=_cparams())(lr, li, ls, br, bi, d_ar, d_ai, d_br, d_bi)


def _dir_spec(a, dr):
    return pl.BlockSpec((None,) + a.shape[1:], lambda i: (dr,) + (0,) * (a.ndim - 1))


def _s5_fwd(u_arr, u_col, w, h0, a_sm, wbr, wbi, cre, ncim, dr, name):
    rev = dr == 1
    l = u_arr.shape[0]
    nb = w // LANES
    spb = cre.shape[-2]
    nsr = a_sm.shape[2]
    assert 2 * spb == 8 * LANES and nb % 2 == 0
    npair = nb // 2
    t = _tile(l, 256)
    n = l // t
    tile = (lambda i: n - 1 - i) if rev else (lambda i: i)

    def body(u_ref, h0_ref, a_ref, wbr_ref, wbi_ref, cre_ref, ncim_ref, y_ref, hr_ref, hi_ref, tr_ref, ti_ref, hfin_ref,
             carry_ref):
        i = pl.program_id(0)

        @pl.when(i == 0)
        def _():
            carry_ref[...] = h0_ref[...]

        for j in range(npair):
            for h_ref, w_ref in ((hr_ref, wbr_ref), (hi_ref, wbi_ref)):
                h_ref[j] = _mxu_dot(u_ref[:, 2 * j * LANES:2 * (j + 1) * LANES], w_ref[j]).reshape(t, 8, LANES)
        slab = lambda ref, part, j: ref[part, 8 * j:8 * j + 8, :]
        ar = [slab(a_ref, 0, j) for j in range(npair)]
        ai = [slab(a_ref, 1, j) for j in range(npair)]

        def steps(blk, c):
            hr, hi = list(c[:npair]), list(c[npair:])
            for q in range(SCAN_UNROLL):
                s = blk * SCAN_UNROLL + q
                row = t - 1 - s if rev else s
                for j in range(npair):
                    hr[j], hi[j] = (ar[j] * hr[j] - ai[j] * hi[j] + hr_ref[j, row],
                                    ar[j] * hi[j] + ai[j] * hr[j] + hi_ref[j, row])
                    hr_ref[j, row] = hr[j]
                    hi_ref[j, row] = hi[j]
            return tuple(hr + hi)

        init = tuple(slab(carry_ref, part, j) for part in range(2) for j in range(npair))
        c = lax.fori_loop(0, t // SCAN_UNROLL, steps, init)
        for part in range(2):
            for j in range(npair):
                carry_ref[part, 8 * j:8 * j + 8, :] = c[part * npair + j]
                hfin_ref[part, 8 * j:8 * j + 8, :] = c[part * npair + j]
        for j in range(npair):
            cols8 = slice(j * 8 * LANES, (j + 1) * 8 * LANES)
            tr_ref[:, cols8] = hr_ref[j].reshape(t, 8 * LANES).astype(tr_ref.dtype)
            ti_ref[:, cols8] = hi_ref[j].reshape(t, 8 * LANES).astype(ti_ref.dtype)
        for k in range(nb):
            cols = slice(k * spb, (k + 1) * spb)
            y_ref[:, k * LANES:(k + 1) * LANES] = (_mxu_dot(tr_ref[:, cols], cre_ref[k]) + _mxu_dot(ti_ref[:, cols], ncim_ref[k]))

    full = lambda a: pl.BlockSpec(a.shape, lambda i: (0,) * a.ndim)
    hspec = pl.BlockSpec((npair, t, 8, LANES), lambda i: (0, tile(i), 0, 0))
    tspec = pl.BlockSpec((t, nsr * LANES), lambda i: (tile(i), 0))
    hsh = jax.ShapeDtypeStruct((npair, l, 8, LANES), F32)
    tsh = jax.ShapeDtypeStruct((l, nsr * LANES), MXU_DTYPE)
    return pl.pallas_call(
        body, name=name,
        out_shape=(jax.ShapeDtypeStruct((l, w), F32), hsh, hsh, tsh, tsh, jax.ShapeDtypeStruct((2, nsr, LANES), F32)),
        grid=(n,),
        in_specs=[pl.BlockSpec((t, w), lambda i: (tile(i), u_col)), full(h0)] + [_dir_spec(a, dr) for a in (a_sm, wbr, wbi, cre, ncim)],
        out_specs=(pl.BlockSpec((t, w), lambda i: (tile(i), 0)), hspec, hspec, tspec, tspec,
                   pl.BlockSpec((2, nsr, LANES), lambda i: (0, 0, 0))),
        scratch_shapes=[pltpu.VMEM((2, nsr, LANES), F32)],
        compiler_params=_cparams("arbitrary"))(u_arr, h0, a_sm, wbr, wbi, cre, ncim)


def _s5_bwd(dys, u_arr, u_col, w, hr, hi, tr, ti, hbound, g_in, a_sm, wbr_t, wbi_t, cre_t, ncim_t, dr, name):
    rev = dr == 1
    l = u_arr.shape[0]
    nb = w // LANES
    spb = wbr_t.shape[-2]
    nsr = a_sm.shape[2]
    npair = nb // 2
    mt = spb // LANES
    rpt = LANES // mt
    t = _tile(l, 128)
    n = l // t
    with_dy = dys is not None
    tile = (lambda i: i) if rev else (lambda i: n - 1 - i)

    def body(*refs):
        if with_dy:
            (dy_ref, u_ref, hr_ref, hi_ref, pr_ref, pi_ref, tr_ref, ti_ref, hb_ref, gin_ref, a_ref, wbrt_ref, wbit_ref,
             cret_ref, ncimt_ref, du_ref, dwbr_ref, dwbi_ref, dcre_ref, dncim_ref, da_ref, gout_ref,
             gr_ref, gi_ref, gtr_ref, gti_ref, carry_ref) = refs
        else:
            (u_ref, hr_ref, hi_ref, pr_ref, pi_ref, hb_ref, gin_ref, a_ref, wbrt_ref, wbit_ref,
             du_ref, dwbr_ref, dwbi_ref, da_ref, gout_ref, gr_ref, gi_ref, gtr_ref, gti_ref, carry_ref) = refs
        i = pl.program_id(0)

        @pl.when(i == 0)
        def _():
            carry_ref[...] = gin_ref[...]
            accs = (dwbr_ref, dwbi_ref, da_ref) + ((dcre_ref, dncim_ref) if with_dy else ())
            for r in accs:
                r[...] = jnp.zeros_like(r)

        if with_dy:
            for j in range(npair):
                for g_ref, c_ref in ((gr_ref, cret_ref), (gi_ref, ncimt_ref)):
                    g_ref[j] = _mxu_dot(dy_ref[:, 2 * j * LANES:2 * (j + 1) * LANES], c_ref[j]).reshape(t, 8, LANES)
        else:
            gr_ref[...] = jnp.zeros_like(gr_ref)
            gi_ref[...] = jnp.zeros_like(gi_ref)
        slab = lambda ref, part, j: ref[part, 8 * j:8 * j + 8, :]
        last = t - 1 if rev else 0
        first = i == n - 1

        ar = [slab(a_ref, 0, j) for j in range(npair)]
        ai = [slab(a_ref, 1, j) for j in range(npair)]

        def steps(blk, c):
            gr, gi, dr, di = (list(c[q * npair:(q + 1) * npair]) for q in range(4))
            for q in range(SCAN_UNROLL):
                s = blk * SCAN_UNROLL + q
                row = s if rev else t - 1 - s
                prow = jnp.minimum(row + 1, t - 1) if rev else jnp.maximum(row - 1, 0)
                for j in range(npair):
                    pr, pi = hr_ref[j, prow], hi_ref[j, prow]
                    gr[j], gi[j] = (gr_ref[j, row] + ar[j] * gr[j] + ai[j] * gi[j],
                                    gi_ref[j, row] + ar[j] * gi[j] - ai[j] * gr[j])
                    gr_ref[j, row] = gr[j]
                    gi_ref[j, row] = gi[j]
                    dr[j], di[j] = dr[j] + gr[j] * pr + gi[j] * pi, di[j] + gi[j] * pr - gr[j] * pi
            return tuple(gr + gi + dr + di)

        init = tuple(slab(ref, part, j) for ref in (carry_ref, da_ref) for part in range(2) for j in range(npair))
        c = lax.fori_loop(0, t // SCAN_UNROLL, steps, init)
        gr, gi, dr, di = (c[q * npair:(q + 1) * npair] for q in range(4))
        for j in range(npair):
            pr = jnp.where(first, slab(hb_ref, 0, j), pr_ref[j, 0]) - hr_ref[j, last]
            pi = jnp.where(first, slab(hb_ref, 1, j), pi_ref[j, 0]) - hi_ref[j, last]
            rows = slice(8 * j, 8 * j + 8)
            da_ref[0, rows, :] = dr[j] + gr[j] * pr + gi[j] * pi
            da_ref[1, rows, :] = di[j] + gi[j] * pr - gr[j] * pi
            for part, val in enumerate((gr[j], gi[j])):
                carry_ref[part, rows, :] = val
                gout_ref[part, rows, :] = val

        for j in range(npair):
            cols8 = slice(j * 8 * LANES, (j + 1) * 8 * LANES)
            gtr_ref[:, cols8] = gr_ref[j].reshape(t, 8 * LANES).astype(gtr_ref.dtype)
            gti_ref[:, cols8] = gi_ref[j].reshape(t, 8 * LANES).astype(gti_ref.dtype)
        for k in range(nb):
            cols = slice(k * spb, (k + 1) * spb)
            lanes = slice(k * LANES, (k + 1) * LANES)
            du_ref[:, lanes] = _mxu_dot(gtr_ref[:, cols], wbrt_ref[k]) + _mxu_dot(gti_ref[:, cols], wbit_ref[k])
            ut = u_ref[:, lanes].T.astype(MXU_DTYPE)
            dyt = dy_ref[:, lanes].T.astype(MXU_DTYPE) if with_dy else None
            for m in range(mt):
                tcols = slice(k * spb + m * LANES, k * spb + (m + 1) * LANES)
                rows = slice(m * rpt, (m + 1) * rpt)
                dwbr_ref[k, m] += _mxu_dot(ut[rows], gtr_ref[:, tcols])
                dwbi_ref[k, m] += _mxu_dot(ut[rows], gti_ref[:, tcols])
                if with_dy:
                    dcre_ref[k, m] += _mxu_dot(dyt[rows], tr_ref[:, tcols])
                    dncim_ref[k, m] += _mxu_dot(dyt[rows], ti_ref[:, tcols])

    full = lambda a: pl.BlockSpec(a.shape, lambda i: (0,) * a.ndim)
    row = lambda cb: pl.BlockSpec((t, w), lambda i: (tile(i), cb))
    hspec = pl.BlockSpec((npair, t, 8, LANES), lambda i: (0, tile(i), 0, 0))
    if rev:
        pspec = pl.BlockSpec((npair, 1, 8, LANES), lambda i: (0, jnp.minimum((tile(i) + 1) * t, l - 1), 0, 0))
    else:
        pspec = pl.BlockSpec((npair, 1, 8, LANES), lambda i: (0, jnp.maximum(tile(i) * t - 1, 0), 0, 0))
    sm = jax.ShapeDtypeStruct((2, nsr, LANES), F32)
    smspec = pl.BlockSpec((2, nsr, LANES), lambda i: (0, 0, 0))
    wsh = csh = jax.ShapeDtypeStruct((nb, mt, rpt, LANES), F32)
    tspec = pl.BlockSpec((t, nsr * LANES), lambda i: (tile(i), 0))
    in_specs = (([row(0)] if with_dy else []) + [row(u_col), hspec, hspec, pspec, pspec] + ([tspec, tspec] if with_dy else [])
                + [full(hbound), full(g_in)]
                + [_dir_spec(a, dr) for a in (a_sm, wbr_t, wbi_t) + ((cre_t, ncim_t) if with_dy else ())])
    args = (([dys] if with_dy else []) + [u_arr, hr, hi, hr, hi] + ([tr, ti] if with_dy else [])
            + [hbound, g_in, a_sm, wbr_t, wbi_t] + ([cre_t, ncim_t] if with_dy else []))
    out_shape = (jax.ShapeDtypeStruct((l, w), F32), wsh, wsh) + ((csh, csh) if with_dy else ()) + (sm, sm)
    out_specs = (row(0), full(wsh), full(wsh)) + ((full(csh), full(csh)) if with_dy else ()) + (smspec, smspec)
    return pl.pallas_call(
        body, name=name, out_shape=out_shape, grid=(n,), in_specs=in_specs, out_specs=out_specs,
        scratch_shapes=[pltpu.VMEM((npair, t, 8, LANES), F32)] * 2 + [pltpu.VMEM((t, nsr * LANES), MXU_DTYPE)] * 2
        + [pltpu.VMEM((2, nsr, LANES), F32)],
        compiler_params=_cparams("arbitrary"))(*args)


def _glu_fwd(y0, y1, proj, cat, d_skip, w_glu, b_glu, w):
    l = y0.shape[0]
    tl = _tile(l, 256)

    def body(y0_ref, y1_ref, u_ref, z_ref, cat_in, d_ref, wg_ref, bg_ref, ys_ref, cat_ref):
        del cat_in
        ys = y0_ref[...] + y1_ref[...] + d_ref[...] * u_ref[...]
        ys_ref[...] = ys
        gy = _gelu(ys)
        s = _mxu_dot(gy, wg_ref[...]) + bg_ref[...]
        cat_ref[...] = (gy * jax.nn.sigmoid(s) * jax.nn.silu(z_ref[...])).astype(cat_ref.dtype)

    row = pl.BlockSpec((tl, w), lambda i: (i, 0))
    vec = pl.BlockSpec((1, w), lambda i: (0, 0))
    return pl.pallas_call(
        body, name="glu_fwd", out_shape=(jax.ShapeDtypeStruct((l, w), F32), jax.ShapeDtypeStruct(cat.shape, cat.dtype)),
        grid=(l // tl,),
        in_specs=[row, row, pl.BlockSpec((tl, w), lambda i: (i, 3)), pl.BlockSpec((tl, w), lambda i: (i, 4)), HBM,
                  vec, pl.BlockSpec((w, w), lambda i: (0, 0)), vec],
        out_specs=(row, pl.BlockSpec((tl, w), lambda i: (i, 1))), input_output_aliases={4: 1},
        compiler_params=_cparams("parallel"))(y0, y1, proj, proj, cat, d_skip, w_glu, b_glu)


def _glu_bwd(dcat, ys, proj, d_skip, w_glu, b_glu, w):
    l = ys.shape[0]
    tl = _tile(l, 256)

    def body(dy_ref, ys_ref, u_ref, z_ref, d_ref, wg_ref, bg_ref, dys_ref, dp_ref, dbg_ref, dd_ref, dwg_ref):
        i = pl.program_id(0)
        ys_t = ys_ref[...]
        z = z_ref[...]
        dyb = dy_ref[...].astype(F32)
        gy = _gelu(ys_t)
        sg = jax.nn.sigmoid(_mxu_dot(gy, wg_ref[...]) + bg_ref[...])
        dp_ref[...] = (dyb * gy * sg * _silu_grad(z)).astype(dp_ref.dtype)
        dglu = dyb * jax.nn.silu(z)
        ds = dglu * gy * sg * (1.0 - sg)
        dgy = dglu * sg + _mxu_dot(ds, wg_ref[...], _NT)
        dys_t = dgy * _gelu_grad(ys_t)
        dys_ref[...] = dys_t

        @pl.when(i == 0)
        def _():
            for r in (dbg_ref, dd_ref, dwg_ref):
                r[...] = jnp.zeros_like(r)

        dbg_ref[...] += jnp.sum(ds, axis=0, keepdims=True)
        dd_ref[...] += jnp.sum(dys_t * u_ref[...], axis=0, keepdims=True)
        dwg_ref[...] += _mxu_dot(gy, ds, _TN)

    row = pl.BlockSpec((tl, w), lambda i: (i, 0))
    vec = pl.BlockSpec((1, w), lambda i: (0, 0))
    mat = pl.BlockSpec((w, w), lambda i: (0, 0))
    v = jax.ShapeDtypeStruct((1, w), F32)
    return pl.pallas_call(
        body, name="glu_bwd",
        out_shape=(jax.ShapeDtypeStruct((l, w), F32), jax.ShapeDtypeStruct((l, 5 * w), MXU_DTYPE), v, v,
                   jax.ShapeDtypeStruct((w, w), F32)),
        grid=(l // tl,),
        in_specs=[pl.BlockSpec((tl, w), lambda i: (i, 1)), row, pl.BlockSpec((tl, w), lambda i: (i, 3)),
                  pl.BlockSpec((tl, w), lambda i: (i, 4)), vec, mat, vec],
        out_specs=(row, pl.BlockSpec((tl, w), lambda i: (i, 4)), vec, vec, mat),
        compiler_params=_cparams("arbitrary"))(dcat, ys, proj, proj, d_skip, w_glu, b_glu)


def _add2(a, b, name):
    l, w = a.shape
    tl = _tile(l, 512)

    def body(a_ref, b_ref, o_ref):
        o_ref[...] = a_ref[...] + b_ref[...]

    row = pl.BlockSpec((tl, w), lambda i: (i, 0))
    return pl.pallas_call(body, name=name, out_shape=jax.ShapeDtypeStruct((l, w), F32), grid=(l // tl,),
                          in_specs=[row, row], out_specs=row, compiler_params=_cparams("parallel"))(a, b)


def _adamw_nd(w, m, v, g, name):
    shape = w.shape
    lead = math.prod(shape[:-2]) if len(shape) > 2 else 1
    b, c = (shape[-2], shape[-1]) if len(shape) >= 2 else (1, shape[-1])
    t3 = (lead, b, c)
    padded_row = -(-b // 8) * 8 * -(-c // LANES) * LANES * 4
    ta = _tile(lead, max(1, (2 << 20) // padded_row))

    def body(w_ref, m_ref, v_ref, g_ref, d_ref, mo_ref, vo_ref):
        gv = g_ref[...]
        mn = ADAM_B1 * m_ref[...] + (1.0 - ADAM_B1) * gv
        vn = ADAM_B2 * v_ref[...] + (1.0 - ADAM_B2) * (gv * gv)
        m_hat = mn / (1.0 - ADAM_B1 ** ADAM_STEP)
        v_hat = vn / (1.0 - ADAM_B2 ** ADAM_STEP)
        d_ref[...] = -ADAM_LR * (m_hat / (jnp.sqrt(v_hat) + ADAM_EPS) + ADAM_WD * w_ref[...])
        mo_ref[...] = mn
        vo_ref[...] = vn

    blk = pl.BlockSpec((ta, b, c), lambda i: (i, 0, 0))
    s = jax.ShapeDtypeStruct(t3, F32)
    outs = pl.pallas_call(body, name=name, out_shape=(s, s, s), grid=(lead // ta,), in_specs=[blk] * 4, out_specs=(blk,) * 3,
                          compiler_params=_cparams("parallel"))(*[a.reshape(t3) for a in (w, m, v, g)])
    return tuple(o.reshape(shape) for o in outs)


def _adamw(w, m, v, gparts, name):
    r, c = w.shape
    np_ = gparts.shape[0]
    tr = _tile(r, max(8, (1 << 18) // c), 8)

    def body(w_ref, m_ref, v_ref, g_ref, go_ref, d_ref, mo_ref, vo_ref):
        g = g_ref[0].astype(F32)
        for p in range(1, np_):
            g = g + g_ref[p].astype(F32)
        mn = ADAM_B1 * m_ref[...] + (1.0 - ADAM_B1) * g
        vn = ADAM_B2 * v_ref[...] + (1.0 - ADAM_B2) * (g * g)
        m_hat = mn / (1.0 - ADAM_B1 ** ADAM_STEP)
        v_hat = vn / (1.0 - ADAM_B2 ** ADAM_STEP)
        go_ref[...] = g
        d_ref[...] = -ADAM_LR * (m_hat / (jnp.sqrt(v_hat) + ADAM_EPS) + ADAM_WD * w_ref[...])
        mo_ref[...] = mn
        vo_ref[...] = vn

    row = pl.BlockSpec((tr, c), lambda i: (i, 0))
    s = jax.ShapeDtypeStruct((r, c), F32)
    return pl.pallas_call(body, name=name, out_shape=(s, s, s, s), grid=(r // tr,),
                          in_specs=[row, row, row, pl.BlockSpec((np_, tr, c), lambda i: (0, i, 0))],
                          out_specs=(row, row, row, row), compiler_params=_cparams("parallel"))(w, m, v, gparts)


def _sum_slots(parts, name):
    np_, r, c = parts.shape

    def body(p_ref, o_ref):
        g = p_ref[0]
        for p in range(1, np_):
            g = g + p_ref[p]
        o_ref[...] = g

    return pl.pallas_call(body, name=name, out_shape=jax.ShapeDtypeStruct((r, c), F32), compiler_params=_cparams())(parts)


def _block_diag(x, gb):
    nd, g, a, b = x.shape
    eye = jnp.eye(gb, dtype=x.dtype)
    y = jnp.einsum("dkgab,gh->dkgahb", x.reshape(nd, g // gb, gb, a, b), eye)
    return y.reshape(nd, g // gb, gb * a, gb * b)


def _tile_diag_extract(y, nd, c, p):
    gpt = LANES // p
    eye = jnp.eye(gpt, dtype=y.dtype)
    x = jnp.einsum("dnacbp,ab->dnacp", y.reshape(nd, -1, gpt, c, gpt, p), eye)
    return x.reshape(nd, -1, c, p)


def kernel(x, c, ctx, c_ctx, w_ada, b_ada, w_in, sgu_ln_g, sgu_ln_b, w_spatial, b_spatial, s5_lam_re, s5_lam_im, s5_log_step, s5_b_re, s5_b_im, s5_c_re, s5_c_im, s5_d, w_glu, b_glu, w_out, ln_g, ln_b, loss_target, m_c_ctx, m_w_ada, m_b_ada, m_w_in, m_sgu_ln_g, m_sgu_ln_b, m_w_spatial, m_b_spatial, m_s5_lam_re, m_s5_lam_im, m_s5_log_step, m_s5_b_re, m_s5_b_im, m_s5_c_re, m_s5_c_im, m_s5_d, m_w_glu, m_b_glu, m_w_out, m_ln_g, m_ln_b, v_c_ctx, v_w_ada, v_b_ada, v_w_in, v_sgu_ln_g, v_sgu_ln_b, v_w_spatial, v_b_spatial, v_s5_lam_re, v_s5_lam_im, v_s5_log_step, v_s5_b_re, v_s5_b_im, v_s5_c_re, v_s5_c_im, v_s5_d, v_w_glu, v_b_glu, v_w_out, v_ln_g, v_ln_b):
    small_names = ["c_ctx", "b_ada", "sgu_ln_g", "sgu_ln_b", "w_spatial", "b_spatial", "s5_lam_re", "s5_lam_im",
                   "s5_log_step", "s5_b_re", "s5_b_im", "s5_c_re", "s5_c_im", "s5_d", "b_glu", "ln_g", "ln_b"]
    env = dict(locals())
    x2, tgt, ctx2 = x[0], loss_target[0], ctx[0]
    l, d = x2.shape
    lc = ctx2.shape[0]
    w = d // 2
    nh = w // HEAD_DIM_A
    nd, g_s5, p_s5, c_s5 = s5_b_re.shape[1:]
    ns = g_s5 * p_s5
    nsr = ns // LANES
    gb = LANES // c_s5
    me = _index(_mesh_pos())
    ada_cols = w_ada.shape[2]

    srows = _silu_rows(c, c_ctx)
    srows_all = _all_gather(srows, 0, "gather_silu")
    s_mat = jnp.concatenate([srows_all[0::8], srows_all[1:2], jnp.zeros((7, d), F32)], axis=0)
    mod_part = _small_dot(s_mat, w_ada[0], "nn", "mod_cols")
    mod_all = _all_gather(mod_part, 1, "gather_mod") + b_ada
    hw_in, tok_a = _exchange_start(w_in[0].astype(MXU_DTYPE), 1, "gather", "start_gather_w_in", (SIBLING,) + SAME_CORE_PEERS)
    mod_all = mod_all + tok_a[0, 0]
    mod_x = lax.dynamic_slice_in_dim(mod_all, me, 1, axis=0)
    mod_c = mod_all[8:9]
    shift_x, scale_x, gate_x = mod_x[:, :d], mod_x[:, d:2 * d], mod_x[:, 2 * d:]
    shift_c, scale_c = mod_c[:, :d], mod_c[:, d:2 * d]

    lr, li = s5_lam_re[0][:, :, None, :], s5_lam_im[0][:, :, None, :]
    ls = s5_log_step[0][:, :, None, None]
    swapped = ("s5_b_re", "s5_b_im")
    for nm in swapped:
        for pre in ("", "m_", "v_"):
            env[pre + nm] = jnp.swapaxes(env[pre + nm], -1, -2)
    br_t, bi_t = env["s5_b_re"][0], env["s5_b_im"][0]
    ab_re, ab_im, bb_re, bb_im = _disc_fwd(lr, li, ls, br_t, bi_t)
    a_sm = jnp.stack([ab_re, ab_im], axis=1).reshape(nd, 2, nsr, LANES)
    wbr = _block_diag(bb_re, 2 * gb).astype(MXU_DTYPE)
    wbi = _block_diag(bb_im, 2 * gb).astype(MXU_DTYPE)
    cre_t = _block_diag(s5_c_re[0], 2 * gb).astype(MXU_DTYPE)
    ncim_t = _block_diag(-s5_c_im[0], 2 * gb).astype(MXU_DTYPE)
    cre = jnp.swapaxes(_block_diag(s5_c_re[0], gb), 2, 3).astype(MXU_DTYPE)
    ncim = jnp.swapaxes(_block_diag(-s5_c_im[0], gb), 2, 3).astype(MXU_DTYPE)
    wbr_t = jnp.swapaxes(_block_diag(bb_re, gb), 2, 3).astype(MXU_DTYPE)
    wbi_t = jnp.swapaxes(_block_diag(bb_im, gb), 2, 3).astype(MXU_DTYPE)
    d_skip = s5_d

    xm = _ln_mod(x2, shift_x, scale_x, "ln_mod_x")
    cm = _ln_mod(ctx2, shift_c, scale_c, "ln_mod_ctx")
    ready = xm[:8, :LANES].astype(F32) + cm[:8, :LANES].astype(F32) + cre[0, 0, :8, :].astype(F32)
    hw_in2, tok_b = _forward_start(_exchange_wait(hw_in, ready, "wait_gather_w_in"), 1, "start_forward_w_in")
    w_in_f = _forward_wait(hw_in2, tok_b, "wait_forward_w_in")
    hw_glu, tok_c = _exchange_start(w_glu[0].astype(MXU_DTYPE), 0, "gather", "start_gather_w_glu")
    hw_out, tok_o = _exchange_start(w_out[0].astype(MXU_DTYPE), 0, "gather", "start_gather_w_out")
    proj = _matmul(xm, w_in_f, mode="nn", name="proj", bm=512, bn=1024, bk=d, dep=tok_c + tok_o)
    ub_c = _matmul(cm, w_in_f, mode="nn", name="proj_ctx", bm=256, bn=w, bk=d, b_n0=3 * w, n=w)
    bsf = jnp.repeat(b_spatial[0].T, HEAD_DIM_A, axis=1)
    ws = w_spatial[0]
    cat = _ga_fwd(proj, sgu_ln_g, sgu_ln_b, ws, bsf, w)
    zeros_state = jnp.zeros((2, nsr, LANES), F32)
    s5c, s5l = [], []
    for dr in range(nd):
        s5c.append(_s5_fwd(ub_c, 0, w, zeros_state, a_sm, wbr, wbi, cre, ncim, dr, f"s5_fwd_ctx{dr}"))
        s5l.append(_s5_fwd(proj, 3, w, s5c[dr][5], a_sm, wbr, wbi, cre, ncim, dr, f"s5_fwd{dr}"))
    w_glu_f = _exchange_wait(hw_glu, s5l[1][0], "wait_gather_w_glu")
    ys, cat = _glu_fwd(s5l[0][0], s5l[1][0], proj, cat, d_skip, w_glu_f, b_glu, w)
    w_out_f = _exchange_wait(hw_out, ys, "wait_gather_w_out")
    out = _matmul(cat, w_out_f, mode="nn", name="out_proj", bm=512, bn=1024, bk=2 * w)
    loss_row, dout, dx_res, dgate, dln_g, dln_b = _post_ln_loss(x2, out, gate_x, ln_g, ln_b, tgt)

    dcat = _matmul(dout, w_out_f, mode="nt", name="d_cat", bm=512, bn=1024, bk=d, out_dtype=MXU_DTYPE)
    dw_out = _matmul(cat, dout, mode="tn", name="d_w_out", bm=1024, bn=1024, bk=2048, out_dtype=MXU_DTYPE)
    hg_out, tok_d = _exchange_start(dw_out, 0, "a2a", "start_a2a_d_w_out")
    dys, dproj, db_glu, dd_skip, dw_glu = _glu_bwd(dcat, ys, proj, d_skip + tok_d[0, 0], w_glu_f, b_glu, w)
    hg_glu, tok_e = _exchange_start(dw_glu.astype(MXU_DTYPE), 0, "a2a", "start_a2a_d_w_glu")
    zeros_state = zeros_state + tok_e[0, 0]
    du_l, du_c, dwbr, dwbi, dcre, dncim, da_sm = [], [], [], [], [], [], []
    for dr in range(nd):
        bl = _s5_bwd(dys, proj, 3, w, *s5l[dr][1:5], s5c[dr][5], zeros_state, a_sm, wbr_t, wbi_t,
                     cre_t, ncim_t, dr, f"s5_bwd{dr}")
        bc = _s5_bwd(None, ub_c, 0, w, s5c[dr][1], s5c[dr][2], None, None, zeros_state, bl[6], a_sm, wbr_t, wbi_t,
                     None, None, dr, f"s5_bwd_ctx{dr}")
        du_l.append(bl[0])
        du_c.append(bc[0])
        dwbr.append(_add2(bl[1].reshape(-1, LANES), bc[1].reshape(-1, LANES), f"sum_dwbr{dr}"))
        dwbi.append(_add2(bl[2].reshape(-1, LANES), bc[2].reshape(-1, LANES), f"sum_dwbi{dr}"))
        dcre.append(bl[3])
        dncim.append(bl[4])
        da_sm.append(_add2(bl[5].reshape(2 * nsr, LANES), bc[3].reshape(2 * nsr, LANES), f"sum_da{dr}"))
    dub_c = _add2(du_c[0], du_c[1], "dub_ctx")
    dwbr, dwbi, dcre, dncim = jnp.stack(dwbr), jnp.stack(dwbi), jnp.stack(dcre), jnp.stack(dncim)
    da_sm = jnp.stack(da_sm).reshape(nd, 2, g_s5, p_s5)
    dproj, dsg, dsb, dws, dbsf = _ga_bwd(proj, dcat, dproj, sgu_ln_g, sgu_ln_b, ws, bsf, dys, du_l[0], du_l[1], d_skip, w)

    dbb_re = _tile_diag_extract(dwbr, nd, c_s5, p_s5)
    dbb_im = _tile_diag_extract(dwbi, nd, c_s5, p_s5)
    dc_re = _tile_diag_extract(dcre, nd, c_s5, p_s5)
    dc_im = -_tile_diag_extract(dncim, nd, c_s5, p_s5)
    dlr, dli, dls, db_re, db_im = _disc_bwd(lr, li, ls, br_t, bi_t, da_sm[:, 0:1].reshape(nd, g_s5, 1, p_s5),
                                            da_sm[:, 1:2].reshape(nd, g_s5, 1, p_s5), dbb_re, dbb_im)
    expand = (jnp.arange(w)[:, None] // HEAD_DIM_A == jnp.arange(LANES)[None, :]).astype(F32)
    db_sp = _lane_group_sum(dbsf, expand, "d_b_spatial")[:, :nh].T

    local = {"sgu_ln_g": dsg, "sgu_ln_b": dsb, "w_spatial": dws, "b_spatial": db_sp,
             "s5_lam_re": dlr, "s5_lam_im": dli, "s5_log_step": dls, "s5_b_re": db_re, "s5_b_im": db_im,
             "s5_c_re": dc_re, "s5_c_im": dc_im, "s5_d": dd_skip, "b_glu": db_glu, "ln_g": dln_g, "ln_b": dln_b}
    reduced = [n for n in small_names if n in local]
    loss_part = (0.5 / d) * jnp.sum(loss_row)
    flat = jnp.concatenate([loss_part.reshape(1)] + [local[n].reshape(-1) for n in reduced])
    unit = N_DEV * 8 * LANES
    total = -(-flat.shape[0] // unit) * unit
    flat = jnp.pad(flat, (0, total - flat.shape[0])).reshape(N_DEV * 8, total // (N_DEV * 8))
    h_small, tok_s = _exchange_start(flat, 0, "a2a", "start_a2a_small")

    dw_in = _matmul(xm, dproj, mode="tn", name="d_w_in", bm=1024, bn=1280, bk=2048, out_dtype=MXU_DTYPE, dep=tok_s)
    dw_in = _matmul(cm, dub_c, mode="tn", name="d_w_in_ctx", bm=1024, bn=w, bk=lc, acc_in=dw_in, acc_n0=3 * w,
                    out_dtype=MXU_DTYPE)
    hg_in, tok_f = _exchange_start(dw_in, 1, "a2a", "start_a2a_d_w_in")
    mine = _sum_slots(_exchange_wait(h_small, dw_in, "wait_a2a_small"), "sum_small")
    h_sums, tok_g = _exchange_start(mine, 0, "gather", "start_gather_small")
    dxm = _matmul(dproj, w_in_f, mode="nt", name="d_xm", bm=512, bn=1024, bk=5 * w, dep=tok_f + tok_g,
                  out_dtype=MXU_DTYPE)
    dcm = _matmul(dub_c, w_in_f, mode="nt", name="d_cm", bm=256, bn=1024, bk=w, b_k0=3 * w, k=w)
    grad_x, dshift_x, dscale_x = _ln_mod_bwd(x2, dxm, scale_x, dx_res, "ln_mod_x_bwd")
    _, dshift_c, dscale_c = _ln_mod_bwd(ctx2, dcm, scale_c, None, "ln_mod_ctx_bwd")

    dmod_rows = jnp.concatenate([jnp.concatenate([dshift_x, dscale_x, dgate], axis=1),
                                 jnp.concatenate([dshift_c, dscale_c, jnp.zeros((1, d), F32)], axis=1),
                                 jnp.zeros((6, 3 * d), F32)], axis=0)
    dmod_all = _all_gather(dmod_rows, 0, "gather_dmod")
    dmod_ctx = _sum_slots(dmod_all[1::8].reshape(N_DEV, 1, 3 * d), "sum_dmod_ctx")
    dmod_mat = jnp.concatenate([dmod_all[0::8], dmod_ctx, jnp.zeros((7, 3 * d), F32)], axis=0)
    db_ada = _sum_slots(dmod_mat[:9].reshape(9, 1, 3 * d), "sum_db_ada")
    dmod_mine = lax.dynamic_slice_in_dim(dmod_mat, me * ada_cols, ada_cols, axis=1)
    dw_ada = _small_dot(s_mat, dmod_mine, "tn", "d_w_ada")
    dsilu_cc = _small_dot(dmod_mine[8:16], w_ada[0], "nt", "d_silu_cctx")[0:1]
    dc_ctx_part = dsilu_cc * _silu_grad(c_ctx.reshape(1, d))
    dc_ctx_rows = jnp.concatenate([dc_ctx_part, jnp.zeros((7, d), F32)], axis=0)
    dc_ctx_all = _all_gather(dc_ctx_rows, 0, "gather_d_c_ctx")
    dc_ctx = _sum_slots(dc_ctx_all[0::8].reshape(N_DEV, 1, d), "sum_d_c_ctx").reshape(d)

    summed = _exchange_wait(h_sums, dc_ctx, "wait_gather_small").reshape(-1)
    loss = summed[0]
    grads, off = {"b_ada": db_ada, "c_ctx": dc_ctx}, 1
    for n in reduced:
        size = math.prod(env[n].shape)
        grads[n] = summed[off:off + size].reshape(env[n].shape)
        off += size

    gp_w_out = _exchange_wait(hg_out, summed, "wait_a2a_d_w_out")
    gp_w_glu = _exchange_wait(hg_glu, summed, "wait_a2a_d_w_glu")
    gp_w_in = _exchange_wait(hg_in, summed, "wait_a2a_d_w_in")
    big = {
        "w_ada": _adamw(w_ada[0], m_w_ada[0], v_w_ada[0], dw_ada[None], "adamw_w_ada"),
        "w_in": _adamw(w_in[0], m_w_in[0], v_w_in[0], gp_w_in, "adamw_w_in"),
        "w_glu": _adamw(w_glu[0], m_w_glu[0], v_w_glu[0], gp_w_glu, "adamw_w_glu"),
        "w_out": _adamw(w_out[0], m_w_out[0], v_w_out[0], gp_w_out, "adamw_w_out"),
    }
    res = {n: tuple(a[None] for a in big[n]) for n in big}
    for n in small_names:
        res[n] = (grads[n],) + _adamw_nd(env[n], env["m_" + n], env["v_" + n], grads[n], "adamw_" + n)
        if n in swapped:
            res[n] = tuple(jnp.swapaxes(a, -1, -2) for a in res[n])

    order = ["c_ctx", "w_ada", "b_ada", "w_in", "sgu_ln_g", "sgu_ln_b", "w_spatial", "b_spatial", "s5_lam_re", "s5_lam_im",
             "s5_log_step", "s5_b_re", "s5_b_im", "s5_c_re", "s5_c_im", "s5_d", "w_glu", "b_glu", "w_out", "ln_g", "ln_b"]
    return (loss, grad_x[None], *[res[n][0] for n in order], *[res[n][1] for n in order],
            *[res[n][2] for n in order], *[res[n][3] for n in order])
```

```python
import functools
import math

import jax
import jax.numpy as jnp
from jax import lax
from jax.experimental import pallas as pl
from jax.experimental.pallas import tpu as pltpu

F32 = jnp.float32
MXU_DTYPE = jnp.bfloat16
N_DEV = 8
MESH_ID = pl.DeviceIdType.MESH
LN_EPS = 1e-6
DEPTH = 1
ALPHA = (2.0 * DEPTH) ** 0.25
CHUNK = 128
HEAD_DIM_A = 128
ADAM_LR, ADAM_B1, ADAM_B2, ADAM_EPS, ADAM_WD, ADAM_STEP = 0.001, 0.9, 0.999, 1e-08, 0.01, 10
LANES = 128
SCAN_UNROLL = 8
VMEM_LIMIT = 56 * 1024 * 1024
HBM = pl.BlockSpec(memory_space=pl.ANY)


def _cparams(*sem):
    return pltpu.CompilerParams(dimension_semantics=sem if sem else None, vmem_limit_bytes=VMEM_LIMIT)


def _tile(n, pref, mult=1):
    if n <= pref:
        return n
    t = pref - pref % mult
    while n % t:
        t -= mult
    return t


def _gelu(x):
    return 0.5 * x * (1.0 + lax.erf(x * (1.0 / math.sqrt(2.0))))


def _gelu_grad(x):
    return 0.5 * (1.0 + lax.erf(x * (1.0 / math.sqrt(2.0)))) + x * jnp.exp(-0.5 * x * x) * (1.0 / math.sqrt(2.0 * math.pi))


def _silu_grad(x):
    s = jax.nn.sigmoid(x)
    return s * (1.0 + x * (1.0 - s))


def _mxu_dot(a, b, dims=(((1,), (0,)), ((), ()))):
    return lax.dot_general(a.astype(MXU_DTYPE), b.astype(MXU_DTYPE), dims, preferred_element_type=F32)


_NT = (((1,), (1,)), ((), ()))
_TN = (((0,), (0,)), ((), ()))


def _mesh_pos():
    return lax.axis_index("x"), lax.axis_index("y"), lax.axis_index("c")


def _peer(pos, r):
    x, y, c = pos
    return ((1 - x) if r & 4 else x, (1 - y) if r & 2 else y, (1 - c) if r & 1 else c)


def _index(pos):
    return 4 * pos[0] + 2 * pos[1] + pos[2]


def _slice_of(ref, axis, idx, size):
    start = idx * size
    if axis == 0:
        return ref.at[pl.ds(start, size)]
    return ref.at[:, pl.ds(start, size)]


def _all_gather(x, axis, name):
    size = x.shape[axis]
    out_shape = tuple(s * N_DEV if a == axis else s for a, s in enumerate(x.shape))

    def body(x_ref, o_ref, send_sems, recv_sems, local_sem):
        me = _mesh_pos()
        mine = pltpu.make_async_copy(x_ref, _slice_of(o_ref, axis, _index(me), size), local_sem)
        mine.start()

        def copy(r, block):
            return pltpu.make_async_remote_copy(
                src_ref=x_ref, dst_ref=_slice_of(o_ref, axis, _index(block), size),
                send_sem=send_sems.at[r - 1], recv_sem=recv_sems.at[r - 1],
                device_id=_peer(me, r), device_id_type=MESH_ID)

        sends = [copy(r, me) for r in range(1, N_DEV)]
        for cp in sends:
            cp.start()
        for r in range(1, N_DEV):
            copy(r, _peer(me, r)).wait_recv()
        for cp in sends:
            cp.wait_send()
        mine.wait()

    return pl.pallas_call(
        body, name=name, out_shape=jax.ShapeDtypeStruct(out_shape, x.dtype),
        in_specs=[HBM], out_specs=HBM,
        scratch_shapes=[pltpu.SemaphoreType.DMA((N_DEV - 1,)), pltpu.SemaphoreType.DMA((N_DEV - 1,)),
                        pltpu.SemaphoreType.DMA],
    )(x)


def _all_to_all(x, axis, name):
    size = x.shape[axis] // N_DEV
    slot = tuple(size if a == axis else s for a, s in enumerate(x.shape))

    def body(x_ref, o_ref, send_sems, recv_sems, local_sem):
        me = _mesh_pos()
        mine = pltpu.make_async_copy(_slice_of(x_ref, axis, _index(me), size), o_ref.at[_index(me)], local_sem)
        mine.start()

        def copy(r, sender, receiver):
            return pltpu.make_async_remote_copy(
                src_ref=_slice_of(x_ref, axis, _index(receiver), size), dst_ref=o_ref.at[_index(sender)],
                send_sem=send_sems.at[r - 1], recv_sem=recv_sems.at[r - 1],
                device_id=_peer(me, r), device_id_type=MESH_ID)

        sends = [copy(r, me, _peer(me, r)) for r in range(1, N_DEV)]
        for cp in sends:
            cp.start()
        for r in range(1, N_DEV):
            copy(r, _peer(me, r), me).wait_recv()
        for cp in sends:
            cp.wait_send()
        mine.wait()

    return pl.pallas_call(
        body, name=name, out_shape=jax.ShapeDtypeStruct((N_DEV,) + slot, x.dtype),
        in_specs=[HBM], out_specs=HBM,
        scratch_shapes=[pltpu.SemaphoreType.DMA((N_DEV - 1,)), pltpu.SemaphoreType.DMA((N_DEV - 1,)),
                        pltpu.SemaphoreType.DMA],
    )(x)


_SEM = pl.BlockSpec(memory_space=pltpu.SEMAPHORE)
_HBM = pl.BlockSpec(memory_space=pltpu.HBM)
_EFFECT = pltpu.SideEffectType.DATAFLOW_SIDE_EFFECTING
ALL_PEERS = tuple(range(1, N_DEV))
SIBLING = 1
SAME_CORE_PEERS = (2, 4, 6)


def _exchange_copy(kind, x_ref, land_ref, axis, size, send_sems, recv_sems, me, rels, q, arriving):
    peer = _peer(me, rels[q])
    sender, receiver = (peer, me) if arriving else (me, peer)
    if kind == "gather":
        src, dst = x_ref, _slice_of(land_ref, axis, _index(sender), size)
    else:
        src, dst = _slice_of(x_ref, axis, _index(receiver), size), land_ref.at[_index(sender)]
    return pltpu.make_async_remote_copy(src_ref=src, dst_ref=dst, send_sem=send_sems.at[q], recv_sem=recv_sems.at[q],
                                        device_id=peer, device_id_type=MESH_ID)


def _local_copy(kind, x_ref, land_ref, axis, size, me, local_sem):
    if kind == "gather":
        return pltpu.make_async_copy(x_ref, _slice_of(land_ref, axis, _index(me), size), local_sem)
    return pltpu.make_async_copy(_slice_of(x_ref, axis, _index(me), size), land_ref.at[_index(me)], local_sem)


def _exchange_start(x, axis, kind, name, rels=ALL_PEERS):
    size = x.shape[axis] if kind == "gather" else x.shape[axis] // N_DEV
    if kind == "gather":
        land_shape = tuple(s * N_DEV if a == axis else s for a, s in enumerate(x.shape))
    else:
        land_shape = (N_DEV,) + tuple(size if a == axis else s for a, s in enumerate(x.shape))

    def body(x_ref, land_ref, send_sems, recv_sems, local_sem, x_thru, land_thru, token):
        del x_thru, land_thru
        me = _mesh_pos()
        _local_copy(kind, x_ref, land_ref, axis, size, me, local_sem).start()
        for q in range(len(rels)):
            _exchange_copy(kind, x_ref, land_ref, axis, size, send_sems, recv_sems, me, rels, q, False).start()
        token[...] = jnp.zeros_like(token)

    sems = pltpu.SemaphoreType.DMA((len(rels),))
    send_sems, recv_sems, local_sem, x_thru, land_thru, token = pl.pallas_call(
        body, name=name,
        out_shape=(sems, sems, pltpu.SemaphoreType.DMA(()), pltpu.HBM(x.shape, x.dtype), pltpu.HBM(land_shape, x.dtype),
                   jax.ShapeDtypeStruct((8, LANES), F32)),
        in_specs=(_HBM, _HBM), out_specs=(_SEM, _SEM, _SEM, _HBM, _HBM, pl.BlockSpec(memory_space=pltpu.VMEM)),
        input_output_aliases={0: 3, 1: 4}, compiler_params=pltpu.CompilerParams(has_side_effects=_EFFECT),
    )(pltpu.with_memory_space_constraint(x, pltpu.HBM),
      pltpu.with_memory_space_constraint(lax.empty(land_shape, x.dtype), pltpu.HBM))
    return (kind, axis, size, rels, send_sems, recv_sems, local_sem, x_thru, land_thru), token


def _exchange_wait(handle, after, name):
    kind, axis, size, rels, send_sems, recv_sems, local_sem, x_thru, land_thru = handle

    def body(x_ref, land_ref, send_sems, recv_sems, local_sem, after_ref, x_dead, got_ref):
        del after_ref, x_dead, got_ref
        me = _mesh_pos()
        _local_copy(kind, x_ref, land_ref, axis, size, me, local_sem).wait()
        for q in range(len(rels)):
            _exchange_copy(kind, x_ref, land_ref, axis, size, send_sems, recv_sems, me, rels, q, False).wait_send()
        for q in range(len(rels)):
            _exchange_copy(kind, x_ref, land_ref, axis, size, send_sems, recv_sems, me, rels, q, True).wait_recv()

    return pl.pallas_call(
        body, name=name, out_shape=(pltpu.HBM(x_thru.shape, x_thru.dtype), pltpu.HBM(land_thru.shape, land_thru.dtype)),
        in_specs=(_HBM, _HBM, _SEM, _SEM, _SEM, HBM), out_specs=(_HBM, _HBM), input_output_aliases={0: 0, 1: 1},
        compiler_params=pltpu.CompilerParams(has_side_effects=_EFFECT),
    )(x_thru, land_thru, send_sems, recv_sems, local_sem, after)[1]


def _forward_copy(land_ref, axis, size, send_sems, recv_sems, me, q, arriving):
    sibling = _peer(me, SIBLING)
    owner = _peer(sibling if arriving else me, SAME_CORE_PEERS[q])
    block = _slice_of(land_ref, axis, _index(owner), size)
    return pltpu.make_async_remote_copy(src_ref=block, dst_ref=block, send_sem=send_sems.at[q], recv_sem=recv_sems.at[q],
                                        device_id=sibling, device_id_type=MESH_ID)


def _forward_start(land, axis, name):
    size = land.shape[axis] // N_DEV

    def body(land_ref, send_sems, recv_sems, land_thru, token):
        del land_thru
        me = _mesh_pos()
        for q in range(len(SAME_CORE_PEERS)):
            _forward_copy(land_ref, axis, size, send_sems, recv_sems, me, q, False).start()
        token[...] = jnp.zeros_like(token)

    sems = pltpu.SemaphoreType.DMA((len(SAME_CORE_PEERS),))
    send_sems, recv_sems, land_thru, token = pl.pallas_call(
        body, name=name, out_shape=(sems, sems, pltpu.HBM(land.shape, land.dtype), jax.ShapeDtypeStruct((8, LANES), F32)),
        in_specs=(_HBM,), out_specs=(_SEM, _SEM, _HBM, pl.BlockSpec(memory_space=pltpu.VMEM)),
        input_output_aliases={0: 2}, compiler_params=pltpu.CompilerParams(has_side_effects=_EFFECT),
    )(land)
    return (axis, size, send_sems, recv_sems, land_thru), token


def _forward_wait(handle, after, name):
    axis, size, send_sems, recv_sems, land_thru = handle

    def body(land_ref, send_sems, recv_sems, after_ref, got_ref):
        del after_ref, got_ref
        me = _mesh_pos()
        for q in range(len(SAME_CORE_PEERS)):
            _forward_copy(land_ref, axis, size, send_sems, recv_sems, me, q, False).wait_send()
        for q in range(len(SAME_CORE_PEERS)):
            _forward_copy(land_ref, axis, size, send_sems, recv_sems, me, q, True).wait_recv()

    return pl.pallas_call(
        body, name=name, out_shape=pltpu.HBM(land_thru.shape, land_thru.dtype),
        in_specs=(_HBM, _SEM, _SEM, HBM), out_specs=_HBM, input_output_aliases={0: 0},
        compiler_params=pltpu.CompilerParams(has_side_effects=_EFFECT),
    )(land_thru, send_sems, recv_sems, after)


def _matmul(a, b, *, mode, name, bm=512, bn=512, bk=512, out_dtype=F32, b_n0=0, n=None, b_k0=0, k=None,
            acc_in=None, acc_n0=0, dep=None):
    if mode == "tn":
        kk, m = a.shape
    else:
        m, kk = a.shape
    if mode == "nn":
        n = b.shape[1] if n is None else n
    elif mode == "nt":
        n = b.shape[0]
        kk = kk if k is None else k
    else:
        n = b.shape[1]
    bm, bn, bk = _tile(m, bm), _tile(n, bn), _tile(kk, bk)
    nk = kk // bk
    assert b_n0 % bn == 0 and b_k0 % bk == 0 and acc_n0 % bn == 0
    dims = {"nn": (((1,), (0,)), ((), ())), "nt": _NT, "tn": _TN}[mode]

    n_in = 2 + (acc_in is not None) + (dep is not None)

    def body(*refs):
        a_ref, b_ref = refs[:2]
        init = refs[2] if acc_in is not None else None
        o_ref = refs[n_in]
        acc_ref = refs[-1] if nk > 1 else None
        p = _mxu_dot(a_ref[...], b_ref[...], dims)
        if nk == 1:
            o_ref[...] = (p if init is None else p + init[...]).astype(out_dtype)
            return
        ki = pl.program_id(2)

        @pl.when(ki == 0)
        def _():
            acc_ref[...] = p if init is None else p + init[...]

        @pl.when(ki > 0)
        def _():
            acc_ref[...] += p

        @pl.when(ki == nk - 1)
        def _():
            o_ref[...] = acc_ref[...].astype(out_dtype)

    a_spec = pl.BlockSpec((bk, bm), lambda j, i, q: (q, i)) if mode == "tn" else pl.BlockSpec((bm, bk), lambda j, i, q: (i, q))
    if mode == "nt":
        b_spec = pl.BlockSpec((bn, bk), lambda j, i, q: (j, q + b_k0 // bk))
    else:
        b_spec = pl.BlockSpec((bk, bn), lambda j, i, q: (q, j + b_n0 // bn))
    in_specs, args, aliases = [a_spec, b_spec], [a, b], {}
    out_map = lambda j, i, q: (i, j + acc_n0 // bn)
    if acc_in is not None:
        in_specs.append(pl.BlockSpec((bm, bn), out_map))
        args.append(acc_in)
        aliases = {2: 0}
        out_shape = jax.ShapeDtypeStruct(acc_in.shape, out_dtype)
    else:
        out_shape = jax.ShapeDtypeStruct((m, n), out_dtype)
    if dep is not None:
        in_specs.append(HBM)
        args.append(dep)
    return pl.pallas_call(
        body, name=name, out_shape=out_shape, grid=(n // bn, m // bm, nk),
        in_specs=in_specs, out_specs=pl.BlockSpec((bm, bn), out_map),
        scratch_shapes=[pltpu.VMEM((bm, bn), F32)] if nk > 1 else [],
        input_output_aliases=aliases,
        compiler_params=_cparams("parallel", "parallel", "arbitrary"),
    )(*args)


def _silu_rows(c, c_ctx):
    d = c.shape[-1]

    def body(c_ref, cc_ref, o_ref):
        o_ref[...] = jnp.zeros_like(o_ref)
        o_ref[0:1, :] = jax.nn.silu(c_ref[...])
        o_ref[1:2, :] = jax.nn.silu(cc_ref[...])

    return pl.pallas_call(body, name="silu_rows", out_shape=jax.ShapeDtypeStruct((8, d), F32))(
        c.reshape(1, d), c_ctx.reshape(1, d))


def _small_dot(a, b, mode, name):
    dims = {"nn": (((1,), (0,)), ((), ())), "nt": _NT, "tn": _TN}[mode]
    m = a.shape[1] if mode == "tn" else a.shape[0]
    n = b.shape[0] if mode == "nt" else b.shape[1]

    def body(a_ref, b_ref, o_ref):
        o_ref[...] = lax.dot_general(a_ref[...], b_ref[...], dims, preferred_element_type=F32,
                                     precision=lax.Precision.HIGHEST)

    return pl.pallas_call(body, name=name, out_shape=jax.ShapeDtypeStruct((m, n), F32),
                          compiler_params=_cparams())(a, b)


def _ln_stats(x):
    mu = jnp.mean(x, axis=-1, keepdims=True)
    xc = x - mu
    var = jnp.mean(xc * xc, axis=-1, keepdims=True)
    rstd = lax.rsqrt(var + LN_EPS)
    return xc * rstd, rstd


def _ln_mod(x, shift, scale, name):
    l, d = x.shape
    tl = _tile(l, 256)

    def body(x_ref, sh_ref, sc_ref, o_ref):
        xhat, _ = _ln_stats(x_ref[...])
        o_ref[...] = (xhat * (1.0 + sc_ref[...]) + sh_ref[...]).astype(o_ref.dtype)

    row = pl.BlockSpec((tl, d), lambda i: (i, 0))
    vec = pl.BlockSpec((1, d), lambda i: (0, 0))
    return pl.pallas_call(body, name=name, out_shape=jax.ShapeDtypeStruct((l, d), MXU_DTYPE), grid=(l // tl,),
                          in_specs=[row, vec, vec], out_specs=row, compiler_params=_cparams("parallel"))(x, shift, scale)


def _ln_mod_bwd(x, dxm, scale, res, name):
    l, d = x.shape
    tl = _tile(l, 256)
    with_res = res is not None

    def body(*refs):
        if with_res:
            x_ref, g_ref, sc_ref, r_ref, dx_ref, dsh_ref, dsc_ref = refs
        else:
            x_ref, g_ref, sc_ref, dx_ref, dsh_ref, dsc_ref = refs
        i = pl.program_id(0)
        xhat, rstd = _ln_stats(x_ref[...])
        g = g_ref[...].astype(F32)
        dxh = g * (1.0 + sc_ref[...])
        dx = rstd * (dxh - jnp.mean(dxh, axis=-1, keepdims=True) - xhat * jnp.mean(dxh * xhat, axis=-1, keepdims=True))
        dx_ref[...] = dx + r_ref[...].astype(F32) if with_res else dx

        @pl.when(i == 0)
        def _():
            dsh_ref[...] = jnp.zeros_like(dsh_ref)
            dsc_ref[...] = jnp.zeros_like(dsc_ref)

        dsh_ref[...] += jnp.sum(g, axis=0, keepdims=True)
        dsc_ref[...] += jnp.sum(g * xhat, axis=0, keepdims=True)

    row = pl.BlockSpec((tl, d), lambda i: (i, 0))
    vec = pl.BlockSpec((1, d), lambda i: (0, 0))
    args = [x, dxm, scale] + ([res] if with_res else [])
    return pl.pallas_call(
        body, name=name,
        out_shape=(jax.ShapeDtypeStruct((l, d), F32), jax.ShapeDtypeStruct((1, d), F32), jax.ShapeDtypeStruct((1, d), F32)),
        grid=(l // tl,), in_specs=[row, row, vec] + ([row] if with_res else []), out_specs=(row, vec, vec),
        compiler_params=_cparams("arbitrary"))(*args)


def _post_ln_loss(x, out, gate, ln_g, ln_b, target):
    l, d = x.shape
    tl = _tile(l, 256)

    def body(x_ref, o_ref, gate_ref, g_ref, b_ref, t_ref, loss_ref, dout_ref, dxr_ref, dgate_ref, dg_ref, db_ref):
        i = pl.program_id(0)
        out_t = o_ref[...]
        gate_v = gate_ref[...]
        rhat, rstd = _ln_stats(ALPHA * x_ref[...] + gate_v * out_t)
        ln_gv = g_ref[...]
        diff = rhat * ln_gv + b_ref[...] - t_ref[...]
        dy = diff * (1.0 / d)
        drh = dy * ln_gv
        dr = rstd * (drh - jnp.mean(drh, axis=-1, keepdims=True) - rhat * jnp.mean(drh * rhat, axis=-1, keepdims=True))
        dout_ref[...] = (gate_v * dr).astype(dout_ref.dtype)
        dxr_ref[...] = (ALPHA * dr).astype(dxr_ref.dtype)

        @pl.when(i == 0)
        def _():
            for r in (loss_ref, dgate_ref, dg_ref, db_ref):
                r[...] = jnp.zeros_like(r)

        loss_ref[...] += jnp.sum(diff * diff, axis=0, keepdims=True)
        dgate_ref[...] += jnp.sum(dr * out_t, axis=0, keepdims=True)
        dg_ref[...] += jnp.sum(dy * rhat, axis=0, keepdims=True)
        db_ref[...] += jnp.sum(dy, axis=0, keepdims=True)

    row = pl.BlockSpec((tl, d), lambda i: (i, 0))
    vec = pl.BlockSpec((1, d), lambda i: (0, 0))
    v = jax.ShapeDtypeStruct((1, d), F32)
    return pl.pallas_call(
        body, name="post_ln_loss",
        out_shape=(v, jax.ShapeDtypeStruct((l, d), MXU_DTYPE), jax.ShapeDtypeStruct((l, d), MXU_DTYPE), v, v, v),
        grid=(l // tl,), in_specs=[row, row, vec, vec, vec, row], out_specs=(vec, row, row, vec, vec, vec),
        compiler_params=_cparams("arbitrary"))(x, out, gate, ln_g, ln_b, target)


def _ga_forward_tile(p, g, b, ws_ref, bsf, w, nc, nh):
    u_raw, v_raw, za = p[:, :w], p[:, w:2 * w], p[:, 2 * w:3 * w]
    gu = _gelu(u_raw)
    vhat, rstd = _ln_stats(_gelu(v_raw))
    vn = vhat * g + b
    rows = []
    for ci in range(nc):
        r0 = ci * CHUNK
        heads = [_mxu_dot(ws_ref[h], vn[r0:r0 + CHUNK, h * HEAD_DIM_A:(h + 1) * HEAD_DIM_A]) for h in range(nh)]
        rows.append(jnp.concatenate(heads, axis=1) + bsf)
    mixed = jnp.concatenate(rows, axis=0) if nc > 1 else rows[0]
    return u_raw, v_raw, za, gu, vhat, rstd, vn, mixed


def _ga_fwd(proj, g, b, ws, bsf, w):
    l = proj.shape[0]
    nh = w // HEAD_DIM_A
    nc = _tile(l // CHUNK, 2)
    tl = nc * CHUNK

    def body(p_ref, g_ref, b_ref, ws_ref, bsf_ref, o_ref):
        _, _, za, gu, _, _, _, mixed = _ga_forward_tile(p_ref[...], g_ref[...], b_ref[...], ws_ref, bsf_ref[...], w, nc, nh)
        o_ref[...] = (gu * mixed * jax.nn.silu(za)).astype(o_ref.dtype)

    vec = pl.BlockSpec((1, w), lambda i: (0, 0))
    return pl.pallas_call(
        body, name="ga_fwd", out_shape=jax.ShapeDtypeStruct((l, 2 * w), MXU_DTYPE), grid=(l // tl,),
        in_specs=[pl.BlockSpec((tl, 3 * w), lambda i: (i, 0)), vec, vec,
                  pl.BlockSpec((nh, CHUNK, CHUNK), lambda i: (0, 0, 0)), pl.BlockSpec((CHUNK, w), lambda i: (0, 0))],
        out_specs=pl.BlockSpec((tl, w), lambda i: (i, 0)), compiler_params=_cparams("parallel"))(proj, g, b, ws, bsf)


def _ga_bwd(proj, dcat, dproj, g, b, ws, bsf, dys, du0, du1, d_skip, w):
    l = proj.shape[0]
    nh = w // HEAD_DIM_A
    nc = _tile(l // CHUNK, 2)
    tl = nc * CHUNK

    def body(p_ref, dy_ref, dp_in, g_ref, b_ref, ws_ref, bsf_ref, dys_ref, du0_ref, du1_ref, d_ref,
             dp_ref, dg_ref, db_ref, dws_ref, dbsf_ref):
        del dp_in
        i = pl.program_id(0)
        dp_ref[:, 3 * w:] = (dys_ref[...] * d_ref[...] + du0_ref[...] + du1_ref[...]).astype(dp_ref.dtype)
        gv = g_ref[...]
        u_raw, v_raw, za, gu, vhat, rstd, vn, mixed = _ga_forward_tile(
            p_ref[...], gv, b_ref[...], ws_ref, bsf_ref[...], w, nc, nh)
        dya = dy_ref[...].astype(F32)
        sz = jax.nn.silu(za)
        dmixed = dya * gu * sz
        dza = dya * gu * mixed * _silu_grad(za)
        dgu = dya * mixed * sz

        @pl.when(i == 0)
        def _():
            for r in (dg_ref, db_ref, dws_ref, dbsf_ref):
                r[...] = jnp.zeros_like(r)

        rows = []
        for ci in range(nc):
            r0 = ci * CHUNK
            heads = []
            for h in range(nh):
                cols = slice(h * HEAD_DIM_A, (h + 1) * HEAD_DIM_A)
                dm = dmixed[r0:r0 + CHUNK, cols]
                heads.append(_mxu_dot(ws_ref[h], dm, _TN))
                dws_ref[h] += _mxu_dot(dm, vn[r0:r0 + CHUNK, cols], _NT)
            rows.append(jnp.concatenate(heads, axis=1))
            dbsf_ref[...] += dmixed[r0:r0 + CHUNK, :]
        dvn = jnp.concatenate(rows, axis=0) if nc > 1 else rows[0]
        dg_ref[...] += jnp.sum(dvn * vhat, axis=0, keepdims=True)
        db_ref[...] += jnp.sum(dvn, axis=0, keepdims=True)
        dvh = dvn * gv
        dgv = rstd * (dvh - jnp.mean(dvh, axis=-1, keepdims=True) - vhat * jnp.mean(dvh * vhat, axis=-1, keepdims=True))
        dp_ref[:, :w] = (dgu * _gelu_grad(u_raw)).astype(dp_ref.dtype)
        dp_ref[:, w:2 * w] = (dgv * _gelu_grad(v_raw)).astype(dp_ref.dtype)
        dp_ref[:, 2 * w:3 * w] = dza.astype(dp_ref.dtype)

    vec = pl.BlockSpec((1, w), lambda i: (0, 0))
    row = pl.BlockSpec((tl, w), lambda i: (i, 0))
    ws_spec = pl.BlockSpec((nh, CHUNK, CHUNK), lambda i: (0, 0, 0))
    bs_spec = pl.BlockSpec((CHUNK, w), lambda i: (0, 0))
    v = jax.ShapeDtypeStruct((1, w), F32)
    return pl.pallas_call(
        body, name="ga_bwd",
        out_shape=(jax.ShapeDtypeStruct(dproj.shape, dproj.dtype), v, v, jax.ShapeDtypeStruct((nh, CHUNK, CHUNK), F32),
                   jax.ShapeDtypeStruct((CHUNK, w), F32)),
        grid=(l // tl,),
        in_specs=[pl.BlockSpec((tl, 3 * w), lambda i: (i, 0)), row, HBM, vec, vec, ws_spec, bs_spec, row, row, row, vec],
        out_specs=(pl.BlockSpec((tl, 4 * w), lambda i: (i, 0)), vec, vec, ws_spec, bs_spec),
        input_output_aliases={2: 0}, compiler_params=_cparams("arbitrary"))(
            proj, dcat, dproj, g, b, ws, bsf, dys, du0, du1, d_skip)


def _lane_group_sum(x, expand, name):
    return _small_dot(x, expand, "nn", name)


def _disc_math(lr, li, ls, br, bi):
    step = jnp.exp(ls)
    dr, di = lr * step, li * step
    mag = jnp.exp(dr)
    ab_re, ab_im = mag * jnp.cos(di), mag * jnp.sin(di)
    den = lr * lr + li * li
    nr, ni = ab_re - 1.0, ab_im
    f_re = (nr * lr + ni * li) / den
    f_im = (ni * lr - nr * li) / den
    bb_re = f_re * br - f_im * bi
    bb_im = f_re * bi + f_im * br
    return ab_re, ab_im, bb_re, bb_im


def _disc_fwd(lr, li, ls, br, bi):
    def body(lr_ref, li_ref, ls_ref, br_ref, bi_ref, o1, o2, o3, o4):
        res = _disc_math(lr_ref[...], li_ref[...], ls_ref[...], br_ref[...], bi_ref[...])
        for o, r in zip((o1, o2, o3, o4), res):
            o[...] = r

    s = lambda a: jax.ShapeDtypeStruct(a.shape, F32)
    return pl.pallas_call(body, name="s5_disc", out_shape=(s(lr), s(lr), s(br), s(br)), compiler_params=_cparams())(
        lr, li, ls, br, bi)


def _disc_bwd(lr, li, ls, br, bi, d_ar, d_ai, d_br, d_bi):
    def body(lr_ref, li_ref, ls_ref, br_ref, bi_ref, c1, c2, c3, c4, o1, o2, o3, o4, o5):
        _, vjp = jax.vjp(_disc_math, lr_ref[...], li_ref[...], ls_ref[...], br_ref[...], bi_ref[...])
        res = vjp((c1[...], c2[...], c3[...], c4[...]))
        for o, r in zip((o1, o2, o3, o4, o5), res):
            o[...] = r

    s = lambda a: jax.ShapeDtypeStruct(a.shape, F32)
    return pl.pallas_call(body, name="s5_disc_bwd", out_shape=(s(lr), s(lr), s(ls), s(br), s(br)),
                          compiler_params=_cparams())(lr, li, ls, br, bi, d_ar, d_ai, d_br, d_bi)


def _dir_spec(a, dr):
    return pl.BlockSpec((None,) + a.shape[1:], lambda i: (dr,) + (0,) * (a.ndim - 1))


def _s5_fwd(u_arr, u_col, w, h0, a_sm, wbr, wbi, cre, ncim, dr, name):
    rev = dr == 1
    l = u_arr.shape[0]
    nb = w // LANES
    spb = wbr.shape[-1]
    nsr = a_sm.shape[2]
    assert 2 * spb == 8 * LANES and nb % 2 == 0
    npair = nb // 2
    t = _tile(l, 256)
    n = l // t
    tile = (lambda i: n - 1 - i) if rev else (lambda i: i)

    def body(u_ref, h0_ref, a_ref, wbr_ref, wbi_ref, cre_ref, ncim_ref, y_ref, hr_ref, hi_ref, tr_ref, ti_ref, hfin_ref,
             carry_ref):
        i = pl.program_id(0)

        @pl.when(i == 0)
        def _():
            carry_ref[...] = h0_ref[...]

        for j in range(npair):
            for h_ref, w_ref in ((hr_ref, wbr_ref), (hi_ref, wbi_ref)):
                blk = [_mxu_dot(u_ref[:, k * LANES:(k + 1) * LANES], w_ref[k]) for k in (2 * j, 2 * j + 1)]
                h_ref[j] = jnp.concatenate(blk, axis=1).reshape(t, 8, LANES)
        slab = lambda ref, part, j: ref[part, 8 * j:8 * j + 8, :]
        ar = [slab(a_ref, 0, j) for j in range(npair)]
        ai = [slab(a_ref, 1, j) for j in range(npair)]

        def steps(blk, c):
            hr, hi = list(c[:npair]), list(c[npair:])
            for q in range(SCAN_UNROLL):
                s = blk * SCAN_UNROLL + q
                row = t - 1 - s if rev else s
                for j in range(npair):
                    hr[j], hi[j] = (ar[j] * hr[j] - ai[j] * hi[j] + hr_ref[j, row],
                                    ar[j] * hi[j] + ai[j] * hr[j] + hi_ref[j, row])
                    hr_ref[j, row] = hr[j]
                    hi_ref[j, row] = hi[j]
            return tuple(hr + hi)

        init = tuple(slab(carry_ref, part, j) for part in range(2) for j in range(npair))
        c = lax.fori_loop(0, t // SCAN_UNROLL, steps, init)
        for part in range(2):
            for j in range(npair):
                carry_ref[part, 8 * j:8 * j + 8, :] = c[part * npair + j]
                hfin_ref[part, 8 * j:8 * j + 8, :] = c[part * npair + j]
        for j in range(npair):
            cols8 = slice(j * 8 * LANES, (j + 1) * 8 * LANES)
            tr_ref[:, cols8] = hr_ref[j].reshape(t, 8 * LANES).astype(tr_ref.dtype)
            ti_ref[:, cols8] = hi_ref[j].reshape(t, 8 * LANES).astype(ti_ref.dtype)
        for k in range(nb):
            cols = slice(k * spb, (k + 1) * spb)
            y_ref[:, k * LANES:(k + 1) * LANES] = (_mxu_dot(tr_ref[:, cols], cre_ref[k]) + _mxu_dot(ti_ref[:, cols], ncim_ref[k]))

    full = lambda a: pl.BlockSpec(a.shape, lambda i: (0,) * a.ndim)
    hspec = pl.BlockSpec((npair, t, 8, LANES), lambda i: (0, tile(i), 0, 0))
    tspec = pl.BlockSpec((t, nsr * LANES), lambda i: (tile(i), 0))
    hsh = jax.ShapeDtypeStruct((npair, l, 8, LANES), F32)
    tsh = jax.ShapeDtypeStruct((l, nsr * LANES), MXU_DTYPE)
    return pl.pallas_call(
        body, name=name,
        out_shape=(jax.ShapeDtypeStruct((l, w), F32), hsh, hsh, tsh, tsh, jax.ShapeDtypeStruct((2, nsr, LANES), F32)),
        grid=(n,),
        in_specs=[pl.BlockSpec((t, w), lambda i: (tile(i), u_col)), full(h0)] + [_dir_spec(a, dr) for a in (a_sm, wbr, wbi, cre, ncim)],
        out_specs=(pl.BlockSpec((t, w), lambda i: (tile(i), 0)), hspec, hspec, tspec, tspec,
                   pl.BlockSpec((2, nsr, LANES), lambda i: (0, 0, 0))),
        scratch_shapes=[pltpu.VMEM((2, nsr, LANES), F32)],
        compiler_params=_cparams("arbitrary"))(u_arr, h0, a_sm, wbr, wbi, cre, ncim)


def _s5_bwd(dys, u_arr, u_col, w, hr, hi, tr, ti, hbound, g_in, a_sm, wbr_t, wbi_t, cre_t, ncim_t, dr, name):
    rev = dr == 1
    l = u_arr.shape[0]
    nb = w // LANES
    spb = wbr_t.shape[-2]
    nsr = a_sm.shape[2]
    npair = nb // 2
    t = _tile(l, 128)
    n = l // t
    with_dy = dys is not None
    tile = (lambda i: i) if rev else (lambda i: n - 1 - i)

    def body(*refs):
        if with_dy:
            (dy_ref, u_ref, hr_ref, hi_ref, pr_ref, pi_ref, tr_ref, ti_ref, hb_ref, gin_ref, a_ref, wbrt_ref, wbit_ref,
             cret_ref, ncimt_ref, du_ref, dwbr_ref, dwbi_ref, dcre_ref, dncim_ref, da_ref, gout_ref,
             gr_ref, gi_ref, gtr_ref, gti_ref, carry_ref) = refs
        else:
            (u_ref, hr_ref, hi_ref, pr_ref, pi_ref, hb_ref, gin_ref, a_ref, wbrt_ref, wbit_ref,
             du_ref, dwbr_ref, dwbi_ref, da_ref, gout_ref, gr_ref, gi_ref, gtr_ref, gti_ref, carry_ref) = refs
        i = pl.program_id(0)

        @pl.when(i == 0)
        def _():
            carry_ref[...] = gin_ref[...]
            accs = (dwbr_ref, dwbi_ref, da_ref) + ((dcre_ref, dncim_ref) if with_dy else ())
            for r in accs:
                r[...] = jnp.zeros_like(r)

        if with_dy:
            for j in range(npair):
                for g_ref, c_ref in ((gr_ref, cret_ref), (gi_ref, ncimt_ref)):
                    blk = [_mxu_dot(dy_ref[:, k * LANES:(k + 1) * LANES], c_ref[k]) for k in (2 * j, 2 * j + 1)]
                    g_ref[j] = jnp.concatenate(blk, axis=1).reshape(t, 8, LANES)
        else:
            gr_ref[...] = jnp.zeros_like(gr_ref)
            gi_ref[...] = jnp.zeros_like(gi_ref)
        slab = lambda ref, part, j: ref[part, 8 * j:8 * j + 8, :]
        last = t - 1 if rev else 0
        first = i == n - 1

        ar = [slab(a_ref, 0, j) for j in range(npair)]
        ai = [slab(a_ref, 1, j) for j in range(npair)]

        def steps(blk, c):
            gr, gi, dr, di = (list(c[q * npair:(q + 1) * npair]) for q in range(4))
            for q in range(SCAN_UNROLL):
                s = blk * SCAN_UNROLL + q
                row = s if rev else t - 1 - s
                prow = jnp.minimum(row + 1, t - 1) if rev else jnp.maximum(row - 1, 0)
                for j in range(npair):
                    pr, pi = hr_ref[j, prow], hi_ref[j, prow]
                    gr[j], gi[j] = (gr_ref[j, row] + ar[j] * gr[j] + ai[j] * gi[j],
                                    gi_ref[j, row] + ar[j] * gi[j] - ai[j] * gr[j])
                    gr_ref[j, row] = gr[j]
                    gi_ref[j, row] = gi[j]
                    dr[j], di[j] = dr[j] + gr[j] * pr + gi[j] * pi, di[j] + gi[j] * pr - gr[j] * pi
            return tuple(gr + gi + dr + di)

        init = tuple(slab(ref, part, j) for ref in (carry_ref, da_ref) for part in range(2) for j in range(npair))
        c = lax.fori_loop(0, t // SCAN_UNROLL, steps, init)
        gr, gi, dr, di = (c[q * npair:(q + 1) * npair] for q in range(4))
        for j in range(npair):
            pr = jnp.where(first, slab(hb_ref, 0, j), pr_ref[j, 0]) - hr_ref[j, last]
            pi = jnp.where(first, slab(hb_ref, 1, j), pi_ref[j, 0]) - hi_ref[j, last]
            rows = slice(8 * j, 8 * j + 8)
            da_ref[0, rows, :] = dr[j] + gr[j] * pr + gi[j] * pi
            da_ref[1, rows, :] = di[j] + gi[j] * pr - gr[j] * pi
            for part, val in enumerate((gr[j], gi[j])):
                carry_ref[part, rows, :] = val
                gout_ref[part, rows, :] = val

        for j in range(npair):
            cols8 = slice(j * 8 * LANES, (j + 1) * 8 * LANES)
            gtr_ref[:, cols8] = gr_ref[j].reshape(t, 8 * LANES).astype(gtr_ref.dtype)
            gti_ref[:, cols8] = gi_ref[j].reshape(t, 8 * LANES).astype(gti_ref.dtype)
        for k in range(nb):
            cols = slice(k * spb, (k + 1) * spb)
            lanes = slice(k * LANES, (k + 1) * LANES)
            du_ref[:, lanes] = _mxu_dot(gtr_ref[:, cols], wbrt_ref[k]) + _mxu_dot(gti_ref[:, cols], wbit_ref[k])
            dwbr_ref[k] += _mxu_dot(u_ref[:, lanes], gtr_ref[:, cols], _TN)
            dwbi_ref[k] += _mxu_dot(u_ref[:, lanes], gti_ref[:, cols], _TN)
            if with_dy:
                dcre_ref[k] += _mxu_dot(tr_ref[:, cols], dy_ref[:, lanes], _TN)
                dncim_ref[k] += _mxu_dot(ti_ref[:, cols], dy_ref[:, lanes], _TN)

    full = lambda a: pl.BlockSpec(a.shape, lambda i: (0,) * a.ndim)
    row = lambda cb: pl.BlockSpec((t, w), lambda i: (tile(i), cb))
    hspec = pl.BlockSpec((npair, t, 8, LANES), lambda i: (0, tile(i), 0, 0))
    if rev:
        pspec = pl.BlockSpec((npair, 1, 8, LANES), lambda i: (0, jnp.minimum((tile(i) + 1) * t, l - 1), 0, 0))
    else:
        pspec = pl.BlockSpec((npair, 1, 8, LANES), lambda i: (0, jnp.maximum(tile(i) * t - 1, 0), 0, 0))
    sm = jax.ShapeDtypeStruct((2, nsr, LANES), F32)
    smspec = pl.BlockSpec((2, nsr, LANES), lambda i: (0, 0, 0))
    wsh = jax.ShapeDtypeStruct((nb, LANES, spb), F32)
    csh = jax.ShapeDtypeStruct((nb, spb, LANES), F32)
    tspec = pl.BlockSpec((t, nsr * LANES), lambda i: (tile(i), 0))
    in_specs = (([row(0)] if with_dy else []) + [row(u_col), hspec, hspec, pspec, pspec] + ([tspec, tspec] if with_dy else [])
                + [full(hbound), full(g_in)]
                + [_dir_spec(a, dr) for a in (a_sm, wbr_t, wbi_t) + ((cre_t, ncim_t) if with_dy else ())])
    args = (([dys] if with_dy else []) + [u_arr, hr, hi, hr, hi] + ([tr, ti] if with_dy else [])
            + [hbound, g_in, a_sm, wbr_t, wbi_t] + ([cre_t, ncim_t] if with_dy else []))
    out_shape = (jax.ShapeDtypeStruct((l, w), F32), wsh, wsh) + ((csh, csh) if with_dy else ()) + (sm, sm)
    out_specs = (row(0), full(wsh), full(wsh)) + ((full(csh), full(csh)) if with_dy else ()) + (smspec, smspec)
    return pl.pallas_call(
        body, name=name, out_shape=out_shape, grid=(n,), in_specs=in_specs, out_specs=out_specs,
        scratch_shapes=[pltpu.VMEM((npair, t, 8, LANES), F32)] * 2 + [pltpu.VMEM((t, nsr * LANES), MXU_DTYPE)] * 2
        + [pltpu.VMEM((2, nsr, LANES), F32)],
        compiler_params=_cparams("arbitrary"))(*args)


def _glu_fwd(y0, y1, proj, cat, d_skip, w_glu, b_glu, w):
    l = y0.shape[0]
    tl = _tile(l, 256)

    def body(y0_ref, y1_ref, u_ref, z_ref, cat_in, d_ref, wg_ref, bg_ref, ys_ref, cat_ref):
        del cat_in
        ys = y0_ref[...] + y1_ref[...] + d_ref[...] * u_ref[...]
        ys_ref[...] = ys
        gy = _gelu(ys)
        s = _mxu_dot(gy, wg_ref[...]) + bg_ref[...]
        cat_ref[...] = (gy * jax.nn.sigmoid(s) * jax.nn.silu(z_ref[...])).astype(cat_ref.dtype)

    row = pl.BlockSpec((tl, w), lambda i: (i, 0))
    vec = pl.BlockSpec((1, w), lambda i: (0, 0))
    return pl.pallas_call(
        body, name="glu_fwd", out_shape=(jax.ShapeDtypeStruct((l, w), F32), jax.ShapeDtypeStruct(cat.shape, cat.dtype)),
        grid=(l // tl,),
        in_specs=[row, row, pl.BlockSpec((tl, w), lambda i: (i, 3)), pl.BlockSpec((tl, w), lambda i: (i, 4)), HBM,
                  vec, pl.BlockSpec((w, w), lambda i: (0, 0)), vec],
        out_specs=(row, pl.BlockSpec((tl, w), lambda i: (i, 1))), input_output_aliases={4: 1},
        compiler_params=_cparams("parallel"))(y0, y1, proj, proj, cat, d_skip, w_glu, b_glu)


def _glu_bwd(dcat, ys, proj, d_skip, w_glu, b_glu, w):
    l = ys.shape[0]
    tl = _tile(l, 256)

    def body(dy_ref, ys_ref, u_ref, z_ref, d_ref, wg_ref, bg_ref, dys_ref, dp_ref, dbg_ref, dd_ref, dwg_ref):
        i = pl.program_id(0)
        ys_t = ys_ref[...]
        z = z_ref[...]
        dyb = dy_ref[...].astype(F32)
        gy = _gelu(ys_t)
        sg = jax.nn.sigmoid(_mxu_dot(gy, wg_ref[...]) + bg_ref[...])
        dp_ref[...] = (dyb * gy * sg * _silu_grad(z)).astype(dp_ref.dtype)
        dglu = dyb * jax.nn.silu(z)
        ds = dglu * gy * sg * (1.0 - sg)
        dgy = dglu * sg + _mxu_dot(ds, wg_ref[...], _NT)
        dys_t = dgy * _gelu_grad(ys_t)
        dys_ref[...] = dys_t

        @pl.when(i == 0)
        def _():
            for r in (dbg_ref, dd_ref, dwg_ref):
                r[...] = jnp.zeros_like(r)

        dbg_ref[...] += jnp.sum(ds, axis=0, keepdims=True)
        dd_ref[...] += jnp.sum(dys_t * u_ref[...], axis=0, keepdims=True)
        dwg_ref[...] += _mxu_dot(gy, ds, _TN)

    row = pl.BlockSpec((tl, w), lambda i: (i, 0))
    vec = pl.BlockSpec((1, w), lambda i: (0, 0))
    mat = pl.BlockSpec((w, w), lambda i: (0, 0))
    v = jax.ShapeDtypeStruct((1, w), F32)
    return pl.pallas_call(
        body, name="glu_bwd",
        out_shape=(jax.ShapeDtypeStruct((l, w), F32), jax.ShapeDtypeStruct((l, 5 * w), MXU_DTYPE), v, v,
                   jax.ShapeDtypeStruct((w, w), F32)),
        grid=(l // tl,),
        in_specs=[pl.BlockSpec((tl, w), lambda i: (i, 1)), row, pl.BlockSpec((tl, w), lambda i: (i, 3)),
                  pl.BlockSpec((tl, w), lambda i: (i, 4)), vec, mat, vec],
        out_specs=(row, pl.BlockSpec((tl, w), lambda i: (i, 4)), vec, vec, mat),
        compiler_params=_cparams("arbitrary"))(dcat, ys, proj, proj, d_skip, w_glu, b_glu)


def _add2(a, b, name):
    l, w = a.shape
    tl = _tile(l, 512)

    def body(a_ref, b_ref, o_ref):
        o_ref[...] = a_ref[...] + b_ref[...]

    row = pl.BlockSpec((tl, w), lambda i: (i, 0))
    return pl.pallas_call(body, name=name, out_shape=jax.ShapeDtypeStruct((l, w), F32), grid=(l // tl,),
                          in_specs=[row, row], out_specs=row, compiler_params=_cparams("parallel"))(a, b)


def _adamw_nd(w, m, v, g, name):
    shape = w.shape
    lead = math.prod(shape[:-2]) if len(shape) > 2 else 1
    b, c = (shape[-2], shape[-1]) if len(shape) >= 2 else (1, shape[-1])
    t3 = (lead, b, c)
    padded_row = -(-b // 8) * 8 * -(-c // LANES) * LANES * 4
    ta = _tile(lead, max(1, (2 << 20) // padded_row))

    def body(w_ref, m_ref, v_ref, g_ref, d_ref, mo_ref, vo_ref):
        gv = g_ref[...]
        mn = ADAM_B1 * m_ref[...] + (1.0 - ADAM_B1) * gv
        vn = ADAM_B2 * v_ref[...] + (1.0 - ADAM_B2) * (gv * gv)
        m_hat = mn / (1.0 - ADAM_B1 ** ADAM_STEP)
        v_hat = vn / (1.0 - ADAM_B2 ** ADAM_STEP)
        d_ref[...] = -ADAM_LR * (m_hat / (jnp.sqrt(v_hat) + ADAM_EPS) + ADAM_WD * w_ref[...])
        mo_ref[...] = mn
        vo_ref[...] = vn

    blk = pl.BlockSpec((ta, b, c), lambda i: (i, 0, 0))
    s = jax.ShapeDtypeStruct(t3, F32)
    outs = pl.pallas_call(body, name=name, out_shape=(s, s, s), grid=(lead // ta,), in_specs=[blk] * 4, out_specs=(blk,) * 3,
                          compiler_params=_cparams("parallel"))(*[a.reshape(t3) for a in (w, m, v, g)])
    return tuple(o.reshape(shape) for o in outs)


def _adamw(w, m, v, gparts, name, dep=None):
    r, c = w.shape
    np_ = gparts.shape[0]
    tr = _tile(r, max(8, (1 << 18) // c), 8)

    def body(w_ref, m_ref, v_ref, g_ref, *rest):
        go_ref, d_ref, mo_ref, vo_ref = rest[-4:]
        g = g_ref[0].astype(F32)
        for p in range(1, np_):
            g = g + g_ref[p].astype(F32)
        mn = ADAM_B1 * m_ref[...] + (1.0 - ADAM_B1) * g
        vn = ADAM_B2 * v_ref[...] + (1.0 - ADAM_B2) * (g * g)
        m_hat = mn / (1.0 - ADAM_B1 ** ADAM_STEP)
        v_hat = vn / (1.0 - ADAM_B2 ** ADAM_STEP)
        go_ref[...] = g
        d_ref[...] = -ADAM_LR * (m_hat / (jnp.sqrt(v_hat) + ADAM_EPS) + ADAM_WD * w_ref[...])
        mo_ref[...] = mn
        vo_ref[...] = vn

    row = pl.BlockSpec((tr, c), lambda i: (i, 0))
    s = jax.ShapeDtypeStruct((r, c), F32)
    extra = [] if dep is None else [dep]
    return pl.pallas_call(body, name=name, out_shape=(s, s, s, s), grid=(r // tr,),
                          in_specs=[row, row, row, pl.BlockSpec((np_, tr, c), lambda i: (0, i, 0))] + [HBM] * len(extra),
                          out_specs=(row, row, row, row), compiler_params=_cparams("parallel"))(w, m, v, gparts, *extra)


def _sum_slots(parts, name):
    np_, r, c = parts.shape

    def body(p_ref, o_ref):
        g = p_ref[0]
        for p in range(1, np_):
            g = g + p_ref[p]
        o_ref[...] = g

    return pl.pallas_call(body, name=name, out_shape=jax.ShapeDtypeStruct((r, c), F32), compiler_params=_cparams())(parts)


def _block_diag(x, gb):
    nd, g, a, b = x.shape
    eye = jnp.eye(gb, dtype=x.dtype)
    y = jnp.einsum("dkgab,gh->dkgahb", x.reshape(nd, g // gb, gb, a, b), eye)
    return y.reshape(nd, g // gb, gb * a, gb * b)


def _block_diag_extract(y, gb, a, b):
    nd, nbk = y.shape[:2]
    eye = jnp.eye(gb, dtype=y.dtype)
    x = jnp.einsum("dkgahb,gh->dkgab", y.reshape(nd, nbk, gb, a, gb, b), eye)
    return x.reshape(nd, nbk * gb, a, b)


def kernel(x, c, ctx, c_ctx, w_ada, b_ada, w_in, sgu_ln_g, sgu_ln_b, w_spatial, b_spatial, s5_lam_re, s5_lam_im, s5_log_step, s5_b_re, s5_b_im, s5_c_re, s5_c_im, s5_d, w_glu, b_glu, w_out, ln_g, ln_b, loss_target, m_c_ctx, m_w_ada, m_b_ada, m_w_in, m_sgu_ln_g, m_sgu_ln_b, m_w_spatial, m_b_spatial, m_s5_lam_re, m_s5_lam_im, m_s5_log_step, m_s5_b_re, m_s5_b_im, m_s5_c_re, m_s5_c_im, m_s5_d, m_w_glu, m_b_glu, m_w_out, m_ln_g, m_ln_b, v_c_ctx, v_w_ada, v_b_ada, v_w_in, v_sgu_ln_g, v_sgu_ln_b, v_w_spatial, v_b_spatial, v_s5_lam_re, v_s5_lam_im, v_s5_log_step, v_s5_b_re, v_s5_b_im, v_s5_c_re, v_s5_c_im, v_s5_d, v_w_glu, v_b_glu, v_w_out, v_ln_g, v_ln_b):
    small_names = ["c_ctx", "b_ada", "sgu_ln_g", "sgu_ln_b", "w_spatial", "b_spatial", "s5_lam_re", "s5_lam_im",
                   "s5_log_step", "s5_b_re", "s5_b_im", "s5_c_re", "s5_c_im", "s5_d", "b_glu", "ln_g", "ln_b"]
    env = dict(locals())
    x2, tgt, ctx2 = x[0], loss_target[0], ctx[0]
    l, d = x2.shape
    lc = ctx2.shape[0]
    w = d // 2
    nh = w // HEAD_DIM_A
    nd, g_s5, p_s5, c_s5 = s5_b_re.shape[1:]
    ns = g_s5 * p_s5
    nsr = ns // LANES
    gb = LANES // c_s5
    me = _index(_mesh_pos())
    ada_cols = w_ada.shape[2]

    srows = _silu_rows(c, c_ctx)
    srows_all = _all_gather(srows, 0, "gather_silu")
    s_mat = jnp.concatenate([srows_all[0::8], srows_all[1:2], jnp.zeros((7, d), F32)], axis=0)
    mod_part = _small_dot(s_mat, w_ada[0], "nn", "mod_cols")
    mod_all = _all_gather(mod_part, 1, "gather_mod") + b_ada
    hw_in, tok_a = _exchange_start(w_in[0].astype(MXU_DTYPE), 1, "gather", "start_gather_w_in", (SIBLING,) + SAME_CORE_PEERS)
    mod_all = mod_all + tok_a[0, 0]
    mod_x = lax.dynamic_slice_in_dim(mod_all, me, 1, axis=0)
    mod_c = mod_all[8:9]
    shift_x, scale_x, gate_x = mod_x[:, :d], mod_x[:, d:2 * d], mod_x[:, 2 * d:]
    shift_c, scale_c = mod_c[:, :d], mod_c[:, d:2 * d]

    lr, li = s5_lam_re[0][:, :, None, :], s5_lam_im[0][:, :, None, :]
    ls = s5_log_step[0][:, :, None, None]
    swapped = ("s5_b_re", "s5_b_im")
    for nm in swapped:
        for pre in ("", "m_", "v_"):
            env[pre + nm] = jnp.swapaxes(env[pre + nm], -1, -2)
    br_t, bi_t = env["s5_b_re"][0], env["s5_b_im"][0]
    ab_re, ab_im, bb_re, bb_im = _disc_fwd(lr, li, ls, br_t, bi_t)
    a_sm = jnp.stack([ab_re, ab_im], axis=1).reshape(nd, 2, nsr, LANES)
    wbr = _block_diag(bb_re, gb).astype(MXU_DTYPE)
    wbi = _block_diag(bb_im, gb).astype(MXU_DTYPE)
    cre_t = _block_diag(s5_c_re[0], gb).astype(MXU_DTYPE)
    ncim_t = _block_diag(-s5_c_im[0], gb).astype(MXU_DTYPE)
    cre, ncim = jnp.swapaxes(cre_t, 2, 3), jnp.swapaxes(ncim_t, 2, 3)
    wbr_t, wbi_t = jnp.swapaxes(wbr, 2, 3), jnp.swapaxes(wbi, 2, 3)
    d_skip = s5_d

    xm = _ln_mod(x2, shift_x, scale_x, "ln_mod_x")
    cm = _ln_mod(ctx2, shift_c, scale_c, "ln_mod_ctx")
    ready = xm[:8, :LANES].astype(F32) + cm[:8, :LANES].astype(F32) + cre[0, 0, :8, :].astype(F32)
    hw_in2, tok_b = _forward_start(_exchange_wait(hw_in, ready, "wait_gather_w_in"), 1, "start_forward_w_in")
    w_in_f = _forward_wait(hw_in2, tok_b, "wait_forward_w_in")
    hw_glu, tok_c = _exchange_start(w_glu[0].astype(MXU_DTYPE), 0, "gather", "start_gather_w_glu")
    hw_out, tok_o = _exchange_start(w_out[0].astype(MXU_DTYPE), 0, "gather", "start_gather_w_out")
    proj = _matmul(xm, w_in_f, mode="nn", name="proj", bm=1024, bn=1024, bk=d, dep=tok_c + tok_o)
    ub_c = _matmul(cm, w_in_f, mode="nn", name="proj_ctx", bm=256, bn=w, bk=d, b_n0=3 * w, n=w)
    bsf = jnp.repeat(b_spatial[0].T, HEAD_DIM_A, axis=1)
    ws = w_spatial[0]
    cat = _ga_fwd(proj, sgu_ln_g, sgu_ln_b, ws, bsf, w)
    zeros_state = jnp.zeros((2, nsr, LANES), F32)
    s5c, s5l = [], []
    for dr in range(nd):
        s5c.append(_s5_fwd(ub_c, 0, w, zeros_state, a_sm, wbr, wbi, cre, ncim, dr, f"s5_fwd_ctx{dr}"))
        s5l.append(_s5_fwd(proj, 3, w, s5c[dr][5], a_sm, wbr, wbi, cre, ncim, dr, f"s5_fwd{dr}"))
    w_glu_f = _exchange_wait(hw_glu, s5l[1][0], "wait_gather_w_glu")
    ys, cat = _glu_fwd(s5l[0][0], s5l[1][0], proj, cat, d_skip, w_glu_f, b_glu, w)
    w_out_f = _exchange_wait(hw_out, ys, "wait_gather_w_out")
    out = _matmul(cat, w_out_f, mode="nn", name="out_proj", bm=1024, bn=1024, bk=2 * w)
    loss_row, dout, dx_res, dgate, dln_g, dln_b = _post_ln_loss(x2, out, gate_x, ln_g, ln_b, tgt)

    dcat = _matmul(dout, w_out_f, mode="nt", name="d_cat", bm=1024, bn=1024, bk=d, out_dtype=MXU_DTYPE)
    dw_out = _matmul(cat, dout, mode="tn", name="d_w_out", bm=1024, bn=1024, bk=2048, out_dtype=MXU_DTYPE)
    hg_out, tok_d = _exchange_start(dw_out, 0, "a2a", "start_a2a_d_w_out")
    dys, dproj, db_glu, dd_skip, dw_glu = _glu_bwd(dcat, ys, proj, d_skip + tok_d[0, 0], w_glu_f, b_glu, w)
    hg_glu, tok_e = _exchange_start(dw_glu.astype(MXU_DTYPE), 0, "a2a", "start_a2a_d_w_glu")
    zeros_state = zeros_state + tok_e[0, 0]
    du_l, du_c, dwbr, dwbi, dcre, dncim, da_sm = [], [], [], [], [], [], []
    nbk, spb = w // LANES, gb * p_s5
    for dr in range(nd):
        bl = _s5_bwd(dys, proj, 3, w, *s5l[dr][1:5], s5c[dr][5], zeros_state, a_sm, wbr_t, wbi_t,
                     cre_t, ncim_t, dr, f"s5_bwd{dr}")
        bc = _s5_bwd(None, ub_c, 0, w, s5c[dr][1], s5c[dr][2], None, None, zeros_state, bl[6], a_sm, wbr_t, wbi_t,
                     None, None, dr, f"s5_bwd_ctx{dr}")
        du_l.append(bl[0])
        du_c.append(bc[0])
        dwbr.append(_add2(bl[1].reshape(nbk * LANES, spb), bc[1].reshape(nbk * LANES, spb), f"sum_dwbr{dr}"))
        dwbi.append(_add2(bl[2].reshape(nbk * LANES, spb), bc[2].reshape(nbk * LANES, spb), f"sum_dwbi{dr}"))
        dcre.append(bl[3])
        dncim.append(bl[4])
        da_sm.append(_add2(bl[5].reshape(2 * nsr, LANES), bc[3].reshape(2 * nsr, LANES), f"sum_da{dr}"))
    dub_c = _add2(du_c[0], du_c[1], "dub_ctx")
    dwbr = jnp.stack(dwbr).reshape(nd, nbk, LANES, spb)
    dwbi = jnp.stack(dwbi).reshape(nd, nbk, LANES, spb)
    dcre, dncim = jnp.stack(dcre), jnp.stack(dncim)
    da_sm = jnp.stack(da_sm).reshape(nd, 2, g_s5, p_s5)
    dproj, dsg, dsb, dws, dbsf = _ga_bwd(proj, dcat, dproj, sgu_ln_g, sgu_ln_b, ws, bsf, dys, du_l[0], du_l[1], d_skip, w)

    dbb_re = _block_diag_extract(dwbr, gb, c_s5, p_s5)
    dbb_im = _block_diag_extract(dwbi, gb, c_s5, p_s5)
    dc_re = jnp.swapaxes(_block_diag_extract(dcre, gb, p_s5, c_s5), 2, 3)
    dc_im = -jnp.swapaxes(_block_diag_extract(dncim, gb, p_s5, c_s5), 2, 3)
    dlr, dli, dls, db_re, db_im = _disc_bwd(lr, li, ls, br_t, bi_t, da_sm[:, 0:1].reshape(nd, g_s5, 1, p_s5),
                                            da_sm[:, 1:2].reshape(nd, g_s5, 1, p_s5), dbb_re, dbb_im)
    expand = (jnp.arange(w)[:, None] // HEAD_DIM_A == jnp.arange(LANES)[None, :]).astype(F32)
    db_sp = _lane_group_sum(dbsf, expand, "d_b_spatial")[:, :nh].T

    local = {"sgu_ln_g": dsg, "sgu_ln_b": dsb, "w_spatial": dws, "b_spatial": db_sp,
             "s5_lam_re": dlr, "s5_lam_im": dli, "s5_log_step": dls, "s5_b_re": db_re, "s5_b_im": db_im,
             "s5_c_re": dc_re, "s5_c_im": dc_im, "s5_d": dd_skip, "b_glu": db_glu, "ln_g": dln_g, "ln_b": dln_b}
    reduced = [n for n in small_names if n in local]
    loss_part = (0.5 / d) * jnp.sum(loss_row)
    flat = jnp.concatenate([loss_part.reshape(1)] + [local[n].reshape(-1) for n in reduced])
    unit = N_DEV * 8 * LANES
    total = -(-flat.shape[0] // unit) * unit
    flat = jnp.pad(flat, (0, total - flat.shape[0])).reshape(N_DEV * 8, total // (N_DEV * 8))
    h_small, tok_s = _exchange_start(flat, 0, "a2a", "start_a2a_small")

    dw_in = _matmul(xm, dproj, mode="tn", name="d_w_in", bm=1024, bn=1280, bk=2048, out_dtype=MXU_DTYPE, dep=tok_s)
    dw_in = _matmul(cm, dub_c, mode="tn", name="d_w_in_ctx", bm=1024, bn=w, bk=lc, acc_in=dw_in, acc_n0=3 * w,
                    out_dtype=MXU_DTYPE)
    hg_in, tok_f = _exchange_start(dw_in, 1, "a2a", "start_a2a_d_w_in")
    mine = _sum_slots(_exchange_wait(h_small, dw_in, "wait_a2a_small"), "sum_small")
    h_sums, tok_g = _exchange_start(mine, 0, "gather", "start_gather_small")
    dxm = _matmul(dproj, w_in_f, mode="nt", name="d_xm", bm=512, bn=1024, bk=5 * w, dep=tok_f + tok_g,
                  out_dtype=MXU_DTYPE)
    dcm = _matmul(dub_c, w_in_f, mode="nt", name="d_cm", bm=256, bn=1024, bk=w, b_k0=3 * w, k=w)
    grad_x, dshift_x, dscale_x = _ln_mod_bwd(x2, dxm, scale_x, dx_res, "ln_mod_x_bwd")
    _, dshift_c, dscale_c = _ln_mod_bwd(ctx2, dcm, scale_c, None, "ln_mod_ctx_bwd")

    dmod_rows = jnp.concatenate([jnp.concatenate([dshift_x, dscale_x, dgate], axis=1),
                                 jnp.concatenate([dshift_c, dscale_c, jnp.zeros((1, d), F32)], axis=1),
                                 jnp.zeros((6, 3 * d), F32)], axis=0)
    h_dmod, tok_m = _exchange_start(dmod_rows, 0, "gather", "start_gather_dmod")

    gp_w_out = _exchange_wait(hg_out, tok_m, "wait_a2a_d_w_out")
    gp_w_glu = _exchange_wait(hg_glu, tok_m, "wait_a2a_d_w_glu")
    gp_w_in = _exchange_wait(hg_in, tok_m, "wait_a2a_d_w_in")
    big = {
        "w_in": _adamw(w_in[0], m_w_in[0], v_w_in[0], gp_w_in, "adamw_w_in"),
        "w_glu": _adamw(w_glu[0], m_w_glu[0], v_w_glu[0], gp_w_glu, "adamw_w_glu"),
        "w_out": _adamw(w_out[0], m_w_out[0], v_w_out[0], gp_w_out, "adamw_w_out"),
    }
    dmod_all = _exchange_wait(h_dmod, big["w_out"][0], "wait_gather_dmod")
    dmod_ctx = _sum_slots(dmod_all[1::8].reshape(N_DEV, 1, 3 * d), "sum_dmod_ctx")
    dmod_mat = jnp.concatenate([dmod_all[0::8], dmod_ctx, jnp.zeros((7, 3 * d), F32)], axis=0)
    db_ada = _sum_slots(dmod_mat[:9].reshape(9, 1, 3 * d), "sum_db_ada")
    dmod_mine = lax.dynamic_slice_in_dim(dmod_mat, me * ada_cols, ada_cols, axis=1)
    dw_ada = _small_dot(s_mat, dmod_mine, "tn", "d_w_ada")
    dsilu_cc = _small_dot(dmod_mine[8:16], w_ada[0], "nt", "d_silu_cctx")[0:1]
    dc_ctx_part = dsilu_cc * _silu_grad(c_ctx.reshape(1, d))
    dc_ctx_rows = jnp.concatenate([dc_ctx_part, jnp.zeros((7, d), F32)], axis=0)
    h_cctx, tok_c2 = _exchange_start(dc_ctx_rows, 0, "gather", "start_gather_d_c_ctx")

    big["w_ada"] = _adamw(w_ada[0], m_w_ada[0], v_w_ada[0], dw_ada[None], "adamw_w_ada", dep=tok_c2)
    summed = _exchange_wait(h_sums, big["w_ada"][0], "wait_gather_small").reshape(-1)
    loss = summed[0]
    grads, off = {"b_ada": db_ada}, 1
    for n in reduced:
        size = math.prod(env[n].shape)
        grads[n] = summed[off:off + size].reshape(env[n].shape)
        off += size
    res = {n: tuple(a[None] for a in big[n]) for n in big}

    def small_step(n):
        res[n] = (grads[n],) + _adamw_nd(env[n], env["m_" + n], env["v_" + n], grads[n], "adamw_" + n)
        if n in swapped:
            res[n] = tuple(jnp.swapaxes(a, -1, -2) for a in res[n])

    for n in small_names:
        if n != "c_ctx":
            small_step(n)
    dc_ctx_all = _exchange_wait(h_cctx, res["w_spatial"][1], "wait_gather_d_c_ctx")
    grads["c_ctx"] = _sum_slots(dc_ctx_all[0::8].reshape(N_DEV, 1, d), "sum_d_c_ctx").reshape(d)
    small_step("c_ctx")

    order = ["c_ctx", "w_ada", "b_ada", "w_in", "sgu_ln_g", "sgu_ln_b", "w_spatial", "b_spatial", "s5_lam_re", "s5_lam_im",
             "s5_log_step", "s5_b_re", "s5_b_im", "s5_c_re", "s5_c_im", "s5_d", "w_glu", "b_glu", "w_out", "ln_g", "ln_b"]
    return (loss, grad_x[None], *[res[n][0] for n in order], *[res[n][1] for n in order],
            *[res[n][2] for n in order], *[res[n][3] for n in order])
```

```python
import functools
import math

import jax
import jax.numpy as jnp
from jax import lax
from jax.experimental import pallas as pl
from jax.experimental.pallas import tpu as pltpu

F32 = jnp.float32
MXU_DTYPE = jnp.bfloat16
N_DEV = 8
MESH_ID = pl.DeviceIdType.MESH
LN_EPS = 1e-6
DEPTH = 1
ALPHA = (2.0 * DEPTH) ** 0.25
CHUNK = 128
HEAD_DIM_A = 128
ADAM_LR, ADAM_B1, ADAM_B2, ADAM_EPS, ADAM_WD, ADAM_STEP = 0.001, 0.9, 0.999, 1e-08, 0.01, 10
LANES = 128
SCAN_UNROLL = 8
VMEM_LIMIT = 56 * 1024 * 1024
HBM = pl.BlockSpec(memory_space=pl.ANY)


def _cparams(*sem):
    return pltpu.CompilerParams(dimension_semantics=sem if sem else None, vmem_limit_bytes=VMEM_LIMIT)


def _tile(n, pref, mult=1):
    if n <= pref:
        return n
    t = pref - pref % mult
    while n % t:
        t -= mult
    return t


def _gelu(x):
    return 0.5 * x * (1.0 + lax.erf(x * (1.0 / math.sqrt(2.0))))


def _gelu_grad(x):
    return 0.5 * (1.0 + lax.erf(x * (1.0 / math.sqrt(2.0)))) + x * jnp.exp(-0.5 * x * x) * (1.0 / math.sqrt(2.0 * math.pi))


def _silu_grad(x):
    s = jax.nn.sigmoid(x)
    return s * (1.0 + x * (1.0 - s))


def _mxu_dot(a, b, dims=(((1,), (0,)), ((), ()))):
    return lax.dot_general(a.astype(MXU_DTYPE), b.astype(MXU_DTYPE), dims, preferred_element_type=F32)


_NT = (((1,), (1,)), ((), ()))
_TN = (((0,), (0,)), ((), ()))


def _mesh_pos():
    return lax.axis_index("x"), lax.axis_index("y"), lax.axis_index("c")


def _peer(pos, r):
    x, y, c = pos
    return ((1 - x) if r & 4 else x, (1 - y) if r & 2 else y, (1 - c) if r & 1 else c)


def _index(pos):
    return 4 * pos[0] + 2 * pos[1] + pos[2]


def _slice_of(ref, axis, idx, size):
    start = idx * size
    if axis == 0:
        return ref.at[pl.ds(start, size)]
    return ref.at[:, pl.ds(start, size)]


def _all_gather(x, axis, name):
    size = x.shape[axis]
    out_shape = tuple(s * N_DEV if a == axis else s for a, s in enumerate(x.shape))

    def body(x_ref, o_ref, send_sems, recv_sems, local_sem):
        me = _mesh_pos()
        mine = pltpu.make_async_copy(x_ref, _slice_of(o_ref, axis, _index(me), size), local_sem)
        mine.start()

        def copy(r, block):
            return pltpu.make_async_remote_copy(
                src_ref=x_ref, dst_ref=_slice_of(o_ref, axis, _index(block), size),
                send_sem=send_sems.at[r - 1], recv_sem=recv_sems.at[r - 1],
                device_id=_peer(me, r), device_id_type=MESH_ID)

        sends = [copy(r, me) for r in range(1, N_DEV)]
        for cp in sends:
            cp.start()
        for r in range(1, N_DEV):
            copy(r, _peer(me, r)).wait_recv()
        for cp in sends:
            cp.wait_send()
        mine.wait()

    return pl.pallas_call(
        body, name=name, out_shape=jax.ShapeDtypeStruct(out_shape, x.dtype),
        in_specs=[HBM], out_specs=HBM,
        scratch_shapes=[pltpu.SemaphoreType.DMA((N_DEV - 1,)), pltpu.SemaphoreType.DMA((N_DEV - 1,)),
                        pltpu.SemaphoreType.DMA],
    )(x)


def _all_to_all(x, axis, name):
    size = x.shape[axis] // N_DEV
    slot = tuple(size if a == axis else s for a, s in enumerate(x.shape))

    def body(x_ref, o_ref, send_sems, recv_sems, local_sem):
        me = _mesh_pos()
        mine = pltpu.make_async_copy(_slice_of(x_ref, axis, _index(me), size), o_ref.at[_index(me)], local_sem)
        mine.start()

        def copy(r, sender, receiver):
            return pltpu.make_async_remote_copy(
                src_ref=_slice_of(x_ref, axis, _index(receiver), size), dst_ref=o_ref.at[_index(sender)],
                send_sem=send_sems.at[r - 1], recv_sem=recv_sems.at[r - 1],
                device_id=_peer(me, r), device_id_type=MESH_ID)

        sends = [copy(r, me, _peer(me, r)) for r in range(1, N_DEV)]
        for cp in sends:
            cp.start()
        for r in range(1, N_DEV):
            copy(r, _peer(me, r), me).wait_recv()
        for cp in sends:
            cp.wait_send()
        mine.wait()

    return pl.pallas_call(
        body, name=name, out_shape=jax.ShapeDtypeStruct((N_DEV,) + slot, x.dtype),
        in_specs=[HBM], out_specs=HBM,
        scratch_shapes=[pltpu.SemaphoreType.DMA((N_DEV - 1,)), pltpu.SemaphoreType.DMA((N_DEV - 1,)),
                        pltpu.SemaphoreType.DMA],
    )(x)


_SEM = pl.BlockSpec(memory_space=pltpu.SEMAPHORE)
_HBM = pl.BlockSpec(memory_space=pltpu.HBM)
_EFFECT = pltpu.SideEffectType.DATAFLOW_SIDE_EFFECTING
ALL_PEERS = tuple(range(1, N_DEV))
SIBLING = 1
SAME_CORE_PEERS = (2, 4, 6)


def _exchange_copy(kind, x_ref, land_ref, axis, size, send_sems, recv_sems, me, rels, q, arriving):
    peer = _peer(me, rels[q])
    sender, receiver = (peer, me) if arriving else (me, peer)
    if kind == "gather":
        src, dst = x_ref, _slice_of(land_ref, axis, _index(sender), size)
    else:
        src, dst = _slice_of(x_ref, axis, _index(receiver), size), land_ref.at[_index(sender)]
    return pltpu.make_async_remote_copy(src_ref=src, dst_ref=dst, send_sem=send_sems.at[q], recv_sem=recv_sems.at[q],
                                        device_id=peer, device_id_type=MESH_ID)


def _local_copy(kind, x_ref, land_ref, axis, size, me, local_sem):
    if kind == "gather":
        return pltpu.make_async_copy(x_ref, _slice_of(land_ref, axis, _index(me), size), local_sem)
    return pltpu.make_async_copy(_slice_of(x_ref, axis, _index(me), size), land_ref.at[_index(me)], local_sem)


def _exchange_start(x, axis, kind, name, rels=ALL_PEERS):
    size = x.shape[axis] if kind == "gather" else x.shape[axis] // N_DEV
    if kind == "gather":
        land_shape = tuple(s * N_DEV if a == axis else s for a, s in enumerate(x.shape))
    else:
        land_shape = (N_DEV,) + tuple(size if a == axis else s for a, s in enumerate(x.shape))

    def body(x_ref, land_ref, send_sems, recv_sems, local_sem, x_thru, land_thru, token):
        del x_thru, land_thru
        me = _mesh_pos()
        _local_copy(kind, x_ref, land_ref, axis, size, me, local_sem).start()
        for q in range(len(rels)):
            _exchange_copy(kind, x_ref, land_ref, axis, size, send_sems, recv_sems, me, rels, q, False).start()
        token[...] = jnp.zeros_like(token)

    sems = pltpu.SemaphoreType.DMA((len(rels),))
    send_sems, recv_sems, local_sem, x_thru, land_thru, token = pl.pallas_call(
        body, name=name,
        out_shape=(sems, sems, pltpu.SemaphoreType.DMA(()), pltpu.HBM(x.shape, x.dtype), pltpu.HBM(land_shape, x.dtype),
                   jax.ShapeDtypeStruct((8, LANES), F32)),
        in_specs=(_HBM, _HBM), out_specs=(_SEM, _SEM, _SEM, _HBM, _HBM, pl.BlockSpec(memory_space=pltpu.VMEM)),
        input_output_aliases={0: 3, 1: 4}, compiler_params=pltpu.CompilerParams(has_side_effects=_EFFECT),
    )(pltpu.with_memory_space_constraint(x, pltpu.HBM),
      pltpu.with_memory_space_constraint(lax.empty(land_shape, x.dtype), pltpu.HBM))
    return (kind, axis, size, rels, send_sems, recv_sems, local_sem, x_thru, land_thru), token


def _exchange_wait(handle, after, name):
    kind, axis, size, rels, send_sems, recv_sems, local_sem, x_thru, land_thru = handle

    def body(x_ref, land_ref, send_sems, recv_sems, local_sem, after_ref, x_dead, got_ref):
        del after_ref, x_dead, got_ref
        me = _mesh_pos()
        _local_copy(kind, x_ref, land_ref, axis, size, me, local_sem).wait()
        for q in range(len(rels)):
            _exchange_copy(kind, x_ref, land_ref, axis, size, send_sems, recv_sems, me, rels, q, False).wait_send()
        for q in range(len(rels)):
            _exchange_copy(kind, x_ref, land_ref, axis, size, send_sems, recv_sems, me, rels, q, True).wait_recv()

    return pl.pallas_call(
        body, name=name, out_shape=(pltpu.HBM(x_thru.shape, x_thru.dtype), pltpu.HBM(land_thru.shape, land_thru.dtype)),
        in_specs=(_HBM, _HBM, _SEM, _SEM, _SEM, HBM), out_specs=(_HBM, _HBM), input_output_aliases={0: 0, 1: 1},
        compiler_params=pltpu.CompilerParams(has_side_effects=_EFFECT),
    )(x_thru, land_thru, send_sems, recv_sems, local_sem, after)[1]


def _forward_copy(land_ref, axis, size, send_sems, recv_sems, me, q, arriving):
    sibling = _peer(me, SIBLING)
    owner = _peer(sibling if arriving else me, SAME_CORE_PEERS[q])
    block = _slice_of(land_ref, axis, _index(owner), size)
    return pltpu.make_async_remote_copy(src_ref=block, dst_ref=block, send_sem=send_sems.at[q], recv_sem=recv_sems.at[q],
                                        device_id=sibling, device_id_type=MESH_ID)


def _forward_start(land, axis, name):
    size = land.shape[axis] // N_DEV

    def body(land_ref, send_sems, recv_sems, land_thru, token):
        del land_thru
        me = _mesh_pos()
        for q in range(len(SAME_CORE_PEERS)):
            _forward_copy(land_ref, axis, size, send_sems, recv_sems, me, q, False).start()
        token[...] = jnp.zeros_like(token)

    sems = pltpu.SemaphoreType.DMA((len(SAME_CORE_PEERS),))
    send_sems, recv_sems, land_thru, token = pl.pallas_call(
        body, name=name, out_shape=(sems, sems, pltpu.HBM(land.shape, land.dtype), jax.ShapeDtypeStruct((8, LANES), F32)),
        in_specs=(_HBM,), out_specs=(_SEM, _SEM, _HBM, pl.BlockSpec(memory_space=pltpu.VMEM)),
        input_output_aliases={0: 2}, compiler_params=pltpu.CompilerParams(has_side_effects=_EFFECT),
    )(land)
    return (axis, size, send_sems, recv_sems, land_thru), token


def _forward_wait(handle, after, name):
    axis, size, send_sems, recv_sems, land_thru = handle

    def body(land_ref, send_sems, recv_sems, after_ref, got_ref):
        del after_ref, got_ref
        me = _mesh_pos()
        for q in range(len(SAME_CORE_PEERS)):
            _forward_copy(land_ref, axis, size, send_sems, recv_sems, me, q, False).wait_send()
        for q in range(len(SAME_CORE_PEERS)):
            _forward_copy(land_ref, axis, size, send_sems, recv_sems, me, q, True).wait_recv()

    return pl.pallas_call(
        body, name=name, out_shape=pltpu.HBM(land_thru.shape, land_thru.dtype),
        in_specs=(_HBM, _SEM, _SEM, HBM), out_specs=_HBM, input_output_aliases={0: 0},
        compiler_params=pltpu.CompilerParams(has_side_effects=_EFFECT),
    )(land_thru, send_sems, recv_sems, after)


def _matmul(a, b, *, mode, name, bm=512, bn=512, bk=512, out_dtype=F32, b_n0=0, n=None, b_k0=0, k=None,
            acc_in=None, acc_n0=0, dep=None):
    if mode == "tn":
        kk, m = a.shape
    else:
        m, kk = a.shape
    if mode == "nn":
        n = b.shape[1] if n is None else n
    elif mode == "nt":
        n = b.shape[0]
        kk = kk if k is None else k
    else:
        n = b.shape[1]
    bm, bn, bk = _tile(m, bm), _tile(n, bn), _tile(kk, bk)
    nk = kk // bk
    assert b_n0 % bn == 0 and b_k0 % bk == 0 and acc_n0 % bn == 0
    dims = {"nn": (((1,), (0,)), ((), ())), "nt": _NT, "tn": _TN}[mode]

    n_in = 2 + (acc_in is not None) + (dep is not None)

    def body(*refs):
        a_ref, b_ref = refs[:2]
        init = refs[2] if acc_in is not None else None
        o_ref = refs[n_in]
        acc_ref = refs[-1] if nk > 1 else None
        p = _mxu_dot(a_ref[...], b_ref[...], dims)
        if nk == 1:
            o_ref[...] = (p if init is None else p + init[...]).astype(out_dtype)
            return
        ki = pl.program_id(2)

        @pl.when(ki == 0)
        def _():
            acc_ref[...] = p if init is None else p + init[...]

        @pl.when(ki > 0)
        def _():
            acc_ref[...] += p

        @pl.when(ki == nk - 1)
        def _():
            o_ref[...] = acc_ref[...].astype(out_dtype)

    a_spec = pl.BlockSpec((bk, bm), lambda j, i, q: (q, i)) if mode == "tn" else pl.BlockSpec((bm, bk), lambda j, i, q: (i, q))
    if mode == "nt":
        b_spec = pl.BlockSpec((bn, bk), lambda j, i, q: (j, q + b_k0 // bk))
    else:
        b_spec = pl.BlockSpec((bk, bn), lambda j, i, q: (q, j + b_n0 // bn))
    in_specs, args, aliases = [a_spec, b_spec], [a, b], {}
    out_map = lambda j, i, q: (i, j + acc_n0 // bn)
    if acc_in is not None:
        in_specs.append(pl.BlockSpec((bm, bn), out_map))
        args.append(acc_in)
        aliases = {2: 0}
        out_shape = jax.ShapeDtypeStruct(acc_in.shape, out_dtype)
    else:
        out_shape = jax.ShapeDtypeStruct((m, n), out_dtype)
    if dep is not None:
        in_specs.append(HBM)
        args.append(dep)
    return pl.pallas_call(
        body, name=name, out_shape=out_shape, grid=(n // bn, m // bm, nk),
        in_specs=in_specs, out_specs=pl.BlockSpec((bm, bn), out_map),
        scratch_shapes=[pltpu.VMEM((bm, bn), F32)] if nk > 1 else [],
        input_output_aliases=aliases,
        compiler_params=_cparams("parallel", "parallel", "arbitrary"),
    )(*args)


def _silu_rows(c, c_ctx):
    d = c.shape[-1]

    def body(c_ref, cc_ref, o_ref):
        o_ref[...] = jnp.zeros_like(o_ref)
        o_ref[0:1, :] = jax.nn.silu(c_ref[...])
        o_ref[1:2, :] = jax.nn.silu(cc_ref[...])

    return pl.pallas_call(body, name="silu_rows", out_shape=jax.ShapeDtypeStruct((8, d), F32))(
        c.reshape(1, d), c_ctx.reshape(1, d))


def _small_dot(a, b, mode, name):
    dims = {"nn": (((1,), (0,)), ((), ())), "nt": _NT, "tn": _TN}[mode]
    m = a.shape[1] if mode == "tn" else a.shape[0]
    n = b.shape[0] if mode == "nt" else b.shape[1]

    def body(a_ref, b_ref, o_ref):
        o_ref[...] = lax.dot_general(a_ref[...], b_ref[...], dims, preferred_element_type=F32,
                                     precision=lax.Precision.HIGHEST)

    return pl.pallas_call(body, name=name, out_shape=jax.ShapeDtypeStruct((m, n), F32),
                          compiler_params=_cparams())(a, b)


def _ln_stats(x):
    mu = jnp.mean(x, axis=-1, keepdims=True)
    xc = x - mu
    var = jnp.mean(xc * xc, axis=-1, keepdims=True)
    rstd = lax.rsqrt(var + LN_EPS)
    return xc * rstd, rstd


def _ln_mod(x, shift, scale, name):
    l, d = x.shape
    tl = _tile(l, 512)

    def body(x_ref, sh_ref, sc_ref, o_ref):
        xhat, _ = _ln_stats(x_ref[...])
        o_ref[...] = (xhat * (1.0 + sc_ref[...]) + sh_ref[...]).astype(o_ref.dtype)

    row = pl.BlockSpec((tl, d), lambda i: (i, 0))
    vec = pl.BlockSpec((1, d), lambda i: (0, 0))
    return pl.pallas_call(body, name=name, out_shape=jax.ShapeDtypeStruct((l, d), MXU_DTYPE), grid=(l // tl,),
                          in_specs=[row, vec, vec], out_specs=row, compiler_params=_cparams("parallel"))(x, shift, scale)


def _ln_mod_bwd(x, dxm, scale, res, name):
    l, d = x.shape
    tl = _tile(l, 512)
    with_res = res is not None

    def body(*refs):
        if with_res:
            x_ref, g_ref, sc_ref, r_ref, dx_ref, dsh_ref, dsc_ref = refs
        else:
            x_ref, g_ref, sc_ref, dx_ref, dsh_ref, dsc_ref = refs
        i = pl.program_id(0)
        xhat, rstd = _ln_stats(x_ref[...])
        g = g_ref[...].astype(F32)
        dxh = g * (1.0 + sc_ref[...])
        dx = rstd * (dxh - jnp.mean(dxh, axis=-1, keepdims=True) - xhat * jnp.mean(dxh * xhat, axis=-1, keepdims=True))
        dx_ref[...] = dx + r_ref[...].astype(F32) if with_res else dx

        @pl.when(i == 0)
        def _():
            dsh_ref[...] = jnp.zeros_like(dsh_ref)
            dsc_ref[...] = jnp.zeros_like(dsc_ref)

        dsh_ref[...] += jnp.sum(g, axis=0, keepdims=True)
        dsc_ref[...] += jnp.sum(g * xhat, axis=0, keepdims=True)

    row = pl.BlockSpec((tl, d), lambda i: (i, 0))
    vec = pl.BlockSpec((1, d), lambda i: (0, 0))
    args = [x, dxm, scale] + ([res] if with_res else [])
    return pl.pallas_call(
        body, name=name,
        out_shape=(jax.ShapeDtypeStruct((l, d), F32), jax.ShapeDtypeStruct((1, d), F32), jax.ShapeDtypeStruct((1, d), F32)),
        grid=(l // tl,), in_specs=[row, row, vec] + ([row] if with_res else []), out_specs=(row, vec, vec),
        compiler_params=_cparams("arbitrary"))(*args)


def _post_ln_loss(x, out, gate, ln_g, ln_b, target):
    l, d = x.shape
    tl = _tile(l, 512)

    def body(x_ref, o_ref, gate_ref, g_ref, b_ref, t_ref, loss_ref, dout_ref, dxr_ref, dgate_ref, dg_ref, db_ref):
        i = pl.program_id(0)
        out_t = o_ref[...]
        gate_v = gate_ref[...]
        rhat, rstd = _ln_stats(ALPHA * x_ref[...] + gate_v * out_t)
        ln_gv = g_ref[...]
        diff = rhat * ln_gv + b_ref[...] - t_ref[...]
        dy = diff * (1.0 / d)
        drh = dy * ln_gv
        dr = rstd * (drh - jnp.mean(drh, axis=-1, keepdims=True) - rhat * jnp.mean(drh * rhat, axis=-1, keepdims=True))
        dout_ref[...] = (gate_v * dr).astype(dout_ref.dtype)
        dxr_ref[...] = (ALPHA * dr).astype(dxr_ref.dtype)

        @pl.when(i == 0)
        def _():
            for r in (loss_ref, dgate_ref, dg_ref, db_ref):
                r[...] = jnp.zeros_like(r)

        loss_ref[...] += jnp.sum(diff * diff, axis=0, keepdims=True)
        dgate_ref[...] += jnp.sum(dr * out_t, axis=0, keepdims=True)
        dg_ref[...] += jnp.sum(dy * rhat, axis=0, keepdims=True)
        db_ref[...] += jnp.sum(dy, axis=0, keepdims=True)

    row = pl.BlockSpec((tl, d), lambda i: (i, 0))
    vec = pl.BlockSpec((1, d), lambda i: (0, 0))
    v = jax.ShapeDtypeStruct((1, d), F32)
    return pl.pallas_call(
        body, name="post_ln_loss",
        out_shape=(v, jax.ShapeDtypeStruct((l, d), MXU_DTYPE), jax.ShapeDtypeStruct((l, d), MXU_DTYPE), v, v, v),
        grid=(l // tl,), in_specs=[row, row, vec, vec, vec, row], out_specs=(vec, row, row, vec, vec, vec),
        compiler_params=_cparams("arbitrary"))(x, out, gate, ln_g, ln_b, target)


def _ga_forward_tile(p, g, b, ws_ref, bsf, w, nc, nh):
    u_raw, v_raw, za = p[:, :w], p[:, w:2 * w], p[:, 2 * w:3 * w]
    gu = _gelu(u_raw)
    vhat, rstd = _ln_stats(_gelu(v_raw))
    vn = vhat * g + b
    rows = []
    for ci in range(nc):
        r0 = ci * CHUNK
        heads = [_mxu_dot(ws_ref[h], vn[r0:r0 + CHUNK, h * HEAD_DIM_A:(h + 1) * HEAD_DIM_A]) for h in range(nh)]
        rows.append(jnp.concatenate(heads, axis=1) + bsf)
    mixed = jnp.concatenate(rows, axis=0) if nc > 1 else rows[0]
    return u_raw, v_raw, za, gu, vhat, rstd, vn, mixed


def _ga_fwd(proj, g, b, ws, bsf, w):
    l = proj.shape[0]
    nh = w // HEAD_DIM_A
    nc = _tile(l // CHUNK, 2)
    tl = nc * CHUNK

    def body(p_ref, g_ref, b_ref, ws_ref, bsf_ref, o_ref):
        _, _, za, gu, _, _, _, mixed = _ga_forward_tile(p_ref[...], g_ref[...], b_ref[...], ws_ref, bsf_ref[...], w, nc, nh)
        o_ref[...] = (gu * mixed * jax.nn.silu(za)).astype(o_ref.dtype)

    vec = pl.BlockSpec((1, w), lambda i: (0, 0))
    return pl.pallas_call(
        body, name="ga_fwd", out_shape=jax.ShapeDtypeStruct((l, 2 * w), MXU_DTYPE), grid=(l // tl,),
        in_specs=[pl.BlockSpec((tl, 3 * w), lambda i: (i, 0)), vec, vec,
                  pl.BlockSpec((nh, CHUNK, CHUNK), lambda i: (0, 0, 0)), pl.BlockSpec((CHUNK, w), lambda i: (0, 0))],
        out_specs=pl.BlockSpec((tl, w), lambda i: (i, 0)), compiler_params=_cparams("parallel"))(proj, g, b, ws, bsf)


def _ga_bwd(proj, dcat, dproj, g, b, ws, bsf, dys, du0, du1, d_skip, w):
    l = proj.shape[0]
    nh = w // HEAD_DIM_A
    nc = _tile(l // CHUNK, 2)
    tl = nc * CHUNK

    def body(p_ref, dy_ref, dp_in, g_ref, b_ref, ws_ref, bsf_ref, dys_ref, du0_ref, du1_ref, d_ref,
             dp_ref, dg_ref, db_ref, dws_ref, dbsf_ref):
        del dp_in
        i = pl.program_id(0)
        dp_ref[:, 3 * w:] = (dys_ref[...] * d_ref[...] + du0_ref[...] + du1_ref[...]).astype(dp_ref.dtype)
        gv = g_ref[...]
        u_raw, v_raw, za, gu, vhat, rstd, vn, mixed = _ga_forward_tile(
            p_ref[...], gv, b_ref[...], ws_ref, bsf_ref[...], w, nc, nh)
        dya = dy_ref[...].astype(F32)
        sz = jax.nn.silu(za)
        dmixed = dya * gu * sz
        dza = dya * gu * mixed * _silu_grad(za)
        dgu = dya * mixed * sz

        @pl.when(i == 0)
        def _():
            for r in (dg_ref, db_ref, dws_ref, dbsf_ref):
                r[...] = jnp.zeros_like(r)

        rows = []
        for ci in range(nc):
            r0 = ci * CHUNK
            heads = []
            for h in range(nh):
                cols = slice(h * HEAD_DIM_A, (h + 1) * HEAD_DIM_A)
                dm = dmixed[r0:r0 + CHUNK, cols]
                heads.append(_mxu_dot(ws_ref[h], dm, _TN))
                dws_ref[h] += _mxu_dot(dm, vn[r0:r0 + CHUNK, cols], _NT)
            rows.append(jnp.concatenate(heads, axis=1))
            dbsf_ref[...] += dmixed[r0:r0 + CHUNK, :]
        dvn = jnp.concatenate(rows, axis=0) if nc > 1 else rows[0]
        dg_ref[...] += jnp.sum(dvn * vhat, axis=0, keepdims=True)
        db_ref[...] += jnp.sum(dvn, axis=0, keepdims=True)
        dvh = dvn * gv
        dgv = rstd * (dvh - jnp.mean(dvh, axis=-1, keepdims=True) - vhat * jnp.mean(dvh * vhat, axis=-1, keepdims=True))
        dp_ref[:, :w] = (dgu * _gelu_grad(u_raw)).astype(dp_ref.dtype)
        dp_ref[:, w:2 * w] = (dgv * _gelu_grad(v_raw)).astype(dp_ref.dtype)
        dp_ref[:, 2 * w:3 * w] = dza.astype(dp_ref.dtype)

    vec = pl.BlockSpec((1, w), lambda i: (0, 0))
    row = pl.BlockSpec((tl, w), lambda i: (i, 0))
    ws_spec = pl.BlockSpec((nh, CHUNK, CHUNK), lambda i: (0, 0, 0))
    bs_spec = pl.BlockSpec((CHUNK, w), lambda i: (0, 0))
    v = jax.ShapeDtypeStruct((1, w), F32)
    return pl.pallas_call(
        body, name="ga_bwd",
        out_shape=(jax.ShapeDtypeStruct(dproj.shape, dproj.dtype), v, v, jax.ShapeDtypeStruct((nh, CHUNK, CHUNK), F32),
                   jax.ShapeDtypeStruct((CHUNK, w), F32)),
        grid=(l // tl,),
        in_specs=[pl.BlockSpec((tl, 3 * w), lambda i: (i, 0)), row, HBM, vec, vec, ws_spec, bs_spec, row, row, row, vec],
        out_specs=(pl.BlockSpec((tl, 4 * w), lambda i: (i, 0)), vec, vec, ws_spec, bs_spec),
        input_output_aliases={2: 0}, compiler_params=_cparams("arbitrary"))(
            proj, dcat, dproj, g, b, ws, bsf, dys, du0, du1, d_skip)


def _lane_group_sum(x, expand, name):
    return _small_dot(x, expand, "nn", name)


def _disc_math(lr, li, ls, br, bi):
    step = jnp.exp(ls)
    dr, di = lr * step, li * step
    mag = jnp.exp(dr)
    ab_re, ab_im = mag * jnp.cos(di), mag * jnp.sin(di)
    den = lr * lr + li * li
    nr, ni = ab_re - 1.0, ab_im
    f_re = (nr * lr + ni * li) / den
    f_im = (ni * lr - nr * li) / den
    bb_re = f_re * br - f_im * bi
    bb_im = f_re * bi + f_im * br
    return ab_re, ab_im, bb_re, bb_im


def _disc_fwd(lr, li, ls, br, bi):
    def body(lr_ref, li_ref, ls_ref, br_ref, bi_ref, o1, o2, o3, o4):
        res = _disc_math(lr_ref[...], li_ref[...], ls_ref[...], br_ref[...], bi_ref[...])
        for o, r in zip((o1, o2, o3, o4), res):
            o[...] = r

    s = lambda a: jax.ShapeDtypeStruct(a.shape, F32)
    return pl.pallas_call(body, name="s5_disc", out_shape=(s(lr), s(lr), s(br), s(br)), compiler_params=_cparams())(
        lr, li, ls, br, bi)


def _disc_bwd(lr, li, ls, br, bi, d_ar, d_ai, d_br, d_bi):
    def body(lr_ref, li_ref, ls_ref, br_ref, bi_ref, c1, c2, c3, c4, o1, o2, o3, o4, o5):
        _, vjp = jax.vjp(_disc_math, lr_ref[...], li_ref[...], ls_ref[...], br_ref[...], bi_ref[...])
        res = vjp((c1[...], c2[...], c3[...], c4[...]))
        for o, r in zip((o1, o2, o3, o4, o5), res):
            o[...] = r

    s = lambda a: jax.ShapeDtypeStruct(a.shape, F32)
    return pl.pallas_call(body, name="s5_disc_bwd", out_shape=(s(lr), s(lr), s(ls), s(br), s(br)),
                          compiler_params=_cparams())(lr, li, ls, br, bi, d_ar, d_ai, d_br, d_bi)


def _dir_spec(a, dr):
    return pl.BlockSpec((None,) + a.shape[1:], lambda i: (dr,) + (0,) * (a.ndim - 1))


def _s5_fwd(u_arr, u_col, w, h0, a_sm, wbr, wbi, cre, ncim, dr, name):
    rev = dr == 1
    l = u_arr.shape[0]
    nb = w // LANES
    spb = wbr.shape[-1]
    nsr = a_sm.shape[2]
    assert 2 * spb == 8 * LANES and nb % 2 == 0
    npair = nb // 2
    t = _tile(l, 256)
    n = l // t
    tile = (lambda i: n - 1 - i) if rev else (lambda i: i)

    def body(u_ref, h0_ref, a_ref, wbr_ref, wbi_ref, cre_ref, ncim_ref, y_ref, hr_ref, hi_ref, tr_ref, ti_ref, hfin_ref,
             carry_ref):
        i = pl.program_id(0)

        @pl.when(i == 0)
        def _():
            carry_ref[...] = h0_ref[...]

        for j in range(npair):
            for h_ref, w_ref in ((hr_ref, wbr_ref), (hi_ref, wbi_ref)):
                blk = [_mxu_dot(u_ref[:, k * LANES:(k + 1) * LANES], w_ref[k]) for k in (2 * j, 2 * j + 1)]
                h_ref[j] = jnp.concatenate(blk, axis=1).reshape(t, 8, LANES)
        slab = lambda ref, part, j: ref[part, 8 * j:8 * j + 8, :]
        ar = [slab(a_ref, 0, j) for j in range(npair)]
        ai = [slab(a_ref, 1, j) for j in range(npair)]

        def steps(blk, c):
            hr, hi = list(c[:npair]), list(c[npair:])
            for q in range(SCAN_UNROLL):
                s = blk * SCAN_UNROLL + q
                row = t - 1 - s if rev else s
                for j in range(npair):
                    hr[j], hi[j] = (ar[j] * hr[j] - ai[j] * hi[j] + hr_ref[j, row],
                                    ar[j] * hi[j] + ai[j] * hr[j] + hi_ref[j, row])
                    hr_ref[j, row] = hr[j]
                    hi_ref[j, row] = hi[j]
            return tuple(hr + hi)

        init = tuple(slab(carry_ref, part, j) for part in range(2) for j in range(npair))
        c = lax.fori_loop(0, t // SCAN_UNROLL, steps, init)
        for part in range(2):
            for j in range(npair):
                carry_ref[part, 8 * j:8 * j + 8, :] = c[part * npair + j]
                hfin_ref[part, 8 * j:8 * j + 8, :] = c[part * npair + j]
        for j in range(npair):
            cols8 = slice(j * 8 * LANES, (j + 1) * 8 * LANES)
            tr_ref[:, cols8] = hr_ref[j].reshape(t, 8 * LANES).astype(tr_ref.dtype)
            ti_ref[:, cols8] = hi_ref[j].reshape(t, 8 * LANES).astype(ti_ref.dtype)
        for k in range(nb):
            cols = slice(k * spb, (k + 1) * spb)
            y_ref[:, k * LANES:(k + 1) * LANES] = (_mxu_dot(tr_ref[:, cols], cre_ref[k]) + _mxu_dot(ti_ref[:, cols], ncim_ref[k]))

    full = lambda a: pl.BlockSpec(a.shape, lambda i: (0,) * a.ndim)
    hspec = pl.BlockSpec((npair, t, 8, LANES), lambda i: (0, tile(i), 0, 0))
    tspec = pl.BlockSpec((t, nsr * LANES), lambda i: (tile(i), 0))
    hsh = jax.ShapeDtypeStruct((npair, l, 8, LANES), F32)
    tsh = jax.ShapeDtypeStruct((l, nsr * LANES), MXU_DTYPE)
    return pl.pallas_call(
        body, name=name,
        out_shape=(jax.ShapeDtypeStruct((l, w), F32), hsh, hsh, tsh, tsh, jax.ShapeDtypeStruct((2, nsr, LANES), F32)),
        grid=(n,),
        in_specs=[pl.BlockSpec((t, w), lambda i: (tile(i), u_col)), full(h0)] + [_dir_spec(a, dr) for a in (a_sm, wbr, wbi, cre, ncim)],
        out_specs=(pl.BlockSpec((t, w), lambda i: (tile(i), 0)), hspec, hspec, tspec, tspec,
                   pl.BlockSpec((2, nsr, LANES), lambda i: (0, 0, 0))),
        scratch_shapes=[pltpu.VMEM((2, nsr, LANES), F32)],
        compiler_params=_cparams("arbitrary"))(u_arr, h0, a_sm, wbr, wbi, cre, ncim)


def _s5_bwd(dys, u_arr, u_col, w, hr, hi, tr, ti, hbound, g_in, a_sm, wbr_t, wbi_t, cre_t, ncim_t, dr, name):
    rev = dr == 1
    l = u_arr.shape[0]
    nb = w // LANES
    spb = wbr_t.shape[-2]
    nsr = a_sm.shape[2]
    npair = nb // 2
    t = _tile(l, 128)
    n = l // t
    with_dy = dys is not None
    tile = (lambda i: i) if rev else (lambda i: n - 1 - i)

    def body(*refs):
        if with_dy:
            (dy_ref, u_ref, hr_ref, hi_ref, pr_ref, pi_ref, tr_ref, ti_ref, hb_ref, gin_ref, a_ref, wbrt_ref, wbit_ref,
             cret_ref, ncimt_ref, du_ref, dwbr_ref, dwbi_ref, dcre_ref, dncim_ref, da_ref, gout_ref,
             gr_ref, gi_ref, gtr_ref, gti_ref, carry_ref) = refs
        else:
            (u_ref, hr_ref, hi_ref, pr_ref, pi_ref, hb_ref, gin_ref, a_ref, wbrt_ref, wbit_ref,
             du_ref, dwbr_ref, dwbi_ref, da_ref, gout_ref, gr_ref, gi_ref, gtr_ref, gti_ref, carry_ref) = refs
        i = pl.program_id(0)

        @pl.when(i == 0)
        def _():
            carry_ref[...] = gin_ref[...]
            accs = (dwbr_ref, dwbi_ref, da_ref) + ((dcre_ref, dncim_ref) if with_dy else ())
            for r in accs:
                r[...] = jnp.zeros_like(r)

        if with_dy:
            for j in range(npair):
                for g_ref, c_ref in ((gr_ref, cret_ref), (gi_ref, ncimt_ref)):
                    blk = [_mxu_dot(dy_ref[:, k * LANES:(k + 1) * LANES], c_ref[k]) for k in (2 * j, 2 * j + 1)]
                    g_ref[j] = jnp.concatenate(blk, axis=1).reshape(t, 8, LANES)
        else:
            gr_ref[...] = jnp.zeros_like(gr_ref)
            gi_ref[...] = jnp.zeros_like(gi_ref)
        slab = lambda ref, part, j: ref[part, 8 * j:8 * j + 8, :]
        last = t - 1 if rev else 0
        first = i == n - 1

        ar = [slab(a_ref, 0, j) for j in range(npair)]
        ai = [slab(a_ref, 1, j) for j in range(npair)]

        def steps(blk, c):
            gr, gi, dr, di = (list(c[q * npair:(q + 1) * npair]) for q in range(4))
            for q in range(SCAN_UNROLL):
                s = blk * SCAN_UNROLL + q
                row = s if rev else t - 1 - s
                prow = jnp.minimum(row + 1, t - 1) if rev else jnp.maximum(row - 1, 0)
                for j in range(npair):
                    pr, pi = hr_ref[j, prow], hi_ref[j, prow]
                    gr[j], gi[j] = (gr_ref[j, row] + ar[j] * gr[j] + ai[j] * gi[j],
                                    gi_ref[j, row] + ar[j] * gi[j] - ai[j] * gr[j])
                    gr_ref[j, row] = gr[j]
                    gi_ref[j, row] = gi[j]
                    dr[j], di[j] = dr[j] + gr[j] * pr + gi[j] * pi, di[j] + gi[j] * pr - gr[j] * pi
            return tuple(gr + gi + dr + di)

        init = tuple(slab(ref, part, j) for ref in (carry_ref, da_ref) for part in range(2) for j in range(npair))
        c = lax.fori_loop(0, t // SCAN_UNROLL, steps, init)
        gr, gi, dr, di = (c[q * npair:(q + 1) * npair] for q in range(4))
        for j in range(npair):
            pr = jnp.where(first, slab(hb_ref, 0, j), pr_ref[j, 0]) - hr_ref[j, last]
            pi = jnp.where(first, slab(hb_ref, 1, j), pi_ref[j, 0]) - hi_ref[j, last]
            rows = slice(8 * j, 8 * j + 8)
            da_ref[0, rows, :] = dr[j] + gr[j] * pr + gi[j] * pi
            da_ref[1, rows, :] = di[j] + gi[j] * pr - gr[j] * pi
            for part, val in enumerate((gr[j], gi[j])):
                carry_ref[part, rows, :] = val
                gout_ref[part, rows, :] = val

        for j in range(npair):
            cols8 = slice(j * 8 * LANES, (j + 1) * 8 * LANES)
            gtr_ref[:, cols8] = gr_ref[j].reshape(t, 8 * LANES).astype(gtr_ref.dtype)
            gti_ref[:, cols8] = gi_ref[j].reshape(t, 8 * LANES).astype(gti_ref.dtype)
        for k in range(nb):
            cols = slice(k * spb, (k + 1) * spb)
            lanes = slice(k * LANES, (k + 1) * LANES)
            du_ref[:, lanes] = _mxu_dot(gtr_ref[:, cols], wbrt_ref[k]) + _mxu_dot(gti_ref[:, cols], wbit_ref[k])
            dwbr_ref[k] += _mxu_dot(u_ref[:, lanes], gtr_ref[:, cols], _TN)
            dwbi_ref[k] += _mxu_dot(u_ref[:, lanes], gti_ref[:, cols], _TN)
            if with_dy:
                dcre_ref[k] += _mxu_dot(tr_ref[:, cols], dy_ref[:, lanes], _TN)
                dncim_ref[k] += _mxu_dot(ti_ref[:, cols], dy_ref[:, lanes], _TN)

    full = lambda a: pl.BlockSpec(a.shape, lambda i: (0,) * a.ndim)
    row = lambda cb: pl.BlockSpec((t, w), lambda i: (tile(i), cb))
    hspec = pl.BlockSpec((npair, t, 8, LANES), lambda i: (0, tile(i), 0, 0))
    if rev:
        pspec = pl.BlockSpec((npair, 1, 8, LANES), lambda i: (0, jnp.minimum((tile(i) + 1) * t, l - 1), 0, 0))
    else:
        pspec = pl.BlockSpec((npair, 1, 8, LANES), lambda i: (0, jnp.maximum(tile(i) * t - 1, 0), 0, 0))
    sm = jax.ShapeDtypeStruct((2, nsr, LANES), F32)
    smspec = pl.BlockSpec((2, nsr, LANES), lambda i: (0, 0, 0))
    wsh = jax.ShapeDtypeStruct((nb, LANES, spb), F32)
    csh = jax.ShapeDtypeStruct((nb, spb, LANES), F32)
    tspec = pl.BlockSpec((t, nsr * LANES), lambda i: (tile(i), 0))
    in_specs = (([row(0)] if with_dy else []) + [row(u_col), hspec, hspec, pspec, pspec] + ([tspec, tspec] if with_dy else [])
                + [full(hbound), full(g_in)]
                + [_dir_spec(a, dr) for a in (a_sm, wbr_t, wbi_t) + ((cre_t, ncim_t) if with_dy else ())])
    args = (([dys] if with_dy else []) + [u_arr, hr, hi, hr, hi] + ([tr, ti] if with_dy else [])
            + [hbound, g_in, a_sm, wbr_t, wbi_t] + ([cre_t, ncim_t] if with_dy else []))
    out_shape = (jax.ShapeDtypeStruct((l, w), F32), wsh, wsh) + ((csh, csh) if with_dy else ()) + (sm, sm)
    out_specs = (row(0), full(wsh), full(wsh)) + ((full(csh), full(csh)) if with_dy else ()) + (smspec, smspec)
    return pl.pallas_call(
        body, name=name, out_shape=out_shape, grid=(n,), in_specs=in_specs, out_specs=out_specs,
        scratch_shapes=[pltpu.VMEM((npair, t, 8, LANES), F32)] * 2 + [pltpu.VMEM((t, nsr * LANES), MXU_DTYPE)] * 2
        + [pltpu.VMEM((2, nsr, LANES), F32)],
        compiler_params=_cparams("arbitrary"))(*args)


def _glu_fwd(y0, y1, proj, cat, d_skip, w_glu, b_glu, w):
    l = y0.shape[0]
    tl = _tile(l, 256)

    def body(y0_ref, y1_ref, u_ref, z_ref, cat_in, d_ref, wg_ref, bg_ref, ys_ref, cat_ref):
        del cat_in
        ys = y0_ref[...] + y1_ref[...] + d_ref[...] * u_ref[...]
        ys_ref[...] = ys
        gy = _gelu(ys)
        s = _mxu_dot(gy, wg_ref[...]) + bg_ref[...]
        cat_ref[...] = (gy * jax.nn.sigmoid(s) * jax.nn.silu(z_ref[...])).astype(cat_ref.dtype)

    row = pl.BlockSpec((tl, w), lambda i: (i, 0))
    vec = pl.BlockSpec((1, w), lambda i: (0, 0))
    return pl.pallas_call(
        body, name="glu_fwd", out_shape=(jax.ShapeDtypeStruct((l, w), F32), jax.ShapeDtypeStruct(cat.shape, cat.dtype)),
        grid=(l // tl,),
        in_specs=[row, row, pl.BlockSpec((tl, w), lambda i: (i, 3)), pl.BlockSpec((tl, w), lambda i: (i, 4)), HBM,
                  vec, pl.BlockSpec((w, w), lambda i: (0, 0)), vec],
        out_specs=(row, pl.BlockSpec((tl, w), lambda i: (i, 1))), input_output_aliases={4: 1},
        compiler_params=_cparams("parallel"))(y0, y1, proj, proj, cat, d_skip, w_glu, b_glu)


def _glu_bwd(dcat, ys, proj, d_skip, w_glu, b_glu, w):
    l = ys.shape[0]
    tl = _tile(l, 256)

    def body(dy_ref, ys_ref, u_ref, z_ref, d_ref, wg_ref, bg_ref, dys_ref, dp_ref, dbg_ref, dd_ref, dwg_ref):
        i = pl.program_id(0)
        ys_t = ys_ref[...]
        z = z_ref[...]
        dyb = dy_ref[...].astype(F32)
        gy = _gelu(ys_t)
        sg = jax.nn.sigmoid(_mxu_dot(gy, wg_ref[...]) + bg_ref[...])
        dp_ref[...] = (dyb * gy * sg * _silu_grad(z)).astype(dp_ref.dtype)
        dglu = dyb * jax.nn.silu(z)
        ds = dglu * gy * sg * (1.0 - sg)
        dgy = dglu * sg + _mxu_dot(ds, wg_ref[...], _NT)
        dys_t = dgy * _gelu_grad(ys_t)
        dys_ref[...] = dys_t

        @pl.when(i == 0)
        def _():
            for r in (dbg_ref, dd_ref, dwg_ref):
                r[...] = jnp.zeros_like(r)

        dbg_ref[...] += jnp.sum(ds, axis=0, keepdims=True)
        dd_ref[...] += jnp.sum(dys_t * u_ref[...], axis=0, keepdims=True)
        dwg_ref[...] += _mxu_dot(gy, ds, _TN)

    row = pl.BlockSpec((tl, w), lambda i: (i, 0))
    vec = pl.BlockSpec((1, w), lambda i: (0, 0))
    mat = pl.BlockSpec((w, w), lambda i: (0, 0))
    v = jax.ShapeDtypeStruct((1, w), F32)
    return pl.pallas_call(
        body, name="glu_bwd",
        out_shape=(jax.ShapeDtypeStruct((l, w), F32), jax.ShapeDtypeStruct((l, 5 * w), MXU_DTYPE), v, v,
                   jax.ShapeDtypeStruct((w, w), F32)),
        grid=(l // tl,),
        in_specs=[pl.BlockSpec((tl, w), lambda i: (i, 1)), row, pl.BlockSpec((tl, w), lambda i: (i, 3)),
                  pl.BlockSpec((tl, w), lambda i: (i, 4)), vec, mat, vec],
        out_specs=(row, pl.BlockSpec((tl, w), lambda i: (i, 4)), vec, vec, mat),
        compiler_params=_cparams("arbitrary"))(dcat, ys, proj, proj, d_skip, w_glu, b_glu)


def _add2(a, b, name):
    l, w = a.shape
    tl = _tile(l, 512)

    def body(a_ref, b_ref, o_ref):
        o_ref[...] = a_ref[...] + b_ref[...]

    row = pl.BlockSpec((tl, w), lambda i: (i, 0))
    return pl.pallas_call(body, name=name, out_shape=jax.ShapeDtypeStruct((l, w), F32), grid=(l // tl,),
                          in_specs=[row, row], out_specs=row, compiler_params=_cparams("parallel"))(a, b)


def _adamw_nd(w, m, v, g, name):
    shape = w.shape
    lead = math.prod(shape[:-2]) if len(shape) > 2 else 1
    b, c = (shape[-2], shape[-1]) if len(shape) >= 2 else (1, shape[-1])
    t3 = (lead, b, c)
    padded_row = -(-b // 8) * 8 * -(-c // LANES) * LANES * 4
    ta = _tile(lead, max(1, (2 << 20) // padded_row))

    def body(w_ref, m_ref, v_ref, g_ref, d_ref, mo_ref, vo_ref):
        gv = g_ref[...]
        mn = ADAM_B1 * m_ref[...] + (1.0 - ADAM_B1) * gv
        vn = ADAM_B2 * v_ref[...] + (1.0 - ADAM_B2) * (gv * gv)
        m_hat = mn / (1.0 - ADAM_B1 ** ADAM_STEP)
        v_hat = vn / (1.0 - ADAM_B2 ** ADAM_STEP)
        d_ref[...] = -ADAM_LR * (m_hat / (jnp.sqrt(v_hat) + ADAM_EPS) + ADAM_WD * w_ref[...])
        mo_ref[...] = mn
        vo_ref[...] = vn

    blk = pl.BlockSpec((ta, b, c), lambda i: (i, 0, 0))
    s = jax.ShapeDtypeStruct(t3, F32)
    outs = pl.pallas_call(body, name=name, out_shape=(s, s, s), grid=(lead // ta,), in_specs=[blk] * 4, out_specs=(blk,) * 3,
                          compiler_params=_cparams("parallel"))(*[a.reshape(t3) for a in (w, m, v, g)])
    return tuple(o.reshape(shape) for o in outs)


def _adamw(w, m, v, gparts, name, dep=None):
    r, c = w.shape
    np_ = gparts.shape[0]
    tr = _tile(r, max(8, (1 << 18) // c), 8)

    def body(w_ref, m_ref, v_ref, g_ref, *rest):
        go_ref, d_ref, mo_ref, vo_ref = rest[-4:]
        g = g_ref[0].astype(F32)
        for p in range(1, np_):
            g = g + g_ref[p].astype(F32)
        mn = ADAM_B1 * m_ref[...] + (1.0 - ADAM_B1) * g
        vn = ADAM_B2 * v_ref[...] + (1.0 - ADAM_B2) * (g * g)
        m_hat = mn / (1.0 - ADAM_B1 ** ADAM_STEP)
        v_hat = vn / (1.0 - ADAM_B2 ** ADAM_STEP)
        go_ref[...] = g
        d_ref[...] = -ADAM_LR * (m_hat / (jnp.sqrt(v_hat) + ADAM_EPS) + ADAM_WD * w_ref[...])
        mo_ref[...] = mn
        vo_ref[...] = vn

    row = pl.BlockSpec((tr, c), lambda i: (i, 0))
    s = jax.ShapeDtypeStruct((r, c), F32)
    extra = [] if dep is None else [dep]
    return pl.pallas_call(body, name=name, out_shape=(s, s, s, s), grid=(r // tr,),
                          in_specs=[row, row, row, pl.BlockSpec((np_, tr, c), lambda i: (0, i, 0))] + [HBM] * len(extra),
                          out_specs=(row, row, row, row), compiler_params=_cparams("parallel"))(w, m, v, gparts, *extra)


def _sum_slots(parts, name):
    np_, r, c = parts.shape

    def body(p_ref, o_ref):
        g = p_ref[0]
        for p in range(1, np_):
            g = g + p_ref[p]
        o_ref[...] = g

    return pl.pallas_call(body, name=name, out_shape=jax.ShapeDtypeStruct((r, c), F32), compiler_params=_cparams())(parts)


def _block_diag(x, gb):
    nd, g, a, b = x.shape
    eye = jnp.eye(gb, dtype=x.dtype)
    y = jnp.einsum("dkgab,gh->dkgahb", x.reshape(nd, g // gb, gb, a, b), eye)
    return y.reshape(nd, g // gb, gb * a, gb * b)


def _block_diag_extract(y, gb, a, b):
    nd, nbk = y.shape[:2]
    eye = jnp.eye(gb, dtype=y.dtype)
    x = jnp.einsum("dkgahb,gh->dkgab", y.reshape(nd, nbk, gb, a, gb, b), eye)
    return x.reshape(nd, nbk * gb, a, b)


def kernel(x, c, ctx, c_ctx, w_ada, b_ada, w_in, sgu_ln_g, sgu_ln_b, w_spatial, b_spatial, s5_lam_re, s5_lam_im, s5_log_step, s5_b_re, s5_b_im, s5_c_re, s5_c_im, s5_d, w_glu, b_glu, w_out, ln_g, ln_b, loss_target, m_c_ctx, m_w_ada, m_b_ada, m_w_in, m_sgu_ln_g, m_sgu_ln_b, m_w_spatial, m_b_spatial, m_s5_lam_re, m_s5_lam_im, m_s5_log_step, m_s5_b_re, m_s5_b_im, m_s5_c_re, m_s5_c_im, m_s5_d, m_w_glu, m_b_glu, m_w_out, m_ln_g, m_ln_b, v_c_ctx, v_w_ada, v_b_ada, v_w_in, v_sgu_ln_g, v_sgu_ln_b, v_w_spatial, v_b_spatial, v_s5_lam_re, v_s5_lam_im, v_s5_log_step, v_s5_b_re, v_s5_b_im, v_s5_c_re, v_s5_c_im, v_s5_d, v_w_glu, v_b_glu, v_w_out, v_ln_g, v_ln_b):
    small_names = ["c_ctx", "b_ada", "sgu_ln_g", "sgu_ln_b", "w_spatial", "b_spatial", "s5_lam_re", "s5_lam_im",
                   "s5_log_step", "s5_b_re", "s5_b_im", "s5_c_re", "s5_c_im", "s5_d", "b_glu", "ln_g", "ln_b"]
    env = dict(locals())
    x2, tgt, ctx2 = x[0], loss_target[0], ctx[0]
    l, d = x2.shape
    lc = ctx2.shape[0]
    w = d // 2
    nh = w // HEAD_DIM_A
    nd, g_s5, p_s5, c_s5 = s5_b_re.shape[1:]
    ns = g_s5 * p_s5
    nsr = ns // LANES
    gb = LANES // c_s5
    me = _index(_mesh_pos())
    ada_cols = w_ada.shape[2]

    srows = _silu_rows(c, c_ctx)
    srows_all = _all_gather(srows, 0, "gather_silu")
    s_mat = jnp.concatenate([srows_all[0::8], srows_all[1:2], jnp.zeros((7, d), F32)], axis=0)
    mod_part = _small_dot(s_mat, w_ada[0], "nn", "mod_cols")
    mod_all = _all_gather(mod_part, 1, "gather_mod") + b_ada
    hw_in, tok_a = _exchange_start(w_in[0].astype(MXU_DTYPE), 1, "gather", "start_gather_w_in", (SIBLING,) + SAME_CORE_PEERS)
    mod_all = mod_all + tok_a[0, 0]
    mod_x = lax.dynamic_slice_in_dim(mod_all, me, 1, axis=0)
    mod_c = mod_all[8:9]
    shift_x, scale_x, gate_x = mod_x[:, :d], mod_x[:, d:2 * d], mod_x[:, 2 * d:]
    shift_c, scale_c = mod_c[:, :d], mod_c[:, d:2 * d]

    lr, li = s5_lam_re[0][:, :, None, :], s5_lam_im[0][:, :, None, :]
    ls = s5_log_step[0][:, :, None, None]
    swapped = ("s5_b_re", "s5_b_im")
    for nm in swapped:
        for pre in ("", "m_", "v_"):
            env[pre + nm] = jnp.swapaxes(env[pre + nm], -1, -2)
    br_t, bi_t = env["s5_b_re"][0], env["s5_b_im"][0]
    ab_re, ab_im, bb_re, bb_im = _disc_fwd(lr, li, ls, br_t, bi_t)
    a_sm = jnp.stack([ab_re, ab_im], axis=1).reshape(nd, 2, nsr, LANES)
    wbr = _block_diag(bb_re.astype(MXU_DTYPE), gb)
    wbi = _block_diag(bb_im.astype(MXU_DTYPE), gb)
    cre_t = _block_diag(s5_c_re[0].astype(MXU_DTYPE), gb)
    ncim_t = _block_diag((-s5_c_im[0]).astype(MXU_DTYPE), gb)
    cre, ncim = jnp.swapaxes(cre_t, 2, 3), jnp.swapaxes(ncim_t, 2, 3)
    wbr_t, wbi_t = jnp.swapaxes(wbr, 2, 3), jnp.swapaxes(wbi, 2, 3)
    d_skip = s5_d

    xm = _ln_mod(x2, shift_x, scale_x, "ln_mod_x")
    cm = _ln_mod(ctx2, shift_c, scale_c, "ln_mod_ctx")
    ready = xm[:8, :LANES].astype(F32) + cm[:8, :LANES].astype(F32) + cre[0, 0, :8, :].astype(F32)
    hw_in2, tok_b = _forward_start(_exchange_wait(hw_in, ready, "wait_gather_w_in"), 1, "start_forward_w_in")
    w_in_f = _forward_wait(hw_in2, tok_b, "wait_forward_w_in")
    hw_glu, tok_c = _exchange_start(w_glu[0].astype(MXU_DTYPE), 0, "gather", "start_gather_w_glu")
    hw_out, tok_o = _exchange_start(w_out[0].astype(MXU_DTYPE), 0, "gather", "start_gather_w_out")
    proj = _matmul(xm, w_in_f, mode="nn", name="proj", bm=1024, bn=1024, bk=d, dep=tok_c + tok_o)
    ub_c = _matmul(cm, w_in_f, mode="nn", name="proj_ctx", bm=256, bn=w, bk=d, b_n0=3 * w, n=w)
    bsf = jnp.repeat(b_spatial[0].T, HEAD_DIM_A, axis=1)
    ws = w_spatial[0]
    cat = _ga_fwd(proj, sgu_ln_g, sgu_ln_b, ws, bsf, w)
    zeros_state = jnp.zeros((2, nsr, LANES), F32)
    s5c, s5l = [], []
    for dr in range(nd):
        s5c.append(_s5_fwd(ub_c, 0, w, zeros_state, a_sm, wbr, wbi, cre, ncim, dr, f"s5_fwd_ctx{dr}"))
        s5l.append(_s5_fwd(proj, 3, w, s5c[dr][5], a_sm, wbr, wbi, cre, ncim, dr, f"s5_fwd{dr}"))
    w_glu_f = _exchange_wait(hw_glu, s5l[1][0], "wait_gather_w_glu")
    ys, cat = _glu_fwd(s5l[0][0], s5l[1][0], proj, cat, d_skip, w_glu_f, b_glu, w)
    w_out_f = _exchange_wait(hw_out, ys, "wait_gather_w_out")
    out = _matmul(cat, w_out_f, mode="nn", name="out_proj", bm=1024, bn=1024, bk=2 * w)
    loss_row, dout, dx_res, dgate, dln_g, dln_b = _post_ln_loss(x2, out, gate_x, ln_g, ln_b, tgt)

    dcat = _matmul(dout, w_out_f, mode="nt", name="d_cat", bm=1024, bn=1024, bk=d, out_dtype=MXU_DTYPE)
    dw_out = _matmul(cat, dout, mode="tn", name="d_w_out", bm=1024, bn=1024, bk=2048, out_dtype=MXU_DTYPE)
    hg_out, tok_d = _exchange_start(dw_out, 0, "a2a", "start_a2a_d_w_out")
    dys, dproj, db_glu, dd_skip, dw_glu = _glu_bwd(dcat, ys, proj, d_skip + tok_d[0, 0], w_glu_f, b_glu, w)
    hg_glu, tok_e = _exchange_start(dw_glu.astype(MXU_DTYPE), 0, "a2a", "start_a2a_d_w_glu")
    zeros_state = zeros_state + tok_e[0, 0]
    du_l, du_c, dwbr, dwbi, dcre, dncim, da_sm = [], [], [], [], [], [], []
    nbk, spb = w // LANES, gb * p_s5
    for dr in range(nd):
        bl = _s5_bwd(dys, proj, 3, w, *s5l[dr][1:5], s5c[dr][5], zeros_state, a_sm, wbr_t, wbi_t,
                     cre_t, ncim_t, dr, f"s5_bwd{dr}")
        bc = _s5_bwd(None, ub_c, 0, w, s5c[dr][1], s5c[dr][2], None, None, zeros_state, bl[6], a_sm, wbr_t, wbi_t,
                     None, None, dr, f"s5_bwd_ctx{dr}")
        du_l.append(bl[0])
        du_c.append(bc[0])
        dwbr.append(_add2(bl[1].reshape(nbk * LANES, spb), bc[1].reshape(nbk * LANES, spb), f"sum_dwbr{dr}"))
        dwbi.append(_add2(bl[2].reshape(nbk * LANES, spb), bc[2].reshape(nbk * LANES, spb), f"sum_dwbi{dr}"))
        dcre.append(bl[3])
        dncim.append(bl[4])
        da_sm.append(_add2(bl[5].reshape(2 * nsr, LANES), bc[3].reshape(2 * nsr, LANES), f"sum_da{dr}"))
    dub_c = _add2(du_c[0], du_c[1], "dub_ctx")
    dwbr = jnp.stack(dwbr).reshape(nd, nbk, LANES, spb)
    dwbi = jnp.stack(dwbi).reshape(nd, nbk, LANES, spb)
    dcre, dncim = jnp.stack(dcre), jnp.stack(dncim)
    da_sm = jnp.stack(da_sm).reshape(nd, 2, g_s5, p_s5)
    dproj, dsg, dsb, dws, dbsf = _ga_bwd(proj, dcat, dproj, sgu_ln_g, sgu_ln_b, ws, bsf, dys, du_l[0], du_l[1], d_skip, w)

    dbb_re = _block_diag_extract(dwbr, gb, c_s5, p_s5)
    dbb_im = _block_diag_extract(dwbi, gb, c_s5, p_s5)
    dc_re = jnp.swapaxes(_block_diag_extract(dcre, gb, p_s5, c_s5), 2, 3)
    dc_im = -jnp.swapaxes(_block_diag_extract(dncim, gb, p_s5, c_s5), 2, 3)
    dlr, dli, dls, db_re, db_im = _disc_bwd(lr, li, ls, br_t, bi_t, da_sm[:, 0:1].reshape(nd, g_s5, 1, p_s5),
                                            da_sm[:, 1:2].reshape(nd, g_s5, 1, p_s5), dbb_re, dbb_im)
    expand = (jnp.arange(w)[:, None] // HEAD_DIM_A == jnp.arange(LANES)[None, :]).astype(F32)
    db_sp = _lane_group_sum(dbsf, expand, "d_b_spatial")[:, :nh].T

    local = {"sgu_ln_g": dsg, "sgu_ln_b": dsb, "w_spatial": dws, "b_spatial": db_sp,
             "s5_lam_re": dlr, "s5_lam_im": dli, "s5_log_step": dls, "s5_b_re": db_re, "s5_b_im": db_im,
             "s5_c_re": dc_re, "s5_c_im": dc_im, "s5_d": dd_skip, "b_glu": db_glu, "ln_g": dln_g, "ln_b": dln_b}
    reduced = sorted(local, key=lambda n: -math.prod(env[n].shape))
    loss_part = (0.5 / d) * jnp.sum(loss_row)
    flat = jnp.concatenate([local[n].reshape(-1) for n in reduced] + [loss_part.reshape(1)])
    unit = N_DEV * 8 * LANES
    total = -(-flat.shape[0] // unit) * unit
    flat = jnp.pad(flat, (0, total - flat.shape[0])).reshape(N_DEV * 8, total // (N_DEV * 8))
    h_small, tok_s = _exchange_start(flat, 0, "a2a", "start_a2a_small")

    dw_in = _matmul(xm, dproj, mode="tn", name="d_w_in", bm=1024, bn=1280, bk=2048, out_dtype=MXU_DTYPE, dep=tok_s)
    dw_in = _matmul(cm, dub_c, mode="tn", name="d_w_in_ctx", bm=1024, bn=w, bk=lc, acc_in=dw_in, acc_n0=3 * w,
                    out_dtype=MXU_DTYPE)
    hg_in, tok_f = _exchange_start(dw_in, 1, "a2a", "start_a2a_d_w_in")
    mine = _sum_slots(_exchange_wait(h_small, dw_in, "wait_a2a_small"), "sum_small")
    h_sums, tok_g = _exchange_start(mine, 0, "gather", "start_gather_small")
    dxm = _matmul(dproj, w_in_f, mode="nt", name="d_xm", bm=1024, bn=512, bk=5 * w, dep=tok_f + tok_g,
                  out_dtype=MXU_DTYPE)
    dcm = _matmul(dub_c, w_in_f, mode="nt", name="d_cm", bm=256, bn=1024, bk=w, b_k0=3 * w, k=w)
    grad_x, dshift_x, dscale_x = _ln_mod_bwd(x2, dxm, scale_x, dx_res, "ln_mod_x_bwd")
    _, dshift_c, dscale_c = _ln_mod_bwd(ctx2, dcm, scale_c, None, "ln_mod_ctx_bwd")

    dmod_rows = jnp.concatenate([jnp.concatenate([dshift_x, dscale_x, dgate], axis=1),
                                 jnp.concatenate([dshift_c, dscale_c, jnp.zeros((1, d), F32)], axis=1),
                                 jnp.zeros((6, 3 * d), F32)], axis=0)
    h_dmod, tok_m = _exchange_start(dmod_rows, 0, "gather", "start_gather_dmod")

    gp_w_out = _exchange_wait(hg_out, tok_m, "wait_a2a_d_w_out")
    gp_w_glu = _exchange_wait(hg_glu, tok_m, "wait_a2a_d_w_glu")
    gp_w_in = _exchange_wait(hg_in, tok_m, "wait_a2a_d_w_in")
    big = {
        "w_in": _adamw(w_in[0], m_w_in[0], v_w_in[0], gp_w_in, "adamw_w_in"),
        "w_glu": _adamw(w_glu[0], m_w_glu[0], v_w_glu[0], gp_w_glu, "adamw_w_glu"),
        "w_out": _adamw(w_out[0], m_w_out[0], v_w_out[0], gp_w_out, "adamw_w_out"),
    }
    dmod_all = _exchange_wait(h_dmod, big["w_out"][0], "wait_gather_dmod")
    dmod_ctx = _sum_slots(dmod_all[1::8].reshape(N_DEV, 1, 3 * d), "sum_dmod_ctx")
    dmod_mat = jnp.concatenate([dmod_all[0::8], dmod_ctx, jnp.zeros((7, 3 * d), F32)], axis=0)
    db_ada = _sum_slots(dmod_mat[:9].reshape(9, 1, 3 * d), "sum_db_ada")
    dmod_mine = lax.dynamic_slice_in_dim(dmod_mat, me * ada_cols, ada_cols, axis=1)
    dw_ada = _small_dot(s_mat, dmod_mine, "tn", "d_w_ada")
    dsilu_cc = _small_dot(dmod_mine[8:16], w_ada[0], "nt", "d_silu_cctx")[0:1]
    dc_ctx_part = dsilu_cc * _silu_grad(c_ctx.reshape(1, d))
    dc_ctx_rows = jnp.concatenate([dc_ctx_part, jnp.zeros((7, d), F32)], axis=0)
    h_cctx, tok_c2 = _exchange_start(dc_ctx_rows, 0, "gather", "start_gather_d_c_ctx")

    big["w_ada"] = _adamw(w_ada[0], m_w_ada[0], v_w_ada[0], dw_ada[None], "adamw_w_ada", dep=tok_c2)
    summed = _exchange_wait(h_sums, big["w_ada"][0], "wait_gather_small").reshape(-1)
    grads, off = {"b_ada": db_ada}, 0
    for n in reduced:
        size = math.prod(env[n].shape)
        grads[n] = summed[off:off + size].reshape(env[n].shape)
        off += size
    loss = summed[off]
    res = {n: tuple(a[None] for a in big[n]) for n in big}

    def small_step(n):
        res[n] = (grads[n],) + _adamw_nd(env[n], env["m_" + n], env["v_" + n], grads[n], "adamw_" + n)
        if n in swapped:
            res[n] = tuple(jnp.swapaxes(a, -1, -2) for a in res[n])

    for n in small_names:
        if n != "c_ctx":
            small_step(n)
    dc_ctx_all = _exchange_wait(h_cctx, res["w_spatial"][1], "wait_gather_d_c_ctx")
    grads["c_ctx"] = _sum_slots(dc_ctx_all[0::8].reshape(N_DEV, 1, d), "sum_d_c_ctx").reshape(d)
    small_step("c_ctx")

    order = ["c_ctx", "w_ada", "b_ada", "w_in", "sgu_ln_g", "sgu_ln_b", "w_spatial", "b_spatial", "s5_lam_re", "s5_lam_im",
             "s5_log_step", "s5_b_re", "s5_b_im", "s5_c_re", "s5_c_im", "s5_d", "w_glu", "b_glu", "w_out", "ln_g", "ln_b"]
    return (loss, grad_x[None], *[res[n][0] for n in order], *[res[n][1] for n in order],
            *[res[n][2] for n in order], *[res[n][3] for n in order])
```

```python
import functools
import math

import jax
import jax.numpy as jnp
from jax import lax
from jax.experimental import pallas as pl
from jax.experimental.pallas import tpu as pltpu

F32 = jnp.float32
MXU_DTYPE = jnp.bfloat16
N_DEV = 8
MESH_ID = pl.DeviceIdType.MESH
LN_EPS = 1e-6
DEPTH = 1
ALPHA = (2.0 * DEPTH) ** 0.25
CHUNK = 128
HEAD_DIM_A = 128
ADAM_LR, ADAM_B1, ADAM_B2, ADAM_EPS, ADAM_WD, ADAM_STEP = 0.001, 0.9, 0.999, 1e-08, 0.01, 10
LANES = 128
SCAN_UNROLL = 8
VMEM_LIMIT = 56 * 1024 * 1024
HBM = pl.BlockSpec(memory_space=pl.ANY)


def _cparams(*sem):
    return pltpu.CompilerParams(dimension_semantics=sem if sem else None, vmem_limit_bytes=VMEM_LIMIT)


def _tile(n, pref, mult=1):
    if n <= pref:
        return n
    t = pref - pref % mult
    while n % t:
        t -= mult
    return t


def _gelu(x):
    return 0.5 * x * (1.0 + lax.erf(x * (1.0 / math.sqrt(2.0))))


def _gelu_grad(x):
    return 0.5 * (1.0 + lax.erf(x * (1.0 / math.sqrt(2.0)))) + x * jnp.exp(-0.5 * x * x) * (1.0 / math.sqrt(2.0 * math.pi))


def _silu_grad(x):
    s = jax.nn.sigmoid(x)
    return s * (1.0 + x * (1.0 - s))


def _mxu_dot(a, b, dims=(((1,), (0,)), ((), ()))):
    return lax.dot_general(a.astype(MXU_DTYPE), b.astype(MXU_DTYPE), dims, preferred_element_type=F32)


_NT = (((1,), (1,)), ((), ()))
_TN = (((0,), (0,)), ((), ()))


def _mesh_pos():
    return lax.axis_index("x"), lax.axis_index("y"), lax.axis_index("c")


def _peer(pos, r):
    x, y, c = pos
    return ((1 - x) if r & 4 else x, (1 - y) if r & 2 else y, (1 - c) if r & 1 else c)


def _index(pos):
    return 4 * pos[0] + 2 * pos[1] + pos[2]


def _slice_of(ref, axis, idx, size):
    start = idx * size
    if axis == 0:
        return ref.at[pl.ds(start, size)]
    return ref.at[:, pl.ds(start, size)]


def _all_gather(x, axis, name):
    size = x.shape[axis]
    out_shape = tuple(s * N_DEV if a == axis else s for a, s in enumerate(x.shape))

    def body(x_ref, o_ref, send_sems, recv_sems, local_sem):
        me = _mesh_pos()
        mine = pltpu.make_async_copy(x_ref, _slice_of(o_ref, axis, _index(me), size), local_sem)
        mine.start()

        def copy(r, block):
            return pltpu.make_async_remote_copy(
                src_ref=x_ref, dst_ref=_slice_of(o_ref, axis, _index(block), size),
                send_sem=send_sems.at[r - 1], recv_sem=recv_sems.at[r - 1],
                device_id=_peer(me, r), device_id_type=MESH_ID)

        sends = [copy(r, me) for r in range(1, N_DEV)]
        for cp in sends:
            cp.start()
        for r in range(1, N_DEV):
            copy(r, _peer(me, r)).wait_recv()
        for cp in sends:
            cp.wait_send()
        mine.wait()

    return pl.pallas_call(
        body, name=name, out_shape=jax.ShapeDtypeStruct(out_shape, x.dtype),
        in_specs=[HBM], out_specs=HBM,
        scratch_shapes=[pltpu.SemaphoreType.DMA((N_DEV - 1,)), pltpu.SemaphoreType.DMA((N_DEV - 1,)),
                        pltpu.SemaphoreType.DMA],
    )(x)


_SEM = pl.BlockSpec(memory_space=pltpu.SEMAPHORE)
_HBM = pl.BlockSpec(memory_space=pltpu.HBM)
_EFFECT = pltpu.SideEffectType.DATAFLOW_SIDE_EFFECTING
ALL_PEERS = tuple(range(1, N_DEV))
SIBLING = 1
SAME_CORE_PEERS = (2, 4, 6)


def _exchange_copy(kind, x_ref, land_ref, axis, size, send_sems, recv_sems, me, rels, q, arriving):
    peer = _peer(me, rels[q])
    sender, receiver = (peer, me) if arriving else (me, peer)
    if kind == "gather":
        src, dst = x_ref, _slice_of(land_ref, axis, _index(sender), size)
    else:
        src, dst = _slice_of(x_ref, axis, _index(receiver), size), land_ref.at[_index(sender)]
    return pltpu.make_async_remote_copy(src_ref=src, dst_ref=dst, send_sem=send_sems.at[q], recv_sem=recv_sems.at[q],
                                        device_id=peer, device_id_type=MESH_ID)


def _local_copy(kind, x_ref, land_ref, axis, size, me, local_sem):
    if kind == "gather":
        return pltpu.make_async_copy(x_ref, _slice_of(land_ref, axis, _index(me), size), local_sem)
    return pltpu.make_async_copy(_slice_of(x_ref, axis, _index(me), size), land_ref.at[_index(me)], local_sem)


def _exchange_start(x, axis, kind, name, rels=ALL_PEERS):
    size = x.shape[axis] if kind == "gather" else x.shape[axis] // N_DEV
    if kind == "gather":
        land_shape = tuple(s * N_DEV if a == axis else s for a, s in enumerate(x.shape))
    else:
        land_shape = (N_DEV,) + tuple(size if a == axis else s for a, s in enumerate(x.shape))

    def body(x_ref, land_ref, send_sems, recv_sems, local_sem, x_thru, land_thru, token):
        del x_thru, land_thru
        me = _mesh_pos()
        _local_copy(kind, x_ref, land_ref, axis, size, me, local_sem).start()
        for q in range(len(rels)):
            _exchange_copy(kind, x_ref, land_ref, axis, size, send_sems, recv_sems, me, rels, q, False).start()
        token[...] = jnp.zeros_like(token)

    sems = pltpu.SemaphoreType.DMA((len(rels),))
    send_sems, recv_sems, local_sem, x_thru, land_thru, token = pl.pallas_call(
        body, name=name,
        out_shape=(sems, sems, pltpu.SemaphoreType.DMA(()), pltpu.HBM(x.shape, x.dtype), pltpu.HBM(land_shape, x.dtype),
                   jax.ShapeDtypeStruct((8, LANES), F32)),
        in_specs=(_HBM, _HBM), out_specs=(_SEM, _SEM, _SEM, _HBM, _HBM, pl.BlockSpec(memory_space=pltpu.VMEM)),
        input_output_aliases={0: 3, 1: 4}, compiler_params=pltpu.CompilerParams(has_side_effects=_EFFECT),
    )(pltpu.with_memory_space_constraint(x, pltpu.HBM),
      pltpu.with_memory_space_constraint(lax.empty(land_shape, x.dtype), pltpu.HBM))
    return (kind, axis, size, rels, send_sems, recv_sems, local_sem, x_thru, land_thru), token


def _exchange_wait(handle, after, name):
    kind, axis, size, rels, send_sems, recv_sems, local_sem, x_thru, land_thru = handle

    def body(x_ref, land_ref, send_sems, recv_sems, local_sem, after_ref, x_dead, got_ref):
        del after_ref, x_dead, got_ref
        me = _mesh_pos()
        _local_copy(kind, x_ref, land_ref, axis, size, me, local_sem).wait()
        for q in range(len(rels)):
            _exchange_copy(kind, x_ref, land_ref, axis, size, send_sems, recv_sems, me, rels, q, False).wait_send()
        for q in range(len(rels)):
            _exchange_copy(kind, x_ref, land_ref, axis, size, send_sems, recv_sems, me, rels, q, True).wait_recv()

    return pl.pallas_call(
        body, name=name, out_shape=(pltpu.HBM(x_thru.shape, x_thru.dtype), pltpu.HBM(land_thru.shape, land_thru.dtype)),
        in_specs=(_HBM, _HBM, _SEM, _SEM, _SEM, HBM), out_specs=(_HBM, _HBM), input_output_aliases={0: 0, 1: 1},
        compiler_params=pltpu.CompilerParams(has_side_effects=_EFFECT),
    )(x_thru, land_thru, send_sems, recv_sems, local_sem, after)[1]


def _forward_copy(land_ref, axis, size, send_sems, recv_sems, me, q, arriving):
    sibling = _peer(me, SIBLING)
    owner = _peer(sibling if arriving else me, SAME_CORE_PEERS[q])
    block = _slice_of(land_ref, axis, _index(owner), size)
    return pltpu.make_async_remote_copy(src_ref=block, dst_ref=block, send_sem=send_sems.at[q], recv_sem=recv_sems.at[q],
                                        device_id=sibling, device_id_type=MESH_ID)


def _forward_start(land, axis, name):
    size = land.shape[axis] // N_DEV

    def body(land_ref, send_sems, recv_sems, land_thru, token):
        del land_thru
        me = _mesh_pos()
        for q in range(len(SAME_CORE_PEERS)):
            _forward_copy(land_ref, axis, size, send_sems, recv_sems, me, q, False).start()
        token[...] = jnp.zeros_like(token)

    sems = pltpu.SemaphoreType.DMA((len(SAME_CORE_PEERS),))
    send_sems, recv_sems, land_thru, token = pl.pallas_call(
        body, name=name, out_shape=(sems, sems, pltpu.HBM(land.shape, land.dtype), jax.ShapeDtypeStruct((8, LANES), F32)),
        in_specs=(_HBM,), out_specs=(_SEM, _SEM, _HBM, pl.BlockSpec(memory_space=pltpu.VMEM)),
        input_output_aliases={0: 2}, compiler_params=pltpu.CompilerParams(has_side_effects=_EFFECT),
    )(land)
    return (axis, size, send_sems, recv_sems, land_thru), token


def _forward_wait(handle, after, name):
    axis, size, send_sems, recv_sems, land_thru = handle

    def body(land_ref, send_sems, recv_sems, after_ref, got_ref):
        del after_ref, got_ref
        me = _mesh_pos()
        for q in range(len(SAME_CORE_PEERS)):
            _forward_copy(land_ref, axis, size, send_sems, recv_sems, me, q, False).wait_send()
        for q in range(len(SAME_CORE_PEERS)):
            _forward_copy(land_ref, axis, size, send_sems, recv_sems, me, q, True).wait_recv()

    return pl.pallas_call(
        body, name=name, out_shape=pltpu.HBM(land_thru.shape, land_thru.dtype),
        in_specs=(_HBM, _SEM, _SEM, HBM), out_specs=_HBM, input_output_aliases={0: 0},
        compiler_params=pltpu.CompilerParams(has_side_effects=_EFFECT),
    )(land_thru, send_sems, recv_sems, after)


WHOLE = 1 << 30
MATMUL_TILES = {
    "proj": (1024, 1024, WHOLE), "proj_ctx": (256, WHOLE, WHOLE), "out_proj": (1024, 1024, WHOLE),
    "d_cat": (1024, 1024, WHOLE), "d_w_out": (1024, 1024, 2048), "d_w_in": (1024, 1280, 2048),
    "d_w_in_ctx": (1024, WHOLE, WHOLE), "d_xm": (1024, 512, WHOLE), "d_cm": (256, 1024, WHOLE),
}


def _matmul(a, b, *, mode, name, out_dtype=F32, b_n0=0, n=None, b_k0=0, k=None, acc_in=None, acc_n0=0, dep=None):
    bm, bn, bk = MATMUL_TILES[name]
    if mode == "tn":
        kk, m = a.shape
    else:
        m, kk = a.shape
    if mode == "nn":
        n = b.shape[1] if n is None else n
    elif mode == "nt":
        n = b.shape[0]
        kk = kk if k is None else k
    else:
        n = b.shape[1]
    bm, bn, bk = _tile(m, bm), _tile(n, bn), _tile(kk, bk)
    nk = kk // bk
    assert b_n0 % bn == 0 and b_k0 % bk == 0 and acc_n0 % bn == 0
    dims = {"nn": (((1,), (0,)), ((), ())), "nt": _NT, "tn": _TN}[mode]

    n_in = 2 + (acc_in is not None) + (dep is not None)

    def body(*refs):
        a_ref, b_ref = refs[:2]
        init = refs[2] if acc_in is not None else None
        o_ref = refs[n_in]
        acc_ref = refs[-1] if nk > 1 else None
        p = _mxu_dot(a_ref[...], b_ref[...], dims)
        if nk == 1:
            o_ref[...] = (p if init is None else p + init[...]).astype(out_dtype)
            return
        ki = pl.program_id(2)

        @pl.when(ki == 0)
        def _():
            acc_ref[...] = p if init is None else p + init[...]

        @pl.when(ki > 0)
        def _():
            acc_ref[...] += p

        @pl.when(ki == nk - 1)
        def _():
            o_ref[...] = acc_ref[...].astype(out_dtype)

    a_spec = pl.BlockSpec((bk, bm), lambda j, i, q: (q, i)) if mode == "tn" else pl.BlockSpec((bm, bk), lambda j, i, q: (i, q))
    if mode == "nt":
        b_spec = pl.BlockSpec((bn, bk), lambda j, i, q: (j, q + b_k0 // bk))
    else:
        b_spec = pl.BlockSpec((bk, bn), lambda j, i, q: (q, j + b_n0 // bn))
    in_specs, args, aliases = [a_spec, b_spec], [a, b], {}
    out_map = lambda j, i, q: (i, j + acc_n0 // bn)
    if acc_in is not None:
        in_specs.append(pl.BlockSpec((bm, bn), out_map))
        args.append(acc_in)
        aliases = {2: 0}
        out_shape = jax.ShapeDtypeStruct(acc_in.shape, out_dtype)
    else:
        out_shape = jax.ShapeDtypeStruct((m, n), out_dtype)
    if dep is not None:
        in_specs.append(HBM)
        args.append(dep)
    return pl.pallas_call(
        body, name=name, out_shape=out_shape, grid=(n // bn, m // bm, nk),
        in_specs=in_specs, out_specs=pl.BlockSpec((bm, bn), out_map),
        scratch_shapes=[pltpu.VMEM((bm, bn), F32)] if nk > 1 else [],
        input_output_aliases=aliases,
        compiler_params=_cparams("parallel", "parallel", "arbitrary"),
    )(*args)


def _silu_rows(c, c_ctx):
    d = c.shape[-1]

    def body(c_ref, cc_ref, o_ref):
        o_ref[...] = jnp.zeros_like(o_ref)
        o_ref[0:1, :] = jax.nn.silu(c_ref[...])
        o_ref[1:2, :] = jax.nn.silu(cc_ref[...])

    return pl.pallas_call(body, name="silu_rows", out_shape=jax.ShapeDtypeStruct((8, d), F32))(
        c.reshape(1, d), c_ctx.reshape(1, d))


def _small_dot(a, b, mode, name):
    dims = {"nn": (((1,), (0,)), ((), ())), "nt": _NT, "tn": _TN}[mode]
    m = a.shape[1] if mode == "tn" else a.shape[0]
    n = b.shape[0] if mode == "nt" else b.shape[1]

    def body(a_ref, b_ref, o_ref):
        o_ref[...] = lax.dot_general(a_ref[...], b_ref[...], dims, preferred_element_type=F32,
                                     precision=lax.Precision.HIGHEST)

    return pl.pallas_call(body, name=name, out_shape=jax.ShapeDtypeStruct((m, n), F32),
                          compiler_params=_cparams())(a, b)


def _ln_stats(x):
    mu = jnp.mean(x, axis=-1, keepdims=True)
    xc = x - mu
    var = jnp.mean(xc * xc, axis=-1, keepdims=True)
    rstd = lax.rsqrt(var + LN_EPS)
    return xc * rstd, rstd


def _ln_mod(x, shift, scale, name):
    l, d = x.shape
    tl = _tile(l, 512)

    def body(x_ref, sh_ref, sc_ref, o_ref):
        xhat, _ = _ln_stats(x_ref[...])
        o_ref[...] = (xhat * (1.0 + sc_ref[...]) + sh_ref[...]).astype(o_ref.dtype)

    row = pl.BlockSpec((tl, d), lambda i: (i, 0))
    vec = pl.BlockSpec((1, d), lambda i: (0, 0))
    return pl.pallas_call(body, name=name, out_shape=jax.ShapeDtypeStruct((l, d), MXU_DTYPE), grid=(l // tl,),
                          in_specs=[row, vec, vec], out_specs=row, compiler_params=_cparams("parallel"))(x, shift, scale)


def _ln_mod_bwd(x, dxm, scale, res, name):
    l, d = x.shape
    tl = _tile(l, 512)
    with_res = res is not None

    def body(*refs):
        if with_res:
            x_ref, g_ref, sc_ref, r_ref, dx_ref, dsh_ref, dsc_ref = refs
        else:
            x_ref, g_ref, sc_ref, dx_ref, dsh_ref, dsc_ref = refs
        i = pl.program_id(0)
        xhat, rstd = _ln_stats(x_ref[...])
        g = g_ref[...].astype(F32)
        dxh = g * (1.0 + sc_ref[...])
        dx = rstd * (dxh - jnp.mean(dxh, axis=-1, keepdims=True) - xhat * jnp.mean(dxh * xhat, axis=-1, keepdims=True))
        dx_ref[...] = dx + r_ref[...].astype(F32) if with_res else dx

        @pl.when(i == 0)
        def _():
            dsh_ref[...] = jnp.zeros_like(dsh_ref)
            dsc_ref[...] = jnp.zeros_like(dsc_ref)

        dsh_ref[...] += jnp.sum(g, axis=0, keepdims=True)
        dsc_ref[...] += jnp.sum(g * xhat, axis=0, keepdims=True)

    row = pl.BlockSpec((tl, d), lambda i: (i, 0))
    vec = pl.BlockSpec((1, d), lambda i: (0, 0))
    args = [x, dxm, scale] + ([res] if with_res else [])
    return pl.pallas_call(
        body, name=name,
        out_shape=(jax.ShapeDtypeStruct((l, d), F32), jax.ShapeDtypeStruct((1, d), F32), jax.ShapeDtypeStruct((1, d), F32)),
        grid=(l // tl,), in_specs=[row, row, vec] + ([row] if with_res else []), out_specs=(row, vec, vec),
        compiler_params=_cparams("arbitrary"))(*args)


def _post_ln_loss(x, out, gate, ln_g, ln_b, target):
    l, d = x.shape
    tl = _tile(l, 512)

    def body(x_ref, o_ref, gate_ref, g_ref, b_ref, t_ref, loss_ref, dout_ref, dxr_ref, dgate_ref, dg_ref, db_ref):
        i = pl.program_id(0)
        out_t = o_ref[...]
        gate_v = gate_ref[...]
        rhat, rstd = _ln_stats(ALPHA * x_ref[...] + gate_v * out_t)
        ln_gv = g_ref[...]
        diff = rhat * ln_gv + b_ref[...] - t_ref[...]
        dy = diff * (1.0 / d)
        drh = dy * ln_gv
        dr = rstd * (drh - jnp.mean(drh, axis=-1, keepdims=True) - rhat * jnp.mean(drh * rhat, axis=-1, keepdims=True))
        dout_ref[...] = (gate_v * dr).astype(dout_ref.dtype)
        dxr_ref[...] = (ALPHA * dr).astype(dxr_ref.dtype)

        @pl.when(i == 0)
        def _():
            for r in (loss_ref, dgate_ref, dg_ref, db_ref):
                r[...] = jnp.zeros_like(r)

        loss_ref[...] += jnp.sum(diff * diff, axis=0, keepdims=True)
        dgate_ref[...] += jnp.sum(dr * out_t, axis=0, keepdims=True)
        dg_ref[...] += jnp.sum(dy * rhat, axis=0, keepdims=True)
        db_ref[...] += jnp.sum(dy, axis=0, keepdims=True)

    row = pl.BlockSpec((tl, d), lambda i: (i, 0))
    vec = pl.BlockSpec((1, d), lambda i: (0, 0))
    v = jax.ShapeDtypeStruct((1, d), F32)
    return pl.pallas_call(
        body, name="post_ln_loss",
        out_shape=(v, jax.ShapeDtypeStruct((l, d), MXU_DTYPE), jax.ShapeDtypeStruct((l, d), MXU_DTYPE), v, v, v),
        grid=(l // tl,), in_specs=[row, row, vec, vec, vec, row], out_specs=(vec, row, row, vec, vec, vec),
        compiler_params=_cparams("arbitrary"))(x, out, gate, ln_g, ln_b, target)


def _ga_forward_tile(p, g, b, ws_ref, bsf, w, nc, nh):
    u_raw, v_raw, za = p[:, :w], p[:, w:2 * w], p[:, 2 * w:3 * w]
    gu = _gelu(u_raw)
    vhat, rstd = _ln_stats(_gelu(v_raw))
    vn = vhat * g + b
    rows = []
    for ci in range(nc):
        r0 = ci * CHUNK
        heads = [_mxu_dot(ws_ref[h], vn[r0:r0 + CHUNK, h * HEAD_DIM_A:(h + 1) * HEAD_DIM_A]) for h in range(nh)]
        rows.append(jnp.concatenate(heads, axis=1) + bsf)
    mixed = jnp.concatenate(rows, axis=0) if nc > 1 else rows[0]
    return u_raw, v_raw, za, gu, vhat, rstd, vn, mixed


def _ga_fwd(proj, g, b, ws, bsf, w):
    l = proj.shape[0]
    nh = w // HEAD_DIM_A
    nc = _tile(l // CHUNK, 2)
    tl = nc * CHUNK

    def body(p_ref, g_ref, b_ref, ws_ref, bsf_ref, o_ref):
        _, _, za, gu, _, _, _, mixed = _ga_forward_tile(p_ref[...], g_ref[...], b_ref[...], ws_ref, bsf_ref[...], w, nc, nh)
        o_ref[...] = (gu * mixed * jax.nn.silu(za)).astype(o_ref.dtype)

    vec = pl.BlockSpec((1, w), lambda i: (0, 0))
    return pl.pallas_call(
        body, name="ga_fwd", out_shape=jax.ShapeDtypeStruct((l, 2 * w), MXU_DTYPE), grid=(l // tl,),
        in_specs=[pl.BlockSpec((tl, 3 * w), lambda i: (i, 0)), vec, vec,
                  pl.BlockSpec((nh, CHUNK, CHUNK), lambda i: (0, 0, 0)), pl.BlockSpec((CHUNK, w), lambda i: (0, 0))],
        out_specs=pl.BlockSpec((tl, w), lambda i: (i, 0)), compiler_params=_cparams("parallel"))(proj, g, b, ws, bsf)


def _ga_bwd(proj, dcat, dproj, g, b, ws, bsf, dys, du0, du1, d_skip, w):
    l = proj.shape[0]
    nh = w // HEAD_DIM_A
    nc = _tile(l // CHUNK, 2)
    tl = nc * CHUNK

    def body(p_ref, dy_ref, dp_in, g_ref, b_ref, ws_ref, bsf_ref, dys_ref, du0_ref, du1_ref, d_ref,
             dp_ref, dg_ref, db_ref, dws_ref, dbsf_ref):
        del dp_in
        i = pl.program_id(0)
        dp_ref[:, 3 * w:] = (dys_ref[...] * d_ref[...] + du0_ref[...] + du1_ref[...]).astype(dp_ref.dtype)
        gv = g_ref[...]
        u_raw, v_raw, za, gu, vhat, rstd, vn, mixed = _ga_forward_tile(
            p_ref[...], gv, b_ref[...], ws_ref, bsf_ref[...], w, nc, nh)
        dya = dy_ref[...].astype(F32)
        sz = jax.nn.silu(za)
        dmixed = dya * gu * sz
        dza = dya * gu * mixed * _silu_grad(za)
        dgu = dya * mixed * sz

        @pl.when(i == 0)
        def _():
            for r in (dg_ref, db_ref, dws_ref, dbsf_ref):
                r[...] = jnp.zeros_like(r)

        rows = []
        for ci in range(nc):
            r0 = ci * CHUNK
            heads = []
            for h in range(nh):
                cols = slice(h * HEAD_DIM_A, (h + 1) * HEAD_DIM_A)
                dm = dmixed[r0:r0 + CHUNK, cols]
                heads.append(_mxu_dot(ws_ref[h], dm, _TN))
                dws_ref[h] += _mxu_dot(dm, vn[r0:r0 + CHUNK, cols], _NT)
            rows.append(jnp.concatenate(heads, axis=1))
            dbsf_ref[...] += dmixed[r0:r0 + CHUNK, :]
        dvn = jnp.concatenate(rows, axis=0) if nc > 1 else rows[0]
        dg_ref[...] += jnp.sum(dvn * vhat, axis=0, keepdims=True)
        db_ref[...] += jnp.sum(dvn, axis=0, keepdims=True)
        dvh = dvn * gv
        dgv = rstd * (dvh - jnp.mean(dvh, axis=-1, keepdims=True) - vhat * jnp.mean(dvh * vhat, axis=-1, keepdims=True))
        dp_ref[:, :w] = (dgu * _gelu_grad(u_raw)).astype(dp_ref.dtype)
        dp_ref[:, w:2 * w] = (dgv * _gelu_grad(v_raw)).astype(dp_ref.dtype)
        dp_ref[:, 2 * w:3 * w] = dza.astype(dp_ref.dtype)

    vec = pl.BlockSpec((1, w), lambda i: (0, 0))
    row = pl.BlockSpec((tl, w), lambda i: (i, 0))
    ws_spec = pl.BlockSpec((nh, CHUNK, CHUNK), lambda i: (0, 0, 0))
    bs_spec = pl.BlockSpec((CHUNK, w), lambda i: (0, 0))
    v = jax.ShapeDtypeStruct((1, w), F32)
    return pl.pallas_call(
        body, name="ga_bwd",
        out_shape=(jax.ShapeDtypeStruct(dproj.shape, dproj.dtype), v, v, jax.ShapeDtypeStruct((nh, CHUNK, CHUNK), F32),
                   jax.ShapeDtypeStruct((CHUNK, w), F32)),
        grid=(l // tl,),
        in_specs=[pl.BlockSpec((tl, 3 * w), lambda i: (i, 0)), row, HBM, vec, vec, ws_spec, bs_spec, row, row, row, vec],
        out_specs=(pl.BlockSpec((tl, 4 * w), lambda i: (i, 0)), vec, vec, ws_spec, bs_spec),
        input_output_aliases={2: 0}, compiler_params=_cparams("arbitrary"))(
            proj, dcat, dproj, g, b, ws, bsf, dys, du0, du1, d_skip)


def _lane_group_sum(x, expand, name):
    return _small_dot(x, expand, "nn", name)


def _disc_math(lr, li, ls, br, bi):
    step = jnp.exp(ls)
    dr, di = lr * step, li * step
    mag = jnp.exp(dr)
    ab_re, ab_im = mag * jnp.cos(di), mag * jnp.sin(di)
    den = lr * lr + li * li
    nr, ni = ab_re - 1.0, ab_im
    f_re = (nr * lr + ni * li) / den
    f_im = (ni * lr - nr * li) / den
    bb_re = f_re * br - f_im * bi
    bb_im = f_re * bi + f_im * br
    return ab_re, ab_im, bb_re, bb_im


def _disc_fwd(lr, li, ls, br, bi):
    def body(lr_ref, li_ref, ls_ref, br_ref, bi_ref, o1, o2, o3, o4):
        res = _disc_math(lr_ref[...], li_ref[...], ls_ref[...], br_ref[...], bi_ref[...])
        for o, r in zip((o1, o2, o3, o4), res):
            o[...] = r

    s = lambda a: jax.ShapeDtypeStruct(a.shape, F32)
    return pl.pallas_call(body, name="s5_disc", out_shape=(s(lr), s(lr), s(br), s(br)), compiler_params=_cparams())(
        lr, li, ls, br, bi)


def _disc_bwd(lr, li, ls, br, bi, d_ar, d_ai, d_br, d_bi):
    def body(lr_ref, li_ref, ls_ref, br_ref, bi_ref, c1, c2, c3, c4, o1, o2, o3, o4, o5):
        _, vjp = jax.vjp(_disc_math, lr_ref[...], li_ref[...], ls_ref[...], br_ref[...], bi_ref[...])
        res = vjp((c1[...], c2[...], c3[...], c4[...]))
        for o, r in zip((o1, o2, o3, o4, o5), res):
            o[...] = r

    s = lambda a: jax.ShapeDtypeStruct(a.shape, F32)
    return pl.pallas_call(body, name="s5_disc_bwd", out_shape=(s(lr), s(lr), s(ls), s(br), s(br)),
                          compiler_params=_cparams())(lr, li, ls, br, bi, d_ar, d_ai, d_br, d_bi)


def _dir_spec(a, dr):
    return pl.BlockSpec((None,) + a.shape[1:], lambda i: (dr,) + (0,) * (a.ndim - 1))


def _s5_fwd(u_arr, u_col, w, h0, a_sm, wbr, wbi, cre, ncim, dr, name):
    rev = dr == 1
    l = u_arr.shape[0]
    nb = w // LANES
    spb = wbr.shape[-1]
    nsr = a_sm.shape[2]
    assert 2 * spb == 8 * LANES and nb % 2 == 0
    npair = nb // 2
    t = _tile(l, 256)
    n = l // t
    tile = (lambda i: n - 1 - i) if rev else (lambda i: i)

    def body(u_ref, h0_ref, a_ref, wbr_ref, wbi_ref, cre_ref, ncim_ref, y_ref, hr_ref, hi_ref, tr_ref, ti_ref, hfin_ref,
             carry_ref):
        i = pl.program_id(0)

        @pl.when(i == 0)
        def _():
            carry_ref[...] = h0_ref[...]

        for j in range(npair):
            for h_ref, w_ref in ((hr_ref, wbr_ref), (hi_ref, wbi_ref)):
                blk = [_mxu_dot(u_ref[:, k * LANES:(k + 1) * LANES], w_ref[k]) for k in (2 * j, 2 * j + 1)]
                h_ref[j] = jnp.concatenate(blk, axis=1).reshape(t, 8, LANES)
        slab = lambda ref, part, j: ref[part, 8 * j:8 * j + 8, :]
        ar = [slab(a_ref, 0, j) for j in range(npair)]
        ai = [slab(a_ref, 1, j) for j in range(npair)]

        def steps(blk, c):
            hr, hi = list(c[:npair]), list(c[npair:])
            for q in range(SCAN_UNROLL):
                s = blk * SCAN_UNROLL + q
                row = t - 1 - s if rev else s
                for j in range(npair):
                    hr[j], hi[j] = (ar[j] * hr[j] - ai[j] * hi[j] + hr_ref[j, row],
                                    ar[j] * hi[j] + ai[j] * hr[j] + hi_ref[j, row])
                    hr_ref[j, row] = hr[j]
                    hi_ref[j, row] = hi[j]
            return tuple(hr + hi)

        init = tuple(slab(carry_ref, part, j) for part in range(2) for j in range(npair))
        c = lax.fori_loop(0, t // SCAN_UNROLL, steps, init)
        for part in range(2):
            for j in range(npair):
                carry_ref[part, 8 * j:8 * j + 8, :] = c[part * npair + j]
                hfin_ref[part, 8 * j:8 * j + 8, :] = c[part * npair + j]
        for j in range(npair):
            cols8 = slice(j * 8 * LANES, (j + 1) * 8 * LANES)
            tr_ref[:, cols8] = hr_ref[j].reshape(t, 8 * LANES).astype(tr_ref.dtype)
            ti_ref[:, cols8] = hi_ref[j].reshape(t, 8 * LANES).astype(ti_ref.dtype)
        for k in range(nb):
            cols = slice(k * spb, (k + 1) * spb)
            y_ref[:, k * LANES:(k + 1) * LANES] = (_mxu_dot(tr_ref[:, cols], cre_ref[k]) + _mxu_dot(ti_ref[:, cols], ncim_ref[k]))

    full = lambda a: pl.BlockSpec(a.shape, lambda i: (0,) * a.ndim)
    hspec = pl.BlockSpec((npair, t, 8, LANES), lambda i: (0, tile(i), 0, 0))
    tspec = pl.BlockSpec((t, nsr * LANES), lambda i: (tile(i), 0))
    hsh = jax.ShapeDtypeStruct((npair, l, 8, LANES), F32)
    tsh = jax.ShapeDtypeStruct((l, nsr * LANES), MXU_DTYPE)
    return pl.pallas_call(
        body, name=name,
        out_shape=(jax.ShapeDtypeStruct((l, w), F32), hsh, hsh, tsh, tsh, jax.ShapeDtypeStruct((2, nsr, LANES), F32)),
        grid=(n,),
        in_specs=[pl.BlockSpec((t, w), lambda i: (tile(i), u_col)), full(h0)] + [_dir_spec(a, dr) for a in (a_sm, wbr, wbi, cre, ncim)],
        out_specs=(pl.BlockSpec((t, w), lambda i: (tile(i), 0)), hspec, hspec, tspec, tspec,
                   pl.BlockSpec((2, nsr, LANES), lambda i: (0, 0, 0))),
        scratch_shapes=[pltpu.VMEM((2, nsr, LANES), F32)],
        compiler_params=_cparams("arbitrary"))(u_arr, h0, a_sm, wbr, wbi, cre, ncim)


def _s5_bwd(dys, u_arr, u_col, w, hr, hi, tr, ti, hbound, g_in, a_sm, wbr_t, wbi_t, cre_t, ncim_t, dr, name):
    rev = dr == 1
    l = u_arr.shape[0]
    nb = w // LANES
    spb = wbr_t.shape[-2]
    nsr = a_sm.shape[2]
    npair = nb // 2
    t = _tile(l, 128)
    n = l // t
    with_dy = dys is not None
    tile = (lambda i: i) if rev else (lambda i: n - 1 - i)

    def body(*refs):
        if with_dy:
            (dy_ref, u_ref, hr_ref, hi_ref, pr_ref, pi_ref, tr_ref, ti_ref, hb_ref, gin_ref, a_ref, wbrt_ref, wbit_ref,
             cret_ref, ncimt_ref, du_ref, dwbr_ref, dwbi_ref, dcre_ref, dncim_ref, da_ref, gout_ref,
             gr_ref, gi_ref, gtr_ref, gti_ref, carry_ref) = refs
        else:
            (u_ref, hr_ref, hi_ref, pr_ref, pi_ref, hb_ref, gin_ref, a_ref, wbrt_ref, wbit_ref,
             du_ref, dwbr_ref, dwbi_ref, da_ref, gout_ref, gr_ref, gi_ref, gtr_ref, gti_ref, carry_ref) = refs
        i = pl.program_id(0)

        @pl.when(i == 0)
        def _():
            carry_ref[...] = gin_ref[...]
            accs = (dwbr_ref, dwbi_ref, da_ref) + ((dcre_ref, dncim_ref) if with_dy else ())
            for r in accs:
                r[...] = jnp.zeros_like(r)

        if with_dy:
            for j in range(npair):
                for g_ref, c_ref in ((gr_ref, cret_ref), (gi_ref, ncimt_ref)):
                    blk = [_mxu_dot(dy_ref[:, k * LANES:(k + 1) * LANES], c_ref[k]) for k in (2 * j, 2 * j + 1)]
                    g_ref[j] = jnp.concatenate(blk, axis=1).reshape(t, 8, LANES)
        else:
            gr_ref[...] = jnp.zeros_like(gr_ref)
            gi_ref[...] = jnp.zeros_like(gi_ref)
        slab = lambda ref, part, j: ref[part, 8 * j:8 * j + 8, :]
        last = t - 1 if rev else 0
        first = i == n - 1

        ar = [slab(a_ref, 0, j) for j in range(npair)]
        ai = [slab(a_ref, 1, j) for j in range(npair)]

        def steps(blk, c):
            gr, gi, dr, di = (list(c[q * npair:(q + 1) * npair]) for q in range(4))
            for q in range(SCAN_UNROLL):
                s = blk * SCAN_UNROLL + q
                row = s if rev else t - 1 - s
                prow = jnp.minimum(row + 1, t - 1) if rev else jnp.maximum(row - 1, 0)
                for j in range(npair):
                    pr, pi = hr_ref[j, prow], hi_ref[j, prow]
                    gr[j], gi[j] = (gr_ref[j, row] + ar[j] * gr[j] + ai[j] * gi[j],
                                    gi_ref[j, row] + ar[j] * gi[j] - ai[j] * gr[j])
                    gr_ref[j, row] = gr[j]
                    gi_ref[j, row] = gi[j]
                    dr[j], di[j] = dr[j] + gr[j] * pr + gi[j] * pi, di[j] + gi[j] * pr - gr[j] * pi
            return tuple(gr + gi + dr + di)

        init = tuple(slab(ref, part, j) for ref in (carry_ref, da_ref) for part in range(2) for j in range(npair))
        c = lax.fori_loop(0, t // SCAN_UNROLL, steps, init)
        gr, gi, dr, di = (c[q * npair:(q + 1) * npair] for q in range(4))
        for j in range(npair):
            pr = jnp.where(first, slab(hb_ref, 0, j), pr_ref[j, 0]) - hr_ref[j, last]
            pi = jnp.where(first, slab(hb_ref, 1, j), pi_ref[j, 0]) - hi_ref[j, last]
            rows = slice(8 * j, 8 * j + 8)
            da_ref[0, rows, :] = dr[j] + gr[j] * pr + gi[j] * pi
            da_ref[1, rows, :] = di[j] + gi[j] * pr - gr[j] * pi
            for part, val in enumerate((gr[j], gi[j])):
                carry_ref[part, rows, :] = val
                gout_ref[part, rows, :] = val

        for j in range(npair):
            cols8 = slice(j * 8 * LANES, (j + 1) * 8 * LANES)
            gtr_ref[:, cols8] = gr_ref[j].reshape(t, 8 * LANES).astype(gtr_ref.dtype)
            gti_ref[:, cols8] = gi_ref[j].reshape(t, 8 * LANES).astype(gti_ref.dtype)
        for k in range(nb):
            cols = slice(k * spb, (k + 1) * spb)
            lanes = slice(k * LANES, (k + 1) * LANES)
            du_ref[:, lanes] = _mxu_dot(gtr_ref[:, cols], wbrt_ref[k]) + _mxu_dot(gti_ref[:, cols], wbit_ref[k])
            dwbr_ref[k] += _mxu_dot(u_ref[:, lanes], gtr_ref[:, cols], _TN)
            dwbi_ref[k] += _mxu_dot(u_ref[:, lanes], gti_ref[:, cols], _TN)
            if with_dy:
                dcre_ref[k] += _mxu_dot(tr_ref[:, cols], dy_ref[:, lanes], _TN)
                dncim_ref[k] += _mxu_dot(ti_ref[:, cols], dy_ref[:, lanes], _TN)

    full = lambda a: pl.BlockSpec(a.shape, lambda i: (0,) * a.ndim)
    row = lambda cb: pl.BlockSpec((t, w), lambda i: (tile(i), cb))
    hspec = pl.BlockSpec((npair, t, 8, LANES), lambda i: (0, tile(i), 0, 0))
    if rev:
        pspec = pl.BlockSpec((npair, 1, 8, LANES), lambda i: (0, jnp.minimum((tile(i) + 1) * t, l - 1), 0, 0))
    else:
        pspec = pl.BlockSpec((npair, 1, 8, LANES), lambda i: (0, jnp.maximum(tile(i) * t - 1, 0), 0, 0))
    sm = jax.ShapeDtypeStruct((2, nsr, LANES), F32)
    smspec = pl.BlockSpec((2, nsr, LANES), lambda i: (0, 0, 0))
    wsh = jax.ShapeDtypeStruct((nb, LANES, spb), F32)
    csh = jax.ShapeDtypeStruct((nb, spb, LANES), F32)
    tspec = pl.BlockSpec((t, nsr * LANES), lambda i: (tile(i), 0))
    in_specs = (([row(0)] if with_dy else []) + [row(u_col), hspec, hspec, pspec, pspec] + ([tspec, tspec] if with_dy else [])
                + [full(hbound), full(g_in)]
                + [_dir_spec(a, dr) for a in (a_sm, wbr_t, wbi_t) + ((cre_t, ncim_t) if with_dy else ())])
    args = (([dys] if with_dy else []) + [u_arr, hr, hi, hr, hi] + ([tr, ti] if with_dy else [])
            + [hbound, g_in, a_sm, wbr_t, wbi_t] + ([cre_t, ncim_t] if with_dy else []))
    out_shape = (jax.ShapeDtypeStruct((l, w), F32), wsh, wsh) + ((csh, csh) if with_dy else ()) + (sm, sm)
    out_specs = (row(0), full(wsh), full(wsh)) + ((full(csh), full(csh)) if with_dy else ()) + (smspec, smspec)
    return pl.pallas_call(
        body, name=name, out_shape=out_shape, grid=(n,), in_specs=in_specs, out_specs=out_specs,
        scratch_shapes=[pltpu.VMEM((npair, t, 8, LANES), F32)] * 2 + [pltpu.VMEM((t, nsr * LANES), MXU_DTYPE)] * 2
        + [pltpu.VMEM((2, nsr, LANES), F32)],
        compiler_params=_cparams("arbitrary"))(*args)


def _glu_fwd(y0, y1, proj, cat, d_skip, w_glu, b_glu, w):
    l = y0.shape[0]
    tl = _tile(l, 256)

    def body(y0_ref, y1_ref, u_ref, z_ref, cat_in, d_ref, wg_ref, bg_ref, ys_ref, cat_ref):
        del cat_in
        ys = y0_ref[...] + y1_ref[...] + d_ref[...] * u_ref[...]
        ys_ref[...] = ys
        gy = _gelu(ys)
        s = _mxu_dot(gy, wg_ref[...]) + bg_ref[...]
        cat_ref[...] = (gy * jax.nn.sigmoid(s) * jax.nn.silu(z_ref[...])).astype(cat_ref.dtype)

    row = pl.BlockSpec((tl, w), lambda i: (i, 0))
    vec = pl.BlockSpec((1, w), lambda i: (0, 0))
    return pl.pallas_call(
        body, name="glu_fwd", out_shape=(jax.ShapeDtypeStruct((l, w), F32), jax.ShapeDtypeStruct(cat.shape, cat.dtype)),
        grid=(l // tl,),
        in_specs=[row, row, pl.BlockSpec((tl, w), lambda i: (i, 3)), pl.BlockSpec((tl, w), lambda i: (i, 4)), HBM,
                  vec, pl.BlockSpec((w, w), lambda i: (0, 0)), vec],
        out_specs=(row, pl.BlockSpec((tl, w), lambda i: (i, 1))), input_output_aliases={4: 1},
        compiler_params=_cparams("parallel"))(y0, y1, proj, proj, cat, d_skip, w_glu, b_glu)


def _glu_bwd(dcat, ys, proj, w_glu, b_glu, w, dep):
    l = ys.shape[0]
    tl = _tile(l, 256)

    def body(dy_ref, ys_ref, u_ref, z_ref, wg_ref, bg_ref, dep_ref, dys_ref, dp_ref, dbg_ref, dd_ref, dwg_ref):
        del dep_ref
        i = pl.program_id(0)
        ys_t = ys_ref[...]
        z = z_ref[...]
        dyb = dy_ref[...].astype(F32)
        gy = _gelu(ys_t)
        sg = jax.nn.sigmoid(_mxu_dot(gy, wg_ref[...]) + bg_ref[...])
        dp_ref[...] = (dyb * gy * sg * _silu_grad(z)).astype(dp_ref.dtype)
        dglu = dyb * jax.nn.silu(z)
        ds = dglu * gy * sg * (1.0 - sg)
        dgy = dglu * sg + _mxu_dot(ds, wg_ref[...], _NT)
        dys_t = dgy * _gelu_grad(ys_t)
        dys_ref[...] = dys_t

        @pl.when(i == 0)
        def _():
            for r in (dbg_ref, dd_ref, dwg_ref):
                r[...] = jnp.zeros_like(r)

        dbg_ref[...] += jnp.sum(ds, axis=0, keepdims=True)
        dd_ref[...] += jnp.sum(dys_t * u_ref[...], axis=0, keepdims=True)
        dwg_ref[...] += _mxu_dot(gy, ds, _TN)

    row = pl.BlockSpec((tl, w), lambda i: (i, 0))
    vec = pl.BlockSpec((1, w), lambda i: (0, 0))
    mat = pl.BlockSpec((w, w), lambda i: (0, 0))
    v = jax.ShapeDtypeStruct((1, w), F32)
    return pl.pallas_call(
        body, name="glu_bwd",
        out_shape=(jax.ShapeDtypeStruct((l, w), F32), jax.ShapeDtypeStruct((l, 5 * w), MXU_DTYPE), v, v,
                   jax.ShapeDtypeStruct((w, w), F32)),
        grid=(l // tl,),
        in_specs=[pl.BlockSpec((tl, w), lambda i: (i, 1)), row, pl.BlockSpec((tl, w), lambda i: (i, 3)),
                  pl.BlockSpec((tl, w), lambda i: (i, 4)), mat, vec, HBM],
        out_specs=(row, pl.BlockSpec((tl, w), lambda i: (i, 4)), vec, vec, mat),
        compiler_params=_cparams("arbitrary"))(dcat, ys, proj, proj, w_glu, b_glu, dep)


def _add2(a, b, name):
    l, w = a.shape
    tl = _tile(l, 512)

    def body(a_ref, b_ref, o_ref):
        o_ref[...] = a_ref[...] + b_ref[...]

    row = pl.BlockSpec((tl, w), lambda i: (i, 0))
    return pl.pallas_call(body, name=name, out_shape=jax.ShapeDtypeStruct((l, w), F32), grid=(l // tl,),
                          in_specs=[row, row], out_specs=row, compiler_params=_cparams("parallel"))(a, b)


def _adamw_nd(w, m, v, g, name):
    shape = w.shape
    lead = math.prod(shape[:-2]) if len(shape) > 2 else 1
    b, c = (shape[-2], shape[-1]) if len(shape) >= 2 else (1, shape[-1])
    t3 = (lead, b, c)
    padded_row = -(-b // 8) * 8 * -(-c // LANES) * LANES * 4
    ta = _tile(lead, max(1, (2 << 20) // padded_row))

    def body(w_ref, m_ref, v_ref, g_ref, d_ref, mo_ref, vo_ref):
        gv = g_ref[...]
        mn = ADAM_B1 * m_ref[...] + (1.0 - ADAM_B1) * gv
        vn = ADAM_B2 * v_ref[...] + (1.0 - ADAM_B2) * (gv * gv)
        m_hat = mn / (1.0 - ADAM_B1 ** ADAM_STEP)
        v_hat = vn / (1.0 - ADAM_B2 ** ADAM_STEP)
        d_ref[...] = -ADAM_LR * (m_hat / (jnp.sqrt(v_hat) + ADAM_EPS) + ADAM_WD * w_ref[...])
        mo_ref[...] = mn
        vo_ref[...] = vn

    blk = pl.BlockSpec((ta, b, c), lambda i: (i, 0, 0))
    s = jax.ShapeDtypeStruct(t3, F32)
    outs = pl.pallas_call(body, name=name, out_shape=(s, s, s), grid=(lead // ta,), in_specs=[blk] * 4, out_specs=(blk,) * 3,
                          compiler_params=_cparams("parallel"))(*[a.reshape(t3) for a in (w, m, v, g)])
    return tuple(o.reshape(shape) for o in outs)


def _adamw(w, m, v, gparts, name, dep=None):
    r, c = w.shape
    np_ = gparts.shape[0]
    tr = _tile(r, max(8, (1 << 18) // c), 8)

    def body(w_ref, m_ref, v_ref, g_ref, *rest):
        go_ref, d_ref, mo_ref, vo_ref = rest[-4:]
        g = g_ref[0].astype(F32)
        for p in range(1, np_):
            g = g + g_ref[p].astype(F32)
        mn = ADAM_B1 * m_ref[...] + (1.0 - ADAM_B1) * g
        vn = ADAM_B2 * v_ref[...] + (1.0 - ADAM_B2) * (g * g)
        m_hat = mn / (1.0 - ADAM_B1 ** ADAM_STEP)
        v_hat = vn / (1.0 - ADAM_B2 ** ADAM_STEP)
        go_ref[...] = g
        d_ref[...] = -ADAM_LR * (m_hat / (jnp.sqrt(v_hat) + ADAM_EPS) + ADAM_WD * w_ref[...])
        mo_ref[...] = mn
        vo_ref[...] = vn

    row = pl.BlockSpec((tr, c), lambda i: (i, 0))
    s = jax.ShapeDtypeStruct((r, c), F32)
    extra = [] if dep is None else [dep]
    return pl.pallas_call(body, name=name, out_shape=(s, s, s, s), grid=(r // tr,),
                          in_specs=[row, row, row, pl.BlockSpec((np_, tr, c), lambda i: (0, i, 0))] + [HBM] * len(extra),
                          out_specs=(row, row, row, row), compiler_params=_cparams("parallel"))(w, m, v, gparts, *extra)


def _sum_slots(parts, name):
    np_, r, c = parts.shape

    def body(p_ref, o_ref):
        g = p_ref[0]
        for p in range(1, np_):
            g = g + p_ref[p]
        o_ref[...] = g

    return pl.pallas_call(body, name=name, out_shape=jax.ShapeDtypeStruct((r, c), F32), compiler_params=_cparams())(parts)


def _block_diag(x, gb):
    nd, g, a, b = x.shape
    eye = jnp.eye(gb, dtype=x.dtype)
    y = jnp.einsum("dkgab,gh->dkgahb", x.reshape(nd, g // gb, gb, a, b), eye)
    return y.reshape(nd, g // gb, gb * a, gb * b)


def _block_diag_extract(y, gb, a, b):
    nd, nbk = y.shape[:2]
    eye = jnp.eye(gb, dtype=y.dtype)
    x = jnp.einsum("dkgahb,gh->dkgab", y.reshape(nd, nbk, gb, a, gb, b), eye)
    return x.reshape(nd, nbk * gb, a, b)


def kernel(x, c, ctx, c_ctx, w_ada, b_ada, w_in, sgu_ln_g, sgu_ln_b, w_spatial, b_spatial, s5_lam_re, s5_lam_im, s5_log_step, s5_b_re, s5_b_im, s5_c_re, s5_c_im, s5_d, w_glu, b_glu, w_out, ln_g, ln_b, loss_target, m_c_ctx, m_w_ada, m_b_ada, m_w_in, m_sgu_ln_g, m_sgu_ln_b, m_w_spatial, m_b_spatial, m_s5_lam_re, m_s5_lam_im, m_s5_log_step, m_s5_b_re, m_s5_b_im, m_s5_c_re, m_s5_c_im, m_s5_d, m_w_glu, m_b_glu, m_w_out, m_ln_g, m_ln_b, v_c_ctx, v_w_ada, v_b_ada, v_w_in, v_sgu_ln_g, v_sgu_ln_b, v_w_spatial, v_b_spatial, v_s5_lam_re, v_s5_lam_im, v_s5_log_step, v_s5_b_re, v_s5_b_im, v_s5_c_re, v_s5_c_im, v_s5_d, v_w_glu, v_b_glu, v_w_out, v_ln_g, v_ln_b):
    small_names = ["c_ctx", "b_ada", "sgu_ln_g", "sgu_ln_b", "w_spatial", "b_spatial", "s5_lam_re", "s5_lam_im",
                   "s5_log_step", "s5_b_re", "s5_b_im", "s5_c_re", "s5_c_im", "s5_d", "b_glu", "ln_g", "ln_b"]
    env = dict(locals())
    x2, tgt, ctx2 = x[0], loss_target[0], ctx[0]
    l, d = x2.shape
    lc = ctx2.shape[0]
    w = d // 2
    nh = w // HEAD_DIM_A
    nd, g_s5, p_s5, c_s5 = s5_b_re.shape[1:]
    ns = g_s5 * p_s5
    nsr = ns // LANES
    gb = LANES // c_s5
    me = _index(_mesh_pos())
    ada_cols = w_ada.shape[2]

    srows = _silu_rows(c, c_ctx)
    srows_all = _all_gather(srows, 0, "gather_silu")
    s_mat = jnp.concatenate([srows_all[0::8], srows_all[1:2], jnp.zeros((7, d), F32)], axis=0)
    mod_part = _small_dot(s_mat, w_ada[0], "nn", "mod_cols")
    mod_all = _all_gather(mod_part, 1, "gather_mod") + b_ada
    hw_in, tok_a = _exchange_start(w_in[0].astype(MXU_DTYPE), 1, "gather", "start_gather_w_in", (SIBLING,) + SAME_CORE_PEERS)
    mod_all = mod_all + tok_a[0, 0]
    mod_x = lax.dynamic_slice_in_dim(mod_all, me, 1, axis=0)
    mod_c = mod_all[8:9]
    shift_x, scale_x, gate_x = mod_x[:, :d], mod_x[:, d:2 * d], mod_x[:, 2 * d:]
    shift_c, scale_c = mod_c[:, :d], mod_c[:, d:2 * d]

    lr, li = s5_lam_re[0][:, :, None, :], s5_lam_im[0][:, :, None, :]
    ls = s5_log_step[0][:, :, None, None]
    swapped = ("s5_b_re", "s5_b_im")
    for nm in swapped:
        for pre in ("", "m_", "v_"):
            env[pre + nm] = jnp.swapaxes(env[pre + nm], -1, -2)
    br_t, bi_t = env["s5_b_re"][0], env["s5_b_im"][0]
    ab_re, ab_im, bb_re, bb_im = _disc_fwd(lr, li, ls, br_t, bi_t)
    a_sm = jnp.stack([ab_re, ab_im], axis=1).reshape(nd, 2, nsr, LANES)
    wbr = _block_diag(bb_re.astype(MXU_DTYPE), gb)
    wbi = _block_diag(bb_im.astype(MXU_DTYPE), gb)
    cre_t = _block_diag(s5_c_re[0].astype(MXU_DTYPE), gb)
    ncim_t = _block_diag((-s5_c_im[0]).astype(MXU_DTYPE), gb)
    cre, ncim = jnp.swapaxes(cre_t, 2, 3), jnp.swapaxes(ncim_t, 2, 3)
    wbr_t, wbi_t = jnp.swapaxes(wbr, 2, 3), jnp.swapaxes(wbi, 2, 3)
    d_skip = s5_d

    xm = _ln_mod(x2, shift_x, scale_x, "ln_mod_x")
    cm = _ln_mod(ctx2, shift_c, scale_c, "ln_mod_ctx")
    ready = xm[:8, :LANES].astype(F32) + cm[:8, :LANES].astype(F32) + cre[0, 0, :8, :].astype(F32)
    hw_in2, tok_b = _forward_start(_exchange_wait(hw_in, ready, "wait_gather_w_in"), 1, "start_forward_w_in")
    w_in_f = _forward_wait(hw_in2, tok_b, "wait_forward_w_in")
    hw_glu, tok_c = _exchange_start(w_glu[0].astype(MXU_DTYPE), 0, "gather", "start_gather_w_glu")
    hw_out, tok_o = _exchange_start(w_out[0].astype(MXU_DTYPE), 0, "gather", "start_gather_w_out")
    proj = _matmul(xm, w_in_f, mode="nn", name="proj", dep=tok_c + tok_o)
    ub_c = _matmul(cm, w_in_f, mode="nn", name="proj_ctx", b_n0=3 * w, n=w)
    bsf = jnp.repeat(b_spatial[0].T, HEAD_DIM_A, axis=1)
    ws = w_spatial[0]
    cat = _ga_fwd(proj, sgu_ln_g, sgu_ln_b, ws, bsf, w)
    zeros_state = jnp.zeros((2, nsr, LANES), F32)
    s5c, s5l = [], []
    for dr in range(nd):
        s5c.append(_s5_fwd(ub_c, 0, w, zeros_state, a_sm, wbr, wbi, cre, ncim, dr, f"s5_fwd_ctx{dr}"))
        s5l.append(_s5_fwd(proj, 3, w, s5c[dr][5], a_sm, wbr, wbi, cre, ncim, dr, f"s5_fwd{dr}"))
    w_glu_f = _exchange_wait(hw_glu, s5l[1][0], "wait_gather_w_glu")
    ys, cat = _glu_fwd(s5l[0][0], s5l[1][0], proj, cat, d_skip, w_glu_f, b_glu, w)
    w_out_f = _exchange_wait(hw_out, ys, "wait_gather_w_out")
    out = _matmul(cat, w_out_f, mode="nn", name="out_proj")
    loss_row, dout, dx_res, dgate, dln_g, dln_b = _post_ln_loss(x2, out, gate_x, ln_g, ln_b, tgt)

    dcat = _matmul(dout, w_out_f, mode="nt", name="d_cat", out_dtype=MXU_DTYPE)
    dw_out = _matmul(cat, dout, mode="tn", name="d_w_out", out_dtype=MXU_DTYPE)
    hg_out, tok_d = _exchange_start(dw_out, 0, "a2a", "start_a2a_d_w_out")
    dys, dproj, db_glu, dd_skip, dw_glu = _glu_bwd(dcat, ys, proj, w_glu_f, b_glu, w, tok_d)
    hg_glu, tok_e = _exchange_start(dw_glu.astype(MXU_DTYPE), 0, "a2a", "start_a2a_d_w_glu")
    zeros_state = zeros_state + tok_e[0, 0]
    du_l, du_c, dwbr, dwbi, dcre, dncim, da_sm = [], [], [], [], [], [], []
    nbk, spb = w // LANES, gb * p_s5
    for dr in range(nd):
        bl = _s5_bwd(dys, proj, 3, w, *s5l[dr][1:5], s5c[dr][5], zeros_state, a_sm, wbr_t, wbi_t,
                     cre_t, ncim_t, dr, f"s5_bwd{dr}")
        bc = _s5_bwd(None, ub_c, 0, w, s5c[dr][1], s5c[dr][2], None, None, zeros_state, bl[6], a_sm, wbr_t, wbi_t,
                     None, None, dr, f"s5_bwd_ctx{dr}")
        du_l.append(bl[0])
        du_c.append(bc[0])
        dwbr.append(_add2(bl[1].reshape(nbk * LANES, spb), bc[1].reshape(nbk * LANES, spb), f"sum_dwbr{dr}"))
        dwbi.append(_add2(bl[2].reshape(nbk * LANES, spb), bc[2].reshape(nbk * LANES, spb), f"sum_dwbi{dr}"))
        dcre.append(bl[3])
        dncim.append(bl[4])
        da_sm.append(_add2(bl[5].reshape(2 * nsr, LANES), bc[3].reshape(2 * nsr, LANES), f"sum_da{dr}"))
    dub_c = _add2(du_c[0], du_c[1], "dub_ctx")
    dwbr = jnp.stack(dwbr).reshape(nd, nbk, LANES, spb)
    dwbi = jnp.stack(dwbi).reshape(nd, nbk, LANES, spb)
    dcre, dncim = jnp.stack(dcre), jnp.stack(dncim)
    da_sm = jnp.stack(da_sm).reshape(nd, 2, g_s5, p_s5)
    dproj, dsg, dsb, dws, dbsf = _ga_bwd(proj, dcat, dproj, sgu_ln_g, sgu_ln_b, ws, bsf, dys, du_l[0], du_l[1], d_skip, w)

    dbb_re = _block_diag_extract(dwbr, gb, c_s5, p_s5)
    dbb_im = _block_diag_extract(dwbi, gb, c_s5, p_s5)
    dc_re = jnp.swapaxes(_block_diag_extract(dcre, gb, p_s5, c_s5), 2, 3)
    dc_im = -jnp.swapaxes(_block_diag_extract(dncim, gb, p_s5, c_s5), 2, 3)
    dlr, dli, dls, db_re, db_im = _disc_bwd(lr, li, ls, br_t, bi_t, da_sm[:, 0:1].reshape(nd, g_s5, 1, p_s5),
                                            da_sm[:, 1:2].reshape(nd, g_s5, 1, p_s5), dbb_re, dbb_im)
    expand = (jnp.arange(w)[:, None] // HEAD_DIM_A == jnp.arange(LANES)[None, :]).astype(F32)
    db_sp = _lane_group_sum(dbsf, expand, "d_b_spatial")[:, :nh].T

    local = {"sgu_ln_g": dsg, "sgu_ln_b": dsb, "w_spatial": dws, "b_spatial": db_sp,
             "s5_lam_re": dlr, "s5_lam_im": dli, "s5_log_step": dls, "s5_b_re": db_re, "s5_b_im": db_im,
             "s5_c_re": dc_re, "s5_c_im": dc_im, "s5_d": dd_skip, "b_glu": db_glu, "ln_g": dln_g, "ln_b": dln_b}
    reduced = sorted(local, key=lambda n: -math.prod(env[n].shape))
    loss_part = (0.5 / d) * jnp.sum(loss_row)
    flat = jnp.concatenate([local[n].reshape(-1) for n in reduced] + [loss_part.reshape(1)])
    unit = N_DEV * 8 * LANES
    total = -(-flat.shape[0] // unit) * unit
    flat = jnp.pad(flat, (0, total - flat.shape[0])).reshape(N_DEV * 8, total // (N_DEV * 8))
    h_small, tok_s = _exchange_start(flat, 0, "a2a", "start_a2a_small")

    dw_in = _matmul(xm, dproj, mode="tn", name="d_w_in", out_dtype=MXU_DTYPE, dep=tok_s)
    dw_in = _matmul(cm, dub_c, mode="tn", name="d_w_in_ctx", acc_in=dw_in, acc_n0=3 * w, out_dtype=MXU_DTYPE)
    hg_in, tok_f = _exchange_start(dw_in, 1, "a2a", "start_a2a_d_w_in")
    mine = _sum_slots(_exchange_wait(h_small, dw_in, "wait_a2a_small"), "sum_small")
    h_sums, tok_g = _exchange_start(mine, 0, "gather", "start_gather_small")
    dxm = _matmul(dproj, w_in_f, mode="nt", name="d_xm", dep=tok_f + tok_g, out_dtype=MXU_DTYPE)
    dcm = _matmul(dub_c, w_in_f, mode="nt", name="d_cm", b_k0=3 * w, k=w)
    grad_x, dshift_x, dscale_x = _ln_mod_bwd(x2, dxm, scale_x, dx_res, "ln_mod_x_bwd")
    _, dshift_c, dscale_c = _ln_mod_bwd(ctx2, dcm, scale_c, None, "ln_mod_ctx_bwd")

    dmod_rows = jnp.concatenate([jnp.concatenate([dshift_x, dscale_x, dgate], axis=1),
                                 jnp.concatenate([dshift_c, dscale_c, jnp.zeros((1, d), F32)], axis=1),
                                 jnp.zeros((6, 3 * d), F32)], axis=0)
    h_dmod, tok_m = _exchange_start(dmod_rows, 0, "gather", "start_gather_dmod")

    gp_w_out = _exchange_wait(hg_out, tok_m, "wait_a2a_d_w_out")
    gp_w_glu = _exchange_wait(hg_glu, tok_m, "wait_a2a_d_w_glu")
    gp_w_in = _exchange_wait(hg_in, tok_m, "wait_a2a_d_w_in")
    big = {
        "w_in": _adamw(w_in[0], m_w_in[0], v_w_in[0], gp_w_in, "adamw_w_in"),
        "w_glu": _adamw(w_glu[0], m_w_glu[0], v_w_glu[0], gp_w_glu, "adamw_w_glu"),
        "w_out": _adamw(w_out[0], m_w_out[0], v_w_out[0], gp_w_out, "adamw_w_out"),
    }
    dmod_all = _exchange_wait(h_dmod, big["w_out"][0], "wait_gather_dmod")
    dmod_ctx = _sum_slots(dmod_all[1::8].reshape(N_DEV, 1, 3 * d), "sum_dmod_ctx")
    dmod_mat = jnp.concatenate([dmod_all[0::8], dmod_ctx, jnp.zeros((7, 3 * d), F32)], axis=0)
    db_ada = _sum_slots(dmod_mat[:9].reshape(9, 1, 3 * d), "sum_db_ada")
    dmod_mine = lax.dynamic_slice_in_dim(dmod_mat, me * ada_cols, ada_cols, axis=1)
    dw_ada = _small_dot(s_mat, dmod_mine, "tn", "d_w_ada")
    dsilu_cc = _small_dot(dmod_mine[8:16], w_ada[0], "nt", "d_silu_cctx")[0:1]
    dc_ctx_part = dsilu_cc * _silu_grad(c_ctx.reshape(1, d))
    dc_ctx_rows = jnp.concatenate([dc_ctx_part, jnp.zeros((7, d), F32)], axis=0)
    h_cctx, tok_c2 = _exchange_start(dc_ctx_rows, 0, "gather", "start_gather_d_c_ctx")

    big["w_ada"] = _adamw(w_ada[0], m_w_ada[0], v_w_ada[0], dw_ada[None], "adamw_w_ada", dep=tok_c2)
    summed = _exchange_wait(h_sums, big["w_ada"][0], "wait_gather_small").reshape(-1)
    grads, off = {"b_ada": db_ada}, 0
    for n in reduced:
        size = math.prod(env[n].shape)
        grads[n] = summed[off:off + size].reshape(env[n].shape)
        off += size
    loss = summed[off]
    res = {n: tuple(a[None] for a in big[n]) for n in big}

    def small_step(n):
        res[n] = (grads[n],) + _adamw_nd(env[n], env["m_" + n], env["v_" + n], grads[n], "adamw_" + n)
        if n in swapped:
            res[n] = tuple(jnp.swapaxes(a, -1, -2) for a in res[n])

    for n in small_names:
        if n != "c_ctx":
            small_step(n)
    dc_ctx_all = _exchange_wait(h_cctx, res["w_spatial"][1], "wait_gather_d_c_ctx")
    grads["c_ctx"] = _sum_slots(dc_ctx_all[0::8].reshape(N_DEV, 1, d), "sum_d_c_ctx").reshape(d)
    small_step("c_ctx")

    order = ["c_ctx", "w_ada", "b_ada", "w_in", "sgu_ln_g", "sgu_ln_b", "w_spatial", "b_spatial", "s5_lam_re", "s5_lam_im",
             "s5_log_step", "s5_b_re", "s5_b_im", "s5_c_re", "s5_c_im", "s5_d", "w_glu", "b_glu", "w_out", "ln_g", "ln_b"]
    return (loss, grad_x[None], *[res[n][0] for n in order], *[res[n][1] for n in order],
            *[res[n][2] for n in order], *[res[n][3] for n in order])
```

```python
import functools
import math

import jax
import jax.numpy as jnp
from jax import lax
from jax.experimental import pallas as pl
from jax.experimental.pallas import tpu as pltpu

F32 = jnp.float32
MXU_DTYPE = jnp.bfloat16
N_DEV = 8
MESH_ID = pl.DeviceIdType.MESH
LN_EPS = 1e-6
DEPTH = 1
ALPHA = (2.0 * DEPTH) ** 0.25
CHUNK = 128
HEAD_DIM_A = 128
ADAM_LR, ADAM_B1, ADAM_B2, ADAM_EPS, ADAM_WD, ADAM_STEP = 0.001, 0.9, 0.999, 1e-08, 0.01, 10
LANES = 128
SCAN_UNROLL = 8
VMEM_LIMIT = 56 * 1024 * 1024
HBM = pl.BlockSpec(memory_space=pl.ANY)


def _cparams(*sem):
    return pltpu.CompilerParams(dimension_semantics=sem if sem else None, vmem_limit_bytes=VMEM_LIMIT)


def _tile(n, pref, mult=1):
    if n <= pref:
        return n
    t = pref - pref % mult
    while n % t:
        t -= mult
    return t


def _gelu(x):
    return 0.5 * x * (1.0 + lax.erf(x * (1.0 / math.sqrt(2.0))))


def _gelu_grad(x):
    return 0.5 * (1.0 + lax.erf(x * (1.0 / math.sqrt(2.0)))) + x * jnp.exp(-0.5 * x * x) * (1.0 / math.sqrt(2.0 * math.pi))


def _silu_grad(x):
    s = jax.nn.sigmoid(x)
    return s * (1.0 + x * (1.0 - s))


def _mxu_dot(a, b, dims=(((1,), (0,)), ((), ()))):
    return lax.dot_general(a.astype(MXU_DTYPE), b.astype(MXU_DTYPE), dims, preferred_element_type=F32)


_NT = (((1,), (1,)), ((), ()))
_TN = (((0,), (0,)), ((), ()))


def _mesh_pos():
    return lax.axis_index("x"), lax.axis_index("y"), lax.axis_index("c")


def _peer(pos, r):
    x, y, c = pos
    return ((1 - x) if r & 4 else x, (1 - y) if r & 2 else y, (1 - c) if r & 1 else c)


def _index(pos):
    return 4 * pos[0] + 2 * pos[1] + pos[2]


def _slice_of(ref, axis, idx, size):
    start = idx * size
    if axis == 0:
        return ref.at[pl.ds(start, size)]
    return ref.at[:, pl.ds(start, size)]


def _all_gather(x, axis, name):
    size = x.shape[axis]
    out_shape = tuple(s * N_DEV if a == axis else s for a, s in enumerate(x.shape))

    def body(x_ref, o_ref, send_sems, recv_sems, local_sem):
        me = _mesh_pos()
        mine = pltpu.make_async_copy(x_ref, _slice_of(o_ref, axis, _index(me), size), local_sem)
        mine.start()

        def copy(r, block):
            return pltpu.make_async_remote_copy(
                src_ref=x_ref, dst_ref=_slice_of(o_ref, axis, _index(block), size),
                send_sem=send_sems.at[r - 1], recv_sem=recv_sems.at[r - 1],
                device_id=_peer(me, r), device_id_type=MESH_ID)

        sends = [copy(r, me) for r in range(1, N_DEV)]
        for cp in sends:
            cp.start()
        for r in range(1, N_DEV):
            copy(r, _peer(me, r)).wait_recv()
        for cp in sends:
            cp.wait_send()
        mine.wait()

    return pl.pallas_call(
        body, name=name, out_shape=jax.ShapeDtypeStruct(out_shape, x.dtype),
        in_specs=[HBM], out_specs=HBM,
        scratch_shapes=[pltpu.SemaphoreType.DMA((N_DEV - 1,)), pltpu.SemaphoreType.DMA((N_DEV - 1,)),
                        pltpu.SemaphoreType.DMA],
    )(x)


_SEM = pl.BlockSpec(memory_space=pltpu.SEMAPHORE)
_HBM = pl.BlockSpec(memory_space=pltpu.HBM)
_EFFECT = pltpu.SideEffectType.DATAFLOW_SIDE_EFFECTING
ALL_PEERS = tuple(range(1, N_DEV))
SIBLING = 1
SAME_CORE_PEERS = (2, 4, 6)


def _exchange_copy(kind, x_ref, land_ref, axis, size, send_sems, recv_sems, me, rels, q, arriving):
    peer = _peer(me, rels[q])
    sender, receiver = (peer, me) if arriving else (me, peer)
    if kind == "gather":
        src, dst = x_ref, _slice_of(land_ref, axis, _index(sender), size)
    else:
        src, dst = _slice_of(x_ref, axis, _index(receiver), size), land_ref.at[_index(sender)]
    return pltpu.make_async_remote_copy(src_ref=src, dst_ref=dst, send_sem=send_sems.at[q], recv_sem=recv_sems.at[q],
                                        device_id=peer, device_id_type=MESH_ID)


def _local_copy(kind, x_ref, land_ref, axis, size, me, local_sem):
    if kind == "gather":
        return pltpu.make_async_copy(x_ref, _slice_of(land_ref, axis, _index(me), size), local_sem)
    return pltpu.make_async_copy(_slice_of(x_ref, axis, _index(me), size), land_ref.at[_index(me)], local_sem)


def _exchange_start(x, axis, kind, name, rels=ALL_PEERS):
    size = x.shape[axis] if kind == "gather" else x.shape[axis] // N_DEV
    if kind == "gather":
        land_shape = tuple(s * N_DEV if a == axis else s for a, s in enumerate(x.shape))
    else:
        land_shape = (N_DEV,) + tuple(size if a == axis else s for a, s in enumerate(x.shape))

    def body(x_ref, land_ref, send_sems, recv_sems, local_sem, x_thru, land_thru, token):
        del x_thru, land_thru
        me = _mesh_pos()
        _local_copy(kind, x_ref, land_ref, axis, size, me, local_sem).start()
        for q in range(len(rels)):
            _exchange_copy(kind, x_ref, land_ref, axis, size, send_sems, recv_sems, me, rels, q, False).start()
        token[...] = jnp.zeros_like(token)

    sems = pltpu.SemaphoreType.DMA((len(rels),))
    send_sems, recv_sems, local_sem, x_thru, land_thru, token = pl.pallas_call(
        body, name=name,
        out_shape=(sems, sems, pltpu.SemaphoreType.DMA(()), pltpu.HBM(x.shape, x.dtype), pltpu.HBM(land_shape, x.dtype),
                   jax.ShapeDtypeStruct((8, LANES), F32)),
        in_specs=(_HBM, _HBM), out_specs=(_SEM, _SEM, _SEM, _HBM, _HBM, pl.BlockSpec(memory_space=pltpu.VMEM)),
        input_output_aliases={0: 3, 1: 4}, compiler_params=pltpu.CompilerParams(has_side_effects=_EFFECT),
    )(pltpu.with_memory_space_constraint(x, pltpu.HBM),
      pltpu.with_memory_space_constraint(lax.empty(land_shape, x.dtype), pltpu.HBM))
    return (kind, axis, size, rels, send_sems, recv_sems, local_sem, x_thru, land_thru), token


def _exchange_wait(handle, after, name):
    kind, axis, size, rels, send_sems, recv_sems, local_sem, x_thru, land_thru = handle

    def body(x_ref, land_ref, send_sems, recv_sems, local_sem, after_ref, x_dead, got_ref):
        del after_ref, x_dead, got_ref
        me = _mesh_pos()
        _local_copy(kind, x_ref, land_ref, axis, size, me, local_sem).wait()
        for q in range(len(rels)):
            _exchange_copy(kind, x_ref, land_ref, axis, size, send_sems, recv_sems, me, rels, q, False).wait_send()
        for q in range(len(rels)):
            _exchange_copy(kind, x_ref, land_ref, axis, size, send_sems, recv_sems, me, rels, q, True).wait_recv()

    return pl.pallas_call(
        body, name=name, out_shape=(pltpu.HBM(x_thru.shape, x_thru.dtype), pltpu.HBM(land_thru.shape, land_thru.dtype)),
        in_specs=(_HBM, _HBM, _SEM, _SEM, _SEM, HBM), out_specs=(_HBM, _HBM), input_output_aliases={0: 0, 1: 1},
        compiler_params=pltpu.CompilerParams(has_side_effects=_EFFECT),
    )(x_thru, land_thru, send_sems, recv_sems, local_sem, after)[1]


def _forward_copy(land_ref, axis, size, send_sems, recv_sems, me, q, arriving):
    sibling = _peer(me, SIBLING)
    owner = _peer(sibling if arriving else me, SAME_CORE_PEERS[q])
    block = _slice_of(land_ref, axis, _index(owner), size)
    return pltpu.make_async_remote_copy(src_ref=block, dst_ref=block, send_sem=send_sems.at[q], recv_sem=recv_sems.at[q],
                                        device_id=sibling, device_id_type=MESH_ID)


def _forward_start(land, axis, name):
    size = land.shape[axis] // N_DEV

    def body(land_ref, send_sems, recv_sems, land_thru, token):
        del land_thru
        me = _mesh_pos()
        for q in range(len(SAME_CORE_PEERS)):
            _forward_copy(land_ref, axis, size, send_sems, recv_sems, me, q, False).start()
        token[...] = jnp.zeros_like(token)

    sems = pltpu.SemaphoreType.DMA((len(SAME_CORE_PEERS),))
    send_sems, recv_sems, land_thru, token = pl.pallas_call(
        body, name=name, out_shape=(sems, sems, pltpu.HBM(land.shape, land.dtype), jax.ShapeDtypeStruct((8, LANES), F32)),
        in_specs=(_HBM,), out_specs=(_SEM, _SEM, _HBM, pl.BlockSpec(memory_space=pltpu.VMEM)),
        input_output_aliases={0: 2}, compiler_params=pltpu.CompilerParams(has_side_effects=_EFFECT),
    )(land)
    return (axis, size, send_sems, recv_sems, land_thru), token


def _forward_wait(handle, after, name):
    axis, size, send_sems, recv_sems, land_thru = handle

    def body(land_ref, send_sems, recv_sems, after_ref, got_ref):
        del after_ref, got_ref
        me = _mesh_pos()
        for q in range(len(SAME_CORE_PEERS)):
            _forward_copy(land_ref, axis, size, send_sems, recv_sems, me, q, False).wait_send()
        for q in range(len(SAME_CORE_PEERS)):
            _forward_copy(land_ref, axis, size, send_sems, recv_sems, me, q, True).wait_recv()

    return pl.pallas_call(
        body, name=name, out_shape=pltpu.HBM(land_thru.shape, land_thru.dtype),
        in_specs=(_HBM, _SEM, _SEM, HBM), out_specs=_HBM, input_output_aliases={0: 0},
        compiler_params=pltpu.CompilerParams(has_side_effects=_EFFECT),
    )(land_thru, send_sems, recv_sems, after)


WHOLE = 1 << 30
MATMUL_TILES = {
    "proj": (1024, 1024, WHOLE), "proj_ctx": (256, WHOLE, WHOLE), "out_proj": (1024, 1024, WHOLE),
    "d_cat": (1024, 1024, WHOLE), "d_w_out": (1024, 1024, 2048), "d_w_in": (1024, 1280, 2048),
    "d_w_in_ctx": (1024, WHOLE, WHOLE), "d_xm": (1024, 512, WHOLE), "d_cm": (256, 1024, WHOLE),
}


def _matmul(a, b, *, mode, name, out_dtype=F32, b_n0=0, n=None, b_k0=0, k=None, acc_in=None, acc_n0=0, dep=None):
    bm, bn, bk = MATMUL_TILES[name]
    if mode == "tn":
        kk, m = a.shape
    else:
        m, kk = a.shape
    if mode == "nn":
        n = b.shape[1] if n is None else n
    elif mode == "nt":
        n = b.shape[0]
        kk = kk if k is None else k
    else:
        n = b.shape[1]
    bm, bn, bk = _tile(m, bm), _tile(n, bn), _tile(kk, bk)
    nk = kk // bk
    assert b_n0 % bn == 0 and b_k0 % bk == 0 and acc_n0 % bn == 0
    dims = {"nn": (((1,), (0,)), ((), ())), "nt": _NT, "tn": _TN}[mode]

    n_in = 2 + (acc_in is not None) + (dep is not None)

    def body(*refs):
        a_ref, b_ref = refs[:2]
        init = refs[2] if acc_in is not None else None
        o_ref = refs[n_in]
        acc_ref = refs[-1] if nk > 1 else None
        p = _mxu_dot(a_ref[...], b_ref[...], dims)
        if nk == 1:
            o_ref[...] = (p if init is None else p + init[...]).astype(out_dtype)
            return
        ki = pl.program_id(2)

        @pl.when(ki == 0)
        def _():
            acc_ref[...] = p if init is None else p + init[...]

        @pl.when(ki > 0)
        def _():
            acc_ref[...] += p

        @pl.when(ki == nk - 1)
        def _():
            o_ref[...] = acc_ref[...].astype(out_dtype)

    a_spec = pl.BlockSpec((bk, bm), lambda j, i, q: (q, i)) if mode == "tn" else pl.BlockSpec((bm, bk), lambda j, i, q: (i, q))
    if mode == "nt":
        b_spec = pl.BlockSpec((bn, bk), lambda j, i, q: (j, q + b_k0 // bk))
    else:
        b_spec = pl.BlockSpec((bk, bn), lambda j, i, q: (q, j + b_n0 // bn))
    in_specs, args, aliases = [a_spec, b_spec], [a, b], {}
    out_map = lambda j, i, q: (i, j + acc_n0 // bn)
    if acc_in is not None:
        in_specs.append(pl.BlockSpec((bm, bn), out_map))
        args.append(acc_in)
        aliases = {2: 0}
        out_shape = jax.ShapeDtypeStruct(acc_in.shape, out_dtype)
    else:
        out_shape = jax.ShapeDtypeStruct((m, n), out_dtype)
    if dep is not None:
        in_specs.append(HBM)
        args.append(dep)
    return pl.pallas_call(
        body, name=name, out_shape=out_shape, grid=(n // bn, m // bm, nk),
        in_specs=in_specs, out_specs=pl.BlockSpec((bm, bn), out_map),
        scratch_shapes=[pltpu.VMEM((bm, bn), F32)] if nk > 1 else [],
        input_output_aliases=aliases,
        compiler_params=_cparams("parallel", "parallel", "arbitrary"),
    )(*args)


def _silu_rows(c, c_ctx):
    d = c.shape[-1]

    def body(c_ref, cc_ref, o_ref):
        o_ref[...] = jnp.zeros_like(o_ref)
        o_ref[0:1, :] = jax.nn.silu(c_ref[...])
        o_ref[1:2, :] = jax.nn.silu(cc_ref[...])

    return pl.pallas_call(body, name="silu_rows", out_shape=jax.ShapeDtypeStruct((8, d), F32))(
        c.reshape(1, d), c_ctx.reshape(1, d))


def _small_dot(a, b, mode, name):
    dims = {"nn": (((1,), (0,)), ((), ())), "nt": _NT, "tn": _TN}[mode]
    m = a.shape[1] if mode == "tn" else a.shape[0]
    n = b.shape[0] if mode == "nt" else b.shape[1]

    def body(a_ref, b_ref, o_ref):
        o_ref[...] = lax.dot_general(a_ref[...], b_ref[...], dims, preferred_element_type=F32,
                                     precision=lax.Precision.HIGHEST)

    return pl.pallas_call(body, name=name, out_shape=jax.ShapeDtypeStruct((m, n), F32),
                          compiler_params=_cparams())(a, b)


def _ln_stats(x):
    mu = jnp.mean(x, axis=-1, keepdims=True)
    xc = x - mu
    var = jnp.mean(xc * xc, axis=-1, keepdims=True)
    rstd = lax.rsqrt(var + LN_EPS)
    return xc * rstd, rstd


def _ln_mod(x, shift, scale, name):
    l, d = x.shape
    tl = _tile(l, 512)

    def body(x_ref, sh_ref, sc_ref, o_ref):
        xhat, _ = _ln_stats(x_ref[...])
        o_ref[...] = (xhat * (1.0 + sc_ref[...]) + sh_ref[...]).astype(o_ref.dtype)

    row = pl.BlockSpec((tl, d), lambda i: (i, 0))
    vec = pl.BlockSpec((1, d), lambda i: (0, 0))
    return pl.pallas_call(body, name=name, out_shape=jax.ShapeDtypeStruct((l, d), MXU_DTYPE), grid=(l // tl,),
                          in_specs=[row, vec, vec], out_specs=row, compiler_params=_cparams("parallel"))(x, shift, scale)


def _ln_mod_bwd(x, dxm, scale, res, name):
    l, d = x.shape
    tl = _tile(l, 512)
    with_res = res is not None

    def body(*refs):
        if with_res:
            x_ref, g_ref, sc_ref, r_ref, dx_ref, dsh_ref, dsc_ref = refs
        else:
            x_ref, g_ref, sc_ref, dx_ref, dsh_ref, dsc_ref = refs
        i = pl.program_id(0)
        xhat, rstd = _ln_stats(x_ref[...])
        g = g_ref[...].astype(F32)
        dxh = g * (1.0 + sc_ref[...])
        dx = rstd * (dxh - jnp.mean(dxh, axis=-1, keepdims=True) - xhat * jnp.mean(dxh * xhat, axis=-1, keepdims=True))
        dx_ref[...] = dx + r_ref[...].astype(F32) if with_res else dx

        @pl.when(i == 0)
        def _():
            dsh_ref[...] = jnp.zeros_like(dsh_ref)
            dsc_ref[...] = jnp.zeros_like(dsc_ref)

        dsh_ref[...] += jnp.sum(g, axis=0, keepdims=True)
        dsc_ref[...] += jnp.sum(g * xhat, axis=0, keepdims=True)

    row = pl.BlockSpec((tl, d), lambda i: (i, 0))
    vec = pl.BlockSpec((1, d), lambda i: (0, 0))
    args = [x, dxm, scale] + ([res] if with_res else [])
    return pl.pallas_call(
        body, name=name,
        out_shape=(jax.ShapeDtypeStruct((l, d), F32), jax.ShapeDtypeStruct((1, d), F32), jax.ShapeDtypeStruct((1, d), F32)),
        grid=(l // tl,), in_specs=[row, row, vec] + ([row] if with_res else []), out_specs=(row, vec, vec),
        compiler_params=_cparams("arbitrary"))(*args)


def _post_ln_loss(x, out, gate, ln_g, ln_b, target):
    l, d = x.shape
    tl = _tile(l, 512)

    def body(x_ref, o_ref, gate_ref, g_ref, b_ref, t_ref, loss_ref, dout_ref, dxr_ref, dgate_ref, dg_ref, db_ref):
        i = pl.program_id(0)
        out_t = o_ref[...]
        gate_v = gate_ref[...]
        rhat, rstd = _ln_stats(ALPHA * x_ref[...] + gate_v * out_t)
        ln_gv = g_ref[...]
        diff = rhat * ln_gv + b_ref[...] - t_ref[...]
        dy = diff * (1.0 / d)
        drh = dy * ln_gv
        dr = rstd * (drh - jnp.mean(drh, axis=-1, keepdims=True) - rhat * jnp.mean(drh * rhat, axis=-1, keepdims=True))
        dout_ref[...] = (gate_v * dr).astype(dout_ref.dtype)
        dxr_ref[...] = (ALPHA * dr).astype(dxr_ref.dtype)

        @pl.when(i == 0)
        def _():
            for r in (loss_ref, dgate_ref, dg_ref, db_ref):
                r[...] = jnp.zeros_like(r)

        loss_ref[...] += jnp.sum(diff * diff, axis=0, keepdims=True)
        dgate_ref[...] += jnp.sum(dr * out_t, axis=0, keepdims=True)
        dg_ref[...] += jnp.sum(dy * rhat, axis=0, keepdims=True)
        db_ref[...] += jnp.sum(dy, axis=0, keepdims=True)

    row = pl.BlockSpec((tl, d), lambda i: (i, 0))
    vec = pl.BlockSpec((1, d), lambda i: (0, 0))
    v = jax.ShapeDtypeStruct((1, d), F32)
    return pl.pallas_call(
        body, name="post_ln_loss",
        out_shape=(v, jax.ShapeDtypeStruct((l, d), MXU_DTYPE), jax.ShapeDtypeStruct((l, d), MXU_DTYPE), v, v, v),
        grid=(l // tl,), in_specs=[row, row, vec, vec, vec, row], out_specs=(vec, row, row, vec, vec, vec),
        compiler_params=_cparams("arbitrary"))(x, out, gate, ln_g, ln_b, target)


def _ga_forward_tile(p, g, b, ws_ref, bsf, w, nc, nh):
    u_raw, v_raw, za = p[:, :w], p[:, w:2 * w], p[:, 2 * w:3 * w]
    gu = _gelu(u_raw)
    vhat, rstd = _ln_stats(_gelu(v_raw))
    vn = vhat * g + b
    rows = []
    for ci in range(nc):
        r0 = ci * CHUNK
        heads = [_mxu_dot(ws_ref[h], vn[r0:r0 + CHUNK, h * HEAD_DIM_A:(h + 1) * HEAD_DIM_A]) for h in range(nh)]
        rows.append(jnp.concatenate(heads, axis=1) + bsf)
    mixed = jnp.concatenate(rows, axis=0) if nc > 1 else rows[0]
    return u_raw, v_raw, za, gu, vhat, rstd, vn, mixed


def _ga_fwd(proj, g, b, ws, bsf, w):
    l = proj.shape[0]
    nh = w // HEAD_DIM_A
    nc = _tile(l // CHUNK, 2)
    tl = nc * CHUNK

    def body(p_ref, g_ref, b_ref, ws_ref, bsf_ref, o_ref):
        _, _, za, gu, _, _, _, mixed = _ga_forward_tile(p_ref[...], g_ref[...], b_ref[...], ws_ref, bsf_ref[...], w, nc, nh)
        o_ref[...] = (gu * mixed * jax.nn.silu(za)).astype(o_ref.dtype)

    vec = pl.BlockSpec((1, w), lambda i: (0, 0))
    return pl.pallas_call(
        body, name="ga_fwd", out_shape=jax.ShapeDtypeStruct((l, 2 * w), MXU_DTYPE), grid=(l // tl,),
        in_specs=[pl.BlockSpec((tl, 3 * w), lambda i: (i, 0)), vec, vec,
                  pl.BlockSpec((nh, CHUNK, CHUNK), lambda i: (0, 0, 0)), pl.BlockSpec((CHUNK, w), lambda i: (0, 0))],
        out_specs=pl.BlockSpec((tl, w), lambda i: (i, 0)), compiler_params=_cparams("parallel"))(proj, g, b, ws, bsf)


def _ga_bwd(proj, dcat, dproj, g, b, ws, bsf, dys, du0, du1, d_skip, w):
    l = proj.shape[0]
    nh = w // HEAD_DIM_A
    nc = _tile(l // CHUNK, 2)
    tl = nc * CHUNK

    def body(p_ref, dy_ref, dp_in, g_ref, b_ref, ws_ref, bsf_ref, dys_ref, du0_ref, du1_ref, d_ref,
             dp_ref, dg_ref, db_ref, dws_ref, dbsf_ref):
        del dp_in
        i = pl.program_id(0)
        dp_ref[:, 3 * w:] = (dys_ref[...] * d_ref[...] + du0_ref[...] + du1_ref[...]).astype(dp_ref.dtype)
        gv = g_ref[...]
        u_raw, v_raw, za, gu, vhat, rstd, vn, mixed = _ga_forward_tile(
            p_ref[...], gv, b_ref[...], ws_ref, bsf_ref[...], w, nc, nh)
        dya = dy_ref[...].astype(F32)
        sz = jax.nn.silu(za)
        dmixed = dya * gu * sz
        dza = dya * gu * mixed * _silu_grad(za)
        dgu = dya * mixed * sz

        @pl.when(i == 0)
        def _():
            for r in (dg_ref, db_ref, dws_ref, dbsf_ref):
                r[...] = jnp.zeros_like(r)

        rows = []
        for ci in range(nc):
            r0 = ci * CHUNK
            heads = []
            for h in range(nh):
                cols = slice(h * HEAD_DIM_A, (h + 1) * HEAD_DIM_A)
                dm = dmixed[r0:r0 + CHUNK, cols]
                heads.append(_mxu_dot(ws_ref[h], dm, _TN))
                dws_ref[h] += _mxu_dot(dm, vn[r0:r0 + CHUNK, cols], _NT)
            rows.append(jnp.concatenate(heads, axis=1))
            dbsf_ref[...] += dmixed[r0:r0 + CHUNK, :]
        dvn = jnp.concatenate(rows, axis=0) if nc > 1 else rows[0]
        dg_ref[...] += jnp.sum(dvn * vhat, axis=0, keepdims=True)
        db_ref[...] += jnp.sum(dvn, axis=0, keepdims=True)
        dvh = dvn * gv
        dgv = rstd * (dvh - jnp.mean(dvh, axis=-1, keepdims=True) - vhat * jnp.mean(dvh * vhat, axis=-1, keepdims=True))
        dp_ref[:, :w] = (dgu * _gelu_grad(u_raw)).astype(dp_ref.dtype)
        dp_ref[:, w:2 * w] = (dgv * _gelu_grad(v_raw)).astype(dp_ref.dtype)
        dp_ref[:, 2 * w:3 * w] = dza.astype(dp_ref.dtype)

    vec = pl.BlockSpec((1, w), lambda i: (0, 0))
    row = pl.BlockSpec((tl, w), lambda i: (i, 0))
    ws_spec = pl.BlockSpec((nh, CHUNK, CHUNK), lambda i: (0, 0, 0))
    bs_spec = pl.BlockSpec((CHUNK, w), lambda i: (0, 0))
    v = jax.ShapeDtypeStruct((1, w), F32)
    return pl.pallas_call(
        body, name="ga_bwd",
        out_shape=(jax.ShapeDtypeStruct(dproj.shape, dproj.dtype), v, v, jax.ShapeDtypeStruct((nh, CHUNK, CHUNK), F32),
                   jax.ShapeDtypeStruct((CHUNK, w), F32)),
        grid=(l // tl,),
        in_specs=[pl.BlockSpec((tl, 3 * w), lambda i: (i, 0)), row, HBM, vec, vec, ws_spec, bs_spec, row, row, row, vec],
        out_specs=(pl.BlockSpec((tl, 4 * w), lambda i: (i, 0)), vec, vec, ws_spec, bs_spec),
        input_output_aliases={2: 0}, compiler_params=_cparams("arbitrary"))(
            proj, dcat, dproj, g, b, ws, bsf, dys, du0, du1, d_skip)


def _lane_group_sum(x, expand, name):
    return _small_dot(x, expand, "nn", name)


def _disc_math(lr, li, ls, br, bi):
    step = jnp.exp(ls)
    dr, di = lr * step, li * step
    mag = jnp.exp(dr)
    ab_re, ab_im = mag * jnp.cos(di), mag * jnp.sin(di)
    den = lr * lr + li * li
    nr, ni = ab_re - 1.0, ab_im
    f_re = (nr * lr + ni * li) / den
    f_im = (ni * lr - nr * li) / den
    bb_re = f_re * br - f_im * bi
    bb_im = f_re * bi + f_im * br
    return ab_re, ab_im, bb_re, bb_im


def _disc_fwd(lr, li, ls, br, bi):
    def body(lr_ref, li_ref, ls_ref, br_ref, bi_ref, o1, o2, o3, o4):
        res = _disc_math(lr_ref[...], li_ref[...], ls_ref[...], br_ref[...], bi_ref[...])
        for o, r in zip((o1, o2, o3, o4), res):
            o[...] = r

    s = lambda a: jax.ShapeDtypeStruct(a.shape, F32)
    return pl.pallas_call(body, name="s5_disc", out_shape=(s(lr), s(lr), s(br), s(br)), compiler_params=_cparams())(
        lr, li, ls, br, bi)


def _disc_bwd(lr, li, ls, br, bi, d_ar, d_ai, d_br, d_bi):
    def body(lr_ref, li_ref, ls_ref, br_ref, bi_ref, c1, c2, c3, c4, o1, o2, o3, o4, o5):
        _, vjp = jax.vjp(_disc_math, lr_ref[...], li_ref[...], ls_ref[...], br_ref[...], bi_ref[...])
        res = vjp((c1[...], c2[...], c3[...], c4[...]))
        for o, r in zip((o1, o2, o3, o4, o5), res):
            o[...] = r

    s = lambda a: jax.ShapeDtypeStruct(a.shape, F32)
    return pl.pallas_call(body, name="s5_disc_bwd", out_shape=(s(lr), s(lr), s(ls), s(br), s(br)),
                          compiler_params=_cparams())(lr, li, ls, br, bi, d_ar, d_ai, d_br, d_bi)


def _dir_spec(a, dr, **kw):
    return pl.BlockSpec((None,) + a.shape[1:], lambda i: (dr,) + (0,) * (a.ndim - 1), **kw)


def _s5_fwd(u_arr, u_col, w, h0, a_sm, wbr, wbi, cre, ncim, dr, name):
    rev = dr == 1
    l = u_arr.shape[0]
    nb = w // LANES
    spb = wbr.shape[-1]
    nsr = a_sm.shape[2]
    assert 2 * spb == 8 * LANES and nb % 2 == 0
    npair = nb // 2
    t = _tile(l, 256)
    n = l // t
    tile = (lambda i: n - 1 - i) if rev else (lambda i: i)

    def body(u_ref, h0_ref, a_ref, wbr_ref, wbi_ref, cre_ref, ncim_ref, y_ref, hr_ref, hi_ref, tr_ref, ti_ref, hfin_ref,
             carry_ref):
        i = pl.program_id(0)

        @pl.when(i == 0)
        def _():
            carry_ref[...] = h0_ref[...]

        for j in range(npair):
            for h_ref, w_ref in ((hr_ref, wbr_ref), (hi_ref, wbi_ref)):
                blk = [_mxu_dot(u_ref[:, k * LANES:(k + 1) * LANES], w_ref[k]) for k in (2 * j, 2 * j + 1)]
                h_ref[j] = jnp.concatenate(blk, axis=1).reshape(t, 8, LANES)
        slab = lambda ref, part, j: ref[part, 8 * j:8 * j + 8, :]
        ar = [slab(a_ref, 0, j) for j in range(npair)]
        ai = [slab(a_ref, 1, j) for j in range(npair)]

        def steps(blk, c):
            hr, hi = list(c[:npair]), list(c[npair:])
            for q in range(SCAN_UNROLL):
                s = blk * SCAN_UNROLL + q
                row = t - 1 - s if rev else s
                for j in range(npair):
                    hr[j], hi[j] = (ar[j] * hr[j] - ai[j] * hi[j] + hr_ref[j, row],
                                    ar[j] * hi[j] + ai[j] * hr[j] + hi_ref[j, row])
                    hr_ref[j, row] = hr[j]
                    hi_ref[j, row] = hi[j]
            return tuple(hr + hi)

        init = tuple(slab(carry_ref, part, j) for part in range(2) for j in range(npair))
        c = lax.fori_loop(0, t // SCAN_UNROLL, steps, init)
        for part in range(2):
            for j in range(npair):
                carry_ref[part, 8 * j:8 * j + 8, :] = c[part * npair + j]
                hfin_ref[part, 8 * j:8 * j + 8, :] = c[part * npair + j]
        for j in range(npair):
            cols8 = slice(j * 8 * LANES, (j + 1) * 8 * LANES)
            tr_ref[:, cols8] = hr_ref[j].reshape(t, 8 * LANES).astype(tr_ref.dtype)
            ti_ref[:, cols8] = hi_ref[j].reshape(t, 8 * LANES).astype(ti_ref.dtype)
        for k in range(nb):
            cols = slice(k * spb, (k + 1) * spb)
            y_ref[:, k * LANES:(k + 1) * LANES] = (_mxu_dot(tr_ref[:, cols], cre_ref[k]) + _mxu_dot(ti_ref[:, cols], ncim_ref[k]))

    full = lambda a: pl.BlockSpec(a.shape, lambda i: (0,) * a.ndim)
    hspec = pl.BlockSpec((npair, t, 8, LANES), lambda i: (0, tile(i), 0, 0))
    tspec = pl.BlockSpec((t, nsr * LANES), lambda i: (tile(i), 0))
    hsh = jax.ShapeDtypeStruct((npair, l, 8, LANES), F32)
    tsh = jax.ShapeDtypeStruct((l, nsr * LANES), MXU_DTYPE)
    return pl.pallas_call(
        body, name=name,
        out_shape=(jax.ShapeDtypeStruct((l, w), F32), hsh, hsh, tsh, tsh, jax.ShapeDtypeStruct((2, nsr, LANES), F32)),
        grid=(n,),
        in_specs=[pl.BlockSpec((t, w), lambda i: (tile(i), u_col)), full(h0)] + [_dir_spec(a, dr) for a in (a_sm, wbr, wbi, cre, ncim)],
        out_specs=(pl.BlockSpec((t, w), lambda i: (tile(i), 0)), hspec, hspec, tspec, tspec,
                   pl.BlockSpec((2, nsr, LANES), lambda i: (0, 0, 0))),
        scratch_shapes=[pltpu.VMEM((2, nsr, LANES), F32)],
        compiler_params=_cparams("arbitrary"))(u_arr, h0, a_sm, wbr, wbi, cre, ncim)


def _s5_bwd(dys, u_arr, u_col, w, hr, hi, tr, ti, hbound, g_in, a_sm, wbr_t, wbi_t, cre_t, ncim_t, dr, name):
    rev = dr == 1
    l = u_arr.shape[0]
    nb = w // LANES
    spb = wbr_t.shape[-2]
    nsr = a_sm.shape[2]
    npair = nb // 2
    t = _tile(l, 256)
    n = l // t
    with_dy = dys is not None
    tile = (lambda i: i) if rev else (lambda i: n - 1 - i)

    def body(*refs):
        if with_dy:
            (dy_ref, u_ref, hr_ref, hi_ref, pr_ref, pi_ref, tr_ref, ti_ref, hb_ref, gin_ref, a_ref, wbrt_ref, wbit_ref,
             cret_ref, ncimt_ref, du_ref, dwbr_ref, dwbi_ref, dcre_ref, dncim_ref, da_ref, gout_ref,
             gr_ref, gi_ref, gtr_ref, gti_ref, carry_ref) = refs
        else:
            (u_ref, hr_ref, hi_ref, pr_ref, pi_ref, hb_ref, gin_ref, a_ref, wbrt_ref, wbit_ref,
             du_ref, dwbr_ref, dwbi_ref, da_ref, gout_ref, gr_ref, gi_ref, gtr_ref, gti_ref, carry_ref) = refs
        i = pl.program_id(0)

        @pl.when(i == 0)
        def _():
            carry_ref[...] = gin_ref[...]
            accs = (dwbr_ref, dwbi_ref, da_ref) + ((dcre_ref, dncim_ref) if with_dy else ())
            for r in accs:
                r[...] = jnp.zeros_like(r)

        if with_dy:
            for j in range(npair):
                for g_ref, c_ref in ((gr_ref, cret_ref), (gi_ref, ncimt_ref)):
                    blk = [_mxu_dot(dy_ref[:, k * LANES:(k + 1) * LANES], c_ref[k]) for k in (2 * j, 2 * j + 1)]
                    g_ref[j] = jnp.concatenate(blk, axis=1).reshape(t, 8, LANES)
        else:
            gr_ref[...] = jnp.zeros_like(gr_ref)
            gi_ref[...] = jnp.zeros_like(gi_ref)
        slab = lambda ref, part, j: ref[part, 8 * j:8 * j + 8, :]
        last = t - 1 if rev else 0
        first = i == n - 1

        ar = [slab(a_ref, 0, j) for j in range(npair)]
        ai = [slab(a_ref, 1, j) for j in range(npair)]

        def steps(blk, c):
            gr, gi, dr, di = (list(c[q * npair:(q + 1) * npair]) for q in range(4))
            for q in range(SCAN_UNROLL):
                s = blk * SCAN_UNROLL + q
                row = s if rev else t - 1 - s
                prow = jnp.minimum(row + 1, t - 1) if rev else jnp.maximum(row - 1, 0)
                for j in range(npair):
                    pr, pi = hr_ref[j, prow], hi_ref[j, prow]
                    gr[j], gi[j] = (gr_ref[j, row] + ar[j] * gr[j] + ai[j] * gi[j],
                                    gi_ref[j, row] + ar[j] * gi[j] - ai[j] * gr[j])
                    gr_ref[j, row] = gr[j]
                    gi_ref[j, row] = gi[j]
                    dr[j], di[j] = dr[j] + gr[j] * pr + gi[j] * pi, di[j] + gi[j] * pr - gr[j] * pi
            return tuple(gr + gi + dr + di)

        init = tuple(slab(ref, part, j) for ref in (carry_ref, da_ref) for part in range(2) for j in range(npair))
        c = lax.fori_loop(0, t // SCAN_UNROLL, steps, init)
        gr, gi, dr, di = (c[q * npair:(q + 1) * npair] for q in range(4))
        for j in range(npair):
            pr = jnp.where(first, slab(hb_ref, 0, j), pr_ref[j, 0]) - hr_ref[j, last]
            pi = jnp.where(first, slab(hb_ref, 1, j), pi_ref[j, 0]) - hi_ref[j, last]
            rows = slice(8 * j, 8 * j + 8)
            da_ref[0, rows, :] = dr[j] + gr[j] * pr + gi[j] * pi
            da_ref[1, rows, :] = di[j] + gi[j] * pr - gr[j] * pi
            for part, val in enumerate((gr[j], gi[j])):
                carry_ref[part, rows, :] = val
                gout_ref[part, rows, :] = val

        for j in range(npair):
            cols8 = slice(j * 8 * LANES, (j + 1) * 8 * LANES)
            gtr_ref[:, cols8] = gr_ref[j].reshape(t, 8 * LANES).astype(gtr_ref.dtype)
            gti_ref[:, cols8] = gi_ref[j].reshape(t, 8 * LANES).astype(gti_ref.dtype)
        for k in range(nb):
            cols = slice(k * spb, (k + 1) * spb)
            lanes = slice(k * LANES, (k + 1) * LANES)
            du_ref[:, lanes] = _mxu_dot(gtr_ref[:, cols], wbrt_ref[k]) + _mxu_dot(gti_ref[:, cols], wbit_ref[k])
            dwbr_ref[k] += _mxu_dot(u_ref[:, lanes], gtr_ref[:, cols], _TN)
            dwbi_ref[k] += _mxu_dot(u_ref[:, lanes], gti_ref[:, cols], _TN)
            if with_dy:
                dcre_ref[k] += _mxu_dot(tr_ref[:, cols], dy_ref[:, lanes], _TN)
                dncim_ref[k] += _mxu_dot(ti_ref[:, cols], dy_ref[:, lanes], _TN)

    once = dict(pipeline_mode=pl.Buffered(1))
    full = lambda a: pl.BlockSpec(a.shape, lambda i: (0,) * a.ndim, **once)
    row = lambda cb: pl.BlockSpec((t, w), lambda i: (tile(i), cb))
    hspec = pl.BlockSpec((npair, t, 8, LANES), lambda i: (0, tile(i), 0, 0))
    if rev:
        pspec = pl.BlockSpec((npair, 1, 8, LANES), lambda i: (0, jnp.minimum((tile(i) + 1) * t, l - 1), 0, 0))
    else:
        pspec = pl.BlockSpec((npair, 1, 8, LANES), lambda i: (0, jnp.maximum(tile(i) * t - 1, 0), 0, 0))
    sm = jax.ShapeDtypeStruct((2, nsr, LANES), F32)
    smspec = pl.BlockSpec((2, nsr, LANES), lambda i: (0, 0, 0))
    wsh = jax.ShapeDtypeStruct((nb, LANES, spb), F32)
    csh = jax.ShapeDtypeStruct((nb, spb, LANES), F32)
    tspec = pl.BlockSpec((t, nsr * LANES), lambda i: (tile(i), 0))
    in_specs = (([row(0)] if with_dy else []) + [row(u_col), hspec, hspec, pspec, pspec] + ([tspec, tspec] if with_dy else [])
                + [full(hbound), full(g_in)]
                + [_dir_spec(a, dr, **once) for a in (a_sm, wbr_t, wbi_t) + ((cre_t, ncim_t) if with_dy else ())])
    args = (([dys] if with_dy else []) + [u_arr, hr, hi, hr, hi] + ([tr, ti] if with_dy else [])
            + [hbound, g_in, a_sm, wbr_t, wbi_t] + ([cre_t, ncim_t] if with_dy else []))
    out_shape = (jax.ShapeDtypeStruct((l, w), F32), wsh, wsh) + ((csh, csh) if with_dy else ()) + (sm, sm)
    out_specs = (row(0), full(wsh), full(wsh)) + ((full(csh), full(csh)) if with_dy else ()) + (smspec, smspec)
    return pl.pallas_call(
        body, name=name, out_shape=out_shape, grid=(n,), in_specs=in_specs, out_specs=out_specs,
        scratch_shapes=[pltpu.VMEM((npair, t, 8, LANES), F32)] * 2 + [pltpu.VMEM((t, nsr * LANES), MXU_DTYPE)] * 2
        + [pltpu.VMEM((2, nsr, LANES), F32)],
        compiler_params=_cparams("arbitrary"))(*args)


def _glu_fwd(y0, y1, proj, cat, d_skip, w_glu, b_glu, w):
    l = y0.shape[0]
    tl = _tile(l, 256)

    def body(y0_ref, y1_ref, u_ref, z_ref, cat_in, d_ref, wg_ref, bg_ref, ys_ref, cat_ref):
        del cat_in
        ys = y0_ref[...] + y1_ref[...] + d_ref[...] * u_ref[...]
        ys_ref[...] = ys
        gy = _gelu(ys)
        s = _mxu_dot(gy, wg_ref[...]) + bg_ref[...]
        cat_ref[...] = (gy * jax.nn.sigmoid(s) * jax.nn.silu(z_ref[...])).astype(cat_ref.dtype)

    row = pl.BlockSpec((tl, w), lambda i: (i, 0))
    vec = pl.BlockSpec((1, w), lambda i: (0, 0))
    return pl.pallas_call(
        body, name="glu_fwd", out_shape=(jax.ShapeDtypeStruct((l, w), F32), jax.ShapeDtypeStruct(cat.shape, cat.dtype)),
        grid=(l // tl,),
        in_specs=[row, row, pl.BlockSpec((tl, w), lambda i: (i, 3)), pl.BlockSpec((tl, w), lambda i: (i, 4)), HBM,
                  vec, pl.BlockSpec((w, w), lambda i: (0, 0)), vec],
        out_specs=(row, pl.BlockSpec((tl, w), lambda i: (i, 1))), input_output_aliases={4: 1},
        compiler_params=_cparams("parallel"))(y0, y1, proj, proj, cat, d_skip, w_glu, b_glu)


def _glu_bwd(dcat, ys, proj, w_glu, b_glu, w, dep):
    l = ys.shape[0]
    tl = _tile(l, 256)

    def body(dy_ref, ys_ref, u_ref, z_ref, wg_ref, bg_ref, dep_ref, dys_ref, dp_ref, dbg_ref, dd_ref, dwg_ref):
        del dep_ref
        i = pl.program_id(0)
        ys_t = ys_ref[...]
        z = z_ref[...]
        dyb = dy_ref[...].astype(F32)
        gy = _gelu(ys_t)
        sg = jax.nn.sigmoid(_mxu_dot(gy, wg_ref[...]) + bg_ref[...])
        dp_ref[...] = (dyb * gy * sg * _silu_grad(z)).astype(dp_ref.dtype)
        dglu = dyb * jax.nn.silu(z)
        ds = dglu * gy * sg * (1.0 - sg)
        dgy = dglu * sg + _mxu_dot(ds, wg_ref[...], _NT)
        dys_t = dgy * _gelu_grad(ys_t)
        dys_ref[...] = dys_t

        @pl.when(i == 0)
        def _():
            for r in (dbg_ref, dd_ref, dwg_ref):
                r[...] = jnp.zeros_like(r)

        dbg_ref[...] += jnp.sum(ds, axis=0, keepdims=True)
        dd_ref[...] += jnp.sum(dys_t * u_ref[...], axis=0, keepdims=True)
        dwg_ref[...] += _mxu_dot(gy, ds, _TN)

    row = pl.BlockSpec((tl, w), lambda i: (i, 0))
    vec = pl.BlockSpec((1, w), lambda i: (0, 0))
    mat = pl.BlockSpec((w, w), lambda i: (0, 0))
    v = jax.ShapeDtypeStruct((1, w), F32)
    return pl.pallas_call(
        body, name="glu_bwd",
        out_shape=(jax.ShapeDtypeStruct((l, w), F32), jax.ShapeDtypeStruct((l, 5 * w), MXU_DTYPE), v, v,
                   jax.ShapeDtypeStruct((w, w), F32)),
        grid=(l // tl,),
        in_specs=[pl.BlockSpec((tl, w), lambda i: (i, 1)), row, pl.BlockSpec((tl, w), lambda i: (i, 3)),
                  pl.BlockSpec((tl, w), lambda i: (i, 4)), mat, vec, HBM],
        out_specs=(row, pl.BlockSpec((tl, w), lambda i: (i, 4)), vec, vec, mat),
        compiler_params=_cparams("arbitrary"))(dcat, ys, proj, proj, w_glu, b_glu, dep)


def _add2(a, b, name):
    l, w = a.shape
    tl = _tile(l, 512)

    def body(a_ref, b_ref, o_ref):
        o_ref[...] = a_ref[...] + b_ref[...]

    row = pl.BlockSpec((tl, w), lambda i: (i, 0))
    return pl.pallas_call(body, name=name, out_shape=jax.ShapeDtypeStruct((l, w), F32), grid=(l // tl,),
                          in_specs=[row, row], out_specs=row, compiler_params=_cparams("parallel"))(a, b)


def _adamw_nd(w, m, v, g, name):
    shape = w.shape
    lead = math.prod(shape[:-2]) if len(shape) > 2 else 1
    b, c = (shape[-2], shape[-1]) if len(shape) >= 2 else (1, shape[-1])
    t3 = (lead, b, c)
    padded_row = -(-b // 8) * 8 * -(-c // LANES) * LANES * 4
    ta = _tile(lead, max(1, (2 << 20) // padded_row))

    def body(w_ref, m_ref, v_ref, g_ref, d_ref, mo_ref, vo_ref):
        gv = g_ref[...]
        mn = ADAM_B1 * m_ref[...] + (1.0 - ADAM_B1) * gv
        vn = ADAM_B2 * v_ref[...] + (1.0 - ADAM_B2) * (gv * gv)
        m_hat = mn / (1.0 - ADAM_B1 ** ADAM_STEP)
        v_hat = vn / (1.0 - ADAM_B2 ** ADAM_STEP)
        d_ref[...] = -ADAM_LR * (m_hat / (jnp.sqrt(v_hat) + ADAM_EPS) + ADAM_WD * w_ref[...])
        mo_ref[...] = mn
        vo_ref[...] = vn

    blk = pl.BlockSpec((ta, b, c), lambda i: (i, 0, 0))
    s = jax.ShapeDtypeStruct(t3, F32)
    outs = pl.pallas_call(body, name=name, out_shape=(s, s, s), grid=(lead // ta,), in_specs=[blk] * 4, out_specs=(blk,) * 3,
                          compiler_params=_cparams("parallel"))(*[a.reshape(t3) for a in (w, m, v, g)])
    return tuple(o.reshape(shape) for o in outs)


def _adamw(w, m, v, gparts, name, dep=None):
    r, c = w.shape
    np_ = gparts.shape[0]
    tr = _tile(r, max(8, (1 << 18) // c), 8)

    def body(w_ref, m_ref, v_ref, g_ref, *rest):
        go_ref, d_ref, mo_ref, vo_ref = rest[-4:]
        g = g_ref[0].astype(F32)
        for p in range(1, np_):
            g = g + g_ref[p].astype(F32)
        mn = ADAM_B1 * m_ref[...] + (1.0 - ADAM_B1) * g
        vn = ADAM_B2 * v_ref[...] + (1.0 - ADAM_B2) * (g * g)
        m_hat = mn / (1.0 - ADAM_B1 ** ADAM_STEP)
        v_hat = vn / (1.0 - ADAM_B2 ** ADAM_STEP)
        go_ref[...] = g
        d_ref[...] = -ADAM_LR * (m_hat / (jnp.sqrt(v_hat) + ADAM_EPS) + ADAM_WD * w_ref[...])
        mo_ref[...] = mn
        vo_ref[...] = vn

    row = pl.BlockSpec((tr, c), lambda i: (i, 0))
    s = jax.ShapeDtypeStruct((r, c), F32)
    extra = [] if dep is None else [dep]
    return pl.pallas_call(body, name=name, out_shape=(s, s, s, s), grid=(r // tr,),
                          in_specs=[row, row, row, pl.BlockSpec((np_, tr, c), lambda i: (0, i, 0))] + [HBM] * len(extra),
                          out_specs=(row, row, row, row), compiler_params=_cparams("parallel"))(w, m, v, gparts, *extra)


def _sum_slots(parts, name):
    np_, r, c = parts.shape

    def body(p_ref, o_ref):
        g = p_ref[0]
        for p in range(1, np_):
            g = g + p_ref[p]
        o_ref[...] = g

    return pl.pallas_call(body, name=name, out_shape=jax.ShapeDtypeStruct((r, c), F32), compiler_params=_cparams())(parts)


def _block_diag(x, gb):
    nd, g, a, b = x.shape
    eye = jnp.eye(gb, dtype=x.dtype)
    y = jnp.einsum("dkgab,gh->dkgahb", x.reshape(nd, g // gb, gb, a, b), eye)
    return y.reshape(nd, g // gb, gb * a, gb * b)


def _block_diag_extract(y, gb, a, b):
    nd, nbk = y.shape[:2]
    eye = jnp.eye(gb, dtype=y.dtype)
    x = jnp.einsum("dkgahb,gh->dkgab", y.reshape(nd, nbk, gb, a, gb, b), eye)
    return x.reshape(nd, nbk * gb, a, b)


def kernel(x, c, ctx, c_ctx, w_ada, b_ada, w_in, sgu_ln_g, sgu_ln_b, w_spatial, b_spatial, s5_lam_re, s5_lam_im, s5_log_step, s5_b_re, s5_b_im, s5_c_re, s5_c_im, s5_d, w_glu, b_glu, w_out, ln_g, ln_b, loss_target, m_c_ctx, m_w_ada, m_b_ada, m_w_in, m_sgu_ln_g, m_sgu_ln_b, m_w_spatial, m_b_spatial, m_s5_lam_re, m_s5_lam_im, m_s5_log_step, m_s5_b_re, m_s5_b_im, m_s5_c_re, m_s5_c_im, m_s5_d, m_w_glu, m_b_glu, m_w_out, m_ln_g, m_ln_b, v_c_ctx, v_w_ada, v_b_ada, v_w_in, v_sgu_ln_g, v_sgu_ln_b, v_w_spatial, v_b_spatial, v_s5_lam_re, v_s5_lam_im, v_s5_log_step, v_s5_b_re, v_s5_b_im, v_s5_c_re, v_s5_c_im, v_s5_d, v_w_glu, v_b_glu, v_w_out, v_ln_g, v_ln_b):
    small_names = ["c_ctx", "b_ada", "sgu_ln_g", "sgu_ln_b", "w_spatial", "b_spatial", "s5_lam_re", "s5_lam_im",
                   "s5_log_step", "s5_b_re", "s5_b_im", "s5_c_re", "s5_c_im", "s5_d", "b_glu", "ln_g", "ln_b"]
    env = dict(locals())
    x2, tgt, ctx2 = x[0], loss_target[0], ctx[0]
    l, d = x2.shape
    lc = ctx2.shape[0]
    w = d // 2
    nh = w // HEAD_DIM_A
    nd, g_s5, p_s5, c_s5 = s5_b_re.shape[1:]
    ns = g_s5 * p_s5
    nsr = ns // LANES
    gb = LANES // c_s5
    me = _index(_mesh_pos())
    ada_cols = w_ada.shape[2]

    srows = _silu_rows(c, c_ctx)
    srows_all = _all_gather(srows, 0, "gather_silu")
    s_mat = jnp.concatenate([srows_all[0::8], srows_all[1:2], jnp.zeros((7, d), F32)], axis=0)
    mod_part = _small_dot(s_mat, w_ada[0], "nn", "mod_cols")
    mod_all = _all_gather(mod_part, 1, "gather_mod") + b_ada
    hw_in, tok_a = _exchange_start(w_in[0].astype(MXU_DTYPE), 1, "gather", "start_gather_w_in", (SIBLING,) + SAME_CORE_PEERS)
    mod_all = mod_all + tok_a[0, 0]
    mod_x = lax.dynamic_slice_in_dim(mod_all, me, 1, axis=0)
    mod_c = mod_all[8:9]
    shift_x, scale_x, gate_x = mod_x[:, :d], mod_x[:, d:2 * d], mod_x[:, 2 * d:]
    shift_c, scale_c = mod_c[:, :d], mod_c[:, d:2 * d]

    lr, li = s5_lam_re[0][:, :, None, :], s5_lam_im[0][:, :, None, :]
    ls = s5_log_step[0][:, :, None, None]
    swapped = ("s5_b_re", "s5_b_im")
    for nm in swapped:
        for pre in ("", "m_", "v_"):
            env[pre + nm] = jnp.swapaxes(env[pre + nm], -1, -2)
    br_t, bi_t = env["s5_b_re"][0], env["s5_b_im"][0]
    ab_re, ab_im, bb_re, bb_im = _disc_fwd(lr, li, ls, br_t, bi_t)
    a_sm = jnp.stack([ab_re, ab_im], axis=1).reshape(nd, 2, nsr, LANES)
    wbr = _block_diag(bb_re.astype(MXU_DTYPE), gb)
    wbi = _block_diag(bb_im.astype(MXU_DTYPE), gb)
    cre_t = _block_diag(s5_c_re[0].astype(MXU_DTYPE), gb)
    ncim_t = _block_diag((-s5_c_im[0]).astype(MXU_DTYPE), gb)
    cre, ncim = jnp.swapaxes(cre_t, 2, 3), jnp.swapaxes(ncim_t, 2, 3)
    wbr_t, wbi_t = jnp.swapaxes(wbr, 2, 3), jnp.swapaxes(wbi, 2, 3)
    d_skip = s5_d

    xm = _ln_mod(x2, shift_x, scale_x, "ln_mod_x")
    cm = _ln_mod(ctx2, shift_c, scale_c, "ln_mod_ctx")
    ready = xm[:8, :LANES].astype(F32) + cm[:8, :LANES].astype(F32) + cre[0, 0, :8, :].astype(F32)
    hw_in2, tok_b = _forward_start(_exchange_wait(hw_in, ready, "wait_gather_w_in"), 1, "start_forward_w_in")
    w_in_f = _forward_wait(hw_in2, tok_b, "wait_forward_w_in")
    hw_glu, tok_c = _exchange_start(w_glu[0].astype(MXU_DTYPE), 0, "gather", "start_gather_w_glu")
    hw_out, tok_o = _exchange_start(w_out[0].astype(MXU_DTYPE), 0, "gather", "start_gather_w_out")
    proj = _matmul(xm, w_in_f, mode="nn", name="proj", dep=tok_c + tok_o)
    ub_c = _matmul(cm, w_in_f, mode="nn", name="proj_ctx", b_n0=3 * w, n=w)
    bsf = jnp.repeat(b_spatial[0].T, HEAD_DIM_A, axis=1)
    ws = w_spatial[0]
    cat = _ga_fwd(proj, sgu_ln_g, sgu_ln_b, ws, bsf, w)
    zeros_state = jnp.zeros((2, nsr, LANES), F32)
    s5c, s5l = [], []
    for dr in range(nd):
        s5c.append(_s5_fwd(ub_c, 0, w, zeros_state, a_sm, wbr, wbi, cre, ncim, dr, f"s5_fwd_ctx{dr}"))
        s5l.append(_s5_fwd(proj, 3, w, s5c[dr][5], a_sm, wbr, wbi, cre, ncim, dr, f"s5_fwd{dr}"))
    w_glu_f = _exchange_wait(hw_glu, s5l[1][0], "wait_gather_w_glu")
    ys, cat = _glu_fwd(s5l[0][0], s5l[1][0], proj, cat, d_skip, w_glu_f, b_glu, w)
    w_out_f = _exchange_wait(hw_out, ys, "wait_gather_w_out")
    out = _matmul(cat, w_out_f, mode="nn", name="out_proj")
    loss_row, dout, dx_res, dgate, dln_g, dln_b = _post_ln_loss(x2, out, gate_x, ln_g, ln_b, tgt)

    dcat = _matmul(dout, w_out_f, mode="nt", name="d_cat", out_dtype=MXU_DTYPE)
    dw_out = _matmul(cat, dout, mode="tn", name="d_w_out", out_dtype=MXU_DTYPE)
    hg_out, tok_d = _exchange_start(dw_out, 0, "a2a", "start_a2a_d_w_out")
    dys, dproj, db_glu, dd_skip, dw_glu = _glu_bwd(dcat, ys, proj, w_glu_f, b_glu, w, tok_d)
    hg_glu, tok_e = _exchange_start(dw_glu.astype(MXU_DTYPE), 0, "a2a", "start_a2a_d_w_glu")
    zeros_state = zeros_state + tok_e[0, 0]
    du_l, du_c, dwbr, dwbi, dcre, dncim, da_sm = [], [], [], [], [], [], []
    nbk, spb = w // LANES, gb * p_s5
    for dr in range(nd):
        bl = _s5_bwd(dys, proj, 3, w, *s5l[dr][1:5], s5c[dr][5], zeros_state, a_sm, wbr_t, wbi_t,
                     cre_t, ncim_t, dr, f"s5_bwd{dr}")
        bc = _s5_bwd(None, ub_c, 0, w, s5c[dr][1], s5c[dr][2], None, None, zeros_state, bl[6], a_sm, wbr_t, wbi_t,
                     None, None, dr, f"s5_bwd_ctx{dr}")
        du_l.append(bl[0])
        du_c.append(bc[0])
        dwbr.append(_add2(bl[1].reshape(nbk * LANES, spb), bc[1].reshape(nbk * LANES, spb), f"sum_dwbr{dr}"))
        dwbi.append(_add2(bl[2].reshape(nbk * LANES, spb), bc[2].reshape(nbk * LANES, spb), f"sum_dwbi{dr}"))
        dcre.append(bl[3])
        dncim.append(bl[4])
        da_sm.append(_add2(bl[5].reshape(2 * nsr, LANES), bc[3].reshape(2 * nsr, LANES), f"sum_da{dr}"))
    dub_c = _add2(du_c[0], du_c[1], "dub_ctx")
    dwbr = jnp.stack(dwbr).reshape(nd, nbk, LANES, spb)
    dwbi = jnp.stack(dwbi).reshape(nd, nbk, LANES, spb)
    dcre, dncim = jnp.stack(dcre), jnp.stack(dncim)
    da_sm = jnp.stack(da_sm).reshape(nd, 2, g_s5, p_s5)
    dproj, dsg, dsb, dws, dbsf = _ga_bwd(proj, dcat, dproj, sgu_ln_g, sgu_ln_b, ws, bsf, dys, du_l[0], du_l[1], d_skip, w)

    dbb_re = _block_diag_extract(dwbr, gb, c_s5, p_s5)
    dbb_im = _block_diag_extract(dwbi, gb, c_s5, p_s5)
    dc_re = jnp.swapaxes(_block_diag_extract(dcre, gb, p_s5, c_s5), 2, 3)
    dc_im = -jnp.swapaxes(_block_diag_extract(dncim, gb, p_s5, c_s5), 2, 3)
    dlr, dli, dls, db_re, db_im = _disc_bwd(lr, li, ls, br_t, bi_t, da_sm[:, 0:1].reshape(nd, g_s5, 1, p_s5),
                                            da_sm[:, 1:2].reshape(nd, g_s5, 1, p_s5), dbb_re, dbb_im)
    expand = (jnp.arange(w)[:, None] // HEAD_DIM_A == jnp.arange(LANES)[None, :]).astype(F32)
    db_sp = _lane_group_sum(dbsf, expand, "d_b_spatial")[:, :nh].T

    local = {"sgu_ln_g": dsg, "sgu_ln_b": dsb, "w_spatial": dws, "b_spatial": db_sp,
             "s5_lam_re": dlr, "s5_lam_im": dli, "s5_log_step": dls, "s5_b_re": db_re, "s5_b_im": db_im,
             "s5_c_re": dc_re, "s5_c_im": dc_im, "s5_d": dd_skip, "b_glu": db_glu, "ln_g": dln_g, "ln_b": dln_b}
    reduced = sorted(local, key=lambda n: -math.prod(env[n].shape))
    loss_part = (0.5 / d) * jnp.sum(loss_row)
    flat = jnp.concatenate([local[n].reshape(-1) for n in reduced] + [loss_part.reshape(1)])
    unit = N_DEV * 8 * LANES
    total = -(-flat.shape[0] // unit) * unit
    flat = jnp.pad(flat, (0, total - flat.shape[0])).reshape(N_DEV * 8, total // (N_DEV * 8))
    h_small, tok_s = _exchange_start(flat, 0, "a2a", "start_a2a_small")

    dw_in = _matmul(xm, dproj, mode="tn", name="d_w_in", out_dtype=MXU_DTYPE, dep=tok_s)
    dw_in = _matmul(cm, dub_c, mode="tn", name="d_w_in_ctx", acc_in=dw_in, acc_n0=3 * w, out_dtype=MXU_DTYPE)
    hg_in, tok_f = _exchange_start(dw_in, 1, "a2a", "start_a2a_d_w_in")
    mine = _sum_slots(_exchange_wait(h_small, dw_in, "wait_a2a_small"), "sum_small")
    h_sums, tok_g = _exchange_start(mine, 0, "gather", "start_gather_small")
    dxm = _matmul(dproj, w_in_f, mode="nt", name="d_xm", dep=tok_f + tok_g, out_dtype=MXU_DTYPE)
    dcm = _matmul(dub_c, w_in_f, mode="nt", name="d_cm", b_k0=3 * w, k=w)
    grad_x, dshift_x, dscale_x = _ln_mod_bwd(x2, dxm, scale_x, dx_res, "ln_mod_x_bwd")
    _, dshift_c, dscale_c = _ln_mod_bwd(ctx2, dcm, scale_c, None, "ln_mod_ctx_bwd")

    dmod_rows = jnp.concatenate([jnp.concatenate([dshift_x, dscale_x, dgate], axis=1),
                                 jnp.concatenate([dshift_c, dscale_c, jnp.zeros((1, d), F32)], axis=1),
                                 jnp.zeros((6, 3 * d), F32)], axis=0)
    h_dmod, tok_m = _exchange_start(dmod_rows, 0, "gather", "start_gather_dmod")

    gp_w_out = _exchange_wait(hg_out, tok_m, "wait_a2a_d_w_out")
    gp_w_glu = _exchange_wait(hg_glu, tok_m, "wait_a2a_d_w_glu")
    gp_w_in = _exchange_wait(hg_in, tok_m, "wait_a2a_d_w_in")
    big = {
        "w_in": _adamw(w_in[0], m_w_in[0], v_w_in[0], gp_w_in, "adamw_w_in"),
        "w_glu": _adamw(w_glu[0], m_w_glu[0], v_w_glu[0], gp_w_glu, "adamw_w_glu"),
        "w_out": _adamw(w_out[0], m_w_out[0], v_w_out[0], gp_w_out, "adamw_w_out"),
    }
    dmod_all = _exchange_wait(h_dmod, big["w_out"][0], "wait_gather_dmod")
    dmod_ctx = _sum_slots(dmod_all[1::8].reshape(N_DEV, 1, 3 * d), "sum_dmod_ctx")
    dmod_mat = jnp.concatenate([dmod_all[0::8], dmod_ctx, jnp.zeros((7, 3 * d), F32)], axis=0)
    db_ada = _sum_slots(dmod_mat[:9].reshape(9, 1, 3 * d), "sum_db_ada")
    dmod_mine = lax.dynamic_slice_in_dim(dmod_mat, me * ada_cols, ada_cols, axis=1)
    dw_ada = _small_dot(s_mat, dmod_mine, "tn", "d_w_ada")
    dsilu_cc = _small_dot(dmod_mine[8:16], w_ada[0], "nt", "d_silu_cctx")[0:1]
    dc_ctx_part = dsilu_cc * _silu_grad(c_ctx.reshape(1, d))
    dc_ctx_rows = jnp.concatenate([dc_ctx_part, jnp.zeros((7, d), F32)], axis=0)
    h_cctx, tok_c2 = _exchange_start(dc_ctx_rows, 0, "gather", "start_gather_d_c_ctx")

    big["w_ada"] = _adamw(w_ada[0], m_w_ada[0], v_w_ada[0], dw_ada[None], "adamw_w_ada", dep=tok_c2)
    summed = _exchange_wait(h_sums, big["w_ada"][0], "wait_gather_small").reshape(-1)
    grads, off = {"b_ada": db_ada}, 0
    for n in reduced:
        size = math.prod(env[n].shape)
        grads[n] = summed[off:off + size].reshape(env[n].shape)
        off += size
    loss = summed[off]
    res = {n: tuple(a[None] for a in big[n]) for n in big}

    def small_step(n):
        res[n] = (grads[n],) + _adamw_nd(env[n], env["m_" + n], env["v_" + n], grads[n], "adamw_" + n)
        if n in swapped:
            res[n] = tuple(jnp.swapaxes(a, -1, -2) for a in res[n])

    for n in small_names:
        if n != "c_ctx":
            small_step(n)
    dc_ctx_all = _exchange_wait(h_cctx, res["w_spatial"][1], "wait_gather_d_c_ctx")
    grads["c_ctx"] = _sum_slots(dc_ctx_all[0::8].reshape(N_DEV, 1, d), "sum_d_c_ctx").reshape(d)
    small_step("c_ctx")

    order = ["c_ctx", "w_ada", "b_ada", "w_in", "sgu_ln_g", "sgu_ln_b", "w_spatial", "b_spatial", "s5_lam_re", "s5_lam_im",
             "s5_log_step", "s5_b_re", "s5_b_im", "s5_c_re", "s5_c_im", "s5_d", "w_glu", "b_glu", "w_out", "ln_g", "ln_b"]
    return (loss, grad_x[None], *[res[n][0] for n in order], *[res[n][1] for n in order],
            *[res[n][2] for n in order], *[res[n][3] for n in order])
```

```python
import functools
import math

import jax
import jax.numpy as jnp
from jax import lax
from jax.experimental import pallas as pl
from jax.experimental.pallas import tpu as pltpu

F32 = jnp.float32
MXU_DTYPE = jnp.bfloat16
N_DEV = 8
MESH_ID = pl.DeviceIdType.MESH
LN_EPS = 1e-6
DEPTH = 1
ALPHA = (2.0 * DEPTH) ** 0.25
CHUNK = 128
HEAD_DIM_A = 128
ADAM_LR, ADAM_B1, ADAM_B2, ADAM_EPS, ADAM_WD, ADAM_STEP = 0.001, 0.9, 0.999, 1e-08, 0.01, 10
LANES = 128
SCAN_UNROLL = 8
VMEM_LIMIT = 56 * 1024 * 1024
HBM = pl.BlockSpec(memory_space=pl.ANY)


def _cparams(*sem):
    return pltpu.CompilerParams(dimension_semantics=sem if sem else None, vmem_limit_bytes=VMEM_LIMIT)


def _tile(n, pref, mult=1):
    if n <= pref:
        return n
    t = pref - pref % mult
    while n % t:
        t -= mult
    return t


def _gelu(x):
    return 0.5 * x * (1.0 + lax.erf(x * (1.0 / math.sqrt(2.0))))


def _gelu_grad(x):
    return 0.5 * (1.0 + lax.erf(x * (1.0 / math.sqrt(2.0)))) + x * jnp.exp(-0.5 * x * x) * (1.0 / math.sqrt(2.0 * math.pi))


def _silu_grad(x):
    s = jax.nn.sigmoid(x)
    return s * (1.0 + x * (1.0 - s))


def _mxu_dot(a, b, dims=(((1,), (0,)), ((), ()))):
    return lax.dot_general(a.astype(MXU_DTYPE), b.astype(MXU_DTYPE), dims, preferred_element_type=F32)


_NT = (((1,), (1,)), ((), ()))
_TN = (((0,), (0,)), ((), ()))


def _mesh_pos():
    return lax.axis_index("x"), lax.axis_index("y"), lax.axis_index("c")


def _peer(pos, r):
    x, y, c = pos
    return ((1 - x) if r & 4 else x, (1 - y) if r & 2 else y, (1 - c) if r & 1 else c)


def _index(pos):
    return 4 * pos[0] + 2 * pos[1] + pos[2]


def _slice_of(ref, axis, idx, size):
    start = idx * size
    if axis == 0:
        return ref.at[pl.ds(start, size)]
    return ref.at[:, pl.ds(start, size)]


def _all_gather(x, axis, name):
    size = x.shape[axis]
    out_shape = tuple(s * N_DEV if a == axis else s for a, s in enumerate(x.shape))

    def body(x_ref, o_ref, send_sems, recv_sems, local_sem):
        me = _mesh_pos()
        mine = pltpu.make_async_copy(x_ref, _slice_of(o_ref, axis, _index(me), size), local_sem)
        mine.start()

        def copy(r, block):
            return pltpu.make_async_remote_copy(
                src_ref=x_ref, dst_ref=_slice_of(o_ref, axis, _index(block), size),
                send_sem=send_sems.at[r - 1], recv_sem=recv_sems.at[r - 1],
                device_id=_peer(me, r), device_id_type=MESH_ID)

        sends = [copy(r, me) for r in range(1, N_DEV)]
        for cp in sends:
            cp.start()
        for r in range(1, N_DEV):
            copy(r, _peer(me, r)).wait_recv()
        for cp in sends:
            cp.wait_send()
        mine.wait()

    return pl.pallas_call(
        body, name=name, out_shape=jax.ShapeDtypeStruct(out_shape, x.dtype),
        in_specs=[HBM], out_specs=HBM,
        scratch_shapes=[pltpu.SemaphoreType.DMA((N_DEV - 1,)), pltpu.SemaphoreType.DMA((N_DEV - 1,)),
                        pltpu.SemaphoreType.DMA],
    )(x)


_SEM = pl.BlockSpec(memory_space=pltpu.SEMAPHORE)
_HBM = pl.BlockSpec(memory_space=pltpu.HBM)
_EFFECT = pltpu.SideEffectType.DATAFLOW_SIDE_EFFECTING
ALL_PEERS = tuple(range(1, N_DEV))
SIBLING = 1
SAME_CORE_PEERS = (2, 4, 6)


def _exchange_copy(kind, x_ref, land_ref, axis, size, send_sems, recv_sems, me, rels, q, arriving):
    peer = _peer(me, rels[q])
    sender, receiver = (peer, me) if arriving else (me, peer)
    if kind == "gather":
        src, dst = x_ref, _slice_of(land_ref, axis, _index(sender), size)
    else:
        src, dst = _slice_of(x_ref, axis, _index(receiver), size), land_ref.at[_index(sender)]
    return pltpu.make_async_remote_copy(src_ref=src, dst_ref=dst, send_sem=send_sems.at[q], recv_sem=recv_sems.at[q],
                                        device_id=peer, device_id_type=MESH_ID)


def _local_copy(kind, x_ref, land_ref, axis, size, me, local_sem):
    if kind == "gather":
        return pltpu.make_async_copy(x_ref, _slice_of(land_ref, axis, _index(me), size), local_sem)
    return pltpu.make_async_copy(_slice_of(x_ref, axis, _index(me), size), land_ref.at[_index(me)], local_sem)


def _exchange_start(x, axis, kind, name, rels=ALL_PEERS):
    size = x.shape[axis] if kind == "gather" else x.shape[axis] // N_DEV
    if kind == "gather":
        land_shape = tuple(s * N_DEV if a == axis else s for a, s in enumerate(x.shape))
    else:
        land_shape = (N_DEV,) + tuple(size if a == axis else s for a, s in enumerate(x.shape))

    def body(x_ref, land_ref, send_sems, recv_sems, local_sem, x_thru, land_thru, token):
        del x_thru, land_thru
        me = _mesh_pos()
        _local_copy(kind, x_ref, land_ref, axis, size, me, local_sem).start()
        for q in range(len(rels)):
            _exchange_copy(kind, x_ref, land_ref, axis, size, send_sems, recv_sems, me, rels, q, False).start()
        token[...] = jnp.zeros_like(token)

    sems = pltpu.SemaphoreType.DMA((len(rels),))
    send_sems, recv_sems, local_sem, x_thru, land_thru, token = pl.pallas_call(
        body, name=name,
        out_shape=(sems, sems, pltpu.SemaphoreType.DMA(()), pltpu.HBM(x.shape, x.dtype), pltpu.HBM(land_shape, x.dtype),
                   jax.ShapeDtypeStruct((8, LANES), F32)),
        in_specs=(_HBM, _HBM), out_specs=(_SEM, _SEM, _SEM, _HBM, _HBM, pl.BlockSpec(memory_space=pltpu.VMEM)),
        input_output_aliases={0: 3, 1: 4}, compiler_params=pltpu.CompilerParams(has_side_effects=_EFFECT),
    )(pltpu.with_memory_space_constraint(x, pltpu.HBM),
      pltpu.with_memory_space_constraint(lax.empty(land_shape, x.dtype), pltpu.HBM))
    return (kind, axis, size, rels, send_sems, recv_sems, local_sem, x_thru, land_thru), token


def _exchange_wait(handle, after, name):
    kind, axis, size, rels, send_sems, recv_sems, local_sem, x_thru, land_thru = handle

    def body(x_ref, land_ref, send_sems, recv_sems, local_sem, after_ref, x_dead, got_ref):
        del after_ref, x_dead, got_ref
        me = _mesh_pos()
        _local_copy(kind, x_ref, land_ref, axis, size, me, local_sem).wait()
        for q in range(len(rels)):
            _exchange_copy(kind, x_ref, land_ref, axis, size, send_sems, recv_sems, me, rels, q, False).wait_send()
        for q in range(len(rels)):
            _exchange_copy(kind, x_ref, land_ref, axis, size, send_sems, recv_sems, me, rels, q, True).wait_recv()

    return pl.pallas_call(
        body, name=name, out_shape=(pltpu.HBM(x_thru.shape, x_thru.dtype), pltpu.HBM(land_thru.shape, land_thru.dtype)),
        in_specs=(_HBM, _HBM, _SEM, _SEM, _SEM, HBM), out_specs=(_HBM, _HBM), input_output_aliases={0: 0, 1: 1},
        compiler_params=pltpu.CompilerParams(has_side_effects=_EFFECT),
    )(x_thru, land_thru, send_sems, recv_sems, local_sem, after)[1]


def _forward_copy(land_ref, axis, size, send_sems, recv_sems, me, q, arriving):
    sibling = _peer(me, SIBLING)
    owner = _peer(sibling if arriving else me, SAME_CORE_PEERS[q])
    block = _slice_of(land_ref, axis, _index(owner), size)
    return pltpu.make_async_remote_copy(src_ref=block, dst_ref=block, send_sem=send_sems.at[q], recv_sem=recv_sems.at[q],
                                        device_id=sibling, device_id_type=MESH_ID)


def _forward_start(land, axis, name):
    size = land.shape[axis] // N_DEV

    def body(land_ref, send_sems, recv_sems, land_thru, token):
        del land_thru
        me = _mesh_pos()
        for q in range(len(SAME_CORE_PEERS)):
            _forward_copy(land_ref, axis, size, send_sems, recv_sems, me, q, False).start()
        token[...] = jnp.zeros_like(token)

    sems = pltpu.SemaphoreType.DMA((len(SAME_CORE_PEERS),))
    send_sems, recv_sems, land_thru, token = pl.pallas_call(
        body, name=name, out_shape=(sems, sems, pltpu.HBM(land.shape, land.dtype), jax.ShapeDtypeStruct((8, LANES), F32)),
        in_specs=(_HBM,), out_specs=(_SEM, _SEM, _HBM, pl.BlockSpec(memory_space=pltpu.VMEM)),
        input_output_aliases={0: 2}, compiler_params=pltpu.CompilerParams(has_side_effects=_EFFECT),
    )(land)
    return (axis, size, send_sems, recv_sems, land_thru), token


def _forward_wait(handle, after, name):
    axis, size, send_sems, recv_sems, land_thru = handle

    def body(land_ref, send_sems, recv_sems, after_ref, got_ref):
        del after_ref, got_ref
        me = _mesh_pos()
        for q in range(len(SAME_CORE_PEERS)):
            _forward_copy(land_ref, axis, size, send_sems, recv_sems, me, q, False).wait_send()
        for q in range(len(SAME_CORE_PEERS)):
            _forward_copy(land_ref, axis, size, send_sems, recv_sems, me, q, True).wait_recv()

    return pl.pallas_call(
        body, name=name, out_shape=pltpu.HBM(land_thru.shape, land_thru.dtype),
        in_specs=(_HBM, _SEM, _SEM, HBM), out_specs=_HBM, input_output_aliases={0: 0},
        compiler_params=pltpu.CompilerParams(has_side_effects=_EFFECT),
    )(land_thru, send_sems, recv_sems, after)


WHOLE = 1 << 30
MATMUL_TILES = {
    "proj": (1024, 1024, WHOLE), "proj_ctx": (256, WHOLE, WHOLE), "out_proj": (1024, 1024, WHOLE),
    "d_cat": (1024, 1024, WHOLE), "d_w_out": (1024, 1024, 2048), "d_w_in": (1024, 1280, 2048),
    "d_w_in_ctx": (1024, WHOLE, WHOLE), "d_xm": (1024, 512, WHOLE), "d_cm": (256, 1024, WHOLE),
}


def _matmul(a, b, *, mode, name, out_dtype=F32, b_n0=0, n=None, b_k0=0, k=None, acc_in=None, acc_n0=0, dep=None):
    bm, bn, bk = MATMUL_TILES[name]
    if mode == "tn":
        kk, m = a.shape
    else:
        m, kk = a.shape
    if mode == "nn":
        n = b.shape[1] if n is None else n
    elif mode == "nt":
        n = b.shape[0]
        kk = kk if k is None else k
    else:
        n = b.shape[1]
    bm, bn, bk = _tile(m, bm), _tile(n, bn), _tile(kk, bk)
    nk = kk // bk
    assert b_n0 % bn == 0 and b_k0 % bk == 0 and acc_n0 % bn == 0
    dims = {"nn": (((1,), (0,)), ((), ())), "nt": _NT, "tn": _TN}[mode]

    n_in = 2 + (acc_in is not None) + (dep is not None)

    def body(*refs):
        a_ref, b_ref = refs[:2]
        init = refs[2] if acc_in is not None else None
        o_ref = refs[n_in]
        acc_ref = refs[-1] if nk > 1 else None
        p = _mxu_dot(a_ref[...], b_ref[...], dims)
        if nk == 1:
            o_ref[...] = (p if init is None else p + init[...]).astype(out_dtype)
            return
        ki = pl.program_id(2)

        @pl.when(ki == 0)
        def _():
            acc_ref[...] = p if init is None else p + init[...]

        @pl.when(ki > 0)
        def _():
            acc_ref[...] += p

        @pl.when(ki == nk - 1)
        def _():
            o_ref[...] = acc_ref[...].astype(out_dtype)

    a_spec = pl.BlockSpec((bk, bm), lambda j, i, q: (q, i)) if mode == "tn" else pl.BlockSpec((bm, bk), lambda j, i, q: (i, q))
    if mode == "nt":
        b_spec = pl.BlockSpec((bn, bk), lambda j, i, q: (j, q + b_k0 // bk))
    else:
        b_spec = pl.BlockSpec((bk, bn), lambda j, i, q: (q, j + b_n0 // bn))
    in_specs, args, aliases = [a_spec, b_spec], [a, b], {}
    out_map = lambda j, i, q: (i, j + acc_n0 // bn)
    if acc_in is not None:
        in_specs.append(pl.BlockSpec((bm, bn), out_map))
        args.append(acc_in)
        aliases = {2: 0}
        out_shape = jax.ShapeDtypeStruct(acc_in.shape, out_dtype)
    else:
        out_shape = jax.ShapeDtypeStruct((m, n), out_dtype)
    if dep is not None:
        in_specs.append(HBM)
        args.append(dep)
    return pl.pallas_call(
        body, name=name, out_shape=out_shape, grid=(n // bn, m // bm, nk),
        in_specs=in_specs, out_specs=pl.BlockSpec((bm, bn), out_map),
        scratch_shapes=[pltpu.VMEM((bm, bn), F32)] if nk > 1 else [],
        input_output_aliases=aliases,
        compiler_params=_cparams("parallel", "parallel", "arbitrary"),
    )(*args)


def _silu_rows(c, c_ctx):
    d = c.shape[-1]

    def body(c_ref, cc_ref, o_ref):
        o_ref[...] = jnp.zeros_like(o_ref)
        o_ref[0:1, :] = jax.nn.silu(c_ref[...])
        o_ref[1:2, :] = jax.nn.silu(cc_ref[...])

    return pl.pallas_call(body, name="silu_rows", out_shape=jax.ShapeDtypeStruct((8, d), F32))(
        c.reshape(1, d), c_ctx.reshape(1, d))


def _small_dot(a, b, mode, name):
    dims = {"nn": (((1,), (0,)), ((), ())), "nt": _NT, "tn": _TN}[mode]
    m = a.shape[1] if mode == "tn" else a.shape[0]
    n = b.shape[0] if mode == "nt" else b.shape[1]

    def body(a_ref, b_ref, o_ref):
        o_ref[...] = lax.dot_general(a_ref[...], b_ref[...], dims, preferred_element_type=F32,
                                     precision=lax.Precision.HIGHEST)

    return pl.pallas_call(body, name=name, out_shape=jax.ShapeDtypeStruct((m, n), F32),
                          compiler_params=_cparams())(a, b)


def _ln_stats(x):
    mu = jnp.mean(x, axis=-1, keepdims=True)
    xc = x - mu
    var = jnp.mean(xc * xc, axis=-1, keepdims=True)
    rstd = lax.rsqrt(var + LN_EPS)
    return xc * rstd, rstd


def _ln_mod(x, shift, scale, name):
    l, d = x.shape
    tl = _tile(l, 512)

    def body(x_ref, sh_ref, sc_ref, o_ref):
        xhat, _ = _ln_stats(x_ref[...])
        o_ref[...] = (xhat * (1.0 + sc_ref[...]) + sh_ref[...]).astype(o_ref.dtype)

    row = pl.BlockSpec((tl, d), lambda i: (i, 0))
    vec = pl.BlockSpec((1, d), lambda i: (0, 0))
    return pl.pallas_call(body, name=name, out_shape=jax.ShapeDtypeStruct((l, d), MXU_DTYPE), grid=(l // tl,),
                          in_specs=[row, vec, vec], out_specs=row, compiler_params=_cparams("parallel"))(x, shift, scale)


def _ln_mod_bwd(x, dxm, scale, res, name):
    l, d = x.shape
    tl = _tile(l, 512)
    with_res = res is not None

    def body(*refs):
        if with_res:
            x_ref, g_ref, sc_ref, r_ref, dx_ref, dsh_ref, dsc_ref = refs
        else:
            x_ref, g_ref, sc_ref, dx_ref, dsh_ref, dsc_ref = refs
        i = pl.program_id(0)
        xhat, rstd = _ln_stats(x_ref[...])
        g = g_ref[...].astype(F32)
        dxh = g * (1.0 + sc_ref[...])
        dx = rstd * (dxh - jnp.mean(dxh, axis=-1, keepdims=True) - xhat * jnp.mean(dxh * xhat, axis=-1, keepdims=True))
        dx_ref[...] = dx + r_ref[...].astype(F32) if with_res else dx

        @pl.when(i == 0)
        def _():
            dsh_ref[...] = jnp.zeros_like(dsh_ref)
            dsc_ref[...] = jnp.zeros_like(dsc_ref)

        dsh_ref[...] += jnp.sum(g, axis=0, keepdims=True)
        dsc_ref[...] += jnp.sum(g * xhat, axis=0, keepdims=True)

    row = pl.BlockSpec((tl, d), lambda i: (i, 0))
    vec = pl.BlockSpec((1, d), lambda i: (0, 0))
    args = [x, dxm, scale] + ([res] if with_res else [])
    return pl.pallas_call(
        body, name=name,
        out_shape=(jax.ShapeDtypeStruct((l, d), F32), jax.ShapeDtypeStruct((1, d), F32), jax.ShapeDtypeStruct((1, d), F32)),
        grid=(l // tl,), in_specs=[row, row, vec] + ([row] if with_res else []), out_specs=(row, vec, vec),
        compiler_params=_cparams("arbitrary"))(*args)


def _post_ln_loss(x, out, gate, ln_g, ln_b, target):
    l, d = x.shape
    tl = _tile(l, 512)

    def body(x_ref, o_ref, gate_ref, g_ref, b_ref, t_ref, loss_ref, dout_ref, dxr_ref, dgate_ref, dg_ref, db_ref):
        i = pl.program_id(0)
        out_t = o_ref[...]
        gate_v = gate_ref[...]
        rhat, rstd = _ln_stats(ALPHA * x_ref[...] + gate_v * out_t)
        ln_gv = g_ref[...]
        diff = rhat * ln_gv + b_ref[...] - t_ref[...]
        dy = diff * (1.0 / d)
        drh = dy * ln_gv
        dr = rstd * (drh - jnp.mean(drh, axis=-1, keepdims=True) - rhat * jnp.mean(drh * rhat, axis=-1, keepdims=True))
        dout_ref[...] = (gate_v * dr).astype(dout_ref.dtype)
        dxr_ref[...] = (ALPHA * dr).astype(dxr_ref.dtype)

        @pl.when(i == 0)
        def _():
            for r in (loss_ref, dgate_ref, dg_ref, db_ref):
                r[...] = jnp.zeros_like(r)

        loss_ref[...] += jnp.sum(diff * diff, axis=0, keepdims=True)
        dgate_ref[...] += jnp.sum(dr * out_t, axis=0, keepdims=True)
        dg_ref[...] += jnp.sum(dy * rhat, axis=0, keepdims=True)
        db_ref[...] += jnp.sum(dy, axis=0, keepdims=True)

    row = pl.BlockSpec((tl, d), lambda i: (i, 0))
    vec = pl.BlockSpec((1, d), lambda i: (0, 0))
    v = jax.ShapeDtypeStruct((1, d), F32)
    return pl.pallas_call(
        body, name="post_ln_loss",
        out_shape=(v, jax.ShapeDtypeStruct((l, d), MXU_DTYPE), jax.ShapeDtypeStruct((l, d), MXU_DTYPE), v, v, v),
        grid=(l // tl,), in_specs=[row, row, vec, vec, vec, row], out_specs=(vec, row, row, vec, vec, vec),
        compiler_params=_cparams("arbitrary"))(x, out, gate, ln_g, ln_b, target)


def _ga_forward_tile(p, g, b, ws_ref, bsf, w, nc, nh):
    u_raw, v_raw, za = p[:, :w], p[:, w:2 * w], p[:, 2 * w:3 * w]
    gu = _gelu(u_raw)
    vhat, rstd = _ln_stats(_gelu(v_raw))
    vn = vhat * g + b
    rows = []
    for ci in range(nc):
        r0 = ci * CHUNK
        heads = [_mxu_dot(ws_ref[h], vn[r0:r0 + CHUNK, h * HEAD_DIM_A:(h + 1) * HEAD_DIM_A]) for h in range(nh)]
        rows.append(jnp.concatenate(heads, axis=1) + bsf)
    mixed = jnp.concatenate(rows, axis=0) if nc > 1 else rows[0]
    return u_raw, v_raw, za, gu, vhat, rstd, vn, mixed


def _ga_fwd(proj, g, b, ws, bsf, w):
    l = proj.shape[0]
    nh = w // HEAD_DIM_A
    nc = _tile(l // CHUNK, 2)
    tl = nc * CHUNK

    def body(p_ref, g_ref, b_ref, ws_ref, bsf_ref, o_ref):
        _, _, za, gu, _, _, _, mixed = _ga_forward_tile(p_ref[...], g_ref[...], b_ref[...], ws_ref, bsf_ref[...], w, nc, nh)
        o_ref[...] = (gu * mixed * jax.nn.silu(za)).astype(o_ref.dtype)

    vec = pl.BlockSpec((1, w), lambda i: (0, 0))
    return pl.pallas_call(
        body, name="ga_fwd", out_shape=jax.ShapeDtypeStruct((l, 2 * w), MXU_DTYPE), grid=(l // tl,),
        in_specs=[pl.BlockSpec((tl, 3 * w), lambda i: (i, 0)), vec, vec,
                  pl.BlockSpec((nh, CHUNK, CHUNK), lambda i: (0, 0, 0)), pl.BlockSpec((CHUNK, w), lambda i: (0, 0))],
        out_specs=pl.BlockSpec((tl, w), lambda i: (i, 0)), compiler_params=_cparams("parallel"))(proj, g, b, ws, bsf)


def _ga_bwd(proj, dcat, dproj, g, b, ws, bsf, dys, du0, du1, d_skip, w):
    l = proj.shape[0]
    nh = w // HEAD_DIM_A
    nc = _tile(l // CHUNK, 2)
    tl = nc * CHUNK

    def body(p_ref, dy_ref, dp_in, g_ref, b_ref, ws_ref, bsf_ref, dys_ref, du0_ref, du1_ref, d_ref,
             dp_ref, dg_ref, db_ref, dws_ref, dbsf_ref):
        del dp_in
        i = pl.program_id(0)
        dp_ref[:, 3 * w:] = (dys_ref[...] * d_ref[...] + du0_ref[...] + du1_ref[...]).astype(dp_ref.dtype)
        gv = g_ref[...]
        u_raw, v_raw, za, gu, vhat, rstd, vn, mixed = _ga_forward_tile(
            p_ref[...], gv, b_ref[...], ws_ref, bsf_ref[...], w, nc, nh)
        dya = dy_ref[...].astype(F32)
        sz = jax.nn.silu(za)
        dmixed = dya * gu * sz
        dza = dya * gu * mixed * _silu_grad(za)
        dgu = dya * mixed * sz

        @pl.when(i == 0)
        def _():
            for r in (dg_ref, db_ref, dws_ref, dbsf_ref):
                r[...] = jnp.zeros_like(r)

        rows = []
        for ci in range(nc):
            r0 = ci * CHUNK
            heads = []
            for h in range(nh):
                cols = slice(h * HEAD_DIM_A, (h + 1) * HEAD_DIM_A)
                dm = dmixed[r0:r0 + CHUNK, cols]
                heads.append(_mxu_dot(ws_ref[h], dm, _TN))
                dws_ref[h] += _mxu_dot(dm, vn[r0:r0 + CHUNK, cols], _NT)
            rows.append(jnp.concatenate(heads, axis=1))
            dbsf_ref[...] += dmixed[r0:r0 + CHUNK, :]
        dvn = jnp.concatenate(rows, axis=0) if nc > 1 else rows[0]
        dg_ref[...] += jnp.sum(dvn * vhat, axis=0, keepdims=True)
        db_ref[...] += jnp.sum(dvn, axis=0, keepdims=True)
        dvh = dvn * gv
        dgv = rstd * (dvh - jnp.mean(dvh, axis=-1, keepdims=True) - vhat * jnp.mean(dvh * vhat, axis=-1, keepdims=True))
        dp_ref[:, :w] = (dgu * _gelu_grad(u_raw)).astype(dp_ref.dtype)
        dp_ref[:, w:2 * w] = (dgv * _gelu_grad(v_raw)).astype(dp_ref.dtype)
        dp_ref[:, 2 * w:3 * w] = dza.astype(dp_ref.dtype)

    vec = pl.BlockSpec((1, w), lambda i: (0, 0))
    row = pl.BlockSpec((tl, w), lambda i: (i, 0))
    ws_spec = pl.BlockSpec((nh, CHUNK, CHUNK), lambda i: (0, 0, 0))
    bs_spec = pl.BlockSpec((CHUNK, w), lambda i: (0, 0))
    v = jax.ShapeDtypeStruct((1, w), F32)
    return pl.pallas_call(
        body, name="ga_bwd",
        out_shape=(jax.ShapeDtypeStruct(dproj.shape, dproj.dtype), v, v, jax.ShapeDtypeStruct((nh, CHUNK, CHUNK), F32),
                   jax.ShapeDtypeStruct((CHUNK, w), F32)),
        grid=(l // tl,),
        in_specs=[pl.BlockSpec((tl, 3 * w), lambda i: (i, 0)), row, HBM, vec, vec, ws_spec, bs_spec, row, row, row, vec],
        out_specs=(pl.BlockSpec((tl, 4 * w), lambda i: (i, 0)), vec, vec, ws_spec, bs_spec),
        input_output_aliases={2: 0}, compiler_params=_cparams("arbitrary"))(
            proj, dcat, dproj, g, b, ws, bsf, dys, du0, du1, d_skip)


def _lane_group_sum(x, expand, name):
    return _small_dot(x, expand, "nn", name)


def _disc_math(lr, li, ls, br, bi):
    step = jnp.exp(ls)
    dr, di = lr * step, li * step
    mag = jnp.exp(dr)
    ab_re, ab_im = mag * jnp.cos(di), mag * jnp.sin(di)
    den = lr * lr + li * li
    nr, ni = ab_re - 1.0, ab_im
    f_re = (nr * lr + ni * li) / den
    f_im = (ni * lr - nr * li) / den
    bb_re = f_re * br - f_im * bi
    bb_im = f_re * bi + f_im * br
    return ab_re, ab_im, bb_re, bb_im


def _disc_fwd(lr, li, ls, br, bi):
    def body(lr_ref, li_ref, ls_ref, br_ref, bi_ref, o1, o2, o3, o4):
        res = _disc_math(lr_ref[...], li_ref[...], ls_ref[...], br_ref[...], bi_ref[...])
        for o, r in zip((o1, o2, o3, o4), res):
            o[...] = r

    s = lambda a: jax.ShapeDtypeStruct(a.shape, F32)
    return pl.pallas_call(body, name="s5_disc", out_shape=(s(lr), s(lr), s(br), s(br)), compiler_params=_cparams())(
        lr, li, ls, br, bi)


def _disc_bwd(lr, li, ls, br, bi, d_ar, d_ai, d_br, d_bi):
    def body(lr_ref, li_ref, ls_ref, br_ref, bi_ref, c1, c2, c3, c4, o1, o2, o3, o4, o5):
        _, vjp = jax.vjp(_disc_math, lr_ref[...], li_ref[...], ls_ref[...], br_ref[...], bi_ref[...])
        res = vjp((c1[...], c2[...], c3[...], c4[...]))
        for o, r in zip((o1, o2, o3, o4, o5), res):
            o[...] = r

    s = lambda a: jax.ShapeDtypeStruct(a.shape, F32)
    return pl.pallas_call(body, name="s5_disc_bwd", out_shape=(s(lr), s(lr), s(ls), s(br), s(br)),
                          compiler_params=_cparams())(lr, li, ls, br, bi, d_ar, d_ai, d_br, d_bi)


def _dir_spec(a, dr, **kw):
    return pl.BlockSpec((None,) + a.shape[1:], lambda i: (dr,) + (0,) * (a.ndim - 1), **kw)


def _s5_fwd(u_arr, u_col, w, h0, a_sm, wbr, wbi, cre, ncim, dr, name):
    rev = dr == 1
    l = u_arr.shape[0]
    nb = w // LANES
    spb = wbr.shape[-1]
    nsr = a_sm.shape[2]
    assert 2 * spb == 8 * LANES and nb % 2 == 0
    npair = nb // 2
    t = _tile(l, 256)
    n = l // t
    tile = (lambda i: n - 1 - i) if rev else (lambda i: i)

    def body(u_ref, h0_ref, a_ref, wbr_ref, wbi_ref, cre_ref, ncim_ref, y_ref, hr_ref, hi_ref, tr_ref, ti_ref, hfin_ref,
             carry_ref):
        i = pl.program_id(0)

        @pl.when(i == 0)
        def _():
            carry_ref[...] = h0_ref[...]

        for j in range(npair):
            for h_ref, w_ref in ((hr_ref, wbr_ref), (hi_ref, wbi_ref)):
                blk = [_mxu_dot(u_ref[:, k * LANES:(k + 1) * LANES], w_ref[k]) for k in (2 * j, 2 * j + 1)]
                h_ref[j] = jnp.concatenate(blk, axis=1).reshape(t, 8, LANES)
        slab = lambda ref, part, j: ref[part, 8 * j:8 * j + 8, :]
        ar = [slab(a_ref, 0, j) for j in range(npair)]
        ai = [slab(a_ref, 1, j) for j in range(npair)]

        def steps(blk, c):
            hr, hi = list(c[:npair]), list(c[npair:])
            for q in range(SCAN_UNROLL):
                s = blk * SCAN_UNROLL + q
                row = t - 1 - s if rev else s
                for j in range(npair):
                    hr[j], hi[j] = (ar[j] * hr[j] - ai[j] * hi[j] + hr_ref[j, row],
                                    ar[j] * hi[j] + ai[j] * hr[j] + hi_ref[j, row])
                    hr_ref[j, row] = hr[j]
                    hi_ref[j, row] = hi[j]
            return tuple(hr + hi)

        init = tuple(slab(carry_ref, part, j) for part in range(2) for j in range(npair))
        c = lax.fori_loop(0, t // SCAN_UNROLL, steps, init)
        for part in range(2):
            for j in range(npair):
                carry_ref[part, 8 * j:8 * j + 8, :] = c[part * npair + j]
                hfin_ref[part, 8 * j:8 * j + 8, :] = c[part * npair + j]
        for j in range(npair):
            cols8 = slice(j * 8 * LANES, (j + 1) * 8 * LANES)
            tr_ref[:, cols8] = hr_ref[j].reshape(t, 8 * LANES).astype(tr_ref.dtype)
            ti_ref[:, cols8] = hi_ref[j].reshape(t, 8 * LANES).astype(ti_ref.dtype)
        for k in range(nb):
            cols = slice(k * spb, (k + 1) * spb)
            y_ref[:, k * LANES:(k + 1) * LANES] = (_mxu_dot(tr_ref[:, cols], cre_ref[k]) + _mxu_dot(ti_ref[:, cols], ncim_ref[k]))

    full = lambda a: pl.BlockSpec(a.shape, lambda i: (0,) * a.ndim)
    hspec = pl.BlockSpec((npair, t, 8, LANES), lambda i: (0, tile(i), 0, 0))
    tspec = pl.BlockSpec((t, nsr * LANES), lambda i: (tile(i), 0))
    hsh = jax.ShapeDtypeStruct((npair, l, 8, LANES), F32)
    tsh = jax.ShapeDtypeStruct((l, nsr * LANES), MXU_DTYPE)
    return pl.pallas_call(
        body, name=name,
        out_shape=(jax.ShapeDtypeStruct((l, w), F32), hsh, hsh, tsh, tsh, jax.ShapeDtypeStruct((2, nsr, LANES), F32)),
        grid=(n,),
        in_specs=[pl.BlockSpec((t, w), lambda i: (tile(i), u_col)), full(h0)] + [_dir_spec(a, dr) for a in (a_sm, wbr, wbi, cre, ncim)],
        out_specs=(pl.BlockSpec((t, w), lambda i: (tile(i), 0)), hspec, hspec, tspec, tspec,
                   pl.BlockSpec((2, nsr, LANES), lambda i: (0, 0, 0))),
        scratch_shapes=[pltpu.VMEM((2, nsr, LANES), F32)],
        compiler_params=_cparams("arbitrary"))(u_arr, h0, a_sm, wbr, wbi, cre, ncim)


def _s5_bwd(dys, u_arr, u_col, w, hr, hi, tr, ti, hbound, g_in, a_sm, wbr_t, wbi_t, cre_t, ncim_t, dr, name):
    rev = dr == 1
    l = u_arr.shape[0]
    nb = w // LANES
    spb = wbr_t.shape[-2]
    nsr = a_sm.shape[2]
    npair = nb // 2
    t = _tile(l, 256 if l >= 1024 else 128)
    n = l // t
    with_dy = dys is not None
    tile = (lambda i: i) if rev else (lambda i: n - 1 - i)

    def body(*refs):
        if with_dy:
            (dy_ref, u_ref, hr_ref, hi_ref, pr_ref, pi_ref, tr_ref, ti_ref, hb_ref, gin_ref, a_ref, wbrt_ref, wbit_ref,
             cret_ref, ncimt_ref, du_ref, dwbr_ref, dwbi_ref, dcre_ref, dncim_ref, da_ref, gout_ref,
             gr_ref, gi_ref, gtr_ref, gti_ref, carry_ref) = refs
        else:
            (u_ref, hr_ref, hi_ref, pr_ref, pi_ref, hb_ref, gin_ref, a_ref, wbrt_ref, wbit_ref,
             du_ref, dwbr_ref, dwbi_ref, da_ref, gout_ref, gr_ref, gi_ref, gtr_ref, gti_ref, carry_ref) = refs
        i = pl.program_id(0)

        @pl.when(i == 0)
        def _():
            carry_ref[...] = gin_ref[...]
            accs = (dwbr_ref, dwbi_ref, da_ref) + ((dcre_ref, dncim_ref) if with_dy else ())
            for r in accs:
                r[...] = jnp.zeros_like(r)

        if with_dy:
            for j in range(npair):
                for g_ref, c_ref in ((gr_ref, cret_ref), (gi_ref, ncimt_ref)):
                    blk = [_mxu_dot(dy_ref[:, k * LANES:(k + 1) * LANES], c_ref[k]) for k in (2 * j, 2 * j + 1)]
                    g_ref[j] = jnp.concatenate(blk, axis=1).reshape(t, 8, LANES)
        else:
            gr_ref[...] = jnp.zeros_like(gr_ref)
            gi_ref[...] = jnp.zeros_like(gi_ref)
        slab = lambda ref, part, j: ref[part, 8 * j:8 * j + 8, :]
        last = t - 1 if rev else 0
        first = i == n - 1

        ar = [slab(a_ref, 0, j) for j in range(npair)]
        ai = [slab(a_ref, 1, j) for j in range(npair)]

        def steps(blk, c):
            gr, gi, dr, di = (list(c[q * npair:(q + 1) * npair]) for q in range(4))
            for q in range(SCAN_UNROLL):
                s = blk * SCAN_UNROLL + q
                row = s if rev else t - 1 - s
                prow = jnp.minimum(row + 1, t - 1) if rev else jnp.maximum(row - 1, 0)
                for j in range(npair):
                    pr, pi = hr_ref[j, prow], hi_ref[j, prow]
                    gr[j], gi[j] = (gr_ref[j, row] + ar[j] * gr[j] + ai[j] * gi[j],
                                    gi_ref[j, row] + ar[j] * gi[j] - ai[j] * gr[j])
                    gr_ref[j, row] = gr[j]
                    gi_ref[j, row] = gi[j]
                    dr[j], di[j] = dr[j] + gr[j] * pr + gi[j] * pi, di[j] + gi[j] * pr - gr[j] * pi
            return tuple(gr + gi + dr + di)

        init = tuple(slab(ref, part, j) for ref in (carry_ref, da_ref) for part in range(2) for j in range(npair))
        c = lax.fori_loop(0, t // SCAN_UNROLL, steps, init)
        gr, gi, dr, di = (c[q * npair:(q + 1) * npair] for q in range(4))
        for j in range(npair):
            pr = jnp.where(first, slab(hb_ref, 0, j), pr_ref[j, 0]) - hr_ref[j, last]
            pi = jnp.where(first, slab(hb_ref, 1, j), pi_ref[j, 0]) - hi_ref[j, last]
            rows = slice(8 * j, 8 * j + 8)
            da_ref[0, rows, :] = dr[j] + gr[j] * pr + gi[j] * pi
            da_ref[1, rows, :] = di[j] + gi[j] * pr - gr[j] * pi
            for part, val in enumerate((gr[j], gi[j])):
                carry_ref[part, rows, :] = val
                gout_ref[part, rows, :] = val

        for j in range(npair):
            cols8 = slice(j * 8 * LANES, (j + 1) * 8 * LANES)
            gtr_ref[:, cols8] = gr_ref[j].reshape(t, 8 * LANES).astype(gtr_ref.dtype)
            gti_ref[:, cols8] = gi_ref[j].reshape(t, 8 * LANES).astype(gti_ref.dtype)
        for k in range(nb):
            cols = slice(k * spb, (k + 1) * spb)
            lanes = slice(k * LANES, (k + 1) * LANES)
            du_ref[:, lanes] = _mxu_dot(gtr_ref[:, cols], wbrt_ref[k]) + _mxu_dot(gti_ref[:, cols], wbit_ref[k])
            dwbr_ref[k] += _mxu_dot(u_ref[:, lanes], gtr_ref[:, cols], _TN)
            dwbi_ref[k] += _mxu_dot(u_ref[:, lanes], gti_ref[:, cols], _TN)
            if with_dy:
                dcre_ref[k] += _mxu_dot(tr_ref[:, cols], dy_ref[:, lanes], _TN)
                dncim_ref[k] += _mxu_dot(ti_ref[:, cols], dy_ref[:, lanes], _TN)

    once = dict(pipeline_mode=pl.Buffered(1))
    full = lambda a: pl.BlockSpec(a.shape, lambda i: (0,) * a.ndim, **once)
    row = lambda cb: pl.BlockSpec((t, w), lambda i: (tile(i), cb))
    hspec = pl.BlockSpec((npair, t, 8, LANES), lambda i: (0, tile(i), 0, 0))
    if rev:
        pspec = pl.BlockSpec((npair, 1, 8, LANES), lambda i: (0, jnp.minimum((tile(i) + 1) * t, l - 1), 0, 0))
    else:
        pspec = pl.BlockSpec((npair, 1, 8, LANES), lambda i: (0, jnp.maximum(tile(i) * t - 1, 0), 0, 0))
    sm = jax.ShapeDtypeStruct((2, nsr, LANES), F32)
    smspec = pl.BlockSpec((2, nsr, LANES), lambda i: (0, 0, 0))
    wsh = jax.ShapeDtypeStruct((nb, LANES, spb), F32)
    csh = jax.ShapeDtypeStruct((nb, spb, LANES), F32)
    tspec = pl.BlockSpec((t, nsr * LANES), lambda i: (tile(i), 0))
    in_specs = (([row(0)] if with_dy else []) + [row(u_col), hspec, hspec, pspec, pspec] + ([tspec, tspec] if with_dy else [])
                + [full(hbound), full(g_in)]
                + [_dir_spec(a, dr, **once) for a in (a_sm, wbr_t, wbi_t) + ((cre_t, ncim_t) if with_dy else ())])
    args = (([dys] if with_dy else []) + [u_arr, hr, hi, hr, hi] + ([tr, ti] if with_dy else [])
            + [hbound, g_in, a_sm, wbr_t, wbi_t] + ([cre_t, ncim_t] if with_dy else []))
    out_shape = (jax.ShapeDtypeStruct((l, w), F32), wsh, wsh) + ((csh, csh) if with_dy else ()) + (sm, sm)
    out_specs = (row(0), full(wsh), full(wsh)) + ((full(csh), full(csh)) if with_dy else ()) + (smspec, smspec)
    return pl.pallas_call(
        body, name=name, out_shape=out_shape, grid=(n,), in_specs=in_specs, out_specs=out_specs,
        scratch_shapes=[pltpu.VMEM((npair, t, 8, LANES), F32)] * 2 + [pltpu.VMEM((t, nsr * LANES), MXU_DTYPE)] * 2
        + [pltpu.VMEM((2, nsr, LANES), F32)],
        compiler_params=_cparams("arbitrary"))(*args)


def _glu_fwd(y0, y1, proj, cat, d_skip, w_glu, b_glu, w):
    l = y0.shape[0]
    tl = _tile(l, 256)

    def body(y0_ref, y1_ref, u_ref, z_ref, cat_in, d_ref, wg_ref, bg_ref, ys_ref, cat_ref):
        del cat_in
        ys = y0_ref[...] + y1_ref[...] + d_ref[...] * u_ref[...]
        ys_ref[...] = ys
        gy = _gelu(ys)
        s = _mxu_dot(gy, wg_ref[...]) + bg_ref[...]
        cat_ref[...] = (gy * jax.nn.sigmoid(s) * jax.nn.silu(z_ref[...])).astype(cat_ref.dtype)

    row = pl.BlockSpec((tl, w), lambda i: (i, 0))
    vec = pl.BlockSpec((1, w), lambda i: (0, 0))
    return pl.pallas_call(
        body, name="glu_fwd", out_shape=(jax.ShapeDtypeStruct((l, w), F32), jax.ShapeDtypeStruct(cat.shape, cat.dtype)),
        grid=(l // tl,),
        in_specs=[row, row, pl.BlockSpec((tl, w), lambda i: (i, 3)), pl.BlockSpec((tl, w), lambda i: (i, 4)), HBM,
                  vec, pl.BlockSpec((w, w), lambda i: (0, 0)), vec],
        out_specs=(row, pl.BlockSpec((tl, w), lambda i: (i, 1))), input_output_aliases={4: 1},
        compiler_params=_cparams("parallel"))(y0, y1, proj, proj, cat, d_skip, w_glu, b_glu)


def _glu_bwd(dcat, ys, proj, w_glu, b_glu, w, dep):
    l = ys.shape[0]
    tl = _tile(l, 256)

    def body(dy_ref, ys_ref, u_ref, z_ref, wg_ref, bg_ref, dep_ref, dys_ref, dp_ref, dbg_ref, dd_ref, dwg_ref):
        del dep_ref
        i = pl.program_id(0)
        ys_t = ys_ref[...]
        z = z_ref[...]
        dyb = dy_ref[...].astype(F32)
        gy = _gelu(ys_t)
        sg = jax.nn.sigmoid(_mxu_dot(gy, wg_ref[...]) + bg_ref[...])
        dp_ref[...] = (dyb * gy * sg * _silu_grad(z)).astype(dp_ref.dtype)
        dglu = dyb * jax.nn.silu(z)
        ds = dglu * gy * sg * (1.0 - sg)
        dgy = dglu * sg + _mxu_dot(ds, wg_ref[...], _NT)
        dys_t = dgy * _gelu_grad(ys_t)
        dys_ref[...] = dys_t

        @pl.when(i == 0)
        def _():
            for r in (dbg_ref, dd_ref, dwg_ref):
                r[...] = jnp.zeros_like(r)

        dbg_ref[...] += jnp.sum(ds, axis=0, keepdims=True)
        dd_ref[...] += jnp.sum(dys_t * u_ref[...], axis=0, keepdims=True)
        dwg_ref[...] += _mxu_dot(gy, ds, _TN)

    row = pl.BlockSpec((tl, w), lambda i: (i, 0))
    vec = pl.BlockSpec((1, w), lambda i: (0, 0))
    mat = pl.BlockSpec((w, w), lambda i: (0, 0))
    v = jax.ShapeDtypeStruct((1, w), F32)
    return pl.pallas_call(
        body, name="glu_bwd",
        out_shape=(jax.ShapeDtypeStruct((l, w), F32), jax.ShapeDtypeStruct((l, 5 * w), MXU_DTYPE), v, v,
                   jax.ShapeDtypeStruct((w, w), F32)),
        grid=(l // tl,),
        in_specs=[pl.BlockSpec((tl, w), lambda i: (i, 1)), row, pl.BlockSpec((tl, w), lambda i: (i, 3)),
                  pl.BlockSpec((tl, w), lambda i: (i, 4)), mat, vec, HBM],
        out_specs=(row, pl.BlockSpec((tl, w), lambda i: (i, 4)), vec, vec, mat),
        compiler_params=_cparams("arbitrary"))(dcat, ys, proj, proj, w_glu, b_glu, dep)


def _add2(a, b, name):
    l, w = a.shape
    tl = _tile(l, 512)

    def body(a_ref, b_ref, o_ref):
        o_ref[...] = a_ref[...] + b_ref[...]

    row = pl.BlockSpec((tl, w), lambda i: (i, 0))
    return pl.pallas_call(body, name=name, out_shape=jax.ShapeDtypeStruct((l, w), F32), grid=(l // tl,),
                          in_specs=[row, row], out_specs=row, compiler_params=_cparams("parallel"))(a, b)


def _adamw_nd(w, m, v, g, name):
    shape = w.shape
    lead = math.prod(shape[:-2]) if len(shape) > 2 else 1
    b, c = (shape[-2], shape[-1]) if len(shape) >= 2 else (1, shape[-1])
    t3 = (lead, b, c)
    padded_row = -(-b // 8) * 8 * -(-c // LANES) * LANES * 4
    ta = _tile(lead, max(1, (2 << 20) // padded_row))

    def body(w_ref, m_ref, v_ref, g_ref, d_ref, mo_ref, vo_ref):
        gv = g_ref[...]
        mn = ADAM_B1 * m_ref[...] + (1.0 - ADAM_B1) * gv
        vn = ADAM_B2 * v_ref[...] + (1.0 - ADAM_B2) * (gv * gv)
        m_hat = mn / (1.0 - ADAM_B1 ** ADAM_STEP)
        v_hat = vn / (1.0 - ADAM_B2 ** ADAM_STEP)
        d_ref[...] = -ADAM_LR * (m_hat / (jnp.sqrt(v_hat) + ADAM_EPS) + ADAM_WD * w_ref[...])
        mo_ref[...] = mn
        vo_ref[...] = vn

    blk = pl.BlockSpec((ta, b, c), lambda i: (i, 0, 0))
    s = jax.ShapeDtypeStruct(t3, F32)
    outs = pl.pallas_call(body, name=name, out_shape=(s, s, s), grid=(lead // ta,), in_specs=[blk] * 4, out_specs=(blk,) * 3,
                          compiler_params=_cparams("parallel"))(*[a.reshape(t3) for a in (w, m, v, g)])
    return tuple(o.reshape(shape) for o in outs)


def _adamw(w, m, v, gparts, name, dep=None):
    r, c = w.shape
    np_ = gparts.shape[0]
    tr = _tile(r, max(8, (1 << 18) // c), 8)

    def body(w_ref, m_ref, v_ref, g_ref, *rest):
        go_ref, d_ref, mo_ref, vo_ref = rest[-4:]
        g = g_ref[0].astype(F32)
        for p in range(1, np_):
            g = g + g_ref[p].astype(F32)
        mn = ADAM_B1 * m_ref[...] + (1.0 - ADAM_B1) * g
        vn = ADAM_B2 * v_ref[...] + (1.0 - ADAM_B2) * (g * g)
        m_hat = mn / (1.0 - ADAM_B1 ** ADAM_STEP)
        v_hat = vn / (1.0 - ADAM_B2 ** ADAM_STEP)
        go_ref[...] = g
        d_ref[...] = -ADAM_LR * (m_hat / (jnp.sqrt(v_hat) + ADAM_EPS) + ADAM_WD * w_ref[...])
        mo_ref[...] = mn
        vo_ref[...] = vn

    row = pl.BlockSpec((tr, c), lambda i: (i, 0))
    s = jax.ShapeDtypeStruct((r, c), F32)
    extra = [] if dep is None else [dep]
    return pl.pallas_call(body, name=name, out_shape=(s, s, s, s), grid=(r // tr,),
                          in_specs=[row, row, row, pl.BlockSpec((np_, tr, c), lambda i: (0, i, 0))] + [HBM] * len(extra),
                          out_specs=(row, row, row, row), compiler_params=_cparams("parallel"))(w, m, v, gparts, *extra)


def _sum_slots(parts, name):
    np_, r, c = parts.shape

    def body(p_ref, o_ref):
        g = p_ref[0]
        for p in range(1, np_):
            g = g + p_ref[p]
        o_ref[...] = g

    return pl.pallas_call(body, name=name, out_shape=jax.ShapeDtypeStruct((r, c), F32), compiler_params=_cparams())(parts)


def _block_diag(x, gb):
    nd, g, a, b = x.shape
    eye = jnp.eye(gb, dtype=x.dtype)
    y = jnp.einsum("dkgab,gh->dkgahb", x.reshape(nd, g // gb, gb, a, b), eye)
    return y.reshape(nd, g // gb, gb * a, gb * b)


def _block_diag_extract(y, gb, a, b):
    nd, nbk = y.shape[:2]
    eye = jnp.eye(gb, dtype=y.dtype)
    x = jnp.einsum("dkgahb,gh->dkgab", y.reshape(nd, nbk, gb, a, gb, b), eye)
    return x.reshape(nd, nbk * gb, a, b)


def kernel(x, c, ctx, c_ctx, w_ada, b_ada, w_in, sgu_ln_g, sgu_ln_b, w_spatial, b_spatial, s5_lam_re, s5_lam_im, s5_log_step, s5_b_re, s5_b_im, s5_c_re, s5_c_im, s5_d, w_glu, b_glu, w_out, ln_g, ln_b, loss_target, m_c_ctx, m_w_ada, m_b_ada, m_w_in, m_sgu_ln_g, m_sgu_ln_b, m_w_spatial, m_b_spatial, m_s5_lam_re, m_s5_lam_im, m_s5_log_step, m_s5_b_re, m_s5_b_im, m_s5_c_re, m_s5_c_im, m_s5_d, m_w_glu, m_b_glu, m_w_out, m_ln_g, m_ln_b, v_c_ctx, v_w_ada, v_b_ada, v_w_in, v_sgu_ln_g, v_sgu_ln_b, v_w_spatial, v_b_spatial, v_s5_lam_re, v_s5_lam_im, v_s5_log_step, v_s5_b_re, v_s5_b_im, v_s5_c_re, v_s5_c_im, v_s5_d, v_w_glu, v_b_glu, v_w_out, v_ln_g, v_ln_b):
    small_names = ["c_ctx", "b_ada", "sgu_ln_g", "sgu_ln_b", "w_spatial", "b_spatial", "s5_lam_re", "s5_lam_im",
                   "s5_log_step", "s5_b_re", "s5_b_im", "s5_c_re", "s5_c_im", "s5_d", "b_glu", "ln_g", "ln_b"]
    env = dict(locals())
    x2, tgt, ctx2 = x[0], loss_target[0], ctx[0]
    l, d = x2.shape
    lc = ctx2.shape[0]
    w = d // 2
    nh = w // HEAD_DIM_A
    nd, g_s5, p_s5, c_s5 = s5_b_re.shape[1:]
    ns = g_s5 * p_s5
    nsr = ns // LANES
    gb = LANES // c_s5
    me = _index(_mesh_pos())
    ada_cols = w_ada.shape[2]

    srows = _silu_rows(c, c_ctx)
    srows_all = _all_gather(srows, 0, "gather_silu")
    s_mat = jnp.concatenate([srows_all[0::8], srows_all[1:2], jnp.zeros((7, d), F32)], axis=0)
    mod_part = _small_dot(s_mat, w_ada[0], "nn", "mod_cols")
    mod_all = _all_gather(mod_part, 1, "gather_mod") + b_ada
    hw_in, tok_a = _exchange_start(w_in[0].astype(MXU_DTYPE), 1, "gather", "start_gather_w_in", (SIBLING,) + SAME_CORE_PEERS)
    mod_all = mod_all + tok_a[0, 0]
    mod_x = lax.dynamic_slice_in_dim(mod_all, me, 1, axis=0)
    mod_c = mod_all[8:9]
    shift_x, scale_x, gate_x = mod_x[:, :d], mod_x[:, d:2 * d], mod_x[:, 2 * d:]
    shift_c, scale_c = mod_c[:, :d], mod_c[:, d:2 * d]

    lr, li = s5_lam_re[0][:, :, None, :], s5_lam_im[0][:, :, None, :]
    ls = s5_log_step[0][:, :, None, None]
    swapped = ("s5_b_re", "s5_b_im")
    for nm in swapped:
        for pre in ("", "m_", "v_"):
            env[pre + nm] = jnp.swapaxes(env[pre + nm], -1, -2)
    br_t, bi_t = env["s5_b_re"][0], env["s5_b_im"][0]
    ab_re, ab_im, bb_re, bb_im = _disc_fwd(lr, li, ls, br_t, bi_t)
    a_sm = jnp.stack([ab_re, ab_im], axis=1).reshape(nd, 2, nsr, LANES)
    wbr = _block_diag(bb_re.astype(MXU_DTYPE), gb)
    wbi = _block_diag(bb_im.astype(MXU_DTYPE), gb)
    cre_t = _block_diag(s5_c_re[0].astype(MXU_DTYPE), gb)
    ncim_t = _block_diag((-s5_c_im[0]).astype(MXU_DTYPE), gb)
    cre, ncim = jnp.swapaxes(cre_t, 2, 3), jnp.swapaxes(ncim_t, 2, 3)
    wbr_t, wbi_t = jnp.swapaxes(wbr, 2, 3), jnp.swapaxes(wbi, 2, 3)
    d_skip = s5_d

    xm = _ln_mod(x2, shift_x, scale_x, "ln_mod_x")
    cm = _ln_mod(ctx2, shift_c, scale_c, "ln_mod_ctx")
    ready = xm[:8, :LANES].astype(F32) + cm[:8, :LANES].astype(F32) + cre[0, 0, :8, :].astype(F32)
    hw_in2, tok_b = _forward_start(_exchange_wait(hw_in, ready, "wait_gather_w_in"), 1, "start_forward_w_in")
    w_in_f = _forward_wait(hw_in2, tok_b, "wait_forward_w_in")
    hw_glu, tok_c = _exchange_start(w_glu[0].astype(MXU_DTYPE), 0, "gather", "start_gather_w_glu")
    hw_out, tok_o = _exchange_start(w_out[0].astype(MXU_DTYPE), 0, "gather", "start_gather_w_out")
    proj = _matmul(xm, w_in_f, mode="nn", name="proj", dep=tok_c + tok_o)
    ub_c = _matmul(cm, w_in_f, mode="nn", name="proj_ctx", b_n0=3 * w, n=w)
    bsf = jnp.repeat(b_spatial[0].T, HEAD_DIM_A, axis=1)
    ws = w_spatial[0]
    cat = _ga_fwd(proj, sgu_ln_g, sgu_ln_b, ws, bsf, w)
    zeros_state = jnp.zeros((2, nsr, LANES), F32)
    s5c, s5l = [], []
    for dr in range(nd):
        s5c.append(_s5_fwd(ub_c, 0, w, zeros_state, a_sm, wbr, wbi, cre, ncim, dr, f"s5_fwd_ctx{dr}"))
        s5l.append(_s5_fwd(proj, 3, w, s5c[dr][5], a_sm, wbr, wbi, cre, ncim, dr, f"s5_fwd{dr}"))
    w_glu_f = _exchange_wait(hw_glu, s5l[1][0], "wait_gather_w_glu")
    ys, cat = _glu_fwd(s5l[0][0], s5l[1][0], proj, cat, d_skip, w_glu_f, b_glu, w)
    w_out_f = _exchange_wait(hw_out, ys, "wait_gather_w_out")
    out = _matmul(cat, w_out_f, mode="nn", name="out_proj")
    loss_row, dout, dx_res, dgate, dln_g, dln_b = _post_ln_loss(x2, out, gate_x, ln_g, ln_b, tgt)

    dcat = _matmul(dout, w_out_f, mode="nt", name="d_cat", out_dtype=MXU_DTYPE)
    dw_out = _matmul(cat, dout, mode="tn", name="d_w_out", out_dtype=MXU_DTYPE)
    hg_out, tok_d = _exchange_start(dw_out, 0, "a2a", "start_a2a_d_w_out")
    dys, dproj, db_glu, dd_skip, dw_glu = _glu_bwd(dcat, ys, proj, w_glu_f, b_glu, w, tok_d)
    hg_glu, tok_e = _exchange_start(dw_glu.astype(MXU_DTYPE), 0, "a2a", "start_a2a_d_w_glu")
    zeros_state = zeros_state + tok_e[0, 0]
    du_l, du_c, dwbr, dwbi, dcre, dncim, da_sm = [], [], [], [], [], [], []
    nbk, spb = w // LANES, gb * p_s5
    for dr in range(nd):
        bl = _s5_bwd(dys, proj, 3, w, *s5l[dr][1:5], s5c[dr][5], zeros_state, a_sm, wbr_t, wbi_t,
                     cre_t, ncim_t, dr, f"s5_bwd{dr}")
        bc = _s5_bwd(None, ub_c, 0, w, s5c[dr][1], s5c[dr][2], None, None, zeros_state, bl[6], a_sm, wbr_t, wbi_t,
                     None, None, dr, f"s5_bwd_ctx{dr}")
        du_l.append(bl[0])
        du_c.append(bc[0])
        dwbr.append(_add2(bl[1].reshape(nbk * LANES, spb), bc[1].reshape(nbk * LANES, spb), f"sum_dwbr{dr}"))
        dwbi.append(_add2(bl[2].reshape(nbk * LANES, spb), bc[2].reshape(nbk * LANES, spb), f"sum_dwbi{dr}"))
        dcre.append(bl[3])
        dncim.append(bl[4])
        da_sm.append(_add2(bl[5].reshape(2 * nsr, LANES), bc[3].reshape(2 * nsr, LANES), f"sum_da{dr}"))
    dub_c = _add2(du_c[0], du_c[1], "dub_ctx")
    dwbr = jnp.stack(dwbr).reshape(nd, nbk, LANES, spb)
    dwbi = jnp.stack(dwbi).reshape(nd, nbk, LANES, spb)
    dcre, dncim = jnp.stack(dcre), jnp.stack(dncim)
    da_sm = jnp.stack(da_sm).reshape(nd, 2, g_s5, p_s5)
    dproj, dsg, dsb, dws, dbsf = _ga_bwd(proj, dcat, dproj, sgu_ln_g, sgu_ln_b, ws, bsf, dys, du_l[0], du_l[1], d_skip, w)

    dbb_re = _block_diag_extract(dwbr, gb, c_s5, p_s5)
    dbb_im = _block_diag_extract(dwbi, gb, c_s5, p_s5)
    dc_re = jnp.swapaxes(_block_diag_extract(dcre, gb, p_s5, c_s5), 2, 3)
    dc_im = -jnp.swapaxes(_block_diag_extract(dncim, gb, p_s5, c_s5), 2, 3)
    dlr, dli, dls, db_re, db_im = _disc_bwd(lr, li, ls, br_t, bi_t, da_sm[:, 0:1].reshape(nd, g_s5, 1, p_s5),
                                            da_sm[:, 1:2].reshape(nd, g_s5, 1, p_s5), dbb_re, dbb_im)
    expand = (jnp.arange(w)[:, None] // HEAD_DIM_A == jnp.arange(LANES)[None, :]).astype(F32)
    db_sp = _lane_group_sum(dbsf, expand, "d_b_spatial")[:, :nh].T

    local = {"sgu_ln_g": dsg, "sgu_ln_b": dsb, "w_spatial": dws, "b_spatial": db_sp,
             "s5_lam_re": dlr, "s5_lam_im": dli, "s5_log_step": dls, "s5_b_re": db_re, "s5_b_im": db_im,
             "s5_c_re": dc_re, "s5_c_im": dc_im, "s5_d": dd_skip, "b_glu": db_glu, "ln_g": dln_g, "ln_b": dln_b}
    reduced = sorted(local, key=lambda n: -math.prod(env[n].shape))
    loss_part = (0.5 / d) * jnp.sum(loss_row)
    flat = jnp.concatenate([local[n].reshape(-1) for n in reduced] + [loss_part.reshape(1)])
    unit = N_DEV * 8 * LANES
    total = -(-flat.shape[0] // unit) * unit
    flat = jnp.pad(flat, (0, total - flat.shape[0])).reshape(N_DEV * 8, total // (N_DEV * 8))
    h_small, tok_s = _exchange_start(flat, 0, "a2a", "start_a2a_small")

    dw_in = _matmul(xm, dproj, mode="tn", name="d_w_in", out_dtype=MXU_DTYPE, dep=tok_s)
    dw_in = _matmul(cm, dub_c, mode="tn", name="d_w_in_ctx", acc_in=dw_in, acc_n0=3 * w, out_dtype=MXU_DTYPE)
    hg_in, tok_f = _exchange_start(dw_in, 1, "a2a", "start_a2a_d_w_in")
    mine = _sum_slots(_exchange_wait(h_small, dw_in, "wait_a2a_small"), "sum_small")
    h_sums, tok_g = _exchange_start(mine, 0, "gather", "start_gather_small")
    dxm = _matmul(dproj, w_in_f, mode="nt", name="d_xm", dep=tok_f + tok_g, out_dtype=MXU_DTYPE)
    dcm = _matmul(dub_c, w_in_f, mode="nt", name="d_cm", b_k0=3 * w, k=w)
    grad_x, dshift_x, dscale_x = _ln_mod_bwd(x2, dxm, scale_x, dx_res, "ln_mod_x_bwd")
    _, dshift_c, dscale_c = _ln_mod_bwd(ctx2, dcm, scale_c, None, "ln_mod_ctx_bwd")

    dmod_rows = jnp.concatenate([jnp.concatenate([dshift_x, dscale_x, dgate], axis=1),
                                 jnp.concatenate([dshift_c, dscale_c, jnp.zeros((1, d), F32)], axis=1),
                                 jnp.zeros((6, 3 * d), F32)], axis=0)
    h_dmod, tok_m = _exchange_start(dmod_rows, 0, "gather", "start_gather_dmod")

    gp_w_out = _exchange_wait(hg_out, tok_m, "wait_a2a_d_w_out")
    gp_w_glu = _exchange_wait(hg_glu, tok_m, "wait_a2a_d_w_glu")
    gp_w_in = _exchange_wait(hg_in, tok_m, "wait_a2a_d_w_in")
    big = {
        "w_in": _adamw(w_in[0], m_w_in[0], v_w_in[0], gp_w_in, "adamw_w_in"),
        "w_glu": _adamw(w_glu[0], m_w_glu[0], v_w_glu[0], gp_w_glu, "adamw_w_glu"),
        "w_out": _adamw(w_out[0], m_w_out[0], v_w_out[0], gp_w_out, "adamw_w_out"),
    }
    dmod_all = _exchange_wait(h_dmod, big["w_out"][0], "wait_gather_dmod")
    dmod_ctx = _sum_slots(dmod_all[1::8].reshape(N_DEV, 1, 3 * d), "sum_dmod_ctx")
    dmod_mat = jnp.concatenate([dmod_all[0::8], dmod_ctx, jnp.zeros((7, 3 * d), F32)], axis=0)
    db_ada = _sum_slots(dmod_mat[:9].reshape(9, 1, 3 * d), "sum_db_ada")
    dmod_mine = lax.dynamic_slice_in_dim(dmod_mat, me * ada_cols, ada_cols, axis=1)
    dw_ada = _small_dot(s_mat, dmod_mine, "tn", "d_w_ada")
    dsilu_cc = _small_dot(dmod_mine[8:16], w_ada[0], "nt", "d_silu_cctx")[0:1]
    dc_ctx_part = dsilu_cc * _silu_grad(c_ctx.reshape(1, d))
    dc_ctx_rows = jnp.concatenate([dc_ctx_part, jnp.zeros((7, d), F32)], axis=0)
    h_cctx, tok_c2 = _exchange_start(dc_ctx_rows, 0, "gather", "start_gather_d_c_ctx")

    big["w_ada"] = _adamw(w_ada[0], m_w_ada[0], v_w_ada[0], dw_ada[None], "adamw_w_ada", dep=tok_c2)
    summed = _exchange_wait(h_sums, big["w_ada"][0], "wait_gather_small").reshape(-1)
    grads, off = {"b_ada": db_ada}, 0
    for n in reduced:
        size = math.prod(env[n].shape)
        grads[n] = summed[off:off + size].reshape(env[n].shape)
        off += size
    loss = summed[off]
    res = {n: tuple(a[None] for a in big[n]) for n in big}

    def small_step(n):
        res[n] = (grads[n],) + _adamw_nd(env[n], env["m_" + n], env["v_" + n], grads[n], "adamw_" + n)
        if n in swapped:
            res[n] = tuple(jnp.swapaxes(a, -1, -2) for a in res[n])

    for n in small_names:
        if n != "c_ctx":
            small_step(n)
    dc_ctx_all = _exchange_wait(h_cctx, res["w_spatial"][1], "wait_gather_d_c_ctx")
    grads["c_ctx"] = _sum_slots(dc_ctx_all[0::8].reshape(N_DEV, 1, d), "sum_d_c_ctx").reshape(d)
    small_step("c_ctx")

    order = ["c_ctx", "w_ada", "b_ada", "w_in", "sgu_ln_g", "sgu_ln_b", "w_spatial", "b_spatial", "s5_lam_re", "s5_lam_im",
             "s5_log_step", "s5_b_re", "s5_b_im", "s5_c_re", "s5_c_im", "s5_d", "w_glu", "b_glu", "w_out", "ln_g", "ln_b"]
    return (loss, grad_x[None], *[res[n][0] for n in order], *[res[n][1] for n in order],
            *[res[n][2] for n in order], *[res[n][3] for n in order])
```

```python
import functools
import math

import jax
import jax.numpy as jnp
from jax import lax
from jax.experimental import pallas as pl
from jax.experimental.pallas import tpu as pltpu

F32 = jnp.float32
MXU_DTYPE = jnp.bfloat16
N_DEV = 8
MESH_ID = pl.DeviceIdType.MESH
LN_EPS = 1e-6
DEPTH = 1
ALPHA = (2.0 * DEPTH) ** 0.25
CHUNK = 128
HEAD_DIM_A = 128
ADAM_LR, ADAM_B1, ADAM_B2, ADAM_EPS, ADAM_WD, ADAM_STEP = 0.001, 0.9, 0.999, 1e-08, 0.01, 10
LANES = 128
SCAN_UNROLL = 8
VMEM_LIMIT = 56 * 1024 * 1024
HBM = pl.BlockSpec(memory_space=pl.ANY)


def _cparams(*sem):
    return pltpu.CompilerParams(dimension_semantics=sem if sem else None, vmem_limit_bytes=VMEM_LIMIT)


def _tile(n, pref, mult=1):
    if n <= pref:
        return n
    t = pref - pref % mult
    while n % t:
        t -= mult
    return t


def _gelu(x):
    return 0.5 * x * (1.0 + lax.erf(x * (1.0 / math.sqrt(2.0))))


def _gelu_grad(x):
    return 0.5 * (1.0 + lax.erf(x * (1.0 / math.sqrt(2.0)))) + x * jnp.exp(-0.5 * x * x) * (1.0 / math.sqrt(2.0 * math.pi))


def _silu_grad(x):
    s = jax.nn.sigmoid(x)
    return s * (1.0 + x * (1.0 - s))


def _mxu_dot(a, b, dims=(((1,), (0,)), ((), ()))):
    return lax.dot_general(a.astype(MXU_DTYPE), b.astype(MXU_DTYPE), dims, preferred_element_type=F32)


_NT = (((1,), (1,)), ((), ()))
_TN = (((0,), (0,)), ((), ()))


def _mesh_pos():
    return lax.axis_index("x"), lax.axis_index("y"), lax.axis_index("c")


def _peer(pos, r):
    x, y, c = pos
    return ((1 - x) if r & 4 else x, (1 - y) if r & 2 else y, (1 - c) if r & 1 else c)


def _index(pos):
    return 4 * pos[0] + 2 * pos[1] + pos[2]


def _slice_of(ref, axis, idx, size):
    start = idx * size
    if axis == 0:
        return ref.at[pl.ds(start, size)]
    return ref.at[:, pl.ds(start, size)]


def _all_gather(x, axis, name):
    size = x.shape[axis]
    out_shape = tuple(s * N_DEV if a == axis else s for a, s in enumerate(x.shape))

    def body(x_ref, o_ref, send_sems, recv_sems, local_sem):
        me = _mesh_pos()
        mine = pltpu.make_async_copy(x_ref, _slice_of(o_ref, axis, _index(me), size), local_sem)
        mine.start()

        def copy(r, block):
            return pltpu.make_async_remote_copy(
                src_ref=x_ref, dst_ref=_slice_of(o_ref, axis, _index(block), size),
                send_sem=send_sems.at[r - 1], recv_sem=recv_sems.at[r - 1],
                device_id=_peer(me, r), device_id_type=MESH_ID)

        sends = [copy(r, me) for r in range(1, N_DEV)]
        for cp in sends:
            cp.start()
        for r in range(1, N_DEV):
            copy(r, _peer(me, r)).wait_recv()
        for cp in sends:
            cp.wait_send()
        mine.wait()

    return pl.pallas_call(
        body, name=name, out_shape=jax.ShapeDtypeStruct(out_shape, x.dtype),
        in_specs=[HBM], out_specs=HBM,
        scratch_shapes=[pltpu.SemaphoreType.DMA((N_DEV - 1,)), pltpu.SemaphoreType.DMA((N_DEV - 1,)),
                        pltpu.SemaphoreType.DMA],
    )(x)


_SEM = pl.BlockSpec(memory_space=pltpu.SEMAPHORE)
_HBM = pl.BlockSpec(memory_space=pltpu.HBM)
_EFFECT = pltpu.SideEffectType.DATAFLOW_SIDE_EFFECTING
ALL_PEERS = tuple(range(1, N_DEV))
SIBLING = 1
SAME_CORE_PEERS = (2, 4, 6)


def _exchange_copy(kind, x_ref, land_ref, axis, size, send_sems, recv_sems, me, rels, q, arriving):
    peer = _peer(me, rels[q])
    sender, receiver = (peer, me) if arriving else (me, peer)
    if kind == "gather":
        src, dst = x_ref, _slice_of(land_ref, axis, _index(sender), size)
    else:
        src, dst = _slice_of(x_ref, axis, _index(receiver), size), land_ref.at[_index(sender)]
    return pltpu.make_async_remote_copy(src_ref=src, dst_ref=dst, send_sem=send_sems.at[q], recv_sem=recv_sems.at[q],
                                        device_id=peer, device_id_type=MESH_ID)


def _local_copy(kind, x_ref, land_ref, axis, size, me, local_sem):
    if kind == "gather":
        return pltpu.make_async_copy(x_ref, _slice_of(land_ref, axis, _index(me), size), local_sem)
    return pltpu.make_async_copy(_slice_of(x_ref, axis, _index(me), size), land_ref.at[_index(me)], local_sem)


def _exchange_start(x, axis, kind, name, rels=ALL_PEERS):
    size = x.shape[axis] if kind == "gather" else x.shape[axis] // N_DEV
    if kind == "gather":
        land_shape = tuple(s * N_DEV if a == axis else s for a, s in enumerate(x.shape))
    else:
        land_shape = (N_DEV,) + tuple(size if a == axis else s for a, s in enumerate(x.shape))

    def body(x_ref, land_ref, send_sems, recv_sems, local_sem, x_thru, land_thru, token):
        del x_thru, land_thru
        me = _mesh_pos()
        _local_copy(kind, x_ref, land_ref, axis, size, me, local_sem).start()
        for q in range(len(rels)):
            _exchange_copy(kind, x_ref, land_ref, axis, size, send_sems, recv_sems, me, rels, q, False).start()
        token[...] = jnp.zeros_like(token)

    sems = pltpu.SemaphoreType.DMA((len(rels),))
    send_sems, recv_sems, local_sem, x_thru, land_thru, token = pl.pallas_call(
        body, name=name,
        out_shape=(sems, sems, pltpu.SemaphoreType.DMA(()), pltpu.HBM(x.shape, x.dtype), pltpu.HBM(land_shape, x.dtype),
                   jax.ShapeDtypeStruct((8, LANES), F32)),
        in_specs=(_HBM, _HBM), out_specs=(_SEM, _SEM, _SEM, _HBM, _HBM, pl.BlockSpec(memory_space=pltpu.VMEM)),
        input_output_aliases={0: 3, 1: 4}, compiler_params=pltpu.CompilerParams(has_side_effects=_EFFECT),
    )(pltpu.with_memory_space_constraint(x, pltpu.HBM),
      pltpu.with_memory_space_constraint(lax.empty(land_shape, x.dtype), pltpu.HBM))
    return (kind, axis, size, rels, send_sems, recv_sems, local_sem, x_thru, land_thru), token


def _exchange_wait(handle, after, name):
    kind, axis, size, rels, send_sems, recv_sems, local_sem, x_thru, land_thru = handle

    def body(x_ref, land_ref, send_sems, recv_sems, local_sem, after_ref, x_dead, got_ref):
        del after_ref, x_dead, got_ref
        me = _mesh_pos()
        _local_copy(kind, x_ref, land_ref, axis, size, me, local_sem).wait()
        for q in range(len(rels)):
            _exchange_copy(kind, x_ref, land_ref, axis, size, send_sems, recv_sems, me, rels, q, False).wait_send()
        for q in range(len(rels)):
            _exchange_copy(kind, x_ref, land_ref, axis, size, send_sems, recv_sems, me, rels, q, True).wait_recv()

    return pl.pallas_call(
        body, name=name, out_shape=(pltpu.HBM(x_thru.shape, x_thru.dtype), pltpu.HBM(land_thru.shape, land_thru.dtype)),
        in_specs=(_HBM, _HBM, _SEM, _SEM, _SEM, HBM), out_specs=(_HBM, _HBM), input_output_aliases={0: 0, 1: 1},
        compiler_params=pltpu.CompilerParams(has_side_effects=_EFFECT),
    )(x_thru, land_thru, send_sems, recv_sems, local_sem, after)[1]


def _forward_copy(land_ref, axis, size, send_sems, recv_sems, me, q, arriving):
    sibling = _peer(me, SIBLING)
    owner = _peer(sibling if arriving else me, SAME_CORE_PEERS[q])
    block = _slice_of(land_ref, axis, _index(owner), size)
    return pltpu.make_async_remote_copy(src_ref=block, dst_ref=block, send_sem=send_sems.at[q], recv_sem=recv_sems.at[q],
                                        device_id=sibling, device_id_type=MESH_ID)


def _forward_start(land, axis, name):
    size = land.shape[axis] // N_DEV

    def body(land_ref, send_sems, recv_sems, land_thru, token):
        del land_thru
        me = _mesh_pos()
        for q in range(len(SAME_CORE_PEERS)):
            _forward_copy(land_ref, axis, size, send_sems, recv_sems, me, q, False).start()
        token[...] = jnp.zeros_like(token)

    sems = pltpu.SemaphoreType.DMA((len(SAME_CORE_PEERS),))
    send_sems, recv_sems, land_thru, token = pl.pallas_call(
        body, name=name, out_shape=(sems, sems, pltpu.HBM(land.shape, land.dtype), jax.ShapeDtypeStruct((8, LANES), F32)),
        in_specs=(_HBM,), out_specs=(_SEM, _SEM, _HBM, pl.BlockSpec(memory_space=pltpu.VMEM)),
        input_output_aliases={0: 2}, compiler_params=pltpu.CompilerParams(has_side_effects=_EFFECT),
    )(land)
    return (axis, size, send_sems, recv_sems, land_thru), token


def _forward_wait(handle, after, name):
    axis, size, send_sems, recv_sems, land_thru = handle

    def body(land_ref, send_sems, recv_sems, after_ref, got_ref):
        del after_ref, got_ref
        me = _mesh_pos()
        for q in range(len(SAME_CORE_PEERS)):
            _forward_copy(land_ref, axis, size, send_sems, recv_sems, me, q, False).wait_send()
        for q in range(len(SAME_CORE_PEERS)):
            _forward_copy(land_ref, axis, size, send_sems, recv_sems, me, q, True).wait_recv()

    return pl.pallas_call(
        body, name=name, out_shape=pltpu.HBM(land_thru.shape, land_thru.dtype),
        in_specs=(_HBM, _SEM, _SEM, HBM), out_specs=_HBM, input_output_aliases={0: 0},
        compiler_params=pltpu.CompilerParams(has_side_effects=_EFFECT),
    )(land_thru, send_sems, recv_sems, after)


WHOLE = 1 << 30
MATMUL_TILES = {
    "proj": (1024, 1024, WHOLE), "proj_ctx": (256, WHOLE, WHOLE), "out_proj": (1024, 1024, WHOLE),
    "d_cat": (1024, 1024, WHOLE), "d_w_out": (1024, 1024, 2048), "d_w_in": (1024, 1280, 2048),
    "d_w_in_ctx": (1024, WHOLE, WHOLE), "d_xm": (1024, 512, WHOLE), "d_cm": (256, 1024, WHOLE),
}


def _matmul(a, b, *, mode, name, out_dtype=F32, b_n0=0, n=None, b_k0=0, k=None, acc_in=None, acc_n0=0, dep=None):
    bm, bn, bk = MATMUL_TILES[name]
    if mode == "tn":
        kk, m = a.shape
    else:
        m, kk = a.shape
    if mode == "nn":
        n = b.shape[1] if n is None else n
    elif mode == "nt":
        n = b.shape[0]
        kk = kk if k is None else k
    else:
        n = b.shape[1]
    bm, bn, bk = _tile(m, bm), _tile(n, bn), _tile(kk, bk)
    nk = kk // bk
    assert b_n0 % bn == 0 and b_k0 % bk == 0 and acc_n0 % bn == 0
    dims = {"nn": (((1,), (0,)), ((), ())), "nt": _NT, "tn": _TN}[mode]

    n_in = 2 + (acc_in is not None) + (dep is not None)

    def body(*refs):
        a_ref, b_ref = refs[:2]
        init = refs[2] if acc_in is not None else None
        o_ref = refs[n_in]
        acc_ref = refs[-1] if nk > 1 else None
        p = _mxu_dot(a_ref[...], b_ref[...], dims)
        if nk == 1:
            o_ref[...] = (p if init is None else p + init[...]).astype(out_dtype)
            return
        ki = pl.program_id(2)

        @pl.when(ki == 0)
        def _():
            acc_ref[...] = p if init is None else p + init[...]

        @pl.when(ki > 0)
        def _():
            acc_ref[...] += p

        @pl.when(ki == nk - 1)
        def _():
            o_ref[...] = acc_ref[...].astype(out_dtype)

    a_spec = pl.BlockSpec((bk, bm), lambda j, i, q: (q, i)) if mode == "tn" else pl.BlockSpec((bm, bk), lambda j, i, q: (i, q))
    if mode == "nt":
        b_spec = pl.BlockSpec((bn, bk), lambda j, i, q: (j, q + b_k0 // bk))
    else:
        b_spec = pl.BlockSpec((bk, bn), lambda j, i, q: (q, j + b_n0 // bn))
    in_specs, args, aliases = [a_spec, b_spec], [a, b], {}
    out_map = lambda j, i, q: (i, j + acc_n0 // bn)
    if acc_in is not None:
        in_specs.append(pl.BlockSpec((bm, bn), out_map))
        args.append(acc_in)
        aliases = {2: 0}
        out_shape = jax.ShapeDtypeStruct(acc_in.shape, out_dtype)
    else:
        out_shape = jax.ShapeDtypeStruct((m, n), out_dtype)
    if dep is not None:
        in_specs.append(HBM)
        args.append(dep)
    return pl.pallas_call(
        body, name=name, out_shape=out_shape, grid=(n // bn, m // bm, nk),
        in_specs=in_specs, out_specs=pl.BlockSpec((bm, bn), out_map),
        scratch_shapes=[pltpu.VMEM((bm, bn), F32)] if nk > 1 else [],
        input_output_aliases=aliases,
        compiler_params=_cparams("parallel", "parallel", "arbitrary"),
    )(*args)


def _silu_rows(c, c_ctx):
    d = c.shape[-1]

    def body(c_ref, cc_ref, o_ref):
        o_ref[...] = jnp.zeros_like(o_ref)
        o_ref[0:1, :] = jax.nn.silu(c_ref[...])
        o_ref[1:2, :] = jax.nn.silu(cc_ref[...])

    return pl.pallas_call(body, name="silu_rows", out_shape=jax.ShapeDtypeStruct((8, d), F32))(
        c.reshape(1, d), c_ctx.reshape(1, d))


def _small_dot(a, b, mode, name):
    dims = {"nn": (((1,), (0,)), ((), ())), "nt": _NT, "tn": _TN}[mode]
    m = a.shape[1] if mode == "tn" else a.shape[0]
    n = b.shape[0] if mode == "nt" else b.shape[1]

    def body(a_ref, b_ref, o_ref):
        o_ref[...] = lax.dot_general(a_ref[...], b_ref[...], dims, preferred_element_type=F32,
                                     precision=lax.Precision.HIGHEST)

    return pl.pallas_call(body, name=name, out_shape=jax.ShapeDtypeStruct((m, n), F32),
                          compiler_params=_cparams())(a, b)


def _ln_stats(x):
    mu = jnp.mean(x, axis=-1, keepdims=True)
    xc = x - mu
    var = jnp.mean(xc * xc, axis=-1, keepdims=True)
    rstd = lax.rsqrt(var + LN_EPS)
    return xc * rstd, rstd


def _ln_mod(x, shift, scale, name):
    l, d = x.shape
    tl = _tile(l, 512)

    def body(x_ref, sh_ref, sc_ref, o_ref):
        xhat, _ = _ln_stats(x_ref[...])
        o_ref[...] = (xhat * (1.0 + sc_ref[...]) + sh_ref[...]).astype(o_ref.dtype)

    row = pl.BlockSpec((tl, d), lambda i: (i, 0))
    vec = pl.BlockSpec((1, d), lambda i: (0, 0))
    return pl.pallas_call(body, name=name, out_shape=jax.ShapeDtypeStruct((l, d), MXU_DTYPE), grid=(l // tl,),
                          in_specs=[row, vec, vec], out_specs=row, compiler_params=_cparams("parallel"))(x, shift, scale)


def _ln_mod_bwd(x, dxm, scale, res, name):
    l, d = x.shape
    tl = _tile(l, 512)
    with_res = res is not None

    def body(*refs):
        if with_res:
            x_ref, g_ref, sc_ref, r_ref, dx_ref, dsh_ref, dsc_ref = refs
        else:
            x_ref, g_ref, sc_ref, dx_ref, dsh_ref, dsc_ref = refs
        i = pl.program_id(0)
        xhat, rstd = _ln_stats(x_ref[...])
        g = g_ref[...].astype(F32)
        dxh = g * (1.0 + sc_ref[...])
        dx = rstd * (dxh - jnp.mean(dxh, axis=-1, keepdims=True) - xhat * jnp.mean(dxh * xhat, axis=-1, keepdims=True))
        dx_ref[...] = dx + r_ref[...].astype(F32) if with_res else dx

        @pl.when(i == 0)
        def _():
            dsh_ref[...] = jnp.zeros_like(dsh_ref)
            dsc_ref[...] = jnp.zeros_like(dsc_ref)

        dsh_ref[...] += jnp.sum(g, axis=0, keepdims=True)
        dsc_ref[...] += jnp.sum(g * xhat, axis=0, keepdims=True)

    row = pl.BlockSpec((tl, d), lambda i: (i, 0))
    vec = pl.BlockSpec((1, d), lambda i: (0, 0))
    args = [x, dxm, scale] + ([res] if with_res else [])
    return pl.pallas_call(
        body, name=name,
        out_shape=(jax.ShapeDtypeStruct((l, d), F32), jax.ShapeDtypeStruct((1, d), F32), jax.ShapeDtypeStruct((1, d), F32)),
        grid=(l // tl,), in_specs=[row, row, vec] + ([row] if with_res else []), out_specs=(row, vec, vec),
        compiler_params=_cparams("arbitrary"))(*args)


def _post_ln_loss(x, out, gate, ln_g, ln_b, target):
    l, d = x.shape
    tl = _tile(l, 512)

    def body(x_ref, o_ref, gate_ref, g_ref, b_ref, t_ref, loss_ref, dout_ref, dxr_ref, dgate_ref, dg_ref, db_ref):
        i = pl.program_id(0)
        out_t = o_ref[...]
        gate_v = gate_ref[...]
        rhat, rstd = _ln_stats(ALPHA * x_ref[...] + gate_v * out_t)
        ln_gv = g_ref[...]
        diff = rhat * ln_gv + b_ref[...] - t_ref[...]
        dy = diff * (1.0 / d)
        drh = dy * ln_gv
        dr = rstd * (drh - jnp.mean(drh, axis=-1, keepdims=True) - rhat * jnp.mean(drh * rhat, axis=-1, keepdims=True))
        dout_ref[...] = (gate_v * dr).astype(dout_ref.dtype)
        dxr_ref[...] = (ALPHA * dr).astype(dxr_ref.dtype)

        @pl.when(i == 0)
        def _():
            for r in (loss_ref, dgate_ref, dg_ref, db_ref):
                r[...] = jnp.zeros_like(r)

        loss_ref[...] += jnp.sum(diff * diff, axis=0, keepdims=True)
        dgate_ref[...] += jnp.sum(dr * out_t, axis=0, keepdims=True)
        dg_ref[...] += jnp.sum(dy * rhat, axis=0, keepdims=True)
        db_ref[...] += jnp.sum(dy, axis=0, keepdims=True)

    row = pl.BlockSpec((tl, d), lambda i: (i, 0))
    vec = pl.BlockSpec((1, d), lambda i: (0, 0))
    v = jax.ShapeDtypeStruct((1, d), F32)
    return pl.pallas_call(
        body, name="post_ln_loss",
        out_shape=(v, jax.ShapeDtypeStruct((l, d), MXU_DTYPE), jax.ShapeDtypeStruct((l, d), MXU_DTYPE), v, v, v),
        grid=(l // tl,), in_specs=[row, row, vec, vec, vec, row], out_specs=(vec, row, row, vec, vec, vec),
        compiler_params=_cparams("arbitrary"))(x, out, gate, ln_g, ln_b, target)


def _ga_forward_tile(p, g, b, ws_ref, bsf, w, nc, nh):
    u_raw, v_raw, za = p[:, :w], p[:, w:2 * w], p[:, 2 * w:3 * w]
    gu = _gelu(u_raw)
    vhat, rstd = _ln_stats(_gelu(v_raw))
    vn = vhat * g + b
    rows = []
    for ci in range(nc):
        r0 = ci * CHUNK
        heads = [_mxu_dot(ws_ref[h], vn[r0:r0 + CHUNK, h * HEAD_DIM_A:(h + 1) * HEAD_DIM_A]) for h in range(nh)]
        rows.append(jnp.concatenate(heads, axis=1) + bsf)
    mixed = jnp.concatenate(rows, axis=0) if nc > 1 else rows[0]
    return u_raw, v_raw, za, gu, vhat, rstd, vn, mixed


def _ga_fwd(proj, g, b, ws, bsf, w):
    l = proj.shape[0]
    nh = w // HEAD_DIM_A
    nc = _tile(l // CHUNK, 2)
    tl = nc * CHUNK

    def body(p_ref, g_ref, b_ref, ws_ref, bsf_ref, o_ref):
        _, _, za, gu, _, _, _, mixed = _ga_forward_tile(p_ref[...], g_ref[...], b_ref[...], ws_ref, bsf_ref[...], w, nc, nh)
        o_ref[...] = (gu * mixed * jax.nn.silu(za)).astype(o_ref.dtype)

    vec = pl.BlockSpec((1, w), lambda i: (0, 0))
    return pl.pallas_call(
        body, name="ga_fwd", out_shape=jax.ShapeDtypeStruct((l, 2 * w), MXU_DTYPE), grid=(l // tl,),
        in_specs=[pl.BlockSpec((tl, 3 * w), lambda i: (i, 0)), vec, vec,
                  pl.BlockSpec((nh, CHUNK, CHUNK), lambda i: (0, 0, 0)), pl.BlockSpec((CHUNK, w), lambda i: (0, 0))],
        out_specs=pl.BlockSpec((tl, w), lambda i: (i, 0)), compiler_params=_cparams("parallel"))(proj, g, b, ws, bsf)


def _ga_bwd(proj, dcat, dproj, g, b, ws, bsf, dys, du0, du1, d_skip, w):
    l = proj.shape[0]
    nh = w // HEAD_DIM_A
    nc = _tile(l // CHUNK, 2)
    tl = nc * CHUNK

    def body(p_ref, dy_ref, dp_in, g_ref, b_ref, ws_ref, bsf_ref, dys_ref, du0_ref, du1_ref, d_ref,
             dp_ref, dg_ref, db_ref, dws_ref, dbsf_ref):
        del dp_in
        i = pl.program_id(0)
        dp_ref[:, 3 * w:] = (dys_ref[...] * d_ref[...] + du0_ref[...] + du1_ref[...]).astype(dp_ref.dtype)
        gv = g_ref[...]
        u_raw, v_raw, za, gu, vhat, rstd, vn, mixed = _ga_forward_tile(
            p_ref[...], gv, b_ref[...], ws_ref, bsf_ref[...], w, nc, nh)
        dya = dy_ref[...].astype(F32)
        sz = jax.nn.silu(za)
        dmixed = dya * gu * sz
        dza = dya * gu * mixed * _silu_grad(za)
        dgu = dya * mixed * sz

        @pl.when(i == 0)
        def _():
            for r in (dg_ref, db_ref, dws_ref, dbsf_ref):
                r[...] = jnp.zeros_like(r)

        rows = []
        for ci in range(nc):
            r0 = ci * CHUNK
            heads = []
            for h in range(nh):
                cols = slice(h * HEAD_DIM_A, (h + 1) * HEAD_DIM_A)
                dm = dmixed[r0:r0 + CHUNK, cols]
                heads.append(_mxu_dot(ws_ref[h], dm, _TN))
                dws_ref[h] += _mxu_dot(dm, vn[r0:r0 + CHUNK, cols], _NT)
            rows.append(jnp.concatenate(heads, axis=1))
            dbsf_ref[...] += dmixed[r0:r0 + CHUNK, :]
        dvn = jnp.concatenate(rows, axis=0) if nc > 1 else rows[0]
        dg_ref[...] += jnp.sum(dvn * vhat, axis=0, keepdims=True)
        db_ref[...] += jnp.sum(dvn, axis=0, keepdims=True)
        dvh = dvn * gv
        dgv = rstd * (dvh - jnp.mean(dvh, axis=-1, keepdims=True) - vhat * jnp.mean(dvh * vhat, axis=-1, keepdims=True))
        dp_ref[:, :w] = (dgu * _gelu_grad(u_raw)).astype(dp_ref.dtype)
        dp_ref[:, w:2 * w] = (dgv * _gelu_grad(v_raw)).astype(dp_ref.dtype)
        dp_ref[:, 2 * w:3 * w] = dza.astype(dp_ref.dtype)

    vec = pl.BlockSpec((1, w), lambda i: (0, 0))
    row = pl.BlockSpec((tl, w), lambda i: (i, 0))
    ws_spec = pl.BlockSpec((nh, CHUNK, CHUNK), lambda i: (0, 0, 0))
    bs_spec = pl.BlockSpec((CHUNK, w), lambda i: (0, 0))
    v = jax.ShapeDtypeStruct((1, w), F32)
    return pl.pallas_call(
        body, name="ga_bwd",
        out_shape=(jax.ShapeDtypeStruct(dproj.shape, dproj.dtype), v, v, jax.ShapeDtypeStruct((nh, CHUNK, CHUNK), F32),
                   jax.ShapeDtypeStruct((CHUNK, w), F32)),
        grid=(l // tl,),
        in_specs=[pl.BlockSpec((tl, 3 * w), lambda i: (i, 0)), row, HBM, vec, vec, ws_spec, bs_spec, row, row, row, vec],
        out_specs=(pl.BlockSpec((tl, 4 * w), lambda i: (i, 0)), vec, vec, ws_spec, bs_spec),
        input_output_aliases={2: 0}, compiler_params=_cparams("arbitrary"))(
            proj, dcat, dproj, g, b, ws, bsf, dys, du0, du1, d_skip)


def _lane_group_sum(x, expand, name):
    return _small_dot(x, expand, "nn", name)


def _disc_math(lr, li, ls, br, bi):
    step = jnp.exp(ls)
    dr, di = lr * step, li * step
    mag = jnp.exp(dr)
    ab_re, ab_im = mag * jnp.cos(di), mag * jnp.sin(di)
    den = lr * lr + li * li
    nr, ni = ab_re - 1.0, ab_im
    f_re = (nr * lr + ni * li) / den
    f_im = (ni * lr - nr * li) / den
    bb_re = f_re * br - f_im * bi
    bb_im = f_re * bi + f_im * br
    return ab_re, ab_im, bb_re, bb_im


def _disc_fwd(lr, li, ls, br, bi):
    def body(lr_ref, li_ref, ls_ref, br_ref, bi_ref, o1, o2, o3, o4):
        res = _disc_math(lr_ref[...], li_ref[...], ls_ref[...], br_ref[...], bi_ref[...])
        for o, r in zip((o1, o2, o3, o4), res):
            o[...] = r

    s = lambda a: jax.ShapeDtypeStruct(a.shape, F32)
    return pl.pallas_call(body, name="s5_disc", out_shape=(s(lr), s(lr), s(br), s(br)), compiler_params=_cparams())(
        lr, li, ls, br, bi)


def _disc_bwd(lr, li, ls, br, bi, d_ar, d_ai, d_br, d_bi):
    def body(lr_ref, li_ref, ls_ref, br_ref, bi_ref, c1, c2, c3, c4, o1, o2, o3, o4, o5):
        _, vjp = jax.vjp(_disc_math, lr_ref[...], li_ref[...], ls_ref[...], br_ref[...], bi_ref[...])
        res = vjp((c1[...], c2[...], c3[...], c4[...]))
        for o, r in zip((o1, o2, o3, o4, o5), res):
            o[...] = r

    s = lambda a: jax.ShapeDtypeStruct(a.shape, F32)
    return pl.pallas_call(body, name="s5_disc_bwd", out_shape=(s(lr), s(lr), s(ls), s(br), s(br)),
                          compiler_params=_cparams())(lr, li, ls, br, bi, d_ar, d_ai, d_br, d_bi)


def _dir_spec(a, dr, **kw):
    return pl.BlockSpec((None,) + a.shape[1:], lambda i: (dr,) + (0,) * (a.ndim - 1), **kw)


def _s5_fwd(u_arr, u_col, w, h0, a_sm, wbr, wbi, cre, ncim, dr, name):
    rev = dr == 1
    l = u_arr.shape[0]
    nb = w // LANES
    spb = wbr.shape[-1]
    nsr = a_sm.shape[2]
    assert 2 * spb == 8 * LANES and nb % 2 == 0
    npair = nb // 2
    t = _tile(l, 256)
    n = l // t
    tile = (lambda i: n - 1 - i) if rev else (lambda i: i)

    def body(u_ref, h0_ref, a_ref, wbr_ref, wbi_ref, cre_ref, ncim_ref, y_ref, hr_ref, hi_ref, tr_ref, ti_ref, hfin_ref,
             carry_ref):
        i = pl.program_id(0)

        @pl.when(i == 0)
        def _():
            carry_ref[...] = h0_ref[...]

        for j in range(npair):
            for h_ref, w_ref in ((hr_ref, wbr_ref), (hi_ref, wbi_ref)):
                blk = [_mxu_dot(u_ref[:, k * LANES:(k + 1) * LANES], w_ref[k]) for k in (2 * j, 2 * j + 1)]
                h_ref[j] = jnp.concatenate(blk, axis=1).reshape(t, 8, LANES)
        slab = lambda ref, part, j: ref[part, 8 * j:8 * j + 8, :]
        ar = [slab(a_ref, 0, j) for j in range(npair)]
        ai = [slab(a_ref, 1, j) for j in range(npair)]

        def steps(blk, c):
            hr, hi = list(c[:npair]), list(c[npair:])
            for q in range(SCAN_UNROLL):
                s = blk * SCAN_UNROLL + q
                row = t - 1 - s if rev else s
                for j in range(npair):
                    hr[j], hi[j] = (ar[j] * hr[j] - ai[j] * hi[j] + hr_ref[j, row],
                                    ar[j] * hi[j] + ai[j] * hr[j] + hi_ref[j, row])
                    hr_ref[j, row] = hr[j]
                    hi_ref[j, row] = hi[j]
            return tuple(hr + hi)

        init = tuple(slab(carry_ref, part, j) for part in range(2) for j in range(npair))
        c = lax.fori_loop(0, t // SCAN_UNROLL, steps, init)
        for part in range(2):
            for j in range(npair):
                carry_ref[part, 8 * j:8 * j + 8, :] = c[part * npair + j]
                hfin_ref[part, 8 * j:8 * j + 8, :] = c[part * npair + j]
        for j in range(npair):
            cols8 = slice(j * 8 * LANES, (j + 1) * 8 * LANES)
            tr_ref[:, cols8] = hr_ref[j].reshape(t, 8 * LANES).astype(tr_ref.dtype)
            ti_ref[:, cols8] = hi_ref[j].reshape(t, 8 * LANES).astype(ti_ref.dtype)
        for k in range(nb):
            cols = slice(k * spb, (k + 1) * spb)
            y_ref[:, k * LANES:(k + 1) * LANES] = (_mxu_dot(tr_ref[:, cols], cre_ref[k]) + _mxu_dot(ti_ref[:, cols], ncim_ref[k]))

    full = lambda a: pl.BlockSpec(a.shape, lambda i: (0,) * a.ndim)
    hspec = pl.BlockSpec((npair, t, 8, LANES), lambda i: (0, tile(i), 0, 0))
    tspec = pl.BlockSpec((t, nsr * LANES), lambda i: (tile(i), 0))
    hsh = jax.ShapeDtypeStruct((npair, l, 8, LANES), F32)
    tsh = jax.ShapeDtypeStruct((l, nsr * LANES), MXU_DTYPE)
    return pl.pallas_call(
        body, name=name,
        out_shape=(jax.ShapeDtypeStruct((l, w), F32), hsh, hsh, tsh, tsh, jax.ShapeDtypeStruct((2, nsr, LANES), F32)),
        grid=(n,),
        in_specs=[pl.BlockSpec((t, w), lambda i: (tile(i), u_col)), full(h0)] + [_dir_spec(a, dr) for a in (a_sm, wbr, wbi, cre, ncim)],
        out_specs=(pl.BlockSpec((t, w), lambda i: (tile(i), 0)), hspec, hspec, tspec, tspec,
                   pl.BlockSpec((2, nsr, LANES), lambda i: (0, 0, 0))),
        scratch_shapes=[pltpu.VMEM((2, nsr, LANES), F32)],
        compiler_params=_cparams("arbitrary"))(u_arr, h0, a_sm, wbr, wbi, cre, ncim)


def _s5_bwd(dys, u_arr, u_col, w, hr, hi, tr, ti, hbound, g_in, a_sm, wbr_t, wbi_t, cre_t, ncim_t, dr, name):
    rev = dr == 1
    l = u_arr.shape[0]
    nb = w // LANES
    spb = wbr_t.shape[-2]
    nsr = a_sm.shape[2]
    npair = nb // 2
    t = _tile(l, 256 if l >= 1024 else 128)
    n = l // t
    with_dy = dys is not None
    tile = (lambda i: i) if rev else (lambda i: n - 1 - i)

    def body(*refs):
        if with_dy:
            (dy_ref, u_ref, hr_ref, hi_ref, pr_ref, pi_ref, tr_ref, ti_ref, hb_ref, gin_ref, a_ref, wbrt_ref, wbit_ref,
             cret_ref, ncimt_ref, du_ref, dwbr_ref, dwbi_ref, dcre_ref, dncim_ref, da_ref, gout_ref,
             gr_ref, gi_ref, gtr_ref, gti_ref, carry_ref) = refs
        else:
            (u_ref, hr_ref, hi_ref, pr_ref, pi_ref, hb_ref, gin_ref, a_ref, wbrt_ref, wbit_ref,
             du_ref, dwbr_ref, dwbi_ref, da_ref, gout_ref, gr_ref, gi_ref, gtr_ref, gti_ref, carry_ref) = refs
        i = pl.program_id(0)

        @pl.when(i == 0)
        def _():
            carry_ref[...] = gin_ref[...]
            accs = (dwbr_ref, dwbi_ref, da_ref) + ((dcre_ref, dncim_ref) if with_dy else ())
            for r in accs:
                r[...] = jnp.zeros_like(r)

        if with_dy:
            for j in range(npair):
                for g_ref, c_ref in ((gr_ref, cret_ref), (gi_ref, ncimt_ref)):
                    blk = [_mxu_dot(dy_ref[:, k * LANES:(k + 1) * LANES], c_ref[k]) for k in (2 * j, 2 * j + 1)]
                    g_ref[j] = jnp.concatenate(blk, axis=1).reshape(t, 8, LANES)
        else:
            gr_ref[...] = jnp.zeros_like(gr_ref)
            gi_ref[...] = jnp.zeros_like(gi_ref)
        slab = lambda ref, part, j: ref[part, 8 * j:8 * j + 8, :]
        last = t - 1 if rev else 0
        first = i == n - 1

        ar = [slab(a_ref, 0, j) for j in range(npair)]
        ai = [slab(a_ref, 1, j) for j in range(npair)]

        def steps(blk, c):
            gr, gi, dr, di = (list(c[q * npair:(q + 1) * npair]) for q in range(4))
            for q in range(SCAN_UNROLL):
                s = blk * SCAN_UNROLL + q
                row = s if rev else t - 1 - s
                prow = jnp.minimum(row + 1, t - 1) if rev else jnp.maximum(row - 1, 0)
                for j in range(npair):
                    pr, pi = hr_ref[j, prow], hi_ref[j, prow]
                    gr[j], gi[j] = (gr_ref[j, row] + ar[j] * gr[j] + ai[j] * gi[j],
                                    gi_ref[j, row] + ar[j] * gi[j] - ai[j] * gr[j])
                    gr_ref[j, row] = gr[j]
                    gi_ref[j, row] = gi[j]
                    dr[j], di[j] = dr[j] + gr[j] * pr + gi[j] * pi, di[j] + gi[j] * pr - gr[j] * pi
            return tuple(gr + gi + dr + di)

        init = tuple(slab(ref, part, j) for ref in (carry_ref, da_ref) for part in range(2) for j in range(npair))
        c = lax.fori_loop(0, t // SCAN_UNROLL, steps, init)
        gr, gi, dr, di = (c[q * npair:(q + 1) * npair] for q in range(4))
        for j in range(npair):
            pr = jnp.where(first, slab(hb_ref, 0, j), pr_ref[j, 0]) - hr_ref[j, last]
            pi = jnp.where(first, slab(hb_ref, 1, j), pi_ref[j, 0]) - hi_ref[j, last]
            rows = slice(8 * j, 8 * j + 8)
            da_ref[0, rows, :] = dr[j] + gr[j] * pr + gi[j] * pi
            da_ref[1, rows, :] = di[j] + gi[j] * pr - gr[j] * pi
            for part, val in enumerate((gr[j], gi[j])):
                carry_ref[part, rows, :] = val
                gout_ref[part, rows, :] = val

        for j in range(npair):
            cols8 = slice(j * 8 * LANES, (j + 1) * 8 * LANES)
            gtr_ref[:, cols8] = gr_ref[j].reshape(t, 8 * LANES).astype(gtr_ref.dtype)
            gti_ref[:, cols8] = gi_ref[j].reshape(t, 8 * LANES).astype(gti_ref.dtype)
        for k in range(nb):
            cols = slice(k * spb, (k + 1) * spb)
            lanes = slice(k * LANES, (k + 1) * LANES)
            du_ref[:, lanes] = _mxu_dot(gtr_ref[:, cols], wbrt_ref[k]) + _mxu_dot(gti_ref[:, cols], wbit_ref[k])
            dwbr_ref[k] += _mxu_dot(u_ref[:, lanes], gtr_ref[:, cols], _TN)
            dwbi_ref[k] += _mxu_dot(u_ref[:, lanes], gti_ref[:, cols], _TN)
            if with_dy:
                dcre_ref[k] += _mxu_dot(tr_ref[:, cols], dy_ref[:, lanes], _TN)
                dncim_ref[k] += _mxu_dot(ti_ref[:, cols], dy_ref[:, lanes], _TN)

    once = dict(pipeline_mode=pl.Buffered(1))
    full = lambda a: pl.BlockSpec(a.shape, lambda i: (0,) * a.ndim, **once)
    row = lambda cb: pl.BlockSpec((t, w), lambda i: (tile(i), cb))
    hspec = pl.BlockSpec((npair, t, 8, LANES), lambda i: (0, tile(i), 0, 0))
    if rev:
        pspec = pl.BlockSpec((npair, 1, 8, LANES), lambda i: (0, jnp.minimum((tile(i) + 1) * t, l - 1), 0, 0))
    else:
        pspec = pl.BlockSpec((npair, 1, 8, LANES), lambda i: (0, jnp.maximum(tile(i) * t - 1, 0), 0, 0))
    sm = jax.ShapeDtypeStruct((2, nsr, LANES), F32)
    smspec = pl.BlockSpec((2, nsr, LANES), lambda i: (0, 0, 0))
    wsh = jax.ShapeDtypeStruct((nb, LANES, spb), F32)
    csh = jax.ShapeDtypeStruct((nb, spb, LANES), F32)
    tspec = pl.BlockSpec((t, nsr * LANES), lambda i: (tile(i), 0))
    in_specs = (([row(0)] if with_dy else []) + [row(u_col), hspec, hspec, pspec, pspec] + ([tspec, tspec] if with_dy else [])
                + [full(hbound), full(g_in)]
                + [_dir_spec(a, dr, **once) for a in (a_sm, wbr_t, wbi_t) + ((cre_t, ncim_t) if with_dy else ())])
    args = (([dys] if with_dy else []) + [u_arr, hr, hi, hr, hi] + ([tr, ti] if with_dy else [])
            + [hbound, g_in, a_sm, wbr_t, wbi_t] + ([cre_t, ncim_t] if with_dy else []))
    out_shape = (jax.ShapeDtypeStruct((l, w), F32), wsh, wsh) + ((csh, csh) if with_dy else ()) + (sm, sm)
    out_specs = (row(0), full(wsh), full(wsh)) + ((full(csh), full(csh)) if with_dy else ()) + (smspec, smspec)
    return pl.pallas_call(
        body, name=name, out_shape=out_shape, grid=(n,), in_specs=in_specs, out_specs=out_specs,
        scratch_shapes=[pltpu.VMEM((npair, t, 8, LANES), F32)] * 2 + [pltpu.VMEM((t, nsr * LANES), MXU_DTYPE)] * 2
        + [pltpu.VMEM((2, nsr, LANES), F32)],
        compiler_params=_cparams("arbitrary"))(*args)


def _glu_fwd(y0, y1, proj, cat, d_skip, w_glu, b_glu, w):
    l = y0.shape[0]
    tl = _tile(l, 512)

    def body(y0_ref, y1_ref, u_ref, z_ref, cat_in, d_ref, wg_ref, bg_ref, ys_ref, cat_ref):
        del cat_in
        ys = y0_ref[...] + y1_ref[...] + d_ref[...] * u_ref[...]
        ys_ref[...] = ys
        gy = _gelu(ys)
        s = _mxu_dot(gy, wg_ref[...]) + bg_ref[...]
        cat_ref[...] = (gy * jax.nn.sigmoid(s) * jax.nn.silu(z_ref[...])).astype(cat_ref.dtype)

    row = pl.BlockSpec((tl, w), lambda i: (i, 0))
    vec = pl.BlockSpec((1, w), lambda i: (0, 0))
    return pl.pallas_call(
        body, name="glu_fwd", out_shape=(jax.ShapeDtypeStruct((l, w), F32), jax.ShapeDtypeStruct(cat.shape, cat.dtype)),
        grid=(l // tl,),
        in_specs=[row, row, pl.BlockSpec((tl, w), lambda i: (i, 3)), pl.BlockSpec((tl, w), lambda i: (i, 4)), HBM,
                  vec, pl.BlockSpec((w, w), lambda i: (0, 0), pipeline_mode=pl.Buffered(1)), vec],
        out_specs=(row, pl.BlockSpec((tl, w), lambda i: (i, 1))), input_output_aliases={4: 1},
        compiler_params=_cparams("parallel"))(y0, y1, proj, proj, cat, d_skip, w_glu, b_glu)


def _glu_bwd(dcat, ys, proj, w_glu, b_glu, w, dep):
    l = ys.shape[0]
    tl = _tile(l, 512)

    def body(dy_ref, ys_ref, u_ref, z_ref, wg_ref, bg_ref, dep_ref, dys_ref, dp_ref, dbg_ref, dd_ref, dwg_ref):
        del dep_ref
        i = pl.program_id(0)
        ys_t = ys_ref[...]
        z = z_ref[...]
        dyb = dy_ref[...].astype(F32)
        gy = _gelu(ys_t)
        sg = jax.nn.sigmoid(_mxu_dot(gy, wg_ref[...]) + bg_ref[...])
        dp_ref[...] = (dyb * gy * sg * _silu_grad(z)).astype(dp_ref.dtype)
        dglu = dyb * jax.nn.silu(z)
        ds = dglu * gy * sg * (1.0 - sg)
        dgy = dglu * sg + _mxu_dot(ds, wg_ref[...], _NT)
        dys_t = dgy * _gelu_grad(ys_t)
        dys_ref[...] = dys_t

        @pl.when(i == 0)
        def _():
            for r in (dbg_ref, dd_ref, dwg_ref):
                r[...] = jnp.zeros_like(r)

        dbg_ref[...] += jnp.sum(ds, axis=0, keepdims=True)
        dd_ref[...] += jnp.sum(dys_t * u_ref[...], axis=0, keepdims=True)
        dwg_ref[...] += _mxu_dot(gy, ds, _TN)

    row = pl.BlockSpec((tl, w), lambda i: (i, 0))
    vec = pl.BlockSpec((1, w), lambda i: (0, 0))
    mat = pl.BlockSpec((w, w), lambda i: (0, 0), pipeline_mode=pl.Buffered(1))
    v = jax.ShapeDtypeStruct((1, w), F32)
    return pl.pallas_call(
        body, name="glu_bwd",
        out_shape=(jax.ShapeDtypeStruct((l, w), F32), jax.ShapeDtypeStruct((l, 5 * w), MXU_DTYPE), v, v,
                   jax.ShapeDtypeStruct((w, w), F32)),
        grid=(l // tl,),
        in_specs=[pl.BlockSpec((tl, w), lambda i: (i, 1)), row, pl.BlockSpec((tl, w), lambda i: (i, 3)),
                  pl.BlockSpec((tl, w), lambda i: (i, 4)), mat, vec, HBM],
        out_specs=(row, pl.BlockSpec((tl, w), lambda i: (i, 4)), vec, vec, mat),
        compiler_params=_cparams("arbitrary"))(dcat, ys, proj, proj, w_glu, b_glu, dep)


def _add2(a, b, name):
    l, w = a.shape
    tl = _tile(l, 512)

    def body(a_ref, b_ref, o_ref):
        o_ref[...] = a_ref[...] + b_ref[...]

    row = pl.BlockSpec((tl, w), lambda i: (i, 0))
    return pl.pallas_call(body, name=name, out_shape=jax.ShapeDtypeStruct((l, w), F32), grid=(l // tl,),
                          in_specs=[row, row], out_specs=row, compiler_params=_cparams("parallel"))(a, b)


def _adamw_nd(w, m, v, g, name):
    shape = w.shape
    lead = math.prod(shape[:-2]) if len(shape) > 2 else 1
    b, c = (shape[-2], shape[-1]) if len(shape) >= 2 else (1, shape[-1])
    t3 = (lead, b, c)
    padded_row = -(-b // 8) * 8 * -(-c // LANES) * LANES * 4
    ta = _tile(lead, max(1, (2 << 20) // padded_row))

    def body(w_ref, m_ref, v_ref, g_ref, d_ref, mo_ref, vo_ref):
        gv = g_ref[...]
        mn = ADAM_B1 * m_ref[...] + (1.0 - ADAM_B1) * gv
        vn = ADAM_B2 * v_ref[...] + (1.0 - ADAM_B2) * (gv * gv)
        m_hat = mn / (1.0 - ADAM_B1 ** ADAM_STEP)
        v_hat = vn / (1.0 - ADAM_B2 ** ADAM_STEP)
        d_ref[...] = -ADAM_LR * (m_hat / (jnp.sqrt(v_hat) + ADAM_EPS) + ADAM_WD * w_ref[...])
        mo_ref[...] = mn
        vo_ref[...] = vn

    blk = pl.BlockSpec((ta, b, c), lambda i: (i, 0, 0))
    s = jax.ShapeDtypeStruct(t3, F32)
    outs = pl.pallas_call(body, name=name, out_shape=(s, s, s), grid=(lead // ta,), in_specs=[blk] * 4, out_specs=(blk,) * 3,
                          compiler_params=_cparams("parallel"))(*[a.reshape(t3) for a in (w, m, v, g)])
    return tuple(o.reshape(shape) for o in outs)


def _adamw(w, m, v, gparts, name, dep=None):
    r, c = w.shape
    np_ = gparts.shape[0]
    tr = _tile(r, max(8, (1 << 18) // c), 8)

    def body(w_ref, m_ref, v_ref, g_ref, *rest):
        go_ref, d_ref, mo_ref, vo_ref = rest[-4:]
        g = g_ref[0].astype(F32)
        for p in range(1, np_):
            g = g + g_ref[p].astype(F32)
        mn = ADAM_B1 * m_ref[...] + (1.0 - ADAM_B1) * g
        vn = ADAM_B2 * v_ref[...] + (1.0 - ADAM_B2) * (g * g)
        m_hat = mn / (1.0 - ADAM_B1 ** ADAM_STEP)
        v_hat = vn / (1.0 - ADAM_B2 ** ADAM_STEP)
        go_ref[...] = g
        d_ref[...] = -ADAM_LR * (m_hat / (jnp.sqrt(v_hat) + ADAM_EPS) + ADAM_WD * w_ref[...])
        mo_ref[...] = mn
        vo_ref[...] = vn

    row = pl.BlockSpec((tr, c), lambda i: (i, 0))
    s = jax.ShapeDtypeStruct((r, c), F32)
    extra = [] if dep is None else [dep]
    return pl.pallas_call(body, name=name, out_shape=(s, s, s, s), grid=(r // tr,),
                          in_specs=[row, row, row, pl.BlockSpec((np_, tr, c), lambda i: (0, i, 0))] + [HBM] * len(extra),
                          out_specs=(row, row, row, row), compiler_params=_cparams("parallel"))(w, m, v, gparts, *extra)


def _sum_slots(parts, name):
    np_, r, c = parts.shape

    def body(p_ref, o_ref):
        g = p_ref[0]
        for p in range(1, np_):
            g = g + p_ref[p]
        o_ref[...] = g

    return pl.pallas_call(body, name=name, out_shape=jax.ShapeDtypeStruct((r, c), F32), compiler_params=_cparams())(parts)


def _block_diag(x, gb):
    nd, g, a, b = x.shape
    eye = jnp.eye(gb, dtype=x.dtype)
    y = jnp.einsum("dkgab,gh->dkgahb", x.reshape(nd, g // gb, gb, a, b), eye)
    return y.reshape(nd, g // gb, gb * a, gb * b)


def _block_diag_extract(y, gb, a, b):
    nd, nbk = y.shape[:2]
    eye = jnp.eye(gb, dtype=y.dtype)
    x = jnp.einsum("dkgahb,gh->dkgab", y.reshape(nd, nbk, gb, a, gb, b), eye)
    return x.reshape(nd, nbk * gb, a, b)


def kernel(x, c, ctx, c_ctx, w_ada, b_ada, w_in, sgu_ln_g, sgu_ln_b, w_spatial, b_spatial, s5_lam_re, s5_lam_im, s5_log_step, s5_b_re, s5_b_im, s5_c_re, s5_c_im, s5_d, w_glu, b_glu, w_out, ln_g, ln_b, loss_target, m_c_ctx, m_w_ada, m_b_ada, m_w_in, m_sgu_ln_g, m_sgu_ln_b, m_w_spatial, m_b_spatial, m_s5_lam_re, m_s5_lam_im, m_s5_log_step, m_s5_b_re, m_s5_b_im, m_s5_c_re, m_s5_c_im, m_s5_d, m_w_glu, m_b_glu, m_w_out, m_ln_g, m_ln_b, v_c_ctx, v_w_ada, v_b_ada, v_w_in, v_sgu_ln_g, v_sgu_ln_b, v_w_spatial, v_b_spatial, v_s5_lam_re, v_s5_lam_im, v_s5_log_step, v_s5_b_re, v_s5_b_im, v_s5_c_re, v_s5_c_im, v_s5_d, v_w_glu, v_b_glu, v_w_out, v_ln_g, v_ln_b):
    small_names = ["c_ctx", "b_ada", "sgu_ln_g", "sgu_ln_b", "w_spatial", "b_spatial", "s5_lam_re", "s5_lam_im",
                   "s5_log_step", "s5_b_re", "s5_b_im", "s5_c_re", "s5_c_im", "s5_d", "b_glu", "ln_g", "ln_b"]
    env = dict(locals())
    x2, tgt, ctx2 = x[0], loss_target[0], ctx[0]
    l, d = x2.shape
    lc = ctx2.shape[0]
    w = d // 2
    nh = w // HEAD_DIM_A
    nd, g_s5, p_s5, c_s5 = s5_b_re.shape[1:]
    ns = g_s5 * p_s5
    nsr = ns // LANES
    gb = LANES // c_s5
    me = _index(_mesh_pos())
    ada_cols = w_ada.shape[2]

    srows = _silu_rows(c, c_ctx)
    srows_all = _all_gather(srows, 0, "gather_silu")
    s_mat = jnp.concatenate([srows_all[0::8], srows_all[1:2], jnp.zeros((7, d), F32)], axis=0)
    mod_part = _small_dot(s_mat, w_ada[0], "nn", "mod_cols")
    mod_all = _all_gather(mod_part, 1, "gather_mod") + b_ada
    hw_in, tok_a = _exchange_start(w_in[0].astype(MXU_DTYPE), 1, "gather", "start_gather_w_in", (SIBLING,) + SAME_CORE_PEERS)
    mod_all = mod_all + tok_a[0, 0]
    mod_x = lax.dynamic_slice_in_dim(mod_all, me, 1, axis=0)
    mod_c = mod_all[8:9]
    shift_x, scale_x, gate_x = mod_x[:, :d], mod_x[:, d:2 * d], mod_x[:, 2 * d:]
    shift_c, scale_c = mod_c[:, :d], mod_c[:, d:2 * d]

    lr, li = s5_lam_re[0][:, :, None, :], s5_lam_im[0][:, :, None, :]
    ls = s5_log_step[0][:, :, None, None]
    swapped = ("s5_b_re", "s5_b_im")
    for nm in swapped:
        for pre in ("", "m_", "v_"):
            env[pre + nm] = jnp.swapaxes(env[pre + nm], -1, -2)
    br_t, bi_t = env["s5_b_re"][0], env["s5_b_im"][0]
    ab_re, ab_im, bb_re, bb_im = _disc_fwd(lr, li, ls, br_t, bi_t)
    a_sm = jnp.stack([ab_re, ab_im], axis=1).reshape(nd, 2, nsr, LANES)
    wbr = _block_diag(bb_re.astype(MXU_DTYPE), gb)
    wbi = _block_diag(bb_im.astype(MXU_DTYPE), gb)
    cre_t = _block_diag(s5_c_re[0].astype(MXU_DTYPE), gb)
    ncim_t = _block_diag((-s5_c_im[0]).astype(MXU_DTYPE), gb)
    cre, ncim = jnp.swapaxes(cre_t, 2, 3), jnp.swapaxes(ncim_t, 2, 3)
    wbr_t, wbi_t = jnp.swapaxes(wbr, 2, 3), jnp.swapaxes(wbi, 2, 3)
    d_skip = s5_d

    xm = _ln_mod(x2, shift_x, scale_x, "ln_mod_x")
    cm = _ln_mod(ctx2, shift_c, scale_c, "ln_mod_ctx")
    ready = xm[:8, :LANES].astype(F32) + cm[:8, :LANES].astype(F32) + cre[0, 0, :8, :].astype(F32)
    hw_in2, tok_b = _forward_start(_exchange_wait(hw_in, ready, "wait_gather_w_in"), 1, "start_forward_w_in")
    w_in_f = _forward_wait(hw_in2, tok_b, "wait_forward_w_in")
    hw_glu, tok_c = _exchange_start(w_glu[0].astype(MXU_DTYPE), 0, "gather", "start_gather_w_glu")
    hw_out, tok_o = _exchange_start(w_out[0].astype(MXU_DTYPE), 0, "gather", "start_gather_w_out")
    proj = _matmul(xm, w_in_f, mode="nn", name="proj", dep=tok_c + tok_o)
    ub_c = _matmul(cm, w_in_f, mode="nn", name="proj_ctx", b_n0=3 * w, n=w)
    bsf = jnp.repeat(b_spatial[0].T, HEAD_DIM_A, axis=1)
    ws = w_spatial[0]
    cat = _ga_fwd(proj, sgu_ln_g, sgu_ln_b, ws, bsf, w)
    zeros_state = jnp.zeros((2, nsr, LANES), F32)
    s5c, s5l = [], []
    for dr in range(nd):
        s5c.append(_s5_fwd(ub_c, 0, w, zeros_state, a_sm, wbr, wbi, cre, ncim, dr, f"s5_fwd_ctx{dr}"))
        s5l.append(_s5_fwd(proj, 3, w, s5c[dr][5], a_sm, wbr, wbi, cre, ncim, dr, f"s5_fwd{dr}"))
    w_glu_f = _exchange_wait(hw_glu, s5l[1][0], "wait_gather_w_glu")
    ys, cat = _glu_fwd(s5l[0][0], s5l[1][0], proj, cat, d_skip, w_glu_f, b_glu, w)
    w_out_f = _exchange_wait(hw_out, ys, "wait_gather_w_out")
    out = _matmul(cat, w_out_f, mode="nn", name="out_proj")
    loss_row, dout, dx_res, dgate, dln_g, dln_b = _post_ln_loss(x2, out, gate_x, ln_g, ln_b, tgt)

    dcat = _matmul(dout, w_out_f, mode="nt", name="d_cat", out_dtype=MXU_DTYPE)
    dw_out = _matmul(cat, dout, mode="tn", name="d_w_out", out_dtype=MXU_DTYPE)
    hg_out, tok_d = _exchange_start(dw_out, 0, "a2a", "start_a2a_d_w_out")
    dys, dproj, db_glu, dd_skip, dw_glu = _glu_bwd(dcat, ys, proj, w_glu_f, b_glu, w, tok_d)
    hg_glu, tok_e = _exchange_start(dw_glu.astype(MXU_DTYPE), 0, "a2a", "start_a2a_d_w_glu")
    zeros_state = zeros_state + tok_e[0, 0]
    du_l, du_c, dwbr, dwbi, dcre, dncim, da_sm = [], [], [], [], [], [], []
    nbk, spb = w // LANES, gb * p_s5
    for dr in range(nd):
        bl = _s5_bwd(dys, proj, 3, w, *s5l[dr][1:5], s5c[dr][5], zeros_state, a_sm, wbr_t, wbi_t,
                     cre_t, ncim_t, dr, f"s5_bwd{dr}")
        bc = _s5_bwd(None, ub_c, 0, w, s5c[dr][1], s5c[dr][2], None, None, zeros_state, bl[6], a_sm, wbr_t, wbi_t,
                     None, None, dr, f"s5_bwd_ctx{dr}")
        du_l.append(bl[0])
        du_c.append(bc[0])
        dwbr.append(_add2(bl[1].reshape(nbk * LANES, spb), bc[1].reshape(nbk * LANES, spb), f"sum_dwbr{dr}"))
        dwbi.append(_add2(bl[2].reshape(nbk * LANES, spb), bc[2].reshape(nbk * LANES, spb), f"sum_dwbi{dr}"))
        dcre.append(bl[3])
        dncim.append(bl[4])
        da_sm.append(_add2(bl[5].reshape(2 * nsr, LANES), bc[3].reshape(2 * nsr, LANES), f"sum_da{dr}"))
    dub_c = _add2(du_c[0], du_c[1], "dub_ctx")
    dwbr = jnp.stack(dwbr).reshape(nd, nbk, LANES, spb)
    dwbi = jnp.stack(dwbi).reshape(nd, nbk, LANES, spb)
    dcre, dncim = jnp.stack(dcre), jnp.stack(dncim)
    da_sm = jnp.stack(da_sm).reshape(nd, 2, g_s5, p_s5)
    dproj, dsg, dsb, dws, dbsf = _ga_bwd(proj, dcat, dproj, sgu_ln_g, sgu_ln_b, ws, bsf, dys, du_l[0], du_l[1], d_skip, w)

    dbb_re = _block_diag_extract(dwbr, gb, c_s5, p_s5)
    dbb_im = _block_diag_extract(dwbi, gb, c_s5, p_s5)
    dc_re = jnp.swapaxes(_block_diag_extract(dcre, gb, p_s5, c_s5), 2, 3)
    dc_im = -jnp.swapaxes(_block_diag_extract(dncim, gb, p_s5, c_s5), 2, 3)
    dlr, dli, dls, db_re, db_im = _disc_bwd(lr, li, ls, br_t, bi_t, da_sm[:, 0:1].reshape(nd, g_s5, 1, p_s5),
                                            da_sm[:, 1:2].reshape(nd, g_s5, 1, p_s5), dbb_re, dbb_im)
    expand = (jnp.arange(w)[:, None] // HEAD_DIM_A == jnp.arange(LANES)[None, :]).astype(F32)
    db_sp = _lane_group_sum(dbsf, expand, "d_b_spatial")[:, :nh].T

    local = {"sgu_ln_g": dsg, "sgu_ln_b": dsb, "w_spatial": dws, "b_spatial": db_sp,
             "s5_lam_re": dlr, "s5_lam_im": dli, "s5_log_step": dls, "s5_b_re": db_re, "s5_b_im": db_im,
             "s5_c_re": dc_re, "s5_c_im": dc_im, "s5_d": dd_skip, "b_glu": db_glu, "ln_g": dln_g, "ln_b": dln_b}
    reduced = sorted(local, key=lambda n: -math.prod(env[n].shape))
    loss_part = (0.5 / d) * jnp.sum(loss_row)
    flat = jnp.concatenate([local[n].reshape(-1) for n in reduced] + [loss_part.reshape(1)])
    unit = N_DEV * 8 * LANES
    total = -(-flat.shape[0] // unit) * unit
    flat = jnp.pad(flat, (0, total - flat.shape[0])).reshape(N_DEV * 8, total // (N_DEV * 8))
    h_small, tok_s = _exchange_start(flat, 0, "a2a", "start_a2a_small")

    dw_in = _matmul(xm, dproj, mode="tn", name="d_w_in", out_dtype=MXU_DTYPE, dep=tok_s)
    dw_in = _matmul(cm, dub_c, mode="tn", name="d_w_in_ctx", acc_in=dw_in, acc_n0=3 * w, out_dtype=MXU_DTYPE)
    hg_in, tok_f = _exchange_start(dw_in, 1, "a2a", "start_a2a_d_w_in")
    mine = _sum_slots(_exchange_wait(h_small, dw_in, "wait_a2a_small"), "sum_small")
    h_sums, tok_g = _exchange_start(mine, 0, "gather", "start_gather_small")
    dxm = _matmul(dproj, w_in_f, mode="nt", name="d_xm", dep=tok_f + tok_g, out_dtype=MXU_DTYPE)
    dcm = _matmul(dub_c, w_in_f, mode="nt", name="d_cm", b_k0=3 * w, k=w)
    grad_x, dshift_x, dscale_x = _ln_mod_bwd(x2, dxm, scale_x, dx_res, "ln_mod_x_bwd")
    _, dshift_c, dscale_c = _ln_mod_bwd(ctx2, dcm, scale_c, None, "ln_mod_ctx_bwd")

    dmod_rows = jnp.concatenate([jnp.concatenate([dshift_x, dscale_x, dgate], axis=1),
                                 jnp.concatenate([dshift_c, dscale_c, jnp.zeros((1, d), F32)], axis=1),
                                 jnp.zeros((6, 3 * d), F32)], axis=0)
    h_dmod, tok_m = _exchange_start(dmod_rows, 0, "gather", "start_gather_dmod")

    gp_w_out = _exchange_wait(hg_out, tok_m, "wait_a2a_d_w_out")
    gp_w_glu = _exchange_wait(hg_glu, tok_m, "wait_a2a_d_w_glu")
    gp_w_in = _exchange_wait(hg_in, tok_m, "wait_a2a_d_w_in")
    big = {
        "w_in": _adamw(w_in[0], m_w_in[0], v_w_in[0], gp_w_in, "adamw_w_in"),
        "w_glu": _adamw(w_glu[0], m_w_glu[0], v_w_glu[0], gp_w_glu, "adamw_w_glu"),
        "w_out": _adamw(w_out[0], m_w_out[0], v_w_out[0], gp_w_out, "adamw_w_out"),
    }
    dmod_all = _exchange_wait(h_dmod, big["w_out"][0], "wait_gather_dmod")
    dmod_ctx = _sum_slots(dmod_all[1::8].reshape(N_DEV, 1, 3 * d), "sum_dmod_ctx")
    dmod_mat = jnp.concatenate([dmod_all[0::8], dmod_ctx, jnp.zeros((7, 3 * d), F32)], axis=0)
    db_ada = _sum_slots(dmod_mat[:9].reshape(9, 1, 3 * d), "sum_db_ada")
    dmod_mine = lax.dynamic_slice_in_dim(dmod_mat, me * ada_cols, ada_cols, axis=1)
    dw_ada = _small_dot(s_mat, dmod_mine, "tn", "d_w_ada")
    dsilu_cc = _small_dot(dmod_mine[8:16], w_ada[0], "nt", "d_silu_cctx")[0:1]
    dc_ctx_part = dsilu_cc * _silu_grad(c_ctx.reshape(1, d))
    dc_ctx_rows = jnp.concatenate([dc_ctx_part, jnp.zeros((7, d), F32)], axis=0)
    h_cctx, tok_c2 = _exchange_start(dc_ctx_rows, 0, "gather", "start_gather_d_c_ctx")

    big["w_ada"] = _adamw(w_ada[0], m_w_ada[0], v_w_ada[0], dw_ada[None], "adamw_w_ada", dep=tok_c2)
    summed = _exchange_wait(h_sums, big["w_ada"][0], "wait_gather_small").reshape(-1)
    grads, off = {"b_ada": db_ada}, 0
    for n in reduced:
        size = math.prod(env[n].shape)
        grads[n] = summed[off:off + size].reshape(env[n].shape)
        off += size
    loss = summed[off]
    res = {n: tuple(a[None] for a in big[n]) for n in big}

    def small_step(n):
        res[n] = (grads[n],) + _adamw_nd(env[n], env["m_" + n], env["v_" + n], grads[n], "adamw_" + n)
        if n in swapped:
            res[n] = tuple(jnp.swapaxes(a, -1, -2) for a in res[n])

    for n in small_names:
        if n != "c_ctx":
            small_step(n)
    dc_ctx_all = _exchange_wait(h_cctx, res["w_spatial"][1], "wait_gather_d_c_ctx")
    grads["c_ctx"] = _sum_slots(dc_ctx_all[0::8].reshape(N_DEV, 1, d), "sum_d_c_ctx").reshape(d)
    small_step("c_ctx")

    order = ["c_ctx", "w_ada", "b_ada", "w_in", "sgu_ln_g", "sgu_ln_b", "w_spatial", "b_spatial", "s5_lam_re", "s5_lam_im",
             "s5_log_step", "s5_b_re", "s5_b_im", "s5_c_re", "s5_c_im", "s5_d", "w_glu", "b_glu", "w_out", "ln_g", "ln_b"]
    return (loss, grad_x[None], *[res[n][0] for n in order], *[res[n][1] for n in order],
            *[res[n][2] for n in order], *[res[n][3] for n in order])
```

```python
import functools
import math

import jax
import jax.numpy as jnp
from jax import lax
from jax.experimental import pallas as pl
from jax.experimental.pallas import tpu as pltpu

F32 = jnp.float32
MXU_DTYPE = jnp.bfloat16
N_DEV = 8
MESH_ID = pl.DeviceIdType.MESH
LN_EPS = 1e-6
DEPTH = 1
ALPHA = (2.0 * DEPTH) ** 0.25
CHUNK = 128
HEAD_DIM_A = 128
ADAM_LR, ADAM_B1, ADAM_B2, ADAM_EPS, ADAM_WD, ADAM_STEP = 0.001, 0.9, 0.999, 1e-08, 0.01, 10
LANES = 128
SCAN_UNROLL = 8
VMEM_LIMIT = 56 * 1024 * 1024
HBM = pl.BlockSpec(memory_space=pl.ANY)


def _cparams(*sem):
    return pltpu.CompilerParams(dimension_semantics=sem if sem else None, vmem_limit_bytes=VMEM_LIMIT)


def _tile(n, pref, mult=1):
    if n <= pref:
        return n
    t = pref - pref % mult
    while n % t:
        t -= mult
    return t


def _gelu(x):
    return 0.5 * x * (1.0 + lax.erf(x * (1.0 / math.sqrt(2.0))))


def _gelu_grad(x):
    return 0.5 * (1.0 + lax.erf(x * (1.0 / math.sqrt(2.0)))) + x * jnp.exp(-0.5 * x * x) * (1.0 / math.sqrt(2.0 * math.pi))


def _silu_grad(x):
    s = jax.nn.sigmoid(x)
    return s * (1.0 + x * (1.0 - s))


def _mxu_dot(a, b, dims=(((1,), (0,)), ((), ()))):
    return lax.dot_general(a.astype(MXU_DTYPE), b.astype(MXU_DTYPE), dims, preferred_element_type=F32)


_NT = (((1,), (1,)), ((), ()))
_TN = (((0,), (0,)), ((), ()))


def _mesh_pos():
    return lax.axis_index("x"), lax.axis_index("y"), lax.axis_index("c")


def _peer(pos, r):
    x, y, c = pos
    return ((1 - x) if r & 4 else x, (1 - y) if r & 2 else y, (1 - c) if r & 1 else c)


def _index(pos):
    return 4 * pos[0] + 2 * pos[1] + pos[2]


def _slice_of(ref, axis, idx, size):
    start = idx * size
    if axis == 0:
        return ref.at[pl.ds(start, size)]
    return ref.at[:, pl.ds(start, size)]


def _all_gather(x, axis, name):
    size = x.shape[axis]
    out_shape = tuple(s * N_DEV if a == axis else s for a, s in enumerate(x.shape))

    def body(x_ref, o_ref, send_sems, recv_sems, local_sem):
        me = _mesh_pos()
        mine = pltpu.make_async_copy(x_ref, _slice_of(o_ref, axis, _index(me), size), local_sem)
        mine.start()

        def copy(r, block):
            return pltpu.make_async_remote_copy(
                src_ref=x_ref, dst_ref=_slice_of(o_ref, axis, _index(block), size),
                send_sem=send_sems.at[r - 1], recv_sem=recv_sems.at[r - 1],
                device_id=_peer(me, r), device_id_type=MESH_ID)

        sends = [copy(r, me) for r in range(1, N_DEV)]
        for cp in sends:
            cp.start()
        for r in range(1, N_DEV):
            copy(r, _peer(me, r)).wait_recv()
        for cp in sends:
            cp.wait_send()
        mine.wait()

    return pl.pallas_call(
        body, name=name, out_shape=jax.ShapeDtypeStruct(out_shape, x.dtype),
        in_specs=[HBM], out_specs=HBM,
        scratch_shapes=[pltpu.SemaphoreType.DMA((N_DEV - 1,)), pltpu.SemaphoreType.DMA((N_DEV - 1,)),
                        pltpu.SemaphoreType.DMA],
    )(x)


_SEM = pl.BlockSpec(memory_space=pltpu.SEMAPHORE)
_HBM = pl.BlockSpec(memory_space=pltpu.HBM)
_EFFECT = pltpu.SideEffectType.DATAFLOW_SIDE_EFFECTING
ALL_PEERS = tuple(range(1, N_DEV))
SIBLING = 1
SAME_CORE_PEERS = (2, 4, 6)


def _exchange_copy(kind, x_ref, land_ref, axis, size, send_sems, recv_sems, me, rels, q, arriving):
    peer = _peer(me, rels[q])
    sender, receiver = (peer, me) if arriving else (me, peer)
    if kind == "gather":
        src, dst = x_ref, _slice_of(land_ref, axis, _index(sender), size)
    else:
        src, dst = _slice_of(x_ref, axis, _index(receiver), size), land_ref.at[_index(sender)]
    return pltpu.make_async_remote_copy(src_ref=src, dst_ref=dst, send_sem=send_sems.at[q], recv_sem=recv_sems.at[q],
                                        device_id=peer, device_id_type=MESH_ID)


def _local_copy(kind, x_ref, land_ref, axis, size, me, local_sem):
    if kind == "gather":
        return pltpu.make_async_copy(x_ref, _slice_of(land_ref, axis, _index(me), size), local_sem)
    return pltpu.make_async_copy(_slice_of(x_ref, axis, _index(me), size), land_ref.at[_index(me)], local_sem)


def _exchange_start(x, axis, kind, name, rels=ALL_PEERS):
    size = x.shape[axis] if kind == "gather" else x.shape[axis] // N_DEV
    if kind == "gather":
        land_shape = tuple(s * N_DEV if a == axis else s for a, s in enumerate(x.shape))
    else:
        land_shape = (N_DEV,) + tuple(size if a == axis else s for a, s in enumerate(x.shape))

    def body(x_ref, land_ref, send_sems, recv_sems, local_sem, x_thru, land_thru, token):
        del x_thru, land_thru
        me = _mesh_pos()
        _local_copy(kind, x_ref, land_ref, axis, size, me, local_sem).start()
        for q in range(len(rels)):
            _exchange_copy(kind, x_ref, land_ref, axis, size, send_sems, recv_sems, me, rels, q, False).start()
        token[...] = jnp.zeros_like(token)

    sems = pltpu.SemaphoreType.DMA((len(rels),))
    send_sems, recv_sems, local_sem, x_thru, land_thru, token = pl.pallas_call(
        body, name=name,
        out_shape=(sems, sems, pltpu.SemaphoreType.DMA(()), pltpu.HBM(x.shape, x.dtype), pltpu.HBM(land_shape, x.dtype),
                   jax.ShapeDtypeStruct((8, LANES), F32)),
        in_specs=(_HBM, _HBM), out_specs=(_SEM, _SEM, _SEM, _HBM, _HBM, pl.BlockSpec(memory_space=pltpu.VMEM)),
        input_output_aliases={0: 3, 1: 4}, compiler_params=pltpu.CompilerParams(has_side_effects=_EFFECT),
    )(pltpu.with_memory_space_constraint(x, pltpu.HBM),
      pltpu.with_memory_space_constraint(lax.empty(land_shape, x.dtype), pltpu.HBM))
    return (kind, axis, size, rels, send_sems, recv_sems, local_sem, x_thru, land_thru), token


def _exchange_wait(handle, after, name):
    kind, axis, size, rels, send_sems, recv_sems, local_sem, x_thru, land_thru = handle

    def body(x_ref, land_ref, send_sems, recv_sems, local_sem, after_ref, x_dead, got_ref):
        del after_ref, x_dead, got_ref
        me = _mesh_pos()
        _local_copy(kind, x_ref, land_ref, axis, size, me, local_sem).wait()
        for q in range(len(rels)):
            _exchange_copy(kind, x_ref, land_ref, axis, size, send_sems, recv_sems, me, rels, q, False).wait_send()
        for q in range(len(rels)):
            _exchange_copy(kind, x_ref, land_ref, axis, size, send_sems, recv_sems, me, rels, q, True).wait_recv()

    return pl.pallas_call(
        body, name=name, out_shape=(pltpu.HBM(x_thru.shape, x_thru.dtype), pltpu.HBM(land_thru.shape, land_thru.dtype)),
        in_specs=(_HBM, _HBM, _SEM, _SEM, _SEM, HBM), out_specs=(_HBM, _HBM), input_output_aliases={0: 0, 1: 1},
        compiler_params=pltpu.CompilerParams(has_side_effects=_EFFECT),
    )(x_thru, land_thru, send_sems, recv_sems, local_sem, after)[1]


def _forward_copy(land_ref, axis, size, send_sems, recv_sems, me, q, arriving):
    sibling = _peer(me, SIBLING)
    owner = _peer(sibling if arriving else me, SAME_CORE_PEERS[q])
    block = _slice_of(land_ref, axis, _index(owner), size)
    return pltpu.make_async_remote_copy(src_ref=block, dst_ref=block, send_sem=send_sems.at[q], recv_sem=recv_sems.at[q],
                                        device_id=sibling, device_id_type=MESH_ID)


def _forward_start(land, axis, name):
    size = land.shape[axis] // N_DEV

    def body(land_ref, send_sems, recv_sems, land_thru, token):
        del land_thru
        me = _mesh_pos()
        for q in range(len(SAME_CORE_PEERS)):
            _forward_copy(land_ref, axis, size, send_sems, recv_sems, me, q, False).start()
        token[...] = jnp.zeros_like(token)

    sems = pltpu.SemaphoreType.DMA((len(SAME_CORE_PEERS),))
    send_sems, recv_sems, land_thru, token = pl.pallas_call(
        body, name=name, out_shape=(sems, sems, pltpu.HBM(land.shape, land.dtype), jax.ShapeDtypeStruct((8, LANES), F32)),
        in_specs=(_HBM,), out_specs=(_SEM, _SEM, _HBM, pl.BlockSpec(memory_space=pltpu.VMEM)),
        input_output_aliases={0: 2}, compiler_params=pltpu.CompilerParams(has_side_effects=_EFFECT),
    )(land)
    return (axis, size, send_sems, recv_sems, land_thru), token


def _forward_wait(handle, after, name):
    axis, size, send_sems, recv_sems, land_thru = handle

    def body(land_ref, send_sems, recv_sems, after_ref, got_ref):
        del after_ref, got_ref
        me = _mesh_pos()
        for q in range(len(SAME_CORE_PEERS)):
            _forward_copy(land_ref, axis, size, send_sems, recv_sems, me, q, False).wait_send()
        for q in range(len(SAME_CORE_PEERS)):
            _forward_copy(land_ref, axis, size, send_sems, recv_sems, me, q, True).wait_recv()

    return pl.pallas_call(
        body, name=name, out_shape=pltpu.HBM(land_thru.shape, land_thru.dtype),
        in_specs=(_HBM, _SEM, _SEM, HBM), out_specs=_HBM, input_output_aliases={0: 0},
        compiler_params=pltpu.CompilerParams(has_side_effects=_EFFECT),
    )(land_thru, send_sems, recv_sems, after)


WHOLE = 1 << 30
MATMUL_TILES = {
    "proj": (1024, 1024, WHOLE), "proj_ctx": (256, WHOLE, WHOLE), "out_proj": (1024, 1024, WHOLE),
    "d_cat": (1024, 1024, WHOLE), "d_w_out": (1024, 1024, 2048), "d_w_in": (1024, 1280, 2048),
    "d_w_in_ctx": (1024, WHOLE, WHOLE), "d_xm": (1024, 512, WHOLE), "d_cm": (256, 1024, WHOLE),
}


def _matmul(a, b, *, mode, name, out_dtype=F32, b_n0=0, n=None, b_k0=0, k=None, acc_in=None, acc_n0=0, dep=None):
    bm, bn, bk = MATMUL_TILES[name]
    if mode == "tn":
        kk, m = a.shape
    else:
        m, kk = a.shape
    if mode == "nn":
        n = b.shape[1] if n is None else n
    elif mode == "nt":
        n = b.shape[0]
        kk = kk if k is None else k
    else:
        n = b.shape[1]
    bm, bn, bk = _tile(m, bm), _tile(n, bn), _tile(kk, bk)
    nk = kk // bk
    assert b_n0 % bn == 0 and b_k0 % bk == 0 and acc_n0 % bn == 0
    dims = {"nn": (((1,), (0,)), ((), ())), "nt": _NT, "tn": _TN}[mode]

    n_in = 2 + (acc_in is not None) + (dep is not None)

    def body(*refs):
        a_ref, b_ref = refs[:2]
        init = refs[2] if acc_in is not None else None
        o_ref = refs[n_in]
        acc_ref = refs[-1] if nk > 1 else None
        p = _mxu_dot(a_ref[...], b_ref[...], dims)
        if nk == 1:
            o_ref[...] = (p if init is None else p + init[...]).astype(out_dtype)
            return
        ki = pl.program_id(2)

        @pl.when(ki == 0)
        def _():
            acc_ref[...] = p if init is None else p + init[...]

        @pl.when(ki > 0)
        def _():
            acc_ref[...] += p

        @pl.when(ki == nk - 1)
        def _():
            o_ref[...] = acc_ref[...].astype(out_dtype)

    a_spec = pl.BlockSpec((bk, bm), lambda j, i, q: (q, i)) if mode == "tn" else pl.BlockSpec((bm, bk), lambda j, i, q: (i, q))
    if mode == "nt":
        b_spec = pl.BlockSpec((bn, bk), lambda j, i, q: (j, q + b_k0 // bk))
    else:
        b_spec = pl.BlockSpec((bk, bn), lambda j, i, q: (q, j + b_n0 // bn))
    in_specs, args, aliases = [a_spec, b_spec], [a, b], {}
    out_map = lambda j, i, q: (i, j + acc_n0 // bn)
    if acc_in is not None:
        in_specs.append(pl.BlockSpec((bm, bn), out_map))
        args.append(acc_in)
        aliases = {2: 0}
        out_shape = jax.ShapeDtypeStruct(acc_in.shape, out_dtype)
    else:
        out_shape = jax.ShapeDtypeStruct((m, n), out_dtype)
    if dep is not None:
        in_specs.append(HBM)
        args.append(dep)
    return pl.pallas_call(
        body, name=name, out_shape=out_shape, grid=(n // bn, m // bm, nk),
        in_specs=in_specs, out_specs=pl.BlockSpec((bm, bn), out_map),
        scratch_shapes=[pltpu.VMEM((bm, bn), F32)] if nk > 1 else [],
        input_output_aliases=aliases,
        compiler_params=_cparams("parallel", "parallel", "arbitrary"),
    )(*args)


def _silu_rows(c, c_ctx):
    d = c.shape[-1]

    def body(c_ref, cc_ref, o_ref):
        o_ref[...] = jnp.zeros_like(o_ref)
        o_ref[0:1, :] = jax.nn.silu(c_ref[...])
        o_ref[1:2, :] = jax.nn.silu(cc_ref[...])

    return pl.pallas_call(body, name="silu_rows", out_shape=jax.ShapeDtypeStruct((8, d), F32))(
        c.reshape(1, d), c_ctx.reshape(1, d))


def _small_dot(a, b, mode, name):
    dims = {"nn": (((1,), (0,)), ((), ())), "nt": _NT, "tn": _TN}[mode]
    m = a.shape[1] if mode == "tn" else a.shape[0]
    n = b.shape[0] if mode == "nt" else b.shape[1]

    def body(a_ref, b_ref, o_ref):
        o_ref[...] = lax.dot_general(a_ref[...], b_ref[...], dims, preferred_element_type=F32,
                                     precision=lax.Precision.HIGHEST)

    return pl.pallas_call(body, name=name, out_shape=jax.ShapeDtypeStruct((m, n), F32),
                          compiler_params=_cparams())(a, b)


def _ln_stats(x):
    mu = jnp.mean(x, axis=-1, keepdims=True)
    xc = x - mu
    var = jnp.mean(xc * xc, axis=-1, keepdims=True)
    rstd = lax.rsqrt(var + LN_EPS)
    return xc * rstd, rstd


def _ln_mod(x, shift, scale, name):
    l, d = x.shape
    tl = _tile(l, 512)

    def body(x_ref, sh_ref, sc_ref, o_ref):
        xhat, _ = _ln_stats(x_ref[...])
        o_ref[...] = (xhat * (1.0 + sc_ref[...]) + sh_ref[...]).astype(o_ref.dtype)

    row = pl.BlockSpec((tl, d), lambda i: (i, 0))
    vec = pl.BlockSpec((1, d), lambda i: (0, 0))
    return pl.pallas_call(body, name=name, out_shape=jax.ShapeDtypeStruct((l, d), MXU_DTYPE), grid=(l // tl,),
                          in_specs=[row, vec, vec], out_specs=row, compiler_params=_cparams("parallel"))(x, shift, scale)


def _ln_mod_bwd(x, dxm, scale, res, name):
    l, d = x.shape
    tl = _tile(l, 512)
    with_res = res is not None

    def body(*refs):
        if with_res:
            x_ref, g_ref, sc_ref, r_ref, dx_ref, dsh_ref, dsc_ref = refs
        else:
            x_ref, g_ref, sc_ref, dx_ref, dsh_ref, dsc_ref = refs
        i = pl.program_id(0)
        xhat, rstd = _ln_stats(x_ref[...])
        g = g_ref[...].astype(F32)
        dxh = g * (1.0 + sc_ref[...])
        dx = rstd * (dxh - jnp.mean(dxh, axis=-1, keepdims=True) - xhat * jnp.mean(dxh * xhat, axis=-1, keepdims=True))
        dx_ref[...] = dx + r_ref[...].astype(F32) if with_res else dx

        @pl.when(i == 0)
        def _():
            dsh_ref[...] = jnp.zeros_like(dsh_ref)
            dsc_ref[...] = jnp.zeros_like(dsc_ref)

        dsh_ref[...] += jnp.sum(g, axis=0, keepdims=True)
        dsc_ref[...] += jnp.sum(g * xhat, axis=0, keepdims=True)

    row = pl.BlockSpec((tl, d), lambda i: (i, 0))
    vec = pl.BlockSpec((1, d), lambda i: (0, 0))
    args = [x, dxm, scale] + ([res] if with_res else [])
    return pl.pallas_call(
        body, name=name,
        out_shape=(jax.ShapeDtypeStruct((l, d), F32), jax.ShapeDtypeStruct((1, d), F32), jax.ShapeDtypeStruct((1, d), F32)),
        grid=(l // tl,), in_specs=[row, row, vec] + ([row] if with_res else []), out_specs=(row, vec, vec),
        compiler_params=_cparams("arbitrary"))(*args)


def _post_ln_loss(x, out, gate, ln_g, ln_b, target):
    l, d = x.shape
    tl = _tile(l, 512)

    def body(x_ref, o_ref, gate_ref, g_ref, b_ref, t_ref, loss_ref, dout_ref, dxr_ref, dgate_ref, dg_ref, db_ref):
        i = pl.program_id(0)
        out_t = o_ref[...]
        gate_v = gate_ref[...]
        rhat, rstd = _ln_stats(ALPHA * x_ref[...] + gate_v * out_t)
        ln_gv = g_ref[...]
        diff = rhat * ln_gv + b_ref[...] - t_ref[...]
        dy = diff * (1.0 / d)
        drh = dy * ln_gv
        dr = rstd * (drh - jnp.mean(drh, axis=-1, keepdims=True) - rhat * jnp.mean(drh * rhat, axis=-1, keepdims=True))
        dout_ref[...] = (gate_v * dr).astype(dout_ref.dtype)
        dxr_ref[...] = (ALPHA * dr).astype(dxr_ref.dtype)

        @pl.when(i == 0)
        def _():
            for r in (loss_ref, dgate_ref, dg_ref, db_ref):
                r[...] = jnp.zeros_like(r)

        loss_ref[...] += jnp.sum(diff * diff, axis=0, keepdims=True)
        dgate_ref[...] += jnp.sum(dr * out_t, axis=0, keepdims=True)
        dg_ref[...] += jnp.sum(dy * rhat, axis=0, keepdims=True)
        db_ref[...] += jnp.sum(dy, axis=0, keepdims=True)

    row = pl.BlockSpec((tl, d), lambda i: (i, 0))
    vec = pl.BlockSpec((1, d), lambda i: (0, 0))
    v = jax.ShapeDtypeStruct((1, d), F32)
    return pl.pallas_call(
        body, name="post_ln_loss",
        out_shape=(v, jax.ShapeDtypeStruct((l, d), MXU_DTYPE), jax.ShapeDtypeStruct((l, d), MXU_DTYPE), v, v, v),
        grid=(l // tl,), in_specs=[row, row, vec, vec, vec, row], out_specs=(vec, row, row, vec, vec, vec),
        compiler_params=_cparams("arbitrary"))(x, out, gate, ln_g, ln_b, target)


def _ga_forward_tile(p, g, b, ws_ref, bsf, w, nc, nh):
    u_raw, v_raw, za = p[:, :w], p[:, w:2 * w], p[:, 2 * w:3 * w]
    gu = _gelu(u_raw)
    vhat, rstd = _ln_stats(_gelu(v_raw))
    vn = vhat * g + b
    rows = []
    for ci in range(nc):
        r0 = ci * CHUNK
        heads = [_mxu_dot(ws_ref[h], vn[r0:r0 + CHUNK, h * HEAD_DIM_A:(h + 1) * HEAD_DIM_A]) for h in range(nh)]
        rows.append(jnp.concatenate(heads, axis=1) + bsf)
    mixed = jnp.concatenate(rows, axis=0) if nc > 1 else rows[0]
    return u_raw, v_raw, za, gu, vhat, rstd, vn, mixed


def _ga_fwd(proj, g, b, ws, bsf, w):
    l = proj.shape[0]
    nh = w // HEAD_DIM_A
    nc = _tile(l // CHUNK, 2)
    tl = nc * CHUNK

    def body(p_ref, g_ref, b_ref, ws_ref, bsf_ref, o_ref):
        _, _, za, gu, _, _, _, mixed = _ga_forward_tile(p_ref[...], g_ref[...], b_ref[...], ws_ref, bsf_ref[...], w, nc, nh)
        o_ref[...] = (gu * mixed * jax.nn.silu(za)).astype(o_ref.dtype)

    vec = pl.BlockSpec((1, w), lambda i: (0, 0))
    return pl.pallas_call(
        body, name="ga_fwd", out_shape=jax.ShapeDtypeStruct((l, 2 * w), MXU_DTYPE), grid=(l // tl,),
        in_specs=[pl.BlockSpec((tl, 3 * w), lambda i: (i, 0)), vec, vec,
                  pl.BlockSpec((nh, CHUNK, CHUNK), lambda i: (0, 0, 0)), pl.BlockSpec((CHUNK, w), lambda i: (0, 0))],
        out_specs=pl.BlockSpec((tl, w), lambda i: (i, 0)), compiler_params=_cparams("parallel"))(proj, g, b, ws, bsf)


def _ga_bwd(proj, dcat, dproj, g, b, ws, bsf, dys, du0, du1, d_skip, w):
    l = proj.shape[0]
    nh = w // HEAD_DIM_A
    nc = _tile(l // CHUNK, 2)
    tl = nc * CHUNK

    def body(p_ref, dy_ref, dp_in, g_ref, b_ref, ws_ref, bsf_ref, dys_ref, du0_ref, du1_ref, d_ref,
             dp_ref, dg_ref, db_ref, dws_ref, dbsf_ref):
        del dp_in
        i = pl.program_id(0)
        dp_ref[:, 3 * w:] = (dys_ref[...] * d_ref[...] + du0_ref[...] + du1_ref[...]).astype(dp_ref.dtype)
        gv = g_ref[...]
        u_raw, v_raw, za, gu, vhat, rstd, vn, mixed = _ga_forward_tile(
            p_ref[...], gv, b_ref[...], ws_ref, bsf_ref[...], w, nc, nh)
        dya = dy_ref[...].astype(F32)
        sz = jax.nn.silu(za)
        dmixed = dya * gu * sz
        dza = dya * gu * mixed * _silu_grad(za)
        dgu = dya * mixed * sz

        @pl.when(i == 0)
        def _():
            for r in (dg_ref, db_ref, dws_ref, dbsf_ref):
                r[...] = jnp.zeros_like(r)

        rows = []
        for ci in range(nc):
            r0 = ci * CHUNK
            heads = []
            for h in range(nh):
                cols = slice(h * HEAD_DIM_A, (h + 1) * HEAD_DIM_A)
                dm = dmixed[r0:r0 + CHUNK, cols]
                heads.append(_mxu_dot(ws_ref[h], dm, _TN))
                dws_ref[h] += _mxu_dot(dm, vn[r0:r0 + CHUNK, cols], _NT)
            rows.append(jnp.concatenate(heads, axis=1))
            dbsf_ref[...] += dmixed[r0:r0 + CHUNK, :]
        dvn = jnp.concatenate(rows, axis=0) if nc > 1 else rows[0]
        dg_ref[...] += jnp.sum(dvn * vhat, axis=0, keepdims=True)
        db_ref[...] += jnp.sum(dvn, axis=0, keepdims=True)
        dvh = dvn * gv
        dgv = rstd * (dvh - jnp.mean(dvh, axis=-1, keepdims=True) - vhat * jnp.mean(dvh * vhat, axis=-1, keepdims=True))
        dp_ref[:, :w] = (dgu * _gelu_grad(u_raw)).astype(dp_ref.dtype)
        dp_ref[:, w:2 * w] = (dgv * _gelu_grad(v_raw)).astype(dp_ref.dtype)
        dp_ref[:, 2 * w:3 * w] = dza.astype(dp_ref.dtype)

    vec = pl.BlockSpec((1, w), lambda i: (0, 0))
    row = pl.BlockSpec((tl, w), lambda i: (i, 0))
    ws_spec = pl.BlockSpec((nh, CHUNK, CHUNK), lambda i: (0, 0, 0))
    bs_spec = pl.BlockSpec((CHUNK, w), lambda i: (0, 0))
    v = jax.ShapeDtypeStruct((1, w), F32)
    return pl.pallas_call(
        body, name="ga_bwd",
        out_shape=(jax.ShapeDtypeStruct(dproj.shape, dproj.dtype), v, v, jax.ShapeDtypeStruct((nh, CHUNK, CHUNK), F32),
                   jax.ShapeDtypeStruct((CHUNK, w), F32)),
        grid=(l // tl,),
        in_specs=[pl.BlockSpec((tl, 3 * w), lambda i: (i, 0)), row, HBM, vec, vec, ws_spec, bs_spec, row, row, row, vec],
        out_specs=(pl.BlockSpec((tl, 4 * w), lambda i: (i, 0)), vec, vec, ws_spec, bs_spec),
        input_output_aliases={2: 0}, compiler_params=_cparams("arbitrary"))(
            proj, dcat, dproj, g, b, ws, bsf, dys, du0, du1, d_skip)


def _lane_group_sum(x, expand, name):
    return _small_dot(x, expand, "nn", name)


def _disc_math(lr, li, ls, br, bi):
    step = jnp.exp(ls)
    dr, di = lr * step, li * step
    mag = jnp.exp(dr)
    ab_re, ab_im = mag * jnp.cos(di), mag * jnp.sin(di)
    den = lr * lr + li * li
    nr, ni = ab_re - 1.0, ab_im
    f_re = (nr * lr + ni * li) / den
    f_im = (ni * lr - nr * li) / den
    bb_re = f_re * br - f_im * bi
    bb_im = f_re * bi + f_im * br
    return ab_re, ab_im, bb_re, bb_im


def _disc_fwd(lr, li, ls, br, bi):
    def body(lr_ref, li_ref, ls_ref, br_ref, bi_ref, o1, o2, o3, o4):
        res = _disc_math(lr_ref[...], li_ref[...], ls_ref[...], br_ref[...], bi_ref[...])
        for o, r in zip((o1, o2, o3, o4), res):
            o[...] = r

    s = lambda a: jax.ShapeDtypeStruct(a.shape, F32)
    return pl.pallas_call(body, name="s5_disc", out_shape=(s(lr), s(lr), s(br), s(br)), compiler_params=_cparams())(
        lr, li, ls, br, bi)


def _disc_bwd(lr, li, ls, br, bi, d_ar, d_ai, d_br, d_bi):
    def body(lr_ref, li_ref, ls_ref, br_ref, bi_ref, c1, c2, c3, c4, o1, o2, o3, o4, o5):
        _, vjp = jax.vjp(_disc_math, lr_ref[...], li_ref[...], ls_ref[...], br_ref[...], bi_ref[...])
        res = vjp((c1[...], c2[...], c3[...], c4[...]))
        for o, r in zip((o1, o2, o3, o4, o5), res):
            o[...] = r

    s = lambda a: jax.ShapeDtypeStruct(a.shape, F32)
    return pl.pallas_call(body, name="s5_disc_bwd", out_shape=(s(lr), s(lr), s(ls), s(br), s(br)),
                          compiler_params=_cparams())(lr, li, ls, br, bi, d_ar, d_ai, d_br, d_bi)


def _dir_spec(a, dr, **kw):
    return pl.BlockSpec((None,) + a.shape[1:], lambda i: (dr,) + (0,) * (a.ndim - 1), **kw)


def _s5_fwd(u_arr, u_col, w, h0, a_sm, wbr, wbi, cre, ncim, dr, name):
    rev = dr == 1
    l = u_arr.shape[0]
    nb = w // LANES
    spb = wbr.shape[-1]
    nsr = a_sm.shape[2]
    assert 2 * spb == 8 * LANES and nb % 2 == 0
    npair = nb // 2
    t = _tile(l, 256)
    n = l // t
    tile = (lambda i: n - 1 - i) if rev else (lambda i: i)

    def body(u_ref, h0_ref, a_ref, wbr_ref, wbi_ref, cre_ref, ncim_ref, y_ref, hr_ref, hi_ref, tr_ref, ti_ref, hfin_ref,
             carry_ref):
        i = pl.program_id(0)

        @pl.when(i == 0)
        def _():
            carry_ref[...] = h0_ref[...]

        for j in range(npair):
            for h_ref, w_ref in ((hr_ref, wbr_ref), (hi_ref, wbi_ref)):
                blk = [_mxu_dot(u_ref[:, k * LANES:(k + 1) * LANES], w_ref[k]) for k in (2 * j, 2 * j + 1)]
                h_ref[j] = jnp.concatenate(blk, axis=1).reshape(t, 8, LANES)
        slab = lambda ref, part, j: ref[part, 8 * j:8 * j + 8, :]
        ar = [slab(a_ref, 0, j) for j in range(npair)]
        ai = [slab(a_ref, 1, j) for j in range(npair)]

        def steps(blk, c):
            hr, hi = list(c[:npair]), list(c[npair:])
            for q in range(SCAN_UNROLL):
                s = blk * SCAN_UNROLL + q
                row = t - 1 - s if rev else s
                for j in range(npair):
                    hr[j], hi[j] = (ar[j] * hr[j] - ai[j] * hi[j] + hr_ref[j, row],
                                    ar[j] * hi[j] + ai[j] * hr[j] + hi_ref[j, row])
                    hr_ref[j, row] = hr[j]
                    hi_ref[j, row] = hi[j]
            return tuple(hr + hi)

        init = tuple(slab(carry_ref, part, j) for part in range(2) for j in range(npair))
        c = lax.fori_loop(0, t // SCAN_UNROLL, steps, init)
        for part in range(2):
            for j in range(npair):
                carry_ref[part, 8 * j:8 * j + 8, :] = c[part * npair + j]
                hfin_ref[part, 8 * j:8 * j + 8, :] = c[part * npair + j]
        for j in range(npair):
            cols8 = slice(j * 8 * LANES, (j + 1) * 8 * LANES)
            tr_ref[:, cols8] = hr_ref[j].reshape(t, 8 * LANES).astype(tr_ref.dtype)
            ti_ref[:, cols8] = hi_ref[j].reshape(t, 8 * LANES).astype(ti_ref.dtype)
        for k in range(nb):
            cols = slice(k * spb, (k + 1) * spb)
            y_ref[:, k * LANES:(k + 1) * LANES] = (_mxu_dot(tr_ref[:, cols], cre_ref[k]) + _mxu_dot(ti_ref[:, cols], ncim_ref[k]))

    full = lambda a: pl.BlockSpec(a.shape, lambda i: (0,) * a.ndim)
    hspec = pl.BlockSpec((npair, t, 8, LANES), lambda i: (0, tile(i), 0, 0))
    tspec = pl.BlockSpec((t, nsr * LANES), lambda i: (tile(i), 0))
    hsh = jax.ShapeDtypeStruct((npair, l, 8, LANES), F32)
    tsh = jax.ShapeDtypeStruct((l, nsr * LANES), MXU_DTYPE)
    return pl.pallas_call(
        body, name=name,
        out_shape=(jax.ShapeDtypeStruct((l, w), F32), hsh, hsh, tsh, tsh, jax.ShapeDtypeStruct((2, nsr, LANES), F32)),
        grid=(n,),
        in_specs=[pl.BlockSpec((t, w), lambda i: (tile(i), u_col)), full(h0)] + [_dir_spec(a, dr) for a in (a_sm, wbr, wbi, cre, ncim)],
        out_specs=(pl.BlockSpec((t, w), lambda i: (tile(i), 0)), hspec, hspec, tspec, tspec,
                   pl.BlockSpec((2, nsr, LANES), lambda i: (0, 0, 0))),
        scratch_shapes=[pltpu.VMEM((2, nsr, LANES), F32)],
        compiler_params=_cparams("arbitrary"))(u_arr, h0, a_sm, wbr, wbi, cre, ncim)


def _s5_bwd(dys, u_arr, u_col, w, hr, hi, tr, ti, hbound, g_in, a_sm, wbr_t, wbi_t, cre_t, ncim_t, dr, name):
    rev = dr == 1
    l = u_arr.shape[0]
    nb = w // LANES
    spb = wbr_t.shape[-2]
    nsr = a_sm.shape[2]
    npair = nb // 2
    t = _tile(l, 256 if l >= 1024 else 128)
    n = l // t
    with_dy = dys is not None
    tile = (lambda i: i) if rev else (lambda i: n - 1 - i)

    def body(*refs):
        if with_dy:
            (dy_ref, u_ref, hr_ref, hi_ref, pr_ref, pi_ref, tr_ref, ti_ref, hb_ref, gin_ref, a_ref, wbrt_ref, wbit_ref,
             cret_ref, ncimt_ref, du_ref, dwbr_ref, dwbi_ref, dcre_ref, dncim_ref, da_ref, gout_ref,
             gr_ref, gi_ref, gtr_ref, gti_ref, carry_ref) = refs
        else:
            (u_ref, hr_ref, hi_ref, pr_ref, pi_ref, hb_ref, gin_ref, a_ref, wbrt_ref, wbit_ref,
             du_ref, dwbr_ref, dwbi_ref, da_ref, gout_ref, gr_ref, gi_ref, gtr_ref, gti_ref, carry_ref) = refs
        i = pl.program_id(0)

        @pl.when(i == 0)
        def _():
            carry_ref[...] = gin_ref[...]
            accs = (dwbr_ref, dwbi_ref, da_ref) + ((dcre_ref, dncim_ref) if with_dy else ())
            for r in accs:
                r[...] = jnp.zeros_like(r)

        if with_dy:
            for j in range(npair):
                for g_ref, c_ref in ((gr_ref, cret_ref), (gi_ref, ncimt_ref)):
                    blk = [_mxu_dot(dy_ref[:, k * LANES:(k + 1) * LANES], c_ref[k]) for k in (2 * j, 2 * j + 1)]
                    g_ref[j] = jnp.concatenate(blk, axis=1).reshape(t, 8, LANES)
        else:
            gr_ref[...] = jnp.zeros_like(gr_ref)
            gi_ref[...] = jnp.zeros_like(gi_ref)
        slab = lambda ref, part, j: ref[part, 8 * j:8 * j + 8, :]
        last = t - 1 if rev else 0
        first = i == n - 1

        ar = [slab(a_ref, 0, j) for j in range(npair)]
        ai = [slab(a_ref, 1, j) for j in range(npair)]

        def steps(blk, c):
            gr, gi, dr, di = (list(c[q * npair:(q + 1) * npair]) for q in range(4))
            for q in range(SCAN_UNROLL):
                s = blk * SCAN_UNROLL + q
                row = s if rev else t - 1 - s
                prow = jnp.minimum(row + 1, t - 1) if rev else jnp.maximum(row - 1, 0)
                for j in range(npair):
                    pr, pi = hr_ref[j, prow], hi_ref[j, prow]
                    gr[j], gi[j] = (gr_ref[j, row] + ar[j] * gr[j] + ai[j] * gi[j],
                                    gi_ref[j, row] + ar[j] * gi[j] - ai[j] * gr[j])
                    gr_ref[j, row] = gr[j]
                    gi_ref[j, row] = gi[j]
                    dr[j], di[j] = dr[j] + gr[j] * pr + gi[j] * pi, di[j] + gi[j] * pr - gr[j] * pi
            return tuple(gr + gi + dr + di)

        init = tuple(slab(ref, part, j) for ref in (carry_ref, da_ref) for part in range(2) for j in range(npair))
        c = lax.fori_loop(0, t // SCAN_UNROLL, steps, init)
        gr, gi, dr, di = (c[q * npair:(q + 1) * npair] for q in range(4))
        for j in range(npair):
            pr = jnp.where(first, slab(hb_ref, 0, j), pr_ref[j, 0]) - hr_ref[j, last]
            pi = jnp.where(first, slab(hb_ref, 1, j), pi_ref[j, 0]) - hi_ref[j, last]
            rows = slice(8 * j, 8 * j + 8)
            da_ref[0, rows, :] = dr[j] + gr[j] * pr + gi[j] * pi
            da_ref[1, rows, :] = di[j] + gi[j] * pr - gr[j] * pi
            for part, val in enumerate((gr[j], gi[j])):
                carry_ref[part, rows, :] = val
                gout_ref[part, rows, :] = val

        for j in range(npair):
            cols8 = slice(j * 8 * LANES, (j + 1) * 8 * LANES)
            gtr_ref[:, cols8] = gr_ref[j].reshape(t, 8 * LANES).astype(gtr_ref.dtype)
            gti_ref[:, cols8] = gi_ref[j].reshape(t, 8 * LANES).astype(gti_ref.dtype)
        for k in range(nb):
            cols = slice(k * spb, (k + 1) * spb)
            lanes = slice(k * LANES, (k + 1) * LANES)
            du_ref[:, lanes] = _mxu_dot(gtr_ref[:, cols], wbrt_ref[k]) + _mxu_dot(gti_ref[:, cols], wbit_ref[k])
            dwbr_ref[k] += _mxu_dot(u_ref[:, lanes], gtr_ref[:, cols], _TN)
            dwbi_ref[k] += _mxu_dot(u_ref[:, lanes], gti_ref[:, cols], _TN)
            if with_dy:
                dcre_ref[k] += _mxu_dot(tr_ref[:, cols], dy_ref[:, lanes], _TN)
                dncim_ref[k] += _mxu_dot(ti_ref[:, cols], dy_ref[:, lanes], _TN)

    once = dict(pipeline_mode=pl.Buffered(1))
    full = lambda a: pl.BlockSpec(a.shape, lambda i: (0,) * a.ndim, **once)
    row = lambda cb: pl.BlockSpec((t, w), lambda i: (tile(i), cb))
    hspec = pl.BlockSpec((npair, t, 8, LANES), lambda i: (0, tile(i), 0, 0))
    if rev:
        pspec = pl.BlockSpec((npair, 1, 8, LANES), lambda i: (0, jnp.minimum((tile(i) + 1) * t, l - 1), 0, 0))
    else:
        pspec = pl.BlockSpec((npair, 1, 8, LANES), lambda i: (0, jnp.maximum(tile(i) * t - 1, 0), 0, 0))
    sm = jax.ShapeDtypeStruct((2, nsr, LANES), F32)
    smspec = pl.BlockSpec((2, nsr, LANES), lambda i: (0, 0, 0))
    wsh = jax.ShapeDtypeStruct((nb, LANES, spb), F32)
    csh = jax.ShapeDtypeStruct((nb, spb, LANES), F32)
    tspec = pl.BlockSpec((t, nsr * LANES), lambda i: (tile(i), 0))
    in_specs = (([row(0)] if with_dy else []) + [row(u_col), hspec, hspec, pspec, pspec] + ([tspec, tspec] if with_dy else [])
                + [full(hbound), full(g_in)]
                + [_dir_spec(a, dr, **once) for a in (a_sm, wbr_t, wbi_t) + ((cre_t, ncim_t) if with_dy else ())])
    args = (([dys] if with_dy else []) + [u_arr, hr, hi, hr, hi] + ([tr, ti] if with_dy else [])
            + [hbound, g_in, a_sm, wbr_t, wbi_t] + ([cre_t, ncim_t] if with_dy else []))
    out_shape = (jax.ShapeDtypeStruct((l, w), F32), wsh, wsh) + ((csh, csh) if with_dy else ()) + (sm, sm)
    out_specs = (row(0), full(wsh), full(wsh)) + ((full(csh), full(csh)) if with_dy else ()) + (smspec, smspec)
    return pl.pallas_call(
        body, name=name, out_shape=out_shape, grid=(n,), in_specs=in_specs, out_specs=out_specs,
        scratch_shapes=[pltpu.VMEM((npair, t, 8, LANES), F32)] * 2 + [pltpu.VMEM((t, nsr * LANES), MXU_DTYPE)] * 2
        + [pltpu.VMEM((2, nsr, LANES), F32)],
        compiler_params=_cparams("arbitrary"))(*args)


def _glu_fwd(y0, y1, proj, cat, d_skip, w_glu, b_glu, w):
    l = y0.shape[0]
    tl = _tile(l, 512)

    def body(y0_ref, y1_ref, u_ref, z_ref, cat_in, d_ref, wg_ref, bg_ref, ys_ref, cat_ref):
        del cat_in
        ys = y0_ref[...] + y1_ref[...] + d_ref[...] * u_ref[...]
        ys_ref[...] = ys
        gy = _gelu(ys)
        s = _mxu_dot(gy, wg_ref[...]) + bg_ref[...]
        cat_ref[...] = (gy * jax.nn.sigmoid(s) * jax.nn.silu(z_ref[...])).astype(cat_ref.dtype)

    row = pl.BlockSpec((tl, w), lambda i: (i, 0))
    vec = pl.BlockSpec((1, w), lambda i: (0, 0))
    return pl.pallas_call(
        body, name="glu_fwd", out_shape=(jax.ShapeDtypeStruct((l, w), F32), jax.ShapeDtypeStruct(cat.shape, cat.dtype)),
        grid=(l // tl,),
        in_specs=[row, row, pl.BlockSpec((tl, w), lambda i: (i, 3)), pl.BlockSpec((tl, w), lambda i: (i, 4)), HBM,
                  vec, pl.BlockSpec((w, w), lambda i: (0, 0), pipeline_mode=pl.Buffered(1)), vec],
        out_specs=(row, pl.BlockSpec((tl, w), lambda i: (i, 1))), input_output_aliases={4: 1},
        compiler_params=_cparams("parallel"))(y0, y1, proj, proj, cat, d_skip, w_glu, b_glu)


def _glu_bwd(dcat, ys, proj, w_glu, b_glu, w, dep):
    l = ys.shape[0]
    tl = _tile(l, 512)

    def body(dy_ref, ys_ref, u_ref, z_ref, wg_ref, bg_ref, dep_ref, dys_ref, dp_ref, dbg_ref, dd_ref, dwg_ref):
        del dep_ref
        i = pl.program_id(0)
        ys_t = ys_ref[...]
        z = z_ref[...]
        dyb = dy_ref[...].astype(F32)
        gy = _gelu(ys_t)
        sg = jax.nn.sigmoid(_mxu_dot(gy, wg_ref[...]) + bg_ref[...])
        dp_ref[...] = (dyb * gy * sg * _silu_grad(z)).astype(dp_ref.dtype)
        dglu = dyb * jax.nn.silu(z)
        ds = dglu * gy * sg * (1.0 - sg)
        dgy = dglu * sg + _mxu_dot(ds, wg_ref[...], _NT)
        dys_t = dgy * _gelu_grad(ys_t)
        dys_ref[...] = dys_t

        @pl.when(i == 0)
        def _():
            for r in (dbg_ref, dd_ref, dwg_ref):
                r[...] = jnp.zeros_like(r)

        dbg_ref[...] += jnp.sum(ds, axis=0, keepdims=True)
        dd_ref[...] += jnp.sum(dys_t * u_ref[...], axis=0, keepdims=True)
        dwg_ref[...] += _mxu_dot(gy, ds, _TN)

    row = pl.BlockSpec((tl, w), lambda i: (i, 0))
    vec = pl.BlockSpec((1, w), lambda i: (0, 0))
    mat = pl.BlockSpec((w, w), lambda i: (0, 0), pipeline_mode=pl.Buffered(1))
    v = jax.ShapeDtypeStruct((1, w), F32)
    return pl.pallas_call(
        body, name="glu_bwd",
        out_shape=(jax.ShapeDtypeStruct((l, w), F32), jax.ShapeDtypeStruct((l, 5 * w), MXU_DTYPE), v, v,
                   jax.ShapeDtypeStruct((w, w), F32)),
        grid=(l // tl,),
        in_specs=[pl.BlockSpec((tl, w), lambda i: (i, 1)), row, pl.BlockSpec((tl, w), lambda i: (i, 3)),
                  pl.BlockSpec((tl, w), lambda i: (i, 4)), mat, vec, HBM],
        out_specs=(row, pl.BlockSpec((tl, w), lambda i: (i, 4)), vec, vec, mat),
        compiler_params=_cparams("arbitrary"))(dcat, ys, proj, proj, w_glu, b_glu, dep)


def _add2(a, b, name):
    l, w = a.shape
    tl = _tile(l, 512)

    def body(a_ref, b_ref, o_ref):
        o_ref[...] = a_ref[...] + b_ref[...]

    row = pl.BlockSpec((tl, w), lambda i: (i, 0))
    return pl.pallas_call(body, name=name, out_shape=jax.ShapeDtypeStruct((l, w), F32), grid=(l // tl,),
                          in_specs=[row, row], out_specs=row, compiler_params=_cparams("parallel"))(a, b)


def _adamw_nd(w, m, v, g, name):
    shape = w.shape
    lead = math.prod(shape[:-2]) if len(shape) > 2 else 1
    b, c = (shape[-2], shape[-1]) if len(shape) >= 2 else (1, shape[-1])
    t3 = (lead, b, c)
    padded_row = -(-b // 8) * 8 * -(-c // LANES) * LANES * 4
    ta = _tile(lead, max(1, (2 << 20) // padded_row))

    def body(w_ref, m_ref, v_ref, g_ref, d_ref, mo_ref, vo_ref):
        d_ref[...], mo_ref[...], vo_ref[...] = _adamw_step(w_ref[...], m_ref[...], v_ref[...], g_ref[...])

    blk = pl.BlockSpec((ta, b, c), lambda i: (i, 0, 0))
    s = jax.ShapeDtypeStruct(t3, F32)
    outs = pl.pallas_call(body, name=name, out_shape=(s, s, s), grid=(lead // ta,), in_specs=[blk] * 4, out_specs=(blk,) * 3,
                          compiler_params=_cparams("parallel"))(*[a.reshape(t3) for a in (w, m, v, g)])
    return tuple(o.reshape(shape) for o in outs)


def _adamw_step(w, m, v, g):
    mn = ADAM_B1 * m + (1.0 - ADAM_B1) * g
    vn = ADAM_B2 * v + (1.0 - ADAM_B2) * (g * g)
    m_hat = mn / (1.0 - ADAM_B1 ** ADAM_STEP)
    v_hat = vn / (1.0 - ADAM_B2 ** ADAM_STEP)
    return -ADAM_LR * (m_hat / (jnp.sqrt(v_hat) + ADAM_EPS) + ADAM_WD * w), mn, vn


def _adamw_many(params, name):
    def as3(a):
        s = a.shape
        lead = math.prod(s[:-2]) if len(s) > 2 else 1
        return a.reshape((lead,) + ((s[-2], s[-1]) if len(s) >= 2 else (1, s[-1])))

    flat = [as3(a) for p in params for a in p]
    n = len(params)

    def body(*refs):
        ins, outs = refs[:4 * n], refs[4 * n:]
        for q in range(n):
            w_ref, m_ref, v_ref, g_ref = ins[4 * q:4 * q + 4]
            for o, val in zip(outs[3 * q:3 * q + 3], _adamw_step(w_ref[...], m_ref[...], v_ref[...], g_ref[...])):
                o[...] = val

    out_shape = [jax.ShapeDtypeStruct(flat[4 * q].shape, F32) for q in range(n) for _ in range(3)]
    outs = pl.pallas_call(body, name=name, out_shape=out_shape, compiler_params=_cparams())(*flat)
    return [tuple(o.reshape(params[q][0].shape) for o in outs[3 * q:3 * q + 3]) for q in range(n)]


def _adamw(w, m, v, gparts, name, dep=None):
    r, c = w.shape
    np_ = gparts.shape[0]
    tr = _tile(r, max(8, (1 << 18) // c), 8)

    def body(w_ref, m_ref, v_ref, g_ref, *rest):
        go_ref, d_ref, mo_ref, vo_ref = rest[-4:]
        g = g_ref[0].astype(F32)
        for p in range(1, np_):
            g = g + g_ref[p].astype(F32)
        go_ref[...] = g
        d_ref[...], mo_ref[...], vo_ref[...] = _adamw_step(w_ref[...], m_ref[...], v_ref[...], g)

    row = pl.BlockSpec((tr, c), lambda i: (i, 0))
    s = jax.ShapeDtypeStruct((r, c), F32)
    extra = [] if dep is None else [dep]
    return pl.pallas_call(body, name=name, out_shape=(s, s, s, s), grid=(r // tr,),
                          in_specs=[row, row, row, pl.BlockSpec((np_, tr, c), lambda i: (0, i, 0))] + [HBM] * len(extra),
                          out_specs=(row, row, row, row), compiler_params=_cparams("parallel"))(w, m, v, gparts, *extra)


def _sum_slots(parts, name):
    np_, r, c = parts.shape

    def body(p_ref, o_ref):
        g = p_ref[0]
        for p in range(1, np_):
            g = g + p_ref[p]
        o_ref[...] = g

    return pl.pallas_call(body, name=name, out_shape=jax.ShapeDtypeStruct((r, c), F32), compiler_params=_cparams())(parts)


def _block_diag(x, gb):
    nd, g, a, b = x.shape
    eye = jnp.eye(gb, dtype=x.dtype)
    y = jnp.einsum("dkgab,gh->dkgahb", x.reshape(nd, g // gb, gb, a, b), eye)
    return y.reshape(nd, g // gb, gb * a, gb * b)


def _block_diag_extract(y, gb, a, b):
    nd, nbk = y.shape[:2]
    eye = jnp.eye(gb, dtype=y.dtype)
    x = jnp.einsum("dkgahb,gh->dkgab", y.reshape(nd, nbk, gb, a, gb, b), eye)
    return x.reshape(nd, nbk * gb, a, b)


def kernel(x, c, ctx, c_ctx, w_ada, b_ada, w_in, sgu_ln_g, sgu_ln_b, w_spatial, b_spatial, s5_lam_re, s5_lam_im, s5_log_step, s5_b_re, s5_b_im, s5_c_re, s5_c_im, s5_d, w_glu, b_glu, w_out, ln_g, ln_b, loss_target, m_c_ctx, m_w_ada, m_b_ada, m_w_in, m_sgu_ln_g, m_sgu_ln_b, m_w_spatial, m_b_spatial, m_s5_lam_re, m_s5_lam_im, m_s5_log_step, m_s5_b_re, m_s5_b_im, m_s5_c_re, m_s5_c_im, m_s5_d, m_w_glu, m_b_glu, m_w_out, m_ln_g, m_ln_b, v_c_ctx, v_w_ada, v_b_ada, v_w_in, v_sgu_ln_g, v_sgu_ln_b, v_w_spatial, v_b_spatial, v_s5_lam_re, v_s5_lam_im, v_s5_log_step, v_s5_b_re, v_s5_b_im, v_s5_c_re, v_s5_c_im, v_s5_d, v_w_glu, v_b_glu, v_w_out, v_ln_g, v_ln_b):
    small_names = ["c_ctx", "b_ada", "sgu_ln_g", "sgu_ln_b", "w_spatial", "b_spatial", "s5_lam_re", "s5_lam_im",
                   "s5_log_step", "s5_b_re", "s5_b_im", "s5_c_re", "s5_c_im", "s5_d", "b_glu", "ln_g", "ln_b"]
    env = dict(locals())
    x2, tgt, ctx2 = x[0], loss_target[0], ctx[0]
    l, d = x2.shape
    lc = ctx2.shape[0]
    w = d // 2
    nh = w // HEAD_DIM_A
    nd, g_s5, p_s5, c_s5 = s5_b_re.shape[1:]
    ns = g_s5 * p_s5
    nsr = ns // LANES
    gb = LANES // c_s5
    me = _index(_mesh_pos())
    ada_cols = w_ada.shape[2]

    srows = _silu_rows(c, c_ctx)
    srows_all = _all_gather(srows, 0, "gather_silu")
    s_mat = jnp.concatenate([srows_all[0::8], srows_all[1:2], jnp.zeros((7, d), F32)], axis=0)
    mod_part = _small_dot(s_mat, w_ada[0], "nn", "mod_cols")
    mod_all = _all_gather(mod_part, 1, "gather_mod") + b_ada
    hw_in, tok_a = _exchange_start(w_in[0].astype(MXU_DTYPE), 1, "gather", "start_gather_w_in", (SIBLING,) + SAME_CORE_PEERS)
    mod_all = mod_all + tok_a[0, 0]
    mod_x = lax.dynamic_slice_in_dim(mod_all, me, 1, axis=0)
    mod_c = mod_all[8:9]
    shift_x, scale_x, gate_x = mod_x[:, :d], mod_x[:, d:2 * d], mod_x[:, 2 * d:]
    shift_c, scale_c = mod_c[:, :d], mod_c[:, d:2 * d]

    lr, li = s5_lam_re[0][:, :, None, :], s5_lam_im[0][:, :, None, :]
    ls = s5_log_step[0][:, :, None, None]
    swapped = ("s5_b_re", "s5_b_im")
    for nm in swapped:
        for pre in ("", "m_", "v_"):
            env[pre + nm] = jnp.swapaxes(env[pre + nm], -1, -2)
    br_t, bi_t = env["s5_b_re"][0], env["s5_b_im"][0]
    ab_re, ab_im, bb_re, bb_im = _disc_fwd(lr, li, ls, br_t, bi_t)
    a_sm = jnp.stack([ab_re, ab_im], axis=1).reshape(nd, 2, nsr, LANES)
    wbr = _block_diag(bb_re.astype(MXU_DTYPE), gb)
    wbi = _block_diag(bb_im.astype(MXU_DTYPE), gb)
    cre_t = _block_diag(s5_c_re[0].astype(MXU_DTYPE), gb)
    ncim_t = _block_diag((-s5_c_im[0]).astype(MXU_DTYPE), gb)
    cre, ncim = jnp.swapaxes(cre_t, 2, 3), jnp.swapaxes(ncim_t, 2, 3)
    wbr_t, wbi_t = jnp.swapaxes(wbr, 2, 3), jnp.swapaxes(wbi, 2, 3)
    d_skip = s5_d

    xm = _ln_mod(x2, shift_x, scale_x, "ln_mod_x")
    cm = _ln_mod(ctx2, shift_c, scale_c, "ln_mod_ctx")
    ready = xm[:8, :LANES].astype(F32) + cm[:8, :LANES].astype(F32) + cre[0, 0, :8, :].astype(F32)
    hw_in2, tok_b = _forward_start(_exchange_wait(hw_in, ready, "wait_gather_w_in"), 1, "start_forward_w_in")
    w_in_f = _forward_wait(hw_in2, tok_b, "wait_forward_w_in")
    hw_glu, tok_c = _exchange_start(w_glu[0].astype(MXU_DTYPE), 0, "gather", "start_gather_w_glu")
    hw_out, tok_o = _exchange_start(w_out[0].astype(MXU_DTYPE), 0, "gather", "start_gather_w_out")
    proj = _matmul(xm, w_in_f, mode="nn", name="proj", dep=tok_c + tok_o)
    ub_c = _matmul(cm, w_in_f, mode="nn", name="proj_ctx", b_n0=3 * w, n=w)
    bsf = jnp.repeat(b_spatial[0].T, HEAD_DIM_A, axis=1)
    ws = w_spatial[0]
    cat = _ga_fwd(proj, sgu_ln_g, sgu_ln_b, ws, bsf, w)
    zeros_state = jnp.zeros((2, nsr, LANES), F32)
    s5c, s5l = [], []
    for dr in range(nd):
        s5c.append(_s5_fwd(ub_c, 0, w, zeros_state, a_sm, wbr, wbi, cre, ncim, dr, f"s5_fwd_ctx{dr}"))
        s5l.append(_s5_fwd(proj, 3, w, s5c[dr][5], a_sm, wbr, wbi, cre, ncim, dr, f"s5_fwd{dr}"))
    w_glu_f = _exchange_wait(hw_glu, s5l[1][0], "wait_gather_w_glu")
    ys, cat = _glu_fwd(s5l[0][0], s5l[1][0], proj, cat, d_skip, w_glu_f, b_glu, w)
    w_out_f = _exchange_wait(hw_out, ys, "wait_gather_w_out")
    out = _matmul(cat, w_out_f, mode="nn", name="out_proj")
    loss_row, dout, dx_res, dgate, dln_g, dln_b = _post_ln_loss(x2, out, gate_x, ln_g, ln_b, tgt)

    dcat = _matmul(dout, w_out_f, mode="nt", name="d_cat", out_dtype=MXU_DTYPE)
    dw_out = _matmul(cat, dout, mode="tn", name="d_w_out", out_dtype=MXU_DTYPE)
    hg_out, tok_d = _exchange_start(dw_out, 0, "a2a", "start_a2a_d_w_out")
    dys, dproj, db_glu, dd_skip, dw_glu = _glu_bwd(dcat, ys, proj, w_glu_f, b_glu, w, tok_d)
    hg_glu, tok_e = _exchange_start(dw_glu.astype(MXU_DTYPE), 0, "a2a", "start_a2a_d_w_glu")
    zeros_state = zeros_state + tok_e[0, 0]
    du_l, du_c, dwbr, dwbi, dcre, dncim, da_sm = [], [], [], [], [], [], []
    nbk, spb = w // LANES, gb * p_s5
    for dr in range(nd):
        bl = _s5_bwd(dys, proj, 3, w, *s5l[dr][1:5], s5c[dr][5], zeros_state, a_sm, wbr_t, wbi_t,
                     cre_t, ncim_t, dr, f"s5_bwd{dr}")
        bc = _s5_bwd(None, ub_c, 0, w, s5c[dr][1], s5c[dr][2], None, None, zeros_state, bl[6], a_sm, wbr_t, wbi_t,
                     None, None, dr, f"s5_bwd_ctx{dr}")
        du_l.append(bl[0])
        du_c.append(bc[0])
        dwbr.append(_add2(bl[1].reshape(nbk * LANES, spb), bc[1].reshape(nbk * LANES, spb), f"sum_dwbr{dr}"))
        dwbi.append(_add2(bl[2].reshape(nbk * LANES, spb), bc[2].reshape(nbk * LANES, spb), f"sum_dwbi{dr}"))
        dcre.append(bl[3])
        dncim.append(bl[4])
        da_sm.append(_add2(bl[5].reshape(2 * nsr, LANES), bc[3].reshape(2 * nsr, LANES), f"sum_da{dr}"))
    dub_c = _add2(du_c[0], du_c[1], "dub_ctx")
    dwbr = jnp.stack(dwbr).reshape(nd, nbk, LANES, spb)
    dwbi = jnp.stack(dwbi).reshape(nd, nbk, LANES, spb)
    dcre, dncim = jnp.stack(dcre), jnp.stack(dncim)
    da_sm = jnp.stack(da_sm).reshape(nd, 2, g_s5, p_s5)
    dproj, dsg, dsb, dws, dbsf = _ga_bwd(proj, dcat, dproj, sgu_ln_g, sgu_ln_b, ws, bsf, dys, du_l[0], du_l[1], d_skip, w)

    dbb_re = _block_diag_extract(dwbr, gb, c_s5, p_s5)
    dbb_im = _block_diag_extract(dwbi, gb, c_s5, p_s5)
    dc_re = jnp.swapaxes(_block_diag_extract(dcre, gb, p_s5, c_s5), 2, 3)
    dc_im = -jnp.swapaxes(_block_diag_extract(dncim, gb, p_s5, c_s5), 2, 3)
    dlr, dli, dls, db_re, db_im = _disc_bwd(lr, li, ls, br_t, bi_t, da_sm[:, 0:1].reshape(nd, g_s5, 1, p_s5),
                                            da_sm[:, 1:2].reshape(nd, g_s5, 1, p_s5), dbb_re, dbb_im)
    expand = (jnp.arange(w)[:, None] // HEAD_DIM_A == jnp.arange(LANES)[None, :]).astype(F32)
    db_sp = _lane_group_sum(dbsf, expand, "d_b_spatial")[:, :nh].T

    local = {"sgu_ln_g": dsg, "sgu_ln_b": dsb, "w_spatial": dws, "b_spatial": db_sp,
             "s5_lam_re": dlr, "s5_lam_im": dli, "s5_log_step": dls, "s5_b_re": db_re, "s5_b_im": db_im,
             "s5_c_re": dc_re, "s5_c_im": dc_im, "s5_d": dd_skip, "b_glu": db_glu, "ln_g": dln_g, "ln_b": dln_b}
    reduced = sorted(local, key=lambda n: -math.prod(env[n].shape))
    loss_part = (0.5 / d) * jnp.sum(loss_row)
    flat = jnp.concatenate([local[n].reshape(-1) for n in reduced] + [loss_part.reshape(1)])
    unit = N_DEV * 8 * LANES
    total = -(-flat.shape[0] // unit) * unit
    flat = jnp.pad(flat, (0, total - flat.shape[0])).reshape(N_DEV * 8, total // (N_DEV * 8))
    h_small, tok_s = _exchange_start(flat, 0, "a2a", "start_a2a_small")

    dw_in = _matmul(xm, dproj, mode="tn", name="d_w_in", out_dtype=MXU_DTYPE, dep=tok_s)
    dw_in = _matmul(cm, dub_c, mode="tn", name="d_w_in_ctx", acc_in=dw_in, acc_n0=3 * w, out_dtype=MXU_DTYPE)
    hg_in, tok_f = _exchange_start(dw_in, 1, "a2a", "start_a2a_d_w_in")
    mine = _sum_slots(_exchange_wait(h_small, dw_in, "wait_a2a_small"), "sum_small")
    h_sums, tok_g = _exchange_start(mine, 0, "gather", "start_gather_small")
    dxm = _matmul(dproj, w_in_f, mode="nt", name="d_xm", dep=tok_f + tok_g, out_dtype=MXU_DTYPE)
    dcm = _matmul(dub_c, w_in_f, mode="nt", name="d_cm", b_k0=3 * w, k=w)
    grad_x, dshift_x, dscale_x = _ln_mod_bwd(x2, dxm, scale_x, dx_res, "ln_mod_x_bwd")
    _, dshift_c, dscale_c = _ln_mod_bwd(ctx2, dcm, scale_c, None, "ln_mod_ctx_bwd")

    dmod_rows = jnp.concatenate([jnp.concatenate([dshift_x, dscale_x, dgate], axis=1),
                                 jnp.concatenate([dshift_c, dscale_c, jnp.zeros((1, d), F32)], axis=1),
                                 jnp.zeros((6, 3 * d), F32)], axis=0)
    h_dmod, tok_m = _exchange_start(dmod_rows, 0, "gather", "start_gather_dmod")

    gp_w_out = _exchange_wait(hg_out, tok_m, "wait_a2a_d_w_out")
    gp_w_glu = _exchange_wait(hg_glu, tok_m, "wait_a2a_d_w_glu")
    gp_w_in = _exchange_wait(hg_in, tok_m, "wait_a2a_d_w_in")
    big = {
        "w_in": _adamw(w_in[0], m_w_in[0], v_w_in[0], gp_w_in, "adamw_w_in"),
        "w_glu": _adamw(w_glu[0], m_w_glu[0], v_w_glu[0], gp_w_glu, "adamw_w_glu"),
        "w_out": _adamw(w_out[0], m_w_out[0], v_w_out[0], gp_w_out, "adamw_w_out"),
    }
    dmod_all = _exchange_wait(h_dmod, big["w_out"][0], "wait_gather_dmod")
    dmod_ctx = _sum_slots(dmod_all[1::8].reshape(N_DEV, 1, 3 * d), "sum_dmod_ctx")
    dmod_mat = jnp.concatenate([dmod_all[0::8], dmod_ctx, jnp.zeros((7, 3 * d), F32)], axis=0)
    db_ada = _sum_slots(dmod_mat[:9].reshape(9, 1, 3 * d), "sum_db_ada")
    dmod_mine = lax.dynamic_slice_in_dim(dmod_mat, me * ada_cols, ada_cols, axis=1)
    dw_ada = _small_dot(s_mat, dmod_mine, "tn", "d_w_ada")
    dsilu_cc = _small_dot(dmod_mine[8:16], w_ada[0], "nt", "d_silu_cctx")[0:1]
    dc_ctx_part = dsilu_cc * _silu_grad(c_ctx.reshape(1, d))
    dc_ctx_rows = jnp.concatenate([dc_ctx_part, jnp.zeros((7, d), F32)], axis=0)
    h_cctx, tok_c2 = _exchange_start(dc_ctx_rows, 0, "gather", "start_gather_d_c_ctx")

    big["w_ada"] = _adamw(w_ada[0], m_w_ada[0], v_w_ada[0], dw_ada[None], "adamw_w_ada", dep=tok_c2)
    summed = _exchange_wait(h_sums, big["w_ada"][0], "wait_gather_small").reshape(-1)
    grads, off = {"b_ada": db_ada}, 0
    for n in reduced:
        size = math.prod(env[n].shape)
        grads[n] = summed[off:off + size].reshape(env[n].shape)
        off += size
    loss = summed[off]
    res = {n: tuple(a[None] for a in big[n]) for n in big}

    def small_step(n):
        res[n] = (grads[n],) + _adamw_nd(env[n], env["m_" + n], env["v_" + n], grads[n], "adamw_" + n)
        if n in swapped:
            res[n] = tuple(jnp.swapaxes(a, -1, -2) for a in res[n])

    large = ("w_spatial", "s5_b_re", "s5_b_im", "s5_c_re", "s5_c_im")
    tiny = [n for n in small_names if n not in large and n != "c_ctx"]
    for n in large:
        small_step(n)
    for n, out in zip(tiny, _adamw_many([(env[n], env["m_" + n], env["v_" + n], grads[n]) for n in tiny], "adamw_tiny")):
        res[n] = (grads[n],) + out
    dc_ctx_all = _exchange_wait(h_cctx, res["w_spatial"][1], "wait_gather_d_c_ctx")
    grads["c_ctx"] = _sum_slots(dc_ctx_all[0::8].reshape(N_DEV, 1, d), "sum_d_c_ctx").reshape(d)
    small_step("c_ctx")

    order = ["c_ctx", "w_ada", "b_ada", "w_in", "sgu_ln_g", "sgu_ln_b", "w_spatial", "b_spatial", "s5_lam_re", "s5_lam_im",
             "s5_log_step", "s5_b_re", "s5_b_im", "s5_c_re", "s5_c_im", "s5_d", "w_glu", "b_glu", "w_out", "ln_g", "ln_b"]
    return (loss, grad_x[None], *[res[n][0] for n in order], *[res[n][1] for n in order],
            *[res[n][2] for n in order], *[res[n][3] for n in order])
```

```python
import functools
import math

import jax
import jax.numpy as jnp
from jax import lax
from jax.experimental import pallas as pl
from jax.experimental.pallas import tpu as pltpu

F32 = jnp.float32
MXU_DTYPE = jnp.bfloat16
N_DEV = 8
MESH_ID = pl.DeviceIdType.MESH
LN_EPS = 1e-6
DEPTH = 1
ALPHA = (2.0 * DEPTH) ** 0.25
CHUNK = 128
HEAD_DIM_A = 128
ADAM_LR, ADAM_B1, ADAM_B2, ADAM_EPS, ADAM_WD, ADAM_STEP = 0.001, 0.9, 0.999, 1e-08, 0.01, 10
LANES = 128
SCAN_UNROLL = 16
VMEM_LIMIT = 56 * 1024 * 1024
HBM = pl.BlockSpec(memory_space=pl.ANY)


def _cparams(*sem):
    return pltpu.CompilerParams(dimension_semantics=sem if sem else None, vmem_limit_bytes=VMEM_LIMIT)


def _tile(n, pref, mult=1):
    if n <= pref:
        return n
    t = pref - pref % mult
    while n % t:
        t -= mult
    return t


def _gelu(x):
    return 0.5 * x * (1.0 + lax.erf(x * (1.0 / math.sqrt(2.0))))


def _gelu_grad(x):
    return 0.5 * (1.0 + lax.erf(x * (1.0 / math.sqrt(2.0)))) + x * jnp.exp(-0.5 * x * x) * (1.0 / math.sqrt(2.0 * math.pi))


def _silu_grad(x):
    s = jax.nn.sigmoid(x)
    return s * (1.0 + x * (1.0 - s))


def _mxu_dot(a, b, dims=(((1,), (0,)), ((), ()))):
    return lax.dot_general(a.astype(MXU_DTYPE), b.astype(MXU_DTYPE), dims, preferred_element_type=F32)


_NT = (((1,), (1,)), ((), ()))
_TN = (((0,), (0,)), ((), ()))


def _mesh_pos():
    return lax.axis_index("x"), lax.axis_index("y"), lax.axis_index("c")


def _peer(pos, r):
    x, y, c = pos
    return ((1 - x) if r & 4 else x, (1 - y) if r & 2 else y, (1 - c) if r & 1 else c)


def _index(pos):
    return 4 * pos[0] + 2 * pos[1] + pos[2]


def _slice_of(ref, axis, idx, size):
    start = idx * size
    if axis == 0:
        return ref.at[pl.ds(start, size)]
    return ref.at[:, pl.ds(start, size)]


def _all_gather(x, axis, name):
    size = x.shape[axis]
    out_shape = tuple(s * N_DEV if a == axis else s for a, s in enumerate(x.shape))

    def body(x_ref, o_ref, send_sems, recv_sems, local_sem):
        me = _mesh_pos()
        mine = pltpu.make_async_copy(x_ref, _slice_of(o_ref, axis, _index(me), size), local_sem)
        mine.start()

        def copy(r, block):
            return pltpu.make_async_remote_copy(
                src_ref=x_ref, dst_ref=_slice_of(o_ref, axis, _index(block), size),
                send_sem=send_sems.at[r - 1], recv_sem=recv_sems.at[r - 1],
                device_id=_peer(me, r), device_id_type=MESH_ID)

        sends = [copy(r, me) for r in range(1, N_DEV)]
        for cp in sends:
            cp.start()
        for r in range(1, N_DEV):
            copy(r, _peer(me, r)).wait_recv()
        for cp in sends:
            cp.wait_send()
        mine.wait()

    return pl.pallas_call(
        body, name=name, out_shape=jax.ShapeDtypeStruct(out_shape, x.dtype),
        in_specs=[HBM], out_specs=HBM,
        scratch_shapes=[pltpu.SemaphoreType.DMA((N_DEV - 1,)), pltpu.SemaphoreType.DMA((N_DEV - 1,)),
                        pltpu.SemaphoreType.DMA],
    )(x)


_SEM = pl.BlockSpec(memory_space=pltpu.SEMAPHORE)
_HBM = pl.BlockSpec(memory_space=pltpu.HBM)
_EFFECT = pltpu.SideEffectType.DATAFLOW_SIDE_EFFECTING
ALL_PEERS = tuple(range(1, N_DEV))
SIBLING = 1
SAME_CORE_PEERS = (2, 4, 6)


def _exchange_copy(kind, x_ref, land_ref, axis, size, send_sems, recv_sems, me, rels, q, arriving):
    peer = _peer(me, rels[q])
    sender, receiver = (peer, me) if arriving else (me, peer)
    if kind == "gather":
        src, dst = x_ref, _slice_of(land_ref, axis, _index(sender), size)
    else:
        src, dst = _slice_of(x_ref, axis, _index(receiver), size), land_ref.at[_index(sender)]
    return pltpu.make_async_remote_copy(src_ref=src, dst_ref=dst, send_sem=send_sems.at[q], recv_sem=recv_sems.at[q],
                                        device_id=peer, device_id_type=MESH_ID)


def _local_copy(kind, x_ref, land_ref, axis, size, me, local_sem):
    if kind == "gather":
        return pltpu.make_async_copy(x_ref, _slice_of(land_ref, axis, _index(me), size), local_sem)
    return pltpu.make_async_copy(_slice_of(x_ref, axis, _index(me), size), land_ref.at[_index(me)], local_sem)


def _exchange_start(x, axis, kind, name, rels=ALL_PEERS):
    size = x.shape[axis] if kind == "gather" else x.shape[axis] // N_DEV
    if kind == "gather":
        land_shape = tuple(s * N_DEV if a == axis else s for a, s in enumerate(x.shape))
    else:
        land_shape = (N_DEV,) + tuple(size if a == axis else s for a, s in enumerate(x.shape))

    def body(x_ref, land_ref, send_sems, recv_sems, local_sem, x_thru, land_thru, token):
        del x_thru, land_thru
        me = _mesh_pos()
        _local_copy(kind, x_ref, land_ref, axis, size, me, local_sem).start()
        for q in range(len(rels)):
            _exchange_copy(kind, x_ref, land_ref, axis, size, send_sems, recv_sems, me, rels, q, False).start()
        token[...] = jnp.zeros_like(token)

    sems = pltpu.SemaphoreType.DMA((len(rels),))
    send_sems, recv_sems, local_sem, x_thru, land_thru, token = pl.pallas_call(
        body, name=name,
        out_shape=(sems, sems, pltpu.SemaphoreType.DMA(()), pltpu.HBM(x.shape, x.dtype), pltpu.HBM(land_shape, x.dtype),
                   jax.ShapeDtypeStruct((8, LANES), F32)),
        in_specs=(_HBM, _HBM), out_specs=(_SEM, _SEM, _SEM, _HBM, _HBM, pl.BlockSpec(memory_space=pltpu.VMEM)),
        input_output_aliases={0: 3, 1: 4}, compiler_params=pltpu.CompilerParams(has_side_effects=_EFFECT),
    )(pltpu.with_memory_space_constraint(x, pltpu.HBM),
      pltpu.with_memory_space_constraint(lax.empty(land_shape, x.dtype), pltpu.HBM))
    return (kind, axis, size, rels, send_sems, recv_sems, local_sem, x_thru, land_thru), token


def _exchange_wait(handle, after, name):
    kind, axis, size, rels, send_sems, recv_sems, local_sem, x_thru, land_thru = handle

    def body(x_ref, land_ref, send_sems, recv_sems, local_sem, after_ref, x_dead, got_ref):
        del after_ref, x_dead, got_ref
        me = _mesh_pos()
        _local_copy(kind, x_ref, land_ref, axis, size, me, local_sem).wait()
        for q in range(len(rels)):
            _exchange_copy(kind, x_ref, land_ref, axis, size, send_sems, recv_sems, me, rels, q, False).wait_send()
        for q in range(len(rels)):
            _exchange_copy(kind, x_ref, land_ref, axis, size, send_sems, recv_sems, me, rels, q, True).wait_recv()

    return pl.pallas_call(
        body, name=name, out_shape=(pltpu.HBM(x_thru.shape, x_thru.dtype), pltpu.HBM(land_thru.shape, land_thru.dtype)),
        in_specs=(_HBM, _HBM, _SEM, _SEM, _SEM, HBM), out_specs=(_HBM, _HBM), input_output_aliases={0: 0, 1: 1},
        compiler_params=pltpu.CompilerParams(has_side_effects=_EFFECT),
    )(x_thru, land_thru, send_sems, recv_sems, local_sem, after)[1]


def _forward_copy(land_ref, axis, size, send_sems, recv_sems, me, q, arriving):
    sibling = _peer(me, SIBLING)
    owner = _peer(sibling if arriving else me, SAME_CORE_PEERS[q])
    block = _slice_of(land_ref, axis, _index(owner), size)
    return pltpu.make_async_remote_copy(src_ref=block, dst_ref=block, send_sem=send_sems.at[q], recv_sem=recv_sems.at[q],
                                        device_id=sibling, device_id_type=MESH_ID)


def _forward_start(land, axis, name):
    size = land.shape[axis] // N_DEV

    def body(land_ref, send_sems, recv_sems, land_thru, token):
        del land_thru
        me = _mesh_pos()
        for q in range(len(SAME_CORE_PEERS)):
            _forward_copy(land_ref, axis, size, send_sems, recv_sems, me, q, False).start()
        token[...] = jnp.zeros_like(token)

    sems = pltpu.SemaphoreType.DMA((len(SAME_CORE_PEERS),))
    send_sems, recv_sems, land_thru, token = pl.pallas_call(
        body, name=name, out_shape=(sems, sems, pltpu.HBM(land.shape, land.dtype), jax.ShapeDtypeStruct((8, LANES), F32)),
        in_specs=(_HBM,), out_specs=(_SEM, _SEM, _HBM, pl.BlockSpec(memory_space=pltpu.VMEM)),
        input_output_aliases={0: 2}, compiler_params=pltpu.CompilerParams(has_side_effects=_EFFECT),
    )(land)
    return (axis, size, send_sems, recv_sems, land_thru), token


def _forward_wait(handle, after, name):
    axis, size, send_sems, recv_sems, land_thru = handle

    def body(land_ref, send_sems, recv_sems, after_ref, got_ref):
        del after_ref, got_ref
        me = _mesh_pos()
        for q in range(len(SAME_CORE_PEERS)):
            _forward_copy(land_ref, axis, size, send_sems, recv_sems, me, q, False).wait_send()
        for q in range(len(SAME_CORE_PEERS)):
            _forward_copy(land_ref, axis, size, send_sems, recv_sems, me, q, True).wait_recv()

    return pl.pallas_call(
        body, name=name, out_shape=pltpu.HBM(land_thru.shape, land_thru.dtype),
        in_specs=(_HBM, _SEM, _SEM, HBM), out_specs=_HBM, input_output_aliases={0: 0},
        compiler_params=pltpu.CompilerParams(has_side_effects=_EFFECT),
    )(land_thru, send_sems, recv_sems, after)


WHOLE = 1 << 30
MATMUL_TILES = {
    "proj": (1024, 1024, WHOLE), "proj_ctx": (256, WHOLE, WHOLE), "out_proj": (1024, 1024, WHOLE),
    "d_cat": (1024, 1024, WHOLE), "d_w_out": (1024, 1024, 2048), "d_w_in": (1024, 1280, 2048),
    "d_w_in_ctx": (1024, WHOLE, WHOLE), "d_xm": (1024, 512, WHOLE), "d_cm": (256, 1024, WHOLE),
}


def _matmul(a, b, *, mode, name, out_dtype=F32, b_n0=0, n=None, b_k0=0, k=None, acc_in=None, acc_n0=0, dep=None):
    bm, bn, bk = MATMUL_TILES[name]
    if mode == "tn":
        kk, m = a.shape
    else:
        m, kk = a.shape
    if mode == "nn":
        n = b.shape[1] if n is None else n
    elif mode == "nt":
        n = b.shape[0]
        kk = kk if k is None else k
    else:
        n = b.shape[1]
    bm, bn, bk = _tile(m, bm), _tile(n, bn), _tile(kk, bk)
    nk = kk // bk
    assert b_n0 % bn == 0 and b_k0 % bk == 0 and acc_n0 % bn == 0
    dims = {"nn": (((1,), (0,)), ((), ())), "nt": _NT, "tn": _TN}[mode]

    n_in = 2 + (acc_in is not None) + (dep is not None)

    def body(*refs):
        a_ref, b_ref = refs[:2]
        init = refs[2] if acc_in is not None else None
        o_ref = refs[n_in]
        acc_ref = refs[-1] if nk > 1 else None
        p = _mxu_dot(a_ref[...], b_ref[...], dims)
        if nk == 1:
            o_ref[...] = (p if init is None else p + init[...]).astype(out_dtype)
            return
        ki = pl.program_id(2)

        @pl.when(ki == 0)
        def _():
            acc_ref[...] = p if init is None else p + init[...]

        @pl.when(ki > 0)
        def _():
            acc_ref[...] += p

        @pl.when(ki == nk - 1)
        def _():
            o_ref[...] = acc_ref[...].astype(out_dtype)

    a_spec = pl.BlockSpec((bk, bm), lambda j, i, q: (q, i)) if mode == "tn" else pl.BlockSpec((bm, bk), lambda j, i, q: (i, q))
    if mode == "nt":
        b_spec = pl.BlockSpec((bn, bk), lambda j, i, q: (j, q + b_k0 // bk))
    else:
        b_spec = pl.BlockSpec((bk, bn), lambda j, i, q: (q, j + b_n0 // bn))
    in_specs, args, aliases = [a_spec, b_spec], [a, b], {}
    out_map = lambda j, i, q: (i, j + acc_n0 // bn)
    if acc_in is not None:
        in_specs.append(pl.BlockSpec((bm, bn), out_map))
        args.append(acc_in)
        aliases = {2: 0}
        out_shape = jax.ShapeDtypeStruct(acc_in.shape, out_dtype)
    else:
        out_shape = jax.ShapeDtypeStruct((m, n), out_dtype)
    if dep is not None:
        in_specs.append(HBM)
        args.append(dep)
    return pl.pallas_call(
        body, name=name, out_shape=out_shape, grid=(n // bn, m // bm, nk),
        in_specs=in_specs, out_specs=pl.BlockSpec((bm, bn), out_map),
        scratch_shapes=[pltpu.VMEM((bm, bn), F32)] if nk > 1 else [],
        input_output_aliases=aliases,
        compiler_params=_cparams("parallel", "parallel", "arbitrary"),
    )(*args)


def _silu_rows(c, c_ctx):
    d = c.shape[-1]

    def body(c_ref, cc_ref, o_ref):
        o_ref[...] = jnp.zeros_like(o_ref)
        o_ref[0:1, :] = jax.nn.silu(c_ref[...])
        o_ref[1:2, :] = jax.nn.silu(cc_ref[...])

    return pl.pallas_call(body, name="silu_rows", out_shape=jax.ShapeDtypeStruct((8, d), F32))(
        c.reshape(1, d), c_ctx.reshape(1, d))


def _small_dot(a, b, mode, name):
    dims = {"nn": (((1,), (0,)), ((), ())), "nt": _NT, "tn": _TN}[mode]
    m = a.shape[1] if mode == "tn" else a.shape[0]
    n = b.shape[0] if mode == "nt" else b.shape[1]

    def body(a_ref, b_ref, o_ref):
        o_ref[...] = lax.dot_general(a_ref[...], b_ref[...], dims, preferred_element_type=F32,
                                     precision=lax.Precision.HIGHEST)

    return pl.pallas_call(body, name=name, out_shape=jax.ShapeDtypeStruct((m, n), F32),
                          compiler_params=_cparams())(a, b)


def _ln_stats(x):
    mu = jnp.mean(x, axis=-1, keepdims=True)
    xc = x - mu
    var = jnp.mean(xc * xc, axis=-1, keepdims=True)
    rstd = lax.rsqrt(var + LN_EPS)
    return xc * rstd, rstd


def _ln_mod(x, shift, scale, name):
    l, d = x.shape
    tl = _tile(l, 512)

    def body(x_ref, sh_ref, sc_ref, o_ref):
        xhat, _ = _ln_stats(x_ref[...])
        o_ref[...] = (xhat * (1.0 + sc_ref[...]) + sh_ref[...]).astype(o_ref.dtype)

    row = pl.BlockSpec((tl, d), lambda i: (i, 0))
    vec = pl.BlockSpec((1, d), lambda i: (0, 0))
    return pl.pallas_call(body, name=name, out_shape=jax.ShapeDtypeStruct((l, d), MXU_DTYPE), grid=(l // tl,),
                          in_specs=[row, vec, vec], out_specs=row, compiler_params=_cparams("parallel"))(x, shift, scale)


def _ln_mod_bwd(x, dxm, scale, res, name):
    l, d = x.shape
    tl = _tile(l, 512)
    with_res = res is not None

    def body(*refs):
        if with_res:
            x_ref, g_ref, sc_ref, r_ref, dx_ref, dsh_ref, dsc_ref = refs
        else:
            x_ref, g_ref, sc_ref, dx_ref, dsh_ref, dsc_ref = refs
        i = pl.program_id(0)
        xhat, rstd = _ln_stats(x_ref[...])
        g = g_ref[...].astype(F32)
        dxh = g * (1.0 + sc_ref[...])
        dx = rstd * (dxh - jnp.mean(dxh, axis=-1, keepdims=True) - xhat * jnp.mean(dxh * xhat, axis=-1, keepdims=True))
        dx_ref[...] = dx + r_ref[...].astype(F32) if with_res else dx

        @pl.when(i == 0)
        def _():
            dsh_ref[...] = jnp.zeros_like(dsh_ref)
            dsc_ref[...] = jnp.zeros_like(dsc_ref)

        dsh_ref[...] += jnp.sum(g, axis=0, keepdims=True)
        dsc_ref[...] += jnp.sum(g * xhat, axis=0, keepdims=True)

    row = pl.BlockSpec((tl, d), lambda i: (i, 0))
    vec = pl.BlockSpec((1, d), lambda i: (0, 0))
    args = [x, dxm, scale] + ([res] if with_res else [])
    return pl.pallas_call(
        body, name=name,
        out_shape=(jax.ShapeDtypeStruct((l, d), F32), jax.ShapeDtypeStruct((1, d), F32), jax.ShapeDtypeStruct((1, d), F32)),
        grid=(l // tl,), in_specs=[row, row, vec] + ([row] if with_res else []), out_specs=(row, vec, vec),
        compiler_params=_cparams("arbitrary"))(*args)


def _post_ln_loss(x, out, gate, ln_g, ln_b, target):
    l, d = x.shape
    tl = _tile(l, 512)

    def body(x_ref, o_ref, gate_ref, g_ref, b_ref, t_ref, loss_ref, dout_ref, dxr_ref, dgate_ref, dg_ref, db_ref):
        i = pl.program_id(0)
        out_t = o_ref[...]
        gate_v = gate_ref[...]
        rhat, rstd = _ln_stats(ALPHA * x_ref[...] + gate_v * out_t)
        ln_gv = g_ref[...]
        diff = rhat * ln_gv + b_ref[...] - t_ref[...]
        dy = diff * (1.0 / d)
        drh = dy * ln_gv
        dr = rstd * (drh - jnp.mean(drh, axis=-1, keepdims=True) - rhat * jnp.mean(drh * rhat, axis=-1, keepdims=True))
        dout_ref[...] = (gate_v * dr).astype(dout_ref.dtype)
        dxr_ref[...] = (ALPHA * dr).astype(dxr_ref.dtype)

        @pl.when(i == 0)
        def _():
            for r in (loss_ref, dgate_ref, dg_ref, db_ref):
                r[...] = jnp.zeros_like(r)

        loss_ref[...] += jnp.sum(diff * diff, axis=0, keepdims=True)
        dgate_ref[...] += jnp.sum(dr * out_t, axis=0, keepdims=True)
        dg_ref[...] += jnp.sum(dy * rhat, axis=0, keepdims=True)
        db_ref[...] += jnp.sum(dy, axis=0, keepdims=True)

    row = pl.BlockSpec((tl, d), lambda i: (i, 0))
    vec = pl.BlockSpec((1, d), lambda i: (0, 0))
    v = jax.ShapeDtypeStruct((1, d), F32)
    return pl.pallas_call(
        body, name="post_ln_loss",
        out_shape=(v, jax.ShapeDtypeStruct((l, d), MXU_DTYPE), jax.ShapeDtypeStruct((l, d), MXU_DTYPE), v, v, v),
        grid=(l // tl,), in_specs=[row, row, vec, vec, vec, row], out_specs=(vec, row, row, vec, vec, vec),
        compiler_params=_cparams("arbitrary"))(x, out, gate, ln_g, ln_b, target)


def _ga_forward_tile(p, g, b, ws_ref, bsf, w, nc, nh):
    u_raw, v_raw, za = p[:, :w], p[:, w:2 * w], p[:, 2 * w:3 * w]
    gu = _gelu(u_raw)
    vhat, rstd = _ln_stats(_gelu(v_raw))
    vn = vhat * g + b
    rows = []
    for ci in range(nc):
        r0 = ci * CHUNK
        heads = [_mxu_dot(ws_ref[h], vn[r0:r0 + CHUNK, h * HEAD_DIM_A:(h + 1) * HEAD_DIM_A]) for h in range(nh)]
        rows.append(jnp.concatenate(heads, axis=1) + bsf)
    mixed = jnp.concatenate(rows, axis=0) if nc > 1 else rows[0]
    return u_raw, v_raw, za, gu, vhat, rstd, vn, mixed


def _ga_fwd(proj, g, b, ws, bsf, w):
    l = proj.shape[0]
    nh = w // HEAD_DIM_A
    nc = _tile(l // CHUNK, 2)
    tl = nc * CHUNK

    def body(p_ref, g_ref, b_ref, ws_ref, bsf_ref, o_ref):
        _, _, za, gu, _, _, _, mixed = _ga_forward_tile(p_ref[...], g_ref[...], b_ref[...], ws_ref, bsf_ref[...], w, nc, nh)
        o_ref[...] = (gu * mixed * jax.nn.silu(za)).astype(o_ref.dtype)

    vec = pl.BlockSpec((1, w), lambda i: (0, 0))
    return pl.pallas_call(
        body, name="ga_fwd", out_shape=jax.ShapeDtypeStruct((l, 2 * w), MXU_DTYPE), grid=(l // tl,),
        in_specs=[pl.BlockSpec((tl, 3 * w), lambda i: (i, 0)), vec, vec,
                  pl.BlockSpec((nh, CHUNK, CHUNK), lambda i: (0, 0, 0)), pl.BlockSpec((CHUNK, w), lambda i: (0, 0))],
        out_specs=pl.BlockSpec((tl, w), lambda i: (i, 0)), compiler_params=_cparams("parallel"))(proj, g, b, ws, bsf)


def _ga_bwd(proj, dcat, dproj, g, b, ws, bsf, dys, du0, du1, d_skip, w):
    l = proj.shape[0]
    nh = w // HEAD_DIM_A
    nc = _tile(l // CHUNK, 2)
    tl = nc * CHUNK

    def body(p_ref, dy_ref, dp_in, g_ref, b_ref, ws_ref, bsf_ref, dys_ref, du0_ref, du1_ref, d_ref,
             dp_ref, dg_ref, db_ref, dws_ref, dbsf_ref):
        del dp_in
        i = pl.program_id(0)
        dp_ref[:, 3 * w:] = (dys_ref[...] * d_ref[...] + du0_ref[...] + du1_ref[...]).astype(dp_ref.dtype)
        gv = g_ref[...]
        u_raw, v_raw, za, gu, vhat, rstd, vn, mixed = _ga_forward_tile(
            p_ref[...], gv, b_ref[...], ws_ref, bsf_ref[...], w, nc, nh)
        dya = dy_ref[...].astype(F32)
        sz = jax.nn.silu(za)
        dmixed = dya * gu * sz
        dza = dya * gu * mixed * _silu_grad(za)
        dgu = dya * mixed * sz

        @pl.when(i == 0)
        def _():
            for r in (dg_ref, db_ref, dws_ref, dbsf_ref):
                r[...] = jnp.zeros_like(r)

        rows = []
        for ci in range(nc):
            r0 = ci * CHUNK
            heads = []
            for h in range(nh):
                cols = slice(h * HEAD_DIM_A, (h + 1) * HEAD_DIM_A)
                dm = dmixed[r0:r0 + CHUNK, cols]
                heads.append(_mxu_dot(ws_ref[h], dm, _TN))
                dws_ref[h] += _mxu_dot(dm, vn[r0:r0 + CHUNK, cols], _NT)
            rows.append(jnp.concatenate(heads, axis=1))
            dbsf_ref[...] += dmixed[r0:r0 + CHUNK, :]
        dvn = jnp.concatenate(rows, axis=0) if nc > 1 else rows[0]
        dg_ref[...] += jnp.sum(dvn * vhat, axis=0, keepdims=True)
        db_ref[...] += jnp.sum(dvn, axis=0, keepdims=True)
        dvh = dvn * gv
        dgv = rstd * (dvh - jnp.mean(dvh, axis=-1, keepdims=True) - vhat * jnp.mean(dvh * vhat, axis=-1, keepdims=True))
        dp_ref[:, :w] = (dgu * _gelu_grad(u_raw)).astype(dp_ref.dtype)
        dp_ref[:, w:2 * w] = (dgv * _gelu_grad(v_raw)).astype(dp_ref.dtype)
        dp_ref[:, 2 * w:3 * w] = dza.astype(dp_ref.dtype)

    vec = pl.BlockSpec((1, w), lambda i: (0, 0))
    row = pl.BlockSpec((tl, w), lambda i: (i, 0))
    ws_spec = pl.BlockSpec((nh, CHUNK, CHUNK), lambda i: (0, 0, 0))
    bs_spec = pl.BlockSpec((CHUNK, w), lambda i: (0, 0))
    v = jax.ShapeDtypeStruct((1, w), F32)
    return pl.pallas_call(
        body, name="ga_bwd",
        out_shape=(jax.ShapeDtypeStruct(dproj.shape, dproj.dtype), v, v, jax.ShapeDtypeStruct((nh, CHUNK, CHUNK), F32),
                   jax.ShapeDtypeStruct((CHUNK, w), F32)),
        grid=(l // tl,),
        in_specs=[pl.BlockSpec((tl, 3 * w), lambda i: (i, 0)), row, HBM, vec, vec, ws_spec, bs_spec, row, row, row, vec],
        out_specs=(pl.BlockSpec((tl, 4 * w), lambda i: (i, 0)), vec, vec, ws_spec, bs_spec),
        input_output_aliases={2: 0}, compiler_params=_cparams("arbitrary"))(
            proj, dcat, dproj, g, b, ws, bsf, dys, du0, du1, d_skip)


def _lane_group_sum(x, expand, name):
    return _small_dot(x, expand, "nn", name)


def _disc_math(lr, li, ls, br, bi):
    step = jnp.exp(ls)
    dr, di = lr * step, li * step
    mag = jnp.exp(dr)
    ab_re, ab_im = mag * jnp.cos(di), mag * jnp.sin(di)
    den = lr * lr + li * li
    nr, ni = ab_re - 1.0, ab_im
    f_re = (nr * lr + ni * li) / den
    f_im = (ni * lr - nr * li) / den
    bb_re = f_re * br - f_im * bi
    bb_im = f_re * bi + f_im * br
    return ab_re, ab_im, bb_re, bb_im


def _disc_fwd(lr, li, ls, br, bi):
    def body(lr_ref, li_ref, ls_ref, br_ref, bi_ref, o1, o2, o3, o4):
        res = _disc_math(lr_ref[...], li_ref[...], ls_ref[...], br_ref[...], bi_ref[...])
        for o, r in zip((o1, o2, o3, o4), res):
            o[...] = r

    s = lambda a: jax.ShapeDtypeStruct(a.shape, F32)
    return pl.pallas_call(body, name="s5_disc", out_shape=(s(lr), s(lr), s(br), s(br)), compiler_params=_cparams())(
        lr, li, ls, br, bi)


def _disc_bwd(lr, li, ls, br, bi, d_ar, d_ai, d_br, d_bi):
    def body(lr_ref, li_ref, ls_ref, br_ref, bi_ref, c1, c2, c3, c4, o1, o2, o3, o4, o5):
        _, vjp = jax.vjp(_disc_math, lr_ref[...], li_ref[...], ls_ref[...], br_ref[...], bi_ref[...])
        res = vjp((c1[...], c2[...], c3[...], c4[...]))
        for o, r in zip((o1, o2, o3, o4, o5), res):
            o[...] = r

    s = lambda a: jax.ShapeDtypeStruct(a.shape, F32)
    return pl.pallas_call(body, name="s5_disc_bwd", out_shape=(s(lr), s(lr), s(ls), s(br), s(br)),
                          compiler_params=_cparams())(lr, li, ls, br, bi, d_ar, d_ai, d_br, d_bi)


def _dir_spec(a, dr, **kw):
    return pl.BlockSpec((None,) + a.shape[1:], lambda i: (dr,) + (0,) * (a.ndim - 1), **kw)


def _s5_fwd(u_arr, u_col, w, h0, a_sm, wbr, wbi, cre, ncim, dr, name):
    rev = dr == 1
    l = u_arr.shape[0]
    nb = w // LANES
    spb = wbr.shape[-1]
    nsr = a_sm.shape[2]
    assert 2 * spb == 8 * LANES and nb % 2 == 0
    npair = nb // 2
    t = _tile(l, 256)
    n = l // t
    tile = (lambda i: n - 1 - i) if rev else (lambda i: i)

    def body(u_ref, h0_ref, a_ref, wbr_ref, wbi_ref, cre_ref, ncim_ref, y_ref, hr_ref, hi_ref, tr_ref, ti_ref, hfin_ref,
             carry_ref):
        i = pl.program_id(0)

        @pl.when(i == 0)
        def _():
            carry_ref[...] = h0_ref[...]

        for j in range(npair):
            for h_ref, w_ref in ((hr_ref, wbr_ref), (hi_ref, wbi_ref)):
                blk = [_mxu_dot(u_ref[:, k * LANES:(k + 1) * LANES], w_ref[k]) for k in (2 * j, 2 * j + 1)]
                h_ref[j] = jnp.concatenate(blk, axis=1).reshape(t, 8, LANES)
        slab = lambda ref, part, j: ref[part, 8 * j:8 * j + 8, :]
        ar = [slab(a_ref, 0, j) for j in range(npair)]
        ai = [slab(a_ref, 1, j) for j in range(npair)]

        def steps(blk, c):
            hr, hi = list(c[:npair]), list(c[npair:])
            for q in range(SCAN_UNROLL):
                s = blk * SCAN_UNROLL + q
                row = t - 1 - s if rev else s
                for j in range(npair):
                    hr[j], hi[j] = (ar[j] * hr[j] - ai[j] * hi[j] + hr_ref[j, row],
                                    ar[j] * hi[j] + ai[j] * hr[j] + hi_ref[j, row])
                    hr_ref[j, row] = hr[j]
                    hi_ref[j, row] = hi[j]
            return tuple(hr + hi)

        init = tuple(slab(carry_ref, part, j) for part in range(2) for j in range(npair))
        c = lax.fori_loop(0, t // SCAN_UNROLL, steps, init)
        for part in range(2):
            for j in range(npair):
                carry_ref[part, 8 * j:8 * j + 8, :] = c[part * npair + j]
                hfin_ref[part, 8 * j:8 * j + 8, :] = c[part * npair + j]
        for j in range(npair):
            cols8 = slice(j * 8 * LANES, (j + 1) * 8 * LANES)
            tr_ref[:, cols8] = hr_ref[j].reshape(t, 8 * LANES).astype(tr_ref.dtype)
            ti_ref[:, cols8] = hi_ref[j].reshape(t, 8 * LANES).astype(ti_ref.dtype)
        for k in range(nb):
            cols = slice(k * spb, (k + 1) * spb)
            y_ref[:, k * LANES:(k + 1) * LANES] = (_mxu_dot(tr_ref[:, cols], cre_ref[k]) + _mxu_dot(ti_ref[:, cols], ncim_ref[k]))

    full = lambda a: pl.BlockSpec(a.shape, lambda i: (0,) * a.ndim)
    hspec = pl.BlockSpec((npair, t, 8, LANES), lambda i: (0, tile(i), 0, 0))
    tspec = pl.BlockSpec((t, nsr * LANES), lambda i: (tile(i), 0))
    hsh = jax.ShapeDtypeStruct((npair, l, 8, LANES), F32)
    tsh = jax.ShapeDtypeStruct((l, nsr * LANES), MXU_DTYPE)
    return pl.pallas_call(
        body, name=name,
        out_shape=(jax.ShapeDtypeStruct((l, w), F32), hsh, hsh, tsh, tsh, jax.ShapeDtypeStruct((2, nsr, LANES), F32)),
        grid=(n,),
        in_specs=[pl.BlockSpec((t, w), lambda i: (tile(i), u_col)), full(h0)] + [_dir_spec(a, dr) for a in (a_sm, wbr, wbi, cre, ncim)],
        out_specs=(pl.BlockSpec((t, w), lambda i: (tile(i), 0)), hspec, hspec, tspec, tspec,
                   pl.BlockSpec((2, nsr, LANES), lambda i: (0, 0, 0))),
        scratch_shapes=[pltpu.VMEM((2, nsr, LANES), F32)],
        compiler_params=_cparams("arbitrary"))(u_arr, h0, a_sm, wbr, wbi, cre, ncim)


def _s5_bwd(dys, u_arr, u_col, w, hr, hi, tr, ti, hbound, g_in, a_sm, wbr_t, wbi_t, cre_t, ncim_t, dr, name):
    rev = dr == 1
    l = u_arr.shape[0]
    nb = w // LANES
    spb = wbr_t.shape[-2]
    nsr = a_sm.shape[2]
    npair = nb // 2
    t = _tile(l, 256 if l >= 1024 else 128)
    n = l // t
    with_dy = dys is not None
    tile = (lambda i: i) if rev else (lambda i: n - 1 - i)

    def body(*refs):
        if with_dy:
            (dy_ref, u_ref, hr_ref, hi_ref, pr_ref, pi_ref, tr_ref, ti_ref, hb_ref, gin_ref, a_ref, wbrt_ref, wbit_ref,
             cret_ref, ncimt_ref, du_ref, dwbr_ref, dwbi_ref, dcre_ref, dncim_ref, da_ref, gout_ref,
             gr_ref, gi_ref, gtr_ref, gti_ref, carry_ref) = refs
        else:
            (u_ref, hr_ref, hi_ref, pr_ref, pi_ref, hb_ref, gin_ref, a_ref, wbrt_ref, wbit_ref,
             du_ref, dwbr_ref, dwbi_ref, da_ref, gout_ref, gr_ref, gi_ref, gtr_ref, gti_ref, carry_ref) = refs
        i = pl.program_id(0)

        @pl.when(i == 0)
        def _():
            carry_ref[...] = gin_ref[...]
            accs = (dwbr_ref, dwbi_ref, da_ref) + ((dcre_ref, dncim_ref) if with_dy else ())
            for r in accs:
                r[...] = jnp.zeros_like(r)

        if with_dy:
            for j in range(npair):
                for g_ref, c_ref in ((gr_ref, cret_ref), (gi_ref, ncimt_ref)):
                    blk = [_mxu_dot(dy_ref[:, k * LANES:(k + 1) * LANES], c_ref[k]) for k in (2 * j, 2 * j + 1)]
                    g_ref[j] = jnp.concatenate(blk, axis=1).reshape(t, 8, LANES)
        else:
            gr_ref[...] = jnp.zeros_like(gr_ref)
            gi_ref[...] = jnp.zeros_like(gi_ref)
        slab = lambda ref, part, j: ref[part, 8 * j:8 * j + 8, :]
        last = t - 1 if rev else 0
        first = i == n - 1

        ar = [slab(a_ref, 0, j) for j in range(npair)]
        ai = [slab(a_ref, 1, j) for j in range(npair)]

        def steps(blk, c):
            gr, gi, dr, di = (list(c[q * npair:(q + 1) * npair]) for q in range(4))
            for q in range(SCAN_UNROLL):
                s = blk * SCAN_UNROLL + q
                row = s if rev else t - 1 - s
                prow = jnp.minimum(row + 1, t - 1) if rev else jnp.maximum(row - 1, 0)
                for j in range(npair):
                    pr, pi = hr_ref[j, prow], hi_ref[j, prow]
                    gr[j], gi[j] = (gr_ref[j, row] + ar[j] * gr[j] + ai[j] * gi[j],
                                    gi_ref[j, row] + ar[j] * gi[j] - ai[j] * gr[j])
                    gr_ref[j, row] = gr[j]
                    gi_ref[j, row] = gi[j]
                    dr[j], di[j] = dr[j] + gr[j] * pr + gi[j] * pi, di[j] + gi[j] * pr - gr[j] * pi
            return tuple(gr + gi + dr + di)

        init = tuple(slab(ref, part, j) for ref in (carry_ref, da_ref) for part in range(2) for j in range(npair))
        c = lax.fori_loop(0, t // SCAN_UNROLL, steps, init)
        gr, gi, dr, di = (c[q * npair:(q + 1) * npair] for q in range(4))
        for j in range(npair):
            pr = jnp.where(first, slab(hb_ref, 0, j), pr_ref[j, 0]) - hr_ref[j, last]
            pi = jnp.where(first, slab(hb_ref, 1, j), pi_ref[j, 0]) - hi_ref[j, last]
            rows = slice(8 * j, 8 * j + 8)
            da_ref[0, rows, :] = dr[j] + gr[j] * pr + gi[j] * pi
            da_ref[1, rows, :] = di[j] + gi[j] * pr - gr[j] * pi
            for part, val in enumerate((gr[j], gi[j])):
                carry_ref[part, rows, :] = val
                gout_ref[part, rows, :] = val

        for j in range(npair):
            cols8 = slice(j * 8 * LANES, (j + 1) * 8 * LANES)
            gtr_ref[:, cols8] = gr_ref[j].reshape(t, 8 * LANES).astype(gtr_ref.dtype)
            gti_ref[:, cols8] = gi_ref[j].reshape(t, 8 * LANES).astype(gti_ref.dtype)
        for k in range(nb):
            cols = slice(k * spb, (k + 1) * spb)
            lanes = slice(k * LANES, (k + 1) * LANES)
            du_ref[:, lanes] = _mxu_dot(gtr_ref[:, cols], wbrt_ref[k]) + _mxu_dot(gti_ref[:, cols], wbit_ref[k])
            dwbr_ref[k] += _mxu_dot(u_ref[:, lanes], gtr_ref[:, cols], _TN)
            dwbi_ref[k] += _mxu_dot(u_ref[:, lanes], gti_ref[:, cols], _TN)
            if with_dy:
                dcre_ref[k] += _mxu_dot(tr_ref[:, cols], dy_ref[:, lanes], _TN)
                dncim_ref[k] += _mxu_dot(ti_ref[:, cols], dy_ref[:, lanes], _TN)

    once = dict(pipeline_mode=pl.Buffered(1))
    full = lambda a: pl.BlockSpec(a.shape, lambda i: (0,) * a.ndim, **once)
    row = lambda cb: pl.BlockSpec((t, w), lambda i: (tile(i), cb))
    hspec = pl.BlockSpec((npair, t, 8, LANES), lambda i: (0, tile(i), 0, 0))
    if rev:
        pspec = pl.BlockSpec((npair, 1, 8, LANES), lambda i: (0, jnp.minimum((tile(i) + 1) * t, l - 1), 0, 0))
    else:
        pspec = pl.BlockSpec((npair, 1, 8, LANES), lambda i: (0, jnp.maximum(tile(i) * t - 1, 0), 0, 0))
    sm = jax.ShapeDtypeStruct((2, nsr, LANES), F32)
    smspec = pl.BlockSpec((2, nsr, LANES), lambda i: (0, 0, 0))
    wsh = jax.ShapeDtypeStruct((nb, LANES, spb), F32)
    csh = jax.ShapeDtypeStruct((nb, spb, LANES), F32)
    tspec = pl.BlockSpec((t, nsr * LANES), lambda i: (tile(i), 0))
    in_specs = (([row(0)] if with_dy else []) + [row(u_col), hspec, hspec, pspec, pspec] + ([tspec, tspec] if with_dy else [])
                + [full(hbound), full(g_in)]
                + [_dir_spec(a, dr, **once) for a in (a_sm, wbr_t, wbi_t) + ((cre_t, ncim_t) if with_dy else ())])
    args = (([dys] if with_dy else []) + [u_arr, hr, hi, hr, hi] + ([tr, ti] if with_dy else [])
            + [hbound, g_in, a_sm, wbr_t, wbi_t] + ([cre_t, ncim_t] if with_dy else []))
    out_shape = (jax.ShapeDtypeStruct((l, w), F32), wsh, wsh) + ((csh, csh) if with_dy else ()) + (sm, sm)
    out_specs = (row(0), full(wsh), full(wsh)) + ((full(csh), full(csh)) if with_dy else ()) + (smspec, smspec)
    return pl.pallas_call(
        body, name=name, out_shape=out_shape, grid=(n,), in_specs=in_specs, out_specs=out_specs,
        scratch_shapes=[pltpu.VMEM((npair, t, 8, LANES), F32)] * 2 + [pltpu.VMEM((t, nsr * LANES), MXU_DTYPE)] * 2
        + [pltpu.VMEM((2, nsr, LANES), F32)],
        compiler_params=_cparams("arbitrary"))(*args)


def _glu_fwd(y0, y1, proj, cat, d_skip, w_glu, b_glu, w):
    l = y0.shape[0]
    tl = _tile(l, 512)

    def body(y0_ref, y1_ref, u_ref, z_ref, cat_in, d_ref, wg_ref, bg_ref, ys_ref, cat_ref):
        del cat_in
        ys = y0_ref[...] + y1_ref[...] + d_ref[...] * u_ref[...]
        ys_ref[...] = ys
        gy = _gelu(ys)
        s = _mxu_dot(gy, wg_ref[...]) + bg_ref[...]
        cat_ref[...] = (gy * jax.nn.sigmoid(s) * jax.nn.silu(z_ref[...])).astype(cat_ref.dtype)

    row = pl.BlockSpec((tl, w), lambda i: (i, 0))
    vec = pl.BlockSpec((1, w), lambda i: (0, 0))
    return pl.pallas_call(
        body, name="glu_fwd", out_shape=(jax.ShapeDtypeStruct((l, w), F32), jax.ShapeDtypeStruct(cat.shape, cat.dtype)),
        grid=(l // tl,),
        in_specs=[row, row, pl.BlockSpec((tl, w), lambda i: (i, 3)), pl.BlockSpec((tl, w), lambda i: (i, 4)), HBM,
                  vec, pl.BlockSpec((w, w), lambda i: (0, 0), pipeline_mode=pl.Buffered(1)), vec],
        out_specs=(row, pl.BlockSpec((tl, w), lambda i: (i, 1))), input_output_aliases={4: 1},
        compiler_params=_cparams("parallel"))(y0, y1, proj, proj, cat, d_skip, w_glu, b_glu)


def _glu_bwd(dcat, ys, proj, w_glu, b_glu, w, dep):
    l = ys.shape[0]
    tl = _tile(l, 512)

    def body(dy_ref, ys_ref, u_ref, z_ref, wg_ref, bg_ref, dep_ref, dys_ref, dp_ref, dbg_ref, dd_ref, dwg_ref):
        del dep_ref
        i = pl.program_id(0)
        ys_t = ys_ref[...]
        z = z_ref[...]
        dyb = dy_ref[...].astype(F32)
        gy = _gelu(ys_t)
        sg = jax.nn.sigmoid(_mxu_dot(gy, wg_ref[...]) + bg_ref[...])
        dp_ref[...] = (dyb * gy * sg * _silu_grad(z)).astype(dp_ref.dtype)
        dglu = dyb * jax.nn.silu(z)
        ds = dglu * gy * sg * (1.0 - sg)
        dgy = dglu * sg + _mxu_dot(ds, wg_ref[...], _NT)
        dys_t = dgy * _gelu_grad(ys_t)
        dys_ref[...] = dys_t

        @pl.when(i == 0)
        def _():
            for r in (dbg_ref, dd_ref, dwg_ref):
                r[...] = jnp.zeros_like(r)

        dbg_ref[...] += jnp.sum(ds, axis=0, keepdims=True)
        dd_ref[...] += jnp.sum(dys_t * u_ref[...], axis=0, keepdims=True)
        dwg_ref[...] += _mxu_dot(gy, ds, _TN)

    row = pl.BlockSpec((tl, w), lambda i: (i, 0))
    vec = pl.BlockSpec((1, w), lambda i: (0, 0))
    mat = pl.BlockSpec((w, w), lambda i: (0, 0), pipeline_mode=pl.Buffered(1))
    v = jax.ShapeDtypeStruct((1, w), F32)
    return pl.pallas_call(
        body, name="glu_bwd",
        out_shape=(jax.ShapeDtypeStruct((l, w), F32), jax.ShapeDtypeStruct((l, 5 * w), MXU_DTYPE), v, v,
                   jax.ShapeDtypeStruct((w, w), F32)),
        grid=(l // tl,),
        in_specs=[pl.BlockSpec((tl, w), lambda i: (i, 1)), row, pl.BlockSpec((tl, w), lambda i: (i, 3)),
                  pl.BlockSpec((tl, w), lambda i: (i, 4)), mat, vec, HBM],
        out_specs=(row, pl.BlockSpec((tl, w), lambda i: (i, 4)), vec, vec, mat),
        compiler_params=_cparams("arbitrary"))(dcat, ys, proj, proj, w_glu, b_glu, dep)


def _add2(a, b, name):
    l, w = a.shape
    tl = _tile(l, 512)

    def body(a_ref, b_ref, o_ref):
        o_ref[...] = a_ref[...] + b_ref[...]

    row = pl.BlockSpec((tl, w), lambda i: (i, 0))
    return pl.pallas_call(body, name=name, out_shape=jax.ShapeDtypeStruct((l, w), F32), grid=(l // tl,),
                          in_specs=[row, row], out_specs=row, compiler_params=_cparams("parallel"))(a, b)


def _adamw_nd(w, m, v, g, name):
    shape = w.shape
    lead = math.prod(shape[:-2]) if len(shape) > 2 else 1
    b, c = (shape[-2], shape[-1]) if len(shape) >= 2 else (1, shape[-1])
    t3 = (lead, b, c)
    padded_row = -(-b // 8) * 8 * -(-c // LANES) * LANES * 4
    ta = _tile(lead, max(1, (2 << 20) // padded_row))

    def body(w_ref, m_ref, v_ref, g_ref, d_ref, mo_ref, vo_ref):
        d_ref[...], mo_ref[...], vo_ref[...] = _adamw_step(w_ref[...], m_ref[...], v_ref[...], g_ref[...])

    blk = pl.BlockSpec((ta, b, c), lambda i: (i, 0, 0))
    s = jax.ShapeDtypeStruct(t3, F32)
    outs = pl.pallas_call(body, name=name, out_shape=(s, s, s), grid=(lead // ta,), in_specs=[blk] * 4, out_specs=(blk,) * 3,
                          compiler_params=_cparams("parallel"))(*[a.reshape(t3) for a in (w, m, v, g)])
    return tuple(o.reshape(shape) for o in outs)


def _adamw_step(w, m, v, g):
    mn = ADAM_B1 * m + (1.0 - ADAM_B1) * g
    vn = ADAM_B2 * v + (1.0 - ADAM_B2) * (g * g)
    m_hat = mn / (1.0 - ADAM_B1 ** ADAM_STEP)
    v_hat = vn / (1.0 - ADAM_B2 ** ADAM_STEP)
    return -ADAM_LR * (m_hat / (jnp.sqrt(v_hat) + ADAM_EPS) + ADAM_WD * w), mn, vn


def _adamw_many(params, name):
    def as3(a):
        s = a.shape
        lead = math.prod(s[:-2]) if len(s) > 2 else 1
        return a.reshape((lead,) + ((s[-2], s[-1]) if len(s) >= 2 else (1, s[-1])))

    flat = [as3(a) for p in params for a in p]
    n = len(params)

    def body(*refs):
        ins, outs = refs[:4 * n], refs[4 * n:]
        for q in range(n):
            w_ref, m_ref, v_ref, g_ref = ins[4 * q:4 * q + 4]
            for o, val in zip(outs[3 * q:3 * q + 3], _adamw_step(w_ref[...], m_ref[...], v_ref[...], g_ref[...])):
                o[...] = val

    out_shape = [jax.ShapeDtypeStruct(flat[4 * q].shape, F32) for q in range(n) for _ in range(3)]
    outs = pl.pallas_call(body, name=name, out_shape=out_shape, compiler_params=_cparams())(*flat)
    return [tuple(o.reshape(params[q][0].shape) for o in outs[3 * q:3 * q + 3]) for q in range(n)]


def _adamw(w, m, v, gparts, name, dep=None):
    r, c = w.shape
    np_ = gparts.shape[0]
    tr = _tile(r, max(8, (1 << 18) // c), 8)

    def body(w_ref, m_ref, v_ref, g_ref, *rest):
        go_ref, d_ref, mo_ref, vo_ref = rest[-4:]
        g = g_ref[0].astype(F32)
        for p in range(1, np_):
            g = g + g_ref[p].astype(F32)
        go_ref[...] = g
        d_ref[...], mo_ref[...], vo_ref[...] = _adamw_step(w_ref[...], m_ref[...], v_ref[...], g)

    row = pl.BlockSpec((tr, c), lambda i: (i, 0))
    s = jax.ShapeDtypeStruct((r, c), F32)
    extra = [] if dep is None else [dep]
    return pl.pallas_call(body, name=name, out_shape=(s, s, s, s), grid=(r // tr,),
                          in_specs=[row, row, row, pl.BlockSpec((np_, tr, c), lambda i: (0, i, 0))] + [HBM] * len(extra),
                          out_specs=(row, row, row, row), compiler_params=_cparams("parallel"))(w, m, v, gparts, *extra)


def _sum_slots(parts, name):
    np_, r, c = parts.shape

    def body(p_ref, o_ref):
        g = p_ref[0]
        for p in range(1, np_):
            g = g + p_ref[p]
        o_ref[...] = g

    return pl.pallas_call(body, name=name, out_shape=jax.ShapeDtypeStruct((r, c), F32), compiler_params=_cparams())(parts)


def _block_diag(x, gb):
    nd, g, a, b = x.shape
    eye = jnp.eye(gb, dtype=x.dtype)
    y = jnp.einsum("dkgab,gh->dkgahb", x.reshape(nd, g // gb, gb, a, b), eye)
    return y.reshape(nd, g // gb, gb * a, gb * b)


def _block_diag_extract(y, gb, a, b):
    nd, nbk = y.shape[:2]
    eye = jnp.eye(gb, dtype=y.dtype)
    x = jnp.einsum("dkgahb,gh->dkgab", y.reshape(nd, nbk, gb, a, gb, b), eye)
    return x.reshape(nd, nbk * gb, a, b)


def kernel(x, c, ctx, c_ctx, w_ada, b_ada, w_in, sgu_ln_g, sgu_ln_b, w_spatial, b_spatial, s5_lam_re, s5_lam_im, s5_log_step, s5_b_re, s5_b_im, s5_c_re, s5_c_im, s5_d, w_glu, b_glu, w_out, ln_g, ln_b, loss_target, m_c_ctx, m_w_ada, m_b_ada, m_w_in, m_sgu_ln_g, m_sgu_ln_b, m_w_spatial, m_b_spatial, m_s5_lam_re, m_s5_lam_im, m_s5_log_step, m_s5_b_re, m_s5_b_im, m_s5_c_re, m_s5_c_im, m_s5_d, m_w_glu, m_b_glu, m_w_out, m_ln_g, m_ln_b, v_c_ctx, v_w_ada, v_b_ada, v_w_in, v_sgu_ln_g, v_sgu_ln_b, v_w_spatial, v_b_spatial, v_s5_lam_re, v_s5_lam_im, v_s5_log_step, v_s5_b_re, v_s5_b_im, v_s5_c_re, v_s5_c_im, v_s5_d, v_w_glu, v_b_glu, v_w_out, v_ln_g, v_ln_b):
    small_names = ["c_ctx", "b_ada", "sgu_ln_g", "sgu_ln_b", "w_spatial", "b_spatial", "s5_lam_re", "s5_lam_im",
                   "s5_log_step", "s5_b_re", "s5_b_im", "s5_c_re", "s5_c_im", "s5_d", "b_glu", "ln_g", "ln_b"]
    env = dict(locals())
    x2, tgt, ctx2 = x[0], loss_target[0], ctx[0]
    l, d = x2.shape
    lc = ctx2.shape[0]
    w = d // 2
    nh = w // HEAD_DIM_A
    nd, g_s5, p_s5, c_s5 = s5_b_re.shape[1:]
    ns = g_s5 * p_s5
    nsr = ns // LANES
    gb = LANES // c_s5
    me = _index(_mesh_pos())
    ada_cols = w_ada.shape[2]

    srows = _silu_rows(c, c_ctx)
    srows_all = _all_gather(srows, 0, "gather_silu")
    s_mat = jnp.concatenate([srows_all[0::8], srows_all[1:2], jnp.zeros((7, d), F32)], axis=0)
    mod_part = _small_dot(s_mat, w_ada[0], "nn", "mod_cols")
    mod_all = _all_gather(mod_part, 1, "gather_mod") + b_ada
    hw_in, tok_a = _exchange_start(w_in[0].astype(MXU_DTYPE), 1, "gather", "start_gather_w_in", (SIBLING,) + SAME_CORE_PEERS)
    mod_all = mod_all + tok_a[0, 0]
    mod_x = lax.dynamic_slice_in_dim(mod_all, me, 1, axis=0)
    mod_c = mod_all[8:9]
    shift_x, scale_x, gate_x = mod_x[:, :d], mod_x[:, d:2 * d], mod_x[:, 2 * d:]
    shift_c, scale_c = mod_c[:, :d], mod_c[:, d:2 * d]

    lr, li = s5_lam_re[0][:, :, None, :], s5_lam_im[0][:, :, None, :]
    ls = s5_log_step[0][:, :, None, None]
    swapped = ("s5_b_re", "s5_b_im")
    for nm in swapped:
        for pre in ("", "m_", "v_"):
            env[pre + nm] = jnp.swapaxes(env[pre + nm], -1, -2)
    br_t, bi_t = env["s5_b_re"][0], env["s5_b_im"][0]
    ab_re, ab_im, bb_re, bb_im = _disc_fwd(lr, li, ls, br_t, bi_t)
    a_sm = jnp.stack([ab_re, ab_im], axis=1).reshape(nd, 2, nsr, LANES)
    wbr = _block_diag(bb_re.astype(MXU_DTYPE), gb)
    wbi = _block_diag(bb_im.astype(MXU_DTYPE), gb)
    cre_t = _block_diag(s5_c_re[0].astype(MXU_DTYPE), gb)
    ncim_t = _block_diag((-s5_c_im[0]).astype(MXU_DTYPE), gb)
    cre, ncim = jnp.swapaxes(cre_t, 2, 3), jnp.swapaxes(ncim_t, 2, 3)
    wbr_t, wbi_t = jnp.swapaxes(wbr, 2, 3), jnp.swapaxes(wbi, 2, 3)
    d_skip = s5_d

    xm = _ln_mod(x2, shift_x, scale_x, "ln_mod_x")
    cm = _ln_mod(ctx2, shift_c, scale_c, "ln_mod_ctx")
    ready = xm[:8, :LANES].astype(F32) + cm[:8, :LANES].astype(F32) + cre[0, 0, :8, :].astype(F32)
    hw_in2, tok_b = _forward_start(_exchange_wait(hw_in, ready, "wait_gather_w_in"), 1, "start_forward_w_in")
    w_in_f = _forward_wait(hw_in2, tok_b, "wait_forward_w_in")
    hw_glu, tok_c = _exchange_start(w_glu[0].astype(MXU_DTYPE), 0, "gather", "start_gather_w_glu")
    hw_out, tok_o = _exchange_start(w_out[0].astype(MXU_DTYPE), 0, "gather", "start_gather_w_out")
    proj = _matmul(xm, w_in_f, mode="nn", name="proj", dep=tok_c + tok_o)
    ub_c = _matmul(cm, w_in_f, mode="nn", name="proj_ctx", b_n0=3 * w, n=w)
    bsf = jnp.repeat(b_spatial[0].T, HEAD_DIM_A, axis=1)
    ws = w_spatial[0]
    cat = _ga_fwd(proj, sgu_ln_g, sgu_ln_b, ws, bsf, w)
    zeros_state = jnp.zeros((2, nsr, LANES), F32)
    s5c, s5l = [], []
    for dr in range(nd):
        s5c.append(_s5_fwd(ub_c, 0, w, zeros_state, a_sm, wbr, wbi, cre, ncim, dr, f"s5_fwd_ctx{dr}"))
        s5l.append(_s5_fwd(proj, 3, w, s5c[dr][5], a_sm, wbr, wbi, cre, ncim, dr, f"s5_fwd{dr}"))
    w_glu_f = _exchange_wait(hw_glu, s5l[1][0], "wait_gather_w_glu")
    ys, cat = _glu_fwd(s5l[0][0], s5l[1][0], proj, cat, d_skip, w_glu_f, b_glu, w)
    w_out_f = _exchange_wait(hw_out, ys, "wait_gather_w_out")
    out = _matmul(cat, w_out_f, mode="nn", name="out_proj")
    loss_row, dout, dx_res, dgate, dln_g, dln_b = _post_ln_loss(x2, out, gate_x, ln_g, ln_b, tgt)

    dcat = _matmul(dout, w_out_f, mode="nt", name="d_cat", out_dtype=MXU_DTYPE)
    dw_out = _matmul(cat, dout, mode="tn", name="d_w_out", out_dtype=MXU_DTYPE)
    hg_out, tok_d = _exchange_start(dw_out, 0, "a2a", "start_a2a_d_w_out")
    dys, dproj, db_glu, dd_skip, dw_glu = _glu_bwd(dcat, ys, proj, w_glu_f, b_glu, w, tok_d)
    hg_glu, tok_e = _exchange_start(dw_glu.astype(MXU_DTYPE), 0, "a2a", "start_a2a_d_w_glu")
    zeros_state = zeros_state + tok_e[0, 0]
    du_l, du_c, dwbr, dwbi, dcre, dncim, da_sm = [], [], [], [], [], [], []
    nbk, spb = w // LANES, gb * p_s5
    for dr in range(nd):
        bl = _s5_bwd(dys, proj, 3, w, *s5l[dr][1:5], s5c[dr][5], zeros_state, a_sm, wbr_t, wbi_t,
                     cre_t, ncim_t, dr, f"s5_bwd{dr}")
        bc = _s5_bwd(None, ub_c, 0, w, s5c[dr][1], s5c[dr][2], None, None, zeros_state, bl[6], a_sm, wbr_t, wbi_t,
                     None, None, dr, f"s5_bwd_ctx{dr}")
        du_l.append(bl[0])
        du_c.append(bc[0])
        dwbr.append(_add2(bl[1].reshape(nbk * LANES, spb), bc[1].reshape(nbk * LANES, spb), f"sum_dwbr{dr}"))
        dwbi.append(_add2(bl[2].reshape(nbk * LANES, spb), bc[2].reshape(nbk * LANES, spb), f"sum_dwbi{dr}"))
        dcre.append(bl[3])
        dncim.append(bl[4])
        da_sm.append(_add2(bl[5].reshape(2 * nsr, LANES), bc[3].reshape(2 * nsr, LANES), f"sum_da{dr}"))
    dub_c = _add2(du_c[0], du_c[1], "dub_ctx")
    dwbr = jnp.stack(dwbr).reshape(nd, nbk, LANES, spb)
    dwbi = jnp.stack(dwbi).reshape(nd, nbk, LANES, spb)
    dcre, dncim = jnp.stack(dcre), jnp.stack(dncim)
    da_sm = jnp.stack(da_sm).reshape(nd, 2, g_s5, p_s5)
    dproj, dsg, dsb, dws, dbsf = _ga_bwd(proj, dcat, dproj, sgu_ln_g, sgu_ln_b, ws, bsf, dys, du_l[0], du_l[1], d_skip, w)

    dbb_re = _block_diag_extract(dwbr, gb, c_s5, p_s5)
    dbb_im = _block_diag_extract(dwbi, gb, c_s5, p_s5)
    dc_re = jnp.swapaxes(_block_diag_extract(dcre, gb, p_s5, c_s5), 2, 3)
    dc_im = -jnp.swapaxes(_block_diag_extract(dncim, gb, p_s5, c_s5), 2, 3)
    dlr, dli, dls, db_re, db_im = _disc_bwd(lr, li, ls, br_t, bi_t, da_sm[:, 0:1].reshape(nd, g_s5, 1, p_s5),
                                            da_sm[:, 1:2].reshape(nd, g_s5, 1, p_s5), dbb_re, dbb_im)
    expand = (jnp.arange(w)[:, None] // HEAD_DIM_A == jnp.arange(LANES)[None, :]).astype(F32)
    db_sp = _lane_group_sum(dbsf, expand, "d_b_spatial")[:, :nh].T

    local = {"sgu_ln_g": dsg, "sgu_ln_b": dsb, "w_spatial": dws, "b_spatial": db_sp,
             "s5_lam_re": dlr, "s5_lam_im": dli, "s5_log_step": dls, "s5_b_re": db_re, "s5_b_im": db_im,
             "s5_c_re": dc_re, "s5_c_im": dc_im, "s5_d": dd_skip, "b_glu": db_glu, "ln_g": dln_g, "ln_b": dln_b}
    reduced = sorted(local, key=lambda n: -math.prod(env[n].shape))
    loss_part = (0.5 / d) * jnp.sum(loss_row)
    flat = jnp.concatenate([local[n].reshape(-1) for n in reduced] + [loss_part.reshape(1)])
    unit = N_DEV * 8 * LANES
    total = -(-flat.shape[0] // unit) * unit
    flat = jnp.pad(flat, (0, total - flat.shape[0])).reshape(N_DEV * 8, total // (N_DEV * 8))
    h_small, tok_s = _exchange_start(flat, 0, "a2a", "start_a2a_small")

    dw_in = _matmul(xm, dproj, mode="tn", name="d_w_in", out_dtype=MXU_DTYPE, dep=tok_s)
    dw_in = _matmul(cm, dub_c, mode="tn", name="d_w_in_ctx", acc_in=dw_in, acc_n0=3 * w, out_dtype=MXU_DTYPE)
    hg_in, tok_f = _exchange_start(dw_in, 1, "a2a", "start_a2a_d_w_in")
    mine = _sum_slots(_exchange_wait(h_small, dw_in, "wait_a2a_small"), "sum_small")
    h_sums, tok_g = _exchange_start(mine, 0, "gather", "start_gather_small")
    dxm = _matmul(dproj, w_in_f, mode="nt", name="d_xm", dep=tok_f + tok_g, out_dtype=MXU_DTYPE)
    dcm = _matmul(dub_c, w_in_f, mode="nt", name="d_cm", b_k0=3 * w, k=w)
    grad_x, dshift_x, dscale_x = _ln_mod_bwd(x2, dxm, scale_x, dx_res, "ln_mod_x_bwd")
    _, dshift_c, dscale_c = _ln_mod_bwd(ctx2, dcm, scale_c, None, "ln_mod_ctx_bwd")

    dmod_rows = jnp.concatenate([jnp.concatenate([dshift_x, dscale_x, dgate], axis=1),
                                 jnp.concatenate([dshift_c, dscale_c, jnp.zeros((1, d), F32)], axis=1),
                                 jnp.zeros((6, 3 * d), F32)], axis=0)
    h_dmod, tok_m = _exchange_start(dmod_rows, 0, "gather", "start_gather_dmod")

    gp_w_out = _exchange_wait(hg_out, tok_m, "wait_a2a_d_w_out")
    gp_w_glu = _exchange_wait(hg_glu, tok_m, "wait_a2a_d_w_glu")
    gp_w_in = _exchange_wait(hg_in, tok_m, "wait_a2a_d_w_in")
    big = {
        "w_in": _adamw(w_in[0], m_w_in[0], v_w_in[0], gp_w_in, "adamw_w_in"),
        "w_glu": _adamw(w_glu[0], m_w_glu[0], v_w_glu[0], gp_w_glu, "adamw_w_glu"),
        "w_out": _adamw(w_out[0], m_w_out[0], v_w_out[0], gp_w_out, "adamw_w_out"),
    }
    dmod_all = _exchange_wait(h_dmod, big["w_out"][0], "wait_gather_dmod")
    dmod_ctx = _sum_slots(dmod_all[1::8].reshape(N_DEV, 1, 3 * d), "sum_dmod_ctx")
    dmod_mat = jnp.concatenate([dmod_all[0::8], dmod_ctx, jnp.zeros((7, 3 * d), F32)], axis=0)
    db_ada = _sum_slots(dmod_mat[:9].reshape(9, 1, 3 * d), "sum_db_ada")
    dmod_mine = lax.dynamic_slice_in_dim(dmod_mat, me * ada_cols, ada_cols, axis=1)
    dw_ada = _small_dot(s_mat, dmod_mine, "tn", "d_w_ada")
    dsilu_cc = _small_dot(dmod_mine[8:16], w_ada[0], "nt", "d_silu_cctx")[0:1]
    dc_ctx_part = dsilu_cc * _silu_grad(c_ctx.reshape(1, d))
    dc_ctx_rows = jnp.concatenate([dc_ctx_part, jnp.zeros((7, d), F32)], axis=0)
    h_cctx, tok_c2 = _exchange_start(dc_ctx_rows, 0, "gather", "start_gather_d_c_ctx")

    big["w_ada"] = _adamw(w_ada[0], m_w_ada[0], v_w_ada[0], dw_ada[None], "adamw_w_ada", dep=tok_c2)
    summed = _exchange_wait(h_sums, big["w_ada"][0], "wait_gather_small").reshape(-1)
    grads, off = {"b_ada": db_ada}, 0
    for n in reduced:
        size = math.prod(env[n].shape)
        grads[n] = summed[off:off + size].reshape(env[n].shape)
        off += size
    loss = summed[off]
    res = {n: tuple(a[None] for a in big[n]) for n in big}

    def small_step(n):
        res[n] = (grads[n],) + _adamw_nd(env[n], env["m_" + n], env["v_" + n], grads[n], "adamw_" + n)
        if n in swapped:
            res[n] = tuple(jnp.swapaxes(a, -1, -2) for a in res[n])

    large = ("w_spatial", "s5_b_re", "s5_b_im", "s5_c_re", "s5_c_im")
    tiny = [n for n in small_names if n not in large and n != "c_ctx"]
    for n in large:
        small_step(n)
    for n, out in zip(tiny, _adamw_many([(env[n], env["m_" + n], env["v_" + n], grads[n]) for n in tiny], "adamw_tiny")):
        res[n] = (grads[n],) + out
    dc_ctx_all = _exchange_wait(h_cctx, res["w_spatial"][1], "wait_gather_d_c_ctx")
    grads["c_ctx"] = _sum_slots(dc_ctx_all[0::8].reshape(N_DEV, 1, d), "sum_d_c_ctx").reshape(d)
    small_step("c_ctx")

    order = ["c_ctx", "w_ada", "b_ada", "w_in", "sgu_ln_g", "sgu_ln_b", "w_spatial", "b_spatial", "s5_lam_re", "s5_lam_im",
             "s5_log_step", "s5_b_re", "s5_b_im", "s5_c_re", "s5_c_im", "s5_d", "w_glu", "b_glu", "w_out", "ln_g", "ln_b"]
    return (loss, grad_x[None], *[res[n][0] for n in order], *[res[n][1] for n in order],
            *[res[n][2] for n in order], *[res[n][3] for n in order])
```

```python
import functools
import math

import jax
import jax.numpy as jnp
from jax import lax
from jax.experimental import pallas as pl
from jax.experimental.pallas import tpu as pltpu

F32 = jnp.float32
MXU_DTYPE = jnp.bfloat16
N_DEV = 8
MESH_ID = pl.DeviceIdType.MESH
LN_EPS = 1e-6
DEPTH = 1
ALPHA = (2.0 * DEPTH) ** 0.25
CHUNK = 128
HEAD_DIM_A = 128
ADAM_LR, ADAM_B1, ADAM_B2, ADAM_EPS, ADAM_WD, ADAM_STEP = 0.001, 0.9, 0.999, 1e-08, 0.01, 10
LANES = 128
SCAN_UNROLL = 16
VMEM_LIMIT = 56 * 1024 * 1024
HBM = pl.BlockSpec(memory_space=pl.ANY)


def _cparams(*sem):
    return pltpu.CompilerParams(dimension_semantics=sem if sem else None, vmem_limit_bytes=VMEM_LIMIT)


def _tile(n, pref, mult=1):
    if n <= pref:
        return n
    t = pref - pref % mult
    while n % t:
        t -= mult
    return t


def _gelu(x):
    return 0.5 * x * (1.0 + lax.erf(x * (1.0 / math.sqrt(2.0))))


def _gelu_grad(x):
    return 0.5 * (1.0 + lax.erf(x * (1.0 / math.sqrt(2.0)))) + x * jnp.exp(-0.5 * x * x) * (1.0 / math.sqrt(2.0 * math.pi))


def _silu_grad(x):
    s = jax.nn.sigmoid(x)
    return s * (1.0 + x * (1.0 - s))


def _mxu_dot(a, b, dims=(((1,), (0,)), ((), ()))):
    return lax.dot_general(a.astype(MXU_DTYPE), b.astype(MXU_DTYPE), dims, preferred_element_type=F32)


_NT = (((1,), (1,)), ((), ()))
_TN = (((0,), (0,)), ((), ()))


def _mesh_pos():
    return lax.axis_index("x"), lax.axis_index("y"), lax.axis_index("c")


def _peer(pos, r):
    x, y, c = pos
    return ((1 - x) if r & 4 else x, (1 - y) if r & 2 else y, (1 - c) if r & 1 else c)


def _index(pos):
    return 4 * pos[0] + 2 * pos[1] + pos[2]


def _slice_of(ref, axis, idx, size):
    start = idx * size
    if axis == 0:
        return ref.at[pl.ds(start, size)]
    return ref.at[:, pl.ds(start, size)]


def _all_gather(x, axis, name):
    size = x.shape[axis]
    out_shape = tuple(s * N_DEV if a == axis else s for a, s in enumerate(x.shape))

    def body(x_ref, o_ref, send_sems, recv_sems, local_sem):
        me = _mesh_pos()
        mine = pltpu.make_async_copy(x_ref, _slice_of(o_ref, axis, _index(me), size), local_sem)
        mine.start()

        def copy(r, block):
            return pltpu.make_async_remote_copy(
                src_ref=x_ref, dst_ref=_slice_of(o_ref, axis, _index(block), size),
                send_sem=send_sems.at[r - 1], recv_sem=recv_sems.at[r - 1],
                device_id=_peer(me, r), device_id_type=MESH_ID)

        sends = [copy(r, me) for r in range(1, N_DEV)]
        for cp in sends:
            cp.start()
        for r in range(1, N_DEV):
            copy(r, _peer(me, r)).wait_recv()
        for cp in sends:
            cp.wait_send()
        mine.wait()

    return pl.pallas_call(
        body, name=name, out_shape=jax.ShapeDtypeStruct(out_shape, x.dtype),
        in_specs=[HBM], out_specs=HBM,
        scratch_shapes=[pltpu.SemaphoreType.DMA((N_DEV - 1,)), pltpu.SemaphoreType.DMA((N_DEV - 1,)),
                        pltpu.SemaphoreType.DMA],
    )(x)


_SEM = pl.BlockSpec(memory_space=pltpu.SEMAPHORE)
_HBM = pl.BlockSpec(memory_space=pltpu.HBM)
_EFFECT = pltpu.SideEffectType.DATAFLOW_SIDE_EFFECTING
ALL_PEERS = tuple(range(1, N_DEV))
SIBLING = 1
SAME_CORE_PEERS = (2, 4, 6)


def _exchange_copy(kind, x_ref, land_ref, axis, size, send_sems, recv_sems, me, rels, q, arriving):
    peer = _peer(me, rels[q])
    sender, receiver = (peer, me) if arriving else (me, peer)
    if kind == "gather":
        src, dst = x_ref, _slice_of(land_ref, axis, _index(sender), size)
    else:
        src, dst = _slice_of(x_ref, axis, _index(receiver), size), land_ref.at[_index(sender)]
    return pltpu.make_async_remote_copy(src_ref=src, dst_ref=dst, send_sem=send_sems.at[q], recv_sem=recv_sems.at[q],
                                        device_id=peer, device_id_type=MESH_ID)


def _local_copy(kind, x_ref, land_ref, axis, size, me, local_sem):
    if kind == "gather":
        return pltpu.make_async_copy(x_ref, _slice_of(land_ref, axis, _index(me), size), local_sem)
    return pltpu.make_async_copy(_slice_of(x_ref, axis, _index(me), size), land_ref.at[_index(me)], local_sem)


def _exchange_start(x, axis, kind, name, rels=ALL_PEERS):
    size = x.shape[axis] if kind == "gather" else x.shape[axis] // N_DEV
    if kind == "gather":
        land_shape = tuple(s * N_DEV if a == axis else s for a, s in enumerate(x.shape))
    else:
        land_shape = (N_DEV,) + tuple(size if a == axis else s for a, s in enumerate(x.shape))

    def body(x_ref, land_ref, send_sems, recv_sems, local_sem, x_thru, land_thru, token):
        del x_thru, land_thru
        me = _mesh_pos()
        _local_copy(kind, x_ref, land_ref, axis, size, me, local_sem).start()
        for q in range(len(rels)):
            _exchange_copy(kind, x_ref, land_ref, axis, size, send_sems, recv_sems, me, rels, q, False).start()
        token[...] = jnp.zeros_like(token)

    sems = pltpu.SemaphoreType.DMA((len(rels),))
    send_sems, recv_sems, local_sem, x_thru, land_thru, token = pl.pallas_call(
        body, name=name,
        out_shape=(sems, sems, pltpu.SemaphoreType.DMA(()), pltpu.HBM(x.shape, x.dtype), pltpu.HBM(land_shape, x.dtype),
                   jax.ShapeDtypeStruct((8, LANES), F32)),
        in_specs=(_HBM, _HBM), out_specs=(_SEM, _SEM, _SEM, _HBM, _HBM, pl.BlockSpec(memory_space=pltpu.VMEM)),
        input_output_aliases={0: 3, 1: 4}, compiler_params=pltpu.CompilerParams(has_side_effects=_EFFECT),
    )(pltpu.with_memory_space_constraint(x, pltpu.HBM),
      pltpu.with_memory_space_constraint(lax.empty(land_shape, x.dtype), pltpu.HBM))
    return (kind, axis, size, rels, send_sems, recv_sems, local_sem, x_thru, land_thru), token


def _exchange_wait(handle, after, name):
    kind, axis, size, rels, send_sems, recv_sems, local_sem, x_thru, land_thru = handle

    def body(x_ref, land_ref, send_sems, recv_sems, local_sem, after_ref, x_dead, got_ref):
        del after_ref, x_dead, got_ref
        me = _mesh_pos()
        _local_copy(kind, x_ref, land_ref, axis, size, me, local_sem).wait()
        for q in range(len(rels)):
            _exchange_copy(kind, x_ref, land_ref, axis, size, send_sems, recv_sems, me, rels, q, False).wait_send()
        for q in range(len(rels)):
            _exchange_copy(kind, x_ref, land_ref, axis, size, send_sems, recv_sems, me, rels, q, True).wait_recv()

    return pl.pallas_call(
        body, name=name, out_shape=(pltpu.HBM(x_thru.shape, x_thru.dtype), pltpu.HBM(land_thru.shape, land_thru.dtype)),
        in_specs=(_HBM, _HBM, _SEM, _SEM, _SEM, HBM), out_specs=(_HBM, _HBM), input_output_aliases={0: 0, 1: 1},
        compiler_params=pltpu.CompilerParams(has_side_effects=_EFFECT),
    )(x_thru, land_thru, send_sems, recv_sems, local_sem, after)[1]


def _forward_copy(land_ref, axis, size, send_sems, recv_sems, me, q, arriving):
    sibling = _peer(me, SIBLING)
    owner = _peer(sibling if arriving else me, SAME_CORE_PEERS[q])
    block = _slice_of(land_ref, axis, _index(owner), size)
    return pltpu.make_async_remote_copy(src_ref=block, dst_ref=block, send_sem=send_sems.at[q], recv_sem=recv_sems.at[q],
                                        device_id=sibling, device_id_type=MESH_ID)


def _forward_start(land, axis, name):
    size = land.shape[axis] // N_DEV

    def body(land_ref, send_sems, recv_sems, land_thru, token):
        del land_thru
        me = _mesh_pos()
        for q in range(len(SAME_CORE_PEERS)):
            _forward_copy(land_ref, axis, size, send_sems, recv_sems, me, q, False).start()
        token[...] = jnp.zeros_like(token)

    sems = pltpu.SemaphoreType.DMA((len(SAME_CORE_PEERS),))
    send_sems, recv_sems, land_thru, token = pl.pallas_call(
        body, name=name, out_shape=(sems, sems, pltpu.HBM(land.shape, land.dtype), jax.ShapeDtypeStruct((8, LANES), F32)),
        in_specs=(_HBM,), out_specs=(_SEM, _SEM, _HBM, pl.BlockSpec(memory_space=pltpu.VMEM)),
        input_output_aliases={0: 2}, compiler_params=pltpu.CompilerParams(has_side_effects=_EFFECT),
    )(land)
    return (axis, size, send_sems, recv_sems, land_thru), token


def _forward_wait(handle, after, name):
    axis, size, send_sems, recv_sems, land_thru = handle

    def body(land_ref, send_sems, recv_sems, after_ref, got_ref):
        del after_ref, got_ref
        me = _mesh_pos()
        for q in range(len(SAME_CORE_PEERS)):
            _forward_copy(land_ref, axis, size, send_sems, recv_sems, me, q, False).wait_send()
        for q in range(len(SAME_CORE_PEERS)):
            _forward_copy(land_ref, axis, size, send_sems, recv_sems, me, q, True).wait_recv()

    return pl.pallas_call(
        body, name=name, out_shape=pltpu.HBM(land_thru.shape, land_thru.dtype),
        in_specs=(_HBM, _SEM, _SEM, HBM), out_specs=_HBM, input_output_aliases={0: 0},
        compiler_params=pltpu.CompilerParams(has_side_effects=_EFFECT),
    )(land_thru, send_sems, recv_sems, after)


WHOLE = 1 << 30
MATMUL_TILES = {
    "proj": (1024, 1024, WHOLE), "proj_ctx": (256, WHOLE, WHOLE), "out_proj": (1024, 1024, WHOLE),
    "d_cat": (1024, 1024, WHOLE), "d_w_out": (1024, 1024, 2048), "d_w_in": (1024, 1280, 2048),
    "d_w_in_ctx": (1024, WHOLE, WHOLE), "d_xm": (1024, 1024, WHOLE), "d_cm": (256, 1024, WHOLE),
}


def _matmul(a, b, *, mode, name, out_dtype=F32, b_n0=0, n=None, b_k0=0, k=None, acc_in=None, acc_n0=0, dep=None):
    bm, bn, bk = MATMUL_TILES[name]
    if mode == "tn":
        kk, m = a.shape
    else:
        m, kk = a.shape
    if mode == "nn":
        n = b.shape[1] if n is None else n
    elif mode == "nt":
        n = b.shape[0]
        kk = kk if k is None else k
    else:
        n = b.shape[1]
    bm, bn, bk = _tile(m, bm), _tile(n, bn), _tile(kk, bk)
    nk = kk // bk
    assert b_n0 % bn == 0 and b_k0 % bk == 0 and acc_n0 % bn == 0
    dims = {"nn": (((1,), (0,)), ((), ())), "nt": _NT, "tn": _TN}[mode]

    n_in = 2 + (acc_in is not None) + (dep is not None)

    def body(*refs):
        a_ref, b_ref = refs[:2]
        init = refs[2] if acc_in is not None else None
        o_ref = refs[n_in]
        acc_ref = refs[-1] if nk > 1 else None
        p = _mxu_dot(a_ref[...], b_ref[...], dims)
        if nk == 1:
            o_ref[...] = (p if init is None else p + init[...]).astype(out_dtype)
            return
        ki = pl.program_id(2)

        @pl.when(ki == 0)
        def _():
            acc_ref[...] = p if init is None else p + init[...]

        @pl.when(ki > 0)
        def _():
            acc_ref[...] += p

        @pl.when(ki == nk - 1)
        def _():
            o_ref[...] = acc_ref[...].astype(out_dtype)

    a_spec = pl.BlockSpec((bk, bm), lambda j, i, q: (q, i)) if mode == "tn" else pl.BlockSpec((bm, bk), lambda j, i, q: (i, q))
    if mode == "nt":
        b_spec = pl.BlockSpec((bn, bk), lambda j, i, q: (j, q + b_k0 // bk))
    else:
        b_spec = pl.BlockSpec((bk, bn), lambda j, i, q: (q, j + b_n0 // bn))
    in_specs, args, aliases = [a_spec, b_spec], [a, b], {}
    out_map = lambda j, i, q: (i, j + acc_n0 // bn)
    if acc_in is not None:
        in_specs.append(pl.BlockSpec((bm, bn), out_map))
        args.append(acc_in)
        aliases = {2: 0}
        out_shape = jax.ShapeDtypeStruct(acc_in.shape, out_dtype)
    else:
        out_shape = jax.ShapeDtypeStruct((m, n), out_dtype)
    if dep is not None:
        in_specs.append(HBM)
        args.append(dep)
    return pl.pallas_call(
        body, name=name, out_shape=out_shape, grid=(n // bn, m // bm, nk),
        in_specs=in_specs, out_specs=pl.BlockSpec((bm, bn), out_map),
        scratch_shapes=[pltpu.VMEM((bm, bn), F32)] if nk > 1 else [],
        input_output_aliases=aliases,
        compiler_params=_cparams("parallel", "parallel", "arbitrary"),
    )(*args)


def _silu_rows(c, c_ctx):
    d = c.shape[-1]

    def body(c_ref, cc_ref, o_ref):
        o_ref[...] = jnp.zeros_like(o_ref)
        o_ref[0:1, :] = jax.nn.silu(c_ref[...])
        o_ref[1:2, :] = jax.nn.silu(cc_ref[...])

    return pl.pallas_call(body, name="silu_rows", out_shape=jax.ShapeDtypeStruct((8, d), F32))(
        c.reshape(1, d), c_ctx.reshape(1, d))


def _small_dot(a, b, mode, name):
    dims = {"nn": (((1,), (0,)), ((), ())), "nt": _NT, "tn": _TN}[mode]
    m = a.shape[1] if mode == "tn" else a.shape[0]
    n = b.shape[0] if mode == "nt" else b.shape[1]

    def body(a_ref, b_ref, o_ref):
        o_ref[...] = lax.dot_general(a_ref[...], b_ref[...], dims, preferred_element_type=F32,
                                     precision=lax.Precision.HIGHEST)

    return pl.pallas_call(body, name=name, out_shape=jax.ShapeDtypeStruct((m, n), F32),
                          compiler_params=_cparams())(a, b)


def _ln_stats(x):
    mu = jnp.mean(x, axis=-1, keepdims=True)
    xc = x - mu
    var = jnp.mean(xc * xc, axis=-1, keepdims=True)
    rstd = lax.rsqrt(var + LN_EPS)
    return xc * rstd, rstd


def _ln_mod(x, shift, scale, name):
    l, d = x.shape
    tl = _tile(l, 512)

    def body(x_ref, sh_ref, sc_ref, o_ref):
        xhat, _ = _ln_stats(x_ref[...])
        o_ref[...] = (xhat * (1.0 + sc_ref[...]) + sh_ref[...]).astype(o_ref.dtype)

    row = pl.BlockSpec((tl, d), lambda i: (i, 0))
    vec = pl.BlockSpec((1, d), lambda i: (0, 0))
    return pl.pallas_call(body, name=name, out_shape=jax.ShapeDtypeStruct((l, d), MXU_DTYPE), grid=(l // tl,),
                          in_specs=[row, vec, vec], out_specs=row, compiler_params=_cparams("parallel"))(x, shift, scale)


def _ln_mod_bwd(x, dxm, scale, res, name):
    l, d = x.shape
    tl = _tile(l, 512)
    with_res = res is not None

    def body(*refs):
        if with_res:
            x_ref, g_ref, sc_ref, r_ref, dx_ref, dsh_ref, dsc_ref = refs
        else:
            x_ref, g_ref, sc_ref, dx_ref, dsh_ref, dsc_ref = refs
        i = pl.program_id(0)
        xhat, rstd = _ln_stats(x_ref[...])
        g = g_ref[...].astype(F32)
        dxh = g * (1.0 + sc_ref[...])
        dx = rstd * (dxh - jnp.mean(dxh, axis=-1, keepdims=True) - xhat * jnp.mean(dxh * xhat, axis=-1, keepdims=True))
        dx_ref[...] = dx + r_ref[...].astype(F32) if with_res else dx

        @pl.when(i == 0)
        def _():
            dsh_ref[...] = jnp.zeros_like(dsh_ref)
            dsc_ref[...] = jnp.zeros_like(dsc_ref)

        dsh_ref[...] += jnp.sum(g, axis=0, keepdims=True)
        dsc_ref[...] += jnp.sum(g * xhat, axis=0, keepdims=True)

    row = pl.BlockSpec((tl, d), lambda i: (i, 0))
    vec = pl.BlockSpec((1, d), lambda i: (0, 0))
    args = [x, dxm, scale] + ([res] if with_res else [])
    return pl.pallas_call(
        body, name=name,
        out_shape=(jax.ShapeDtypeStruct((l, d), F32), jax.ShapeDtypeStruct((1, d), F32), jax.ShapeDtypeStruct((1, d), F32)),
        grid=(l // tl,), in_specs=[row, row, vec] + ([row] if with_res else []), out_specs=(row, vec, vec),
        compiler_params=_cparams("arbitrary"))(*args)


def _post_ln_loss(x, out, gate, ln_g, ln_b, target):
    l, d = x.shape
    tl = _tile(l, 512)

    def body(x_ref, o_ref, gate_ref, g_ref, b_ref, t_ref, loss_ref, dout_ref, dxr_ref, dgate_ref, dg_ref, db_ref):
        i = pl.program_id(0)
        out_t = o_ref[...]
        gate_v = gate_ref[...]
        rhat, rstd = _ln_stats(ALPHA * x_ref[...] + gate_v * out_t)
        ln_gv = g_ref[...]
        diff = rhat * ln_gv + b_ref[...] - t_ref[...]
        dy = diff * (1.0 / d)
        drh = dy * ln_gv
        dr = rstd * (drh - jnp.mean(drh, axis=-1, keepdims=True) - rhat * jnp.mean(drh * rhat, axis=-1, keepdims=True))
        dout_ref[...] = (gate_v * dr).astype(dout_ref.dtype)
        dxr_ref[...] = (ALPHA * dr).astype(dxr_ref.dtype)

        @pl.when(i == 0)
        def _():
            for r in (loss_ref, dgate_ref, dg_ref, db_ref):
                r[...] = jnp.zeros_like(r)

        loss_ref[...] += jnp.sum(diff * diff, axis=0, keepdims=True)
        dgate_ref[...] += jnp.sum(dr * out_t, axis=0, keepdims=True)
        dg_ref[...] += jnp.sum(dy * rhat, axis=0, keepdims=True)
        db_ref[...] += jnp.sum(dy, axis=0, keepdims=True)

    row = pl.BlockSpec((tl, d), lambda i: (i, 0))
    vec = pl.BlockSpec((1, d), lambda i: (0, 0))
    v = jax.ShapeDtypeStruct((1, d), F32)
    return pl.pallas_call(
        body, name="post_ln_loss",
        out_shape=(v, jax.ShapeDtypeStruct((l, d), MXU_DTYPE), jax.ShapeDtypeStruct((l, d), MXU_DTYPE), v, v, v),
        grid=(l // tl,), in_specs=[row, row, vec, vec, vec, row], out_specs=(vec, row, row, vec, vec, vec),
        compiler_params=_cparams("arbitrary"))(x, out, gate, ln_g, ln_b, target)


def _ga_forward_tile(p, g, b, ws_ref, bsf, w, nc, nh):
    u_raw, v_raw, za = p[:, :w], p[:, w:2 * w], p[:, 2 * w:3 * w]
    gu = _gelu(u_raw)
    vhat, rstd = _ln_stats(_gelu(v_raw))
    vn = vhat * g + b
    rows = []
    for ci in range(nc):
        r0 = ci * CHUNK
        heads = [_mxu_dot(ws_ref[h], vn[r0:r0 + CHUNK, h * HEAD_DIM_A:(h + 1) * HEAD_DIM_A]) for h in range(nh)]
        rows.append(jnp.concatenate(heads, axis=1) + bsf)
    mixed = jnp.concatenate(rows, axis=0) if nc > 1 else rows[0]
    return u_raw, v_raw, za, gu, vhat, rstd, vn, mixed


def _ga_fwd(proj, g, b, ws, bsf, w):
    l = proj.shape[0]
    nh = w // HEAD_DIM_A
    nc = _tile(l // CHUNK, 2)
    tl = nc * CHUNK

    def body(p_ref, g_ref, b_ref, ws_ref, bsf_ref, o_ref):
        _, _, za, gu, _, _, _, mixed = _ga_forward_tile(p_ref[...], g_ref[...], b_ref[...], ws_ref, bsf_ref[...], w, nc, nh)
        o_ref[...] = (gu * mixed * jax.nn.silu(za)).astype(o_ref.dtype)

    vec = pl.BlockSpec((1, w), lambda i: (0, 0))
    return pl.pallas_call(
        body, name="ga_fwd", out_shape=jax.ShapeDtypeStruct((l, 2 * w), MXU_DTYPE), grid=(l // tl,),
        in_specs=[pl.BlockSpec((tl, 3 * w), lambda i: (i, 0)), vec, vec,
                  pl.BlockSpec((nh, CHUNK, CHUNK), lambda i: (0, 0, 0)), pl.BlockSpec((CHUNK, w), lambda i: (0, 0))],
        out_specs=pl.BlockSpec((tl, w), lambda i: (i, 0)), compiler_params=_cparams("parallel"))(proj, g, b, ws, bsf)


def _ga_bwd(proj, dcat, dproj, g, b, ws, bsf, dys, du0, du1, d_skip, w):
    l = proj.shape[0]
    nh = w // HEAD_DIM_A
    nc = _tile(l // CHUNK, 2)
    tl = nc * CHUNK

    def body(p_ref, dy_ref, dp_in, g_ref, b_ref, ws_ref, bsf_ref, dys_ref, du0_ref, du1_ref, d_ref,
             dp_ref, dg_ref, db_ref, dws_ref, dbsf_ref):
        del dp_in
        i = pl.program_id(0)
        dp_ref[:, 3 * w:] = (dys_ref[...] * d_ref[...] + du0_ref[...] + du1_ref[...]).astype(dp_ref.dtype)
        gv = g_ref[...]
        u_raw, v_raw, za, gu, vhat, rstd, vn, mixed = _ga_forward_tile(
            p_ref[...], gv, b_ref[...], ws_ref, bsf_ref[...], w, nc, nh)
        dya = dy_ref[...].astype(F32)
        sz = jax.nn.silu(za)
        dmixed = dya * gu * sz
        dza = dya * gu * mixed * _silu_grad(za)
        dgu = dya * mixed * sz

        @pl.when(i == 0)
        def _():
            for r in (dg_ref, db_ref, dws_ref, dbsf_ref):
                r[...] = jnp.zeros_like(r)

        rows = []
        for ci in range(nc):
            r0 = ci * CHUNK
            heads = []
            for h in range(nh):
                cols = slice(h * HEAD_DIM_A, (h + 1) * HEAD_DIM_A)
                dm = dmixed[r0:r0 + CHUNK, cols]
                heads.append(_mxu_dot(ws_ref[h], dm, _TN))
                dws_ref[h] += _mxu_dot(dm, vn[r0:r0 + CHUNK, cols], _NT)
            rows.append(jnp.concatenate(heads, axis=1))
            dbsf_ref[...] += dmixed[r0:r0 + CHUNK, :]
        dvn = jnp.concatenate(rows, axis=0) if nc > 1 else rows[0]
        dg_ref[...] += jnp.sum(dvn * vhat, axis=0, keepdims=True)
        db_ref[...] += jnp.sum(dvn, axis=0, keepdims=True)
        dvh = dvn * gv
        dgv = rstd * (dvh - jnp.mean(dvh, axis=-1, keepdims=True) - vhat * jnp.mean(dvh * vhat, axis=-1, keepdims=True))
        dp_ref[:, :w] = (dgu * _gelu_grad(u_raw)).astype(dp_ref.dtype)
        dp_ref[:, w:2 * w] = (dgv * _gelu_grad(v_raw)).astype(dp_ref.dtype)
        dp_ref[:, 2 * w:3 * w] = dza.astype(dp_ref.dtype)

    vec = pl.BlockSpec((1, w), lambda i: (0, 0))
    row = pl.BlockSpec((tl, w), lambda i: (i, 0))
    ws_spec = pl.BlockSpec((nh, CHUNK, CHUNK), lambda i: (0, 0, 0))
    bs_spec = pl.BlockSpec((CHUNK, w), lambda i: (0, 0))
    v = jax.ShapeDtypeStruct((1, w), F32)
    return pl.pallas_call(
        body, name="ga_bwd",
        out_shape=(jax.ShapeDtypeStruct(dproj.shape, dproj.dtype), v, v, jax.ShapeDtypeStruct((nh, CHUNK, CHUNK), F32),
                   jax.ShapeDtypeStruct((CHUNK, w), F32)),
        grid=(l // tl,),
        in_specs=[pl.BlockSpec((tl, 3 * w), lambda i: (i, 0)), row, HBM, vec, vec, ws_spec, bs_spec, row, row, row, vec],
        out_specs=(pl.BlockSpec((tl, 4 * w), lambda i: (i, 0)), vec, vec, ws_spec, bs_spec),
        input_output_aliases={2: 0}, compiler_params=_cparams("arbitrary"))(
            proj, dcat, dproj, g, b, ws, bsf, dys, du0, du1, d_skip)


def _lane_group_sum(x, expand, name):
    return _small_dot(x, expand, "nn", name)


def _disc_math(lr, li, ls, br, bi):
    step = jnp.exp(ls)
    dr, di = lr * step, li * step
    mag = jnp.exp(dr)
    ab_re, ab_im = mag * jnp.cos(di), mag * jnp.sin(di)
    den = lr * lr + li * li
    nr, ni = ab_re - 1.0, ab_im
    f_re = (nr * lr + ni * li) / den
    f_im = (ni * lr - nr * li) / den
    bb_re = f_re * br - f_im * bi
    bb_im = f_re * bi + f_im * br
    return ab_re, ab_im, bb_re, bb_im


def _disc_fwd(lr, li, ls, br, bi):
    def body(lr_ref, li_ref, ls_ref, br_ref, bi_ref, o1, o2, o3, o4):
        res = _disc_math(lr_ref[...], li_ref[...], ls_ref[...], br_ref[...], bi_ref[...])
        for o, r in zip((o1, o2, o3, o4), res):
            o[...] = r

    s = lambda a: jax.ShapeDtypeStruct(a.shape, F32)
    return pl.pallas_call(body, name="s5_disc", out_shape=(s(lr), s(lr), s(br), s(br)), compiler_params=_cparams())(
        lr, li, ls, br, bi)


def _disc_bwd(lr, li, ls, br, bi, d_ar, d_ai, d_br, d_bi):
    def body(lr_ref, li_ref, ls_ref, br_ref, bi_ref, c1, c2, c3, c4, o1, o2, o3, o4, o5):
        _, vjp = jax.vjp(_disc_math, lr_ref[...], li_ref[...], ls_ref[...], br_ref[...], bi_ref[...])
        res = vjp((c1[...], c2[...], c3[...], c4[...]))
        for o, r in zip((o1, o2, o3, o4, o5), res):
            o[...] = r

    s = lambda a: jax.ShapeDtypeStruct(a.shape, F32)
    return pl.pallas_call(body, name="s5_disc_bwd", out_shape=(s(lr), s(lr), s(ls), s(br), s(br)),
                          compiler_params=_cparams())(lr, li, ls, br, bi, d_ar, d_ai, d_br, d_bi)


def _dir_spec(a, dr, **kw):
    return pl.BlockSpec((None,) + a.shape[1:], lambda i: (dr,) + (0,) * (a.ndim - 1), **kw)


def _s5_fwd(u_arr, u_col, w, h0, a_sm, wbr, wbi, cre, ncim, dr, name):
    rev = dr == 1
    l = u_arr.shape[0]
    nb = w // LANES
    spb = wbr.shape[-1]
    nsr = a_sm.shape[2]
    assert 2 * spb == 8 * LANES and nb % 2 == 0
    npair = nb // 2
    t = _tile(l, 256)
    n = l // t
    tile = (lambda i: n - 1 - i) if rev else (lambda i: i)

    def body(u_ref, h0_ref, a_ref, wbr_ref, wbi_ref, cre_ref, ncim_ref, y_ref, hr_ref, hi_ref, tr_ref, ti_ref, hfin_ref,
             carry_ref):
        i = pl.program_id(0)

        @pl.when(i == 0)
        def _():
            carry_ref[...] = h0_ref[...]

        for j in range(npair):
            for h_ref, w_ref in ((hr_ref, wbr_ref), (hi_ref, wbi_ref)):
                blk = [_mxu_dot(u_ref[:, k * LANES:(k + 1) * LANES], w_ref[k]) for k in (2 * j, 2 * j + 1)]
                h_ref[j] = jnp.concatenate(blk, axis=1).reshape(t, 8, LANES)
        slab = lambda ref, part, j: ref[part, 8 * j:8 * j + 8, :]
        ar = [slab(a_ref, 0, j) for j in range(npair)]
        ai = [slab(a_ref, 1, j) for j in range(npair)]

        def steps(blk, c):
            hr, hi = list(c[:npair]), list(c[npair:])
            for q in range(SCAN_UNROLL):
                s = blk * SCAN_UNROLL + q
                row = t - 1 - s if rev else s
                for j in range(npair):
                    hr[j], hi[j] = (ar[j] * hr[j] - ai[j] * hi[j] + hr_ref[j, row],
                                    ar[j] * hi[j] + ai[j] * hr[j] + hi_ref[j, row])
                    hr_ref[j, row] = hr[j]
                    hi_ref[j, row] = hi[j]
            return tuple(hr + hi)

        init = tuple(slab(carry_ref, part, j) for part in range(2) for j in range(npair))
        c = lax.fori_loop(0, t // SCAN_UNROLL, steps, init)
        for part in range(2):
            for j in range(npair):
                carry_ref[part, 8 * j:8 * j + 8, :] = c[part * npair + j]
                hfin_ref[part, 8 * j:8 * j + 8, :] = c[part * npair + j]
        for j in range(npair):
            cols8 = slice(j * 8 * LANES, (j + 1) * 8 * LANES)
            tr_ref[:, cols8] = hr_ref[j].reshape(t, 8 * LANES).astype(tr_ref.dtype)
            ti_ref[:, cols8] = hi_ref[j].reshape(t, 8 * LANES).astype(ti_ref.dtype)
        for k in range(nb):
            cols = slice(k * spb, (k + 1) * spb)
            y_ref[:, k * LANES:(k + 1) * LANES] = (_mxu_dot(tr_ref[:, cols], cre_ref[k]) + _mxu_dot(ti_ref[:, cols], ncim_ref[k]))

    full = lambda a: pl.BlockSpec(a.shape, lambda i: (0,) * a.ndim)
    hspec = pl.BlockSpec((npair, t, 8, LANES), lambda i: (0, tile(i), 0, 0))
    tspec = pl.BlockSpec((t, nsr * LANES), lambda i: (tile(i), 0))
    hsh = jax.ShapeDtypeStruct((npair, l, 8, LANES), F32)
    tsh = jax.ShapeDtypeStruct((l, nsr * LANES), MXU_DTYPE)
    return pl.pallas_call(
        body, name=name,
        out_shape=(jax.ShapeDtypeStruct((l, w), F32), hsh, hsh, tsh, tsh, jax.ShapeDtypeStruct((2, nsr, LANES), F32)),
        grid=(n,),
        in_specs=[pl.BlockSpec((t, w), lambda i: (tile(i), u_col)), full(h0)] + [_dir_spec(a, dr) for a in (a_sm, wbr, wbi, cre, ncim)],
        out_specs=(pl.BlockSpec((t, w), lambda i: (tile(i), 0)), hspec, hspec, tspec, tspec,
                   pl.BlockSpec((2, nsr, LANES), lambda i: (0, 0, 0))),
        scratch_shapes=[pltpu.VMEM((2, nsr, LANES), F32)],
        compiler_params=_cparams("arbitrary"))(u_arr, h0, a_sm, wbr, wbi, cre, ncim)


def _s5_bwd(dys, u_arr, u_col, w, hr, hi, tr, ti, hbound, g_in, a_sm, wbr_t, wbi_t, cre_t, ncim_t, dr, name):
    rev = dr == 1
    l = u_arr.shape[0]
    nb = w // LANES
    spb = wbr_t.shape[-2]
    nsr = a_sm.shape[2]
    npair = nb // 2
    t = _tile(l, 256 if l >= 1024 else 128)
    n = l // t
    with_dy = dys is not None
    tile = (lambda i: i) if rev else (lambda i: n - 1 - i)

    def body(*refs):
        if with_dy:
            (dy_ref, u_ref, hr_ref, hi_ref, pr_ref, pi_ref, tr_ref, ti_ref, hb_ref, gin_ref, a_ref, wbrt_ref, wbit_ref,
             cret_ref, ncimt_ref, du_ref, dwbr_ref, dwbi_ref, dcre_ref, dncim_ref, da_ref, gout_ref,
             gr_ref, gi_ref, gtr_ref, gti_ref, carry_ref) = refs
        else:
            (u_ref, hr_ref, hi_ref, pr_ref, pi_ref, hb_ref, gin_ref, a_ref, wbrt_ref, wbit_ref,
             du_ref, dwbr_ref, dwbi_ref, da_ref, gout_ref, gr_ref, gi_ref, gtr_ref, gti_ref, carry_ref) = refs
        i = pl.program_id(0)

        @pl.when(i == 0)
        def _():
            carry_ref[...] = gin_ref[...]
            accs = (dwbr_ref, dwbi_ref, da_ref) + ((dcre_ref, dncim_ref) if with_dy else ())
            for r in accs:
                r[...] = jnp.zeros_like(r)

        if with_dy:
            for j in range(npair):
                for g_ref, c_ref in ((gr_ref, cret_ref), (gi_ref, ncimt_ref)):
                    blk = [_mxu_dot(dy_ref[:, k * LANES:(k + 1) * LANES], c_ref[k]) for k in (2 * j, 2 * j + 1)]
                    g_ref[j] = jnp.concatenate(blk, axis=1).reshape(t, 8, LANES)
        else:
            gr_ref[...] = jnp.zeros_like(gr_ref)
            gi_ref[...] = jnp.zeros_like(gi_ref)
        slab = lambda ref, part, j: ref[part, 8 * j:8 * j + 8, :]
        last = t - 1 if rev else 0
        first = i == n - 1

        ar = [slab(a_ref, 0, j) for j in range(npair)]
        ai = [slab(a_ref, 1, j) for j in range(npair)]

        def steps(blk, c):
            gr, gi, dr, di = (list(c[q * npair:(q + 1) * npair]) for q in range(4))
            for q in range(SCAN_UNROLL):
                s = blk * SCAN_UNROLL + q
                row = s if rev else t - 1 - s
                prow = jnp.minimum(row + 1, t - 1) if rev else jnp.maximum(row - 1, 0)
                for j in range(npair):
                    pr, pi = hr_ref[j, prow], hi_ref[j, prow]
                    gr[j], gi[j] = (gr_ref[j, row] + ar[j] * gr[j] + ai[j] * gi[j],
                                    gi_ref[j, row] + ar[j] * gi[j] - ai[j] * gr[j])
                    gr_ref[j, row] = gr[j]
                    gi_ref[j, row] = gi[j]
                    dr[j], di[j] = dr[j] + gr[j] * pr + gi[j] * pi, di[j] + gi[j] * pr - gr[j] * pi
            return tuple(gr + gi + dr + di)

        init = tuple(slab(ref, part, j) for ref in (carry_ref, da_ref) for part in range(2) for j in range(npair))
        c = lax.fori_loop(0, t // SCAN_UNROLL, steps, init)
        gr, gi, dr, di = (c[q * npair:(q + 1) * npair] for q in range(4))
        for j in range(npair):
            pr = jnp.where(first, slab(hb_ref, 0, j), pr_ref[j, 0]) - hr_ref[j, last]
            pi = jnp.where(first, slab(hb_ref, 1, j), pi_ref[j, 0]) - hi_ref[j, last]
            rows = slice(8 * j, 8 * j + 8)
            da_ref[0, rows, :] = dr[j] + gr[j] * pr + gi[j] * pi
            da_ref[1, rows, :] = di[j] + gi[j] * pr - gr[j] * pi
            for part, val in enumerate((gr[j], gi[j])):
                carry_ref[part, rows, :] = val
                gout_ref[part, rows, :] = val

        for j in range(npair):
            cols8 = slice(j * 8 * LANES, (j + 1) * 8 * LANES)
            gtr_ref[:, cols8] = gr_ref[j].reshape(t, 8 * LANES).astype(gtr_ref.dtype)
            gti_ref[:, cols8] = gi_ref[j].reshape(t, 8 * LANES).astype(gti_ref.dtype)
        for k in range(nb):
            cols = slice(k * spb, (k + 1) * spb)
            lanes = slice(k * LANES, (k + 1) * LANES)
            du_ref[:, lanes] = _mxu_dot(gtr_ref[:, cols], wbrt_ref[k]) + _mxu_dot(gti_ref[:, cols], wbit_ref[k])
            dwbr_ref[k] += _mxu_dot(u_ref[:, lanes], gtr_ref[:, cols], _TN)
            dwbi_ref[k] += _mxu_dot(u_ref[:, lanes], gti_ref[:, cols], _TN)
            if with_dy:
                dcre_ref[k] += _mxu_dot(tr_ref[:, cols], dy_ref[:, lanes], _TN)
                dncim_ref[k] += _mxu_dot(ti_ref[:, cols], dy_ref[:, lanes], _TN)

    once = dict(pipeline_mode=pl.Buffered(1))
    full = lambda a: pl.BlockSpec(a.shape, lambda i: (0,) * a.ndim, **once)
    row = lambda cb: pl.BlockSpec((t, w), lambda i: (tile(i), cb))
    hspec = pl.BlockSpec((npair, t, 8, LANES), lambda i: (0, tile(i), 0, 0))
    if rev:
        pspec = pl.BlockSpec((npair, 1, 8, LANES), lambda i: (0, jnp.minimum((tile(i) + 1) * t, l - 1), 0, 0))
    else:
        pspec = pl.BlockSpec((npair, 1, 8, LANES), lambda i: (0, jnp.maximum(tile(i) * t - 1, 0), 0, 0))
    sm = jax.ShapeDtypeStruct((2, nsr, LANES), F32)
    smspec = pl.BlockSpec((2, nsr, LANES), lambda i: (0, 0, 0))
    wsh = jax.ShapeDtypeStruct((nb, LANES, spb), F32)
    csh = jax.ShapeDtypeStruct((nb, spb, LANES), F32)
    tspec = pl.BlockSpec((t, nsr * LANES), lambda i: (tile(i), 0))
    in_specs = (([row(0)] if with_dy else []) + [row(u_col), hspec, hspec, pspec, pspec] + ([tspec, tspec] if with_dy else [])
                + [full(hbound), full(g_in)]
                + [_dir_spec(a, dr, **once) for a in (a_sm, wbr_t, wbi_t) + ((cre_t, ncim_t) if with_dy else ())])
    args = (([dys] if with_dy else []) + [u_arr, hr, hi, hr, hi] + ([tr, ti] if with_dy else [])
            + [hbound, g_in, a_sm, wbr_t, wbi_t] + ([cre_t, ncim_t] if with_dy else []))
    out_shape = (jax.ShapeDtypeStruct((l, w), F32), wsh, wsh) + ((csh, csh) if with_dy else ()) + (sm, sm)
    out_specs = (row(0), full(wsh), full(wsh)) + ((full(csh), full(csh)) if with_dy else ()) + (smspec, smspec)
    return pl.pallas_call(
        body, name=name, out_shape=out_shape, grid=(n,), in_specs=in_specs, out_specs=out_specs,
        scratch_shapes=[pltpu.VMEM((npair, t, 8, LANES), F32)] * 2 + [pltpu.VMEM((t, nsr * LANES), MXU_DTYPE)] * 2
        + [pltpu.VMEM((2, nsr, LANES), F32)],
        compiler_params=_cparams("arbitrary"))(*args)


def _glu_fwd(y0, y1, proj, cat, d_skip, w_glu, b_glu, w):
    l = y0.shape[0]
    tl = _tile(l, 512)

    def body(y0_ref, y1_ref, u_ref, z_ref, cat_in, d_ref, wg_ref, bg_ref, ys_ref, cat_ref):
        del cat_in
        ys = y0_ref[...] + y1_ref[...] + d_ref[...] * u_ref[...]
        ys_ref[...] = ys
        gy = _gelu(ys)
        s = _mxu_dot(gy, wg_ref[...]) + bg_ref[...]
        cat_ref[...] = (gy * jax.nn.sigmoid(s) * jax.nn.silu(z_ref[...])).astype(cat_ref.dtype)

    row = pl.BlockSpec((tl, w), lambda i: (i, 0))
    vec = pl.BlockSpec((1, w), lambda i: (0, 0))
    return pl.pallas_call(
        body, name="glu_fwd", out_shape=(jax.ShapeDtypeStruct((l, w), F32), jax.ShapeDtypeStruct(cat.shape, cat.dtype)),
        grid=(l // tl,),
        in_specs=[row, row, pl.BlockSpec((tl, w), lambda i: (i, 3)), pl.BlockSpec((tl, w), lambda i: (i, 4)), HBM,
                  vec, pl.BlockSpec((w, w), lambda i: (0, 0), pipeline_mode=pl.Buffered(1)), vec],
        out_specs=(row, pl.BlockSpec((tl, w), lambda i: (i, 1))), input_output_aliases={4: 1},
        compiler_params=_cparams("parallel"))(y0, y1, proj, proj, cat, d_skip, w_glu, b_glu)


def _glu_bwd(dcat, ys, proj, w_glu, b_glu, w, dep):
    l = ys.shape[0]
    tl = _tile(l, 512)

    def body(dy_ref, ys_ref, u_ref, z_ref, wg_ref, bg_ref, dep_ref, dys_ref, dp_ref, dbg_ref, dd_ref, dwg_ref):
        del dep_ref
        i = pl.program_id(0)
        ys_t = ys_ref[...]
        z = z_ref[...]
        dyb = dy_ref[...].astype(F32)
        gy = _gelu(ys_t)
        sg = jax.nn.sigmoid(_mxu_dot(gy, wg_ref[...]) + bg_ref[...])
        dp_ref[...] = (dyb * gy * sg * _silu_grad(z)).astype(dp_ref.dtype)
        dglu = dyb * jax.nn.silu(z)
        ds = dglu * gy * sg * (1.0 - sg)
        dgy = dglu * sg + _mxu_dot(ds, wg_ref[...], _NT)
        dys_t = dgy * _gelu_grad(ys_t)
        dys_ref[...] = dys_t

        @pl.when(i == 0)
        def _():
            for r in (dbg_ref, dd_ref, dwg_ref):
                r[...] = jnp.zeros_like(r)

        dbg_ref[...] += jnp.sum(ds, axis=0, keepdims=True)
        dd_ref[...] += jnp.sum(dys_t * u_ref[...], axis=0, keepdims=True)
        dwg_ref[...] += _mxu_dot(gy, ds, _TN)

    row = pl.BlockSpec((tl, w), lambda i: (i, 0))
    vec = pl.BlockSpec((1, w), lambda i: (0, 0))
    mat = pl.BlockSpec((w, w), lambda i: (0, 0), pipeline_mode=pl.Buffered(1))
    v = jax.ShapeDtypeStruct((1, w), F32)
    return pl.pallas_call(
        body, name="glu_bwd",
        out_shape=(jax.ShapeDtypeStruct((l, w), F32), jax.ShapeDtypeStruct((l, 5 * w), MXU_DTYPE), v, v,
                   jax.ShapeDtypeStruct((w, w), F32)),
        grid=(l // tl,),
        in_specs=[pl.BlockSpec((tl, w), lambda i: (i, 1)), row, pl.BlockSpec((tl, w), lambda i: (i, 3)),
                  pl.BlockSpec((tl, w), lambda i: (i, 4)), mat, vec, HBM],
        out_specs=(row, pl.BlockSpec((tl, w), lambda i: (i, 4)), vec, vec, mat),
        compiler_params=_cparams("arbitrary"))(dcat, ys, proj, proj, w_glu, b_glu, dep)


def _add2(a, b, name):
    l, w = a.shape
    tl = _tile(l, 512)

    def body(a_ref, b_ref, o_ref):
        o_ref[...] = a_ref[...] + b_ref[...]

    row = pl.BlockSpec((tl, w), lambda i: (i, 0))
    return pl.pallas_call(body, name=name, out_shape=jax.ShapeDtypeStruct((l, w), F32), grid=(l // tl,),
                          in_specs=[row, row], out_specs=row, compiler_params=_cparams("parallel"))(a, b)


def _adamw_nd(w, m, v, g, name):
    shape = w.shape
    lead = math.prod(shape[:-2]) if len(shape) > 2 else 1
    b, c = (shape[-2], shape[-1]) if len(shape) >= 2 else (1, shape[-1])
    t3 = (lead, b, c)
    padded_row = -(-b // 8) * 8 * -(-c // LANES) * LANES * 4
    ta = _tile(lead, max(1, (2 << 20) // padded_row))

    def body(w_ref, m_ref, v_ref, g_ref, d_ref, mo_ref, vo_ref):
        d_ref[...], mo_ref[...], vo_ref[...] = _adamw_step(w_ref[...], m_ref[...], v_ref[...], g_ref[...])

    blk = pl.BlockSpec((ta, b, c), lambda i: (i, 0, 0))
    s = jax.ShapeDtypeStruct(t3, F32)
    outs = pl.pallas_call(body, name=name, out_shape=(s, s, s), grid=(lead // ta,), in_specs=[blk] * 4, out_specs=(blk,) * 3,
                          compiler_params=_cparams("parallel"))(*[a.reshape(t3) for a in (w, m, v, g)])
    return tuple(o.reshape(shape) for o in outs)


def _adamw_step(w, m, v, g):
    mn = ADAM_B1 * m + (1.0 - ADAM_B1) * g
    vn = ADAM_B2 * v + (1.0 - ADAM_B2) * (g * g)
    m_hat = mn / (1.0 - ADAM_B1 ** ADAM_STEP)
    v_hat = vn / (1.0 - ADAM_B2 ** ADAM_STEP)
    return -ADAM_LR * (m_hat / (jnp.sqrt(v_hat) + ADAM_EPS) + ADAM_WD * w), mn, vn


def _adamw_many(params, name):
    def as3(a):
        s = a.shape
        lead = math.prod(s[:-2]) if len(s) > 2 else 1
        return a.reshape((lead,) + ((s[-2], s[-1]) if len(s) >= 2 else (1, s[-1])))

    flat = [as3(a) for p in params for a in p]
    n = len(params)

    def body(*refs):
        ins, outs = refs[:4 * n], refs[4 * n:]
        for q in range(n):
            w_ref, m_ref, v_ref, g_ref = ins[4 * q:4 * q + 4]
            for o, val in zip(outs[3 * q:3 * q + 3], _adamw_step(w_ref[...], m_ref[...], v_ref[...], g_ref[...])):
                o[...] = val

    out_shape = [jax.ShapeDtypeStruct(flat[4 * q].shape, F32) for q in range(n) for _ in range(3)]
    outs = pl.pallas_call(body, name=name, out_shape=out_shape, compiler_params=_cparams())(*flat)
    return [tuple(o.reshape(params[q][0].shape) for o in outs[3 * q:3 * q + 3]) for q in range(n)]


def _adamw(w, m, v, gparts, name, dep=None):
    r, c = w.shape
    np_ = gparts.shape[0]
    tr = _tile(r, max(8, (1 << 18) // c), 8)

    def body(w_ref, m_ref, v_ref, g_ref, *rest):
        go_ref, d_ref, mo_ref, vo_ref = rest[-4:]
        g = g_ref[0].astype(F32)
        for p in range(1, np_):
            g = g + g_ref[p].astype(F32)
        go_ref[...] = g
        d_ref[...], mo_ref[...], vo_ref[...] = _adamw_step(w_ref[...], m_ref[...], v_ref[...], g)

    row = pl.BlockSpec((tr, c), lambda i: (i, 0))
    s = jax.ShapeDtypeStruct((r, c), F32)
    extra = [] if dep is None else [dep]
    return pl.pallas_call(body, name=name, out_shape=(s, s, s, s), grid=(r // tr,),
                          in_specs=[row, row, row, pl.BlockSpec((np_, tr, c), lambda i: (0, i, 0))] + [HBM] * len(extra),
                          out_specs=(row, row, row, row), compiler_params=_cparams("parallel"))(w, m, v, gparts, *extra)


def _sum_slots(parts, name):
    np_, r, c = parts.shape

    def body(p_ref, o_ref):
        g = p_ref[0]
        for p in range(1, np_):
            g = g + p_ref[p]
        o_ref[...] = g

    return pl.pallas_call(body, name=name, out_shape=jax.ShapeDtypeStruct((r, c), F32), compiler_params=_cparams())(parts)


def _block_diag(x, gb):
    nd, g, a, b = x.shape
    eye = jnp.eye(gb, dtype=x.dtype)
    y = jnp.einsum("dkgab,gh->dkgahb", x.reshape(nd, g // gb, gb, a, b), eye)
    return y.reshape(nd, g // gb, gb * a, gb * b)


def _block_diag_extract(y, gb, a, b):
    nd, nbk = y.shape[:2]
    eye = jnp.eye(gb, dtype=y.dtype)
    x = jnp.einsum("dkgahb,gh->dkgab", y.reshape(nd, nbk, gb, a, gb, b), eye)
    return x.reshape(nd, nbk * gb, a, b)


def kernel(x, c, ctx, c_ctx, w_ada, b_ada, w_in, sgu_ln_g, sgu_ln_b, w_spatial, b_spatial, s5_lam_re, s5_lam_im, s5_log_step, s5_b_re, s5_b_im, s5_c_re, s5_c_im, s5_d, w_glu, b_glu, w_out, ln_g, ln_b, loss_target, m_c_ctx, m_w_ada, m_b_ada, m_w_in, m_sgu_ln_g, m_sgu_ln_b, m_w_spatial, m_b_spatial, m_s5_lam_re, m_s5_lam_im, m_s5_log_step, m_s5_b_re, m_s5_b_im, m_s5_c_re, m_s5_c_im, m_s5_d, m_w_glu, m_b_glu, m_w_out, m_ln_g, m_ln_b, v_c_ctx, v_w_ada, v_b_ada, v_w_in, v_sgu_ln_g, v_sgu_ln_b, v_w_spatial, v_b_spatial, v_s5_lam_re, v_s5_lam_im, v_s5_log_step, v_s5_b_re, v_s5_b_im, v_s5_c_re, v_s5_c_im, v_s5_d, v_w_glu, v_b_glu, v_w_out, v_ln_g, v_ln_b):
    small_names = ["c_ctx", "b_ada", "sgu_ln_g", "sgu_ln_b", "w_spatial", "b_spatial", "s5_lam_re", "s5_lam_im",
                   "s5_log_step", "s5_b_re", "s5_b_im", "s5_c_re", "s5_c_im", "s5_d", "b_glu", "ln_g", "ln_b"]
    env = dict(locals())
    x2, tgt, ctx2 = x[0], loss_target[0], ctx[0]
    l, d = x2.shape
    lc = ctx2.shape[0]
    w = d // 2
    nh = w // HEAD_DIM_A
    nd, g_s5, p_s5, c_s5 = s5_b_re.shape[1:]
    ns = g_s5 * p_s5
    nsr = ns // LANES
    gb = LANES // c_s5
    me = _index(_mesh_pos())
    ada_cols = w_ada.shape[2]

    srows = _silu_rows(c, c_ctx)
    srows_all = _all_gather(srows, 0, "gather_silu")
    s_mat = jnp.concatenate([srows_all[0::8], srows_all[1:2], jnp.zeros((7, d), F32)], axis=0)
    mod_part = _small_dot(s_mat, w_ada[0], "nn", "mod_cols")
    mod_all = _all_gather(mod_part, 1, "gather_mod") + b_ada
    hw_in, tok_a = _exchange_start(w_in[0].astype(MXU_DTYPE), 1, "gather", "start_gather_w_in", (SIBLING,) + SAME_CORE_PEERS)
    mod_all = mod_all + tok_a[0, 0]
    mod_x = lax.dynamic_slice_in_dim(mod_all, me, 1, axis=0)
    mod_c = mod_all[8:9]
    shift_x, scale_x, gate_x = mod_x[:, :d], mod_x[:, d:2 * d], mod_x[:, 2 * d:]
    shift_c, scale_c = mod_c[:, :d], mod_c[:, d:2 * d]

    lr, li = s5_lam_re[0][:, :, None, :], s5_lam_im[0][:, :, None, :]
    ls = s5_log_step[0][:, :, None, None]
    swapped = ("s5_b_re", "s5_b_im")
    for nm in swapped:
        for pre in ("", "m_", "v_"):
            env[pre + nm] = jnp.swapaxes(env[pre + nm], -1, -2)
    br_t, bi_t = env["s5_b_re"][0], env["s5_b_im"][0]
    ab_re, ab_im, bb_re, bb_im = _disc_fwd(lr, li, ls, br_t, bi_t)
    a_sm = jnp.stack([ab_re, ab_im], axis=1).reshape(nd, 2, nsr, LANES)
    wbr = _block_diag(bb_re.astype(MXU_DTYPE), gb)
    wbi = _block_diag(bb_im.astype(MXU_DTYPE), gb)
    cre_t = _block_diag(s5_c_re[0].astype(MXU_DTYPE), gb)
    ncim_t = _block_diag((-s5_c_im[0]).astype(MXU_DTYPE), gb)
    cre, ncim = jnp.swapaxes(cre_t, 2, 3), jnp.swapaxes(ncim_t, 2, 3)
    wbr_t, wbi_t = jnp.swapaxes(wbr, 2, 3), jnp.swapaxes(wbi, 2, 3)
    d_skip = s5_d

    xm = _ln_mod(x2, shift_x, scale_x, "ln_mod_x")
    cm = _ln_mod(ctx2, shift_c, scale_c, "ln_mod_ctx")
    ready = xm[:8, :LANES].astype(F32) + cm[:8, :LANES].astype(F32) + cre[0, 0, :8, :].astype(F32)
    hw_in2, tok_b = _forward_start(_exchange_wait(hw_in, ready, "wait_gather_w_in"), 1, "start_forward_w_in")
    w_in_f = _forward_wait(hw_in2, tok_b, "wait_forward_w_in")
    hw_glu, tok_c = _exchange_start(w_glu[0].astype(MXU_DTYPE), 0, "gather", "start_gather_w_glu")
    hw_out, tok_o = _exchange_start(w_out[0].astype(MXU_DTYPE), 0, "gather", "start_gather_w_out")
    proj = _matmul(xm, w_in_f, mode="nn", name="proj", dep=tok_c + tok_o)
    ub_c = _matmul(cm, w_in_f, mode="nn", name="proj_ctx", b_n0=3 * w, n=w)
    bsf = jnp.repeat(b_spatial[0].T, HEAD_DIM_A, axis=1)
    ws = w_spatial[0]
    cat = _ga_fwd(proj, sgu_ln_g, sgu_ln_b, ws, bsf, w)
    zeros_state = jnp.zeros((2, nsr, LANES), F32)
    s5c, s5l = [], []
    for dr in range(nd):
        s5c.append(_s5_fwd(ub_c, 0, w, zeros_state, a_sm, wbr, wbi, cre, ncim, dr, f"s5_fwd_ctx{dr}"))
        s5l.append(_s5_fwd(proj, 3, w, s5c[dr][5], a_sm, wbr, wbi, cre, ncim, dr, f"s5_fwd{dr}"))
    w_glu_f = _exchange_wait(hw_glu, s5l[1][0], "wait_gather_w_glu")
    ys, cat = _glu_fwd(s5l[0][0], s5l[1][0], proj, cat, d_skip, w_glu_f, b_glu, w)
    w_out_f = _exchange_wait(hw_out, ys, "wait_gather_w_out")
    out = _matmul(cat, w_out_f, mode="nn", name="out_proj")
    loss_row, dout, dx_res, dgate, dln_g, dln_b = _post_ln_loss(x2, out, gate_x, ln_g, ln_b, tgt)

    dcat = _matmul(dout, w_out_f, mode="nt", name="d_cat", out_dtype=MXU_DTYPE)
    dw_out = _matmul(cat, dout, mode="tn", name="d_w_out", out_dtype=MXU_DTYPE)
    hg_out, tok_d = _exchange_start(dw_out, 0, "a2a", "start_a2a_d_w_out")
    dys, dproj, db_glu, dd_skip, dw_glu = _glu_bwd(dcat, ys, proj, w_glu_f, b_glu, w, tok_d)
    hg_glu, tok_e = _exchange_start(dw_glu.astype(MXU_DTYPE), 0, "a2a", "start_a2a_d_w_glu")
    zeros_state = zeros_state + tok_e[0, 0]
    du_l, du_c, dwbr, dwbi, dcre, dncim, da_sm = [], [], [], [], [], [], []
    nbk, spb = w // LANES, gb * p_s5
    for dr in range(nd):
        bl = _s5_bwd(dys, proj, 3, w, *s5l[dr][1:5], s5c[dr][5], zeros_state, a_sm, wbr_t, wbi_t,
                     cre_t, ncim_t, dr, f"s5_bwd{dr}")
        bc = _s5_bwd(None, ub_c, 0, w, s5c[dr][1], s5c[dr][2], None, None, zeros_state, bl[6], a_sm, wbr_t, wbi_t,
                     None, None, dr, f"s5_bwd_ctx{dr}")
        du_l.append(bl[0])
        du_c.append(bc[0])
        dwbr.append(_add2(bl[1].reshape(nbk * LANES, spb), bc[1].reshape(nbk * LANES, spb), f"sum_dwbr{dr}"))
        dwbi.append(_add2(bl[2].reshape(nbk * LANES, spb), bc[2].reshape(nbk * LANES, spb), f"sum_dwbi{dr}"))
        dcre.append(bl[3])
        dncim.append(bl[4])
        da_sm.append(_add2(bl[5].reshape(2 * nsr, LANES), bc[3].reshape(2 * nsr, LANES), f"sum_da{dr}"))
    dub_c = _add2(du_c[0], du_c[1], "dub_ctx")
    dwbr = jnp.stack(dwbr).reshape(nd, nbk, LANES, spb)
    dwbi = jnp.stack(dwbi).reshape(nd, nbk, LANES, spb)
    dcre, dncim = jnp.stack(dcre), jnp.stack(dncim)
    da_sm = jnp.stack(da_sm).reshape(nd, 2, g_s5, p_s5)
    dproj, dsg, dsb, dws, dbsf = _ga_bwd(proj, dcat, dproj, sgu_ln_g, sgu_ln_b, ws, bsf, dys, du_l[0], du_l[1], d_skip, w)

    dbb_re = _block_diag_extract(dwbr, gb, c_s5, p_s5)
    dbb_im = _block_diag_extract(dwbi, gb, c_s5, p_s5)
    dc_re = jnp.swapaxes(_block_diag_extract(dcre, gb, p_s5, c_s5), 2, 3)
    dc_im = -jnp.swapaxes(_block_diag_extract(dncim, gb, p_s5, c_s5), 2, 3)
    dlr, dli, dls, db_re, db_im = _disc_bwd(lr, li, ls, br_t, bi_t, da_sm[:, 0:1].reshape(nd, g_s5, 1, p_s5),
                                            da_sm[:, 1:2].reshape(nd, g_s5, 1, p_s5), dbb_re, dbb_im)
    expand = (jnp.arange(w)[:, None] // HEAD_DIM_A == jnp.arange(LANES)[None, :]).astype(F32)
    db_sp = _lane_group_sum(dbsf, expand, "d_b_spatial")[:, :nh].T

    local = {"sgu_ln_g": dsg, "sgu_ln_b": dsb, "w_spatial": dws, "b_spatial": db_sp,
             "s5_lam_re": dlr, "s5_lam_im": dli, "s5_log_step": dls, "s5_b_re": db_re, "s5_b_im": db_im,
             "s5_c_re": dc_re, "s5_c_im": dc_im, "s5_d": dd_skip, "b_glu": db_glu, "ln_g": dln_g, "ln_b": dln_b}
    reduced = sorted(local, key=lambda n: -math.prod(env[n].shape))
    loss_part = (0.5 / d) * jnp.sum(loss_row)
    flat = jnp.concatenate([local[n].reshape(-1) for n in reduced] + [loss_part.reshape(1)])
    unit = N_DEV * 8 * LANES
    total = -(-flat.shape[0] // unit) * unit
    flat = jnp.pad(flat, (0, total - flat.shape[0])).reshape(N_DEV * 8, total // (N_DEV * 8))
    h_small, tok_s = _exchange_start(flat, 0, "a2a", "start_a2a_small")

    dw_in = _matmul(xm, dproj, mode="tn", name="d_w_in", out_dtype=MXU_DTYPE, dep=tok_s)
    dw_in = _matmul(cm, dub_c, mode="tn", name="d_w_in_ctx", acc_in=dw_in, acc_n0=3 * w, out_dtype=MXU_DTYPE)
    hg_in, tok_f = _exchange_start(dw_in, 1, "a2a", "start_a2a_d_w_in")
    mine = _sum_slots(_exchange_wait(h_small, dw_in, "wait_a2a_small"), "sum_small")
    h_sums, tok_g = _exchange_start(mine, 0, "gather", "start_gather_small")
    dxm = _matmul(dproj, w_in_f, mode="nt", name="d_xm", dep=tok_f + tok_g, out_dtype=MXU_DTYPE)
    dcm = _matmul(dub_c, w_in_f, mode="nt", name="d_cm", b_k0=3 * w, k=w)
    grad_x, dshift_x, dscale_x = _ln_mod_bwd(x2, dxm, scale_x, dx_res, "ln_mod_x_bwd")
    _, dshift_c, dscale_c = _ln_mod_bwd(ctx2, dcm, scale_c, None, "ln_mod_ctx_bwd")

    dmod_rows = jnp.concatenate([jnp.concatenate([dshift_x, dscale_x, dgate], axis=1),
                                 jnp.concatenate([dshift_c, dscale_c, jnp.zeros((1, d), F32)], axis=1),
                                 jnp.zeros((6, 3 * d), F32)], axis=0)
    h_dmod, tok_m = _exchange_start(dmod_rows, 0, "gather", "start_gather_dmod")

    gp_w_out = _exchange_wait(hg_out, tok_m, "wait_a2a_d_w_out")
    gp_w_glu = _exchange_wait(hg_glu, tok_m, "wait_a2a_d_w_glu")
    gp_w_in = _exchange_wait(hg_in, tok_m, "wait_a2a_d_w_in")
    big = {
        "w_in": _adamw(w_in[0], m_w_in[0], v_w_in[0], gp_w_in, "adamw_w_in"),
        "w_glu": _adamw(w_glu[0], m_w_glu[0], v_w_glu[0], gp_w_glu, "adamw_w_glu"),
        "w_out": _adamw(w_out[0], m_w_out[0], v_w_out[0], gp_w_out, "adamw_w_out"),
    }
    dmod_all = _exchange_wait(h_dmod, big["w_out"][0], "wait_gather_dmod")
    dmod_ctx = _sum_slots(dmod_all[1::8].reshape(N_DEV, 1, 3 * d), "sum_dmod_ctx")
    dmod_mat = jnp.concatenate([dmod_all[0::8], dmod_ctx, jnp.zeros((7, 3 * d), F32)], axis=0)
    db_ada = _sum_slots(dmod_mat[:9].reshape(9, 1, 3 * d), "sum_db_ada")
    dmod_mine = lax.dynamic_slice_in_dim(dmod_mat, me * ada_cols, ada_cols, axis=1)
    dw_ada = _small_dot(s_mat, dmod_mine, "tn", "d_w_ada")
    dsilu_cc = _small_dot(dmod_mine[8:16], w_ada[0], "nt", "d_silu_cctx")[0:1]
    dc_ctx_part = dsilu_cc * _silu_grad(c_ctx.reshape(1, d))
    dc_ctx_rows = jnp.concatenate([dc_ctx_part, jnp.zeros((7, d), F32)], axis=0)
    h_cctx, tok_c2 = _exchange_start(dc_ctx_rows, 0, "gather", "start_gather_d_c_ctx")

    big["w_ada"] = _adamw(w_ada[0], m_w_ada[0], v_w_ada[0], dw_ada[None], "adamw_w_ada", dep=tok_c2)
    summed = _exchange_wait(h_sums, big["w_ada"][0], "wait_gather_small").reshape(-1)
    grads, off = {"b_ada": db_ada}, 0
    for n in reduced:
        size = math.prod(env[n].shape)
        grads[n] = summed[off:off + size].reshape(env[n].shape)
        off += size
    loss = summed[off]
    res = {n: tuple(a[None] for a in big[n]) for n in big}

    def small_step(n):
        res[n] = (grads[n],) + _adamw_nd(env[n], env["m_" + n], env["v_" + n], grads[n], "adamw_" + n)
        if n in swapped:
            res[n] = tuple(jnp.swapaxes(a, -1, -2) for a in res[n])

    large = ("w_spatial", "s5_b_re", "s5_b_im", "s5_c_re", "s5_c_im")
    tiny = [n for n in small_names if n not in large and n != "c_ctx"]
    for n in large:
        small_step(n)
    for n, out in zip(tiny, _adamw_many([(env[n], env["m_" + n], env["v_" + n], grads[n]) for n in tiny], "adamw_tiny")):
        res[n] = (grads[n],) + out
    dc_ctx_all = _exchange_wait(h_cctx, res["w_spatial"][1], "wait_gather_d_c_ctx")
    grads["c_ctx"] = _sum_slots(dc_ctx_all[0::8].reshape(N_DEV, 1, d), "sum_d_c_ctx").reshape(d)
    small_step("c_ctx")

    order = ["c_ctx", "w_ada", "b_ada", "w_in", "sgu_ln_g", "sgu_ln_b", "w_spatial", "b_spatial", "s5_lam_re", "s5_lam_im",
             "s5_log_step", "s5_b_re", "s5_b_im", "s5_c_re", "s5_c_im", "s5_d", "w_glu", "b_glu", "w_out", "ln_g", "ln_b"]
    return (loss, grad_x[None], *[res[n][0] for n in order], *[res[n][1] for n in order],
            *[res[n][2] for n in order], *[res[n][3] for n in order])
```

```python
import functools
import math

import jax
import jax.numpy as jnp
from jax import lax
from jax.experimental import pallas as pl
from jax.experimental.pallas import tpu as pltpu

F32 = jnp.float32
MXU_DTYPE = jnp.bfloat16
N_DEV = 8
MESH_ID = pl.DeviceIdType.MESH
LN_EPS = 1e-6
DEPTH = 1
ALPHA = (2.0 * DEPTH) ** 0.25
CHUNK = 128
HEAD_DIM_A = 128
ADAM_LR, ADAM_B1, ADAM_B2, ADAM_EPS, ADAM_WD, ADAM_STEP = 0.001, 0.9, 0.999, 1e-08, 0.01, 10
LANES = 128
SCAN_UNROLL = 16
VMEM_LIMIT = 56 * 1024 * 1024
HBM = pl.BlockSpec(memory_space=pl.ANY)


def _cparams(*sem):
    return pltpu.CompilerParams(dimension_semantics=sem if sem else None, vmem_limit_bytes=VMEM_LIMIT)


def _tile(n, pref, mult=1):
    if n <= pref:
        return n
    t = pref - pref % mult
    while n % t:
        t -= mult
    return t


def _gelu(x):
    return 0.5 * x * (1.0 + lax.erf(x * (1.0 / math.sqrt(2.0))))


def _gelu_grad(x):
    return 0.5 * (1.0 + lax.erf(x * (1.0 / math.sqrt(2.0)))) + x * jnp.exp(-0.5 * x * x) * (1.0 / math.sqrt(2.0 * math.pi))


def _silu_grad(x):
    s = jax.nn.sigmoid(x)
    return s * (1.0 + x * (1.0 - s))


def _mxu_dot(a, b, dims=(((1,), (0,)), ((), ()))):
    return lax.dot_general(a.astype(MXU_DTYPE), b.astype(MXU_DTYPE), dims, preferred_element_type=F32)


_NT = (((1,), (1,)), ((), ()))
_TN = (((0,), (0,)), ((), ()))


def _mesh_pos():
    return lax.axis_index("x"), lax.axis_index("y"), lax.axis_index("c")


def _peer(pos, r):
    x, y, c = pos
    return ((1 - x) if r & 4 else x, (1 - y) if r & 2 else y, (1 - c) if r & 1 else c)


def _index(pos):
    return 4 * pos[0] + 2 * pos[1] + pos[2]


def _slice_of(ref, axis, idx, size):
    start = idx * size
    if axis == 0:
        return ref.at[pl.ds(start, size)]
    return ref.at[:, pl.ds(start, size)]


def _all_gather(x, axis, name):
    size = x.shape[axis]
    out_shape = tuple(s * N_DEV if a == axis else s for a, s in enumerate(x.shape))

    def body(x_ref, o_ref, send_sems, recv_sems, local_sem):
        me = _mesh_pos()
        mine = pltpu.make_async_copy(x_ref, _slice_of(o_ref, axis, _index(me), size), local_sem)
        mine.start()

        def copy(r, block):
            return pltpu.make_async_remote_copy(
                src_ref=x_ref, dst_ref=_slice_of(o_ref, axis, _index(block), size),
                send_sem=send_sems.at[r - 1], recv_sem=recv_sems.at[r - 1],
                device_id=_peer(me, r), device_id_type=MESH_ID)

        sends = [copy(r, me) for r in range(1, N_DEV)]
        for cp in sends:
            cp.start()
        for r in range(1, N_DEV):
            copy(r, _peer(me, r)).wait_recv()
        for cp in sends:
            cp.wait_send()
        mine.wait()

    return pl.pallas_call(
        body, name=name, out_shape=jax.ShapeDtypeStruct(out_shape, x.dtype),
        in_specs=[HBM], out_specs=HBM,
        scratch_shapes=[pltpu.SemaphoreType.DMA((N_DEV - 1,)), pltpu.SemaphoreType.DMA((N_DEV - 1,)),
                        pltpu.SemaphoreType.DMA],
    )(x)


_SEM = pl.BlockSpec(memory_space=pltpu.SEMAPHORE)
_HBM = pl.BlockSpec(memory_space=pltpu.HBM)
_EFFECT = pltpu.SideEffectType.DATAFLOW_SIDE_EFFECTING
ALL_PEERS = tuple(range(1, N_DEV))
SIBLING = 1
SAME_CORE_PEERS = (2, 4, 6)


def _exchange_copy(kind, x_ref, land_ref, axis, size, send_sems, recv_sems, me, rels, q, arriving):
    peer = _peer(me, rels[q])
    sender, receiver = (peer, me) if arriving else (me, peer)
    if kind == "gather":
        src, dst = x_ref, _slice_of(land_ref, axis, _index(sender), size)
    else:
        src, dst = _slice_of(x_ref, axis, _index(receiver), size), land_ref.at[_index(sender)]
    return pltpu.make_async_remote_copy(src_ref=src, dst_ref=dst, send_sem=send_sems.at[q], recv_sem=recv_sems.at[q],
                                        device_id=peer, device_id_type=MESH_ID)


def _local_copy(kind, x_ref, land_ref, axis, size, me, local_sem):
    if kind == "gather":
        return pltpu.make_async_copy(x_ref, _slice_of(land_ref, axis, _index(me), size), local_sem)
    return pltpu.make_async_copy(_slice_of(x_ref, axis, _index(me), size), land_ref.at[_index(me)], local_sem)


def _exchange_start(x, axis, kind, name, rels=ALL_PEERS):
    size = x.shape[axis] if kind == "gather" else x.shape[axis] // N_DEV
    if kind == "gather":
        land_shape = tuple(s * N_DEV if a == axis else s for a, s in enumerate(x.shape))
    else:
        land_shape = (N_DEV,) + tuple(size if a == axis else s for a, s in enumerate(x.shape))

    def body(x_ref, land_ref, send_sems, recv_sems, local_sem, x_thru, land_thru, token):
        del x_thru, land_thru
        me = _mesh_pos()
        _local_copy(kind, x_ref, land_ref, axis, size, me, local_sem).start()
        for q in range(len(rels)):
            _exchange_copy(kind, x_ref, land_ref, axis, size, send_sems, recv_sems, me, rels, q, False).start()
        token[...] = jnp.zeros_like(token)

    sems = pltpu.SemaphoreType.DMA((len(rels),))
    send_sems, recv_sems, local_sem, x_thru, land_thru, token = pl.pallas_call(
        body, name=name,
        out_shape=(sems, sems, pltpu.SemaphoreType.DMA(()), pltpu.HBM(x.shape, x.dtype), pltpu.HBM(land_shape, x.dtype),
                   jax.ShapeDtypeStruct((8, LANES), F32)),
        in_specs=(_HBM, _HBM), out_specs=(_SEM, _SEM, _SEM, _HBM, _HBM, pl.BlockSpec(memory_space=pltpu.VMEM)),
        input_output_aliases={0: 3, 1: 4}, compiler_params=pltpu.CompilerParams(has_side_effects=_EFFECT),
    )(pltpu.with_memory_space_constraint(x, pltpu.HBM),
      pltpu.with_memory_space_constraint(lax.empty(land_shape, x.dtype), pltpu.HBM))
    return (kind, axis, size, rels, send_sems, recv_sems, local_sem, x_thru, land_thru), token


def _exchange_wait(handle, after, name):
    kind, axis, size, rels, send_sems, recv_sems, local_sem, x_thru, land_thru = handle

    def body(x_ref, land_ref, send_sems, recv_sems, local_sem, after_ref, x_dead, got_ref):
        del after_ref, x_dead, got_ref
        me = _mesh_pos()
        _local_copy(kind, x_ref, land_ref, axis, size, me, local_sem).wait()
        for q in range(len(rels)):
            _exchange_copy(kind, x_ref, land_ref, axis, size, send_sems, recv_sems, me, rels, q, False).wait_send()
        for q in range(len(rels)):
            _exchange_copy(kind, x_ref, land_ref, axis, size, send_sems, recv_sems, me, rels, q, True).wait_recv()

    return pl.pallas_call(
        body, name=name, out_shape=(pltpu.HBM(x_thru.shape, x_thru.dtype), pltpu.HBM(land_thru.shape, land_thru.dtype)),
        in_specs=(_HBM, _HBM, _SEM, _SEM, _SEM, HBM), out_specs=(_HBM, _HBM), input_output_aliases={0: 0, 1: 1},
        compiler_params=pltpu.CompilerParams(has_side_effects=_EFFECT),
    )(x_thru, land_thru, send_sems, recv_sems, local_sem, after)[1]


def _forward_copy(land_ref, axis, size, send_sems, recv_sems, me, q, arriving):
    sibling = _peer(me, SIBLING)
    owner = _peer(sibling if arriving else me, SAME_CORE_PEERS[q])
    block = _slice_of(land_ref, axis, _index(owner), size)
    return pltpu.make_async_remote_copy(src_ref=block, dst_ref=block, send_sem=send_sems.at[q], recv_sem=recv_sems.at[q],
                                        device_id=sibling, device_id_type=MESH_ID)


def _forward_start(land, axis, name):
    size = land.shape[axis] // N_DEV

    def body(land_ref, send_sems, recv_sems, land_thru, token):
        del land_thru
        me = _mesh_pos()
        for q in range(len(SAME_CORE_PEERS)):
            _forward_copy(land_ref, axis, size, send_sems, recv_sems, me, q, False).start()
        token[...] = jnp.zeros_like(token)

    sems = pltpu.SemaphoreType.DMA((len(SAME_CORE_PEERS),))
    send_sems, recv_sems, land_thru, token = pl.pallas_call(
        body, name=name, out_shape=(sems, sems, pltpu.HBM(land.shape, land.dtype), jax.ShapeDtypeStruct((8, LANES), F32)),
        in_specs=(_HBM,), out_specs=(_SEM, _SEM, _HBM, pl.BlockSpec(memory_space=pltpu.VMEM)),
        input_output_aliases={0: 2}, compiler_params=pltpu.CompilerParams(has_side_effects=_EFFECT),
    )(land)
    return (axis, size, send_sems, recv_sems, land_thru), token


def _forward_wait(handle, after, name):
    axis, size, send_sems, recv_sems, land_thru = handle

    def body(land_ref, send_sems, recv_sems, after_ref, got_ref):
        del after_ref, got_ref
        me = _mesh_pos()
        for q in range(len(SAME_CORE_PEERS)):
            _forward_copy(land_ref, axis, size, send_sems, recv_sems, me, q, False).wait_send()
        for q in range(len(SAME_CORE_PEERS)):
            _forward_copy(land_ref, axis, size, send_sems, recv_sems, me, q, True).wait_recv()

    return pl.pallas_call(
        body, name=name, out_shape=pltpu.HBM(land_thru.shape, land_thru.dtype),
        in_specs=(_HBM, _SEM, _SEM, HBM), out_specs=_HBM, input_output_aliases={0: 0},
        compiler_params=pltpu.CompilerParams(has_side_effects=_EFFECT),
    )(land_thru, send_sems, recv_sems, after)


WHOLE = 1 << 30
MATMUL_TILES = {
    "proj": (1024, 1024, WHOLE), "proj_ctx": (256, WHOLE, WHOLE), "out_proj": (1024, 2048, WHOLE),
    "d_cat": (1024, 2048, WHOLE), "d_w_out": (1024, 1024, 2048), "d_w_in": (1024, 1280, 2048),
    "d_w_in_ctx": (1024, WHOLE, WHOLE), "d_xm": (1024, 1024, WHOLE), "d_cm": (256, 1024, WHOLE),
}


def _matmul(a, b, *, mode, name, out_dtype=F32, b_n0=0, n=None, b_k0=0, k=None, acc_in=None, acc_n0=0, dep=None):
    bm, bn, bk = MATMUL_TILES[name]
    if mode == "tn":
        kk, m = a.shape
    else:
        m, kk = a.shape
    if mode == "nn":
        n = b.shape[1] if n is None else n
    elif mode == "nt":
        n = b.shape[0]
        kk = kk if k is None else k
    else:
        n = b.shape[1]
    bm, bn, bk = _tile(m, bm), _tile(n, bn), _tile(kk, bk)
    nk = kk // bk
    assert b_n0 % bn == 0 and b_k0 % bk == 0 and acc_n0 % bn == 0
    dims = {"nn": (((1,), (0,)), ((), ())), "nt": _NT, "tn": _TN}[mode]

    n_in = 2 + (acc_in is not None) + (dep is not None)

    def body(*refs):
        a_ref, b_ref = refs[:2]
        init = refs[2] if acc_in is not None else None
        o_ref = refs[n_in]
        acc_ref = refs[-1] if nk > 1 else None
        p = _mxu_dot(a_ref[...], b_ref[...], dims)
        if nk == 1:
            o_ref[...] = (p if init is None else p + init[...]).astype(out_dtype)
            return
        ki = pl.program_id(2)

        @pl.when(ki == 0)
        def _():
            acc_ref[...] = p if init is None else p + init[...]

        @pl.when(ki > 0)
        def _():
            acc_ref[...] += p

        @pl.when(ki == nk - 1)
        def _():
            o_ref[...] = acc_ref[...].astype(out_dtype)

    a_spec = pl.BlockSpec((bk, bm), lambda j, i, q: (q, i)) if mode == "tn" else pl.BlockSpec((bm, bk), lambda j, i, q: (i, q))
    if mode == "nt":
        b_spec = pl.BlockSpec((bn, bk), lambda j, i, q: (j, q + b_k0 // bk))
    else:
        b_spec = pl.BlockSpec((bk, bn), lambda j, i, q: (q, j + b_n0 // bn))
    in_specs, args, aliases = [a_spec, b_spec], [a, b], {}
    out_map = lambda j, i, q: (i, j + acc_n0 // bn)
    if acc_in is not None:
        in_specs.append(pl.BlockSpec((bm, bn), out_map))
        args.append(acc_in)
        aliases = {2: 0}
        out_shape = jax.ShapeDtypeStruct(acc_in.shape, out_dtype)
    else:
        out_shape = jax.ShapeDtypeStruct((m, n), out_dtype)
    if dep is not None:
        in_specs.append(HBM)
        args.append(dep)
    return pl.pallas_call(
        body, name=name, out_shape=out_shape, grid=(n // bn, m // bm, nk),
        in_specs=in_specs, out_specs=pl.BlockSpec((bm, bn), out_map),
        scratch_shapes=[pltpu.VMEM((bm, bn), F32)] if nk > 1 else [],
        input_output_aliases=aliases,
        compiler_params=_cparams("parallel", "parallel", "arbitrary"),
    )(*args)


def _silu_rows(c, c_ctx):
    d = c.shape[-1]

    def body(c_ref, cc_ref, o_ref):
        o_ref[...] = jnp.zeros_like(o_ref)
        o_ref[0:1, :] = jax.nn.silu(c_ref[...])
        o_ref[1:2, :] = jax.nn.silu(cc_ref[...])

    return pl.pallas_call(body, name="silu_rows", out_shape=jax.ShapeDtypeStruct((8, d), F32))(
        c.reshape(1, d), c_ctx.reshape(1, d))


def _small_dot(a, b, mode, name):
    dims = {"nn": (((1,), (0,)), ((), ())), "nt": _NT, "tn": _TN}[mode]
    m = a.shape[1] if mode == "tn" else a.shape[0]
    n = b.shape[0] if mode == "nt" else b.shape[1]

    def body(a_ref, b_ref, o_ref):
        o_ref[...] = lax.dot_general(a_ref[...], b_ref[...], dims, preferred_element_type=F32,
                                     precision=lax.Precision.HIGHEST)

    return pl.pallas_call(body, name=name, out_shape=jax.ShapeDtypeStruct((m, n), F32),
                          compiler_params=_cparams())(a, b)


def _ln_stats(x):
    mu = jnp.mean(x, axis=-1, keepdims=True)
    xc = x - mu
    var = jnp.mean(xc * xc, axis=-1, keepdims=True)
    rstd = lax.rsqrt(var + LN_EPS)
    return xc * rstd, rstd


def _ln_mod(x, shift, scale, name):
    l, d = x.shape
    tl = _tile(l, 512)

    def body(x_ref, sh_ref, sc_ref, o_ref):
        xhat, _ = _ln_stats(x_ref[...])
        o_ref[...] = (xhat * (1.0 + sc_ref[...]) + sh_ref[...]).astype(o_ref.dtype)

    row = pl.BlockSpec((tl, d), lambda i: (i, 0))
    vec = pl.BlockSpec((1, d), lambda i: (0, 0))
    return pl.pallas_call(body, name=name, out_shape=jax.ShapeDtypeStruct((l, d), MXU_DTYPE), grid=(l // tl,),
                          in_specs=[row, vec, vec], out_specs=row, compiler_params=_cparams("parallel"))(x, shift, scale)


def _ln_mod_bwd(x, dxm, scale, res, name):
    l, d = x.shape
    tl = _tile(l, 512)
    with_res = res is not None

    def body(*refs):
        if with_res:
            x_ref, g_ref, sc_ref, r_ref, dx_ref, dsh_ref, dsc_ref = refs
        else:
            x_ref, g_ref, sc_ref, dx_ref, dsh_ref, dsc_ref = refs
        i = pl.program_id(0)
        xhat, rstd = _ln_stats(x_ref[...])
        g = g_ref[...].astype(F32)
        dxh = g * (1.0 + sc_ref[...])
        dx = rstd * (dxh - jnp.mean(dxh, axis=-1, keepdims=True) - xhat * jnp.mean(dxh * xhat, axis=-1, keepdims=True))
        dx_ref[...] = dx + r_ref[...].astype(F32) if with_res else dx

        @pl.when(i == 0)
        def _():
            dsh_ref[...] = jnp.zeros_like(dsh_ref)
            dsc_ref[...] = jnp.zeros_like(dsc_ref)

        dsh_ref[...] += jnp.sum(g, axis=0, keepdims=True)
        dsc_ref[...] += jnp.sum(g * xhat, axis=0, keepdims=True)

    row = pl.BlockSpec((tl, d), lambda i: (i, 0))
    vec = pl.BlockSpec((1, d), lambda i: (0, 0))
    args = [x, dxm, scale] + ([res] if with_res else [])
    return pl.pallas_call(
        body, name=name,
        out_shape=(jax.ShapeDtypeStruct((l, d), F32), jax.ShapeDtypeStruct((1, d), F32), jax.ShapeDtypeStruct((1, d), F32)),
        grid=(l // tl,), in_specs=[row, row, vec] + ([row] if with_res else []), out_specs=(row, vec, vec),
        compiler_params=_cparams("arbitrary"))(*args)


def _post_ln_loss(x, out, gate, ln_g, ln_b, target):
    l, d = x.shape
    tl = _tile(l, 512)

    def body(x_ref, o_ref, gate_ref, g_ref, b_ref, t_ref, loss_ref, dout_ref, dxr_ref, dgate_ref, dg_ref, db_ref):
        i = pl.program_id(0)
        out_t = o_ref[...]
        gate_v = gate_ref[...]
        rhat, rstd = _ln_stats(ALPHA * x_ref[...] + gate_v * out_t)
        ln_gv = g_ref[...]
        diff = rhat * ln_gv + b_ref[...] - t_ref[...]
        dy = diff * (1.0 / d)
        drh = dy * ln_gv
        dr = rstd * (drh - jnp.mean(drh, axis=-1, keepdims=True) - rhat * jnp.mean(drh * rhat, axis=-1, keepdims=True))
        dout_ref[...] = (gate_v * dr).astype(dout_ref.dtype)
        dxr_ref[...] = (ALPHA * dr).astype(dxr_ref.dtype)

        @pl.when(i == 0)
        def _():
            for r in (loss_ref, dgate_ref, dg_ref, db_ref):
                r[...] = jnp.zeros_like(r)

        loss_ref[...] += jnp.sum(diff * diff, axis=0, keepdims=True)
        dgate_ref[...] += jnp.sum(dr * out_t, axis=0, keepdims=True)
        dg_ref[...] += jnp.sum(dy * rhat, axis=0, keepdims=True)
        db_ref[...] += jnp.sum(dy, axis=0, keepdims=True)

    row = pl.BlockSpec((tl, d), lambda i: (i, 0))
    vec = pl.BlockSpec((1, d), lambda i: (0, 0))
    v = jax.ShapeDtypeStruct((1, d), F32)
    return pl.pallas_call(
        body, name="post_ln_loss",
        out_shape=(v, jax.ShapeDtypeStruct((l, d), MXU_DTYPE), jax.ShapeDtypeStruct((l, d), MXU_DTYPE), v, v, v),
        grid=(l // tl,), in_specs=[row, row, vec, vec, vec, row], out_specs=(vec, row, row, vec, vec, vec),
        compiler_params=_cparams("arbitrary"))(x, out, gate, ln_g, ln_b, target)


def _ga_forward_tile(p, g, b, ws_ref, bsf, w, nc, nh):
    u_raw, v_raw, za = p[:, :w], p[:, w:2 * w], p[:, 2 * w:3 * w]
    gu = _gelu(u_raw)
    vhat, rstd = _ln_stats(_gelu(v_raw))
    vn = vhat * g + b
    rows = []
    for ci in range(nc):
        r0 = ci * CHUNK
        heads = [_mxu_dot(ws_ref[h], vn[r0:r0 + CHUNK, h * HEAD_DIM_A:(h + 1) * HEAD_DIM_A]) for h in range(nh)]
        rows.append(jnp.concatenate(heads, axis=1) + bsf)
    mixed = jnp.concatenate(rows, axis=0) if nc > 1 else rows[0]
    return u_raw, v_raw, za, gu, vhat, rstd, vn, mixed


def _ga_fwd(proj, g, b, ws, bsf, w):
    l = proj.shape[0]
    nh = w // HEAD_DIM_A
    nc = _tile(l // CHUNK, 2)
    tl = nc * CHUNK

    def body(p_ref, g_ref, b_ref, ws_ref, bsf_ref, o_ref):
        _, _, za, gu, _, _, _, mixed = _ga_forward_tile(p_ref[...], g_ref[...], b_ref[...], ws_ref, bsf_ref[...], w, nc, nh)
        o_ref[...] = (gu * mixed * jax.nn.silu(za)).astype(o_ref.dtype)

    vec = pl.BlockSpec((1, w), lambda i: (0, 0))
    return pl.pallas_call(
        body, name="ga_fwd", out_shape=jax.ShapeDtypeStruct((l, 2 * w), MXU_DTYPE), grid=(l // tl,),
        in_specs=[pl.BlockSpec((tl, 3 * w), lambda i: (i, 0)), vec, vec,
                  pl.BlockSpec((nh, CHUNK, CHUNK), lambda i: (0, 0, 0)), pl.BlockSpec((CHUNK, w), lambda i: (0, 0))],
        out_specs=pl.BlockSpec((tl, w), lambda i: (i, 0)), compiler_params=_cparams("parallel"))(proj, g, b, ws, bsf)


def _ga_bwd(proj, dcat, dproj, g, b, ws, bsf, dys, du0, du1, d_skip, w):
    l = proj.shape[0]
    nh = w // HEAD_DIM_A
    nc = _tile(l // CHUNK, 2)
    tl = nc * CHUNK

    def body(p_ref, dy_ref, dp_in, g_ref, b_ref, ws_ref, bsf_ref, dys_ref, du0_ref, du1_ref, d_ref,
             dp_ref, dg_ref, db_ref, dws_ref, dbsf_ref):
        del dp_in
        i = pl.program_id(0)
        dp_ref[:, 3 * w:] = (dys_ref[...] * d_ref[...] + du0_ref[...] + du1_ref[...]).astype(dp_ref.dtype)
        gv = g_ref[...]
        u_raw, v_raw, za, gu, vhat, rstd, vn, mixed = _ga_forward_tile(
            p_ref[...], gv, b_ref[...], ws_ref, bsf_ref[...], w, nc, nh)
        dya = dy_ref[...].astype(F32)
        sz = jax.nn.silu(za)
        dmixed = dya * gu * sz
        dza = dya * gu * mixed * _silu_grad(za)
        dgu = dya * mixed * sz

        @pl.when(i == 0)
        def _():
            for r in (dg_ref, db_ref, dws_ref, dbsf_ref):
                r[...] = jnp.zeros_like(r)

        rows = []
        for ci in range(nc):
            r0 = ci * CHUNK
            heads = []
            for h in range(nh):
                cols = slice(h * HEAD_DIM_A, (h + 1) * HEAD_DIM_A)
                dm = dmixed[r0:r0 + CHUNK, cols]
                heads.append(_mxu_dot(ws_ref[h], dm, _TN))
                dws_ref[h] += _mxu_dot(dm, vn[r0:r0 + CHUNK, cols], _NT)
            rows.append(jnp.concatenate(heads, axis=1))
            dbsf_ref[...] += dmixed[r0:r0 + CHUNK, :]
        dvn = jnp.concatenate(rows, axis=0) if nc > 1 else rows[0]
        dg_ref[...] += jnp.sum(dvn * vhat, axis=0, keepdims=True)
        db_ref[...] += jnp.sum(dvn, axis=0, keepdims=True)
        dvh = dvn * gv
        dgv = rstd * (dvh - jnp.mean(dvh, axis=-1, keepdims=True) - vhat * jnp.mean(dvh * vhat, axis=-1, keepdims=True))
        dp_ref[:, :w] = (dgu * _gelu_grad(u_raw)).astype(dp_ref.dtype)
        dp_ref[:, w:2 * w] = (dgv * _gelu_grad(v_raw)).astype(dp_ref.dtype)
        dp_ref[:, 2 * w:3 * w] = dza.astype(dp_ref.dtype)

    vec = pl.BlockSpec((1, w), lambda i: (0, 0))
    row = pl.BlockSpec((tl, w), lambda i: (i, 0))
    ws_spec = pl.BlockSpec((nh, CHUNK, CHUNK), lambda i: (0, 0, 0))
    bs_spec = pl.BlockSpec((CHUNK, w), lambda i: (0, 0))
    v = jax.ShapeDtypeStruct((1, w), F32)
    return pl.pallas_call(
        body, name="ga_bwd",
        out_shape=(jax.ShapeDtypeStruct(dproj.shape, dproj.dtype), v, v, jax.ShapeDtypeStruct((nh, CHUNK, CHUNK), F32),
                   jax.ShapeDtypeStruct((CHUNK, w), F32)),
        grid=(l // tl,),
        in_specs=[pl.BlockSpec((tl, 3 * w), lambda i: (i, 0)), row, HBM, vec, vec, ws_spec, bs_spec, row, row, row, vec],
        out_specs=(pl.BlockSpec((tl, 4 * w), lambda i: (i, 0)), vec, vec, ws_spec, bs_spec),
        input_output_aliases={2: 0}, compiler_params=_cparams("arbitrary"))(
            proj, dcat, dproj, g, b, ws, bsf, dys, du0, du1, d_skip)


def _lane_group_sum(x, expand, name):
    return _small_dot(x, expand, "nn", name)


def _disc_math(lr, li, ls, br, bi):
    step = jnp.exp(ls)
    dr, di = lr * step, li * step
    mag = jnp.exp(dr)
    ab_re, ab_im = mag * jnp.cos(di), mag * jnp.sin(di)
    den = lr * lr + li * li
    nr, ni = ab_re - 1.0, ab_im
    f_re = (nr * lr + ni * li) / den
    f_im = (ni * lr - nr * li) / den
    bb_re = f_re * br - f_im * bi
    bb_im = f_re * bi + f_im * br
    return ab_re, ab_im, bb_re, bb_im


def _disc_fwd(lr, li, ls, br, bi):
    def body(lr_ref, li_ref, ls_ref, br_ref, bi_ref, o1, o2, o3, o4):
        res = _disc_math(lr_ref[...], li_ref[...], ls_ref[...], br_ref[...], bi_ref[...])
        for o, r in zip((o1, o2, o3, o4), res):
            o[...] = r

    s = lambda a: jax.ShapeDtypeStruct(a.shape, F32)
    return pl.pallas_call(body, name="s5_disc", out_shape=(s(lr), s(lr), s(br), s(br)), compiler_params=_cparams())(
        lr, li, ls, br, bi)


def _disc_bwd(lr, li, ls, br, bi, d_ar, d_ai, d_br, d_bi):
    def body(lr_ref, li_ref, ls_ref, br_ref, bi_ref, c1, c2, c3, c4, o1, o2, o3, o4, o5):
        _, vjp = jax.vjp(_disc_math, lr_ref[...], li_ref[...], ls_ref[...], br_ref[...], bi_ref[...])
        res = vjp((c1[...], c2[...], c3[...], c4[...]))
        for o, r in zip((o1, o2, o3, o4, o5), res):
            o[...] = r

    s = lambda a: jax.ShapeDtypeStruct(a.shape, F32)
    return pl.pallas_call(body, name="s5_disc_bwd", out_shape=(s(lr), s(lr), s(ls), s(br), s(br)),
                          compiler_params=_cparams())(lr, li, ls, br, bi, d_ar, d_ai, d_br, d_bi)


def _dir_spec(a, dr, **kw):
    return pl.BlockSpec((None,) + a.shape[1:], lambda i: (dr,) + (0,) * (a.ndim - 1), **kw)


def _s5_fwd(u_arr, u_col, w, h0, a_sm, wbr, wbi, cre, ncim, dr, name):
    rev = dr == 1
    l = u_arr.shape[0]
    nb = w // LANES
    spb = wbr.shape[-1]
    nsr = a_sm.shape[2]
    assert 2 * spb == 8 * LANES and nb % 2 == 0
    npair = nb // 2
    t = _tile(l, 256)
    n = l // t
    tile = (lambda i: n - 1 - i) if rev else (lambda i: i)

    def body(u_ref, h0_ref, a_ref, wbr_ref, wbi_ref, cre_ref, ncim_ref, y_ref, hr_ref, hi_ref, tr_ref, ti_ref, hfin_ref,
             carry_ref):
        i = pl.program_id(0)

        @pl.when(i == 0)
        def _():
            carry_ref[...] = h0_ref[...]

        for j in range(npair):
            for h_ref, w_ref in ((hr_ref, wbr_ref), (hi_ref, wbi_ref)):
                blk = [_mxu_dot(u_ref[:, k * LANES:(k + 1) * LANES], w_ref[k]) for k in (2 * j, 2 * j + 1)]
                h_ref[j] = jnp.concatenate(blk, axis=1).reshape(t, 8, LANES)
        slab = lambda ref, part, j: ref[part, 8 * j:8 * j + 8, :]
        ar = [slab(a_ref, 0, j) for j in range(npair)]
        ai = [slab(a_ref, 1, j) for j in range(npair)]

        def steps(blk, c):
            hr, hi = list(c[:npair]), list(c[npair:])
            for q in range(SCAN_UNROLL):
                s = blk * SCAN_UNROLL + q
                row = t - 1 - s if rev else s
                for j in range(npair):
                    hr[j], hi[j] = (ar[j] * hr[j] - ai[j] * hi[j] + hr_ref[j, row],
                                    ar[j] * hi[j] + ai[j] * hr[j] + hi_ref[j, row])
                    hr_ref[j, row] = hr[j]
                    hi_ref[j, row] = hi[j]
            return tuple(hr + hi)

        init = tuple(slab(carry_ref, part, j) for part in range(2) for j in range(npair))
        c = lax.fori_loop(0, t // SCAN_UNROLL, steps, init)
        for part in range(2):
            for j in range(npair):
                carry_ref[part, 8 * j:8 * j + 8, :] = c[part * npair + j]
                hfin_ref[part, 8 * j:8 * j + 8, :] = c[part * npair + j]
        for j in range(npair):
            cols8 = slice(j * 8 * LANES, (j + 1) * 8 * LANES)
            tr_ref[:, cols8] = hr_ref[j].reshape(t, 8 * LANES).astype(tr_ref.dtype)
            ti_ref[:, cols8] = hi_ref[j].reshape(t, 8 * LANES).astype(ti_ref.dtype)
        for k in range(nb):
            cols = slice(k * spb, (k + 1) * spb)
            y_ref[:, k * LANES:(k + 1) * LANES] = (_mxu_dot(tr_ref[:, cols], cre_ref[k]) + _mxu_dot(ti_ref[:, cols], ncim_ref[k]))

    full = lambda a: pl.BlockSpec(a.shape, lambda i: (0,) * a.ndim)
    hspec = pl.BlockSpec((npair, t, 8, LANES), lambda i: (0, tile(i), 0, 0))
    tspec = pl.BlockSpec((t, nsr * LANES), lambda i: (tile(i), 0))
    hsh = jax.ShapeDtypeStruct((npair, l, 8, LANES), F32)
    tsh = jax.ShapeDtypeStruct((l, nsr * LANES), MXU_DTYPE)
    return pl.pallas_call(
        body, name=name,
        out_shape=(jax.ShapeDtypeStruct((l, w), F32), hsh, hsh, tsh, tsh, jax.ShapeDtypeStruct((2, nsr, LANES), F32)),
        grid=(n,),
        in_specs=[pl.BlockSpec((t, w), lambda i: (tile(i), u_col)), full(h0)] + [_dir_spec(a, dr) for a in (a_sm, wbr, wbi, cre, ncim)],
        out_specs=(pl.BlockSpec((t, w), lambda i: (tile(i), 0)), hspec, hspec, tspec, tspec,
                   pl.BlockSpec((2, nsr, LANES), lambda i: (0, 0, 0))),
        scratch_shapes=[pltpu.VMEM((2, nsr, LANES), F32)],
        compiler_params=_cparams("arbitrary"))(u_arr, h0, a_sm, wbr, wbi, cre, ncim)


def _s5_bwd(dys, u_arr, u_col, w, hr, hi, tr, ti, hbound, g_in, a_sm, wbr_t, wbi_t, cre_t, ncim_t, dr, name):
    rev = dr == 1
    l = u_arr.shape[0]
    nb = w // LANES
    spb = wbr_t.shape[-2]
    nsr = a_sm.shape[2]
    npair = nb // 2
    t = _tile(l, 256 if l >= 1024 else 128)
    n = l // t
    with_dy = dys is not None
    tile = (lambda i: i) if rev else (lambda i: n - 1 - i)

    def body(*refs):
        if with_dy:
            (dy_ref, u_ref, hr_ref, hi_ref, pr_ref, pi_ref, tr_ref, ti_ref, hb_ref, gin_ref, a_ref, wbrt_ref, wbit_ref,
             cret_ref, ncimt_ref, du_ref, dwbr_ref, dwbi_ref, dcre_ref, dncim_ref, da_ref, gout_ref,
             gr_ref, gi_ref, gtr_ref, gti_ref, carry_ref) = refs
        else:
            (u_ref, hr_ref, hi_ref, pr_ref, pi_ref, hb_ref, gin_ref, a_ref, wbrt_ref, wbit_ref,
             du_ref, dwbr_ref, dwbi_ref, da_ref, gout_ref, gr_ref, gi_ref, gtr_ref, gti_ref, carry_ref) = refs
        i = pl.program_id(0)

        @pl.when(i == 0)
        def _():
            carry_ref[...] = gin_ref[...]
            accs = (dwbr_ref, dwbi_ref, da_ref) + ((dcre_ref, dncim_ref) if with_dy else ())
            for r in accs:
                r[...] = jnp.zeros_like(r)

        if with_dy:
            for j in range(npair):
                for g_ref, c_ref in ((gr_ref, cret_ref), (gi_ref, ncimt_ref)):
                    blk = [_mxu_dot(dy_ref[:, k * LANES:(k + 1) * LANES], c_ref[k]) for k in (2 * j, 2 * j + 1)]
                    g_ref[j] = jnp.concatenate(blk, axis=1).reshape(t, 8, LANES)
        else:
            gr_ref[...] = jnp.zeros_like(gr_ref)
            gi_ref[...] = jnp.zeros_like(gi_ref)
        slab = lambda ref, part, j: ref[part, 8 * j:8 * j + 8, :]
        last = t - 1 if rev else 0
        first = i == n - 1

        ar = [slab(a_ref, 0, j) for j in range(npair)]
        ai = [slab(a_ref, 1, j) for j in range(npair)]

        def steps(blk, c):
            gr, gi, dr, di = (list(c[q * npair:(q + 1) * npair]) for q in range(4))
            for q in range(SCAN_UNROLL):
                s = blk * SCAN_UNROLL + q
                row = s if rev else t - 1 - s
                prow = jnp.minimum(row + 1, t - 1) if rev else jnp.maximum(row - 1, 0)
                for j in range(npair):
                    pr, pi = hr_ref[j, prow], hi_ref[j, prow]
                    gr[j], gi[j] = (gr_ref[j, row] + ar[j] * gr[j] + ai[j] * gi[j],
                                    gi_ref[j, row] + ar[j] * gi[j] - ai[j] * gr[j])
                    gr_ref[j, row] = gr[j]
                    gi_ref[j, row] = gi[j]
                    dr[j], di[j] = dr[j] + gr[j] * pr + gi[j] * pi, di[j] + gi[j] * pr - gr[j] * pi
            return tuple(gr + gi + dr + di)

        init = tuple(slab(ref, part, j) for ref in (carry_ref, da_ref) for part in range(2) for j in range(npair))
        c = lax.fori_loop(0, t // SCAN_UNROLL, steps, init)
        gr, gi, dr, di = (c[q * npair:(q + 1) * npair] for q in range(4))
        for j in range(npair):
            pr = jnp.where(first, slab(hb_ref, 0, j), pr_ref[j, 0]) - hr_ref[j, last]
            pi = jnp.where(first, slab(hb_ref, 1, j), pi_ref[j, 0]) - hi_ref[j, last]
            rows = slice(8 * j, 8 * j + 8)
            da_ref[0, rows, :] = dr[j] + gr[j] * pr + gi[j] * pi
            da_ref[1, rows, :] = di[j] + gi[j] * pr - gr[j] * pi
            for part, val in enumerate((gr[j], gi[j])):
                carry_ref[part, rows, :] = val
                gout_ref[part, rows, :] = val

        for j in range(npair):
            cols8 = slice(j * 8 * LANES, (j + 1) * 8 * LANES)
            gtr_ref[:, cols8] = gr_ref[j].reshape(t, 8 * LANES).astype(gtr_ref.dtype)
            gti_ref[:, cols8] = gi_ref[j].reshape(t, 8 * LANES).astype(gti_ref.dtype)
        for k in range(nb):
            cols = slice(k * spb, (k + 1) * spb)
            lanes = slice(k * LANES, (k + 1) * LANES)
            du_ref[:, lanes] = _mxu_dot(gtr_ref[:, cols], wbrt_ref[k]) + _mxu_dot(gti_ref[:, cols], wbit_ref[k])
            dwbr_ref[k] += _mxu_dot(u_ref[:, lanes], gtr_ref[:, cols], _TN)
            dwbi_ref[k] += _mxu_dot(u_ref[:, lanes], gti_ref[:, cols], _TN)
            if with_dy:
                dcre_ref[k] += _mxu_dot(tr_ref[:, cols], dy_ref[:, lanes], _TN)
                dncim_ref[k] += _mxu_dot(ti_ref[:, cols], dy_ref[:, lanes], _TN)

    once = dict(pipeline_mode=pl.Buffered(1))
    full = lambda a: pl.BlockSpec(a.shape, lambda i: (0,) * a.ndim, **once)
    row = lambda cb: pl.BlockSpec((t, w), lambda i: (tile(i), cb))
    hspec = pl.BlockSpec((npair, t, 8, LANES), lambda i: (0, tile(i), 0, 0))
    if rev:
        pspec = pl.BlockSpec((npair, 1, 8, LANES), lambda i: (0, jnp.minimum((tile(i) + 1) * t, l - 1), 0, 0))
    else:
        pspec = pl.BlockSpec((npair, 1, 8, LANES), lambda i: (0, jnp.maximum(tile(i) * t - 1, 0), 0, 0))
    sm = jax.ShapeDtypeStruct((2, nsr, LANES), F32)
    smspec = pl.BlockSpec((2, nsr, LANES), lambda i: (0, 0, 0))
    wsh = jax.ShapeDtypeStruct((nb, LANES, spb), F32)
    csh = jax.ShapeDtypeStruct((nb, spb, LANES), F32)
    tspec = pl.BlockSpec((t, nsr * LANES), lambda i: (tile(i), 0))
    in_specs = (([row(0)] if with_dy else []) + [row(u_col), hspec, hspec, pspec, pspec] + ([tspec, tspec] if with_dy else [])
                + [full(hbound), full(g_in)]
                + [_dir_spec(a, dr, **once) for a in (a_sm, wbr_t, wbi_t) + ((cre_t, ncim_t) if with_dy else ())])
    args = (([dys] if with_dy else []) + [u_arr, hr, hi, hr, hi] + ([tr, ti] if with_dy else [])
            + [hbound, g_in, a_sm, wbr_t, wbi_t] + ([cre_t, ncim_t] if with_dy else []))
    out_shape = (jax.ShapeDtypeStruct((l, w), F32), wsh, wsh) + ((csh, csh) if with_dy else ()) + (sm, sm)
    out_specs = (row(0), full(wsh), full(wsh)) + ((full(csh), full(csh)) if with_dy else ()) + (smspec, smspec)
    return pl.pallas_call(
        body, name=name, out_shape=out_shape, grid=(n,), in_specs=in_specs, out_specs=out_specs,
        scratch_shapes=[pltpu.VMEM((npair, t, 8, LANES), F32)] * 2 + [pltpu.VMEM((t, nsr * LANES), MXU_DTYPE)] * 2
        + [pltpu.VMEM((2, nsr, LANES), F32)],
        compiler_params=_cparams("arbitrary"))(*args)


def _glu_fwd(y0, y1, proj, cat, d_skip, w_glu, b_glu, w):
    l = y0.shape[0]
    tl = _tile(l, 512)

    def body(y0_ref, y1_ref, u_ref, z_ref, cat_in, d_ref, wg_ref, bg_ref, ys_ref, cat_ref):
        del cat_in
        ys = y0_ref[...] + y1_ref[...] + d_ref[...] * u_ref[...]
        ys_ref[...] = ys
        gy = _gelu(ys)
        s = _mxu_dot(gy, wg_ref[...]) + bg_ref[...]
        cat_ref[...] = (gy * jax.nn.sigmoid(s) * jax.nn.silu(z_ref[...])).astype(cat_ref.dtype)

    row = pl.BlockSpec((tl, w), lambda i: (i, 0))
    vec = pl.BlockSpec((1, w), lambda i: (0, 0))
    return pl.pallas_call(
        body, name="glu_fwd", out_shape=(jax.ShapeDtypeStruct((l, w), F32), jax.ShapeDtypeStruct(cat.shape, cat.dtype)),
        grid=(l // tl,),
        in_specs=[row, row, pl.BlockSpec((tl, w), lambda i: (i, 3)), pl.BlockSpec((tl, w), lambda i: (i, 4)), HBM,
                  vec, pl.BlockSpec((w, w), lambda i: (0, 0), pipeline_mode=pl.Buffered(1)), vec],
        out_specs=(row, pl.BlockSpec((tl, w), lambda i: (i, 1))), input_output_aliases={4: 1},
        compiler_params=_cparams("parallel"))(y0, y1, proj, proj, cat, d_skip, w_glu, b_glu)


def _glu_bwd(dcat, ys, proj, w_glu, b_glu, w, dep):
    l = ys.shape[0]
    tl = _tile(l, 512)

    def body(dy_ref, ys_ref, u_ref, z_ref, wg_ref, bg_ref, dep_ref, dys_ref, dp_ref, dbg_ref, dd_ref, dwg_ref):
        del dep_ref
        i = pl.program_id(0)
        ys_t = ys_ref[...]
        z = z_ref[...]
        dyb = dy_ref[...].astype(F32)
        gy = _gelu(ys_t)
        sg = jax.nn.sigmoid(_mxu_dot(gy, wg_ref[...]) + bg_ref[...])
        dp_ref[...] = (dyb * gy * sg * _silu_grad(z)).astype(dp_ref.dtype)
        dglu = dyb * jax.nn.silu(z)
        ds = dglu * gy * sg * (1.0 - sg)
        dgy = dglu * sg + _mxu_dot(ds, wg_ref[...], _NT)
        dys_t = dgy * _gelu_grad(ys_t)
        dys_ref[...] = dys_t

        @pl.when(i == 0)
        def _():
            for r in (dbg_ref, dd_ref, dwg_ref):
                r[...] = jnp.zeros_like(r)

        dbg_ref[...] += jnp.sum(ds, axis=0, keepdims=True)
        dd_ref[...] += jnp.sum(dys_t * u_ref[...], axis=0, keepdims=True)
        dwg_ref[...] += _mxu_dot(gy, ds, _TN)

    row = pl.BlockSpec((tl, w), lambda i: (i, 0))
    vec = pl.BlockSpec((1, w), lambda i: (0, 0))
    mat = pl.BlockSpec((w, w), lambda i: (0, 0), pipeline_mode=pl.Buffered(1))
    v = jax.ShapeDtypeStruct((1, w), F32)
    return pl.pallas_call(
        body, name="glu_bwd",
        out_shape=(jax.ShapeDtypeStruct((l, w), F32), jax.ShapeDtypeStruct((l, 5 * w), MXU_DTYPE), v, v,
                   jax.ShapeDtypeStruct((w, w), F32)),
        grid=(l // tl,),
        in_specs=[pl.BlockSpec((tl, w), lambda i: (i, 1)), row, pl.BlockSpec((tl, w), lambda i: (i, 3)),
                  pl.BlockSpec((tl, w), lambda i: (i, 4)), mat, vec, HBM],
        out_specs=(row, pl.BlockSpec((tl, w), lambda i: (i, 4)), vec, vec, mat),
        compiler_params=_cparams("arbitrary"))(dcat, ys, proj, proj, w_glu, b_glu, dep)


def _add2(a, b, name):
    l, w = a.shape
    tl = _tile(l, 512)

    def body(a_ref, b_ref, o_ref):
        o_ref[...] = a_ref[...] + b_ref[...]

    row = pl.BlockSpec((tl, w), lambda i: (i, 0))
    return pl.pallas_call(body, name=name, out_shape=jax.ShapeDtypeStruct((l, w), F32), grid=(l // tl,),
                          in_specs=[row, row], out_specs=row, compiler_params=_cparams("parallel"))(a, b)


def _adamw_nd(w, m, v, g, name):
    shape = w.shape
    lead = math.prod(shape[:-2]) if len(shape) > 2 else 1
    b, c = (shape[-2], shape[-1]) if len(shape) >= 2 else (1, shape[-1])
    t3 = (lead, b, c)
    padded_row = -(-b // 8) * 8 * -(-c // LANES) * LANES * 4
    ta = _tile(lead, max(1, (2 << 20) // padded_row))

    def body(w_ref, m_ref, v_ref, g_ref, d_ref, mo_ref, vo_ref):
        d_ref[...], mo_ref[...], vo_ref[...] = _adamw_step(w_ref[...], m_ref[...], v_ref[...], g_ref[...])

    blk = pl.BlockSpec((ta, b, c), lambda i: (i, 0, 0))
    s = jax.ShapeDtypeStruct(t3, F32)
    outs = pl.pallas_call(body, name=name, out_shape=(s, s, s), grid=(lead // ta,), in_specs=[blk] * 4, out_specs=(blk,) * 3,
                          compiler_params=_cparams("parallel"))(*[a.reshape(t3) for a in (w, m, v, g)])
    return tuple(o.reshape(shape) for o in outs)


def _adamw_step(w, m, v, g):
    mn = ADAM_B1 * m + (1.0 - ADAM_B1) * g
    vn = ADAM_B2 * v + (1.0 - ADAM_B2) * (g * g)
    m_hat = mn / (1.0 - ADAM_B1 ** ADAM_STEP)
    v_hat = vn / (1.0 - ADAM_B2 ** ADAM_STEP)
    return -ADAM_LR * (m_hat / (jnp.sqrt(v_hat) + ADAM_EPS) + ADAM_WD * w), mn, vn


def _adamw_many(params, name):
    def as3(a):
        s = a.shape
        lead = math.prod(s[:-2]) if len(s) > 2 else 1
        return a.reshape((lead,) + ((s[-2], s[-1]) if len(s) >= 2 else (1, s[-1])))

    flat = [as3(a) for p in params for a in p]
    n = len(params)

    def body(*refs):
        ins, outs = refs[:4 * n], refs[4 * n:]
        for q in range(n):
            w_ref, m_ref, v_ref, g_ref = ins[4 * q:4 * q + 4]
            for o, val in zip(outs[3 * q:3 * q + 3], _adamw_step(w_ref[...], m_ref[...], v_ref[...], g_ref[...])):
                o[...] = val

    out_shape = [jax.ShapeDtypeStruct(flat[4 * q].shape, F32) for q in range(n) for _ in range(3)]
    outs = pl.pallas_call(body, name=name, out_shape=out_shape, compiler_params=_cparams())(*flat)
    return [tuple(o.reshape(params[q][0].shape) for o in outs[3 * q:3 * q + 3]) for q in range(n)]


def _adamw(w, m, v, gparts, name, dep=None):
    r, c = w.shape
    np_ = gparts.shape[0]
    tr = _tile(r, max(8, (1 << 18) // c), 8)

    def body(w_ref, m_ref, v_ref, g_ref, *rest):
        go_ref, d_ref, mo_ref, vo_ref = rest[-4:]
        g = g_ref[0].astype(F32)
        for p in range(1, np_):
            g = g + g_ref[p].astype(F32)
        go_ref[...] = g
        d_ref[...], mo_ref[...], vo_ref[...] = _adamw_step(w_ref[...], m_ref[...], v_ref[...], g)

    row = pl.BlockSpec((tr, c), lambda i: (i, 0))
    s = jax.ShapeDtypeStruct((r, c), F32)
    extra = [] if dep is None else [dep]
    return pl.pallas_call(body, name=name, out_shape=(s, s, s, s), grid=(r // tr,),
                          in_specs=[row, row, row, pl.BlockSpec((np_, tr, c), lambda i: (0, i, 0))] + [HBM] * len(extra),
                          out_specs=(row, row, row, row), compiler_params=_cparams("parallel"))(w, m, v, gparts, *extra)


def _sum_slots(parts, name):
    np_, r, c = parts.shape

    def body(p_ref, o_ref):
        g = p_ref[0]
        for p in range(1, np_):
            g = g + p_ref[p]
        o_ref[...] = g

    return pl.pallas_call(body, name=name, out_shape=jax.ShapeDtypeStruct((r, c), F32), compiler_params=_cparams())(parts)


def _block_diag(x, gb):
    nd, g, a, b = x.shape
    eye = jnp.eye(gb, dtype=x.dtype)
    y = jnp.einsum("dkgab,gh->dkgahb", x.reshape(nd, g // gb, gb, a, b), eye)
    return y.reshape(nd, g // gb, gb * a, gb * b)


def _block_diag_extract(y, gb, a, b):
    nd, nbk = y.shape[:2]
    eye = jnp.eye(gb, dtype=y.dtype)
    x = jnp.einsum("dkgahb,gh->dkgab", y.reshape(nd, nbk, gb, a, gb, b), eye)
    return x.reshape(nd, nbk * gb, a, b)


def kernel(x, c, ctx, c_ctx, w_ada, b_ada, w_in, sgu_ln_g, sgu_ln_b, w_spatial, b_spatial, s5_lam_re, s5_lam_im, s5_log_step, s5_b_re, s5_b_im, s5_c_re, s5_c_im, s5_d, w_glu, b_glu, w_out, ln_g, ln_b, loss_target, m_c_ctx, m_w_ada, m_b_ada, m_w_in, m_sgu_ln_g, m_sgu_ln_b, m_w_spatial, m_b_spatial, m_s5_lam_re, m_s5_lam_im, m_s5_log_step, m_s5_b_re, m_s5_b_im, m_s5_c_re, m_s5_c_im, m_s5_d, m_w_glu, m_b_glu, m_w_out, m_ln_g, m_ln_b, v_c_ctx, v_w_ada, v_b_ada, v_w_in, v_sgu_ln_g, v_sgu_ln_b, v_w_spatial, v_b_spatial, v_s5_lam_re, v_s5_lam_im, v_s5_log_step, v_s5_b_re, v_s5_b_im, v_s5_c_re, v_s5_c_im, v_s5_d, v_w_glu, v_b_glu, v_w_out, v_ln_g, v_ln_b):
    small_names = ["c_ctx", "b_ada", "sgu_ln_g", "sgu_ln_b", "w_spatial", "b_spatial", "s5_lam_re", "s5_lam_im",
                   "s5_log_step", "s5_b_re", "s5_b_im", "s5_c_re", "s5_c_im", "s5_d", "b_glu", "ln_g", "ln_b"]
    env = dict(locals())
    x2, tgt, ctx2 = x[0], loss_target[0], ctx[0]
    l, d = x2.shape
    lc = ctx2.shape[0]
    w = d // 2
    nh = w // HEAD_DIM_A
    nd, g_s5, p_s5, c_s5 = s5_b_re.shape[1:]
    ns = g_s5 * p_s5
    nsr = ns // LANES
    gb = LANES // c_s5
    me = _index(_mesh_pos())
    ada_cols = w_ada.shape[2]

    srows = _silu_rows(c, c_ctx)
    srows_all = _all_gather(srows, 0, "gather_silu")
    s_mat = jnp.concatenate([srows_all[0::8], srows_all[1:2], jnp.zeros((7, d), F32)], axis=0)
    mod_part = _small_dot(s_mat, w_ada[0], "nn", "mod_cols")
    mod_all = _all_gather(mod_part, 1, "gather_mod") + b_ada
    hw_in, tok_a = _exchange_start(w_in[0].astype(MXU_DTYPE), 1, "gather", "start_gather_w_in", (SIBLING,) + SAME_CORE_PEERS)
    mod_all = mod_all + tok_a[0, 0]
    mod_x = lax.dynamic_slice_in_dim(mod_all, me, 1, axis=0)
    mod_c = mod_all[8:9]
    shift_x, scale_x, gate_x = mod_x[:, :d], mod_x[:, d:2 * d], mod_x[:, 2 * d:]
    shift_c, scale_c = mod_c[:, :d], mod_c[:, d:2 * d]

    lr, li = s5_lam_re[0][:, :, None, :], s5_lam_im[0][:, :, None, :]
    ls = s5_log_step[0][:, :, None, None]
    swapped = ("s5_b_re", "s5_b_im")
    for nm in swapped:
        for pre in ("", "m_", "v_"):
            env[pre + nm] = jnp.swapaxes(env[pre + nm], -1, -2)
    br_t, bi_t = env["s5_b_re"][0], env["s5_b_im"][0]
    ab_re, ab_im, bb_re, bb_im = _disc_fwd(lr, li, ls, br_t, bi_t)
    a_sm = jnp.stack([ab_re, ab_im], axis=1).reshape(nd, 2, nsr, LANES)
    wbr = _block_diag(bb_re.astype(MXU_DTYPE), gb)
    wbi = _block_diag(bb_im.astype(MXU_DTYPE), gb)
    cre_t = _block_diag(s5_c_re[0].astype(MXU_DTYPE), gb)
    ncim_t = _block_diag((-s5_c_im[0]).astype(MXU_DTYPE), gb)
    cre, ncim = jnp.swapaxes(cre_t, 2, 3), jnp.swapaxes(ncim_t, 2, 3)
    wbr_t, wbi_t = jnp.swapaxes(wbr, 2, 3), jnp.swapaxes(wbi, 2, 3)
    d_skip = s5_d

    xm = _ln_mod(x2, shift_x, scale_x, "ln_mod_x")
    cm = _ln_mod(ctx2, shift_c, scale_c, "ln_mod_ctx")
    ready = xm[:8, :LANES].astype(F32) + cm[:8, :LANES].astype(F32) + cre[0, 0, :8, :].astype(F32)
    hw_in2, tok_b = _forward_start(_exchange_wait(hw_in, ready, "wait_gather_w_in"), 1, "start_forward_w_in")
    w_in_f = _forward_wait(hw_in2, tok_b, "wait_forward_w_in")
    hw_glu, tok_c = _exchange_start(w_glu[0].astype(MXU_DTYPE), 0, "gather", "start_gather_w_glu")
    hw_out, tok_o = _exchange_start(w_out[0].astype(MXU_DTYPE), 0, "gather", "start_gather_w_out")
    proj = _matmul(xm, w_in_f, mode="nn", name="proj", dep=tok_c + tok_o)
    ub_c = _matmul(cm, w_in_f, mode="nn", name="proj_ctx", b_n0=3 * w, n=w)
    bsf = jnp.repeat(b_spatial[0].T, HEAD_DIM_A, axis=1)
    ws = w_spatial[0]
    cat = _ga_fwd(proj, sgu_ln_g, sgu_ln_b, ws, bsf, w)
    zeros_state = jnp.zeros((2, nsr, LANES), F32)
    s5c, s5l = [], []
    for dr in range(nd):
        s5c.append(_s5_fwd(ub_c, 0, w, zeros_state, a_sm, wbr, wbi, cre, ncim, dr, f"s5_fwd_ctx{dr}"))
        s5l.append(_s5_fwd(proj, 3, w, s5c[dr][5], a_sm, wbr, wbi, cre, ncim, dr, f"s5_fwd{dr}"))
    w_glu_f = _exchange_wait(hw_glu, s5l[1][0], "wait_gather_w_glu")
    ys, cat = _glu_fwd(s5l[0][0], s5l[1][0], proj, cat, d_skip, w_glu_f, b_glu, w)
    w_out_f = _exchange_wait(hw_out, ys, "wait_gather_w_out")
    out = _matmul(cat, w_out_f, mode="nn", name="out_proj")
    loss_row, dout, dx_res, dgate, dln_g, dln_b = _post_ln_loss(x2, out, gate_x, ln_g, ln_b, tgt)

    dcat = _matmul(dout, w_out_f, mode="nt", name="d_cat", out_dtype=MXU_DTYPE)
    dw_out = _matmul(cat, dout, mode="tn", name="d_w_out", out_dtype=MXU_DTYPE)
    hg_out, tok_d = _exchange_start(dw_out, 0, "a2a", "start_a2a_d_w_out")
    dys, dproj, db_glu, dd_skip, dw_glu = _glu_bwd(dcat, ys, proj, w_glu_f, b_glu, w, tok_d)
    hg_glu, tok_e = _exchange_start(dw_glu.astype(MXU_DTYPE), 0, "a2a", "start_a2a_d_w_glu")
    zeros_state = zeros_state + tok_e[0, 0]
    du_l, du_c, dwbr, dwbi, dcre, dncim, da_sm = [], [], [], [], [], [], []
    nbk, spb = w // LANES, gb * p_s5
    for dr in range(nd):
        bl = _s5_bwd(dys, proj, 3, w, *s5l[dr][1:5], s5c[dr][5], zeros_state, a_sm, wbr_t, wbi_t,
                     cre_t, ncim_t, dr, f"s5_bwd{dr}")
        bc = _s5_bwd(None, ub_c, 0, w, s5c[dr][1], s5c[dr][2], None, None, zeros_state, bl[6], a_sm, wbr_t, wbi_t,
                     None, None, dr, f"s5_bwd_ctx{dr}")
        du_l.append(bl[0])
        du_c.append(bc[0])
        dwbr.append(_add2(bl[1].reshape(nbk * LANES, spb), bc[1].reshape(nbk * LANES, spb), f"sum_dwbr{dr}"))
        dwbi.append(_add2(bl[2].reshape(nbk * LANES, spb), bc[2].reshape(nbk * LANES, spb), f"sum_dwbi{dr}"))
        dcre.append(bl[3])
        dncim.append(bl[4])
        da_sm.append(_add2(bl[5].reshape(2 * nsr, LANES), bc[3].reshape(2 * nsr, LANES), f"sum_da{dr}"))
    dub_c = _add2(du_c[0], du_c[1], "dub_ctx")
    dwbr = jnp.stack(dwbr).reshape(nd, nbk, LANES, spb)
    dwbi = jnp.stack(dwbi).reshape(nd, nbk, LANES, spb)
    dcre, dncim = jnp.stack(dcre), jnp.stack(dncim)
    da_sm = jnp.stack(da_sm).reshape(nd, 2, g_s5, p_s5)
    dproj, dsg, dsb, dws, dbsf = _ga_bwd(proj, dcat, dproj, sgu_ln_g, sgu_ln_b, ws, bsf, dys, du_l[0], du_l[1], d_skip, w)

    dbb_re = _block_diag_extract(dwbr, gb, c_s5, p_s5)
    dbb_im = _block_diag_extract(dwbi, gb, c_s5, p_s5)
    dc_re = jnp.swapaxes(_block_diag_extract(dcre, gb, p_s5, c_s5), 2, 3)
    dc_im = -jnp.swapaxes(_block_diag_extract(dncim, gb, p_s5, c_s5), 2, 3)
    dlr, dli, dls, db_re, db_im = _disc_bwd(lr, li, ls, br_t, bi_t, da_sm[:, 0:1].reshape(nd, g_s5, 1, p_s5),
                                            da_sm[:, 1:2].reshape(nd, g_s5, 1, p_s5), dbb_re, dbb_im)
    expand = (jnp.arange(w)[:, None] // HEAD_DIM_A == jnp.arange(LANES)[None, :]).astype(F32)
    db_sp = _lane_group_sum(dbsf, expand, "d_b_spatial")[:, :nh].T

    local = {"sgu_ln_g": dsg, "sgu_ln_b": dsb, "w_spatial": dws, "b_spatial": db_sp,
             "s5_lam_re": dlr, "s5_lam_im": dli, "s5_log_step": dls, "s5_b_re": db_re, "s5_b_im": db_im,
             "s5_c_re": dc_re, "s5_c_im": dc_im, "s5_d": dd_skip, "b_glu": db_glu, "ln_g": dln_g, "ln_b": dln_b}
    reduced = sorted(local, key=lambda n: -math.prod(env[n].shape))
    loss_part = (0.5 / d) * jnp.sum(loss_row)
    flat = jnp.concatenate([local[n].reshape(-1) for n in reduced] + [loss_part.reshape(1)])
    unit = N_DEV * 8 * LANES
    total = -(-flat.shape[0] // unit) * unit
    flat = jnp.pad(flat, (0, total - flat.shape[0])).reshape(N_DEV * 8, total // (N_DEV * 8))
    h_small, tok_s = _exchange_start(flat, 0, "a2a", "start_a2a_small")

    dw_in = _matmul(xm, dproj, mode="tn", name="d_w_in", out_dtype=MXU_DTYPE, dep=tok_s)
    dw_in = _matmul(cm, dub_c, mode="tn", name="d_w_in_ctx", acc_in=dw_in, acc_n0=3 * w, out_dtype=MXU_DTYPE)
    hg_in, tok_f = _exchange_start(dw_in, 1, "a2a", "start_a2a_d_w_in")
    mine = _sum_slots(_exchange_wait(h_small, dw_in, "wait_a2a_small"), "sum_small")
    h_sums, tok_g = _exchange_start(mine, 0, "gather", "start_gather_small")
    dxm = _matmul(dproj, w_in_f, mode="nt", name="d_xm", dep=tok_f + tok_g, out_dtype=MXU_DTYPE)
    dcm = _matmul(dub_c, w_in_f, mode="nt", name="d_cm", b_k0=3 * w, k=w)
    grad_x, dshift_x, dscale_x = _ln_mod_bwd(x2, dxm, scale_x, dx_res, "ln_mod_x_bwd")
    _, dshift_c, dscale_c = _ln_mod_bwd(ctx2, dcm, scale_c, None, "ln_mod_ctx_bwd")

    dmod_rows = jnp.concatenate([jnp.concatenate([dshift_x, dscale_x, dgate], axis=1),
                                 jnp.concatenate([dshift_c, dscale_c, jnp.zeros((1, d), F32)], axis=1),
                                 jnp.zeros((6, 3 * d), F32)], axis=0)
    h_dmod, tok_m = _exchange_start(dmod_rows, 0, "gather", "start_gather_dmod")

    gp_w_out = _exchange_wait(hg_out, tok_m, "wait_a2a_d_w_out")
    gp_w_glu = _exchange_wait(hg_glu, tok_m, "wait_a2a_d_w_glu")
    gp_w_in = _exchange_wait(hg_in, tok_m, "wait_a2a_d_w_in")
    big = {
        "w_in": _adamw(w_in[0], m_w_in[0], v_w_in[0], gp_w_in, "adamw_w_in"),
        "w_glu": _adamw(w_glu[0], m_w_glu[0], v_w_glu[0], gp_w_glu, "adamw_w_glu"),
        "w_out": _adamw(w_out[0], m_w_out[0], v_w_out[0], gp_w_out, "adamw_w_out"),
    }
    dmod_all = _exchange_wait(h_dmod, big["w_out"][0], "wait_gather_dmod")
    dmod_ctx = _sum_slots(dmod_all[1::8].reshape(N_DEV, 1, 3 * d), "sum_dmod_ctx")
    dmod_mat = jnp.concatenate([dmod_all[0::8], dmod_ctx, jnp.zeros((7, 3 * d), F32)], axis=0)
    db_ada = _sum_slots(dmod_mat[:9].reshape(9, 1, 3 * d), "sum_db_ada")
    dmod_mine = lax.dynamic_slice_in_dim(dmod_mat, me * ada_cols, ada_cols, axis=1)
    dw_ada = _small_dot(s_mat, dmod_mine, "tn", "d_w_ada")
    dsilu_cc = _small_dot(dmod_mine[8:16], w_ada[0], "nt", "d_silu_cctx")[0:1]
    dc_ctx_part = dsilu_cc * _silu_grad(c_ctx.reshape(1, d))
    dc_ctx_rows = jnp.concatenate([dc_ctx_part, jnp.zeros((7, d), F32)], axis=0)
    h_cctx, tok_c2 = _exchange_start(dc_ctx_rows, 0, "gather", "start_gather_d_c_ctx")

    big["w_ada"] = _adamw(w_ada[0], m_w_ada[0], v_w_ada[0], dw_ada[None], "adamw_w_ada", dep=tok_c2)
    summed = _exchange_wait(h_sums, big["w_ada"][0], "wait_gather_small").reshape(-1)
    grads, off = {"b_ada": db_ada}, 0
    for n in reduced:
        size = math.prod(env[n].shape)
        grads[n] = summed[off:off + size].reshape(env[n].shape)
        off += size
    loss = summed[off]
    res = {n: tuple(a[None] for a in big[n]) for n in big}

    def small_step(n):
        res[n] = (grads[n],) + _adamw_nd(env[n], env["m_" + n], env["v_" + n], grads[n], "adamw_" + n)
        if n in swapped:
            res[n] = tuple(jnp.swapaxes(a, -1, -2) for a in res[n])

    large = ("w_spatial", "s5_b_re", "s5_b_im", "s5_c_re", "s5_c_im")
    tiny = [n for n in small_names if n not in large and n != "c_ctx"]
    for n in large:
        small_step(n)
    for n, out in zip(tiny, _adamw_many([(env[n], env["m_" + n], env["v_" + n], grads[n]) for n in tiny], "adamw_tiny")):
        res[n] = (grads[n],) + out
    dc_ctx_all = _exchange_wait(h_cctx, res["w_spatial"][1], "wait_gather_d_c_ctx")
    grads["c_ctx"] = _sum_slots(dc_ctx_all[0::8].reshape(N_DEV, 1, d), "sum_d_c_ctx").reshape(d)
    small_step("c_ctx")

    order = ["c_ctx", "w_ada", "b_ada", "w_in", "sgu_ln_g", "sgu_ln_b", "w_spatial", "b_spatial", "s5_lam_re", "s5_lam_im",
             "s5_log_step", "s5_b_re", "s5_b_im", "s5_c_re", "s5_c_im", "s5_d", "w_glu", "b_glu", "w_out", "ln_g", "ln_b"]
    return (loss, grad_x[None], *[res[n][0] for n in order], *[res[n][1] for n in order],
            *[res[n][2] for n in order], *[res[n][3] for n in order])
```

```python
import functools
import math

import jax
import jax.numpy as jnp
from jax import lax
from jax.experimental import pallas as pl
from jax.experimental.pallas import tpu as pltpu

F32 = jnp.float32
MXU_DTYPE = jnp.bfloat16
N_DEV = 8
MESH_ID = pl.DeviceIdType.MESH
LN_EPS = 1e-6
DEPTH = 1
ALPHA = (2.0 * DEPTH) ** 0.25
CHUNK = 128
HEAD_DIM_A = 128
ADAM_LR, ADAM_B1, ADAM_B2, ADAM_EPS, ADAM_WD, ADAM_STEP = 0.001, 0.9, 0.999, 1e-08, 0.01, 10
LANES = 128
SCAN_UNROLL = 16
VMEM_LIMIT = 56 * 1024 * 1024
HBM = pl.BlockSpec(memory_space=pl.ANY)


def _cparams(*sem):
    return pltpu.CompilerParams(dimension_semantics=sem if sem else None, vmem_limit_bytes=VMEM_LIMIT)


def _tile(n, pref, mult=1):
    if n <= pref:
        return n
    t = pref - pref % mult
    while n % t:
        t -= mult
    return t


def _gelu(x):
    return 0.5 * x * (1.0 + lax.erf(x * (1.0 / math.sqrt(2.0))))


def _gelu_grad(x):
    return 0.5 * (1.0 + lax.erf(x * (1.0 / math.sqrt(2.0)))) + x * jnp.exp(-0.5 * x * x) * (1.0 / math.sqrt(2.0 * math.pi))


def _silu_grad(x):
    s = jax.nn.sigmoid(x)
    return s * (1.0 + x * (1.0 - s))


def _mxu_dot(a, b, dims=(((1,), (0,)), ((), ()))):
    return lax.dot_general(a.astype(MXU_DTYPE), b.astype(MXU_DTYPE), dims, preferred_element_type=F32)


_NT = (((1,), (1,)), ((), ()))
_TN = (((0,), (0,)), ((), ()))


def _mesh_pos():
    return lax.axis_index("x"), lax.axis_index("y"), lax.axis_index("c")


def _peer(pos, r):
    x, y, c = pos
    return ((1 - x) if r & 4 else x, (1 - y) if r & 2 else y, (1 - c) if r & 1 else c)


def _index(pos):
    return 4 * pos[0] + 2 * pos[1] + pos[2]


def _slice_of(ref, axis, idx, size):
    start = idx * size
    if axis == 0:
        return ref.at[pl.ds(start, size)]
    return ref.at[:, pl.ds(start, size)]


def _all_gather(x, axis, name):
    size = x.shape[axis]
    out_shape = tuple(s * N_DEV if a == axis else s for a, s in enumerate(x.shape))

    def body(x_ref, o_ref, send_sems, recv_sems, local_sem):
        me = _mesh_pos()
        mine = pltpu.make_async_copy(x_ref, _slice_of(o_ref, axis, _index(me), size), local_sem)
        mine.start()

        def copy(r, block):
            return pltpu.make_async_remote_copy(
                src_ref=x_ref, dst_ref=_slice_of(o_ref, axis, _index(block), size),
                send_sem=send_sems.at[r - 1], recv_sem=recv_sems.at[r - 1],
                device_id=_peer(me, r), device_id_type=MESH_ID)

        sends = [copy(r, me) for r in range(1, N_DEV)]
        for cp in sends:
            cp.start()
        for r in range(1, N_DEV):
            copy(r, _peer(me, r)).wait_recv()
        for cp in sends:
            cp.wait_send()
        mine.wait()

    return pl.pallas_call(
        body, name=name, out_shape=jax.ShapeDtypeStruct(out_shape, x.dtype),
        in_specs=[HBM], out_specs=HBM,
        scratch_shapes=[pltpu.SemaphoreType.DMA((N_DEV - 1,)), pltpu.SemaphoreType.DMA((N_DEV - 1,)),
                        pltpu.SemaphoreType.DMA],
    )(x)


_SEM = pl.BlockSpec(memory_space=pltpu.SEMAPHORE)
_HBM = pl.BlockSpec(memory_space=pltpu.HBM)
_EFFECT = pltpu.SideEffectType.DATAFLOW_SIDE_EFFECTING
ALL_PEERS = tuple(range(1, N_DEV))
SIBLING = 1
SAME_CORE_PEERS = (2, 4, 6)


def _exchange_copy(kind, x_ref, land_ref, axis, size, send_sems, recv_sems, me, rels, q, arriving):
    peer = _peer(me, rels[q])
    sender, receiver = (peer, me) if arriving else (me, peer)
    if kind == "gather":
        src, dst = x_ref, _slice_of(land_ref, axis, _index(sender), size)
    else:
        src, dst = _slice_of(x_ref, axis, _index(receiver), size), land_ref.at[_index(sender)]
    return pltpu.make_async_remote_copy(src_ref=src, dst_ref=dst, send_sem=send_sems.at[q], recv_sem=recv_sems.at[q],
                                        device_id=peer, device_id_type=MESH_ID)


def _local_copy(kind, x_ref, land_ref, axis, size, me, local_sem):
    if kind == "gather":
        return pltpu.make_async_copy(x_ref, _slice_of(land_ref, axis, _index(me), size), local_sem)
    return pltpu.make_async_copy(_slice_of(x_ref, axis, _index(me), size), land_ref.at[_index(me)], local_sem)


def _exchange_start(x, axis, kind, name, rels=ALL_PEERS):
    size = x.shape[axis] if kind == "gather" else x.shape[axis] // N_DEV
    if kind == "gather":
        land_shape = tuple(s * N_DEV if a == axis else s for a, s in enumerate(x.shape))
    else:
        land_shape = (N_DEV,) + tuple(size if a == axis else s for a, s in enumerate(x.shape))

    def body(x_ref, land_ref, send_sems, recv_sems, local_sem, x_thru, land_thru, token):
        del x_thru, land_thru
        me = _mesh_pos()
        _local_copy(kind, x_ref, land_ref, axis, size, me, local_sem).start()
        for q in range(len(rels)):
            _exchange_copy(kind, x_ref, land_ref, axis, size, send_sems, recv_sems, me, rels, q, False).start()
        token[...] = jnp.zeros_like(token)

    sems = pltpu.SemaphoreType.DMA((len(rels),))
    send_sems, recv_sems, local_sem, x_thru, land_thru, token = pl.pallas_call(
        body, name=name,
        out_shape=(sems, sems, pltpu.SemaphoreType.DMA(()), pltpu.HBM(x.shape, x.dtype), pltpu.HBM(land_shape, x.dtype),
                   jax.ShapeDtypeStruct((8, LANES), F32)),
        in_specs=(_HBM, _HBM), out_specs=(_SEM, _SEM, _SEM, _HBM, _HBM, pl.BlockSpec(memory_space=pltpu.VMEM)),
        input_output_aliases={0: 3, 1: 4}, compiler_params=pltpu.CompilerParams(has_side_effects=_EFFECT),
    )(pltpu.with_memory_space_constraint(x, pltpu.HBM),
      pltpu.with_memory_space_constraint(lax.empty(land_shape, x.dtype), pltpu.HBM))
    return (kind, axis, size, rels, send_sems, recv_sems, local_sem, x_thru, land_thru), token


def _exchange_wait(handle, after, name):
    kind, axis, size, rels, send_sems, recv_sems, local_sem, x_thru, land_thru = handle

    def body(x_ref, land_ref, send_sems, recv_sems, local_sem, after_ref, x_dead, got_ref):
        del after_ref, x_dead, got_ref
        me = _mesh_pos()
        _local_copy(kind, x_ref, land_ref, axis, size, me, local_sem).wait()
        for q in range(len(rels)):
            _exchange_copy(kind, x_ref, land_ref, axis, size, send_sems, recv_sems, me, rels, q, False).wait_send()
        for q in range(len(rels)):
            _exchange_copy(kind, x_ref, land_ref, axis, size, send_sems, recv_sems, me, rels, q, True).wait_recv()

    return pl.pallas_call(
        body, name=name, out_shape=(pltpu.HBM(x_thru.shape, x_thru.dtype), pltpu.HBM(land_thru.shape, land_thru.dtype)),
        in_specs=(_HBM, _HBM, _SEM, _SEM, _SEM, HBM), out_specs=(_HBM, _HBM), input_output_aliases={0: 0, 1: 1},
        compiler_params=pltpu.CompilerParams(has_side_effects=_EFFECT),
    )(x_thru, land_thru, send_sems, recv_sems, local_sem, after)[1]


def _forward_copy(land_ref, axis, size, send_sems, recv_sems, me, q, arriving):
    sibling = _peer(me, SIBLING)
    owner = _peer(sibling if arriving else me, SAME_CORE_PEERS[q])
    block = _slice_of(land_ref, axis, _index(owner), size)
    return pltpu.make_async_remote_copy(src_ref=block, dst_ref=block, send_sem=send_sems.at[q], recv_sem=recv_sems.at[q],
                                        device_id=sibling, device_id_type=MESH_ID)


def _forward_start(land, axis, name):
    size = land.shape[axis] // N_DEV

    def body(land_ref, send_sems, recv_sems, land_thru, token):
        del land_thru
        me = _mesh_pos()
        for q in range(len(SAME_CORE_PEERS)):
            _forward_copy(land_ref, axis, size, send_sems, recv_sems, me, q, False).start()
        token[...] = jnp.zeros_like(token)

    sems = pltpu.SemaphoreType.DMA((len(SAME_CORE_PEERS),))
    send_sems, recv_sems, land_thru, token = pl.pallas_call(
        body, name=name, out_shape=(sems, sems, pltpu.HBM(land.shape, land.dtype), jax.ShapeDtypeStruct((8, LANES), F32)),
        in_specs=(_HBM,), out_specs=(_SEM, _SEM, _HBM, pl.BlockSpec(memory_space=pltpu.VMEM)),
        input_output_aliases={0: 2}, compiler_params=pltpu.CompilerParams(has_side_effects=_EFFECT),
    )(land)
    return (axis, size, send_sems, recv_sems, land_thru), token


def _forward_wait(handle, after, name):
    axis, size, send_sems, recv_sems, land_thru = handle

    def body(land_ref, send_sems, recv_sems, after_ref, got_ref):
        del after_ref, got_ref
        me = _mesh_pos()
        for q in range(len(SAME_CORE_PEERS)):
            _forward_copy(land_ref, axis, size, send_sems, recv_sems, me, q, False).wait_send()
        for q in range(len(SAME_CORE_PEERS)):
            _forward_copy(land_ref, axis, size, send_sems, recv_sems, me, q, True).wait_recv()

    return pl.pallas_call(
        body, name=name, out_shape=pltpu.HBM(land_thru.shape, land_thru.dtype),
        in_specs=(_HBM, _SEM, _SEM, HBM), out_specs=_HBM, input_output_aliases={0: 0},
        compiler_params=pltpu.CompilerParams(has_side_effects=_EFFECT),
    )(land_thru, send_sems, recv_sems, after)


WHOLE = 1 << 30
MATMUL_TILES = {
    "proj": (1024, 1024, WHOLE), "proj_ctx": (256, WHOLE, WHOLE), "out_proj": (1024, 1024, WHOLE),
    "d_cat": (1024, 1024, WHOLE), "d_w_out": (1024, 1024, 2048), "d_w_in": (1024, 1280, 2048),
    "d_w_in_ctx": (1024, WHOLE, WHOLE), "d_xm": (1024, 1024, WHOLE), "d_cm": (256, 1024, WHOLE),
}


def _matmul(a, b, *, mode, name, out_dtype=F32, b_n0=0, n=None, b_k0=0, k=None, acc_in=None, acc_n0=0, dep=None):
    bm, bn, bk = MATMUL_TILES[name]
    if mode == "tn":
        kk, m = a.shape
    else:
        m, kk = a.shape
    if mode == "nn":
        n = b.shape[1] if n is None else n
    elif mode == "nt":
        n = b.shape[0]
        kk = kk if k is None else k
    else:
        n = b.shape[1]
    bm, bn, bk = _tile(m, bm), _tile(n, bn), _tile(kk, bk)
    nk = kk // bk
    assert b_n0 % bn == 0 and b_k0 % bk == 0 and acc_n0 % bn == 0
    dims = {"nn": (((1,), (0,)), ((), ())), "nt": _NT, "tn": _TN}[mode]

    n_in = 2 + (acc_in is not None) + (dep is not None)

    def body(*refs):
        a_ref, b_ref = refs[:2]
        init = refs[2] if acc_in is not None else None
        o_ref = refs[n_in]
        acc_ref = refs[-1] if nk > 1 else None
        p = _mxu_dot(a_ref[...], b_ref[...], dims)
        if nk == 1:
            o_ref[...] = (p if init is None else p + init[...]).astype(out_dtype)
            return
        ki = pl.program_id(2)

        @pl.when(ki == 0)
        def _():
            acc_ref[...] = p if init is None else p + init[...]

        @pl.when(ki > 0)
        def _():
            acc_ref[...] += p

        @pl.when(ki == nk - 1)
        def _():
            o_ref[...] = acc_ref[...].astype(out_dtype)

    a_spec = pl.BlockSpec((bk, bm), lambda j, i, q: (q, i)) if mode == "tn" else pl.BlockSpec((bm, bk), lambda j, i, q: (i, q))
    if mode == "nt":
        b_spec = pl.BlockSpec((bn, bk), lambda j, i, q: (j, q + b_k0 // bk))
    else:
        b_spec = pl.BlockSpec((bk, bn), lambda j, i, q: (q, j + b_n0 // bn))
    in_specs, args, aliases = [a_spec, b_spec], [a, b], {}
    out_map = lambda j, i, q: (i, j + acc_n0 // bn)
    if acc_in is not None:
        in_specs.append(pl.BlockSpec((bm, bn), out_map))
        args.append(acc_in)
        aliases = {2: 0}
        out_shape = jax.ShapeDtypeStruct(acc_in.shape, out_dtype)
    else:
        out_shape = jax.ShapeDtypeStruct((m, n), out_dtype)
    if dep is not None:
        in_specs.append(HBM)
        args.append(dep)
    return pl.pallas_call(
        body, name=name, out_shape=out_shape, grid=(n // bn, m // bm, nk),
        in_specs=in_specs, out_specs=pl.BlockSpec((bm, bn), out_map),
        scratch_shapes=[pltpu.VMEM((bm, bn), F32)] if nk > 1 else [],
        input_output_aliases=aliases,
        compiler_params=_cparams("parallel", "parallel", "arbitrary"),
    )(*args)


def _silu_rows(c, c_ctx):
    d = c.shape[-1]

    def body(c_ref, cc_ref, o_ref):
        o_ref[...] = jnp.zeros_like(o_ref)
        o_ref[0:1, :] = jax.nn.silu(c_ref[...])
        o_ref[1:2, :] = jax.nn.silu(cc_ref[...])

    return pl.pallas_call(body, name="silu_rows", out_shape=jax.ShapeDtypeStruct((8, d), F32))(
        c.reshape(1, d), c_ctx.reshape(1, d))


def _small_dot(a, b, mode, name):
    dims = {"nn": (((1,), (0,)), ((), ())), "nt": _NT, "tn": _TN}[mode]
    m = a.shape[1] if mode == "tn" else a.shape[0]
    n = b.shape[0] if mode == "nt" else b.shape[1]

    def body(a_ref, b_ref, o_ref):
        o_ref[...] = lax.dot_general(a_ref[...], b_ref[...], dims, preferred_element_type=F32,
                                     precision=lax.Precision.HIGHEST)

    return pl.pallas_call(body, name=name, out_shape=jax.ShapeDtypeStruct((m, n), F32),
                          compiler_params=_cparams())(a, b)


def _ln_stats(x):
    mu = jnp.mean(x, axis=-1, keepdims=True)
    xc = x - mu
    var = jnp.mean(xc * xc, axis=-1, keepdims=True)
    rstd = lax.rsqrt(var + LN_EPS)
    return xc * rstd, rstd


def _ln_mod(x, shift, scale, name):
    l, d = x.shape
    tl = _tile(l, 512)

    def body(x_ref, sh_ref, sc_ref, o_ref):
        xhat, _ = _ln_stats(x_ref[...])
        o_ref[...] = (xhat * (1.0 + sc_ref[...]) + sh_ref[...]).astype(o_ref.dtype)

    row = pl.BlockSpec((tl, d), lambda i: (i, 0))
    vec = pl.BlockSpec((1, d), lambda i: (0, 0))
    return pl.pallas_call(body, name=name, out_shape=jax.ShapeDtypeStruct((l, d), MXU_DTYPE), grid=(l // tl,),
                          in_specs=[row, vec, vec], out_specs=row, compiler_params=_cparams("parallel"))(x, shift, scale)


def _ln_mod_bwd(x, dxm, scale, res, name):
    l, d = x.shape
    tl = _tile(l, 512)
    with_res = res is not None

    def body(*refs):
        if with_res:
            x_ref, g_ref, sc_ref, r_ref, dx_ref, dsh_ref, dsc_ref = refs
        else:
            x_ref, g_ref, sc_ref, dx_ref, dsh_ref, dsc_ref = refs
        i = pl.program_id(0)
        xhat, rstd = _ln_stats(x_ref[...])
        g = g_ref[...].astype(F32)
        dxh = g * (1.0 + sc_ref[...])
        dx = rstd * (dxh - jnp.mean(dxh, axis=-1, keepdims=True) - xhat * jnp.mean(dxh * xhat, axis=-1, keepdims=True))
        dx_ref[...] = dx + r_ref[...].astype(F32) if with_res else dx

        @pl.when(i == 0)
        def _():
            dsh_ref[...] = jnp.zeros_like(dsh_ref)
            dsc_ref[...] = jnp.zeros_like(dsc_ref)

        dsh_ref[...] += jnp.sum(g, axis=0, keepdims=True)
        dsc_ref[...] += jnp.sum(g * xhat, axis=0, keepdims=True)

    row = pl.BlockSpec((tl, d), lambda i: (i, 0))
    vec = pl.BlockSpec((1, d), lambda i: (0, 0))
    args = [x, dxm, scale] + ([res] if with_res else [])
    return pl.pallas_call(
        body, name=name,
        out_shape=(jax.ShapeDtypeStruct((l, d), F32), jax.ShapeDtypeStruct((1, d), F32), jax.ShapeDtypeStruct((1, d), F32)),
        grid=(l // tl,), in_specs=[row, row, vec] + ([row] if with_res else []), out_specs=(row, vec, vec),
        compiler_params=_cparams("arbitrary"))(*args)


def _post_ln_loss(x, out, gate, ln_g, ln_b, target):
    l, d = x.shape
    tl = _tile(l, 512)

    def body(x_ref, o_ref, gate_ref, g_ref, b_ref, t_ref, loss_ref, dout_ref, dxr_ref, dgate_ref, dg_ref, db_ref):
        i = pl.program_id(0)
        out_t = o_ref[...]
        gate_v = gate_ref[...]
        rhat, rstd = _ln_stats(ALPHA * x_ref[...] + gate_v * out_t)
        ln_gv = g_ref[...]
        diff = rhat * ln_gv + b_ref[...] - t_ref[...]
        dy = diff * (1.0 / d)
        drh = dy * ln_gv
        dr = rstd * (drh - jnp.mean(drh, axis=-1, keepdims=True) - rhat * jnp.mean(drh * rhat, axis=-1, keepdims=True))
        dout_ref[...] = (gate_v * dr).astype(dout_ref.dtype)
        dxr_ref[...] = (ALPHA * dr).astype(dxr_ref.dtype)

        @pl.when(i == 0)
        def _():
            for r in (loss_ref, dgate_ref, dg_ref, db_ref):
                r[...] = jnp.zeros_like(r)

        loss_ref[...] += jnp.sum(diff * diff, axis=0, keepdims=True)
        dgate_ref[...] += jnp.sum(dr * out_t, axis=0, keepdims=True)
        dg_ref[...] += jnp.sum(dy * rhat, axis=0, keepdims=True)
        db_ref[...] += jnp.sum(dy, axis=0, keepdims=True)

    row = pl.BlockSpec((tl, d), lambda i: (i, 0))
    vec = pl.BlockSpec((1, d), lambda i: (0, 0))
    v = jax.ShapeDtypeStruct((1, d), F32)
    return pl.pallas_call(
        body, name="post_ln_loss",
        out_shape=(v, jax.ShapeDtypeStruct((l, d), MXU_DTYPE), jax.ShapeDtypeStruct((l, d), MXU_DTYPE), v, v, v),
        grid=(l // tl,), in_specs=[row, row, vec, vec, vec, row], out_specs=(vec, row, row, vec, vec, vec),
        compiler_params=_cparams("arbitrary"))(x, out, gate, ln_g, ln_b, target)


def _ga_forward_tile(p, g, b, ws_ref, bsf, w, nc, nh):
    u_raw, v_raw, za = p[:, :w], p[:, w:2 * w], p[:, 2 * w:3 * w]
    gu = _gelu(u_raw)
    vhat, rstd = _ln_stats(_gelu(v_raw))
    vn = vhat * g + b
    rows = []
    for ci in range(nc):
        r0 = ci * CHUNK
        heads = [_mxu_dot(ws_ref[h], vn[r0:r0 + CHUNK, h * HEAD_DIM_A:(h + 1) * HEAD_DIM_A]) for h in range(nh)]
        rows.append(jnp.concatenate(heads, axis=1) + bsf)
    mixed = jnp.concatenate(rows, axis=0) if nc > 1 else rows[0]
    return u_raw, v_raw, za, gu, vhat, rstd, vn, mixed


def _ga_fwd(proj, g, b, ws, bsf, w):
    l = proj.shape[0]
    nh = w // HEAD_DIM_A
    nc = _tile(l // CHUNK, 2)
    tl = nc * CHUNK

    def body(p_ref, g_ref, b_ref, ws_ref, bsf_ref, o_ref):
        _, _, za, gu, _, _, _, mixed = _ga_forward_tile(p_ref[...], g_ref[...], b_ref[...], ws_ref, bsf_ref[...], w, nc, nh)
        o_ref[...] = (gu * mixed * jax.nn.silu(za)).astype(o_ref.dtype)

    vec = pl.BlockSpec((1, w), lambda i: (0, 0))
    return pl.pallas_call(
        body, name="ga_fwd", out_shape=jax.ShapeDtypeStruct((l, 2 * w), MXU_DTYPE), grid=(l // tl,),
        in_specs=[pl.BlockSpec((tl, 3 * w), lambda i: (i, 0)), vec, vec,
                  pl.BlockSpec((nh, CHUNK, CHUNK), lambda i: (0, 0, 0)), pl.BlockSpec((CHUNK, w), lambda i: (0, 0))],
        out_specs=pl.BlockSpec((tl, w), lambda i: (i, 0)), compiler_params=_cparams("parallel"))(proj, g, b, ws, bsf)


def _ga_bwd(proj, dcat, dproj, g, b, ws, bsf, dys, du0, du1, d_skip, w):
    l = proj.shape[0]
    nh = w // HEAD_DIM_A
    nc = _tile(l // CHUNK, 2)
    tl = nc * CHUNK

    def body(p_ref, dy_ref, dp_in, g_ref, b_ref, ws_ref, bsf_ref, dys_ref, du0_ref, du1_ref, d_ref,
             dp_ref, dg_ref, db_ref, dws_ref, dbsf_ref):
        del dp_in
        i = pl.program_id(0)
        dp_ref[:, 3 * w:] = (dys_ref[...] * d_ref[...] + du0_ref[...] + du1_ref[...]).astype(dp_ref.dtype)
        gv = g_ref[...]
        u_raw, v_raw, za, gu, vhat, rstd, vn, mixed = _ga_forward_tile(
            p_ref[...], gv, b_ref[...], ws_ref, bsf_ref[...], w, nc, nh)
        dya = dy_ref[...].astype(F32)
        sz = jax.nn.silu(za)
        dmixed = dya * gu * sz
        dza = dya * gu * mixed * _silu_grad(za)
        dgu = dya * mixed * sz

        @pl.when(i == 0)
        def _():
            for r in (dg_ref, db_ref, dws_ref, dbsf_ref):
                r[...] = jnp.zeros_like(r)

        rows = []
        for ci in range(nc):
            r0 = ci * CHUNK
            heads = []
            for h in range(nh):
                cols = slice(h * HEAD_DIM_A, (h + 1) * HEAD_DIM_A)
                dm = dmixed[r0:r0 + CHUNK, cols]
                heads.append(_mxu_dot(ws_ref[h], dm, _TN))
                dws_ref[h] += _mxu_dot(dm, vn[r0:r0 + CHUNK, cols], _NT)
            rows.append(jnp.concatenate(heads, axis=1))
            dbsf_ref[...] += dmixed[r0:r0 + CHUNK, :]
        dvn = jnp.concatenate(rows, axis=0) if nc > 1 else rows[0]
        dg_ref[...] += jnp.sum(dvn * vhat, axis=0, keepdims=True)
        db_ref[...] += jnp.sum(dvn, axis=0, keepdims=True)
        dvh = dvn * gv
        dgv = rstd * (dvh - jnp.mean(dvh, axis=-1, keepdims=True) - vhat * jnp.mean(dvh * vhat, axis=-1, keepdims=True))
        dp_ref[:, :w] = (dgu * _gelu_grad(u_raw)).astype(dp_ref.dtype)
        dp_ref[:, w:2 * w] = (dgv * _gelu_grad(v_raw)).astype(dp_ref.dtype)
        dp_ref[:, 2 * w:3 * w] = dza.astype(dp_ref.dtype)

    vec = pl.BlockSpec((1, w), lambda i: (0, 0))
    row = pl.BlockSpec((tl, w), lambda i: (i, 0))
    ws_spec = pl.BlockSpec((nh, CHUNK, CHUNK), lambda i: (0, 0, 0))
    bs_spec = pl.BlockSpec((CHUNK, w), lambda i: (0, 0))
    v = jax.ShapeDtypeStruct((1, w), F32)
    return pl.pallas_call(
        body, name="ga_bwd",
        out_shape=(jax.ShapeDtypeStruct(dproj.shape, dproj.dtype), v, v, jax.ShapeDtypeStruct((nh, CHUNK, CHUNK), F32),
                   jax.ShapeDtypeStruct((CHUNK, w), F32)),
        grid=(l // tl,),
        in_specs=[pl.BlockSpec((tl, 3 * w), lambda i: (i, 0)), row, HBM, vec, vec, ws_spec, bs_spec, row, row, row, vec],
        out_specs=(pl.BlockSpec((tl, 4 * w), lambda i: (i, 0)), vec, vec, ws_spec, bs_spec),
        input_output_aliases={2: 0}, compiler_params=_cparams("arbitrary"))(
            proj, dcat, dproj, g, b, ws, bsf, dys, du0, du1, d_skip)


def _lane_group_sum(x, expand, name):
    return _small_dot(x, expand, "nn", name)


def _disc_math(lr, li, ls, br, bi):
    step = jnp.exp(ls)
    dr, di = lr * step, li * step
    mag = jnp.exp(dr)
    ab_re, ab_im = mag * jnp.cos(di), mag * jnp.sin(di)
    den = lr * lr + li * li
    nr, ni = ab_re - 1.0, ab_im
    f_re = (nr * lr + ni * li) / den
    f_im = (ni * lr - nr * li) / den
    bb_re = f_re * br - f_im * bi
    bb_im = f_re * bi + f_im * br
    return ab_re, ab_im, bb_re, bb_im


def _disc_fwd(lr, li, ls, br, bi):
    def body(lr_ref, li_ref, ls_ref, br_ref, bi_ref, o1, o2, o3, o4):
        res = _disc_math(lr_ref[...], li_ref[...], ls_ref[...], br_ref[...], bi_ref[...])
        for o, r in zip((o1, o2, o3, o4), res):
            o[...] = r

    s = lambda a: jax.ShapeDtypeStruct(a.shape, F32)
    return pl.pallas_call(body, name="s5_disc", out_shape=(s(lr), s(lr), s(br), s(br)), compiler_params=_cparams())(
        lr, li, ls, br, bi)


def _disc_bwd(lr, li, ls, br, bi, d_ar, d_ai, d_br, d_bi):
    def body(lr_ref, li_ref, ls_ref, br_ref, bi_ref, c1, c2, c3, c4, o1, o2, o3, o4, o5):
        _, vjp = jax.vjp(_disc_math, lr_ref[...], li_ref[...], ls_ref[...], br_ref[...], bi_ref[...])
        res = vjp((c1[...], c2[...], c3[...], c4[...]))
        for o, r in zip((o1, o2, o3, o4, o5), res):
            o[...] = r

    s = lambda a: jax.ShapeDtypeStruct(a.shape, F32)
    return pl.pallas_call(body, name="s5_disc_bwd", out_shape=(s(lr), s(lr), s(ls), s(br), s(br)),
                          compiler_params=_cparams())(lr, li, ls, br, bi, d_ar, d_ai, d_br, d_bi)


def _dir_spec(a, dr, **kw):
    return pl.BlockSpec((None,) + a.shape[1:], lambda i: (dr,) + (0,) * (a.ndim - 1), **kw)


def _s5_fwd(u_arr, u_col, w, h0, a_sm, wbr, wbi, cre, ncim, dr, name):
    rev = dr == 1
    l = u_arr.shape[0]
    nb = w // LANES
    spb = wbr.shape[-1]
    nsr = a_sm.shape[2]
    assert 2 * spb == 8 * LANES and nb % 2 == 0
    npair = nb // 2
    t = _tile(l, 256)
    n = l // t
    tile = (lambda i: n - 1 - i) if rev else (lambda i: i)

    def body(u_ref, h0_ref, a_ref, wbr_ref, wbi_ref, cre_ref, ncim_ref, y_ref, hr_ref, hi_ref, hfin_ref,
             carry_ref, tr_ref, ti_ref):
        i = pl.program_id(0)

        @pl.when(i == 0)
        def _():
            carry_ref[...] = h0_ref[...]

        for j in range(npair):
            for h_ref, w_ref in ((hr_ref, wbr_ref), (hi_ref, wbi_ref)):
                blk = [_mxu_dot(u_ref[:, k * LANES:(k + 1) * LANES], w_ref[k]) for k in (2 * j, 2 * j + 1)]
                h_ref[j] = jnp.concatenate(blk, axis=1).reshape(t, 8, LANES)
        slab = lambda ref, part, j: ref[part, 8 * j:8 * j + 8, :]
        ar = [slab(a_ref, 0, j) for j in range(npair)]
        ai = [slab(a_ref, 1, j) for j in range(npair)]

        def steps(blk, c):
            hr, hi = list(c[:npair]), list(c[npair:])
            for q in range(SCAN_UNROLL):
                s = blk * SCAN_UNROLL + q
                row = t - 1 - s if rev else s
                for j in range(npair):
                    hr[j], hi[j] = (ar[j] * hr[j] - ai[j] * hi[j] + hr_ref[j, row],
                                    ar[j] * hi[j] + ai[j] * hr[j] + hi_ref[j, row])
                    hr_ref[j, row] = hr[j]
                    hi_ref[j, row] = hi[j]
            return tuple(hr + hi)

        init = tuple(slab(carry_ref, part, j) for part in range(2) for j in range(npair))
        c = lax.fori_loop(0, t // SCAN_UNROLL, steps, init)
        for part in range(2):
            for j in range(npair):
                carry_ref[part, 8 * j:8 * j + 8, :] = c[part * npair + j]
                hfin_ref[part, 8 * j:8 * j + 8, :] = c[part * npair + j]
        for j in range(npair):
            cols8 = slice(j * 8 * LANES, (j + 1) * 8 * LANES)
            tr_ref[:, cols8] = hr_ref[j].reshape(t, 8 * LANES).astype(tr_ref.dtype)
            ti_ref[:, cols8] = hi_ref[j].reshape(t, 8 * LANES).astype(ti_ref.dtype)
        for k in range(nb):
            cols = slice(k * spb, (k + 1) * spb)
            y_ref[:, k * LANES:(k + 1) * LANES] = (_mxu_dot(tr_ref[:, cols], cre_ref[k]) + _mxu_dot(ti_ref[:, cols], ncim_ref[k]))

    full = lambda a: pl.BlockSpec(a.shape, lambda i: (0,) * a.ndim)
    hspec = pl.BlockSpec((npair, t, 8, LANES), lambda i: (0, tile(i), 0, 0))
    hsh = jax.ShapeDtypeStruct((npair, l, 8, LANES), F32)
    return pl.pallas_call(
        body, name=name,
        out_shape=(jax.ShapeDtypeStruct((l, w), F32), hsh, hsh, jax.ShapeDtypeStruct((2, nsr, LANES), F32)),
        grid=(n,),
        in_specs=[pl.BlockSpec((t, w), lambda i: (tile(i), u_col)), full(h0)] + [_dir_spec(a, dr) for a in (a_sm, wbr, wbi, cre, ncim)],
        out_specs=(pl.BlockSpec((t, w), lambda i: (tile(i), 0)), hspec, hspec,
                   pl.BlockSpec((2, nsr, LANES), lambda i: (0, 0, 0))),
        scratch_shapes=[pltpu.VMEM((2, nsr, LANES), F32)] + [pltpu.VMEM((t, nsr * LANES), MXU_DTYPE)] * 2,
        compiler_params=_cparams("arbitrary"))(u_arr, h0, a_sm, wbr, wbi, cre, ncim)


def _s5_bwd(dys, u_arr, u_col, w, hr, hi, hbound, g_in, a_sm, wbr_t, wbi_t, cre_t, ncim_t, dr, name):
    rev = dr == 1
    l = u_arr.shape[0]
    nb = w // LANES
    spb = wbr_t.shape[-2]
    nsr = a_sm.shape[2]
    npair = nb // 2
    t = _tile(l, 256 if l >= 1024 else 128)
    n = l // t
    with_dy = dys is not None
    tile = (lambda i: i) if rev else (lambda i: n - 1 - i)

    def body(*refs):
        if with_dy:
            (dy_ref, u_ref, hr_ref, hi_ref, pr_ref, pi_ref, hb_ref, gin_ref, a_ref, wbrt_ref, wbit_ref,
             cret_ref, ncimt_ref, du_ref, dwbr_ref, dwbi_ref, dcre_ref, dncim_ref, da_ref, gout_ref,
             gr_ref, gi_ref, gtr_ref, gti_ref, carry_ref) = refs
        else:
            (u_ref, hr_ref, hi_ref, pr_ref, pi_ref, hb_ref, gin_ref, a_ref, wbrt_ref, wbit_ref,
             du_ref, dwbr_ref, dwbi_ref, da_ref, gout_ref, gr_ref, gi_ref, gtr_ref, gti_ref, carry_ref) = refs
        i = pl.program_id(0)

        @pl.when(i == 0)
        def _():
            carry_ref[...] = gin_ref[...]
            accs = (dwbr_ref, dwbi_ref, da_ref) + ((dcre_ref, dncim_ref) if with_dy else ())
            for r in accs:
                r[...] = jnp.zeros_like(r)

        if with_dy:
            for j in range(npair):
                cols8 = slice(j * 8 * LANES, (j + 1) * 8 * LANES)
                gtr_ref[:, cols8] = hr_ref[j].reshape(t, 8 * LANES).astype(gtr_ref.dtype)
                gti_ref[:, cols8] = hi_ref[j].reshape(t, 8 * LANES).astype(gti_ref.dtype)
            for k in range(nb):
                cols = slice(k * spb, (k + 1) * spb)
                lanes = slice(k * LANES, (k + 1) * LANES)
                dcre_ref[k] += _mxu_dot(gtr_ref[:, cols], dy_ref[:, lanes], _TN)
                dncim_ref[k] += _mxu_dot(gti_ref[:, cols], dy_ref[:, lanes], _TN)
            for j in range(npair):
                for g_ref, c_ref in ((gr_ref, cret_ref), (gi_ref, ncimt_ref)):
                    blk = [_mxu_dot(dy_ref[:, k * LANES:(k + 1) * LANES], c_ref[k]) for k in (2 * j, 2 * j + 1)]
                    g_ref[j] = jnp.concatenate(blk, axis=1).reshape(t, 8, LANES)
        else:
            gr_ref[...] = jnp.zeros_like(gr_ref)
            gi_ref[...] = jnp.zeros_like(gi_ref)
        slab = lambda ref, part, j: ref[part, 8 * j:8 * j + 8, :]
        last = t - 1 if rev else 0
        first = i == n - 1

        ar = [slab(a_ref, 0, j) for j in range(npair)]
        ai = [slab(a_ref, 1, j) for j in range(npair)]

        def steps(blk, c):
            gr, gi, dr, di = (list(c[q * npair:(q + 1) * npair]) for q in range(4))
            for q in range(SCAN_UNROLL):
                s = blk * SCAN_UNROLL + q
                row = s if rev else t - 1 - s
                prow = jnp.minimum(row + 1, t - 1) if rev else jnp.maximum(row - 1, 0)
                for j in range(npair):
                    pr, pi = hr_ref[j, prow], hi_ref[j, prow]
                    gr[j], gi[j] = (gr_ref[j, row] + ar[j] * gr[j] + ai[j] * gi[j],
                                    gi_ref[j, row] + ar[j] * gi[j] - ai[j] * gr[j])
                    gr_ref[j, row] = gr[j]
                    gi_ref[j, row] = gi[j]
                    dr[j], di[j] = dr[j] + gr[j] * pr + gi[j] * pi, di[j] + gi[j] * pr - gr[j] * pi
            return tuple(gr + gi + dr + di)

        init = tuple(slab(ref, part, j) for ref in (carry_ref, da_ref) for part in range(2) for j in range(npair))
        c = lax.fori_loop(0, t // SCAN_UNROLL, steps, init)
        gr, gi, dr, di = (c[q * npair:(q + 1) * npair] for q in range(4))
        for j in range(npair):
            pr = jnp.where(first, slab(hb_ref, 0, j), pr_ref[j, 0]) - hr_ref[j, last]
            pi = jnp.where(first, slab(hb_ref, 1, j), pi_ref[j, 0]) - hi_ref[j, last]
            rows = slice(8 * j, 8 * j + 8)
            da_ref[0, rows, :] = dr[j] + gr[j] * pr + gi[j] * pi
            da_ref[1, rows, :] = di[j] + gi[j] * pr - gr[j] * pi
            for part, val in enumerate((gr[j], gi[j])):
                carry_ref[part, rows, :] = val
                gout_ref[part, rows, :] = val

        for j in range(npair):
            cols8 = slice(j * 8 * LANES, (j + 1) * 8 * LANES)
            gtr_ref[:, cols8] = gr_ref[j].reshape(t, 8 * LANES).astype(gtr_ref.dtype)
            gti_ref[:, cols8] = gi_ref[j].reshape(t, 8 * LANES).astype(gti_ref.dtype)
        for k in range(nb):
            cols = slice(k * spb, (k + 1) * spb)
            lanes = slice(k * LANES, (k + 1) * LANES)
            du_ref[:, lanes] = _mxu_dot(gtr_ref[:, cols], wbrt_ref[k]) + _mxu_dot(gti_ref[:, cols], wbit_ref[k])
            dwbr_ref[k] += _mxu_dot(u_ref[:, lanes], gtr_ref[:, cols], _TN)
            dwbi_ref[k] += _mxu_dot(u_ref[:, lanes], gti_ref[:, cols], _TN)

    once = dict(pipeline_mode=pl.Buffered(1))
    full = lambda a: pl.BlockSpec(a.shape, lambda i: (0,) * a.ndim, **once)
    row = lambda cb: pl.BlockSpec((t, w), lambda i: (tile(i), cb))
    hspec = pl.BlockSpec((npair, t, 8, LANES), lambda i: (0, tile(i), 0, 0))
    if rev:
        pspec = pl.BlockSpec((npair, 1, 8, LANES), lambda i: (0, jnp.minimum((tile(i) + 1) * t, l - 1), 0, 0))
    else:
        pspec = pl.BlockSpec((npair, 1, 8, LANES), lambda i: (0, jnp.maximum(tile(i) * t - 1, 0), 0, 0))
    sm = jax.ShapeDtypeStruct((2, nsr, LANES), F32)
    smspec = pl.BlockSpec((2, nsr, LANES), lambda i: (0, 0, 0))
    wsh = jax.ShapeDtypeStruct((nb, LANES, spb), F32)
    csh = jax.ShapeDtypeStruct((nb, spb, LANES), F32)
    in_specs = (([row(0)] if with_dy else []) + [row(u_col), hspec, hspec, pspec, pspec, full(hbound), full(g_in)]
                + [_dir_spec(a, dr, **once) for a in (a_sm, wbr_t, wbi_t) + ((cre_t, ncim_t) if with_dy else ())])
    args = (([dys] if with_dy else []) + [u_arr, hr, hi, hr, hi, hbound, g_in, a_sm, wbr_t, wbi_t]
            + ([cre_t, ncim_t] if with_dy else []))
    out_shape = (jax.ShapeDtypeStruct((l, w), F32), wsh, wsh) + ((csh, csh) if with_dy else ()) + (sm, sm)
    out_specs = (row(0), full(wsh), full(wsh)) + ((full(csh), full(csh)) if with_dy else ()) + (smspec, smspec)
    return pl.pallas_call(
        body, name=name, out_shape=out_shape, grid=(n,), in_specs=in_specs, out_specs=out_specs,
        scratch_shapes=[pltpu.VMEM((npair, t, 8, LANES), F32)] * 2 + [pltpu.VMEM((t, nsr * LANES), MXU_DTYPE)] * 2
        + [pltpu.VMEM((2, nsr, LANES), F32)],
        compiler_params=_cparams("arbitrary"))(*args)


def _glu_fwd(y0, y1, proj, cat, d_skip, w_glu, b_glu, w):
    l = y0.shape[0]
    tl = _tile(l, 512)

    def body(y0_ref, y1_ref, u_ref, z_ref, cat_in, d_ref, wg_ref, bg_ref, ys_ref, cat_ref):
        del cat_in
        ys = y0_ref[...] + y1_ref[...] + d_ref[...] * u_ref[...]
        ys_ref[...] = ys
        gy = _gelu(ys)
        s = _mxu_dot(gy, wg_ref[...]) + bg_ref[...]
        cat_ref[...] = (gy * jax.nn.sigmoid(s) * jax.nn.silu(z_ref[...])).astype(cat_ref.dtype)

    row = pl.BlockSpec((tl, w), lambda i: (i, 0))
    vec = pl.BlockSpec((1, w), lambda i: (0, 0))
    return pl.pallas_call(
        body, name="glu_fwd", out_shape=(jax.ShapeDtypeStruct((l, w), F32), jax.ShapeDtypeStruct(cat.shape, cat.dtype)),
        grid=(l // tl,),
        in_specs=[row, row, pl.BlockSpec((tl, w), lambda i: (i, 3)), pl.BlockSpec((tl, w), lambda i: (i, 4)), HBM,
                  vec, pl.BlockSpec((w, w), lambda i: (0, 0), pipeline_mode=pl.Buffered(1)), vec],
        out_specs=(row, pl.BlockSpec((tl, w), lambda i: (i, 1))), input_output_aliases={4: 1},
        compiler_params=_cparams("parallel"))(y0, y1, proj, proj, cat, d_skip, w_glu, b_glu)


def _glu_bwd(dcat, ys, proj, w_glu, b_glu, w, dep):
    l = ys.shape[0]
    tl = _tile(l, 512)

    def body(dy_ref, ys_ref, u_ref, z_ref, wg_ref, bg_ref, dep_ref, dys_ref, dp_ref, dbg_ref, dd_ref, dwg_ref):
        del dep_ref
        i = pl.program_id(0)
        ys_t = ys_ref[...]
        z = z_ref[...]
        dyb = dy_ref[...].astype(F32)
        gy = _gelu(ys_t)
        sg = jax.nn.sigmoid(_mxu_dot(gy, wg_ref[...]) + bg_ref[...])
        dp_ref[...] = (dyb * gy * sg * _silu_grad(z)).astype(dp_ref.dtype)
        dglu = dyb * jax.nn.silu(z)
        ds = dglu * gy * sg * (1.0 - sg)
        dgy = dglu * sg + _mxu_dot(ds, wg_ref[...], _NT)
        dys_t = dgy * _gelu_grad(ys_t)
        dys_ref[...] = dys_t

        @pl.when(i == 0)
        def _():
            for r in (dbg_ref, dd_ref, dwg_ref):
                r[...] = jnp.zeros_like(r)

        dbg_ref[...] += jnp.sum(ds, axis=0, keepdims=True)
        dd_ref[...] += jnp.sum(dys_t * u_ref[...], axis=0, keepdims=True)
        dwg_ref[...] += _mxu_dot(gy, ds, _TN)

    row = pl.BlockSpec((tl, w), lambda i: (i, 0))
    vec = pl.BlockSpec((1, w), lambda i: (0, 0))
    mat = pl.BlockSpec((w, w), lambda i: (0, 0), pipeline_mode=pl.Buffered(1))
    v = jax.ShapeDtypeStruct((1, w), F32)
    return pl.pallas_call(
        body, name="glu_bwd",
        out_shape=(jax.ShapeDtypeStruct((l, w), F32), jax.ShapeDtypeStruct((l, 5 * w), MXU_DTYPE), v, v,
                   jax.ShapeDtypeStruct((w, w), F32)),
        grid=(l // tl,),
        in_specs=[pl.BlockSpec((tl, w), lambda i: (i, 1)), row, pl.BlockSpec((tl, w), lambda i: (i, 3)),
                  pl.BlockSpec((tl, w), lambda i: (i, 4)), mat, vec, HBM],
        out_specs=(row, pl.BlockSpec((tl, w), lambda i: (i, 4)), vec, vec, mat),
        compiler_params=_cparams("arbitrary"))(dcat, ys, proj, proj, w_glu, b_glu, dep)


def _add2(a, b, name):
    l, w = a.shape
    tl = _tile(l, 512)

    def body(a_ref, b_ref, o_ref):
        o_ref[...] = a_ref[...] + b_ref[...]

    row = pl.BlockSpec((tl, w), lambda i: (i, 0))
    return pl.pallas_call(body, name=name, out_shape=jax.ShapeDtypeStruct((l, w), F32), grid=(l // tl,),
                          in_specs=[row, row], out_specs=row, compiler_params=_cparams("parallel"))(a, b)


def _adamw_nd(w, m, v, g, name):
    shape = w.shape
    lead = math.prod(shape[:-2]) if len(shape) > 2 else 1
    b, c = (shape[-2], shape[-1]) if len(shape) >= 2 else (1, shape[-1])
    t3 = (lead, b, c)
    padded_row = -(-b // 8) * 8 * -(-c // LANES) * LANES * 4
    ta = _tile(lead, max(1, (2 << 20) // padded_row))

    def body(w_ref, m_ref, v_ref, g_ref, d_ref, mo_ref, vo_ref):
        d_ref[...], mo_ref[...], vo_ref[...] = _adamw_step(w_ref[...], m_ref[...], v_ref[...], g_ref[...])

    blk = pl.BlockSpec((ta, b, c), lambda i: (i, 0, 0))
    s = jax.ShapeDtypeStruct(t3, F32)
    outs = pl.pallas_call(body, name=name, out_shape=(s, s, s), grid=(lead // ta,), in_specs=[blk] * 4, out_specs=(blk,) * 3,
                          compiler_params=_cparams("parallel"))(*[a.reshape(t3) for a in (w, m, v, g)])
    return tuple(o.reshape(shape) for o in outs)


def _adamw_step(w, m, v, g):
    mn = ADAM_B1 * m + (1.0 - ADAM_B1) * g
    vn = ADAM_B2 * v + (1.0 - ADAM_B2) * (g * g)
    m_hat = mn / (1.0 - ADAM_B1 ** ADAM_STEP)
    v_hat = vn / (1.0 - ADAM_B2 ** ADAM_STEP)
    return -ADAM_LR * (m_hat / (jnp.sqrt(v_hat) + ADAM_EPS) + ADAM_WD * w), mn, vn


def _adamw_many(params, name):
    def as3(a):
        s = a.shape
        lead = math.prod(s[:-2]) if len(s) > 2 else 1
        return a.reshape((lead,) + ((s[-2], s[-1]) if len(s) >= 2 else (1, s[-1])))

    flat = [as3(a) for p in params for a in p]
    n = len(params)

    def body(*refs):
        ins, outs = refs[:4 * n], refs[4 * n:]
        for q in range(n):
            w_ref, m_ref, v_ref, g_ref = ins[4 * q:4 * q + 4]
            for o, val in zip(outs[3 * q:3 * q + 3], _adamw_step(w_ref[...], m_ref[...], v_ref[...], g_ref[...])):
                o[...] = val

    out_shape = [jax.ShapeDtypeStruct(flat[4 * q].shape, F32) for q in range(n) for _ in range(3)]
    outs = pl.pallas_call(body, name=name, out_shape=out_shape, compiler_params=_cparams())(*flat)
    return [tuple(o.reshape(params[q][0].shape) for o in outs[3 * q:3 * q + 3]) for q in range(n)]


def _adamw(w, m, v, gparts, name, dep=None):
    r, c = w.shape
    np_ = gparts.shape[0]
    tr = _tile(r, max(8, (1 << 18) // c), 8)

    def body(w_ref, m_ref, v_ref, g_ref, *rest):
        go_ref, d_ref, mo_ref, vo_ref = rest[-4:]
        g = g_ref[0].astype(F32)
        for p in range(1, np_):
            g = g + g_ref[p].astype(F32)
        go_ref[...] = g
        d_ref[...], mo_ref[...], vo_ref[...] = _adamw_step(w_ref[...], m_ref[...], v_ref[...], g)

    row = pl.BlockSpec((tr, c), lambda i: (i, 0))
    s = jax.ShapeDtypeStruct((r, c), F32)
    extra = [] if dep is None else [dep]
    return pl.pallas_call(body, name=name, out_shape=(s, s, s, s), grid=(r // tr,),
                          in_specs=[row, row, row, pl.BlockSpec((np_, tr, c), lambda i: (0, i, 0))] + [HBM] * len(extra),
                          out_specs=(row, row, row, row), compiler_params=_cparams("parallel"))(w, m, v, gparts, *extra)


def _sum_slots(parts, name):
    np_, r, c = parts.shape

    def body(p_ref, o_ref):
        g = p_ref[0]
        for p in range(1, np_):
            g = g + p_ref[p]
        o_ref[...] = g

    return pl.pallas_call(body, name=name, out_shape=jax.ShapeDtypeStruct((r, c), F32), compiler_params=_cparams())(parts)


def _block_diag(x, gb):
    nd, g, a, b = x.shape
    eye = jnp.eye(gb, dtype=x.dtype)
    y = jnp.einsum("dkgab,gh->dkgahb", x.reshape(nd, g // gb, gb, a, b), eye)
    return y.reshape(nd, g // gb, gb * a, gb * b)


def _block_diag_extract(y, gb, a, b):
    nd, nbk = y.shape[:2]
    eye = jnp.eye(gb, dtype=y.dtype)
    x = jnp.einsum("dkgahb,gh->dkgab", y.reshape(nd, nbk, gb, a, gb, b), eye)
    return x.reshape(nd, nbk * gb, a, b)


def kernel(x, c, ctx, c_ctx, w_ada, b_ada, w_in, sgu_ln_g, sgu_ln_b, w_spatial, b_spatial, s5_lam_re, s5_lam_im, s5_log_step, s5_b_re, s5_b_im, s5_c_re, s5_c_im, s5_d, w_glu, b_glu, w_out, ln_g, ln_b, loss_target, m_c_ctx, m_w_ada, m_b_ada, m_w_in, m_sgu_ln_g, m_sgu_ln_b, m_w_spatial, m_b_spatial, m_s5_lam_re, m_s5_lam_im, m_s5_log_step, m_s5_b_re, m_s5_b_im, m_s5_c_re, m_s5_c_im, m_s5_d, m_w_glu, m_b_glu, m_w_out, m_ln_g, m_ln_b, v_c_ctx, v_w_ada, v_b_ada, v_w_in, v_sgu_ln_g, v_sgu_ln_b, v_w_spatial, v_b_spatial, v_s5_lam_re, v_s5_lam_im, v_s5_log_step, v_s5_b_re, v_s5_b_im, v_s5_c_re, v_s5_c_im, v_s5_d, v_w_glu, v_b_glu, v_w_out, v_ln_g, v_ln_b):
    small_names = ["c_ctx", "b_ada", "sgu_ln_g", "sgu_ln_b", "w_spatial", "b_spatial", "s5_lam_re", "s5_lam_im",
                   "s5_log_step", "s5_b_re", "s5_b_im", "s5_c_re", "s5_c_im", "s5_d", "b_glu", "ln_g", "ln_b"]
    env = dict(locals())
    x2, tgt, ctx2 = x[0], loss_target[0], ctx[0]
    l, d = x2.shape
    lc = ctx2.shape[0]
    w = d // 2
    nh = w // HEAD_DIM_A
    nd, g_s5, p_s5, c_s5 = s5_b_re.shape[1:]
    ns = g_s5 * p_s5
    nsr = ns // LANES
    gb = LANES // c_s5
    me = _index(_mesh_pos())
    ada_cols = w_ada.shape[2]

    srows = _silu_rows(c, c_ctx)
    srows_all = _all_gather(srows, 0, "gather_silu")
    s_mat = jnp.concatenate([srows_all[0::8], srows_all[1:2], jnp.zeros((7, d), F32)], axis=0)
    mod_part = _small_dot(s_mat, w_ada[0], "nn", "mod_cols")
    mod_all = _all_gather(mod_part, 1, "gather_mod") + b_ada
    hw_in, tok_a = _exchange_start(w_in[0].astype(MXU_DTYPE), 1, "gather", "start_gather_w_in", (SIBLING,) + SAME_CORE_PEERS)
    mod_all = mod_all + tok_a[0, 0]
    mod_x = lax.dynamic_slice_in_dim(mod_all, me, 1, axis=0)
    mod_c = mod_all[8:9]
    shift_x, scale_x, gate_x = mod_x[:, :d], mod_x[:, d:2 * d], mod_x[:, 2 * d:]
    shift_c, scale_c = mod_c[:, :d], mod_c[:, d:2 * d]

    lr, li = s5_lam_re[0][:, :, None, :], s5_lam_im[0][:, :, None, :]
    ls = s5_log_step[0][:, :, None, None]
    swapped = ("s5_b_re", "s5_b_im")
    for nm in swapped:
        for pre in ("", "m_", "v_"):
            env[pre + nm] = jnp.swapaxes(env[pre + nm], -1, -2)
    br_t, bi_t = env["s5_b_re"][0], env["s5_b_im"][0]
    ab_re, ab_im, bb_re, bb_im = _disc_fwd(lr, li, ls, br_t, bi_t)
    a_sm = jnp.stack([ab_re, ab_im], axis=1).reshape(nd, 2, nsr, LANES)
    wbr = _block_diag(bb_re.astype(MXU_DTYPE), gb)
    wbi = _block_diag(bb_im.astype(MXU_DTYPE), gb)
    cre_t = _block_diag(s5_c_re[0].astype(MXU_DTYPE), gb)
    ncim_t = _block_diag((-s5_c_im[0]).astype(MXU_DTYPE), gb)
    cre, ncim = jnp.swapaxes(cre_t, 2, 3), jnp.swapaxes(ncim_t, 2, 3)
    wbr_t, wbi_t = jnp.swapaxes(wbr, 2, 3), jnp.swapaxes(wbi, 2, 3)
    d_skip = s5_d

    xm = _ln_mod(x2, shift_x, scale_x, "ln_mod_x")
    cm = _ln_mod(ctx2, shift_c, scale_c, "ln_mod_ctx")
    ready = xm[:8, :LANES].astype(F32) + cm[:8, :LANES].astype(F32) + cre[0, 0, :8, :].astype(F32)
    hw_in2, tok_b = _forward_start(_exchange_wait(hw_in, ready, "wait_gather_w_in"), 1, "start_forward_w_in")
    w_in_f = _forward_wait(hw_in2, tok_b, "wait_forward_w_in")
    hw_glu, tok_c = _exchange_start(w_glu[0].astype(MXU_DTYPE), 0, "gather", "start_gather_w_glu")
    hw_out, tok_o = _exchange_start(w_out[0].astype(MXU_DTYPE), 0, "gather", "start_gather_w_out")
    proj = _matmul(xm, w_in_f, mode="nn", name="proj", dep=tok_c + tok_o)
    ub_c = _matmul(cm, w_in_f, mode="nn", name="proj_ctx", b_n0=3 * w, n=w)
    bsf = jnp.repeat(b_spatial[0].T, HEAD_DIM_A, axis=1)
    ws = w_spatial[0]
    cat = _ga_fwd(proj, sgu_ln_g, sgu_ln_b, ws, bsf, w)
    zeros_state = jnp.zeros((2, nsr, LANES), F32)
    s5c, s5l = [], []
    for dr in range(nd):
        s5c.append(_s5_fwd(ub_c, 0, w, zeros_state, a_sm, wbr, wbi, cre, ncim, dr, f"s5_fwd_ctx{dr}"))
        s5l.append(_s5_fwd(proj, 3, w, s5c[dr][3], a_sm, wbr, wbi, cre, ncim, dr, f"s5_fwd{dr}"))
    w_glu_f = _exchange_wait(hw_glu, s5l[1][0], "wait_gather_w_glu")
    ys, cat = _glu_fwd(s5l[0][0], s5l[1][0], proj, cat, d_skip, w_glu_f, b_glu, w)
    w_out_f = _exchange_wait(hw_out, ys, "wait_gather_w_out")
    out = _matmul(cat, w_out_f, mode="nn", name="out_proj")
    loss_row, dout, dx_res, dgate, dln_g, dln_b = _post_ln_loss(x2, out, gate_x, ln_g, ln_b, tgt)

    dcat = _matmul(dout, w_out_f, mode="nt", name="d_cat", out_dtype=MXU_DTYPE)
    dw_out = _matmul(cat, dout, mode="tn", name="d_w_out", out_dtype=MXU_DTYPE)
    hg_out, tok_d = _exchange_start(dw_out, 0, "a2a", "start_a2a_d_w_out")
    dys, dproj, db_glu, dd_skip, dw_glu = _glu_bwd(dcat, ys, proj, w_glu_f, b_glu, w, tok_d)
    hg_glu, tok_e = _exchange_start(dw_glu.astype(MXU_DTYPE), 0, "a2a", "start_a2a_d_w_glu")
    zeros_state = zeros_state + tok_e[0, 0]
    du_l, du_c, dwbr, dwbi, dcre, dncim, da_sm = [], [], [], [], [], [], []
    nbk, spb = w // LANES, gb * p_s5
    for dr in range(nd):
        bl = _s5_bwd(dys, proj, 3, w, s5l[dr][1], s5l[dr][2], s5c[dr][3], zeros_state, a_sm, wbr_t, wbi_t,
                     cre_t, ncim_t, dr, f"s5_bwd{dr}")
        bc = _s5_bwd(None, ub_c, 0, w, s5c[dr][1], s5c[dr][2], zeros_state, bl[6], a_sm, wbr_t, wbi_t,
                     None, None, dr, f"s5_bwd_ctx{dr}")
        du_l.append(bl[0])
        du_c.append(bc[0])
        dwbr.append(_add2(bl[1].reshape(nbk * LANES, spb), bc[1].reshape(nbk * LANES, spb), f"sum_dwbr{dr}"))
        dwbi.append(_add2(bl[2].reshape(nbk * LANES, spb), bc[2].reshape(nbk * LANES, spb), f"sum_dwbi{dr}"))
        dcre.append(bl[3])
        dncim.append(bl[4])
        da_sm.append(_add2(bl[5].reshape(2 * nsr, LANES), bc[3].reshape(2 * nsr, LANES), f"sum_da{dr}"))
    dub_c = _add2(du_c[0], du_c[1], "dub_ctx")
    dwbr = jnp.stack(dwbr).reshape(nd, nbk, LANES, spb)
    dwbi = jnp.stack(dwbi).reshape(nd, nbk, LANES, spb)
    dcre, dncim = jnp.stack(dcre), jnp.stack(dncim)
    da_sm = jnp.stack(da_sm).reshape(nd, 2, g_s5, p_s5)
    dproj, dsg, dsb, dws, dbsf = _ga_bwd(proj, dcat, dproj, sgu_ln_g, sgu_ln_b, ws, bsf, dys, du_l[0], du_l[1], d_skip, w)

    dbb_re = _block_diag_extract(dwbr, gb, c_s5, p_s5)
    dbb_im = _block_diag_extract(dwbi, gb, c_s5, p_s5)
    dc_re = jnp.swapaxes(_block_diag_extract(dcre, gb, p_s5, c_s5), 2, 3)
    dc_im = -jnp.swapaxes(_block_diag_extract(dncim, gb, p_s5, c_s5), 2, 3)
    dlr, dli, dls, db_re, db_im = _disc_bwd(lr, li, ls, br_t, bi_t, da_sm[:, 0:1].reshape(nd, g_s5, 1, p_s5),
                                            da_sm[:, 1:2].reshape(nd, g_s5, 1, p_s5), dbb_re, dbb_im)
    expand = (jnp.arange(w)[:, None] // HEAD_DIM_A == jnp.arange(LANES)[None, :]).astype(F32)
    db_sp = _lane_group_sum(dbsf, expand, "d_b_spatial")[:, :nh].T

    local = {"sgu_ln_g": dsg, "sgu_ln_b": dsb, "w_spatial": dws, "b_spatial": db_sp,
             "s5_lam_re": dlr, "s5_lam_im": dli, "s5_log_step": dls, "s5_b_re": db_re, "s5_b_im": db_im,
             "s5_c_re": dc_re, "s5_c_im": dc_im, "s5_d": dd_skip, "b_glu": db_glu, "ln_g": dln_g, "ln_b": dln_b}
    reduced = sorted(local, key=lambda n: -math.prod(env[n].shape))
    loss_part = (0.5 / d) * jnp.sum(loss_row)
    flat = jnp.concatenate([local[n].reshape(-1) for n in reduced] + [loss_part.reshape(1)])
    unit = N_DEV * 8 * LANES
    total = -(-flat.shape[0] // unit) * unit
    flat = jnp.pad(flat, (0, total - flat.shape[0])).reshape(N_DEV * 8, total // (N_DEV * 8))
    h_small, tok_s = _exchange_start(flat, 0, "a2a", "start_a2a_small")

    dw_in = _matmul(xm, dproj, mode="tn", name="d_w_in", out_dtype=MXU_DTYPE, dep=tok_s)
    dw_in = _matmul(cm, dub_c, mode="tn", name="d_w_in_ctx", acc_in=dw_in, acc_n0=3 * w, out_dtype=MXU_DTYPE)
    hg_in, tok_f = _exchange_start(dw_in, 1, "a2a", "start_a2a_d_w_in")
    mine = _sum_slots(_exchange_wait(h_small, dw_in, "wait_a2a_small"), "sum_small")
    h_sums, tok_g = _exchange_start(mine, 0, "gather", "start_gather_small")
    dxm = _matmul(dproj, w_in_f, mode="nt", name="d_xm", dep=tok_f + tok_g, out_dtype=MXU_DTYPE)
    dcm = _matmul(dub_c, w_in_f, mode="nt", name="d_cm", b_k0=3 * w, k=w)
    grad_x, dshift_x, dscale_x = _ln_mod_bwd(x2, dxm, scale_x, dx_res, "ln_mod_x_bwd")
    _, dshift_c, dscale_c = _ln_mod_bwd(ctx2, dcm, scale_c, None, "ln_mod_ctx_bwd")

    dmod_rows = jnp.concatenate([jnp.concatenate([dshift_x, dscale_x, dgate], axis=1),
                                 jnp.concatenate([dshift_c, dscale_c, jnp.zeros((1, d), F32)], axis=1),
                                 jnp.zeros((6, 3 * d), F32)], axis=0)
    h_dmod, tok_m = _exchange_start(dmod_rows, 0, "gather", "start_gather_dmod")

    gp_w_out = _exchange_wait(hg_out, tok_m, "wait_a2a_d_w_out")
    gp_w_glu = _exchange_wait(hg_glu, tok_m, "wait_a2a_d_w_glu")
    gp_w_in = _exchange_wait(hg_in, tok_m, "wait_a2a_d_w_in")
    big = {
        "w_in": _adamw(w_in[0], m_w_in[0], v_w_in[0], gp_w_in, "adamw_w_in"),
        "w_glu": _adamw(w_glu[0], m_w_glu[0], v_w_glu[0], gp_w_glu, "adamw_w_glu"),
        "w_out": _adamw(w_out[0], m_w_out[0], v_w_out[0], gp_w_out, "adamw_w_out"),
    }
    dmod_all = _exchange_wait(h_dmod, big["w_out"][0], "wait_gather_dmod")
    dmod_ctx = _sum_slots(dmod_all[1::8].reshape(N_DEV, 1, 3 * d), "sum_dmod_ctx")
    dmod_mat = jnp.concatenate([dmod_all[0::8], dmod_ctx, jnp.zeros((7, 3 * d), F32)], axis=0)
    db_ada = _sum_slots(dmod_mat[:9].reshape(9, 1, 3 * d), "sum_db_ada")
    dmod_mine = lax.dynamic_slice_in_dim(dmod_mat, me * ada_cols, ada_cols, axis=1)
    dw_ada = _small_dot(s_mat, dmod_mine, "tn", "d_w_ada")
    dsilu_cc = _small_dot(dmod_mine[8:16], w_ada[0], "nt", "d_silu_cctx")[0:1]
    dc_ctx_part = dsilu_cc * _silu_grad(c_ctx.reshape(1, d))
    dc_ctx_rows = jnp.concatenate([dc_ctx_part, jnp.zeros((7, d), F32)], axis=0)
    h_cctx, tok_c2 = _exchange_start(dc_ctx_rows, 0, "gather", "start_gather_d_c_ctx")

    big["w_ada"] = _adamw(w_ada[0], m_w_ada[0], v_w_ada[0], dw_ada[None], "adamw_w_ada", dep=tok_c2)
    summed = _exchange_wait(h_sums, big["w_ada"][0], "wait_gather_small").reshape(-1)
    grads, off = {"b_ada": db_ada}, 0
    for n in reduced:
        size = math.prod(env[n].shape)
        grads[n] = summed[off:off + size].reshape(env[n].shape)
        off += size
    loss = summed[off]
    res = {n: tuple(a[None] for a in big[n]) for n in big}

    def small_step(n):
        res[n] = (grads[n],) + _adamw_nd(env[n], env["m_" + n], env["v_" + n], grads[n], "adamw_" + n)
        if n in swapped:
            res[n] = tuple(jnp.swapaxes(a, -1, -2) for a in res[n])

    large = ("w_spatial", "s5_b_re", "s5_b_im", "s5_c_re", "s5_c_im")
    tiny = [n for n in small_names if n not in large and n != "c_ctx"]
    for n in large:
        small_step(n)
    for n, out in zip(tiny, _adamw_many([(env[n], env["m_" + n], env["v_" + n], grads[n]) for n in tiny], "adamw_tiny")):
        res[n] = (grads[n],) + out
    dc_ctx_all = _exchange_wait(h_cctx, res["w_spatial"][1], "wait_gather_d_c_ctx")
    grads["c_ctx"] = _sum_slots(dc_ctx_all[0::8].reshape(N_DEV, 1, d), "sum_d_c_ctx").reshape(d)
    small_step("c_ctx")

    order = ["c_ctx", "w_ada", "b_ada", "w_in", "sgu_ln_g", "sgu_ln_b", "w_spatial", "b_spatial", "s5_lam_re", "s5_lam_im",
             "s5_log_step", "s5_b_re", "s5_b_im", "s5_c_re", "s5_c_im", "s5_d", "w_glu", "b_glu", "w_out", "ln_g", "ln_b"]
    return (loss, grad_x[None], *[res[n][0] for n in order], *[res[n][1] for n in order],
            *[res[n][2] for n in order], *[res[n][3] for n in order])
```
